```python
import jax, jax.numpy as jnp
from jax import lax
import numpy as np

D_MODEL = 2048
BATCH = 8
SEQ = 8192
DEPTH = 1

RET_HEADS = 4
RET_HEAD_DIM = D_MODEL // 8
RET_WIDTH = RET_HEADS * RET_HEAD_DIM
HGRN_HEADS = 8
HGRN_HEAD_DIM = D_MODEL // 16
HGRN_WIDTH = HGRN_HEADS * HGRN_HEAD_DIM
MIX_WIDTH = RET_WIDTH + HGRN_WIDTH
IN_COLS = 4 * RET_WIDTH + 4 * HGRN_WIDTH
D_FF = ((8 * D_MODEL // 3 + 255) // 256) * 256
RET_CHUNK = 128
HGRN_CHUNK = 64
ROPE_BASE = 10000.0
EPS = 1e-6
FFN_RESIDUAL_WEIGHT = 0.5

kernel_name = "hymba_style_retention_hgrn2_macaron"


def rmsnorm(x, g):
    x32 = x.astype(jnp.float32)
    y = x32 * lax.rsqrt(jnp.mean(x32 * x32, axis=-1, keepdims=True) + EPS)
    return (y * g.astype(jnp.float32)).astype(x.dtype)


def swiglu(h, w_gate, w_up, w_down):
    return (jax.nn.silu(h @ w_gate) * (h @ w_up)) @ w_down


def rope(x):
    d = x.shape[-1]
    s = x.shape[1]
    inv = jnp.power(ROPE_BASE, -jnp.arange(0, d, 2, dtype=jnp.float32) / d)
    ang = jnp.arange(s, dtype=jnp.float32)[:, None] * inv[None, :]
    cos = jnp.cos(ang)[None, :, None, :]
    sin = jnp.sin(ang)[None, :, None, :]
    x32 = x.astype(jnp.float32)
    x1, x2 = x32[..., : d // 2], x32[..., d // 2:]
    return jnp.concatenate([x1 * cos - x2 * sin, x2 * cos + x1 * sin], axis=-1)


def retention_chunkwise(q, k, v):
    b, s, h, dk = q.shape
    dv = v.shape[-1]
    c = RET_CHUNK
    n = s // c
    log_gamma = jnp.log(1.0 - jnp.exp2(-5.0 - jnp.arange(h, dtype=jnp.float32)))
    q = q.astype(jnp.float32).reshape(b, n, c, h, dk)
    k = k.astype(jnp.float32).reshape(b, n, c, h, dk)
    v = v.astype(jnp.float32).reshape(b, n, c, h, dv)
    idx = jnp.arange(c, dtype=jnp.float32)
    rel = idx[:, None] - idx[None, :]
    mask = rel >= 0
    decay = jnp.where(mask[None], jnp.exp(log_gamma[:, None, None] * jnp.where(mask, rel, 0.0)[None]), 0.0)
    scores = jnp.einsum('bnihd,bnjhd->bnhij', q, k) * decay[None, None]
    inner = jnp.einsum('bnhij,bnjhe->bnihe', scores, v)
    k_dec = k * jnp.exp(log_gamma[None, :] * (c - 1.0 - idx)[:, None])[None, None, :, :, None]
    kv = jnp.einsum('bnjhd,bnjhe->nbhde', k_dec, v)
    g_chunk = jnp.exp(log_gamma * c)[None, :, None, None]

    def step(state, kv_n):
        return g_chunk * state + kv_n, state

    _, r_prev = lax.scan(step, jnp.zeros((b, h, dk, dv), jnp.float32), kv)
    q_dec = q * jnp.exp(log_gamma[None, :] * (idx + 1.0)[:, None])[None, None, :, :, None]
    cross = jnp.einsum('bnihd,nbhde->bnihe', q_dec, r_prev)
    return (inner + cross).reshape(b, s, h, dv)


def hgrn2_chunkwise(q, k, v, log_f):
    b, s, h, dk = q.shape
    dv = v.shape[-1]
    c = HGRN_CHUNK
    n = s // c

    def to_chunks(t):
        return t.astype(jnp.float32).reshape(b, n, c, h, t.shape[-1]).transpose(1, 0, 3, 2, 4)

    causal = jnp.tril(jnp.ones((c, c), dtype=bool))

    def step(state, inp):
        q_c, k_c, v_c, lf_c = inp
        cum = jnp.cumsum(lf_c, axis=-2)
        diff = cum[:, :, :, None, :] - cum[:, :, None, :, :]
        pair_decay = jnp.exp(jnp.where(causal[:, :, None], diff, -jnp.inf))
        attn = jnp.einsum('bhtd,bhjd,bhtjd->bhtj', q_c, k_c, pair_decay)
        o = jnp.einsum('bhtj,bhje->bhte', attn, v_c) + jnp.einsum('bhtd,bhde->bhte', q_c * jnp.exp(cum), state)
        last = cum[:, :, -1:, :]
        state = jnp.exp(last[:, :, 0, :])[..., None] * state + jnp.einsum('bhjd,bhje->bhde', k_c * jnp.exp(last - cum), v_c)
        return state, o

    _, o = lax.scan(step, jnp.zeros((b, h, dk, dv), jnp.float32),
                    (to_chunks(q), to_chunks(k), to_chunks(v), to_chunks(log_f)))
    return o.transpose(1, 0, 3, 2, 4).reshape(b, s, h, dv)


def hgrn_lower_bounds(lb_logits):
    logits = jnp.concatenate([lb_logits.astype(jnp.float32), jnp.zeros((1, lb_logits.shape[-1]), jnp.float32)], axis=0)
    return jnp.cumsum(jax.nn.softmax(logits, axis=0), axis=0)[:DEPTH]


def hybrid_mixer(h, w_in, ret_norm_g, lb, hgrn_norm_g, w_out):
    b, s, _ = h.shape
    proj = h @ w_in
    splits = np.cumsum([RET_WIDTH] * 4 + [HGRN_WIDTH] * 3).tolist()
    rq, rk, rv, rg, hq, hf, hi, hg = jnp.split(proj, splits, axis=-1)
    rq = rope(rq.reshape(b, s, RET_HEADS, RET_HEAD_DIM)) * (RET_HEAD_DIM ** -0.5)
    rk = rope(rk.reshape(b, s, RET_HEADS, RET_HEAD_DIM))
    ret = retention_chunkwise(rq, rk, rv.reshape(b, s, RET_HEADS, RET_HEAD_DIM))
    mu = jnp.mean(ret, axis=-1, keepdims=True)
    var = jnp.mean(jnp.square(ret - mu), axis=-1, keepdims=True)
    ret = ((ret - mu) * lax.rsqrt(var + EPS)).reshape(b, s, RET_WIDTH)
    ret = ret * ret_norm_g.astype(jnp.float32) * jax.nn.silu(rg.astype(jnp.float32))
    z = hf.astype(jnp.float32).reshape(b, s, HGRN_HEADS, HGRN_HEAD_DIM)
    lbh = lb.reshape(HGRN_HEADS, HGRN_HEAD_DIM)
    f = lbh + (1.0 - lbh) * jax.nn.sigmoid(z)
    key = (1.0 - lbh) * jax.nn.sigmoid(-z)
    hq_act = jax.nn.silu(hq.astype(jnp.float32)).reshape(b, s, HGRN_HEADS, HGRN_HEAD_DIM)
    hv = hi.reshape(b, s, HGRN_HEADS, HGRN_HEAD_DIM)
    hg_out = hgrn2_chunkwise(hq_act, key, hv, jnp.log(f))
    hg_out = (hg_out * lax.rsqrt(jnp.mean(hg_out * hg_out, axis=-1, keepdims=True) + EPS)).reshape(b, s, HGRN_WIDTH)
    hg_out = hg_out * hgrn_norm_g.astype(jnp.float32) * jax.nn.silu(hg.astype(jnp.float32))
    merged = jnp.concatenate([ret, hg_out], axis=-1).astype(h.dtype)
    return merged @ w_out


def _fwd_setup_inputs(seed: int = 0) -> dict:
    key = jax.random.key(seed)
    ks = jax.random.split(key, 16)
    f32 = jnp.float32

    def w(k, shape, fan_in):
        return jax.random.normal(k, shape, f32) * (fan_in ** -0.5)

    def gain(k, shape):
        return 1.0 + 0.02 * jax.random.normal(k, shape, f32)

    return {
        "x": jax.random.normal(ks[0], (BATCH, SEQ, D_MODEL), f32),
        "ffn1_norm": gain(ks[1], (DEPTH, D_MODEL)),
        "ffn1_w_gate": w(ks[2], (DEPTH, D_MODEL, D_FF), D_MODEL),
        "ffn1_w_up": w(ks[3], (DEPTH, D_MODEL, D_FF), D_MODEL),
        "ffn1_w_down": w(ks[4], (DEPTH, D_FF, D_MODEL), D_FF),
        "mix_norm": gain(ks[5], (DEPTH, D_MODEL)),
        "w_in": w(ks[6], (DEPTH, D_MODEL, IN_COLS), D_MODEL),
        "ret_norm_g": gain(ks[7], (DEPTH, RET_WIDTH)),
        "hgrn_lb_logits": 0.5 * jax.random.normal(ks[8], (DEPTH, HGRN_WIDTH), f32),
        "hgrn_norm_g": gain(ks[9], (DEPTH, HGRN_WIDTH)),
        "w_out": w(ks[10], (DEPTH, MIX_WIDTH, D_MODEL), MIX_WIDTH),
        "ffn2_norm": gain(ks[11], (DEPTH, D_MODEL)),
        "ffn2_w_gate": w(ks[12], (DEPTH, D_MODEL, D_FF), D_MODEL),
        "ffn2_w_up": w(ks[13], (DEPTH, D_MODEL, D_FF), D_MODEL),
        "ffn2_w_down": w(ks[14], (DEPTH, D_FF, D_MODEL), D_FF),
        "final_norm": gain(ks[15], (D_MODEL,)),
    }


def _fwd_reference(x, ffn1_norm, ffn1_w_gate, ffn1_w_up, ffn1_w_down, mix_norm, w_in, ret_norm_g,
              hgrn_lb_logits, hgrn_norm_g, w_out, ffn2_norm, ffn2_w_gate, ffn2_w_up, ffn2_w_down, final_norm):
    lbs = hgrn_lower_bounds(hgrn_lb_logits)
    for l in range(DEPTH):
        y = swiglu(rmsnorm(x, ffn1_norm[l]), ffn1_w_gate[l], ffn1_w_up[l], ffn1_w_down[l])
        x = x + (FFN_RESIDUAL_WEIGHT * y).astype(x.dtype)
        y = hybrid_mixer(rmsnorm(x, mix_norm[l]), w_in[l], ret_norm_g[l], lbs[l], hgrn_norm_g[l], w_out[l])
        x = x + y.astype(x.dtype)
        y = swiglu(rmsnorm(x, ffn2_norm[l]), ffn2_w_gate[l], ffn2_w_up[l], ffn2_w_down[l])
        x = x + (FFN_RESIDUAL_WEIGHT * y).astype(x.dtype)
    return rmsnorm(x, final_norm)


import jax as _jax
import jax.numpy as _jnp

TWIN_FORMAT = 'train_step'
FWD_PARAMS = ['x', 'ffn1_norm', 'ffn1_w_gate', 'ffn1_w_up', 'ffn1_w_down', 'mix_norm', 'w_in', 'ret_norm_g', 'hgrn_lb_logits', 'hgrn_norm_g', 'w_out', 'ffn2_norm', 'ffn2_w_gate', 'ffn2_w_up', 'ffn2_w_down', 'final_norm']
TWIN_WEIGHTS = ['ffn1_norm', 'ffn1_w_gate', 'ffn1_w_up', 'ffn1_w_down', 'mix_norm', 'w_in', 'ret_norm_g', 'hgrn_lb_logits', 'hgrn_norm_g', 'w_out', 'ffn2_norm', 'ffn2_w_gate', 'ffn2_w_up', 'ffn2_w_down', 'final_norm']
TWIN_DIFF_INPUT = 'x'
TWIN_INPUTS = ['x', 'ffn1_norm', 'ffn1_w_gate', 'ffn1_w_up', 'ffn1_w_down', 'mix_norm', 'w_in', 'ret_norm_g', 'hgrn_lb_logits', 'hgrn_norm_g', 'w_out', 'ffn2_norm', 'ffn2_w_gate', 'ffn2_w_up', 'ffn2_w_down', 'final_norm', 'loss_target', 'm_ffn1_norm', 'm_ffn1_w_gate', 'm_ffn1_w_up', 'm_ffn1_w_down', 'm_mix_norm', 'm_w_in', 'm_ret_norm_g', 'm_hgrn_lb_logits', 'm_hgrn_norm_g', 'm_w_out', 'm_ffn2_norm', 'm_ffn2_w_gate', 'm_ffn2_w_up', 'm_ffn2_w_down', 'm_final_norm', 'v_ffn1_norm', 'v_ffn1_w_gate', 'v_ffn1_w_up', 'v_ffn1_w_down', 'v_mix_norm', 'v_w_in', 'v_ret_norm_g', 'v_hgrn_lb_logits', 'v_hgrn_norm_g', 'v_w_out', 'v_ffn2_norm', 'v_ffn2_w_gate', 'v_ffn2_w_up', 'v_ffn2_w_down', 'v_final_norm']
TWIN_OUTPUTS = ['loss', 'grad_x', 'grad_ffn1_norm', 'grad_ffn1_w_gate', 'grad_ffn1_w_up', 'grad_ffn1_w_down', 'grad_mix_norm', 'grad_w_in', 'grad_ret_norm_g', 'grad_hgrn_lb_logits', 'grad_hgrn_norm_g', 'grad_w_out', 'grad_ffn2_norm', 'grad_ffn2_w_gate', 'grad_ffn2_w_up', 'grad_ffn2_w_down', 'grad_final_norm', 'delta_ffn1_norm', 'delta_ffn1_w_gate', 'delta_ffn1_w_up', 'delta_ffn1_w_down', 'delta_mix_norm', 'delta_w_in', 'delta_ret_norm_g', 'delta_hgrn_lb_logits', 'delta_hgrn_norm_g', 'delta_w_out', 'delta_ffn2_norm', 'delta_ffn2_w_gate', 'delta_ffn2_w_up', 'delta_ffn2_w_down', 'delta_final_norm', 'new_m_ffn1_norm', 'new_m_ffn1_w_gate', 'new_m_ffn1_w_up', 'new_m_ffn1_w_down', 'new_m_mix_norm', 'new_m_w_in', 'new_m_ret_norm_g', 'new_m_hgrn_lb_logits', 'new_m_hgrn_norm_g', 'new_m_w_out', 'new_m_ffn2_norm', 'new_m_ffn2_w_gate', 'new_m_ffn2_w_up', 'new_m_ffn2_w_down', 'new_m_final_norm', 'new_v_ffn1_norm', 'new_v_ffn1_w_gate', 'new_v_ffn1_w_up', 'new_v_ffn1_w_down', 'new_v_mix_norm', 'new_v_w_in', 'new_v_ret_norm_g', 'new_v_hgrn_lb_logits', 'new_v_hgrn_norm_g', 'new_v_w_out', 'new_v_ffn2_norm', 'new_v_ffn2_w_gate', 'new_v_ffn2_w_up', 'new_v_ffn2_w_down', 'new_v_final_norm']
TWIN_LEAF_KINDS = {'loss': 'loss', 'grad_x': 'grad_x', 'grad_ffn1_norm': 'grad_w', 'grad_ffn1_w_gate': 'grad_w', 'grad_ffn1_w_up': 'grad_w', 'grad_ffn1_w_down': 'grad_w', 'grad_mix_norm': 'grad_w', 'grad_w_in': 'grad_w', 'grad_ret_norm_g': 'grad_w', 'grad_hgrn_lb_logits': 'grad_w', 'grad_hgrn_norm_g': 'grad_w', 'grad_w_out': 'grad_w', 'grad_ffn2_norm': 'grad_w', 'grad_ffn2_w_gate': 'grad_w', 'grad_ffn2_w_up': 'grad_w', 'grad_ffn2_w_down': 'grad_w', 'grad_final_norm': 'grad_w', 'delta_ffn1_norm': 'delta_w', 'delta_ffn1_w_gate': 'delta_w', 'delta_ffn1_w_up': 'delta_w', 'delta_ffn1_w_down': 'delta_w', 'delta_mix_norm': 'delta_w', 'delta_w_in': 'delta_w', 'delta_ret_norm_g': 'delta_w', 'delta_hgrn_lb_logits': 'delta_w', 'delta_hgrn_norm_g': 'delta_w', 'delta_w_out': 'delta_w', 'delta_ffn2_norm': 'delta_w', 'delta_ffn2_w_gate': 'delta_w', 'delta_ffn2_w_up': 'delta_w', 'delta_ffn2_w_down': 'delta_w', 'delta_final_norm': 'delta_w', 'new_m_ffn1_norm': 'new_m', 'new_m_ffn1_w_gate': 'new_m', 'new_m_ffn1_w_up': 'new_m', 'new_m_ffn1_w_down': 'new_m', 'new_m_mix_norm': 'new_m', 'new_m_w_in': 'new_m', 'new_m_ret_norm_g': 'new_m', 'new_m_hgrn_lb_logits': 'new_m', 'new_m_hgrn_norm_g': 'new_m', 'new_m_w_out': 'new_m', 'new_m_ffn2_norm': 'new_m', 'new_m_ffn2_w_gate': 'new_m', 'new_m_ffn2_w_up': 'new_m', 'new_m_ffn2_w_down': 'new_m', 'new_m_final_norm': 'new_m', 'new_v_ffn1_norm': 'new_v', 'new_v_ffn1_w_gate': 'new_v', 'new_v_ffn1_w_up': 'new_v', 'new_v_ffn1_w_down': 'new_v', 'new_v_mix_norm': 'new_v', 'new_v_w_in': 'new_v', 'new_v_ret_norm_g': 'new_v', 'new_v_hgrn_lb_logits': 'new_v', 'new_v_hgrn_norm_g': 'new_v', 'new_v_w_out': 'new_v', 'new_v_ffn2_norm': 'new_v', 'new_v_ffn2_w_gate': 'new_v', 'new_v_ffn2_w_up': 'new_v', 'new_v_ffn2_w_down': 'new_v', 'new_v_final_norm': 'new_v'}


def _forward(args):
    return _fwd_reference(*[args[k] for k in FWD_PARAMS])


def _output_shape():
    def fwd():
        inp = _fwd_setup_inputs(0)
        return _fwd_reference(*[inp[k] for k in FWD_PARAMS])
    out = _jax.eval_shape(fwd)
    return out.shape, out.dtype

N_MICROBATCH = 1
ADAM_LR = 0.001
ADAM_B1 = 0.9
ADAM_B2 = 0.999
ADAM_EPS = 1e-08
ADAM_WD = 0.01
ADAM_STEP = 10
PER_EXAMPLE_BATCH_AXIS = {'x': 0, 'loss_target': 0}
SHARED_INPUTS = []
_WEIGHT_DTYPES = {'ffn1_norm': _jnp.float32, 'ffn1_w_gate': _jnp.float32, 'ffn1_w_up': _jnp.float32, 'ffn1_w_down': _jnp.float32, 'mix_norm': _jnp.float32, 'w_in': _jnp.float32, 'ret_norm_g': _jnp.float32, 'hgrn_lb_logits': _jnp.float32, 'hgrn_norm_g': _jnp.float32, 'w_out': _jnp.float32, 'ffn2_norm': _jnp.float32, 'ffn2_w_gate': _jnp.float32, 'ffn2_w_up': _jnp.float32, 'ffn2_w_down': _jnp.float32, 'final_norm': _jnp.float32}
MOMENT_SCALE = {'ffn1_norm': 3.627119e-01, 'ffn1_w_gate': 1.600821e-01, 'ffn1_w_up': 1.550093e-01, 'ffn1_w_down': 2.569256e-01, 'mix_norm': 6.476365e-01, 'w_in': 3.203078e-01, 'ret_norm_g': 6.542224e-02, 'hgrn_lb_logits': 6.224731e-03, 'hgrn_norm_g': 6.342562e-02, 'w_out': 6.427201e-02, 'ffn2_norm': 4.445291e-02, 'ffn2_w_gate': 1.883405e-02, 'ffn2_w_up': 1.822887e-02, 'ffn2_w_down': 3.027643e-02, 'final_norm': 3.197657e+01}


def _to_microbatches(a, axis):
    t = _jnp.moveaxis(a, axis, 0)
    t = t.reshape((N_MICROBATCH, t.shape[0] // N_MICROBATCH) + t.shape[1:])
    return _jnp.moveaxis(t, 1, axis + 1)


def setup_inputs(seed: int = 0) -> dict:
    inp = _fwd_setup_inputs(seed)
    key = _jax.random.fold_in(_jax.random.key(seed), 7919)
    shape, _ = _output_shape()
    out = dict(inp)
    out["loss_target"] = _jax.random.normal(_jax.random.fold_in(key, 0), shape, _jnp.float32)
    for i, name in enumerate(TWIN_WEIGHTS):
        w = inp[name].astype(_jnp.float32)
        if MOMENT_SCALE is None:
            s = _jnp.sqrt(_jnp.mean(_jnp.square(w)) + 1e-30)
        else:
            s = MOMENT_SCALE[name]
        km, kv = _jax.random.split(_jax.random.fold_in(key, i + 1))
        out[name] = w
        out["m_" + name] = s * _jax.random.normal(km, w.shape, _jnp.float32)
        out["v_" + name] = (s * s) * _jax.random.uniform(kv, w.shape, _jnp.float32, 0.5, 1.5)
    if N_MICROBATCH > 1:
        for name, axis in PER_EXAMPLE_BATCH_AXIS.items():
            out[name] = _to_microbatches(out[name], axis)
    return {'x': out['x'], 'ffn1_norm': out['ffn1_norm'], 'ffn1_w_gate': out['ffn1_w_gate'], 'ffn1_w_up': out['ffn1_w_up'], 'ffn1_w_down': out['ffn1_w_down'], 'mix_norm': out['mix_norm'], 'w_in': out['w_in'], 'ret_norm_g': out['ret_norm_g'], 'hgrn_lb_logits': out['hgrn_lb_logits'], 'hgrn_norm_g': out['hgrn_norm_g'], 'w_out': out['w_out'], 'ffn2_norm': out['ffn2_norm'], 'ffn2_w_gate': out['ffn2_w_gate'], 'ffn2_w_up': out['ffn2_w_up'], 'ffn2_w_down': out['ffn2_w_down'], 'final_norm': out['final_norm'], 'loss_target': out['loss_target'], 'm_ffn1_norm': out['m_ffn1_norm'], 'm_ffn1_w_gate': out['m_ffn1_w_gate'], 'm_ffn1_w_up': out['m_ffn1_w_up'], 'm_ffn1_w_down': out['m_ffn1_w_down'], 'm_mix_norm': out['m_mix_norm'], 'm_w_in': out['m_w_in'], 'm_ret_norm_g': out['m_ret_norm_g'], 'm_hgrn_lb_logits': out['m_hgrn_lb_logits'], 'm_hgrn_norm_g': out['m_hgrn_norm_g'], 'm_w_out': out['m_w_out'], 'm_ffn2_norm': out['m_ffn2_norm'], 'm_ffn2_w_gate': out['m_ffn2_w_gate'], 'm_ffn2_w_up': out['m_ffn2_w_up'], 'm_ffn2_w_down': out['m_ffn2_w_down'], 'm_final_norm': out['m_final_norm'], 'v_ffn1_norm': out['v_ffn1_norm'], 'v_ffn1_w_gate': out['v_ffn1_w_gate'], 'v_ffn1_w_up': out['v_ffn1_w_up'], 'v_ffn1_w_down': out['v_ffn1_w_down'], 'v_mix_norm': out['v_mix_norm'], 'v_w_in': out['v_w_in'], 'v_ret_norm_g': out['v_ret_norm_g'], 'v_hgrn_lb_logits': out['v_hgrn_lb_logits'], 'v_hgrn_norm_g': out['v_hgrn_norm_g'], 'v_w_out': out['v_w_out'], 'v_ffn2_norm': out['v_ffn2_norm'], 'v_ffn2_w_gate': out['v_ffn2_w_gate'], 'v_ffn2_w_up': out['v_ffn2_w_up'], 'v_ffn2_w_down': out['v_ffn2_w_down'], 'v_final_norm': out['v_final_norm']}


def _loss(weights, diff, rest, loss_target):
    with _jax.named_scope("forward"):
        args = {**rest, TWIN_DIFF_INPUT: diff, **{k: w.astype(_WEIGHT_DTYPES[k]) for k, w in weights.items()}}
        y = _forward(args)
    with _jax.named_scope("loss_head"):
        err = _jnp.square(y.astype(_jnp.float32) - loss_target)
        return 0.5 * _jnp.sum(_jnp.mean(err, axis=-1)) if err.ndim else 0.5 * err


def _adamw(w, g, m, v):
    m = ADAM_B1 * m + (1.0 - ADAM_B1) * g
    v = ADAM_B2 * v + (1.0 - ADAM_B2) * _jnp.square(g)
    m_hat = m / (1.0 - ADAM_B1 ** ADAM_STEP)
    v_hat = v / (1.0 - ADAM_B2 ** ADAM_STEP)
    delta = -ADAM_LR * (m_hat / (_jnp.sqrt(v_hat) + ADAM_EPS) + ADAM_WD * w)
    return delta, m, v


def reference(x, ffn1_norm, ffn1_w_gate, ffn1_w_up, ffn1_w_down, mix_norm, w_in, ret_norm_g, hgrn_lb_logits, hgrn_norm_g, w_out, ffn2_norm, ffn2_w_gate, ffn2_w_up, ffn2_w_down, final_norm, loss_target, m_ffn1_norm, m_ffn1_w_gate, m_ffn1_w_up, m_ffn1_w_down, m_mix_norm, m_w_in, m_ret_norm_g, m_hgrn_lb_logits, m_hgrn_norm_g, m_w_out, m_ffn2_norm, m_ffn2_w_gate, m_ffn2_w_up, m_ffn2_w_down, m_final_norm, v_ffn1_norm, v_ffn1_w_gate, v_ffn1_w_up, v_ffn1_w_down, v_mix_norm, v_w_in, v_ret_norm_g, v_hgrn_lb_logits, v_hgrn_norm_g, v_w_out, v_ffn2_norm, v_ffn2_w_gate, v_ffn2_w_up, v_ffn2_w_down, v_final_norm):
    given = dict(x=x, ffn1_norm=ffn1_norm, ffn1_w_gate=ffn1_w_gate, ffn1_w_up=ffn1_w_up, ffn1_w_down=ffn1_w_down, mix_norm=mix_norm, w_in=w_in, ret_norm_g=ret_norm_g, hgrn_lb_logits=hgrn_lb_logits, hgrn_norm_g=hgrn_norm_g, w_out=w_out, ffn2_norm=ffn2_norm, ffn2_w_gate=ffn2_w_gate, ffn2_w_up=ffn2_w_up, ffn2_w_down=ffn2_w_down, final_norm=final_norm, loss_target=loss_target, m_ffn1_norm=m_ffn1_norm, m_ffn1_w_gate=m_ffn1_w_gate, m_ffn1_w_up=m_ffn1_w_up, m_ffn1_w_down=m_ffn1_w_down, m_mix_norm=m_mix_norm, m_w_in=m_w_in, m_ret_norm_g=m_ret_norm_g, m_hgrn_lb_logits=m_hgrn_lb_logits, m_hgrn_norm_g=m_hgrn_norm_g, m_w_out=m_w_out, m_ffn2_norm=m_ffn2_norm, m_ffn2_w_gate=m_ffn2_w_gate, m_ffn2_w_up=m_ffn2_w_up, m_ffn2_w_down=m_ffn2_w_down, m_final_norm=m_final_norm, v_ffn1_norm=v_ffn1_norm, v_ffn1_w_gate=v_ffn1_w_gate, v_ffn1_w_up=v_ffn1_w_up, v_ffn1_w_down=v_ffn1_w_down, v_mix_norm=v_mix_norm, v_w_in=v_w_in, v_ret_norm_g=v_ret_norm_g, v_hgrn_lb_logits=v_hgrn_lb_logits, v_hgrn_norm_g=v_hgrn_norm_g, v_w_out=v_w_out, v_ffn2_norm=v_ffn2_norm, v_ffn2_w_gate=v_ffn2_w_gate, v_ffn2_w_up=v_ffn2_w_up, v_ffn2_w_down=v_ffn2_w_down, v_final_norm=v_final_norm)
    weights = {n: given[n] for n in TWIN_WEIGHTS}
    shared = {n: given[n] for n in SHARED_INPUTS}
    per_example = {n: given[n] for n in ['x']}
    grad_fn = _jax.value_and_grad(_loss, argnums=(0, 1))

    def one_microbatch(ex, loss_target):
        ex = dict(ex)
        diff = ex.pop(TWIN_DIFF_INPUT)
        return grad_fn(weights, diff, {**shared, **ex}, loss_target)

    if N_MICROBATCH == 1:
        loss, (grad_w, grad_x) = one_microbatch(per_example, given["loss_target"])
    else:
        def body(carry, xs):
            loss_sum, grad_sum = carry
            l_k, (gw_k, gx_k) = one_microbatch(xs[0], xs[1])
            with _jax.named_scope("update"):
                return (loss_sum + l_k, _jax.tree.map(_jnp.add, grad_sum, gw_k)), gx_k

        init = (_jnp.zeros((), _jnp.float32), _jax.tree.map(_jnp.zeros_like, weights))
        (loss, grad_w), grad_x = _jax.lax.scan(body, init, (per_example, given["loss_target"]))
    with _jax.named_scope("update"):
        delta_w, new_m, new_v = {}, {}, {}
        for n in TWIN_WEIGHTS:
            delta_w[n], new_m[n], new_v[n] = _adamw(weights[n], grad_w[n], given["m_" + n], given["v_" + n])
    return (loss, grad_x, *[grad_w[n] for n in TWIN_WEIGHTS], *[delta_w[n] for n in TWIN_WEIGHTS],
            *[new_m[n] for n in TWIN_WEIGHTS], *[new_v[n] for n in TWIN_WEIGHTS])
```

```python
import functools

import jax
import jax.numpy as jnp
from jax import lax
from jax.experimental import pallas as pl
from jax.experimental.pallas import tpu as pltpu

BF = jnp.bfloat16
F32 = jnp.float32
MESH = pl.DeviceIdType.MESH
HBM_SPEC = pl.BlockSpec(memory_space=pltpu.HBM)

N_DEV = 8
LANE = 128
EPS = 1e-6
ROPE_BASE = 10000.0
RET_HEADS = 4
HGRN_HEADS = 8
RET_CHUNK = 128
HGRN_BLOCK = 16
FFN_RESIDUAL_WEIGHT = 0.5
ADAM_LR = 0.001
ADAM_B1 = 0.9
ADAM_B2 = 0.999
ADAM_EPS = 1e-08
ADAM_WD = 0.01
ADAM_STEP = 10
VMEM_LIMIT = 56 * 1024 * 1024


def _tile(n, pref, mult=8):
    t = min(pref, n)
    t -= t % mult
    while t >= mult:
        if n % t == 0:
            return t
        t -= mult
    return n


def _params(*sem):
    return pltpu.CompilerParams(dimension_semantics=sem, vmem_limit_bytes=VMEM_LIMIT)


def _sigmoid(v):
    return 1.0 / (1.0 + jnp.exp(-v))


def _dot(a, b):
    return jnp.dot(a, b, preferred_element_type=F32)


def _dot_nt(a, b):
    return lax.dot_general(a, b, (((1,), (1,)), ((), ())), preferred_element_type=F32)


def _dot_tn(a, b):
    return lax.dot_general(a, b, (((0,), (0,)), ((), ())), preferred_element_type=F32)


def _rmsnorm_call(x, gain, name):
    s, d = x.shape
    tm = _tile(s, 512)

    def body(x_ref, g_ref, o_ref):
        xv = x_ref[...]
        r = lax.rsqrt(jnp.mean(xv * xv, axis=-1, keepdims=True) + EPS)
        o_ref[...] = (xv * r * g_ref[...]).astype(BF)

    return pl.pallas_call(
        body, name=name, grid=(s // tm,),
        in_specs=[pl.BlockSpec((tm, d), lambda i: (i, 0)), pl.BlockSpec((1, d), lambda i: (0, 0))],
        out_specs=pl.BlockSpec((tm, d), lambda i: (i, 0)),
        out_shape=jax.ShapeDtypeStruct((s, d), BF),
        compiler_params=_params("parallel"),
    )(x, gain)


def _ffn_up_call(h, wg, wu, name):
    s, d = h.shape
    nj, _, k = wg.shape
    tm = _tile(s, 512)

    def body(h_ref, wg_ref, wu_ref, g_ref, u_ref, a_ref):
        hv = h_ref[...]
        g = _dot(hv, wg_ref[...])
        u = _dot(hv, wu_ref[...])
        g_ref[...] = g
        u_ref[...] = u
        a_ref[...] = (g * _sigmoid(g) * u).astype(BF)

    act = pl.BlockSpec((tm, k), lambda i, j: (i, j))
    wsp = pl.BlockSpec((None, d, k), lambda i, j: (j, 0, 0))
    return pl.pallas_call(
        body, name=name, grid=(s // tm, nj),
        in_specs=[pl.BlockSpec((tm, d), lambda i, j: (i, 0)), wsp, wsp],
        out_specs=[act, act, act],
        out_shape=[jax.ShapeDtypeStruct((s, nj * k), F32), jax.ShapeDtypeStruct((s, nj * k), F32),
                   jax.ShapeDtypeStruct((s, nj * k), BF)],
        compiler_params=_params("parallel", "arbitrary"),
    )(h, wg, wu)


def _proj_call(h, w, name):
    s, d = h.shape
    nj, _, k = w.shape
    tm = _tile(s, 512)

    def body(h_ref, w_ref, o_ref):
        o_ref[...] = _dot(h_ref[...], w_ref[...])

    return pl.pallas_call(
        body, name=name, grid=(s // tm, nj),
        in_specs=[pl.BlockSpec((tm, d), lambda i, j: (i, 0)), pl.BlockSpec((None, d, k), lambda i, j: (j, 0, 0))],
        out_specs=pl.BlockSpec((tm, k), lambda i, j: (i, j)),
        out_shape=jax.ShapeDtypeStruct((s, nj * k), F32),
        compiler_params=_params("parallel", "arbitrary"),
    )(h, w)


def _down_call(a, w, resid, gain, scale, name):
    s = a.shape[0]
    nj, k, d = w.shape
    tm = _tile(s, 512)
    with_norm = gain is not None

    def body(*refs):
        if with_norm:
            a_ref, w_ref, r_ref, g_ref, x_ref, h_ref, acc = refs
        else:
            a_ref, w_ref, r_ref, x_ref, acc = refs
        j = pl.program_id(1)

        @pl.when(j == 0)
        def _():
            acc[...] = jnp.zeros_like(acc)

        acc[...] += _dot(a_ref[...], w_ref[...])

        @pl.when(j == nj - 1)
        def _():
            xn = r_ref[...] + (scale * acc[...])
            x_ref[...] = xn
            if with_norm:
                r = lax.rsqrt(jnp.mean(xn * xn, axis=-1, keepdims=True) + EPS)
                h_ref[...] = (xn * r * g_ref[...]).astype(BF)

    row = pl.BlockSpec((tm, d), lambda i, j: (i, 0))
    in_specs = [pl.BlockSpec((tm, k), lambda i, j: (i, j)), pl.BlockSpec((None, k, d), lambda i, j: (j, 0, 0)), row]
    args = [a, w, resid]
    out_specs = [row]
    out_shape = [jax.ShapeDtypeStruct((s, d), F32)]
    if with_norm:
        in_specs.append(pl.BlockSpec((1, d), lambda i, j: (0, 0)))
        args.append(gain)
        out_specs.append(row)
        out_shape.append(jax.ShapeDtypeStruct((s, d), BF))
    return pl.pallas_call(
        body, name=name, grid=(s // tm, nj),
        in_specs=in_specs, out_specs=out_specs, out_shape=out_shape,
        scratch_shapes=[pltpu.VMEM((tm, d), F32)],
        compiler_params=_params("parallel", "arbitrary"),
    )(*args)


def _loss_call(x, target, gain, name):
    s, d = x.shape
    tm = _tile(s, 512)

    def body(x_ref, t_ref, g_ref, loss_ref, dx_ref, dxb_ref, dg_ref):
        i = pl.program_id(0)

        @pl.when(i == 0)
        def _():
            loss_ref[...] = jnp.zeros_like(loss_ref)
            dg_ref[...] = jnp.zeros_like(dg_ref)

        xv = x_ref[...]
        gv = g_ref[...]
        r = lax.rsqrt(jnp.mean(xv * xv, axis=-1, keepdims=True) + EPS)
        xhat = xv * r
        err = xhat * gv - t_ref[...]
        per_tok = jnp.mean(err * err, axis=-1, keepdims=True)
        loss_ref[...] += 0.5 * jnp.sum(per_tok, axis=0, keepdims=True)
        dout = err * (1.0 / d)
        dg_ref[...] += jnp.sum(dout * xhat, axis=0, keepdims=True)
        dxhat = dout * gv
        dx = r * (dxhat - xhat * jnp.mean(dxhat * xhat, axis=-1, keepdims=True))
        dx_ref[...] = dx
        dxb_ref[...] = dx.astype(BF)

    row = pl.BlockSpec((tm, d), lambda i: (i, 0))
    vec = pl.BlockSpec((1, d), lambda i: (0, 0))
    return pl.pallas_call(
        body, name=name, grid=(s // tm,),
        in_specs=[row, row, vec],
        out_specs=[pl.BlockSpec((1, 1), lambda i: (0, 0)), row, row, vec],
        out_shape=[jax.ShapeDtypeStruct((1, 1), F32), jax.ShapeDtypeStruct((s, d), F32),
                   jax.ShapeDtypeStruct((s, d), BF), jax.ShapeDtypeStruct((1, d), F32)],
        compiler_params=_params("arbitrary"),
    )(x, target, gain)


def _bwd_up_call(dy, wd, g, u, scale, name):
    s, d = dy.shape
    nj, k, _ = wd.shape
    tm = _tile(s, 512)

    def body(dy_ref, w_ref, g_ref, u_ref, dg_ref, du_ref):
        da = scale * _dot_nt(dy_ref[...], w_ref[...])
        gv = g_ref[...]
        sig = _sigmoid(gv)
        du_ref[...] = (da * gv * sig).astype(BF)
        dg_ref[...] = (da * u_ref[...] * sig * (1.0 + gv * (1.0 - sig))).astype(BF)

    act = pl.BlockSpec((tm, k), lambda i, j: (i, j))
    return pl.pallas_call(
        body, name=name, grid=(s // tm, nj),
        in_specs=[pl.BlockSpec((tm, d), lambda i, j: (i, 0)), pl.BlockSpec((None, k, d), lambda i, j: (j, 0, 0)), act, act],
        out_specs=[act, act],
        out_shape=[jax.ShapeDtypeStruct((s, nj * k), BF), jax.ShapeDtypeStruct((s, nj * k), BF)],
        compiler_params=_params("parallel", "arbitrary"),
    )(dy, wd, g, u)


def _nt_call(dy, w, name):
    s, d = dy.shape
    nj, k, _ = w.shape
    tm = _tile(s, 512)

    def body(dy_ref, w_ref, o_ref):
        o_ref[...] = _dot_nt(dy_ref[...], w_ref[...])

    return pl.pallas_call(
        body, name=name, grid=(s // tm, nj),
        in_specs=[pl.BlockSpec((tm, d), lambda i, j: (i, 0)), pl.BlockSpec((None, k, d), lambda i, j: (j, 0, 0))],
        out_specs=pl.BlockSpec((tm, k), lambda i, j: (i, j)),
        out_shape=jax.ShapeDtypeStruct((s, nj * k), F32),
        compiler_params=_params("parallel", "arbitrary"),
    )(dy, w)


def _bwd_down_call(pairs, dres, xin, gain, name):
    s, d = xin.shape
    nj, _, k = pairs[0][1].shape
    npair = len(pairs)
    tm = _tile(s, 256)

    def body(*refs):
        a_refs = refs[0:2 * npair:2]
        w_refs = refs[1:2 * npair:2]
        dres_ref, x_ref, g_ref, dx_ref, dxb_ref, dg_ref, acc = refs[2 * npair:]
        i = pl.program_id(0)
        j = pl.program_id(1)

        @pl.when(j == 0)
        def _():
            acc[...] = jnp.zeros_like(acc)

        @pl.when((i == 0) & (j == 0))
        def _():
            dg_ref[...] = jnp.zeros_like(dg_ref)

        for a_ref, w_ref in zip(a_refs, w_refs):
            acc[...] += _dot_nt(a_ref[...], w_ref[...])

        @pl.when(j == nj - 1)
        def _():
            xv = x_ref[...]
            r = lax.rsqrt(jnp.mean(xv * xv, axis=-1, keepdims=True) + EPS)
            xhat = xv * r
            dh = acc[...]
            dg_ref[...] += jnp.sum(dh * xhat, axis=0, keepdims=True)
            dxhat = dh * g_ref[...]
            dx = dres_ref[...] + r * (dxhat - xhat * jnp.mean(dxhat * xhat, axis=-1, keepdims=True))
            dx_ref[...] = dx
            dxb_ref[...] = dx.astype(BF)

    row = pl.BlockSpec((tm, d), lambda i, j: (i, 0))
    vec = pl.BlockSpec((1, d), lambda i, j: (0, 0))
    in_specs, args = [], []
    for a, w in pairs:
        in_specs += [pl.BlockSpec((tm, k), lambda i, j: (i, j)), pl.BlockSpec((None, d, k), lambda i, j: (j, 0, 0))]
        args += [a, w]
    in_specs += [row, row, vec]
    args += [dres, xin, gain]
    return pl.pallas_call(
        body, name=name, grid=(s // tm, nj),
        in_specs=in_specs, out_specs=[row, row, vec],
        out_shape=[jax.ShapeDtypeStruct((s, d), F32), jax.ShapeDtypeStruct((s, d), BF), jax.ShapeDtypeStruct((1, d), F32)],
        scratch_shapes=[pltpu.VMEM((tm, d), F32)],
        compiler_params=_params("arbitrary", "arbitrary"),
    )(*args)


def _wgrad_call(a, b, nj, a_blocked, scale, name):
    s = a.shape[0]
    ka = a.shape[1] // nj if a_blocked else a.shape[1]
    kb = b.shape[1] if a_blocked else b.shape[1] // nj
    ts = _tile(s, 1024)
    ns = s // ts

    def body(a_ref, b_ref, o_ref):
        t = pl.program_id(1)

        @pl.when(t == 0)
        def _():
            o_ref[...] = jnp.zeros_like(o_ref)

        o_ref[...] += _dot_tn(a_ref[...], b_ref[...])
        if scale != 1.0:
            @pl.when(t == ns - 1)
            def _():
                o_ref[...] = o_ref[...] * scale

    a_spec = pl.BlockSpec((ts, ka), (lambda j, t: (t, j)) if a_blocked else (lambda j, t: (t, 0)))
    b_spec = pl.BlockSpec((ts, kb), (lambda j, t: (t, 0)) if a_blocked else (lambda j, t: (t, j)))
    return pl.pallas_call(
        body, name=name, grid=(nj, ns),
        in_specs=[a_spec, b_spec],
        out_specs=pl.BlockSpec((None, ka, kb), lambda j, t: (j, 0, 0)),
        out_shape=jax.ShapeDtypeStruct((nj, ka, kb), F32),
        compiler_params=_params("parallel", "arbitrary"),
    )(a, b)


def _rope(v, cos, sin):
    half = v.shape[-1] // 2
    v1, v2 = v[:, :half], v[:, half:]
    return jnp.concatenate([v1 * cos - v2 * sin, v2 * cos + v1 * sin], axis=-1)


def _rope_bwd(dv, cos, sin):
    half = dv.shape[-1] // 2
    d1, d2 = dv[:, :half], dv[:, half:]
    return jnp.concatenate([d1 * cos + d2 * sin, d2 * cos - d1 * sin], axis=-1)


def _ret_consts(hd):
    c = RET_CHUNK
    log_gamma = jnp.log(1.0 - jnp.exp2(-5.0 - jnp.arange(RET_HEADS, dtype=F32)))
    idx = jnp.arange(c, dtype=F32)
    rel = idx[:, None] - idx[None, :]
    mask = rel >= 0
    decay = jnp.where(mask[None], jnp.exp(log_gamma[:, None, None] * jnp.where(mask, rel, 0.0)[None]), 0.0)
    qdec = jnp.exp(log_gamma[:, None] * (idx + 1.0)[None, :])
    kdec = jnp.exp(log_gamma[:, None] * (c - 1.0 - idx)[None, :])
    gchunk = jnp.exp(log_gamma * c)
    bc = lambda t: jnp.broadcast_to(t[:, :, None], (RET_HEADS, t.shape[1], hd))
    return decay, bc(qdec), bc(kdec), bc(gchunk[:, None])


def _rope_tables(s, hd):
    inv = jnp.power(ROPE_BASE, -jnp.arange(0, hd, 2, dtype=F32) / hd)
    ang = jnp.arange(s, dtype=F32)[:, None] * inv[None, :]
    return jnp.cos(ang), jnp.sin(ang)


def _ret_fwd_call(proj, cos, sin, consts, gret, name):
    s = proj.shape[0]
    w = proj.shape[1] // 8
    hd = w // RET_HEADS
    c = RET_CHUNK
    tt = _tile(s, 512, c)
    nc = tt // c
    decay, qdec, kdec, gch = consts
    scale = hd ** -0.5

    def body(q_ref, k_ref, v_ref, gate_ref, cos_ref, sin_ref, dec_ref, qd_ref, kd_ref, gc_ref, gn_ref,
             o_ref, m_ref, st_ref, state):
        @pl.when(pl.program_id(1) == 0)
        def _():
            state[...] = jnp.zeros_like(state)

        dec = dec_ref[...]
        for ci in range(nc):
            rows = slice(ci * c, (ci + 1) * c)
            cs, sn = cos_ref[rows, :], sin_ref[rows, :]
            q = _rope(q_ref[rows, :], cs, sn) * scale
            k = _rope(k_ref[rows, :], cs, sn)
            vb = v_ref[rows, :].astype(BF)
            sc = _dot_nt(q.astype(BF), k.astype(BF)) * dec
            prev = state[...]
            st_ref[ci] = prev
            o = _dot(sc.astype(BF), vb) + _dot((q * qd_ref[...]).astype(BF), prev.astype(BF))
            state[...] = gc_ref[...] * prev + _dot_tn((k * kd_ref[...]).astype(BF), vb)
            o_ref[rows, :] = o
            mu = jnp.mean(o, axis=-1, keepdims=True)
            cen = o - mu
            xhat = cen * lax.rsqrt(jnp.mean(cen * cen, axis=-1, keepdims=True) + EPS)
            gt = gate_ref[rows, :]
            m_ref[rows, :] = (xhat * gn_ref[...] * (gt * _sigmoid(gt))).astype(BF)

    nh = RET_HEADS
    comp = lambda j: pl.BlockSpec((tt, hd), lambda h, t, j=j: (t, j * nh + h))
    tab = pl.BlockSpec((tt, hd // 2), lambda h, t: (t, 0))
    per_head = lambda r: pl.BlockSpec((None, r, hd), lambda h, t: (h, 0, 0))
    return pl.pallas_call(
        body, name=name, grid=(nh, s // tt),
        in_specs=[comp(0), comp(1), comp(2), comp(3), tab, tab,
                  pl.BlockSpec((None, c, c), lambda h, t: (h, 0, 0)), per_head(c), per_head(c), per_head(1),
                  pl.BlockSpec((1, hd), lambda h, t: (0, h))],
        out_specs=[pl.BlockSpec((tt, hd), lambda h, t: (t, h)), pl.BlockSpec((tt, hd), lambda h, t: (t, h)),
                   pl.BlockSpec((None, nc, hd, hd), lambda h, t: (h, t, 0, 0))],
        out_shape=[jax.ShapeDtypeStruct((s, w), F32), jax.ShapeDtypeStruct((s, w), BF),
                   jax.ShapeDtypeStruct((nh, s // c, hd, hd), F32)],
        scratch_shapes=[pltpu.VMEM((hd, hd), F32)],
        compiler_params=_params("parallel", "arbitrary"),
    )(proj, proj, proj, proj, cos, sin, decay, qdec, kdec, gch, gret)


def _ret_bwd_call(proj, cos, sin, consts, gret, o_raw, states, dmerged, name):
    s = proj.shape[0]
    w = proj.shape[1] // 8
    hd = w // RET_HEADS
    c = RET_CHUNK
    tt = _tile(s, 512, c)
    nc = tt // c
    nt = s // tt
    decay, qdec, kdec, gch = consts
    scale = hd ** -0.5

    def body(q_ref, k_ref, v_ref, gate_ref, cos_ref, sin_ref, dec_ref, qd_ref, kd_ref, gc_ref, gn_ref,
             o_ref, st_ref, dm_ref, dq_ref, dk_ref, dv_ref, dgate_ref, dgn_ref, dstate):
        @pl.when(pl.program_id(1) == 0)
        def _():
            dstate[...] = jnp.zeros_like(dstate)
            dgn_ref[...] = jnp.zeros_like(dgn_ref)

        dec = dec_ref[...]
        gn = gn_ref[...]
        for ci in reversed(range(nc)):
            rows = slice(ci * c, (ci + 1) * c)
            cs, sn = cos_ref[rows, :], sin_ref[rows, :]
            q = _rope(q_ref[rows, :], cs, sn) * scale
            k = _rope(k_ref[rows, :], cs, sn)
            qb, kb = q.astype(BF), k.astype(BF)
            vb = v_ref[rows, :].astype(BF)
            sc = _dot_nt(qb, kb) * dec
            o = o_ref[rows, :]
            mu = jnp.mean(o, axis=-1, keepdims=True)
            cen = o - mu
            rstd = lax.rsqrt(jnp.mean(cen * cen, axis=-1, keepdims=True) + EPS)
            xhat = cen * rstd
            gt = gate_ref[rows, :]
            sig = _sigmoid(gt)
            sg = gt * sig
            dm = dm_ref[rows, :]
            dgn_ref[...] += jnp.sum(dm * xhat * sg, axis=0, keepdims=True)
            dgate_ref[rows, :] = (dm * xhat * gn * sig * (1.0 + gt * (1.0 - sig))).astype(BF)
            dxhat = dm * gn * sg
            do = rstd * (dxhat - jnp.mean(dxhat, axis=-1, keepdims=True)
                         - xhat * jnp.mean(dxhat * xhat, axis=-1, keepdims=True))
            dob = do.astype(BF)
            prev = st_ref[ci]
            ds = dstate[...]
            dsb = ds.astype(BF)
            dsc = (_dot_nt(dob, vb) * dec).astype(BF)
            dq = _dot(dsc, kb) + _dot_nt(dob, prev.astype(BF)) * qd_ref[...]
            dk = _dot_tn(dsc, qb) + _dot_nt(vb, dsb) * kd_ref[...]
            dv = _dot_tn(sc.astype(BF), dob) + _dot((k * kd_ref[...]).astype(BF), dsb)
            dstate[...] = gc_ref[...] * ds + _dot_tn((q * qd_ref[...]).astype(BF), dob)
            dq_ref[rows, :] = _rope_bwd(dq * scale, cs, sn).astype(BF)
            dk_ref[rows, :] = _rope_bwd(dk, cs, sn).astype(BF)
            dv_ref[rows, :] = dv.astype(BF)

    nh = RET_HEADS
    rev = lambda t: nt - 1 - t
    comp = lambda j: pl.BlockSpec((tt, hd), lambda h, t, j=j: (rev(t), j * nh + h))
    tab = pl.BlockSpec((tt, hd // 2), lambda h, t: (rev(t), 0))
    per_head = lambda r: pl.BlockSpec((None, r, hd), lambda h, t: (h, 0, 0))
    head_cols = pl.BlockSpec((tt, hd), lambda h, t: (rev(t), h))
    gvec = pl.BlockSpec((1, hd), lambda h, t: (0, h))
    act = jax.ShapeDtypeStruct((s, w), BF)
    return pl.pallas_call(
        body, name=name, grid=(nh, nt),
        in_specs=[comp(0), comp(1), comp(2), comp(3), tab, tab,
                  pl.BlockSpec((None, c, c), lambda h, t: (h, 0, 0)), per_head(c), per_head(c), per_head(1), gvec,
                  head_cols, pl.BlockSpec((None, nc, hd, hd), lambda h, t: (h, rev(t), 0, 0)), head_cols],
        out_specs=[head_cols, head_cols, head_cols, head_cols, gvec],
        out_shape=[act, act, act, act, jax.ShapeDtypeStruct((1, w), F32)],
        scratch_shapes=[pltpu.VMEM((hd, hd), F32)],
        compiler_params=_params("parallel", "arbitrary"),
    )(proj, proj, proj, proj, cos, sin, decay, qdec, kdec, gch, gret, o_raw, states, dmerged)


def _tri(n, upper):
    r = lax.broadcasted_iota(jnp.int32, (n, n), 0)
    cidx = lax.broadcasted_iota(jnp.int32, (n, n), 1)
    return jnp.where((cidx >= r) if upper else (cidx <= r), 1.0, 0.0).astype(F32)


def _dot_exact(a, b):
    return jnp.dot(a, b, preferred_element_type=F32, precision=lax.Precision.HIGHEST)


def _hgrn_gates(z, lbv):
    sz = _sigmoid(z)
    oml = 1.0 - lbv
    f = lbv + oml * sz
    key = oml * (1.0 - sz)
    return sz, f, key


def _hgrn_fwd_call(proj, lb_logits, ghg, name):
    s = proj.shape[0]
    w = proj.shape[1] // 8
    nh = HGRN_HEADS
    hd = w // nh
    bs = HGRN_BLOCK
    tt = _tile(s, 256, bs)
    nb = tt // bs

    def body(q_ref, z_ref, v_ref, gate_ref, lb_ref, gn_ref, o_ref, m_ref, st_ref, state):
        @pl.when(pl.program_id(1) == 0)
        def _():
            state[...] = jnp.zeros_like(state)

        lbv = _sigmoid(lb_ref[...])
        gn = gn_ref[...]
        tril = _tri(bs, upper=False)
        row_id = lax.broadcasted_iota(jnp.int32, (bs, hd), 0)

        def block(b, carry):
            rows = pl.ds(pl.multiple_of(b * bs, bs), bs)
            _, f, key = _hgrn_gates(z_ref[rows, :], lbv)
            qr = q_ref[rows, :]
            q = qr * _sigmoid(qr)
            v = v_ref[rows, :]
            g = _dot_exact(tril, jnp.log(f))
            glast = g[bs - 1:bs, :]
            prev = state[...]
            st_ref[b] = prev
            o = _dot_nt((q * jnp.exp(g)).astype(BF), prev.astype(BF))
            for j in range(bs):
                wj = jnp.where(row_id >= j, jnp.exp(jnp.minimum(g - g[j:j + 1, :], 0.0)), 0.0)
                a = jnp.sum(q * key[j:j + 1, :] * wj, axis=-1, keepdims=True)
                o = o + a * v[j:j + 1, :]
            kt = key * jnp.exp(glast - g)
            state[...] = prev * jnp.exp(glast) + _dot_tn(v.astype(BF), kt.astype(BF))
            o_ref[rows, :] = o
            gt = gate_ref[rows, :]
            xhat = o * lax.rsqrt(jnp.mean(o * o, axis=-1, keepdims=True) + EPS)
            m_ref[rows, :] = (xhat * gn * (gt * _sigmoid(gt))).astype(BF)
            return carry

        lax.fori_loop(0, nb, block, 0)

    comp = lambda j: pl.BlockSpec((tt, hd), lambda h, t, j=j: (t, j * nh + h))
    gvec = pl.BlockSpec((1, hd), lambda h, t: (0, h))
    head_cols = pl.BlockSpec((tt, hd), lambda h, t: (t, h))
    return pl.pallas_call(
        body, name=name, grid=(nh, s // tt),
        in_specs=[comp(4), comp(5), comp(6), comp(7), gvec, gvec],
        out_specs=[head_cols, head_cols, pl.BlockSpec((None, nb, hd, hd), lambda h, t: (h, t, 0, 0))],
        out_shape=[jax.ShapeDtypeStruct((s, w), F32), jax.ShapeDtypeStruct((s, w), BF),
                   jax.ShapeDtypeStruct((nh, s // bs, hd, hd), F32)],
        scratch_shapes=[pltpu.VMEM((hd, hd), F32)],
        compiler_params=_params("parallel", "arbitrary"),
    )(proj, proj, proj, proj, lb_logits, ghg)


def _hgrn_bwd_call(proj, lb_logits, ghg, o_raw, states, dmerged, name):
    s = proj.shape[0]
    w = proj.shape[1] // 8
    nh = HGRN_HEADS
    hd = w // nh
    bs = HGRN_BLOCK
    tt = _tile(s, 256, bs)
    nb = tt // bs
    nt = s // tt

    def body(q_ref, z_ref, v_ref, gate_ref, lb_ref, gn_ref, o_ref, st_ref, dm_ref,
             dq_ref, dz_ref, dv_ref, dgate_ref, dlb_ref, dgn_ref, dstate, dk_rows, dv_rows):
        @pl.when(pl.program_id(1) == 0)
        def _():
            dstate[...] = jnp.zeros_like(dstate)
            dlb_ref[...] = jnp.zeros_like(dlb_ref)
            dgn_ref[...] = jnp.zeros_like(dgn_ref)

        lbv = _sigmoid(lb_ref[...])
        oml = 1.0 - lbv
        gn = gn_ref[...]
        tril = _tri(bs, upper=False)
        triu = _tri(bs, upper=True)
        row_id = lax.broadcasted_iota(jnp.int32, (bs, hd), 0)

        def block(step, carry):
            b = nb - 1 - step
            rows = pl.ds(pl.multiple_of(b * bs, bs), bs)
            sz, f, key = _hgrn_gates(z_ref[rows, :], lbv)
            qr = q_ref[rows, :]
            sq = _sigmoid(qr)
            q = qr * sq
            v = v_ref[rows, :]
            g = _dot_exact(tril, jnp.log(f))
            eg = jnp.exp(g)
            glast = g[bs - 1:bs, :]
            egl = jnp.exp(glast)
            ktail = jnp.exp(glast - g)
            o = o_ref[rows, :]
            rstd = lax.rsqrt(jnp.mean(o * o, axis=-1, keepdims=True) + EPS)
            xhat = o * rstd
            gt = gate_ref[rows, :]
            sig = _sigmoid(gt)
            sg = gt * sig
            dm = dm_ref[rows, :]
            dgn_ref[...] += jnp.sum(dm * xhat * sg, axis=0, keepdims=True)
            dgate_ref[rows, :] = (dm * xhat * gn * sig * (1.0 + gt * (1.0 - sig))).astype(BF)
            dxhat = dm * gn * sg
            do = rstd * (dxhat - xhat * jnp.mean(dxhat * xhat, axis=-1, keepdims=True))
            dob = do.astype(BF)
            prev = st_ref[b]
            ds = dstate[...]
            dsb = ds.astype(BF)
            qg = q * eg
            kt = key * ktail
            dq = _dot(dob, prev.astype(BF)) * eg
            dk = _dot(v.astype(BF), dsb) * ktail
            dv = _dot_nt(kt.astype(BF), dsb)
            dg_last = jnp.sum(key * dk, axis=0, keepdims=True) + egl * jnp.sum(prev * ds, axis=0, keepdims=True)
            dstate[...] = ds * egl + _dot_tn(dob, qg.astype(BF))
            for j in range(bs):
                wj = jnp.where(row_id >= j, jnp.exp(jnp.minimum(g - g[j:j + 1, :], 0.0)), 0.0)
                kj = key[j:j + 1, :]
                a = jnp.sum(q * kj * wj, axis=-1, keepdims=True)
                da = jnp.sum(do * v[j:j + 1, :], axis=-1, keepdims=True)
                dv_rows[j:j + 1, :] = jnp.sum(a * do, axis=0, keepdims=True)
                dq = dq + da * kj * wj
                dk_rows[j:j + 1, :] = jnp.sum(da * q * wj, axis=0, keepdims=True)
            dk = dk + dk_rows[...]
            dv = dv + dv_rows[...]
            dg = q * dq - key * dk + jnp.where(row_id == bs - 1, dg_last, 0.0)
            dlf = _dot_exact(triu, dg)
            dfk = dlf / f - dk
            dlb_ref[...] += jnp.sum(dfk * (1.0 - sz), axis=0, keepdims=True) * (lbv * oml)
            dz_ref[rows, :] = (dfk * oml * sz * (1.0 - sz)).astype(BF)
            dq_ref[rows, :] = (dq * sq * (1.0 + qr * (1.0 - sq))).astype(BF)
            dv_ref[rows, :] = dv.astype(BF)
            return carry

        lax.fori_loop(0, nb, block, 0)

    rev = lambda t: nt - 1 - t
    comp = lambda j: pl.BlockSpec((tt, hd), lambda h, t, j=j: (rev(t), j * nh + h))
    gvec = pl.BlockSpec((1, hd), lambda h, t: (0, h))
    head_cols = pl.BlockSpec((tt, hd), lambda h, t: (rev(t), h))
    act = jax.ShapeDtypeStruct((s, w), BF)
    vec = jax.ShapeDtypeStruct((1, w), F32)
    return pl.pallas_call(
        body, name=name, grid=(nh, nt),
        in_specs=[comp(4), comp(5), comp(6), comp(7), gvec, gvec, head_cols,
                  pl.BlockSpec((None, nb, hd, hd), lambda h, t: (h, rev(t), 0, 0)), head_cols],
        out_specs=[head_cols, head_cols, head_cols, head_cols, gvec, gvec],
        out_shape=[act, act, act, act, vec, vec],
        scratch_shapes=[pltpu.VMEM((hd, hd), F32), pltpu.VMEM((bs, hd), F32), pltpu.VMEM((bs, hd), F32)],
        compiler_params=_params("parallel", "arbitrary"),
    )(proj, proj, proj, proj, lb_logits, ghg, o_raw, states, dmerged)


def _position():
    return lax.axis_index("x"), lax.axis_index("y"), lax.axis_index("c")


def _all_gather_call(shards, name):
    n = len(shards)

    def body(*refs):
        ins, outs = refs[:n], refs[n:2 * n]
        send_sems, recv_sems, local_sems = refs[2 * n:]
        x, y, c = _position()
        me, sibling = (x, y, c), (x, y, 1 - c)
        chips = [(1 - x, y), (x, 1 - y), (1 - x, 1 - y)]

        def slot(a, p):
            return outs[a].at[4 * p[0] + 2 * p[1] + p[2]]

        def copy(a, k, block, to, src=None):
            return pltpu.make_async_remote_copy(
                src_ref=slot(a, block) if src is None else src, dst_ref=slot(a, block),
                send_sem=send_sems.at[a * 7 + k], recv_sem=recv_sems.at[a * 7 + k],
                device_id=to, device_id_type=MESH)

        mine = [pltpu.make_async_copy(ins[a], slot(a, me), local_sems.at[a]) for a in range(n)]
        for cp in mine:
            cp.start()
        first = []
        for a in range(n):
            first.append(copy(a, 0, me, sibling, src=ins[a]))
            first += [copy(a, 1 + j, me, (*chip, c), src=ins[a]) for j, chip in enumerate(chips)]
        for cp in first:
            cp.start()
        passed = []
        for j, chip in enumerate(chips):
            for a in range(n):
                copy(a, 1 + j, (*chip, c), me).wait_recv()
                fwd = copy(a, 4 + j, (*chip, c), sibling)
                fwd.start()
                passed.append(fwd)
        for a in range(n):
            copy(a, 0, sibling, me).wait_recv()
            for j, chip in enumerate(chips):
                copy(a, 4 + j, (*chip, 1 - c), me).wait_recv()
        for cp in first + passed:
            cp.wait_send()
        for cp in mine:
            cp.wait()

    return pl.pallas_call(
        body, name=name,
        in_specs=[HBM_SPEC] * n, out_specs=[HBM_SPEC] * n,
        out_shape=[jax.ShapeDtypeStruct((N_DEV,) + t.shape, t.dtype) for t in shards],
        scratch_shapes=[pltpu.SemaphoreType.DMA((7 * n,)), pltpu.SemaphoreType.DMA((7 * n,)),
                        pltpu.SemaphoreType.DMA((n,))],
    )(*shards)


def _exchange_sibling_call(grads, name):
    n = len(grads)

    def body(*refs):
        ins, outs = refs[:n], refs[n:2 * n]
        send_sems, recv_sems = refs[2 * n:]
        x, y, c = _position()
        copies = []
        for a in range(n):
            for q in range(4):
                copies.append(pltpu.make_async_remote_copy(
                    src_ref=ins[a].at[2 * q + (1 - c)], dst_ref=outs[a].at[q],
                    send_sem=send_sems.at[a * 4 + q], recv_sem=recv_sems.at[a * 4 + q],
                    device_id=(x, y, 1 - c), device_id_type=MESH))
        for cp in copies:
            cp.start()
        for cp in copies:
            cp.wait()

    return pl.pallas_call(
        body, name=name,
        in_specs=[HBM_SPEC] * n, out_specs=[HBM_SPEC] * n,
        out_shape=[jax.ShapeDtypeStruct((4,) + t.shape[1:], t.dtype) for t in grads],
        scratch_shapes=[pltpu.SemaphoreType.DMA((4 * n,)), pltpu.SemaphoreType.DMA((4 * n,))],
    )(*grads)


def _exchange_chips_call(partials, name):
    n = len(partials)

    def body(*refs):
        ins, outs = refs[:n], refs[n:2 * n]
        send_sems, recv_sems = refs[2 * n:]
        x, y, c = _position()
        chips = [(1 - x, y), (x, 1 - y), (1 - x, 1 - y)]
        copies = []
        for a in range(n):
            for k, chip in enumerate(chips):
                copies.append(pltpu.make_async_remote_copy(
                    src_ref=ins[a].at[2 * chip[0] + chip[1]], dst_ref=outs[a].at[k],
                    send_sem=send_sems.at[a * 3 + k], recv_sem=recv_sems.at[a * 3 + k],
                    device_id=(*chip, c), device_id_type=MESH))
        for cp in copies:
            cp.start()
        for cp in copies:
            cp.wait()

    return pl.pallas_call(
        body, name=name,
        in_specs=[HBM_SPEC] * n, out_specs=[HBM_SPEC] * n,
        out_shape=[jax.ShapeDtypeStruct((3,) + t.shape[1:], t.dtype) for t in partials],
        scratch_shapes=[pltpu.SemaphoreType.DMA((3 * n,)), pltpu.SemaphoreType.DMA((3 * n,))],
    )(*partials)


def _pair_sum_call(grad, recv, parity, name):
    _, r, ccols = grad.shape
    tr = _tile(r, 256)

    def body(par_ref, g_ref, r_ref, p_ref, pb_ref):
        del par_ref
        p = g_ref[...] + r_ref[...]
        p_ref[...] = p
        pb_ref[...] = p.astype(BF)

    blk = lambda fn: pl.BlockSpec((None, tr, ccols), fn)
    return pl.pallas_call(
        body, name=name,
        grid_spec=pltpu.PrefetchScalarGridSpec(
            num_scalar_prefetch=1, grid=(4, r // tr),
            in_specs=[blk(lambda q, i, par: (2 * q + par[0], i, 0)), blk(lambda q, i, par: (q, i, 0))],
            out_specs=[blk(lambda q, i, par: (q, i, 0)), blk(lambda q, i, par: (q, i, 0))]),
        out_shape=[jax.ShapeDtypeStruct((4, r, ccols), F32), jax.ShapeDtypeStruct((4, r, ccols), BF)],
        compiler_params=_params("parallel", "parallel"),
    )(parity, grad, recv)


def _adamw_math(w, g, m, v):
    m = ADAM_B1 * m + (1.0 - ADAM_B1) * g
    v = ADAM_B2 * v + (1.0 - ADAM_B2) * (g * g)
    m_hat = m / (1.0 - ADAM_B1 ** ADAM_STEP)
    v_hat = v / (1.0 - ADAM_B2 ** ADAM_STEP)
    delta = -ADAM_LR * (m_hat / (jnp.sqrt(v_hat) + ADAM_EPS) + ADAM_WD * w)
    return delta, m, v


def _adamw_matrix_call(partial, recv, chip, w, m, v, name):
    r, ccols = w.shape
    tr = _tile(r, 256)

    def body(chip_ref, p_ref, r_ref, w_ref, m_ref, v_ref, g_out, d_out, m_out, v_out):
        del chip_ref
        g = p_ref[...] + r_ref[0].astype(F32) + r_ref[1].astype(F32) + r_ref[2].astype(F32)
        delta, mn, vn = _adamw_math(w_ref[...], g, m_ref[...], v_ref[...])
        g_out[...] = g
        d_out[...] = delta
        m_out[...] = mn
        v_out[...] = vn

    mat = pl.BlockSpec((tr, ccols), lambda i, ch: (i, 0))
    shp = jax.ShapeDtypeStruct((r, ccols), F32)
    return pl.pallas_call(
        body, name=name,
        grid_spec=pltpu.PrefetchScalarGridSpec(
            num_scalar_prefetch=1, grid=(r // tr,),
            in_specs=[pl.BlockSpec((None, tr, ccols), lambda i, ch: (ch[0], i, 0)),
                      pl.BlockSpec((3, tr, ccols), lambda i, ch: (0, i, 0)), mat, mat, mat],
            out_specs=[mat, mat, mat, mat]),
        out_shape=[shp, shp, shp, shp],
        compiler_params=_params("parallel"),
    )(chip, partial, recv, w, m, v)


def _adamw_vector_call(gathered, w, m, v, name):
    n = w.shape[1]

    def body(p_ref, w_ref, m_ref, v_ref, g_out, d_out, m_out, v_out):
        g = p_ref[0:1, :]
        for k in range(1, N_DEV):
            g = g + p_ref[k:k + 1, :]
        delta, mn, vn = _adamw_math(w_ref[...], g, m_ref[...], v_ref[...])
        g_out[...] = g
        d_out[...] = delta
        m_out[...] = mn
        v_out[...] = vn

    shp = jax.ShapeDtypeStruct((1, n), F32)
    return pl.pallas_call(body, name=name, out_shape=[shp, shp, shp, shp])(gathered, w, m, v)


def _round_up(n, mult):
    return (n + mult - 1) // mult * mult


def kernel(x, ffn1_norm, ffn1_w_gate, ffn1_w_up, ffn1_w_down, mix_norm, w_in, ret_norm_g, hgrn_lb_logits, hgrn_norm_g, w_out, ffn2_norm, ffn2_w_gate, ffn2_w_up, ffn2_w_down, final_norm, loss_target, m_ffn1_norm, m_ffn1_w_gate, m_ffn1_w_up, m_ffn1_w_down, m_mix_norm, m_w_in, m_ret_norm_g, m_hgrn_lb_logits, m_hgrn_norm_g, m_w_out, m_ffn2_norm, m_ffn2_w_gate, m_ffn2_w_up, m_ffn2_w_down, m_final_norm, v_ffn1_norm, v_ffn1_w_gate, v_ffn1_w_up, v_ffn1_w_down, v_mix_norm, v_w_in, v_ret_norm_g, v_hgrn_lb_logits, v_hgrn_norm_g, v_w_out, v_ffn2_norm, v_ffn2_w_gate, v_ffn2_w_up, v_ffn2_w_down, v_final_norm):
    xs = x[0]
    target = loss_target[0]
    s, d = xs.shape
    f_loc = ffn1_w_gate.shape[2]
    fp = _round_up(f_loc, LANE)
    pad_cols = lambda t: jnp.pad(t[0], ((0, 0), (0, fp - f_loc)))
    pad_rows = lambda t: jnp.pad(t[0], ((0, fp - f_loc), (0, 0)))

    mat_names = ["ffn1_w_gate", "ffn1_w_up", "ffn1_w_down", "w_in", "w_out", "ffn2_w_gate", "ffn2_w_up", "ffn2_w_down"]
    mat_pad = [pad_cols, pad_cols, pad_rows, lambda t: t[0], lambda t: t[0], pad_cols, pad_cols, pad_rows]
    mat_w = [p(t) for p, t in zip(mat_pad, [ffn1_w_gate, ffn1_w_up, ffn1_w_down, w_in, w_out, ffn2_w_gate, ffn2_w_up, ffn2_w_down])]
    mat_m = [p(t) for p, t in zip(mat_pad, [m_ffn1_w_gate, m_ffn1_w_up, m_ffn1_w_down, m_w_in, m_w_out, m_ffn2_w_gate, m_ffn2_w_up, m_ffn2_w_down])]
    mat_v = [p(t) for p, t in zip(mat_pad, [v_ffn1_w_gate, v_ffn1_w_up, v_ffn1_w_down, v_w_in, v_w_out, v_ffn2_w_gate, v_ffn2_w_up, v_ffn2_w_down])]

    wg1, wu1, wd1, win, wout, wg2, wu2, wd2 = _all_gather_call([t.astype(BF) for t in mat_w], "gather_weights")

    h1 = _rmsnorm_call(xs, ffn1_norm, "ffn1_norm")
    g1, u1, a1 = _ffn_up_call(h1, wg1, wu1, "ffn1_up")
    x1, h2 = _down_call(a1, wd1, xs, mix_norm, FFN_RESIDUAL_WEIGHT, "ffn1_down")
    proj = _proj_call(h2, win, "mix_in")
    wmix = proj.shape[1] // 8
    cos, sin = _rope_tables(s, wmix // RET_HEADS)
    consts = _ret_consts(wmix // RET_HEADS)
    o_ret, m_ret, st_ret = _ret_fwd_call(proj, cos, sin, consts, ret_norm_g, "ret_fwd")
    o_hg, m_hg, st_hg = _hgrn_fwd_call(proj, hgrn_lb_logits, hgrn_norm_g, "hgrn_fwd")
    merged = jnp.concatenate([m_ret, m_hg], axis=1)
    x2, h3 = _down_call(merged, wout, x1, ffn2_norm, 1.0, "mix_out")
    g2, u2, a2 = _ffn_up_call(h3, wg2, wu2, "ffn2_up")
    (x3,) = _down_call(a2, wd2, x2, None, FFN_RESIDUAL_WEIGHT, "ffn2_down")
    loss_part, dx3, dx3b, gv_final = _loss_call(x3, target, final_norm[None, :], "loss_head")

    dg2, du2 = _bwd_up_call(dx3b, wd2, g2, u2, FFN_RESIDUAL_WEIGHT, "ffn2_bwd_up")
    dx2, dx2b, gv_n3 = _bwd_down_call([(dg2, wg2), (du2, wu2)], dx3, x2, ffn2_norm, "ffn2_bwd_down")
    gm_g2 = _wgrad_call(h3, dg2, N_DEV, False, 1.0, "ffn2_wgrad_gate")
    gm_u2 = _wgrad_call(h3, du2, N_DEV, False, 1.0, "ffn2_wgrad_up")
    gm_d2 = _wgrad_call(a2, dx3b, N_DEV, True, FFN_RESIDUAL_WEIGHT, "ffn2_wgrad_down")

    dmerged = _nt_call(dx2b, wout, "mix_out_bwd")
    gm_out = _wgrad_call(merged, dx2b, N_DEV, True, 1.0, "mix_out_wgrad")
    drq, drk, drv, drg, gv_ret = _ret_bwd_call(proj, cos, sin, consts, ret_norm_g, o_ret, st_ret,
                                               dmerged[:, :wmix], "ret_bwd")
    dhq, dhf, dhi, dhg, gv_lb, gv_hg = _hgrn_bwd_call(proj, hgrn_lb_logits, hgrn_norm_g, o_hg, st_hg,
                                                       dmerged[:, wmix:], "hgrn_bwd")
    dproj = jnp.concatenate([drq, drk, drv, drg, dhq, dhf, dhi, dhg], axis=1)
    dx1, dx1b, gv_n2 = _bwd_down_call([(dproj, win)], dx2, x1, mix_norm, "mix_in_bwd")
    gm_in = _wgrad_call(h2, dproj, N_DEV, False, 1.0, "mix_in_wgrad")

    dg1, du1 = _bwd_up_call(dx1b, wd1, g1, u1, FFN_RESIDUAL_WEIGHT, "ffn1_bwd_up")
    dx0, _, gv_n1 = _bwd_down_call([(dg1, wg1), (du1, wu1)], dx1, xs, ffn1_norm, "ffn1_bwd_down")
    gm_g1 = _wgrad_call(h1, dg1, N_DEV, False, 1.0, "ffn1_wgrad_gate")
    gm_u1 = _wgrad_call(h1, du1, N_DEV, False, 1.0, "ffn1_wgrad_up")
    gm_d1 = _wgrad_call(a1, dx1b, N_DEV, True, FFN_RESIDUAL_WEIGHT, "ffn1_wgrad_down")

    cx, cy, cc = _position()
    parity = jnp.reshape(cc, (1,)).astype(jnp.int32)
    chip = jnp.reshape(2 * cx + cy, (1,)).astype(jnp.int32)
    mat_g = [gm_g1, gm_u1, gm_d1, gm_in, gm_out, gm_g2, gm_u2, gm_d2]
    from_sibling = _exchange_sibling_call(mat_g, "grads_to_sibling")
    sums = [_pair_sum_call(g, r, parity, "pair_sum_" + nm) for g, r, nm in zip(mat_g, from_sibling, mat_names)]
    from_chips = _exchange_chips_call([pb for _, pb in sums], "grads_to_chips")
    mat_out = {}
    for i, nm in enumerate(mat_names):
        res = _adamw_matrix_call(sums[i][0], from_chips[i], chip, mat_w[i], mat_m[i], mat_v[i], "adamw_" + nm)
        if nm.endswith("gate") or nm.endswith("up"):
            res = [t[:, :f_loc] for t in res]
        elif nm.endswith("down"):
            res = [t[:f_loc, :] for t in res]
        mat_out[nm] = [t[None] for t in res]

    vec_names = ["ffn1_norm", "mix_norm", "ret_norm_g", "hgrn_lb_logits", "hgrn_norm_g", "ffn2_norm", "final_norm"]
    vec_g = [gv_n1, gv_n2, gv_ret, gv_lb, gv_hg, gv_n3, gv_final]
    vec_w = [ffn1_norm, mix_norm, ret_norm_g, hgrn_lb_logits, hgrn_norm_g, ffn2_norm, final_norm[None, :]]
    vec_m = [m_ffn1_norm, m_mix_norm, m_ret_norm_g, m_hgrn_lb_logits, m_hgrn_norm_g, m_ffn2_norm, m_final_norm[None, :]]
    vec_v = [v_ffn1_norm, v_mix_norm, v_ret_norm_g, v_hgrn_lb_logits, v_hgrn_norm_g, v_ffn2_norm, v_final_norm[None, :]]
    cat = lambda ts: jnp.concatenate(ts, axis=1)
    (vec_all,) = _all_gather_call([cat(vec_g)], "gather_vector_grads")
    vres = _adamw_vector_call(vec_all[:, 0, :], cat(vec_w), cat(vec_m), cat(vec_v), "adamw_vectors")
    vec_out = {}
    off = 0
    for nm, t in zip(vec_names, vec_w):
        n = t.shape[1]
        parts = [r[:, off:off + n] for r in vres]
        if nm == "final_norm":
            parts = [p[0] for p in parts]
        vec_out[nm] = parts
        off += n

    loss = lax.psum(loss_part[0, 0], ("x", "y", "c"))
    order = ["ffn1_norm", "ffn1_w_gate", "ffn1_w_up", "ffn1_w_down", "mix_norm", "w_in", "ret_norm_g", "hgrn_lb_logits",
             "hgrn_norm_g", "w_out", "ffn2_norm", "ffn2_w_gate", "ffn2_w_up", "ffn2_w_down", "final_norm"]
    res = {**mat_out, **vec_out}
    outs = [loss, dx0[None]]
    for kind in range(4):
        outs += [res[nm][kind] for nm in order]
    return tuple(outs)
```

```python
import functools

import jax
import jax.numpy as jnp
from jax import lax
from jax.experimental import pallas as pl
from jax.experimental.pallas import tpu as pltpu

BF = jnp.bfloat16
F32 = jnp.float32
MESH = pl.DeviceIdType.MESH
HBM_SPEC = pl.BlockSpec(memory_space=pltpu.HBM)

N_DEV = 8
LANE = 128
EPS = 1e-6
ROPE_BASE = 10000.0
RET_HEADS = 4
HGRN_HEADS = 8
RET_CHUNK = 128
HGRN_BLOCK = 16
FFN_RESIDUAL_WEIGHT = 0.5
ADAM_LR = 0.001
ADAM_B1 = 0.9
ADAM_B2 = 0.999
ADAM_EPS = 1e-08
ADAM_WD = 0.01
ADAM_STEP = 10
VMEM_LIMIT = 56 * 1024 * 1024


def _tile(n, pref, mult=8):
    t = min(pref, n)
    t -= t % mult
    while t >= mult:
        if n % t == 0:
            return t
        t -= mult
    return n


def _params(*sem):
    return pltpu.CompilerParams(dimension_semantics=sem, vmem_limit_bytes=VMEM_LIMIT)


def _sigmoid(v):
    return 1.0 / (1.0 + jnp.exp(-v))


def _dot(a, b):
    return jnp.dot(a, b, preferred_element_type=F32)


def _dot_nt(a, b):
    return lax.dot_general(a, b, (((1,), (1,)), ((), ())), preferred_element_type=F32)


def _dot_tn(a, b):
    return lax.dot_general(a, b, (((0,), (0,)), ((), ())), preferred_element_type=F32)


def _rmsnorm_call(x, gain, name):
    s, d = x.shape
    tm = _tile(s, 512)

    def body(x_ref, g_ref, o_ref):
        xv = x_ref[...]
        r = lax.rsqrt(jnp.mean(xv * xv, axis=-1, keepdims=True) + EPS)
        o_ref[...] = (xv * r * g_ref[...]).astype(BF)

    return pl.pallas_call(
        body, name=name, grid=(s // tm,),
        in_specs=[pl.BlockSpec((tm, d), lambda i: (i, 0)), pl.BlockSpec((1, d), lambda i: (0, 0))],
        out_specs=pl.BlockSpec((tm, d), lambda i: (i, 0)),
        out_shape=jax.ShapeDtypeStruct((s, d), BF),
        compiler_params=_params("parallel"),
    )(x, gain)


def _ffn_up_call(h, wg, wu, name):
    s, d = h.shape
    nj, _, k = wg.shape
    tm = _tile(s, 512)

    def body(h_ref, wg_ref, wu_ref, g_ref, u_ref, a_ref):
        hv = h_ref[...]
        g = _dot(hv, wg_ref[...])
        u = _dot(hv, wu_ref[...])
        g_ref[...] = g
        u_ref[...] = u
        a_ref[...] = (g * _sigmoid(g) * u).astype(BF)

    act = pl.BlockSpec((tm, k), lambda i, j: (i, j))
    wsp = pl.BlockSpec((None, d, k), lambda i, j: (j, 0, 0))
    return pl.pallas_call(
        body, name=name, grid=(s // tm, nj),
        in_specs=[pl.BlockSpec((tm, d), lambda i, j: (i, 0)), wsp, wsp],
        out_specs=[act, act, act],
        out_shape=[jax.ShapeDtypeStruct((s, nj * k), F32), jax.ShapeDtypeStruct((s, nj * k), F32),
                   jax.ShapeDtypeStruct((s, nj * k), BF)],
        compiler_params=_params("parallel", "arbitrary"),
    )(h, wg, wu)


def _proj_call(h, w, name):
    s, d = h.shape
    nj, _, k = w.shape
    tm = _tile(s, 512)

    def body(h_ref, w_ref, o_ref):
        o_ref[...] = _dot(h_ref[...], w_ref[...])

    return pl.pallas_call(
        body, name=name, grid=(s // tm, nj),
        in_specs=[pl.BlockSpec((tm, d), lambda i, j: (i, 0)), pl.BlockSpec((None, d, k), lambda i, j: (j, 0, 0))],
        out_specs=pl.BlockSpec((tm, k), lambda i, j: (i, j)),
        out_shape=jax.ShapeDtypeStruct((s, nj * k), F32),
        compiler_params=_params("parallel", "arbitrary"),
    )(h, w)


def _down_call(a, w, resid, gain, scale, name):
    s = a.shape[0]
    nj, k, d = w.shape
    tm = _tile(s, 512)
    with_norm = gain is not None

    def body(*refs):
        if with_norm:
            a_ref, w_ref, r_ref, g_ref, x_ref, h_ref, acc = refs
        else:
            a_ref, w_ref, r_ref, x_ref, acc = refs
        j = pl.program_id(1)

        @pl.when(j == 0)
        def _():
            acc[...] = jnp.zeros_like(acc)

        acc[...] += _dot(a_ref[...], w_ref[...])

        @pl.when(j == nj - 1)
        def _():
            xn = r_ref[...] + (scale * acc[...])
            x_ref[...] = xn
            if with_norm:
                r = lax.rsqrt(jnp.mean(xn * xn, axis=-1, keepdims=True) + EPS)
                h_ref[...] = (xn * r * g_ref[...]).astype(BF)

    row = pl.BlockSpec((tm, d), lambda i, j: (i, 0))
    in_specs = [pl.BlockSpec((tm, k), lambda i, j: (i, j)), pl.BlockSpec((None, k, d), lambda i, j: (j, 0, 0)), row]
    args = [a, w, resid]
    out_specs = [row]
    out_shape = [jax.ShapeDtypeStruct((s, d), F32)]
    if with_norm:
        in_specs.append(pl.BlockSpec((1, d), lambda i, j: (0, 0)))
        args.append(gain)
        out_specs.append(row)
        out_shape.append(jax.ShapeDtypeStruct((s, d), BF))
    return pl.pallas_call(
        body, name=name, grid=(s // tm, nj),
        in_specs=in_specs, out_specs=out_specs, out_shape=out_shape,
        scratch_shapes=[pltpu.VMEM((tm, d), F32)],
        compiler_params=_params("parallel", "arbitrary"),
    )(*args)


def _loss_call(x, target, gain, name):
    s, d = x.shape
    tm = _tile(s, 512)

    def body(x_ref, t_ref, g_ref, loss_ref, dx_ref, dxb_ref, dg_ref):
        i = pl.program_id(0)

        @pl.when(i == 0)
        def _():
            loss_ref[...] = jnp.zeros_like(loss_ref)
            dg_ref[...] = jnp.zeros_like(dg_ref)

        xv = x_ref[...]
        gv = g_ref[...]
        r = lax.rsqrt(jnp.mean(xv * xv, axis=-1, keepdims=True) + EPS)
        xhat = xv * r
        err = xhat * gv - t_ref[...]
        per_tok = jnp.mean(err * err, axis=-1, keepdims=True)
        loss_ref[...] += 0.5 * jnp.sum(per_tok, axis=0, keepdims=True)
        dout = err * (1.0 / d)
        dg_ref[...] += jnp.sum(dout * xhat, axis=0, keepdims=True)
        dxhat = dout * gv
        dx = r * (dxhat - xhat * jnp.mean(dxhat * xhat, axis=-1, keepdims=True))
        dx_ref[...] = dx
        dxb_ref[...] = dx.astype(BF)

    row = pl.BlockSpec((tm, d), lambda i: (i, 0))
    vec = pl.BlockSpec((1, d), lambda i: (0, 0))
    return pl.pallas_call(
        body, name=name, grid=(s // tm,),
        in_specs=[row, row, vec],
        out_specs=[pl.BlockSpec((1, 1), lambda i: (0, 0)), row, row, vec],
        out_shape=[jax.ShapeDtypeStruct((1, 1), F32), jax.ShapeDtypeStruct((s, d), F32),
                   jax.ShapeDtypeStruct((s, d), BF), jax.ShapeDtypeStruct((1, d), F32)],
        compiler_params=_params("arbitrary"),
    )(x, target, gain)


def _bwd_up_call(dy, wd, g, u, scale, name):
    s, d = dy.shape
    nj, k, _ = wd.shape
    tm = _tile(s, 512)

    def body(dy_ref, w_ref, g_ref, u_ref, dg_ref, du_ref):
        da = scale * _dot_nt(dy_ref[...], w_ref[...])
        gv = g_ref[...]
        sig = _sigmoid(gv)
        du_ref[...] = (da * gv * sig).astype(BF)
        dg_ref[...] = (da * u_ref[...] * sig * (1.0 + gv * (1.0 - sig))).astype(BF)

    act = pl.BlockSpec((tm, k), lambda i, j: (i, j))
    return pl.pallas_call(
        body, name=name, grid=(s // tm, nj),
        in_specs=[pl.BlockSpec((tm, d), lambda i, j: (i, 0)), pl.BlockSpec((None, k, d), lambda i, j: (j, 0, 0)), act, act],
        out_specs=[act, act],
        out_shape=[jax.ShapeDtypeStruct((s, nj * k), BF), jax.ShapeDtypeStruct((s, nj * k), BF)],
        compiler_params=_params("parallel", "arbitrary"),
    )(dy, wd, g, u)


def _nt_call(dy, w, name):
    s, d = dy.shape
    nj, k, _ = w.shape
    tm = _tile(s, 512)

    def body(dy_ref, w_ref, o_ref):
        o_ref[...] = _dot_nt(dy_ref[...], w_ref[...])

    return pl.pallas_call(
        body, name=name, grid=(s // tm, nj),
        in_specs=[pl.BlockSpec((tm, d), lambda i, j: (i, 0)), pl.BlockSpec((None, k, d), lambda i, j: (j, 0, 0))],
        out_specs=pl.BlockSpec((tm, k), lambda i, j: (i, j)),
        out_shape=jax.ShapeDtypeStruct((s, nj * k), F32),
        compiler_params=_params("parallel", "arbitrary"),
    )(dy, w)


def _bwd_down_call(pairs, dres, xin, gain, name):
    s, d = xin.shape
    nj, _, k = pairs[0][1].shape
    npair = len(pairs)
    tm = _tile(s, 512)
    strip = _tile(tm, 128)

    def body(*refs):
        a_refs = refs[0:2 * npair:2]
        w_refs = refs[1:2 * npair:2]
        dres_ref, x_ref, g_ref, dx_ref, dxb_ref, dg_ref, acc = refs[2 * npair:]
        i = pl.program_id(0)
        j = pl.program_id(1)

        @pl.when(j == 0)
        def _():
            acc[...] = jnp.zeros_like(acc)

        @pl.when((i == 0) & (j == 0))
        def _():
            dg_ref[...] = jnp.zeros_like(dg_ref)

        for a_ref, w_ref in zip(a_refs, w_refs):
            acc[...] += _dot_nt(a_ref[...], w_ref[...])

        @pl.when(j == nj - 1)
        def _():
            for r0 in range(0, tm, strip):
                rows = slice(r0, r0 + strip)
                xv = x_ref[rows, :]
                r = lax.rsqrt(jnp.mean(xv * xv, axis=-1, keepdims=True) + EPS)
                xhat = xv * r
                dh = acc[rows, :]
                dg_ref[...] += jnp.sum(dh * xhat, axis=0, keepdims=True)
                dxhat = dh * g_ref[...]
                dx = dres_ref[rows, :] + r * (dxhat - xhat * jnp.mean(dxhat * xhat, axis=-1, keepdims=True))
                dx_ref[rows, :] = dx
                dxb_ref[rows, :] = dx.astype(BF)

    row = pl.BlockSpec((tm, d), lambda i, j: (i, 0))
    vec = pl.BlockSpec((1, d), lambda i, j: (0, 0))
    in_specs, args = [], []
    for a, w in pairs:
        in_specs += [pl.BlockSpec((tm, k), lambda i, j: (i, j)), pl.BlockSpec((None, d, k), lambda i, j: (j, 0, 0))]
        args += [a, w]
    once = pl.BlockSpec((tm, d), lambda i, j: (i, 0), pipeline_mode=pl.Buffered(1))
    in_specs += [once, once, vec]
    args += [dres, xin, gain]
    return pl.pallas_call(
        body, name=name, grid=(s // tm, nj),
        in_specs=in_specs, out_specs=[row, row, vec],
        out_shape=[jax.ShapeDtypeStruct((s, d), F32), jax.ShapeDtypeStruct((s, d), BF), jax.ShapeDtypeStruct((1, d), F32)],
        scratch_shapes=[pltpu.VMEM((tm, d), F32)],
        compiler_params=_params("arbitrary", "arbitrary"),
    )(*args)


def _wgrad_call(a, b, nj, a_blocked, scale, name):
    s = a.shape[0]
    ka = a.shape[1] // nj if a_blocked else a.shape[1]
    kb = b.shape[1] if a_blocked else b.shape[1] // nj
    ts = _tile(s, 1024)
    ns = s // ts

    def body(a_ref, b_ref, o_ref):
        t = pl.program_id(1)

        @pl.when(t == 0)
        def _():
            o_ref[...] = jnp.zeros_like(o_ref)

        o_ref[...] += _dot_tn(a_ref[...], b_ref[...])
        if scale != 1.0:
            @pl.when(t == ns - 1)
            def _():
                o_ref[...] = o_ref[...] * scale

    a_spec = pl.BlockSpec((ts, ka), (lambda j, t: (t, j)) if a_blocked else (lambda j, t: (t, 0)))
    b_spec = pl.BlockSpec((ts, kb), (lambda j, t: (t, 0)) if a_blocked else (lambda j, t: (t, j)))
    return pl.pallas_call(
        body, name=name, grid=(nj, ns),
        in_specs=[a_spec, b_spec],
        out_specs=pl.BlockSpec((None, ka, kb), lambda j, t: (j, 0, 0)),
        out_shape=jax.ShapeDtypeStruct((nj, ka, kb), F32),
        compiler_params=_params("parallel", "arbitrary"),
    )(a, b)


def _rope(v, cos, sin):
    half = v.shape[-1] // 2
    v1, v2 = v[:, :half], v[:, half:]
    return jnp.concatenate([v1 * cos - v2 * sin, v2 * cos + v1 * sin], axis=-1)


def _rope_bwd(dv, cos, sin):
    half = dv.shape[-1] // 2
    d1, d2 = dv[:, :half], dv[:, half:]
    return jnp.concatenate([d1 * cos + d2 * sin, d2 * cos - d1 * sin], axis=-1)


def _ret_consts(hd):
    c = RET_CHUNK
    log_gamma = jnp.log(1.0 - jnp.exp2(-5.0 - jnp.arange(RET_HEADS, dtype=F32)))
    idx = jnp.arange(c, dtype=F32)
    rel = idx[:, None] - idx[None, :]
    mask = rel >= 0
    decay = jnp.where(mask[None], jnp.exp(log_gamma[:, None, None] * jnp.where(mask, rel, 0.0)[None]), 0.0)
    qdec = jnp.exp(log_gamma[:, None] * (idx + 1.0)[None, :])
    kdec = jnp.exp(log_gamma[:, None] * (c - 1.0 - idx)[None, :])
    gchunk = jnp.exp(log_gamma * c)
    bc = lambda t: jnp.broadcast_to(t[:, :, None], (RET_HEADS, t.shape[1], hd))
    return decay, bc(qdec), bc(kdec), bc(gchunk[:, None])


def _rope_tables(s, hd):
    inv = jnp.power(ROPE_BASE, -jnp.arange(0, hd, 2, dtype=F32) / hd)
    ang = jnp.arange(s, dtype=F32)[:, None] * inv[None, :]
    return jnp.cos(ang), jnp.sin(ang)


def _ret_fwd_call(proj, cos, sin, consts, gret, name):
    s = proj.shape[0]
    w = proj.shape[1] // 8
    hd = w // RET_HEADS
    c = RET_CHUNK
    tt = _tile(s, 512, c)
    nc = tt // c
    decay, qdec, kdec, gch = consts
    scale = hd ** -0.5

    def body(q_ref, k_ref, v_ref, gate_ref, cos_ref, sin_ref, dec_ref, qd_ref, kd_ref, gc_ref, gn_ref,
             o_ref, m_ref, st_ref, state):
        @pl.when(pl.program_id(1) == 0)
        def _():
            state[...] = jnp.zeros_like(state)

        dec = dec_ref[...]
        for ci in range(nc):
            rows = slice(ci * c, (ci + 1) * c)
            cs, sn = cos_ref[rows, :], sin_ref[rows, :]
            q = _rope(q_ref[rows, :], cs, sn) * scale
            k = _rope(k_ref[rows, :], cs, sn)
            vb = v_ref[rows, :].astype(BF)
            sc = _dot_nt(q.astype(BF), k.astype(BF)) * dec
            prev = state[...]
            st_ref[ci] = prev
            o = _dot(sc.astype(BF), vb) + _dot((q * qd_ref[...]).astype(BF), prev.astype(BF))
            state[...] = gc_ref[...] * prev + _dot_tn((k * kd_ref[...]).astype(BF), vb)
            o_ref[rows, :] = o
            mu = jnp.mean(o, axis=-1, keepdims=True)
            cen = o - mu
            xhat = cen * lax.rsqrt(jnp.mean(cen * cen, axis=-1, keepdims=True) + EPS)
            gt = gate_ref[rows, :]
            m_ref[rows, :] = (xhat * gn_ref[...] * (gt * _sigmoid(gt))).astype(BF)

    nh = RET_HEADS
    comp = lambda j: pl.BlockSpec((tt, hd), lambda h, t, j=j: (t, j * nh + h))
    tab = pl.BlockSpec((tt, hd // 2), lambda h, t: (t, 0))
    per_head = lambda r: pl.BlockSpec((None, r, hd), lambda h, t: (h, 0, 0))
    return pl.pallas_call(
        body, name=name, grid=(nh, s // tt),
        in_specs=[comp(0), comp(1), comp(2), comp(3), tab, tab,
                  pl.BlockSpec((None, c, c), lambda h, t: (h, 0, 0)), per_head(c), per_head(c), per_head(1),
                  pl.BlockSpec((1, hd), lambda h, t: (0, h))],
        out_specs=[pl.BlockSpec((tt, hd), lambda h, t: (t, h)), pl.BlockSpec((tt, hd), lambda h, t: (t, h)),
                   pl.BlockSpec((None, nc, hd, hd), lambda h, t: (h, t, 0, 0))],
        out_shape=[jax.ShapeDtypeStruct((s, w), F32), jax.ShapeDtypeStruct((s, w), BF),
                   jax.ShapeDtypeStruct((nh, s // c, hd, hd), F32)],
        scratch_shapes=[pltpu.VMEM((hd, hd), F32)],
        compiler_params=_params("parallel", "arbitrary"),
    )(proj, proj, proj, proj, cos, sin, decay, qdec, kdec, gch, gret)


def _ret_bwd_call(proj, cos, sin, consts, gret, o_raw, states, dmerged, name):
    s = proj.shape[0]
    w = proj.shape[1] // 8
    hd = w // RET_HEADS
    c = RET_CHUNK
    tt = _tile(s, 512, c)
    nc = tt // c
    nt = s // tt
    decay, qdec, kdec, gch = consts
    scale = hd ** -0.5

    def body(q_ref, k_ref, v_ref, gate_ref, cos_ref, sin_ref, dec_ref, qd_ref, kd_ref, gc_ref, gn_ref,
             o_ref, st_ref, dm_ref, dq_ref, dk_ref, dv_ref, dgate_ref, dgn_ref, dstate):
        @pl.when(pl.program_id(1) == 0)
        def _():
            dstate[...] = jnp.zeros_like(dstate)
            dgn_ref[...] = jnp.zeros_like(dgn_ref)

        dec = dec_ref[...]
        gn = gn_ref[...]
        for ci in reversed(range(nc)):
            rows = slice(ci * c, (ci + 1) * c)
            cs, sn = cos_ref[rows, :], sin_ref[rows, :]
            q = _rope(q_ref[rows, :], cs, sn) * scale
            k = _rope(k_ref[rows, :], cs, sn)
            qb, kb = q.astype(BF), k.astype(BF)
            vb = v_ref[rows, :].astype(BF)
            sc = _dot_nt(qb, kb) * dec
            o = o_ref[rows, :]
            mu = jnp.mean(o, axis=-1, keepdims=True)
            cen = o - mu
            rstd = lax.rsqrt(jnp.mean(cen * cen, axis=-1, keepdims=True) + EPS)
            xhat = cen * rstd
            gt = gate_ref[rows, :]
            sig = _sigmoid(gt)
            sg = gt * sig
            dm = dm_ref[rows, :]
            dgn_ref[...] += jnp.sum(dm * xhat * sg, axis=0, keepdims=True)
            dgate_ref[rows, :] = (dm * xhat * gn * sig * (1.0 + gt * (1.0 - sig))).astype(BF)
            dxhat = dm * gn * sg
            do = rstd * (dxhat - jnp.mean(dxhat, axis=-1, keepdims=True)
                         - xhat * jnp.mean(dxhat * xhat, axis=-1, keepdims=True))
            dob = do.astype(BF)
            prev = st_ref[ci]
            ds = dstate[...]
            dsb = ds.astype(BF)
            dsc = (_dot_nt(dob, vb) * dec).astype(BF)
            dq = _dot(dsc, kb) + _dot_nt(dob, prev.astype(BF)) * qd_ref[...]
            dk = _dot_tn(dsc, qb) + _dot_nt(vb, dsb) * kd_ref[...]
            dv = _dot_tn(sc.astype(BF), dob) + _dot((k * kd_ref[...]).astype(BF), dsb)
            dstate[...] = gc_ref[...] * ds + _dot_tn((q * qd_ref[...]).astype(BF), dob)
            dq_ref[rows, :] = _rope_bwd(dq * scale, cs, sn).astype(BF)
            dk_ref[rows, :] = _rope_bwd(dk, cs, sn).astype(BF)
            dv_ref[rows, :] = dv.astype(BF)

    nh = RET_HEADS
    rev = lambda t: nt - 1 - t
    comp = lambda j: pl.BlockSpec((tt, hd), lambda h, t, j=j: (rev(t), j * nh + h))
    tab = pl.BlockSpec((tt, hd // 2), lambda h, t: (rev(t), 0))
    per_head = lambda r: pl.BlockSpec((None, r, hd), lambda h, t: (h, 0, 0))
    head_cols = pl.BlockSpec((tt, hd), lambda h, t: (rev(t), h))
    gvec = pl.BlockSpec((1, hd), lambda h, t: (0, h))
    act = jax.ShapeDtypeStruct((s, w), BF)
    return pl.pallas_call(
        body, name=name, grid=(nh, nt),
        in_specs=[comp(0), comp(1), comp(2), comp(3), tab, tab,
                  pl.BlockSpec((None, c, c), lambda h, t: (h, 0, 0)), per_head(c), per_head(c), per_head(1), gvec,
                  head_cols, pl.BlockSpec((None, nc, hd, hd), lambda h, t: (h, rev(t), 0, 0)), head_cols],
        out_specs=[head_cols, head_cols, head_cols, head_cols, gvec],
        out_shape=[act, act, act, act, jax.ShapeDtypeStruct((1, w), F32)],
        scratch_shapes=[pltpu.VMEM((hd, hd), F32)],
        compiler_params=_params("parallel", "arbitrary"),
    )(proj, proj, proj, proj, cos, sin, decay, qdec, kdec, gch, gret, o_raw, states, dmerged)


def _block_tri(n, bs, upper):
    r = jnp.arange(n)[:, None]
    cidx = jnp.arange(n)[None, :]
    same = (r // bs) == (cidx // bs)
    return jnp.where(same & ((cidx >= r) if upper else (cidx <= r)), 1.0, 0.0).astype(F32)


def _dot_exact(a, b):
    return jnp.dot(a, b, preferred_element_type=F32, precision=lax.Precision.HIGHEST)


def _hgrn_gates(z, lbv):
    sz = _sigmoid(z)
    oml = 1.0 - lbv
    f = lbv + oml * sz
    key = oml * (1.0 - sz)
    return sz, f, key


def _hgrn_fwd_call(proj, lb_logits, ghg, name):
    s = proj.shape[0]
    w = proj.shape[1] // 8
    nh = HGRN_HEADS
    hd = w // nh
    bs = HGRN_BLOCK
    tt = _tile(s, 256, bs)
    nb = tt // bs

    def body(q_ref, z_ref, v_ref, gate_ref, lb_ref, gn_ref, tril_ref, o_ref, m_ref, st_ref, state, upd):
        @pl.when(pl.program_id(1) == 0)
        def _():
            state[...] = jnp.zeros_like(state)

        lbv = _sigmoid(lb_ref[...])
        _, f, key = _hgrn_gates(z_ref[...], lbv)
        qr = q_ref[...]
        q = qr * _sigmoid(qr)
        v = v_ref[...]
        g = _dot_exact(tril_ref[...], jnp.log(f))
        blocks = lambda t: t.reshape(nb, bs, hd)
        g3, q3, k3, v3 = blocks(g), blocks(q), blocks(key), blocks(v)
        glast3 = g3[:, bs - 1:bs, :]
        row_id = lax.broadcasted_iota(jnp.int32, (nb, bs, hd), 1)
        o3 = jnp.zeros((nb, bs, hd), F32)
        for j in range(bs):
            wj = jnp.where(row_id >= j, jnp.exp(jnp.minimum(g3 - g3[:, j:j + 1, :], 0.0)), 0.0)
            a = jnp.sum(q3 * k3[:, j:j + 1, :] * wj, axis=-1, keepdims=True)
            o3 = o3 + a * v3[:, j:j + 1, :]
        ktb = (k3 * jnp.exp(glast3 - g3)).reshape(tt, hd).astype(BF)
        vb = v.astype(BF)
        for b in range(nb):
            rows = slice(b * bs, (b + 1) * bs)
            upd[b] = _dot_tn(vb[rows, :], ktb[rows, :])
        egl3 = jnp.exp(glast3)
        st = state[...]
        for b in range(nb):
            st_ref[b] = st
            st = st * egl3[b] + upd[b]
        state[...] = st
        qgb = (q * jnp.exp(g)).astype(BF)
        o_intra = o3.reshape(tt, hd)
        gn = gn_ref[...]
        for b in range(nb):
            rows = slice(b * bs, (b + 1) * bs)
            o = o_intra[rows, :] + _dot_nt(qgb[rows, :], st_ref[b].astype(BF))
            o_ref[rows, :] = o
            gt = gate_ref[rows, :]
            xhat = o * lax.rsqrt(jnp.mean(o * o, axis=-1, keepdims=True) + EPS)
            m_ref[rows, :] = (xhat * gn * (gt * _sigmoid(gt))).astype(BF)

    comp = lambda j: pl.BlockSpec((tt, hd), lambda h, t, j=j: (t, j * nh + h))
    gvec = pl.BlockSpec((1, hd), lambda h, t: (0, h))
    head_cols = pl.BlockSpec((tt, hd), lambda h, t: (t, h))
    return pl.pallas_call(
        body, name=name, grid=(nh, s // tt),
        in_specs=[comp(4), comp(5), comp(6), comp(7), gvec, gvec, pl.BlockSpec((tt, tt), lambda h, t: (0, 0))],
        out_specs=[head_cols, head_cols, pl.BlockSpec((None, nb, hd, hd), lambda h, t: (h, t, 0, 0))],
        out_shape=[jax.ShapeDtypeStruct((s, w), F32), jax.ShapeDtypeStruct((s, w), BF),
                   jax.ShapeDtypeStruct((nh, s // bs, hd, hd), F32)],
        scratch_shapes=[pltpu.VMEM((hd, hd), F32), pltpu.VMEM((nb, hd, hd), F32)],
        compiler_params=_params("parallel", "arbitrary"),
    )(proj, proj, proj, proj, lb_logits, ghg, _block_tri(tt, bs, upper=False))


def _hgrn_bwd_call(proj, lb_logits, ghg, o_raw, states, dmerged, name):
    s = proj.shape[0]
    w = proj.shape[1] // 8
    nh = HGRN_HEADS
    hd = w // nh
    bs = HGRN_BLOCK
    tt = _tile(s, 256, bs)
    nb = tt // bs
    nt = s // tt

    def body(q_ref, z_ref, v_ref, gate_ref, lb_ref, gn_ref, tril_ref, triu_ref, o_ref, st_ref, dm_ref,
             dq_ref, dz_ref, dv_ref, dgate_ref, dlb_ref, dgn_ref,
             dstate, ds_all, inc, dq_s, dk_s, dv_s, dgl_s, dk_rows, dv_rows):
        @pl.when(pl.program_id(1) == 0)
        def _():
            dstate[...] = jnp.zeros_like(dstate)
            dlb_ref[...] = jnp.zeros_like(dlb_ref)
            dgn_ref[...] = jnp.zeros_like(dgn_ref)

        lbv = _sigmoid(lb_ref[...])
        oml = 1.0 - lbv
        gn = gn_ref[...]
        sz, f, key = _hgrn_gates(z_ref[...], lbv)
        qr = q_ref[...]
        sq = _sigmoid(qr)
        q = qr * sq
        v = v_ref[...]
        g = _dot_exact(tril_ref[...], jnp.log(f))
        eg = jnp.exp(g)
        blocks = lambda t: t.reshape(nb, bs, hd)
        g3, q3, k3, v3 = blocks(g), blocks(q), blocks(key), blocks(v)
        glast3 = g3[:, bs - 1:bs, :]
        egl3 = jnp.exp(glast3)
        ktail3 = jnp.exp(glast3 - g3)
        o = o_ref[...]
        rstd = lax.rsqrt(jnp.mean(o * o, axis=-1, keepdims=True) + EPS)
        xhat = o * rstd
        gt = gate_ref[...]
        sig = _sigmoid(gt)
        sg = gt * sig
        dm = dm_ref[...]
        dgn_ref[...] += jnp.sum(dm * xhat * sg, axis=0, keepdims=True)
        dgate_ref[...] = (dm * xhat * gn * sig * (1.0 + gt * (1.0 - sig))).astype(BF)
        dxhat = dm * gn * sg
        do = rstd * (dxhat - xhat * jnp.mean(dxhat * xhat, axis=-1, keepdims=True))
        dob = do.astype(BF)
        do3 = blocks(do)
        qgb = (q * eg).astype(BF)
        for b in range(nb):
            rows = slice(b * bs, (b + 1) * bs)
            inc[b] = _dot_tn(dob[rows, :], qgb[rows, :])
        ds = dstate[...]
        for b in reversed(range(nb)):
            ds_all[b] = ds
            ds = ds * egl3[b] + inc[b]
        dstate[...] = ds
        ktb = (k3 * ktail3).reshape(tt, hd).astype(BF)
        vb = v.astype(BF)
        for b in range(nb):
            rows = slice(b * bs, (b + 1) * bs)
            prev = st_ref[b]
            dsb = ds_all[b]
            dsbb = dsb.astype(BF)
            dq_s[rows, :] = _dot(dob[rows, :], prev.astype(BF))
            dk_s[rows, :] = _dot(vb[rows, :], dsbb)
            dv_s[rows, :] = _dot_nt(ktb[rows, :], dsbb)
            dgl_s[b] = jnp.sum(prev * dsb, axis=0, keepdims=True)
        dq3 = blocks(dq_s[...] * eg)
        dk3 = blocks(dk_s[...]) * ktail3
        dg_last3 = jnp.sum(k3 * dk3, axis=1, keepdims=True) + egl3 * dgl_s[...]
        row_id = lax.broadcasted_iota(jnp.int32, (nb, bs, hd), 1)
        for j in range(bs):
            wj = jnp.where(row_id >= j, jnp.exp(jnp.minimum(g3 - g3[:, j:j + 1, :], 0.0)), 0.0)
            kj = k3[:, j:j + 1, :]
            a = jnp.sum(q3 * kj * wj, axis=-1, keepdims=True)
            da = jnp.sum(do3 * v3[:, j:j + 1, :], axis=-1, keepdims=True)
            dv_rows[:, j:j + 1, :] = jnp.sum(a * do3, axis=1, keepdims=True)
            dq3 = dq3 + da * kj * wj
            dk_rows[:, j:j + 1, :] = jnp.sum(da * q3 * wj, axis=1, keepdims=True)
        dk3 = dk3 + dk_rows[...]
        dv = dv_s[...] + dv_rows[...].reshape(tt, hd)
        dg3 = q3 * dq3 - k3 * dk3 + jnp.where(row_id == bs - 1, dg_last3, 0.0)
        dlf = _dot_exact(triu_ref[...], dg3.reshape(tt, hd))
        dk = dk3.reshape(tt, hd)
        dfk = dlf / f - dk
        dlb_ref[...] += jnp.sum(dfk * (1.0 - sz), axis=0, keepdims=True) * (lbv * oml)
        dz_ref[...] = (dfk * oml * sz * (1.0 - sz)).astype(BF)
        dq_ref[...] = (dq3.reshape(tt, hd) * sq * (1.0 + qr * (1.0 - sq))).astype(BF)
        dv_ref[...] = dv.astype(BF)

    rev = lambda t: nt - 1 - t
    comp = lambda j: pl.BlockSpec((tt, hd), lambda h, t, j=j: (rev(t), j * nh + h))
    gvec = pl.BlockSpec((1, hd), lambda h, t: (0, h))
    head_cols = pl.BlockSpec((tt, hd), lambda h, t: (rev(t), h))
    tri = pl.BlockSpec((tt, tt), lambda h, t: (0, 0))
    act = jax.ShapeDtypeStruct((s, w), BF)
    vec = jax.ShapeDtypeStruct((1, w), F32)
    tile_f32 = pltpu.VMEM((tt, hd), F32)
    return pl.pallas_call(
        body, name=name, grid=(nh, nt),
        in_specs=[comp(4), comp(5), comp(6), comp(7), gvec, gvec, tri, tri, head_cols,
                  pl.BlockSpec((None, nb, hd, hd), lambda h, t: (h, rev(t), 0, 0)), head_cols],
        out_specs=[head_cols, head_cols, head_cols, head_cols, gvec, gvec],
        out_shape=[act, act, act, act, vec, vec],
        scratch_shapes=[pltpu.VMEM((hd, hd), F32), pltpu.VMEM((nb, hd, hd), F32), pltpu.VMEM((nb, hd, hd), F32),
                        tile_f32, tile_f32, tile_f32, pltpu.VMEM((nb, 1, hd), F32),
                        pltpu.VMEM((nb, bs, hd), F32), pltpu.VMEM((nb, bs, hd), F32)],
        compiler_params=_params("parallel", "arbitrary"),
    )(proj, proj, proj, proj, lb_logits, ghg, _block_tri(tt, bs, upper=False), _block_tri(tt, bs, upper=True),
      o_raw, states, dmerged)


def _position():
    return lax.axis_index("x"), lax.axis_index("y"), lax.axis_index("c")


def _all_gather_call(shards, name):
    n = len(shards)

    def body(*refs):
        ins, outs = refs[:n], refs[n:2 * n]
        send_sems, recv_sems, local_sems = refs[2 * n:]
        x, y, c = _position()
        me, sibling = (x, y, c), (x, y, 1 - c)
        chips = [(1 - x, y), (x, 1 - y), (1 - x, 1 - y)]

        def slot(a, p):
            return outs[a].at[4 * p[0] + 2 * p[1] + p[2]]

        def copy(a, k, block, to, src=None):
            return pltpu.make_async_remote_copy(
                src_ref=slot(a, block) if src is None else src, dst_ref=slot(a, block),
                send_sem=send_sems.at[a * 7 + k], recv_sem=recv_sems.at[a * 7 + k],
                device_id=to, device_id_type=MESH)

        mine = [pltpu.make_async_copy(ins[a], slot(a, me), local_sems.at[a]) for a in range(n)]
        for cp in mine:
            cp.start()
        first = []
        for a in range(n):
            first.append(copy(a, 0, me, sibling, src=ins[a]))
            first += [copy(a, 1 + j, me, (*chip, c), src=ins[a]) for j, chip in enumerate(chips)]
        for cp in first:
            cp.start()
        passed = []
        for j, chip in enumerate(chips):
            for a in range(n):
                copy(a, 1 + j, (*chip, c), me).wait_recv()
                fwd = copy(a, 4 + j, (*chip, c), sibling)
                fwd.start()
                passed.append(fwd)
        for a in range(n):
            copy(a, 0, sibling, me).wait_recv()
            for j, chip in enumerate(chips):
                copy(a, 4 + j, (*chip, 1 - c), me).wait_recv()
        for cp in first + passed:
            cp.wait_send()
        for cp in mine:
            cp.wait()

    return pl.pallas_call(
        body, name=name,
        in_specs=[HBM_SPEC] * n, out_specs=[HBM_SPEC] * n,
        out_shape=[jax.ShapeDtypeStruct((N_DEV,) + t.shape, t.dtype) for t in shards],
        scratch_shapes=[pltpu.SemaphoreType.DMA((7 * n,)), pltpu.SemaphoreType.DMA((7 * n,)),
                        pltpu.SemaphoreType.DMA((n,))],
    )(*shards)


def _exchange_sibling_call(grads, name):
    n = len(grads)

    def body(*refs):
        ins, outs = refs[:n], refs[n:2 * n]
        send_sems, recv_sems = refs[2 * n:]
        x, y, c = _position()
        copies = []
        for a in range(n):
            for q in range(4):
                copies.append(pltpu.make_async_remote_copy(
                    src_ref=ins[a].at[2 * q + (1 - c)], dst_ref=outs[a].at[q],
                    send_sem=send_sems.at[a * 4 + q], recv_sem=recv_sems.at[a * 4 + q],
                    device_id=(x, y, 1 - c), device_id_type=MESH))
        for cp in copies:
            cp.start()
        for cp in copies:
            cp.wait()

    return pl.pallas_call(
        body, name=name,
        in_specs=[HBM_SPEC] * n, out_specs=[HBM_SPEC] * n,
        out_shape=[jax.ShapeDtypeStruct((4,) + t.shape[1:], t.dtype) for t in grads],
        scratch_shapes=[pltpu.SemaphoreType.DMA((4 * n,)), pltpu.SemaphoreType.DMA((4 * n,))],
    )(*grads)


def _exchange_chips_call(partials, name):
    n = len(partials)

    def body(*refs):
        ins, outs = refs[:n], refs[n:2 * n]
        send_sems, recv_sems = refs[2 * n:]
        x, y, c = _position()
        chips = [(1 - x, y), (x, 1 - y), (1 - x, 1 - y)]
        copies = []
        for a in range(n):
            for k, chip in enumerate(chips):
                copies.append(pltpu.make_async_remote_copy(
                    src_ref=ins[a].at[2 * chip[0] + chip[1]], dst_ref=outs[a].at[k],
                    send_sem=send_sems.at[a * 3 + k], recv_sem=recv_sems.at[a * 3 + k],
                    device_id=(*chip, c), device_id_type=MESH))
        for cp in copies:
            cp.start()
        for cp in copies:
            cp.wait()

    return pl.pallas_call(
        body, name=name,
        in_specs=[HBM_SPEC] * n, out_specs=[HBM_SPEC] * n,
        out_shape=[jax.ShapeDtypeStruct((3,) + t.shape[1:], t.dtype) for t in partials],
        scratch_shapes=[pltpu.SemaphoreType.DMA((3 * n,)), pltpu.SemaphoreType.DMA((3 * n,))],
    )(*partials)


def _pair_sum_call(grad, recv, parity, name):
    _, r, ccols = grad.shape
    tr = _tile(r, 256)

    def body(par_ref, g_ref, r_ref, p_ref, pb_ref):
        del par_ref
        p = g_ref[...] + r_ref[...]
        p_ref[...] = p
        pb_ref[...] = p.astype(BF)

    blk = lambda fn: pl.BlockSpec((None, tr, ccols), fn)
    return pl.pallas_call(
        body, name=name,
        grid_spec=pltpu.PrefetchScalarGridSpec(
            num_scalar_prefetch=1, grid=(4, r // tr),
            in_specs=[blk(lambda q, i, par: (2 * q + par[0], i, 0)), blk(lambda q, i, par: (q, i, 0))],
            out_specs=[blk(lambda q, i, par: (q, i, 0)), blk(lambda q, i, par: (q, i, 0))]),
        out_shape=[jax.ShapeDtypeStruct((4, r, ccols), F32), jax.ShapeDtypeStruct((4, r, ccols), BF)],
        compiler_params=_params("parallel", "parallel"),
    )(parity, grad, recv)


def _adamw_math(w, g, m, v):
    m = ADAM_B1 * m + (1.0 - ADAM_B1) * g
    v = ADAM_B2 * v + (1.0 - ADAM_B2) * (g * g)
    m_hat = m / (1.0 - ADAM_B1 ** ADAM_STEP)
    v_hat = v / (1.0 - ADAM_B2 ** ADAM_STEP)
    delta = -ADAM_LR * (m_hat / (jnp.sqrt(v_hat) + ADAM_EPS) + ADAM_WD * w)
    return delta, m, v


def _adamw_matrix_call(partial, recv, chip, w, m, v, name):
    r, ccols = w.shape
    tr = _tile(r, 256)

    def body(chip_ref, p_ref, r_ref, w_ref, m_ref, v_ref, g_out, d_out, m_out, v_out):
        del chip_ref
        g = p_ref[...] + r_ref[0].astype(F32) + r_ref[1].astype(F32) + r_ref[2].astype(F32)
        delta, mn, vn = _adamw_math(w_ref[...], g, m_ref[...], v_ref[...])
        g_out[...] = g
        d_out[...] = delta
        m_out[...] = mn
        v_out[...] = vn

    mat = pl.BlockSpec((tr, ccols), lambda i, ch: (i, 0))
    shp = jax.ShapeDtypeStruct((r, ccols), F32)
    return pl.pallas_call(
        body, name=name,
        grid_spec=pltpu.PrefetchScalarGridSpec(
            num_scalar_prefetch=1, grid=(r // tr,),
            in_specs=[pl.BlockSpec((None, tr, ccols), lambda i, ch: (ch[0], i, 0)),
                      pl.BlockSpec((3, tr, ccols), lambda i, ch: (0, i, 0)), mat, mat, mat],
            out_specs=[mat, mat, mat, mat]),
        out_shape=[shp, shp, shp, shp],
        compiler_params=_params("parallel"),
    )(chip, partial, recv, w, m, v)


def _adamw_vector_call(gathered, w, m, v, name):
    n = w.shape[1]

    def body(p_ref, w_ref, m_ref, v_ref, g_out, d_out, m_out, v_out):
        g = p_ref[0:1, :]
        for k in range(1, N_DEV):
            g = g + p_ref[k:k + 1, :]
        delta, mn, vn = _adamw_math(w_ref[...], g, m_ref[...], v_ref[...])
        g_out[...] = g
        d_out[...] = delta
        m_out[...] = mn
        v_out[...] = vn

    shp = jax.ShapeDtypeStruct((1, n), F32)
    return pl.pallas_call(body, name=name, out_shape=[shp, shp, shp, shp])(gathered, w, m, v)


def _round_up(n, mult):
    return (n + mult - 1) // mult * mult


def kernel(x, ffn1_norm, ffn1_w_gate, ffn1_w_up, ffn1_w_down, mix_norm, w_in, ret_norm_g, hgrn_lb_logits, hgrn_norm_g, w_out, ffn2_norm, ffn2_w_gate, ffn2_w_up, ffn2_w_down, final_norm, loss_target, m_ffn1_norm, m_ffn1_w_gate, m_ffn1_w_up, m_ffn1_w_down, m_mix_norm, m_w_in, m_ret_norm_g, m_hgrn_lb_logits, m_hgrn_norm_g, m_w_out, m_ffn2_norm, m_ffn2_w_gate, m_ffn2_w_up, m_ffn2_w_down, m_final_norm, v_ffn1_norm, v_ffn1_w_gate, v_ffn1_w_up, v_ffn1_w_down, v_mix_norm, v_w_in, v_ret_norm_g, v_hgrn_lb_logits, v_hgrn_norm_g, v_w_out, v_ffn2_norm, v_ffn2_w_gate, v_ffn2_w_up, v_ffn2_w_down, v_final_norm):
    xs = x[0]
    target = loss_target[0]
    s, d = xs.shape
    f_loc = ffn1_w_gate.shape[2]
    fp = _round_up(f_loc, LANE)
    pad_cols = lambda t: jnp.pad(t[0], ((0, 0), (0, fp - f_loc)))
    pad_rows = lambda t: jnp.pad(t[0], ((0, fp - f_loc), (0, 0)))

    mat_names = ["ffn1_w_gate", "ffn1_w_up", "ffn1_w_down", "w_in", "w_out", "ffn2_w_gate", "ffn2_w_up", "ffn2_w_down"]
    mat_pad = [pad_cols, pad_cols, pad_rows, lambda t: t[0], lambda t: t[0], pad_cols, pad_cols, pad_rows]
    mat_w = [p(t) for p, t in zip(mat_pad, [ffn1_w_gate, ffn1_w_up, ffn1_w_down, w_in, w_out, ffn2_w_gate, ffn2_w_up, ffn2_w_down])]
    mat_m = [p(t) for p, t in zip(mat_pad, [m_ffn1_w_gate, m_ffn1_w_up, m_ffn1_w_down, m_w_in, m_w_out, m_ffn2_w_gate, m_ffn2_w_up, m_ffn2_w_down])]
    mat_v = [p(t) for p, t in zip(mat_pad, [v_ffn1_w_gate, v_ffn1_w_up, v_ffn1_w_down, v_w_in, v_w_out, v_ffn2_w_gate, v_ffn2_w_up, v_ffn2_w_down])]

    wg1, wu1, wd1, win, wout, wg2, wu2, wd2 = _all_gather_call([t.astype(BF) for t in mat_w], "gather_weights")

    h1 = _rmsnorm_call(xs, ffn1_norm, "ffn1_norm")
    g1, u1, a1 = _ffn_up_call(h1, wg1, wu1, "ffn1_up")
    x1, h2 = _down_call(a1, wd1, xs, mix_norm, FFN_RESIDUAL_WEIGHT, "ffn1_down")
    proj = _proj_call(h2, win, "mix_in")
    wmix = proj.shape[1] // 8
    cos, sin = _rope_tables(s, wmix // RET_HEADS)
    consts = _ret_consts(wmix // RET_HEADS)
    o_ret, m_ret, st_ret = _ret_fwd_call(proj, cos, sin, consts, ret_norm_g, "ret_fwd")
    o_hg, m_hg, st_hg = _hgrn_fwd_call(proj, hgrn_lb_logits, hgrn_norm_g, "hgrn_fwd")
    merged = jnp.concatenate([m_ret, m_hg], axis=1)
    x2, h3 = _down_call(merged, wout, x1, ffn2_norm, 1.0, "mix_out")
    g2, u2, a2 = _ffn_up_call(h3, wg2, wu2, "ffn2_up")
    (x3,) = _down_call(a2, wd2, x2, None, FFN_RESIDUAL_WEIGHT, "ffn2_down")
    loss_part, dx3, dx3b, gv_final = _loss_call(x3, target, final_norm[None, :], "loss_head")

    dg2, du2 = _bwd_up_call(dx3b, wd2, g2, u2, FFN_RESIDUAL_WEIGHT, "ffn2_bwd_up")
    dx2, dx2b, gv_n3 = _bwd_down_call([(dg2, wg2), (du2, wu2)], dx3, x2, ffn2_norm, "ffn2_bwd_down")
    gm_g2 = _wgrad_call(h3, dg2, N_DEV, False, 1.0, "ffn2_wgrad_gate")
    gm_u2 = _wgrad_call(h3, du2, N_DEV, False, 1.0, "ffn2_wgrad_up")
    gm_d2 = _wgrad_call(a2, dx3b, N_DEV, True, FFN_RESIDUAL_WEIGHT, "ffn2_wgrad_down")

    dmerged = _nt_call(dx2b, wout, "mix_out_bwd")
    gm_out = _wgrad_call(merged, dx2b, N_DEV, True, 1.0, "mix_out_wgrad")
    drq, drk, drv, drg, gv_ret = _ret_bwd_call(proj, cos, sin, consts, ret_norm_g, o_ret, st_ret,
                                               dmerged[:, :wmix], "ret_bwd")
    dhq, dhf, dhi, dhg, gv_lb, gv_hg = _hgrn_bwd_call(proj, hgrn_lb_logits, hgrn_norm_g, o_hg, st_hg,
                                                       dmerged[:, wmix:], "hgrn_bwd")
    dproj = jnp.concatenate([drq, drk, drv, drg, dhq, dhf, dhi, dhg], axis=1)
    dx1, dx1b, gv_n2 = _bwd_down_call([(dproj, win)], dx2, x1, mix_norm, "mix_in_bwd")
    gm_in = _wgrad_call(h2, dproj, N_DEV, False, 1.0, "mix_in_wgrad")

    dg1, du1 = _bwd_up_call(dx1b, wd1, g1, u1, FFN_RESIDUAL_WEIGHT, "ffn1_bwd_up")
    dx0, _, gv_n1 = _bwd_down_call([(dg1, wg1), (du1, wu1)], dx1, xs, ffn1_norm, "ffn1_bwd_down")
    gm_g1 = _wgrad_call(h1, dg1, N_DEV, False, 1.0, "ffn1_wgrad_gate")
    gm_u1 = _wgrad_call(h1, du1, N_DEV, False, 1.0, "ffn1_wgrad_up")
    gm_d1 = _wgrad_call(a1, dx1b, N_DEV, True, FFN_RESIDUAL_WEIGHT, "ffn1_wgrad_down")

    cx, cy, cc = _position()
    parity = jnp.reshape(cc, (1,)).astype(jnp.int32)
    chip = jnp.reshape(2 * cx + cy, (1,)).astype(jnp.int32)
    mat_g = [gm_g1, gm_u1, gm_d1, gm_in, gm_out, gm_g2, gm_u2, gm_d2]
    from_sibling = _exchange_sibling_call(mat_g, "grads_to_sibling")
    sums = [_pair_sum_call(g, r, parity, "pair_sum_" + nm) for g, r, nm in zip(mat_g, from_sibling, mat_names)]
    from_chips = _exchange_chips_call([pb for _, pb in sums], "grads_to_chips")
    mat_out = {}
    for i, nm in enumerate(mat_names):
        res = _adamw_matrix_call(sums[i][0], from_chips[i], chip, mat_w[i], mat_m[i], mat_v[i], "adamw_" + nm)
        if nm.endswith("gate") or nm.endswith("up"):
            res = [t[:, :f_loc] for t in res]
        elif nm.endswith("down"):
            res = [t[:f_loc, :] for t in res]
        mat_out[nm] = [t[None] for t in res]

    vec_names = ["ffn1_norm", "mix_norm", "ret_norm_g", "hgrn_lb_logits", "hgrn_norm_g", "ffn2_norm", "final_norm"]
    vec_g = [gv_n1, gv_n2, gv_ret, gv_lb, gv_hg, gv_n3, gv_final]
    vec_w = [ffn1_norm, mix_norm, ret_norm_g, hgrn_lb_logits, hgrn_norm_g, ffn2_norm, final_norm[None, :]]
    vec_m = [m_ffn1_norm, m_mix_norm, m_ret_norm_g, m_hgrn_lb_logits, m_hgrn_norm_g, m_ffn2_norm, m_final_norm[None, :]]
    vec_v = [v_ffn1_norm, v_mix_norm, v_ret_norm_g, v_hgrn_lb_logits, v_hgrn_norm_g, v_ffn2_norm, v_final_norm[None, :]]
    cat = lambda ts: jnp.concatenate(ts, axis=1)
    (vec_all,) = _all_gather_call([cat(vec_g)], "gather_vector_grads")
    vres = _adamw_vector_call(vec_all[:, 0, :], cat(vec_w), cat(vec_m), cat(vec_v), "adamw_vectors")
    vec_out = {}
    off = 0
    for nm, t in zip(vec_names, vec_w):
        n = t.shape[1]
        parts = [r[:, off:off + n] for r in vres]
        if nm == "final_norm":
            parts = [p[0] for p in parts]
        vec_out[nm] = parts
        off += n

    loss = lax.psum(loss_part[0, 0], ("x", "y", "c"))
    order = ["ffn1_norm", "ffn1_w_gate", "ffn1_w_up", "ffn1_w_down", "mix_norm", "w_in", "ret_norm_g", "hgrn_lb_logits",
             "hgrn_norm_g", "w_out", "ffn2_norm", "ffn2_w_gate", "ffn2_w_up", "ffn2_w_down", "final_norm"]
    res = {**mat_out, **vec_out}
    outs = [loss, dx0[None]]
    for kind in range(4):
        outs += [res[nm][kind] for nm in order]
    return tuple(outs)
```

```python
import functools

import jax
import jax.numpy as jnp
from jax import lax
from jax.experimental import pallas as pl
from jax.experimental.pallas import tpu as pltpu

BF = jnp.bfloat16
F32 = jnp.float32
MESH = pl.DeviceIdType.MESH
HBM_SPEC = pl.BlockSpec(memory_space=pltpu.HBM)

N_DEV = 8
LANE = 128
EPS = 1e-6
ROPE_BASE = 10000.0
RET_HEADS = 4
HGRN_HEADS = 8
RET_CHUNK = 128
HGRN_BLOCK = 16
FFN_RESIDUAL_WEIGHT = 0.5
ADAM_LR = 0.001
ADAM_B1 = 0.9
ADAM_B2 = 0.999
ADAM_EPS = 1e-08
ADAM_WD = 0.01
ADAM_STEP = 10
VMEM_LIMIT = 56 * 1024 * 1024


def _tile(n, pref, mult=8):
    t = min(pref, n)
    t -= t % mult
    while t >= mult:
        if n % t == 0:
            return t
        t -= mult
    return n


def _params(*sem):
    return pltpu.CompilerParams(dimension_semantics=sem, vmem_limit_bytes=VMEM_LIMIT)


class _Comm:
    def __init__(self, operands, out_shape, n_sems, start, finish, aliases=None):
        self.operands = list(operands)
        self.out_shape = list(out_shape)
        self.n_sems = n_sems
        self.start = start
        self.finish = finish
        self.aliases = dict(aliases or {})


def _launch(body, *, name, grid, in_specs, out_specs, out_shape, sem, args, scratch_shapes=(), comm=None):
    in_specs, out_specs, out_shape = list(in_specs), list(out_specs), list(out_shape)
    scratch_shapes = list(scratch_shapes)
    if comm is None:
        res = pl.pallas_call(body, name=name, grid=grid, in_specs=in_specs, out_specs=out_specs, out_shape=out_shape,
                             scratch_shapes=scratch_shapes, compiler_params=_params(*sem))(*args)
        return list(res), []
    n_in, n_out, n_scr = len(in_specs), len(out_specs), len(scratch_shapes)
    ci, co = len(comm.operands), len(comm.out_shape)

    def carrying(*refs):
        bounds = [0, n_in, n_in + ci, n_in + ci + n_out, n_in + ci + n_out + co, n_in + ci + n_out + co + n_scr]
        ins, cins, outs, couts, scr = [refs[a:b] for a, b in zip(bounds[:-1], bounds[1:])]
        send_sems, recv_sems = refs[bounds[-1]:]
        ids = [pl.program_id(k) for k in range(len(grid))]
        first = functools.reduce(jnp.logical_and, [i == 0 for i in ids])
        last = functools.reduce(jnp.logical_and, [i == g - 1 for i, g in zip(ids, grid)])

        @pl.when(first)
        def _():
            comm.start(cins, couts, send_sems, recv_sems)

        body(*ins, *outs, *scr)

        @pl.when(last)
        def _():
            comm.finish(cins, couts, send_sems, recv_sems)

    res = pl.pallas_call(
        carrying, name=name, grid=grid,
        in_specs=in_specs + [HBM_SPEC] * ci, out_specs=out_specs + [HBM_SPEC] * co,
        out_shape=out_shape + comm.out_shape,
        scratch_shapes=scratch_shapes + [pltpu.SemaphoreType.DMA((comm.n_sems,)), pltpu.SemaphoreType.DMA((comm.n_sems,))],
        input_output_aliases={n_in + a: n_out + b for a, b in comm.aliases.items()},
        compiler_params=_params(*(["arbitrary"] * len(grid))),
    )(*args, *comm.operands)
    return list(res[:n_out]), list(res[n_out:])


def _comm_only_call(comm, name):
    def body(*refs):
        ci, co = len(comm.operands), len(comm.out_shape)
        cins, couts = refs[:ci], refs[ci:ci + co]
        send_sems, recv_sems = refs[ci + co:]
        comm.start(cins, couts, send_sems, recv_sems)
        comm.finish(cins, couts, send_sems, recv_sems)

    return pl.pallas_call(
        body, name=name,
        in_specs=[HBM_SPEC] * len(comm.operands), out_specs=[HBM_SPEC] * len(comm.out_shape),
        out_shape=comm.out_shape,
        scratch_shapes=[pltpu.SemaphoreType.DMA((comm.n_sems,)), pltpu.SemaphoreType.DMA((comm.n_sems,))],
        input_output_aliases=comm.aliases,
    )(*comm.operands)


def _sigmoid(v):
    return 1.0 / (1.0 + jnp.exp(-v))


def _dot(a, b):
    return jnp.dot(a, b, preferred_element_type=F32)


def _dot_nt(a, b):
    return lax.dot_general(a, b, (((1,), (1,)), ((), ())), preferred_element_type=F32)


def _dot_tn(a, b):
    return lax.dot_general(a, b, (((0,), (0,)), ((), ())), preferred_element_type=F32)


def _rmsnorm_call(x, gain, name):
    s, d = x.shape
    tm = _tile(s, 512)

    def body(x_ref, g_ref, o_ref):
        xv = x_ref[...]
        r = lax.rsqrt(jnp.mean(xv * xv, axis=-1, keepdims=True) + EPS)
        o_ref[...] = (xv * r * g_ref[...]).astype(BF)

    return pl.pallas_call(
        body, name=name, grid=(s // tm,),
        in_specs=[pl.BlockSpec((tm, d), lambda i: (i, 0)), pl.BlockSpec((1, d), lambda i: (0, 0))],
        out_specs=pl.BlockSpec((tm, d), lambda i: (i, 0)),
        out_shape=jax.ShapeDtypeStruct((s, d), BF),
        compiler_params=_params("parallel"),
    )(x, gain)


def _ffn_up_call(h, wg, wu, name, comm=None):
    s, d = h.shape
    nj, _, k = wg.shape
    tm = _tile(s, 512)

    def body(h_ref, wg_ref, wu_ref, g_ref, u_ref, a_ref):
        hv = h_ref[...]
        g = _dot(hv, wg_ref[...])
        u = _dot(hv, wu_ref[...])
        g_ref[...] = g
        u_ref[...] = u
        a_ref[...] = (g * _sigmoid(g) * u).astype(BF)

    act = pl.BlockSpec((tm, k), lambda i, j: (i, j))
    wsp = pl.BlockSpec((None, d, k), lambda i, j: (j, 0, 0))
    return _launch(
        body, name=name, grid=(s // tm, nj),
        in_specs=[pl.BlockSpec((tm, d), lambda i, j: (i, 0)), wsp, wsp],
        out_specs=[act, act, act],
        out_shape=[jax.ShapeDtypeStruct((s, nj * k), F32), jax.ShapeDtypeStruct((s, nj * k), F32),
                   jax.ShapeDtypeStruct((s, nj * k), BF)],
        sem=("parallel", "arbitrary"), args=(h, wg, wu), comm=comm)


def _proj_call(h, w, name):
    s, d = h.shape
    nj, _, k = w.shape
    tm = _tile(s, 512)

    def body(h_ref, w_ref, o_ref):
        o_ref[...] = _dot(h_ref[...], w_ref[...])

    return pl.pallas_call(
        body, name=name, grid=(s // tm, nj),
        in_specs=[pl.BlockSpec((tm, d), lambda i, j: (i, 0)), pl.BlockSpec((None, d, k), lambda i, j: (j, 0, 0))],
        out_specs=pl.BlockSpec((tm, k), lambda i, j: (i, j)),
        out_shape=jax.ShapeDtypeStruct((s, nj * k), F32),
        compiler_params=_params("parallel", "arbitrary"),
    )(h, w)


def _down_call(a, w, resid, gain, scale, name, comm=None):
    s = a.shape[0]
    nj, k, d = w.shape
    tm = _tile(s, 512)
    with_norm = gain is not None

    def body(*refs):
        if with_norm:
            a_ref, w_ref, r_ref, g_ref, x_ref, h_ref, acc = refs
        else:
            a_ref, w_ref, r_ref, x_ref, acc = refs
        j = pl.program_id(1)

        @pl.when(j == 0)
        def _():
            acc[...] = jnp.zeros_like(acc)

        acc[...] += _dot(a_ref[...], w_ref[...])

        @pl.when(j == nj - 1)
        def _():
            xn = r_ref[...] + (scale * acc[...])
            x_ref[...] = xn
            if with_norm:
                r = lax.rsqrt(jnp.mean(xn * xn, axis=-1, keepdims=True) + EPS)
                h_ref[...] = (xn * r * g_ref[...]).astype(BF)

    row = pl.BlockSpec((tm, d), lambda i, j: (i, 0))
    in_specs = [pl.BlockSpec((tm, k), lambda i, j: (i, j)), pl.BlockSpec((None, k, d), lambda i, j: (j, 0, 0)), row]
    args = [a, w, resid]
    out_specs = [row]
    out_shape = [jax.ShapeDtypeStruct((s, d), F32)]
    if with_norm:
        in_specs.append(pl.BlockSpec((1, d), lambda i, j: (0, 0)))
        args.append(gain)
        out_specs.append(row)
        out_shape.append(jax.ShapeDtypeStruct((s, d), BF))
    return _launch(
        body, name=name, grid=(s // tm, nj),
        in_specs=in_specs, out_specs=out_specs, out_shape=out_shape,
        scratch_shapes=[pltpu.VMEM((tm, d), F32)],
        sem=("parallel", "arbitrary"), args=args, comm=comm)


def _loss_call(x, target, gain, name):
    s, d = x.shape
    tm = _tile(s, 512)

    def body(x_ref, t_ref, g_ref, loss_ref, dx_ref, dxb_ref, dg_ref):
        i = pl.program_id(0)

        @pl.when(i == 0)
        def _():
            loss_ref[...] = jnp.zeros_like(loss_ref)
            dg_ref[...] = jnp.zeros_like(dg_ref)

        xv = x_ref[...]
        gv = g_ref[...]
        r = lax.rsqrt(jnp.mean(xv * xv, axis=-1, keepdims=True) + EPS)
        xhat = xv * r
        err = xhat * gv - t_ref[...]
        per_tok = jnp.mean(err * err, axis=-1, keepdims=True)
        loss_ref[...] += 0.5 * jnp.sum(per_tok, axis=0, keepdims=True)
        dout = err * (1.0 / d)
        dg_ref[...] += jnp.sum(dout * xhat, axis=0, keepdims=True)
        dxhat = dout * gv
        dx = r * (dxhat - xhat * jnp.mean(dxhat * xhat, axis=-1, keepdims=True))
        dx_ref[...] = dx
        dxb_ref[...] = dx.astype(BF)

    row = pl.BlockSpec((tm, d), lambda i: (i, 0))
    vec = pl.BlockSpec((1, d), lambda i: (0, 0))
    return pl.pallas_call(
        body, name=name, grid=(s // tm,),
        in_specs=[row, row, vec],
        out_specs=[pl.BlockSpec((1, 1), lambda i: (0, 0)), row, row, vec],
        out_shape=[jax.ShapeDtypeStruct((1, 1), F32), jax.ShapeDtypeStruct((s, d), F32),
                   jax.ShapeDtypeStruct((s, d), BF), jax.ShapeDtypeStruct((1, d), F32)],
        compiler_params=_params("arbitrary"),
    )(x, target, gain)


def _bwd_up_call(dy, wd, g, u, scale, name, comm=None):
    s, d = dy.shape
    nj, k, _ = wd.shape
    tm = _tile(s, 512)

    def body(dy_ref, w_ref, g_ref, u_ref, dg_ref, du_ref):
        da = scale * _dot_nt(dy_ref[...], w_ref[...])
        gv = g_ref[...]
        sig = _sigmoid(gv)
        du_ref[...] = (da * gv * sig).astype(BF)
        dg_ref[...] = (da * u_ref[...] * sig * (1.0 + gv * (1.0 - sig))).astype(BF)

    act = pl.BlockSpec((tm, k), lambda i, j: (i, j))
    return _launch(
        body, name=name, grid=(s // tm, nj),
        in_specs=[pl.BlockSpec((tm, d), lambda i, j: (i, 0)), pl.BlockSpec((None, k, d), lambda i, j: (j, 0, 0)), act, act],
        out_specs=[act, act],
        out_shape=[jax.ShapeDtypeStruct((s, nj * k), BF), jax.ShapeDtypeStruct((s, nj * k), BF)],
        sem=("parallel", "arbitrary"), args=(dy, wd, g, u), comm=comm)


def _nt_call(dy, w, name, comm=None):
    s, d = dy.shape
    nj, k, _ = w.shape
    tm = _tile(s, 512)

    def body(dy_ref, w_ref, o_ref):
        o_ref[...] = _dot_nt(dy_ref[...], w_ref[...])

    return _launch(
        body, name=name, grid=(s // tm, nj),
        in_specs=[pl.BlockSpec((tm, d), lambda i, j: (i, 0)), pl.BlockSpec((None, k, d), lambda i, j: (j, 0, 0))],
        out_specs=[pl.BlockSpec((tm, k), lambda i, j: (i, j))],
        out_shape=[jax.ShapeDtypeStruct((s, nj * k), F32)],
        sem=("parallel", "arbitrary"), args=(dy, w), comm=comm)


def _bwd_down_call(pairs, dres, xin, gain, name, comm=None):
    s, d = xin.shape
    nj, _, k = pairs[0][1].shape
    npair = len(pairs)
    tm = _tile(s, 512)
    strip = _tile(tm, 128)

    def body(*refs):
        a_refs = refs[0:2 * npair:2]
        w_refs = refs[1:2 * npair:2]
        dres_ref, x_ref, g_ref, dx_ref, dxb_ref, dg_ref, acc = refs[2 * npair:]
        i = pl.program_id(0)
        j = pl.program_id(1)

        @pl.when(j == 0)
        def _():
            acc[...] = jnp.zeros_like(acc)

        @pl.when((i == 0) & (j == 0))
        def _():
            dg_ref[...] = jnp.zeros_like(dg_ref)

        for a_ref, w_ref in zip(a_refs, w_refs):
            acc[...] += _dot_nt(a_ref[...], w_ref[...])

        @pl.when(j == nj - 1)
        def _():
            for r0 in range(0, tm, strip):
                rows = slice(r0, r0 + strip)
                xv = x_ref[rows, :]
                r = lax.rsqrt(jnp.mean(xv * xv, axis=-1, keepdims=True) + EPS)
                xhat = xv * r
                dh = acc[rows, :]
                dg_ref[...] += jnp.sum(dh * xhat, axis=0, keepdims=True)
                dxhat = dh * g_ref[...]
                dx = dres_ref[rows, :] + r * (dxhat - xhat * jnp.mean(dxhat * xhat, axis=-1, keepdims=True))
                dx_ref[rows, :] = dx
                dxb_ref[rows, :] = dx.astype(BF)

    row = pl.BlockSpec((tm, d), lambda i, j: (i, 0))
    vec = pl.BlockSpec((1, d), lambda i, j: (0, 0))
    in_specs, args = [], []
    for a, w in pairs:
        in_specs += [pl.BlockSpec((tm, k), lambda i, j: (i, j)), pl.BlockSpec((None, d, k), lambda i, j: (j, 0, 0))]
        args += [a, w]
    once = pl.BlockSpec((tm, d), lambda i, j: (i, 0), pipeline_mode=pl.Buffered(1))
    in_specs += [once, once, vec]
    args += [dres, xin, gain]
    return _launch(
        body, name=name, grid=(s // tm, nj),
        in_specs=in_specs, out_specs=[row, row, vec],
        out_shape=[jax.ShapeDtypeStruct((s, d), F32), jax.ShapeDtypeStruct((s, d), BF), jax.ShapeDtypeStruct((1, d), F32)],
        scratch_shapes=[pltpu.VMEM((tm, d), F32)],
        sem=("arbitrary", "arbitrary"), args=args, comm=comm)


def _wgrad_call(a, b, nj, a_blocked, scale, name):
    s = a.shape[0]
    ka = a.shape[1] // nj if a_blocked else a.shape[1]
    kb = b.shape[1] if a_blocked else b.shape[1] // nj
    ts = _tile(s, 1024)
    ns = s // ts

    def body(a_ref, b_ref, o_ref):
        t = pl.program_id(1)

        @pl.when(t == 0)
        def _():
            o_ref[...] = jnp.zeros_like(o_ref)

        o_ref[...] += _dot_tn(a_ref[...], b_ref[...])
        if scale != 1.0:
            @pl.when(t == ns - 1)
            def _():
                o_ref[...] = o_ref[...] * scale

    a_spec = pl.BlockSpec((ts, ka), (lambda j, t: (t, j)) if a_blocked else (lambda j, t: (t, 0)))
    b_spec = pl.BlockSpec((ts, kb), (lambda j, t: (t, 0)) if a_blocked else (lambda j, t: (t, j)))
    return pl.pallas_call(
        body, name=name, grid=(nj, ns),
        in_specs=[a_spec, b_spec],
        out_specs=pl.BlockSpec((None, ka, kb), lambda j, t: (j, 0, 0)),
        out_shape=jax.ShapeDtypeStruct((nj, ka, kb), F32),
        compiler_params=_params("parallel", "arbitrary"),
    )(a, b)


def _rope(v, cos, sin):
    half = v.shape[-1] // 2
    v1, v2 = v[:, :half], v[:, half:]
    return jnp.concatenate([v1 * cos - v2 * sin, v2 * cos + v1 * sin], axis=-1)


def _rope_bwd(dv, cos, sin):
    half = dv.shape[-1] // 2
    d1, d2 = dv[:, :half], dv[:, half:]
    return jnp.concatenate([d1 * cos + d2 * sin, d2 * cos - d1 * sin], axis=-1)


def _ret_consts(hd):
    c = RET_CHUNK
    log_gamma = jnp.log(1.0 - jnp.exp2(-5.0 - jnp.arange(RET_HEADS, dtype=F32)))
    idx = jnp.arange(c, dtype=F32)
    rel = idx[:, None] - idx[None, :]
    mask = rel >= 0
    decay = jnp.where(mask[None], jnp.exp(log_gamma[:, None, None] * jnp.where(mask, rel, 0.0)[None]), 0.0)
    qdec = jnp.exp(log_gamma[:, None] * (idx + 1.0)[None, :])
    kdec = jnp.exp(log_gamma[:, None] * (c - 1.0 - idx)[None, :])
    gchunk = jnp.exp(log_gamma * c)
    bc = lambda t: jnp.broadcast_to(t[:, :, None], (RET_HEADS, t.shape[1], hd))
    return decay, bc(qdec), bc(kdec), bc(gchunk[:, None])


def _rope_tables(s, hd):
    inv = jnp.power(ROPE_BASE, -jnp.arange(0, hd, 2, dtype=F32) / hd)
    ang = jnp.arange(s, dtype=F32)[:, None] * inv[None, :]
    return jnp.cos(ang), jnp.sin(ang)


def _ret_fwd_call(proj, cos, sin, consts, gret, name):
    s = proj.shape[0]
    w = proj.shape[1] // 8
    hd = w // RET_HEADS
    c = RET_CHUNK
    tt = _tile(s, 512, c)
    nc = tt // c
    decay, qdec, kdec, gch = consts
    scale = hd ** -0.5

    def body(q_ref, k_ref, v_ref, gate_ref, cos_ref, sin_ref, dec_ref, qd_ref, kd_ref, gc_ref, gn_ref,
             o_ref, m_ref, st_ref, state):
        @pl.when(pl.program_id(1) == 0)
        def _():
            state[...] = jnp.zeros_like(state)

        dec = dec_ref[...]
        for ci in range(nc):
            rows = slice(ci * c, (ci + 1) * c)
            cs, sn = cos_ref[rows, :], sin_ref[rows, :]
            q = _rope(q_ref[rows, :], cs, sn) * scale
            k = _rope(k_ref[rows, :], cs, sn)
            vb = v_ref[rows, :].astype(BF)
            sc = _dot_nt(q.astype(BF), k.astype(BF)) * dec
            prev = state[...]
            st_ref[ci] = prev
            o = _dot(sc.astype(BF), vb) + _dot((q * qd_ref[...]).astype(BF), prev.astype(BF))
            state[...] = gc_ref[...] * prev + _dot_tn((k * kd_ref[...]).astype(BF), vb)
            o_ref[rows, :] = o
            mu = jnp.mean(o, axis=-1, keepdims=True)
            cen = o - mu
            xhat = cen * lax.rsqrt(jnp.mean(cen * cen, axis=-1, keepdims=True) + EPS)
            gt = gate_ref[rows, :]
            m_ref[rows, :] = (xhat * gn_ref[...] * (gt * _sigmoid(gt))).astype(BF)

    nh = RET_HEADS
    comp = lambda j: pl.BlockSpec((tt, hd), lambda h, t, j=j: (t, j * nh + h))
    tab = pl.BlockSpec((tt, hd // 2), lambda h, t: (t, 0))
    per_head = lambda r: pl.BlockSpec((None, r, hd), lambda h, t: (h, 0, 0))
    return pl.pallas_call(
        body, name=name, grid=(nh, s // tt),
        in_specs=[comp(0), comp(1), comp(2), comp(3), tab, tab,
                  pl.BlockSpec((None, c, c), lambda h, t: (h, 0, 0)), per_head(c), per_head(c), per_head(1),
                  pl.BlockSpec((1, hd), lambda h, t: (0, h))],
        out_specs=[pl.BlockSpec((tt, hd), lambda h, t: (t, h)), pl.BlockSpec((tt, hd), lambda h, t: (t, h)),
                   pl.BlockSpec((None, nc, hd, hd), lambda h, t: (h, t, 0, 0))],
        out_shape=[jax.ShapeDtypeStruct((s, w), F32), jax.ShapeDtypeStruct((s, w), BF),
                   jax.ShapeDtypeStruct((nh, s // c, hd, hd), F32)],
        scratch_shapes=[pltpu.VMEM((hd, hd), F32)],
        compiler_params=_params("parallel", "arbitrary"),
    )(proj, proj, proj, proj, cos, sin, decay, qdec, kdec, gch, gret)


def _ret_bwd_call(proj, cos, sin, consts, gret, o_raw, states, dmerged, name):
    s = proj.shape[0]
    w = proj.shape[1] // 8
    hd = w // RET_HEADS
    c = RET_CHUNK
    tt = _tile(s, 512, c)
    nc = tt // c
    nt = s // tt
    decay, qdec, kdec, gch = consts
    scale = hd ** -0.5

    def body(q_ref, k_ref, v_ref, gate_ref, cos_ref, sin_ref, dec_ref, qd_ref, kd_ref, gc_ref, gn_ref,
             o_ref, st_ref, dm_ref, dq_ref, dk_ref, dv_ref, dgate_ref, dgn_ref, dstate):
        @pl.when(pl.program_id(1) == 0)
        def _():
            dstate[...] = jnp.zeros_like(dstate)
            dgn_ref[...] = jnp.zeros_like(dgn_ref)

        dec = dec_ref[...]
        gn = gn_ref[...]
        for ci in reversed(range(nc)):
            rows = slice(ci * c, (ci + 1) * c)
            cs, sn = cos_ref[rows, :], sin_ref[rows, :]
            q = _rope(q_ref[rows, :], cs, sn) * scale
            k = _rope(k_ref[rows, :], cs, sn)
            qb, kb = q.astype(BF), k.astype(BF)
            vb = v_ref[rows, :].astype(BF)
            sc = _dot_nt(qb, kb) * dec
            o = o_ref[rows, :]
            mu = jnp.mean(o, axis=-1, keepdims=True)
            cen = o - mu
            rstd = lax.rsqrt(jnp.mean(cen * cen, axis=-1, keepdims=True) + EPS)
            xhat = cen * rstd
            gt = gate_ref[rows, :]
            sig = _sigmoid(gt)
            sg = gt * sig
            dm = dm_ref[rows, :]
            dgn_ref[...] += jnp.sum(dm * xhat * sg, axis=0, keepdims=True)
            dgate_ref[rows, :] = (dm * xhat * gn * sig * (1.0 + gt * (1.0 - sig))).astype(BF)
            dxhat = dm * gn * sg
            do = rstd * (dxhat - jnp.mean(dxhat, axis=-1, keepdims=True)
                         - xhat * jnp.mean(dxhat * xhat, axis=-1, keepdims=True))
            dob = do.astype(BF)
            prev = st_ref[ci]
            ds = dstate[...]
            dsb = ds.astype(BF)
            dsc = (_dot_nt(dob, vb) * dec).astype(BF)
            dq = _dot(dsc, kb) + _dot_nt(dob, prev.astype(BF)) * qd_ref[...]
            dk = _dot_tn(dsc, qb) + _dot_nt(vb, dsb) * kd_ref[...]
            dv = _dot_tn(sc.astype(BF), dob) + _dot((k * kd_ref[...]).astype(BF), dsb)
            dstate[...] = gc_ref[...] * ds + _dot_tn((q * qd_ref[...]).astype(BF), dob)
            dq_ref[rows, :] = _rope_bwd(dq * scale, cs, sn).astype(BF)
            dk_ref[rows, :] = _rope_bwd(dk, cs, sn).astype(BF)
            dv_ref[rows, :] = dv.astype(BF)

    nh = RET_HEADS
    rev = lambda t: nt - 1 - t
    comp = lambda j: pl.BlockSpec((tt, hd), lambda h, t, j=j: (rev(t), j * nh + h))
    tab = pl.BlockSpec((tt, hd // 2), lambda h, t: (rev(t), 0))
    per_head = lambda r: pl.BlockSpec((None, r, hd), lambda h, t: (h, 0, 0))
    head_cols = pl.BlockSpec((tt, hd), lambda h, t: (rev(t), h))
    gvec = pl.BlockSpec((1, hd), lambda h, t: (0, h))
    act = jax.ShapeDtypeStruct((s, w), BF)
    return pl.pallas_call(
        body, name=name, grid=(nh, nt),
        in_specs=[comp(0), comp(1), comp(2), comp(3), tab, tab,
                  pl.BlockSpec((None, c, c), lambda h, t: (h, 0, 0)), per_head(c), per_head(c), per_head(1), gvec,
                  head_cols, pl.BlockSpec((None, nc, hd, hd), lambda h, t: (h, rev(t), 0, 0)), head_cols],
        out_specs=[head_cols, head_cols, head_cols, head_cols, gvec],
        out_shape=[act, act, act, act, jax.ShapeDtypeStruct((1, w), F32)],
        scratch_shapes=[pltpu.VMEM((hd, hd), F32)],
        compiler_params=_params("parallel", "arbitrary"),
    )(proj, proj, proj, proj, cos, sin, decay, qdec, kdec, gch, gret, o_raw, states, dmerged)


def _block_tri(n, bs, upper):
    r = jnp.arange(n)[:, None]
    cidx = jnp.arange(n)[None, :]
    same = (r // bs) == (cidx // bs)
    return jnp.where(same & ((cidx >= r) if upper else (cidx <= r)), 1.0, 0.0).astype(F32)


def _dot_exact(a, b):
    return jnp.dot(a, b, preferred_element_type=F32, precision=lax.Precision.HIGHEST)


def _hgrn_gates(z, lbv):
    sz = _sigmoid(z)
    oml = 1.0 - lbv
    f = lbv + oml * sz
    key = oml * (1.0 - sz)
    return sz, f, key


def _hgrn_fwd_call(proj, lb_logits, ghg, name):
    s = proj.shape[0]
    w = proj.shape[1] // 8
    nh = HGRN_HEADS
    hd = w // nh
    bs = HGRN_BLOCK
    tt = _tile(s, 256, bs)
    nb = tt // bs

    def body(q_ref, z_ref, v_ref, gate_ref, lb_ref, gn_ref, tril_ref, o_ref, m_ref, st_ref, state, upd):
        @pl.when(pl.program_id(1) == 0)
        def _():
            state[...] = jnp.zeros_like(state)

        lbv = _sigmoid(lb_ref[...])
        _, f, key = _hgrn_gates(z_ref[...], lbv)
        qr = q_ref[...]
        q = qr * _sigmoid(qr)
        v = v_ref[...]
        g = _dot_exact(tril_ref[...], jnp.log(f))
        blocks = lambda t: t.reshape(nb, bs, hd)
        g3, q3, k3, v3 = blocks(g), blocks(q), blocks(key), blocks(v)
        glast3 = g3[:, bs - 1:bs, :]
        row_id = lax.broadcasted_iota(jnp.int32, (nb, bs, hd), 1)
        o3 = jnp.zeros((nb, bs, hd), F32)
        for j in range(bs):
            wj = jnp.where(row_id >= j, jnp.exp(jnp.minimum(g3 - g3[:, j:j + 1, :], 0.0)), 0.0)
            a = jnp.sum(q3 * k3[:, j:j + 1, :] * wj, axis=-1, keepdims=True)
            o3 = o3 + a * v3[:, j:j + 1, :]
        ktb = (k3 * jnp.exp(glast3 - g3)).reshape(tt, hd).astype(BF)
        vb = v.astype(BF)
        for b in range(nb):
            rows = slice(b * bs, (b + 1) * bs)
            upd[b] = _dot_tn(vb[rows, :], ktb[rows, :])
        egl3 = jnp.exp(glast3)
        st = state[...]
        for b in range(nb):
            st_ref[b] = st
            st = st * egl3[b] + upd[b]
        state[...] = st
        qgb = (q * jnp.exp(g)).astype(BF)
        o_intra = o3.reshape(tt, hd)
        gn = gn_ref[...]
        for b in range(nb):
            rows = slice(b * bs, (b + 1) * bs)
            o = o_intra[rows, :] + _dot_nt(qgb[rows, :], st_ref[b].astype(BF))
            o_ref[rows, :] = o
            gt = gate_ref[rows, :]
            xhat = o * lax.rsqrt(jnp.mean(o * o, axis=-1, keepdims=True) + EPS)
            m_ref[rows, :] = (xhat * gn * (gt * _sigmoid(gt))).astype(BF)

    comp = lambda j: pl.BlockSpec((tt, hd), lambda h, t, j=j: (t, j * nh + h))
    gvec = pl.BlockSpec((1, hd), lambda h, t: (0, h))
    head_cols = pl.BlockSpec((tt, hd), lambda h, t: (t, h))
    return pl.pallas_call(
        body, name=name, grid=(nh, s // tt),
        in_specs=[comp(4), comp(5), comp(6), comp(7), gvec, gvec, pl.BlockSpec((tt, tt), lambda h, t: (0, 0))],
        out_specs=[head_cols, head_cols, pl.BlockSpec((None, nb, hd, hd), lambda h, t: (h, t, 0, 0))],
        out_shape=[jax.ShapeDtypeStruct((s, w), F32), jax.ShapeDtypeStruct((s, w), BF),
                   jax.ShapeDtypeStruct((nh, s // bs, hd, hd), F32)],
        scratch_shapes=[pltpu.VMEM((hd, hd), F32), pltpu.VMEM((nb, hd, hd), F32)],
        compiler_params=_params("parallel", "arbitrary"),
    )(proj, proj, proj, proj, lb_logits, ghg, _block_tri(tt, bs, upper=False))


def _hgrn_bwd_call(proj, lb_logits, ghg, o_raw, states, dmerged, name, comm=None):
    s = proj.shape[0]
    w = proj.shape[1] // 8
    nh = HGRN_HEADS
    hd = w // nh
    bs = HGRN_BLOCK
    tt = _tile(s, 256, bs)
    nb = tt // bs
    nt = s // tt

    def body(q_ref, z_ref, v_ref, gate_ref, lb_ref, gn_ref, tril_ref, triu_ref, o_ref, st_ref, dm_ref,
             dq_ref, dz_ref, dv_ref, dgate_ref, dlb_ref, dgn_ref,
             dstate, ds_all, inc, dq_s, dk_s, dv_s, dgl_s, dk_rows, dv_rows):
        @pl.when(pl.program_id(1) == 0)
        def _():
            dstate[...] = jnp.zeros_like(dstate)
            dlb_ref[...] = jnp.zeros_like(dlb_ref)
            dgn_ref[...] = jnp.zeros_like(dgn_ref)

        lbv = _sigmoid(lb_ref[...])
        oml = 1.0 - lbv
        gn = gn_ref[...]
        sz, f, key = _hgrn_gates(z_ref[...], lbv)
        qr = q_ref[...]
        sq = _sigmoid(qr)
        q = qr * sq
        v = v_ref[...]
        g = _dot_exact(tril_ref[...], jnp.log(f))
        eg = jnp.exp(g)
        blocks = lambda t: t.reshape(nb, bs, hd)
        g3, q3, k3, v3 = blocks(g), blocks(q), blocks(key), blocks(v)
        glast3 = g3[:, bs - 1:bs, :]
        egl3 = jnp.exp(glast3)
        ktail3 = jnp.exp(glast3 - g3)
        o = o_ref[...]
        rstd = lax.rsqrt(jnp.mean(o * o, axis=-1, keepdims=True) + EPS)
        xhat = o * rstd
        gt = gate_ref[...]
        sig = _sigmoid(gt)
        sg = gt * sig
        dm = dm_ref[...]
        dgn_ref[...] += jnp.sum(dm * xhat * sg, axis=0, keepdims=True)
        dgate_ref[...] = (dm * xhat * gn * sig * (1.0 + gt * (1.0 - sig))).astype(BF)
        dxhat = dm * gn * sg
        do = rstd * (dxhat - xhat * jnp.mean(dxhat * xhat, axis=-1, keepdims=True))
        dob = do.astype(BF)
        do3 = blocks(do)
        qgb = (q * eg).astype(BF)
        for b in range(nb):
            rows = slice(b * bs, (b + 1) * bs)
            inc[b] = _dot_tn(dob[rows, :], qgb[rows, :])
        ds = dstate[...]
        for b in reversed(range(nb)):
            ds_all[b] = ds
            ds = ds * egl3[b] + inc[b]
        dstate[...] = ds
        ktb = (k3 * ktail3).reshape(tt, hd).astype(BF)
        vb = v.astype(BF)
        for b in range(nb):
            rows = slice(b * bs, (b + 1) * bs)
            prev = st_ref[b]
            dsb = ds_all[b]
            dsbb = dsb.astype(BF)
            dq_s[rows, :] = _dot(dob[rows, :], prev.astype(BF))
            dk_s[rows, :] = _dot(vb[rows, :], dsbb)
            dv_s[rows, :] = _dot_nt(ktb[rows, :], dsbb)
            dgl_s[b] = jnp.sum(prev * dsb, axis=0, keepdims=True)
        dq3 = blocks(dq_s[...] * eg)
        dk3 = blocks(dk_s[...]) * ktail3
        dg_last3 = jnp.sum(k3 * dk3, axis=1, keepdims=True) + egl3 * dgl_s[...]
        row_id = lax.broadcasted_iota(jnp.int32, (nb, bs, hd), 1)
        for j in range(bs):
            wj = jnp.where(row_id >= j, jnp.exp(jnp.minimum(g3 - g3[:, j:j + 1, :], 0.0)), 0.0)
            kj = k3[:, j:j + 1, :]
            a = jnp.sum(q3 * kj * wj, axis=-1, keepdims=True)
            da = jnp.sum(do3 * v3[:, j:j + 1, :], axis=-1, keepdims=True)
            dv_rows[:, j:j + 1, :] = jnp.sum(a * do3, axis=1, keepdims=True)
            dq3 = dq3 + da * kj * wj
            dk_rows[:, j:j + 1, :] = jnp.sum(da * q3 * wj, axis=1, keepdims=True)
        dk3 = dk3 + dk_rows[...]
        dv = dv_s[...] + dv_rows[...].reshape(tt, hd)
        dg3 = q3 * dq3 - k3 * dk3 + jnp.where(row_id == bs - 1, dg_last3, 0.0)
        dlf = _dot_exact(triu_ref[...], dg3.reshape(tt, hd))
        dk = dk3.reshape(tt, hd)
        dfk = dlf / f - dk
        dlb_ref[...] += jnp.sum(dfk * (1.0 - sz), axis=0, keepdims=True) * (lbv * oml)
        dz_ref[...] = (dfk * oml * sz * (1.0 - sz)).astype(BF)
        dq_ref[...] = (dq3.reshape(tt, hd) * sq * (1.0 + qr * (1.0 - sq))).astype(BF)
        dv_ref[...] = dv.astype(BF)

    rev = lambda t: nt - 1 - t
    comp = lambda j: pl.BlockSpec((tt, hd), lambda h, t, j=j: (rev(t), j * nh + h))
    gvec = pl.BlockSpec((1, hd), lambda h, t: (0, h))
    head_cols = pl.BlockSpec((tt, hd), lambda h, t: (rev(t), h))
    tri = pl.BlockSpec((tt, tt), lambda h, t: (0, 0))
    act = jax.ShapeDtypeStruct((s, w), BF)
    vec = jax.ShapeDtypeStruct((1, w), F32)
    tile_f32 = pltpu.VMEM((tt, hd), F32)
    return _launch(
        body, name=name, grid=(nh, nt),
        in_specs=[comp(4), comp(5), comp(6), comp(7), gvec, gvec, tri, tri, head_cols,
                  pl.BlockSpec((None, nb, hd, hd), lambda h, t: (h, rev(t), 0, 0)), head_cols],
        out_specs=[head_cols, head_cols, head_cols, head_cols, gvec, gvec],
        out_shape=[act, act, act, act, vec, vec],
        scratch_shapes=[pltpu.VMEM((hd, hd), F32), pltpu.VMEM((nb, hd, hd), F32), pltpu.VMEM((nb, hd, hd), F32),
                        tile_f32, tile_f32, tile_f32, pltpu.VMEM((nb, 1, hd), F32),
                        pltpu.VMEM((nb, bs, hd), F32), pltpu.VMEM((nb, bs, hd), F32)],
        sem=("parallel", "arbitrary"),
        args=(proj, proj, proj, proj, lb_logits, ghg, _block_tri(tt, bs, upper=False), _block_tri(tt, bs, upper=True),
              o_raw, states, dmerged), comm=comm)


def _position():
    return lax.axis_index("x"), lax.axis_index("y"), lax.axis_index("c")


def _all_gather_call(shards, name):
    n = len(shards)

    def body(*refs):
        ins, outs = refs[:n], refs[n:2 * n]
        send_sems, recv_sems, local_sems = refs[2 * n:]
        x, y, c = _position()
        me, sibling = (x, y, c), (x, y, 1 - c)
        chips = [(1 - x, y), (x, 1 - y), (1 - x, 1 - y)]

        def slot(a, p):
            return outs[a].at[4 * p[0] + 2 * p[1] + p[2]]

        def copy(a, k, block, to, src=None):
            return pltpu.make_async_remote_copy(
                src_ref=slot(a, block) if src is None else src, dst_ref=slot(a, block),
                send_sem=send_sems.at[a * 7 + k], recv_sem=recv_sems.at[a * 7 + k],
                device_id=to, device_id_type=MESH)

        mine = [pltpu.make_async_copy(ins[a], slot(a, me), local_sems.at[a]) for a in range(n)]
        for cp in mine:
            cp.start()
        first = []
        for a in range(n):
            first.append(copy(a, 0, me, sibling, src=ins[a]))
            first += [copy(a, 1 + j, me, (*chip, c), src=ins[a]) for j, chip in enumerate(chips)]
        for cp in first:
            cp.start()
        passed = []
        for j, chip in enumerate(chips):
            for a in range(n):
                copy(a, 1 + j, (*chip, c), me).wait_recv()
                fwd = copy(a, 4 + j, (*chip, c), sibling)
                fwd.start()
                passed.append(fwd)
        for a in range(n):
            copy(a, 0, sibling, me).wait_recv()
            for j, chip in enumerate(chips):
                copy(a, 4 + j, (*chip, 1 - c), me).wait_recv()
        for cp in first + passed:
            cp.wait_send()
        for cp in mine:
            cp.wait()

    return pl.pallas_call(
        body, name=name,
        in_specs=[HBM_SPEC] * n, out_specs=[HBM_SPEC] * n,
        out_shape=[jax.ShapeDtypeStruct((N_DEV,) + t.shape, t.dtype) for t in shards],
        scratch_shapes=[pltpu.SemaphoreType.DMA((7 * n,)), pltpu.SemaphoreType.DMA((7 * n,)),
                        pltpu.SemaphoreType.DMA((n,))],
    )(*shards)


def _slot(ref, p):
    return ref.at[4 * p[0] + 2 * p[1] + p[2]]


def _gather_round1(shards):
    n = len(shards)

    def plan(ins, outs, send_sems, recv_sems):
        x, y, c = _position()
        me = (x, y, c)
        peers = [(x, y, 1 - c), (1 - x, y, c), (x, 1 - y, c), (1 - x, 1 - y, c)]
        sends, recvs, local = [], [], []
        for a in range(n):
            local.append(pltpu.make_async_copy(ins[a], _slot(outs[a], me), send_sems.at[4 * n + a]))
            for k, peer in enumerate(peers):
                sems = dict(send_sem=send_sems.at[4 * a + k], recv_sem=recv_sems.at[4 * a + k],
                            device_id=peer, device_id_type=MESH)
                sends.append(pltpu.make_async_remote_copy(src_ref=ins[a], dst_ref=_slot(outs[a], me), **sems))
                recvs.append(pltpu.make_async_remote_copy(src_ref=ins[a], dst_ref=_slot(outs[a], peer), **sems))
        return sends, recvs, local

    def start(*refs):
        sends, _, local = plan(*refs)
        for cp in local + sends:
            cp.start()

    def finish(*refs):
        sends, recvs, local = plan(*refs)
        for cp in recvs:
            cp.wait_recv()
        for cp in sends:
            cp.wait_send()
        for cp in local:
            cp.wait()

    return _Comm(shards, [jax.ShapeDtypeStruct((N_DEV,) + t.shape, t.dtype) for t in shards], 5 * n, start, finish)


def _gather_round2(gathered):
    n = len(gathered)

    def plan(ins, outs, send_sems, recv_sems):
        x, y, c = _position()
        chips = [(1 - x, y), (x, 1 - y), (1 - x, 1 - y)]
        sends, recvs = [], []
        for a in range(n):
            for k, chip in enumerate(chips):
                sems = dict(send_sem=send_sems.at[3 * a + k], recv_sem=recv_sems.at[3 * a + k],
                            device_id=(x, y, 1 - c), device_id_type=MESH)
                sends.append(pltpu.make_async_remote_copy(
                    src_ref=_slot(ins[a], (*chip, c)), dst_ref=_slot(outs[a], (*chip, c)), **sems))
                recvs.append(pltpu.make_async_remote_copy(
                    src_ref=_slot(ins[a], (*chip, c)), dst_ref=_slot(outs[a], (*chip, 1 - c)), **sems))
        return sends, recvs

    def start(*refs):
        for cp in plan(*refs)[0]:
            cp.start()

    def finish(*refs):
        sends, recvs = plan(*refs)
        for cp in recvs:
            cp.wait_recv()
        for cp in sends:
            cp.wait_send()

    return _Comm(gathered, [jax.ShapeDtypeStruct(t.shape, t.dtype) for t in gathered], 3 * n, start, finish,
                 aliases={a: a for a in range(n)})


def _sibling_exchange(grads):
    n = len(grads)

    def plan(ins, outs, send_sems, recv_sems):
        x, y, c = _position()
        return [pltpu.make_async_remote_copy(
            src_ref=ins[a].at[2 * q + (1 - c)], dst_ref=outs[a].at[q],
            send_sem=send_sems.at[a * 4 + q], recv_sem=recv_sems.at[a * 4 + q],
            device_id=(x, y, 1 - c), device_id_type=MESH) for a in range(n) for q in range(4)]

    def start(*refs):
        for cp in plan(*refs):
            cp.start()

    def finish(*refs):
        for cp in plan(*refs):
            cp.wait()

    return _Comm(grads, [jax.ShapeDtypeStruct((4,) + t.shape[1:], t.dtype) for t in grads], 4 * n, start, finish)


def _chip_exchange(partials):
    n = len(partials)

    def plan(ins, outs, send_sems, recv_sems):
        x, y, c = _position()
        chips = [(1 - x, y), (x, 1 - y), (1 - x, 1 - y)]
        return [pltpu.make_async_remote_copy(
            src_ref=ins[a].at[2 * chip[0] + chip[1]], dst_ref=outs[a].at[k],
            send_sem=send_sems.at[a * 3 + k], recv_sem=recv_sems.at[a * 3 + k],
            device_id=(*chip, c), device_id_type=MESH) for a in range(n) for k, chip in enumerate(chips)]

    def start(*refs):
        for cp in plan(*refs):
            cp.start()

    def finish(*refs):
        for cp in plan(*refs):
            cp.wait()

    return _Comm(partials, [jax.ShapeDtypeStruct((3,) + t.shape[1:], t.dtype) for t in partials], 3 * n, start, finish)


def _pair_sum_call(grad, recv, parity, name):
    _, r, ccols = grad.shape
    tr = _tile(r, 256)

    def body(par_ref, g_ref, r_ref, p_ref, pb_ref):
        del par_ref
        p = g_ref[...] + r_ref[...]
        p_ref[...] = p
        pb_ref[...] = p.astype(BF)

    blk = lambda fn: pl.BlockSpec((None, tr, ccols), fn)
    return pl.pallas_call(
        body, name=name,
        grid_spec=pltpu.PrefetchScalarGridSpec(
            num_scalar_prefetch=1, grid=(4, r // tr),
            in_specs=[blk(lambda q, i, par: (2 * q + par[0], i, 0)), blk(lambda q, i, par: (q, i, 0))],
            out_specs=[blk(lambda q, i, par: (q, i, 0)), blk(lambda q, i, par: (q, i, 0))]),
        out_shape=[jax.ShapeDtypeStruct((4, r, ccols), F32), jax.ShapeDtypeStruct((4, r, ccols), BF)],
        compiler_params=_params("parallel", "parallel"),
    )(parity, grad, recv)


def _adamw_math(w, g, m, v):
    m = ADAM_B1 * m + (1.0 - ADAM_B1) * g
    v = ADAM_B2 * v + (1.0 - ADAM_B2) * (g * g)
    m_hat = m / (1.0 - ADAM_B1 ** ADAM_STEP)
    v_hat = v / (1.0 - ADAM_B2 ** ADAM_STEP)
    delta = -ADAM_LR * (m_hat / (jnp.sqrt(v_hat) + ADAM_EPS) + ADAM_WD * w)
    return delta, m, v


def _adamw_matrix_call(partial, recv, chip, w, m, v, name):
    r, ccols = w.shape
    tr = _tile(r, 256)

    def body(chip_ref, p_ref, r_ref, w_ref, m_ref, v_ref, g_out, d_out, m_out, v_out):
        del chip_ref
        g = p_ref[...] + r_ref[0].astype(F32) + r_ref[1].astype(F32) + r_ref[2].astype(F32)
        delta, mn, vn = _adamw_math(w_ref[...], g, m_ref[...], v_ref[...])
        g_out[...] = g
        d_out[...] = delta
        m_out[...] = mn
        v_out[...] = vn

    mat = pl.BlockSpec((tr, ccols), lambda i, ch: (i, 0))
    shp = jax.ShapeDtypeStruct((r, ccols), F32)
    return pl.pallas_call(
        body, name=name,
        grid_spec=pltpu.PrefetchScalarGridSpec(
            num_scalar_prefetch=1, grid=(r // tr,),
            in_specs=[pl.BlockSpec((None, tr, ccols), lambda i, ch: (ch[0], i, 0)),
                      pl.BlockSpec((3, tr, ccols), lambda i, ch: (0, i, 0)), mat, mat, mat],
            out_specs=[mat, mat, mat, mat]),
        out_shape=[shp, shp, shp, shp],
        compiler_params=_params("parallel"),
    )(chip, partial, recv, w, m, v)


def _adamw_vector_call(gathered, w, m, v, name):
    n = w.shape[1]

    def body(p_ref, w_ref, m_ref, v_ref, g_out, d_out, m_out, v_out):
        g = p_ref[0:1, :]
        for k in range(1, N_DEV):
            g = g + p_ref[k:k + 1, :]
        delta, mn, vn = _adamw_math(w_ref[...], g, m_ref[...], v_ref[...])
        g_out[...] = g
        d_out[...] = delta
        m_out[...] = mn
        v_out[...] = vn

    shp = jax.ShapeDtypeStruct((1, n), F32)
    return pl.pallas_call(body, name=name, out_shape=[shp, shp, shp, shp])(gathered, w, m, v)


def _round_up(n, mult):
    return (n + mult - 1) // mult * mult


def kernel(x, ffn1_norm, ffn1_w_gate, ffn1_w_up, ffn1_w_down, mix_norm, w_in, ret_norm_g, hgrn_lb_logits, hgrn_norm_g, w_out, ffn2_norm, ffn2_w_gate, ffn2_w_up, ffn2_w_down, final_norm, loss_target, m_ffn1_norm, m_ffn1_w_gate, m_ffn1_w_up, m_ffn1_w_down, m_mix_norm, m_w_in, m_ret_norm_g, m_hgrn_lb_logits, m_hgrn_norm_g, m_w_out, m_ffn2_norm, m_ffn2_w_gate, m_ffn2_w_up, m_ffn2_w_down, m_final_norm, v_ffn1_norm, v_ffn1_w_gate, v_ffn1_w_up, v_ffn1_w_down, v_mix_norm, v_w_in, v_ret_norm_g, v_hgrn_lb_logits, v_hgrn_norm_g, v_w_out, v_ffn2_norm, v_ffn2_w_gate, v_ffn2_w_up, v_ffn2_w_down, v_final_norm):
    xs = x[0]
    target = loss_target[0]
    s, d = xs.shape
    f_loc = ffn1_w_gate.shape[2]
    fp = _round_up(f_loc, LANE)
    pad_cols = lambda t: jnp.pad(t[0], ((0, 0), (0, fp - f_loc)))
    pad_rows = lambda t: jnp.pad(t[0], ((0, fp - f_loc), (0, 0)))

    mat_names = ["ffn1_w_gate", "ffn1_w_up", "ffn1_w_down", "w_in", "w_out", "ffn2_w_gate", "ffn2_w_up", "ffn2_w_down"]
    mat_pad = [pad_cols, pad_cols, pad_rows, lambda t: t[0], lambda t: t[0], pad_cols, pad_cols, pad_rows]
    mat_w = [p(t) for p, t in zip(mat_pad, [ffn1_w_gate, ffn1_w_up, ffn1_w_down, w_in, w_out, ffn2_w_gate, ffn2_w_up, ffn2_w_down])]
    mat_m = [p(t) for p, t in zip(mat_pad, [m_ffn1_w_gate, m_ffn1_w_up, m_ffn1_w_down, m_w_in, m_w_out, m_ffn2_w_gate, m_ffn2_w_up, m_ffn2_w_down])]
    mat_v = [p(t) for p, t in zip(mat_pad, [v_ffn1_w_gate, v_ffn1_w_up, v_ffn1_w_down, v_w_in, v_w_out, v_ffn2_w_gate, v_ffn2_w_up, v_ffn2_w_down])]

    cx, cy, cc = _position()
    parity = jnp.reshape(cc, (1,)).astype(jnp.int32)
    chip = jnp.reshape(2 * cx + cy, (1,)).astype(jnp.int32)
    mat_index = {nm: i for i, nm in enumerate(mat_names)}
    mat_out = {}

    def pair_sums(names, grads, from_sibling):
        return [_pair_sum_call(g, r, parity, "pair_sum_" + nm) for nm, g, r in zip(names, grads, from_sibling)]

    def update(names, sums, from_chips):
        for nm, (p, _), r in zip(names, sums, from_chips):
            i = mat_index[nm]
            res = _adamw_matrix_call(p, r, chip, mat_w[i], mat_m[i], mat_v[i], "adamw_" + nm)
            if nm.endswith("gate") or nm.endswith("up"):
                res = [t[:, :f_loc] for t in res]
            elif nm.endswith("down"):
                res = [t[:f_loc, :] for t in res]
            mat_out[nm] = [t[None] for t in res]

    shards = [t.astype(BF) for t in mat_w]
    wg1, wu1, wd1 = _all_gather_call(shards[:3], "gather_ffn1")

    h1 = _rmsnorm_call(xs, ffn1_norm, "ffn1_norm")
    (g1, u1, a1), landed = _ffn_up_call(h1, wg1, wu1, "ffn1_up", comm=_gather_round1(shards[3:]))
    (x1, h2), (win, wout, wg2, wu2, wd2) = _down_call(a1, wd1, xs, mix_norm, FFN_RESIDUAL_WEIGHT, "ffn1_down",
                                                      comm=_gather_round2(landed))
    proj = _proj_call(h2, win, "mix_in")
    wmix = proj.shape[1] // 8
    cos, sin = _rope_tables(s, wmix // RET_HEADS)
    consts = _ret_consts(wmix // RET_HEADS)
    o_ret, m_ret, st_ret = _ret_fwd_call(proj, cos, sin, consts, ret_norm_g, "ret_fwd")
    o_hg, m_hg, st_hg = _hgrn_fwd_call(proj, hgrn_lb_logits, hgrn_norm_g, "hgrn_fwd")
    merged = jnp.concatenate([m_ret, m_hg], axis=1)
    (x2, h3), _ = _down_call(merged, wout, x1, ffn2_norm, 1.0, "mix_out")
    (g2, u2, a2), _ = _ffn_up_call(h3, wg2, wu2, "ffn2_up")
    (x3,), _ = _down_call(a2, wd2, x2, None, FFN_RESIDUAL_WEIGHT, "ffn2_down")
    loss_part, dx3, dx3b, gv_final = _loss_call(x3, target, final_norm[None, :], "loss_head")

    (dg2, du2), _ = _bwd_up_call(dx3b, wd2, g2, u2, FFN_RESIDUAL_WEIGHT, "ffn2_bwd_up")
    (dx2, dx2b, gv_n3), _ = _bwd_down_call([(dg2, wg2), (du2, wu2)], dx3, x2, ffn2_norm, "ffn2_bwd_down")
    names_a = ["ffn2_w_gate", "ffn2_w_up", "ffn2_w_down"]
    grads_a = [_wgrad_call(h3, dg2, N_DEV, False, 1.0, "ffn2_wgrad_gate"),
               _wgrad_call(h3, du2, N_DEV, False, 1.0, "ffn2_wgrad_up"),
               _wgrad_call(a2, dx3b, N_DEV, True, FFN_RESIDUAL_WEIGHT, "ffn2_wgrad_down")]

    (dmerged,), sib_a = _nt_call(dx2b, wout, "mix_out_bwd", comm=_sibling_exchange(grads_a))
    gm_out = _wgrad_call(merged, dx2b, N_DEV, True, 1.0, "mix_out_wgrad")
    sums_a = pair_sums(names_a, grads_a, sib_a)
    drq, drk, drv, drg, gv_ret = _ret_bwd_call(proj, cos, sin, consts, ret_norm_g, o_ret, st_ret,
                                               dmerged[:, :wmix], "ret_bwd")
    (dhq, dhf, dhi, dhg, gv_lb, gv_hg), chips_a = _hgrn_bwd_call(
        proj, hgrn_lb_logits, hgrn_norm_g, o_hg, st_hg, dmerged[:, wmix:], "hgrn_bwd",
        comm=_chip_exchange([pb for _, pb in sums_a]))
    update(names_a, sums_a, chips_a)
    dproj = jnp.concatenate([drq, drk, drv, drg, dhq, dhf, dhi, dhg], axis=1)
    (dx1, dx1b, gv_n2), _ = _bwd_down_call([(dproj, win)], dx2, x1, mix_norm, "mix_in_bwd")
    names_b = ["w_out", "w_in", "ffn1_w_down"]
    grads_b = [gm_out, _wgrad_call(h2, dproj, N_DEV, False, 1.0, "mix_in_wgrad"),
               _wgrad_call(a1, dx1b, N_DEV, True, FFN_RESIDUAL_WEIGHT, "ffn1_wgrad_down")]

    (dg1, du1), sib_b = _bwd_up_call(dx1b, wd1, g1, u1, FFN_RESIDUAL_WEIGHT, "ffn1_bwd_up",
                                     comm=_sibling_exchange(grads_b))
    sums_b = pair_sums(names_b, grads_b, sib_b)
    (dx0, _, gv_n1), chips_b = _bwd_down_call([(dg1, wg1), (du1, wu1)], dx1, xs, ffn1_norm, "ffn1_bwd_down",
                                              comm=_chip_exchange([pb for _, pb in sums_b]))
    update(names_b, sums_b, chips_b)
    names_c = ["ffn1_w_gate", "ffn1_w_up"]
    grads_c = [_wgrad_call(h1, dg1, N_DEV, False, 1.0, "ffn1_wgrad_gate"),
               _wgrad_call(h1, du1, N_DEV, False, 1.0, "ffn1_wgrad_up")]
    sums_c = pair_sums(names_c, grads_c, _comm_only_call(_sibling_exchange(grads_c), "ffn1_grads_to_sibling"))
    update(names_c, sums_c, _comm_only_call(_chip_exchange([pb for _, pb in sums_c]), "ffn1_grads_to_chips"))

    vec_names = ["ffn1_norm", "mix_norm", "ret_norm_g", "hgrn_lb_logits", "hgrn_norm_g", "ffn2_norm", "final_norm"]
    vec_g = [gv_n1, gv_n2, gv_ret, gv_lb, gv_hg, gv_n3, gv_final]
    vec_w = [ffn1_norm, mix_norm, ret_norm_g, hgrn_lb_logits, hgrn_norm_g, ffn2_norm, final_norm[None, :]]
    vec_m = [m_ffn1_norm, m_mix_norm, m_ret_norm_g, m_hgrn_lb_logits, m_hgrn_norm_g, m_ffn2_norm, m_final_norm[None, :]]
    vec_v = [v_ffn1_norm, v_mix_norm, v_ret_norm_g, v_hgrn_lb_logits, v_hgrn_norm_g, v_ffn2_norm, v_final_norm[None, :]]
    cat = lambda ts: jnp.concatenate(ts, axis=1)
    (vec_all,) = _all_gather_call([cat(vec_g)], "gather_vector_grads")
    vres = _adamw_vector_call(vec_all[:, 0, :], cat(vec_w), cat(vec_m), cat(vec_v), "adamw_vectors")
    vec_out = {}
    off = 0
    for nm, t in zip(vec_names, vec_w):
        n = t.shape[1]
        parts = [r[:, off:off + n] for r in vres]
        if nm == "final_norm":
            parts = [p[0] for p in parts]
        vec_out[nm] = parts
        off += n

    loss = lax.psum(loss_part[0, 0], ("x", "y", "c"))
    order = ["ffn1_norm", "ffn1_w_gate", "ffn1_w_up", "ffn1_w_down", "mix_norm", "w_in", "ret_norm_g", "hgrn_lb_logits",
             "hgrn_norm_g", "w_out", "ffn2_norm", "ffn2_w_gate", "ffn2_w_up", "ffn2_w_down", "final_norm"]
    res = {**mat_out, **vec_out}
    outs = [loss, dx0[None]]
    for kind in range(4):
        outs += [res[nm][kind] for nm in order]
    return tuple(outs)
```

```python
import functools

import jax
import jax.numpy as jnp
from jax import lax
from jax.experimental import pallas as pl
from jax.experimental.pallas import tpu as pltpu

BF = jnp.bfloat16
F32 = jnp.float32
MESH = pl.DeviceIdType.MESH
HBM_SPEC = pl.BlockSpec(memory_space=pltpu.HBM)

N_DEV = 8
LANE = 128
EPS = 1e-6
ROPE_BASE = 10000.0
RET_HEADS = 4
HGRN_HEADS = 8
RET_CHUNK = 128
HGRN_BLOCK = 16
FFN_RESIDUAL_WEIGHT = 0.5
ADAM_LR = 0.001
ADAM_B1 = 0.9
ADAM_B2 = 0.999
ADAM_EPS = 1e-08
ADAM_WD = 0.01
ADAM_STEP = 10
VMEM_LIMIT = 56 * 1024 * 1024


def _tile(n, pref, mult=8):
    t = min(pref, n)
    t -= t % mult
    while t >= mult:
        if n % t == 0:
            return t
        t -= mult
    return n


def _params(*sem):
    return pltpu.CompilerParams(dimension_semantics=sem, vmem_limit_bytes=VMEM_LIMIT)


class _Comm:
    def __init__(self, operands, out_shape, n_sems, start, finish, aliases=None):
        self.operands = list(operands)
        self.out_shape = list(out_shape)
        self.n_sems = n_sems
        self.start = start
        self.finish = finish
        self.aliases = dict(aliases or {})


def _launch(body, *, name, grid, in_specs, out_specs, out_shape, sem, args, scratch_shapes=(), comm=None):
    in_specs, out_specs, out_shape = list(in_specs), list(out_specs), list(out_shape)
    scratch_shapes = list(scratch_shapes)
    if comm is None:
        res = pl.pallas_call(body, name=name, grid=grid, in_specs=in_specs, out_specs=out_specs, out_shape=out_shape,
                             scratch_shapes=scratch_shapes, compiler_params=_params(*sem))(*args)
        return list(res), []
    n_in, n_out, n_scr = len(in_specs), len(out_specs), len(scratch_shapes)
    ci, co = len(comm.operands), len(comm.out_shape)

    def carrying(*refs):
        bounds = [0, n_in, n_in + ci, n_in + ci + n_out, n_in + ci + n_out + co, n_in + ci + n_out + co + n_scr]
        ins, cins, outs, couts, scr = [refs[a:b] for a, b in zip(bounds[:-1], bounds[1:])]
        send_sems, recv_sems = refs[bounds[-1]:]
        ids = [pl.program_id(k) for k in range(len(grid))]
        first = functools.reduce(jnp.logical_and, [i == 0 for i in ids])
        last = functools.reduce(jnp.logical_and, [i == g - 1 for i, g in zip(ids, grid)])

        @pl.when(first)
        def _():
            comm.start(cins, couts, send_sems, recv_sems)

        body(*ins, *outs, *scr)

        @pl.when(last)
        def _():
            comm.finish(cins, couts, send_sems, recv_sems)

    res = pl.pallas_call(
        carrying, name=name, grid=grid,
        in_specs=in_specs + [HBM_SPEC] * ci, out_specs=out_specs + [HBM_SPEC] * co,
        out_shape=out_shape + comm.out_shape,
        scratch_shapes=scratch_shapes + [pltpu.SemaphoreType.DMA((comm.n_sems,)), pltpu.SemaphoreType.DMA((comm.n_sems,))],
        input_output_aliases={n_in + a: n_out + b for a, b in comm.aliases.items()},
        compiler_params=_params(*(["arbitrary"] * len(grid))),
    )(*args, *comm.operands)
    return list(res[:n_out]), list(res[n_out:])


def _comm_only_call(comm, name):
    def body(*refs):
        ci, co = len(comm.operands), len(comm.out_shape)
        cins, couts = refs[:ci], refs[ci:ci + co]
        send_sems, recv_sems = refs[ci + co:]
        comm.start(cins, couts, send_sems, recv_sems)
        comm.finish(cins, couts, send_sems, recv_sems)

    return pl.pallas_call(
        body, name=name,
        in_specs=[HBM_SPEC] * len(comm.operands), out_specs=[HBM_SPEC] * len(comm.out_shape),
        out_shape=comm.out_shape,
        scratch_shapes=[pltpu.SemaphoreType.DMA((comm.n_sems,)), pltpu.SemaphoreType.DMA((comm.n_sems,))],
        input_output_aliases=comm.aliases,
    )(*comm.operands)


def _sigmoid(v):
    return 1.0 / (1.0 + jnp.exp(-v))


def _dot(a, b):
    return jnp.dot(a, b, preferred_element_type=F32)


def _dot_nt(a, b):
    return lax.dot_general(a, b, (((1,), (1,)), ((), ())), preferred_element_type=F32)


def _dot_tn(a, b):
    return lax.dot_general(a, b, (((0,), (0,)), ((), ())), preferred_element_type=F32)


def _rmsnorm_call(x, gain, name):
    s, d = x.shape
    tm = _tile(s, 512)

    def body(x_ref, g_ref, o_ref):
        xv = x_ref[...]
        r = lax.rsqrt(jnp.mean(xv * xv, axis=-1, keepdims=True) + EPS)
        o_ref[...] = (xv * r * g_ref[...]).astype(BF)

    return pl.pallas_call(
        body, name=name, grid=(s // tm,),
        in_specs=[pl.BlockSpec((tm, d), lambda i: (i, 0)), pl.BlockSpec((1, d), lambda i: (0, 0))],
        out_specs=pl.BlockSpec((tm, d), lambda i: (i, 0)),
        out_shape=jax.ShapeDtypeStruct((s, d), BF),
        compiler_params=_params("parallel"),
    )(x, gain)


def _ffn_up_call(h, wg, wu, name, comm=None):
    s, d = h.shape
    nj, _, k = wg.shape
    tm = _tile(s, 512)

    def body(h_ref, wg_ref, wu_ref, g_ref, u_ref, a_ref):
        hv = h_ref[...]
        g = _dot(hv, wg_ref[...])
        u = _dot(hv, wu_ref[...])
        g_ref[...] = g
        u_ref[...] = u
        a_ref[...] = (g * _sigmoid(g) * u).astype(BF)

    act = pl.BlockSpec((tm, k), lambda i, j: (i, j))
    wsp = pl.BlockSpec((None, d, k), lambda i, j: (j, 0, 0))
    return _launch(
        body, name=name, grid=(s // tm, nj),
        in_specs=[pl.BlockSpec((tm, d), lambda i, j: (i, 0)), wsp, wsp],
        out_specs=[act, act, act],
        out_shape=[jax.ShapeDtypeStruct((s, nj * k), F32), jax.ShapeDtypeStruct((s, nj * k), F32),
                   jax.ShapeDtypeStruct((s, nj * k), BF)],
        sem=("parallel", "arbitrary"), args=(h, wg, wu), comm=comm)


def _proj_call(h, w, name):
    s, d = h.shape
    nj, _, k = w.shape
    tm = _tile(s, 512)

    def body(h_ref, w_ref, o_ref):
        o_ref[...] = _dot(h_ref[...], w_ref[...])

    return pl.pallas_call(
        body, name=name, grid=(s // tm, nj),
        in_specs=[pl.BlockSpec((tm, d), lambda i, j: (i, 0)), pl.BlockSpec((None, d, k), lambda i, j: (j, 0, 0))],
        out_specs=pl.BlockSpec((tm, k), lambda i, j: (i, j)),
        out_shape=jax.ShapeDtypeStruct((s, nj * k), F32),
        compiler_params=_params("parallel", "arbitrary"),
    )(h, w)


def _down_call(a, w, resid, gain, scale, name, comm=None):
    s = a.shape[0]
    nj, k, d = w.shape
    tm = _tile(s, 512)
    with_norm = gain is not None

    def body(*refs):
        if with_norm:
            a_ref, w_ref, r_ref, g_ref, x_ref, h_ref, acc = refs
        else:
            a_ref, w_ref, r_ref, x_ref, acc = refs
        j = pl.program_id(1)

        @pl.when(j == 0)
        def _():
            acc[...] = jnp.zeros_like(acc)

        acc[...] += _dot(a_ref[...], w_ref[...])

        @pl.when(j == nj - 1)
        def _():
            xn = r_ref[...] + (scale * acc[...])
            x_ref[...] = xn
            if with_norm:
                r = lax.rsqrt(jnp.mean(xn * xn, axis=-1, keepdims=True) + EPS)
                h_ref[...] = (xn * r * g_ref[...]).astype(BF)

    row = pl.BlockSpec((tm, d), lambda i, j: (i, 0))
    in_specs = [pl.BlockSpec((tm, k), lambda i, j: (i, j)), pl.BlockSpec((None, k, d), lambda i, j: (j, 0, 0)), row]
    args = [a, w, resid]
    out_specs = [row]
    out_shape = [jax.ShapeDtypeStruct((s, d), F32)]
    if with_norm:
        in_specs.append(pl.BlockSpec((1, d), lambda i, j: (0, 0)))
        args.append(gain)
        out_specs.append(row)
        out_shape.append(jax.ShapeDtypeStruct((s, d), BF))
    return _launch(
        body, name=name, grid=(s // tm, nj),
        in_specs=in_specs, out_specs=out_specs, out_shape=out_shape,
        scratch_shapes=[pltpu.VMEM((tm, d), F32)],
        sem=("parallel", "arbitrary"), args=args, comm=comm)


def _loss_call(x, target, gain, name):
    s, d = x.shape
    tm = _tile(s, 512)

    def body(x_ref, t_ref, g_ref, loss_ref, dx_ref, dxb_ref, dg_ref):
        i = pl.program_id(0)

        @pl.when(i == 0)
        def _():
            loss_ref[...] = jnp.zeros_like(loss_ref)
            dg_ref[...] = jnp.zeros_like(dg_ref)

        xv = x_ref[...]
        gv = g_ref[...]
        r = lax.rsqrt(jnp.mean(xv * xv, axis=-1, keepdims=True) + EPS)
        xhat = xv * r
        err = xhat * gv - t_ref[...]
        per_tok = jnp.mean(err * err, axis=-1, keepdims=True)
        loss_ref[...] += 0.5 * jnp.sum(per_tok, axis=0, keepdims=True)
        dout = err * (1.0 / d)
        dg_ref[...] += jnp.sum(dout * xhat, axis=0, keepdims=True)
        dxhat = dout * gv
        dx = r * (dxhat - xhat * jnp.mean(dxhat * xhat, axis=-1, keepdims=True))
        dx_ref[...] = dx
        dxb_ref[...] = dx.astype(BF)

    row = pl.BlockSpec((tm, d), lambda i: (i, 0))
    vec = pl.BlockSpec((1, d), lambda i: (0, 0))
    return pl.pallas_call(
        body, name=name, grid=(s // tm,),
        in_specs=[row, row, vec],
        out_specs=[pl.BlockSpec((1, 1), lambda i: (0, 0)), row, row, vec],
        out_shape=[jax.ShapeDtypeStruct((1, 1), F32), jax.ShapeDtypeStruct((s, d), F32),
                   jax.ShapeDtypeStruct((s, d), BF), jax.ShapeDtypeStruct((1, d), F32)],
        compiler_params=_params("arbitrary"),
    )(x, target, gain)


def _bwd_up_call(dy, wd, g, u, scale, name, comm=None):
    s, d = dy.shape
    nj, k, _ = wd.shape
    tm = _tile(s, 512)

    def body(dy_ref, w_ref, g_ref, u_ref, dg_ref, du_ref):
        da = scale * _dot_nt(dy_ref[...], w_ref[...])
        gv = g_ref[...]
        sig = _sigmoid(gv)
        du_ref[...] = (da * gv * sig).astype(BF)
        dg_ref[...] = (da * u_ref[...] * sig * (1.0 + gv * (1.0 - sig))).astype(BF)

    act = pl.BlockSpec((tm, k), lambda i, j: (i, j))
    return _launch(
        body, name=name, grid=(s // tm, nj),
        in_specs=[pl.BlockSpec((tm, d), lambda i, j: (i, 0)), pl.BlockSpec((None, k, d), lambda i, j: (j, 0, 0)), act, act],
        out_specs=[act, act],
        out_shape=[jax.ShapeDtypeStruct((s, nj * k), BF), jax.ShapeDtypeStruct((s, nj * k), BF)],
        sem=("parallel", "arbitrary"), args=(dy, wd, g, u), comm=comm)


def _nt_call(dy, w, name, comm=None):
    s, d = dy.shape
    nj, k, _ = w.shape
    tm = _tile(s, 512)

    def body(dy_ref, w_ref, o_ref):
        o_ref[...] = _dot_nt(dy_ref[...], w_ref[...])

    return _launch(
        body, name=name, grid=(s // tm, nj),
        in_specs=[pl.BlockSpec((tm, d), lambda i, j: (i, 0)), pl.BlockSpec((None, k, d), lambda i, j: (j, 0, 0))],
        out_specs=[pl.BlockSpec((tm, k), lambda i, j: (i, j))],
        out_shape=[jax.ShapeDtypeStruct((s, nj * k), F32)],
        sem=("parallel", "arbitrary"), args=(dy, w), comm=comm)


def _bwd_down_call(pairs, dres, xin, gain, name, comm=None):
    s, d = xin.shape
    nj, _, k = pairs[0][1].shape
    npair = len(pairs)
    tm = _tile(s, 512)
    strip = _tile(tm, 128)

    def body(*refs):
        a_refs = refs[0:2 * npair:2]
        w_refs = refs[1:2 * npair:2]
        dres_ref, x_ref, g_ref, dx_ref, dxb_ref, dg_ref, acc = refs[2 * npair:]
        i = pl.program_id(0)
        j = pl.program_id(1)

        @pl.when(j == 0)
        def _():
            acc[...] = jnp.zeros_like(acc)

        @pl.when((i == 0) & (j == 0))
        def _():
            dg_ref[...] = jnp.zeros_like(dg_ref)

        for a_ref, w_ref in zip(a_refs, w_refs):
            acc[...] += _dot_nt(a_ref[...], w_ref[...])

        @pl.when(j == nj - 1)
        def _():
            for r0 in range(0, tm, strip):
                rows = slice(r0, r0 + strip)
                xv = x_ref[rows, :]
                r = lax.rsqrt(jnp.mean(xv * xv, axis=-1, keepdims=True) + EPS)
                xhat = xv * r
                dh = acc[rows, :]
                dg_ref[...] += jnp.sum(dh * xhat, axis=0, keepdims=True)
                dxhat = dh * g_ref[...]
                dx = dres_ref[rows, :] + r * (dxhat - xhat * jnp.mean(dxhat * xhat, axis=-1, keepdims=True))
                dx_ref[rows, :] = dx
                dxb_ref[rows, :] = dx.astype(BF)

    row = pl.BlockSpec((tm, d), lambda i, j: (i, 0))
    vec = pl.BlockSpec((1, d), lambda i, j: (0, 0))
    in_specs, args = [], []
    for a, w in pairs:
        in_specs += [pl.BlockSpec((tm, k), lambda i, j: (i, j)), pl.BlockSpec((None, d, k), lambda i, j: (j, 0, 0))]
        args += [a, w]
    once = pl.BlockSpec((tm, d), lambda i, j: (i, 0), pipeline_mode=pl.Buffered(1))
    in_specs += [once, once, vec]
    args += [dres, xin, gain]
    return _launch(
        body, name=name, grid=(s // tm, nj),
        in_specs=in_specs, out_specs=[row, row, vec],
        out_shape=[jax.ShapeDtypeStruct((s, d), F32), jax.ShapeDtypeStruct((s, d), BF), jax.ShapeDtypeStruct((1, d), F32)],
        scratch_shapes=[pltpu.VMEM((tm, d), F32)],
        sem=("arbitrary", "arbitrary"), args=args, comm=comm)


def _wgrad_call(a, b, nj, a_blocked, scale, name, comm=None):
    s = a.shape[0]
    ka = a.shape[1] // nj if a_blocked else a.shape[1]
    kb = b.shape[1] if a_blocked else b.shape[1] // nj
    ts = _tile(s, 1024)
    ns = s // ts

    def body(a_ref, b_ref, o_ref):
        t = pl.program_id(1)

        @pl.when(t == 0)
        def _():
            o_ref[...] = jnp.zeros_like(o_ref)

        o_ref[...] += _dot_tn(a_ref[...], b_ref[...])
        if scale != 1.0:
            @pl.when(t == ns - 1)
            def _():
                o_ref[...] = o_ref[...] * scale

    a_spec = pl.BlockSpec((ts, ka), (lambda j, t: (t, j)) if a_blocked else (lambda j, t: (t, 0)))
    b_spec = pl.BlockSpec((ts, kb), (lambda j, t: (t, 0)) if a_blocked else (lambda j, t: (t, j)))
    (out,), landed = _launch(
        body, name=name, grid=(nj, ns),
        in_specs=[a_spec, b_spec],
        out_specs=[pl.BlockSpec((None, ka, kb), lambda j, t: (j, 0, 0))],
        out_shape=[jax.ShapeDtypeStruct((nj, ka, kb), F32)],
        sem=("parallel", "arbitrary"), args=(a, b), comm=comm)
    return out if comm is None else (out, landed)


def _rope(v, cos, sin):
    half = v.shape[-1] // 2
    v1, v2 = v[:, :half], v[:, half:]
    return jnp.concatenate([v1 * cos - v2 * sin, v2 * cos + v1 * sin], axis=-1)


def _rope_bwd(dv, cos, sin):
    half = dv.shape[-1] // 2
    d1, d2 = dv[:, :half], dv[:, half:]
    return jnp.concatenate([d1 * cos + d2 * sin, d2 * cos - d1 * sin], axis=-1)


def _ret_consts(hd):
    c = RET_CHUNK
    log_gamma = jnp.log(1.0 - jnp.exp2(-5.0 - jnp.arange(RET_HEADS, dtype=F32)))
    idx = jnp.arange(c, dtype=F32)
    rel = idx[:, None] - idx[None, :]
    mask = rel >= 0
    decay = jnp.where(mask[None], jnp.exp(log_gamma[:, None, None] * jnp.where(mask, rel, 0.0)[None]), 0.0)
    qdec = jnp.exp(log_gamma[:, None] * (idx + 1.0)[None, :])
    kdec = jnp.exp(log_gamma[:, None] * (c - 1.0 - idx)[None, :])
    gchunk = jnp.exp(log_gamma * c)
    bc = lambda t: jnp.broadcast_to(t[:, :, None], (RET_HEADS, t.shape[1], hd))
    return decay, bc(qdec), bc(kdec), bc(gchunk[:, None])


def _rope_tables(s, hd):
    inv = jnp.power(ROPE_BASE, -jnp.arange(0, hd, 2, dtype=F32) / hd)
    ang = jnp.arange(s, dtype=F32)[:, None] * inv[None, :]
    return jnp.cos(ang), jnp.sin(ang)


def _ret_fwd_call(proj, cos, sin, consts, gret, name):
    s = proj.shape[0]
    w = proj.shape[1] // 8
    hd = w // RET_HEADS
    c = RET_CHUNK
    tt = _tile(s, 512, c)
    nc = tt // c
    decay, qdec, kdec, gch = consts
    scale = hd ** -0.5

    def body(q_ref, k_ref, v_ref, gate_ref, cos_ref, sin_ref, dec_ref, qd_ref, kd_ref, gc_ref, gn_ref,
             o_ref, m_ref, st_ref, state):
        @pl.when(pl.program_id(1) == 0)
        def _():
            state[...] = jnp.zeros_like(state)

        dec = dec_ref[...]
        for ci in range(nc):
            rows = slice(ci * c, (ci + 1) * c)
            cs, sn = cos_ref[rows, :], sin_ref[rows, :]
            q = _rope(q_ref[rows, :], cs, sn) * scale
            k = _rope(k_ref[rows, :], cs, sn)
            vb = v_ref[rows, :].astype(BF)
            sc = _dot_nt(q.astype(BF), k.astype(BF)) * dec
            prev = state[...]
            st_ref[ci] = prev
            o = _dot(sc.astype(BF), vb) + _dot((q * qd_ref[...]).astype(BF), prev.astype(BF))
            state[...] = gc_ref[...] * prev + _dot_tn((k * kd_ref[...]).astype(BF), vb)
            o_ref[rows, :] = o
            mu = jnp.mean(o, axis=-1, keepdims=True)
            cen = o - mu
            xhat = cen * lax.rsqrt(jnp.mean(cen * cen, axis=-1, keepdims=True) + EPS)
            gt = gate_ref[rows, :]
            m_ref[rows, :] = (xhat * gn_ref[...] * (gt * _sigmoid(gt))).astype(BF)

    nh = RET_HEADS
    comp = lambda j: pl.BlockSpec((tt, hd), lambda h, t, j=j: (t, j * nh + h))
    tab = pl.BlockSpec((tt, hd // 2), lambda h, t: (t, 0))
    per_head = lambda r: pl.BlockSpec((None, r, hd), lambda h, t: (h, 0, 0))
    return pl.pallas_call(
        body, name=name, grid=(nh, s // tt),
        in_specs=[comp(0), comp(1), comp(2), comp(3), tab, tab,
                  pl.BlockSpec((None, c, c), lambda h, t: (h, 0, 0)), per_head(c), per_head(c), per_head(1),
                  pl.BlockSpec((1, hd), lambda h, t: (0, h))],
        out_specs=[pl.BlockSpec((tt, hd), lambda h, t: (t, h)), pl.BlockSpec((tt, hd), lambda h, t: (t, h)),
                   pl.BlockSpec((None, nc, hd, hd), lambda h, t: (h, t, 0, 0))],
        out_shape=[jax.ShapeDtypeStruct((s, w), F32), jax.ShapeDtypeStruct((s, w), BF),
                   jax.ShapeDtypeStruct((nh, s // c, hd, hd), F32)],
        scratch_shapes=[pltpu.VMEM((hd, hd), F32)],
        compiler_params=_params("parallel", "arbitrary"),
    )(proj, proj, proj, proj, cos, sin, decay, qdec, kdec, gch, gret)


def _ret_bwd_call(proj, cos, sin, consts, gret, o_raw, states, dmerged, name):
    s = proj.shape[0]
    w = proj.shape[1] // 8
    hd = w // RET_HEADS
    c = RET_CHUNK
    tt = _tile(s, 512, c)
    nc = tt // c
    nt = s // tt
    decay, qdec, kdec, gch = consts
    scale = hd ** -0.5

    def body(q_ref, k_ref, v_ref, gate_ref, cos_ref, sin_ref, dec_ref, qd_ref, kd_ref, gc_ref, gn_ref,
             o_ref, st_ref, dm_ref, dq_ref, dk_ref, dv_ref, dgate_ref, dgn_ref, dstate):
        @pl.when(pl.program_id(1) == 0)
        def _():
            dstate[...] = jnp.zeros_like(dstate)
            dgn_ref[...] = jnp.zeros_like(dgn_ref)

        dec = dec_ref[...]
        gn = gn_ref[...]
        for ci in reversed(range(nc)):
            rows = slice(ci * c, (ci + 1) * c)
            cs, sn = cos_ref[rows, :], sin_ref[rows, :]
            q = _rope(q_ref[rows, :], cs, sn) * scale
            k = _rope(k_ref[rows, :], cs, sn)
            qb, kb = q.astype(BF), k.astype(BF)
            vb = v_ref[rows, :].astype(BF)
            sc = _dot_nt(qb, kb) * dec
            o = o_ref[rows, :]
            mu = jnp.mean(o, axis=-1, keepdims=True)
            cen = o - mu
            rstd = lax.rsqrt(jnp.mean(cen * cen, axis=-1, keepdims=True) + EPS)
            xhat = cen * rstd
            gt = gate_ref[rows, :]
            sig = _sigmoid(gt)
            sg = gt * sig
            dm = dm_ref[rows, :]
            dgn_ref[...] += jnp.sum(dm * xhat * sg, axis=0, keepdims=True)
            dgate_ref[rows, :] = (dm * xhat * gn * sig * (1.0 + gt * (1.0 - sig))).astype(BF)
            dxhat = dm * gn * sg
            do = rstd * (dxhat - jnp.mean(dxhat, axis=-1, keepdims=True)
                         - xhat * jnp.mean(dxhat * xhat, axis=-1, keepdims=True))
            dob = do.astype(BF)
            prev = st_ref[ci]
            ds = dstate[...]
            dsb = ds.astype(BF)
            dsc = (_dot_nt(dob, vb) * dec).astype(BF)
            dq = _dot(dsc, kb) + _dot_nt(dob, prev.astype(BF)) * qd_ref[...]
            dk = _dot_tn(dsc, qb) + _dot_nt(vb, dsb) * kd_ref[...]
            dv = _dot_tn(sc.astype(BF), dob) + _dot((k * kd_ref[...]).astype(BF), dsb)
            dstate[...] = gc_ref[...] * ds + _dot_tn((q * qd_ref[...]).astype(BF), dob)
            dq_ref[rows, :] = _rope_bwd(dq * scale, cs, sn).astype(BF)
            dk_ref[rows, :] = _rope_bwd(dk, cs, sn).astype(BF)
            dv_ref[rows, :] = dv.astype(BF)

    nh = RET_HEADS
    rev = lambda t: nt - 1 - t
    comp = lambda j: pl.BlockSpec((tt, hd), lambda h, t, j=j: (rev(t), j * nh + h))
    tab = pl.BlockSpec((tt, hd // 2), lambda h, t: (rev(t), 0))
    per_head = lambda r: pl.BlockSpec((None, r, hd), lambda h, t: (h, 0, 0))
    head_cols = pl.BlockSpec((tt, hd), lambda h, t: (rev(t), h))
    gvec = pl.BlockSpec((1, hd), lambda h, t: (0, h))
    act = jax.ShapeDtypeStruct((s, w), BF)
    return pl.pallas_call(
        body, name=name, grid=(nh, nt),
        in_specs=[comp(0), comp(1), comp(2), comp(3), tab, tab,
                  pl.BlockSpec((None, c, c), lambda h, t: (h, 0, 0)), per_head(c), per_head(c), per_head(1), gvec,
                  head_cols, pl.BlockSpec((None, nc, hd, hd), lambda h, t: (h, rev(t), 0, 0)), head_cols],
        out_specs=[head_cols, head_cols, head_cols, head_cols, gvec],
        out_shape=[act, act, act, act, jax.ShapeDtypeStruct((1, w), F32)],
        scratch_shapes=[pltpu.VMEM((hd, hd), F32)],
        compiler_params=_params("parallel", "arbitrary"),
    )(proj, proj, proj, proj, cos, sin, decay, qdec, kdec, gch, gret, o_raw, states, dmerged)


def _block_tri(n, bs, upper):
    r = jnp.arange(n)[:, None]
    cidx = jnp.arange(n)[None, :]
    same = (r // bs) == (cidx // bs)
    return jnp.where(same & ((cidx >= r) if upper else (cidx <= r)), 1.0, 0.0).astype(F32)


def _dot_exact(a, b):
    return jnp.dot(a, b, preferred_element_type=F32, precision=lax.Precision.HIGHEST)


def _hgrn_gates(z, lbv):
    sz = _sigmoid(z)
    oml = 1.0 - lbv
    f = lbv + oml * sz
    key = oml * (1.0 - sz)
    return sz, f, key


def _hgrn_fwd_call(proj, lb_logits, ghg, name):
    s = proj.shape[0]
    w = proj.shape[1] // 8
    nh = HGRN_HEADS
    hd = w // nh
    bs = HGRN_BLOCK
    tt = _tile(s, 256, bs)
    nb = tt // bs

    def body(q_ref, z_ref, v_ref, gate_ref, lb_ref, gn_ref, tril_ref, o_ref, m_ref, st_ref, state, upd):
        @pl.when(pl.program_id(1) == 0)
        def _():
            state[...] = jnp.zeros_like(state)

        lbv = _sigmoid(lb_ref[...])
        _, f, key = _hgrn_gates(z_ref[...], lbv)
        qr = q_ref[...]
        q = qr * _sigmoid(qr)
        v = v_ref[...]
        g = _dot_exact(tril_ref[...], jnp.log(f))
        blocks = lambda t: t.reshape(nb, bs, hd)
        g3, q3, k3, v3 = blocks(g), blocks(q), blocks(key), blocks(v)
        glast3 = g3[:, bs - 1:bs, :]
        row_id = lax.broadcasted_iota(jnp.int32, (nb, bs, hd), 1)
        o3 = jnp.zeros((nb, bs, hd), F32)
        for j in range(bs):
            wj = jnp.where(row_id >= j, jnp.exp(jnp.minimum(g3 - g3[:, j:j + 1, :], 0.0)), 0.0)
            a = jnp.sum(q3 * k3[:, j:j + 1, :] * wj, axis=-1, keepdims=True)
            o3 = o3 + a * v3[:, j:j + 1, :]
        ktb = (k3 * jnp.exp(glast3 - g3)).reshape(tt, hd).astype(BF)
        vb = v.astype(BF)
        for b in range(nb):
            rows = slice(b * bs, (b + 1) * bs)
            upd[b] = _dot_tn(vb[rows, :], ktb[rows, :])
        egl3 = jnp.exp(glast3)
        st = state[...]
        for b in range(nb):
            st_ref[b] = st
            st = st * egl3[b] + upd[b]
        state[...] = st
        qgb = (q * jnp.exp(g)).astype(BF)
        o_intra = o3.reshape(tt, hd)
        gn = gn_ref[...]
        for b in range(nb):
            rows = slice(b * bs, (b + 1) * bs)
            o = o_intra[rows, :] + _dot_nt(qgb[rows, :], st_ref[b].astype(BF))
            o_ref[rows, :] = o
            gt = gate_ref[rows, :]
            xhat = o * lax.rsqrt(jnp.mean(o * o, axis=-1, keepdims=True) + EPS)
            m_ref[rows, :] = (xhat * gn * (gt * _sigmoid(gt))).astype(BF)

    comp = lambda j: pl.BlockSpec((tt, hd), lambda h, t, j=j: (t, j * nh + h))
    gvec = pl.BlockSpec((1, hd), lambda h, t: (0, h))
    head_cols = pl.BlockSpec((tt, hd), lambda h, t: (t, h))
    return pl.pallas_call(
        body, name=name, grid=(nh, s // tt),
        in_specs=[comp(4), comp(5), comp(6), comp(7), gvec, gvec, pl.BlockSpec((tt, tt), lambda h, t: (0, 0))],
        out_specs=[head_cols, head_cols, pl.BlockSpec((None, nb, hd, hd), lambda h, t: (h, t, 0, 0))],
        out_shape=[jax.ShapeDtypeStruct((s, w), F32), jax.ShapeDtypeStruct((s, w), BF),
                   jax.ShapeDtypeStruct((nh, s // bs, hd, hd), F32)],
        scratch_shapes=[pltpu.VMEM((hd, hd), F32), pltpu.VMEM((nb, hd, hd), F32)],
        compiler_params=_params("parallel", "arbitrary"),
    )(proj, proj, proj, proj, lb_logits, ghg, _block_tri(tt, bs, upper=False))


def _hgrn_bwd_call(proj, lb_logits, ghg, o_raw, states, dmerged, name, comm=None):
    s = proj.shape[0]
    w = proj.shape[1] // 8
    nh = HGRN_HEADS
    hd = w // nh
    bs = HGRN_BLOCK
    tt = _tile(s, 256, bs)
    nb = tt // bs
    nt = s // tt

    def body(q_ref, z_ref, v_ref, gate_ref, lb_ref, gn_ref, tril_ref, triu_ref, o_ref, st_ref, dm_ref,
             dq_ref, dz_ref, dv_ref, dgate_ref, dlb_ref, dgn_ref,
             dstate, ds_all, inc, dq_s, dk_s, dv_s, dgl_s, dk_rows, dv_rows):
        @pl.when(pl.program_id(1) == 0)
        def _():
            dstate[...] = jnp.zeros_like(dstate)
            dlb_ref[...] = jnp.zeros_like(dlb_ref)
            dgn_ref[...] = jnp.zeros_like(dgn_ref)

        lbv = _sigmoid(lb_ref[...])
        oml = 1.0 - lbv
        gn = gn_ref[...]
        sz, f, key = _hgrn_gates(z_ref[...], lbv)
        qr = q_ref[...]
        sq = _sigmoid(qr)
        q = qr * sq
        v = v_ref[...]
        g = _dot_exact(tril_ref[...], jnp.log(f))
        eg = jnp.exp(g)
        blocks = lambda t: t.reshape(nb, bs, hd)
        g3, q3, k3, v3 = blocks(g), blocks(q), blocks(key), blocks(v)
        glast3 = g3[:, bs - 1:bs, :]
        egl3 = jnp.exp(glast3)
        ktail3 = jnp.exp(glast3 - g3)
        o = o_ref[...]
        rstd = lax.rsqrt(jnp.mean(o * o, axis=-1, keepdims=True) + EPS)
        xhat = o * rstd
        gt = gate_ref[...]
        sig = _sigmoid(gt)
        sg = gt * sig
        dm = dm_ref[...]
        dgn_ref[...] += jnp.sum(dm * xhat * sg, axis=0, keepdims=True)
        dgate_ref[...] = (dm * xhat * gn * sig * (1.0 + gt * (1.0 - sig))).astype(BF)
        dxhat = dm * gn * sg
        do = rstd * (dxhat - xhat * jnp.mean(dxhat * xhat, axis=-1, keepdims=True))
        dob = do.astype(BF)
        do3 = blocks(do)
        qgb = (q * eg).astype(BF)
        for b in range(nb):
            rows = slice(b * bs, (b + 1) * bs)
            inc[b] = _dot_tn(dob[rows, :], qgb[rows, :])
        ds = dstate[...]
        for b in reversed(range(nb)):
            ds_all[b] = ds
            ds = ds * egl3[b] + inc[b]
        dstate[...] = ds
        ktb = (k3 * ktail3).reshape(tt, hd).astype(BF)
        vb = v.astype(BF)
        for b in range(nb):
            rows = slice(b * bs, (b + 1) * bs)
            prev = st_ref[b]
            dsb = ds_all[b]
            dsbb = dsb.astype(BF)
            dq_s[rows, :] = _dot(dob[rows, :], prev.astype(BF))
            dk_s[rows, :] = _dot(vb[rows, :], dsbb)
            dv_s[rows, :] = _dot_nt(ktb[rows, :], dsbb)
            dgl_s[b] = jnp.sum(prev * dsb, axis=0, keepdims=True)
        dq3 = blocks(dq_s[...] * eg)
        dk3 = blocks(dk_s[...]) * ktail3
        dg_last3 = jnp.sum(k3 * dk3, axis=1, keepdims=True) + egl3 * dgl_s[...]
        row_id = lax.broadcasted_iota(jnp.int32, (nb, bs, hd), 1)
        for j in range(bs):
            wj = jnp.where(row_id >= j, jnp.exp(jnp.minimum(g3 - g3[:, j:j + 1, :], 0.0)), 0.0)
            kj = k3[:, j:j + 1, :]
            a = jnp.sum(q3 * kj * wj, axis=-1, keepdims=True)
            da = jnp.sum(do3 * v3[:, j:j + 1, :], axis=-1, keepdims=True)
            dv_rows[:, j:j + 1, :] = jnp.sum(a * do3, axis=1, keepdims=True)
            dq3 = dq3 + da * kj * wj
            dk_rows[:, j:j + 1, :] = jnp.sum(da * q3 * wj, axis=1, keepdims=True)
        dk3 = dk3 + dk_rows[...]
        dv = dv_s[...] + dv_rows[...].reshape(tt, hd)
        dg3 = q3 * dq3 - k3 * dk3 + jnp.where(row_id == bs - 1, dg_last3, 0.0)
        dlf = _dot_exact(triu_ref[...], dg3.reshape(tt, hd))
        dk = dk3.reshape(tt, hd)
        dfk = dlf / f - dk
        dlb_ref[...] += jnp.sum(dfk * (1.0 - sz), axis=0, keepdims=True) * (lbv * oml)
        dz_ref[...] = (dfk * oml * sz * (1.0 - sz)).astype(BF)
        dq_ref[...] = (dq3.reshape(tt, hd) * sq * (1.0 + qr * (1.0 - sq))).astype(BF)
        dv_ref[...] = dv.astype(BF)

    rev = lambda t: nt - 1 - t
    comp = lambda j: pl.BlockSpec((tt, hd), lambda h, t, j=j: (rev(t), j * nh + h))
    gvec = pl.BlockSpec((1, hd), lambda h, t: (0, h))
    head_cols = pl.BlockSpec((tt, hd), lambda h, t: (rev(t), h))
    tri = pl.BlockSpec((tt, tt), lambda h, t: (0, 0))
    act = jax.ShapeDtypeStruct((s, w), BF)
    vec = jax.ShapeDtypeStruct((1, w), F32)
    tile_f32 = pltpu.VMEM((tt, hd), F32)
    return _launch(
        body, name=name, grid=(nh, nt),
        in_specs=[comp(4), comp(5), comp(6), comp(7), gvec, gvec, tri, tri, head_cols,
                  pl.BlockSpec((None, nb, hd, hd), lambda h, t: (h, rev(t), 0, 0)),
                  pl.BlockSpec((tt, hd), lambda h, t: (rev(t), nh + h))],
        out_specs=[head_cols, head_cols, head_cols, head_cols, gvec, gvec],
        out_shape=[act, act, act, act, vec, vec],
        scratch_shapes=[pltpu.VMEM((hd, hd), F32), pltpu.VMEM((nb, hd, hd), F32), pltpu.VMEM((nb, hd, hd), F32),
                        tile_f32, tile_f32, tile_f32, pltpu.VMEM((nb, 1, hd), F32),
                        pltpu.VMEM((nb, bs, hd), F32), pltpu.VMEM((nb, bs, hd), F32)],
        sem=("parallel", "arbitrary"),
        args=(proj, proj, proj, proj, lb_logits, ghg, _block_tri(tt, bs, upper=False), _block_tri(tt, bs, upper=True),
              o_raw, states, dmerged), comm=comm)


def _position():
    return lax.axis_index("x"), lax.axis_index("y"), lax.axis_index("c")


def _all_gather_call(shards, name):
    n = len(shards)

    def body(*refs):
        ins, outs = refs[:n], refs[n:2 * n]
        send_sems, recv_sems, local_sems = refs[2 * n:]
        x, y, c = _position()
        me, sibling = (x, y, c), (x, y, 1 - c)
        chips = [(1 - x, y), (x, 1 - y), (1 - x, 1 - y)]

        def slot(a, p):
            return outs[a].at[4 * p[0] + 2 * p[1] + p[2]]

        def copy(a, k, block, to, src=None):
            return pltpu.make_async_remote_copy(
                src_ref=slot(a, block) if src is None else src, dst_ref=slot(a, block),
                send_sem=send_sems.at[a * 7 + k], recv_sem=recv_sems.at[a * 7 + k],
                device_id=to, device_id_type=MESH)

        mine = [pltpu.make_async_copy(ins[a], slot(a, me), local_sems.at[a]) for a in range(n)]
        for cp in mine:
            cp.start()
        first = []
        for a in range(n):
            first.append(copy(a, 0, me, sibling, src=ins[a]))
            first += [copy(a, 1 + j, me, (*chip, c), src=ins[a]) for j, chip in enumerate(chips)]
        for cp in first:
            cp.start()
        passed = []
        for j, chip in enumerate(chips):
            for a in range(n):
                copy(a, 1 + j, (*chip, c), me).wait_recv()
                fwd = copy(a, 4 + j, (*chip, c), sibling)
                fwd.start()
                passed.append(fwd)
        for a in range(n):
            copy(a, 0, sibling, me).wait_recv()
            for j, chip in enumerate(chips):
                copy(a, 4 + j, (*chip, 1 - c), me).wait_recv()
        for cp in first + passed:
            cp.wait_send()
        for cp in mine:
            cp.wait()

    return pl.pallas_call(
        body, name=name,
        in_specs=[HBM_SPEC] * n, out_specs=[HBM_SPEC] * n,
        out_shape=[jax.ShapeDtypeStruct((N_DEV,) + t.shape, t.dtype) for t in shards],
        scratch_shapes=[pltpu.SemaphoreType.DMA((7 * n,)), pltpu.SemaphoreType.DMA((7 * n,)),
                        pltpu.SemaphoreType.DMA((n,))],
    )(*shards)


def _slot(ref, p):
    return ref.at[4 * p[0] + 2 * p[1] + p[2]]


def _gather_round1(shards):
    n = len(shards)

    def plan(ins, outs, send_sems, recv_sems):
        x, y, c = _position()
        me = (x, y, c)
        peers = [(x, y, 1 - c), (1 - x, y, c), (x, 1 - y, c), (1 - x, 1 - y, c)]
        sends, recvs, local = [], [], []
        for a in range(n):
            local.append(pltpu.make_async_copy(ins[a], _slot(outs[a], me), send_sems.at[4 * n + a]))
            for k, peer in enumerate(peers):
                sems = dict(send_sem=send_sems.at[4 * a + k], recv_sem=recv_sems.at[4 * a + k],
                            device_id=peer, device_id_type=MESH)
                sends.append(pltpu.make_async_remote_copy(src_ref=ins[a], dst_ref=_slot(outs[a], me), **sems))
                recvs.append(pltpu.make_async_remote_copy(src_ref=ins[a], dst_ref=_slot(outs[a], peer), **sems))
        return sends, recvs, local

    def start(*refs):
        sends, _, local = plan(*refs)
        for cp in local + sends:
            cp.start()

    def finish(*refs):
        sends, recvs, local = plan(*refs)
        for cp in recvs:
            cp.wait_recv()
        for cp in sends:
            cp.wait_send()
        for cp in local:
            cp.wait()

    return _Comm(shards, [jax.ShapeDtypeStruct((N_DEV,) + t.shape, t.dtype) for t in shards], 5 * n, start, finish)


def _gather_round2(gathered):
    n = len(gathered)

    def plan(ins, outs, send_sems, recv_sems):
        x, y, c = _position()
        chips = [(1 - x, y), (x, 1 - y), (1 - x, 1 - y)]
        sends, recvs = [], []
        for a in range(n):
            for k, chip in enumerate(chips):
                sems = dict(send_sem=send_sems.at[3 * a + k], recv_sem=recv_sems.at[3 * a + k],
                            device_id=(x, y, 1 - c), device_id_type=MESH)
                sends.append(pltpu.make_async_remote_copy(
                    src_ref=_slot(ins[a], (*chip, c)), dst_ref=_slot(outs[a], (*chip, c)), **sems))
                recvs.append(pltpu.make_async_remote_copy(
                    src_ref=_slot(ins[a], (*chip, c)), dst_ref=_slot(outs[a], (*chip, 1 - c)), **sems))
        return sends, recvs

    def start(*refs):
        for cp in plan(*refs)[0]:
            cp.start()

    def finish(*refs):
        sends, recvs = plan(*refs)
        for cp in recvs:
            cp.wait_recv()
        for cp in sends:
            cp.wait_send()

    return _Comm(gathered, [jax.ShapeDtypeStruct(t.shape, t.dtype) for t in gathered], 3 * n, start, finish,
                 aliases={a: a for a in range(n)})


def _sibling_exchange(grads):
    n = len(grads)

    def plan(ins, outs, send_sems, recv_sems):
        x, y, c = _position()
        return [pltpu.make_async_remote_copy(
            src_ref=ins[a].at[2 * q + (1 - c)], dst_ref=outs[a].at[q],
            send_sem=send_sems.at[a * 4 + q], recv_sem=recv_sems.at[a * 4 + q],
            device_id=(x, y, 1 - c), device_id_type=MESH) for a in range(n) for q in range(4)]

    def start(*refs):
        for cp in plan(*refs):
            cp.start()

    def finish(*refs):
        for cp in plan(*refs):
            cp.wait()

    return _Comm(grads, [jax.ShapeDtypeStruct((4,) + t.shape[1:], t.dtype) for t in grads], 4 * n, start, finish)


def _chip_exchange(partials):
    n = len(partials)

    def plan(ins, outs, send_sems, recv_sems):
        x, y, c = _position()
        chips = [(1 - x, y), (x, 1 - y), (1 - x, 1 - y)]
        return [pltpu.make_async_remote_copy(
            src_ref=ins[a].at[2 * chip[0] + chip[1]], dst_ref=outs[a].at[k],
            send_sem=send_sems.at[a * 3 + k], recv_sem=recv_sems.at[a * 3 + k],
            device_id=(*chip, c), device_id_type=MESH) for a in range(n) for k, chip in enumerate(chips)]

    def start(*refs):
        for cp in plan(*refs):
            cp.start()

    def finish(*refs):
        for cp in plan(*refs):
            cp.wait()

    return _Comm(partials, [jax.ShapeDtypeStruct((3,) + t.shape[1:], t.dtype) for t in partials], 3 * n, start, finish)


class _SemWindow:
    def __init__(self, sems, offset):
        self._sems, self._offset = sems, offset

    @property
    def at(self):
        return self

    def __getitem__(self, i):
        return self._sems.at[self._offset + i]


def _join(parts):
    def each(fn_name, cins, couts, send_sems, recv_sems):
        i = o = sem = 0
        for p in parts:
            ni, no = len(p.operands), len(p.out_shape)
            getattr(p, fn_name)(cins[i:i + ni], couts[o:o + no], _SemWindow(send_sems, sem), _SemWindow(recv_sems, sem))
            i, o, sem = i + ni, o + no, sem + p.n_sems

    assert not any(p.aliases for p in parts)
    return _Comm([t for p in parts for t in p.operands], [t for p in parts for t in p.out_shape],
                 sum(p.n_sems for p in parts), functools.partial(each, "start"), functools.partial(each, "finish"))


def _pair_sum_call(grad, recv, parity, name):
    _, r, ccols = grad.shape
    tr = _tile(r, 256)

    def body(par_ref, g_ref, r_ref, p_ref, pb_ref):
        del par_ref
        p = g_ref[...] + r_ref[...]
        p_ref[...] = p
        pb_ref[...] = p.astype(BF)

    blk = lambda fn: pl.BlockSpec((None, tr, ccols), fn)
    return pl.pallas_call(
        body, name=name,
        grid_spec=pltpu.PrefetchScalarGridSpec(
            num_scalar_prefetch=1, grid=(4, r // tr),
            in_specs=[blk(lambda q, i, par: (2 * q + par[0], i, 0)), blk(lambda q, i, par: (q, i, 0))],
            out_specs=[blk(lambda q, i, par: (q, i, 0)), blk(lambda q, i, par: (q, i, 0))]),
        out_shape=[jax.ShapeDtypeStruct((4, r, ccols), F32), jax.ShapeDtypeStruct((4, r, ccols), BF)],
        compiler_params=_params("parallel", "parallel"),
    )(parity, grad, recv)


def _adamw_math(w, g, m, v):
    m = ADAM_B1 * m + (1.0 - ADAM_B1) * g
    v = ADAM_B2 * v + (1.0 - ADAM_B2) * (g * g)
    m_hat = m / (1.0 - ADAM_B1 ** ADAM_STEP)
    v_hat = v / (1.0 - ADAM_B2 ** ADAM_STEP)
    delta = -ADAM_LR * (m_hat / (jnp.sqrt(v_hat) + ADAM_EPS) + ADAM_WD * w)
    return delta, m, v


def _adamw_matrix_call(partial, recv, chip, w, m, v, name):
    r, ccols = w.shape
    gcols = partial.shape[2]
    tr = _tile(r, 256)

    def body(chip_ref, p_ref, r_ref, w_ref, m_ref, v_ref, g_out, d_out, m_out, v_out):
        del chip_ref
        cols = pl.ds(0, ccols)
        g = (p_ref[:, cols] + r_ref[0, :, cols].astype(F32) + r_ref[1, :, cols].astype(F32)
             + r_ref[2, :, cols].astype(F32))
        delta, mn, vn = _adamw_math(w_ref[...], g, m_ref[...], v_ref[...])
        g_out[...] = g
        d_out[...] = delta
        m_out[...] = mn
        v_out[...] = vn

    mat = pl.BlockSpec((tr, ccols), lambda i, ch: (i, 0))
    shp = jax.ShapeDtypeStruct((r, ccols), F32)
    return pl.pallas_call(
        body, name=name,
        grid_spec=pltpu.PrefetchScalarGridSpec(
            num_scalar_prefetch=1, grid=(r // tr,),
            in_specs=[pl.BlockSpec((None, tr, gcols), lambda i, ch: (ch[0], i, 0)),
                      pl.BlockSpec((3, tr, gcols), lambda i, ch: (0, i, 0)), mat, mat, mat],
            out_specs=[mat, mat, mat, mat]),
        out_shape=[shp, shp, shp, shp],
        compiler_params=_params("parallel"),
    )(chip, partial, recv, w, m, v)


def _adamw_vector_call(gathered, w, m, v, name):
    n = w.shape[1]

    def body(p_ref, w_ref, m_ref, v_ref, g_out, d_out, m_out, v_out):
        g = p_ref[0:1, :]
        for k in range(1, N_DEV):
            g = g + p_ref[k:k + 1, :]
        delta, mn, vn = _adamw_math(w_ref[...], g, m_ref[...], v_ref[...])
        g_out[...] = g
        d_out[...] = delta
        m_out[...] = mn
        v_out[...] = vn

    shp = jax.ShapeDtypeStruct((1, n), F32)
    return pl.pallas_call(body, name=name, out_shape=[shp, shp, shp, shp])(gathered, w, m, v)


def _round_up(n, mult):
    return (n + mult - 1) // mult * mult


def kernel(x, ffn1_norm, ffn1_w_gate, ffn1_w_up, ffn1_w_down, mix_norm, w_in, ret_norm_g, hgrn_lb_logits, hgrn_norm_g, w_out, ffn2_norm, ffn2_w_gate, ffn2_w_up, ffn2_w_down, final_norm, loss_target, m_ffn1_norm, m_ffn1_w_gate, m_ffn1_w_up, m_ffn1_w_down, m_mix_norm, m_w_in, m_ret_norm_g, m_hgrn_lb_logits, m_hgrn_norm_g, m_w_out, m_ffn2_norm, m_ffn2_w_gate, m_ffn2_w_up, m_ffn2_w_down, m_final_norm, v_ffn1_norm, v_ffn1_w_gate, v_ffn1_w_up, v_ffn1_w_down, v_mix_norm, v_w_in, v_ret_norm_g, v_hgrn_lb_logits, v_hgrn_norm_g, v_w_out, v_ffn2_norm, v_ffn2_w_gate, v_ffn2_w_up, v_ffn2_w_down, v_final_norm):
    xs = x[0]
    target = loss_target[0]
    s, d = xs.shape
    f_loc = ffn1_w_gate.shape[2]
    fp = _round_up(f_loc, LANE)
    pad_cols = lambda t: jnp.pad(t[0], ((0, 0), (0, fp - f_loc)))
    pad_rows = lambda t: jnp.pad(t[0], ((0, fp - f_loc), (0, 0)))

    mat_names = ["ffn1_w_gate", "ffn1_w_up", "ffn1_w_down", "w_in", "w_out", "ffn2_w_gate", "ffn2_w_up", "ffn2_w_down"]
    mat_pad = [pad_cols, pad_cols, pad_rows, lambda t: t[0], lambda t: t[0], pad_cols, pad_cols, pad_rows]
    mat_w = [ffn1_w_gate, ffn1_w_up, ffn1_w_down, w_in, w_out, ffn2_w_gate, ffn2_w_up, ffn2_w_down]
    mat_m = [m_ffn1_w_gate, m_ffn1_w_up, m_ffn1_w_down, m_w_in, m_w_out, m_ffn2_w_gate, m_ffn2_w_up, m_ffn2_w_down]
    mat_v = [v_ffn1_w_gate, v_ffn1_w_up, v_ffn1_w_down, v_w_in, v_w_out, v_ffn2_w_gate, v_ffn2_w_up, v_ffn2_w_down]

    cx, cy, cc = _position()
    parity = jnp.reshape(cc, (1,)).astype(jnp.int32)
    chip = jnp.reshape(2 * cx + cy, (1,)).astype(jnp.int32)
    mat_index = {nm: i for i, nm in enumerate(mat_names)}
    mat_out = {}

    def pair_sums(names, grads, from_sibling):
        return [_pair_sum_call(g, r, parity, "pair_sum_" + nm) for nm, g, r in zip(names, grads, from_sibling)]

    def update(names, sums, from_chips):
        for nm, (p, _), r in zip(names, sums, from_chips):
            i = mat_index[nm]
            res = _adamw_matrix_call(p, r, chip, mat_w[i][0], mat_m[i][0], mat_v[i][0], "adamw_" + nm)
            mat_out[nm] = [t[None] for t in res]

    shards = [p(t.astype(BF)) for p, t in zip(mat_pad, mat_w)]
    wg1, wu1, wd1 = _all_gather_call(shards[:3], "gather_ffn1")

    h1 = _rmsnorm_call(xs, ffn1_norm, "ffn1_norm")
    (g1, u1, a1), landed = _ffn_up_call(h1, wg1, wu1, "ffn1_up", comm=_gather_round1(shards[3:]))
    (x1, h2), (win, wout, wg2, wu2, wd2) = _down_call(a1, wd1, xs, mix_norm, FFN_RESIDUAL_WEIGHT, "ffn1_down",
                                                      comm=_gather_round2(landed))
    proj = _proj_call(h2, win, "mix_in")
    wmix = proj.shape[1] // 8
    cos, sin = _rope_tables(s, wmix // RET_HEADS)
    consts = _ret_consts(wmix // RET_HEADS)
    o_ret, m_ret, st_ret = _ret_fwd_call(proj, cos, sin, consts, ret_norm_g, "ret_fwd")
    o_hg, m_hg, st_hg = _hgrn_fwd_call(proj, hgrn_lb_logits, hgrn_norm_g, "hgrn_fwd")
    merged = jnp.concatenate([m_ret, m_hg], axis=1)
    wout_wide = wout.reshape(2, wout.shape[0] * wout.shape[1] // 2, d)
    (x2, h3), _ = _down_call(merged, wout_wide, x1, ffn2_norm, 1.0, "mix_out")
    (g2, u2, a2), _ = _ffn_up_call(h3, wg2, wu2, "ffn2_up")
    (x3,), _ = _down_call(a2, wd2, x2, None, FFN_RESIDUAL_WEIGHT, "ffn2_down")
    loss_part, dx3, dx3b, gv_final = _loss_call(x3, target, final_norm[None, :], "loss_head")

    (dg2, du2), _ = _bwd_up_call(dx3b, wd2, g2, u2, FFN_RESIDUAL_WEIGHT, "ffn2_bwd_up")
    (dx2, dx2b, gv_n3), _ = _bwd_down_call([(dg2, wg2), (du2, wu2)], dx3, x2, ffn2_norm, "ffn2_bwd_down")
    names_a = ["ffn2_w_gate", "ffn2_w_up", "ffn2_w_down"]
    grads_a = [_wgrad_call(h3, dg2, N_DEV, False, 1.0, "ffn2_wgrad_gate"),
               _wgrad_call(h3, du2, N_DEV, False, 1.0, "ffn2_wgrad_up"),
               _wgrad_call(a2, dx3b, N_DEV, True, FFN_RESIDUAL_WEIGHT, "ffn2_wgrad_down")]

    (dmerged,), sib_a = _nt_call(dx2b, wout_wide, "mix_out_bwd", comm=_sibling_exchange(grads_a))
    gm_out = _wgrad_call(merged, dx2b, 2, True, 1.0, "mix_out_wgrad").reshape(wout.shape)
    sums_a = pair_sums(names_a, grads_a, sib_a)
    drq, drk, drv, drg, gv_ret = _ret_bwd_call(proj, cos, sin, consts, ret_norm_g, o_ret, st_ret, dmerged, "ret_bwd")
    (dhq, dhf, dhi, dhg, gv_lb, gv_hg), chips_a = _hgrn_bwd_call(
        proj, hgrn_lb_logits, hgrn_norm_g, o_hg, st_hg, dmerged, "hgrn_bwd",
        comm=_chip_exchange([pb for _, pb in sums_a]))
    update(names_a, sums_a, chips_a)
    dproj = jnp.concatenate([drq, drk, drv, drg, dhq, dhf, dhi, dhg], axis=1)
    (dx1, dx1b, gv_n2), _ = _bwd_down_call([(dproj, win)], dx2, x1, mix_norm, "mix_in_bwd")
    names_b = ["w_out", "w_in", "ffn1_w_down"]
    grads_b = [gm_out, _wgrad_call(h2, dproj, N_DEV, False, 1.0, "mix_in_wgrad"),
               _wgrad_call(a1, dx1b, N_DEV, True, FFN_RESIDUAL_WEIGHT, "ffn1_wgrad_down")]

    (dg1, du1), sib_b = _bwd_up_call(dx1b, wd1, g1, u1, FFN_RESIDUAL_WEIGHT, "ffn1_bwd_up",
                                     comm=_sibling_exchange(grads_b))
    sums_b = pair_sums(names_b, grads_b, sib_b)
    gm_g1, chips_b = _wgrad_call(h1, dg1, N_DEV, False, 1.0, "ffn1_wgrad_gate",
                                 comm=_chip_exchange([pb for _, pb in sums_b]))
    update(names_b, sums_b, chips_b)
    gm_u1, sib_g = _wgrad_call(h1, du1, N_DEV, False, 1.0, "ffn1_wgrad_up", comm=_sibling_exchange([gm_g1]))
    sums_g = pair_sums(["ffn1_w_gate"], [gm_g1], sib_g)
    (dx0, _, gv_n1), landed = _bwd_down_call(
        [(dg1, wg1), (du1, wu1)], dx1, xs, ffn1_norm, "ffn1_bwd_down",
        comm=_join([_chip_exchange([sums_g[0][1]]), _sibling_exchange([gm_u1])]))
    update(["ffn1_w_gate"], sums_g, landed[:1])
    sums_u = pair_sums(["ffn1_w_up"], [gm_u1], landed[1:])
    update(["ffn1_w_up"], sums_u, _comm_only_call(_chip_exchange([sums_u[0][1]]), "ffn1_up_grads_to_chips"))

    vec_names = ["ffn1_norm", "mix_norm", "ret_norm_g", "hgrn_lb_logits", "hgrn_norm_g", "ffn2_norm", "final_norm"]
    vec_g = [gv_n1, gv_n2, gv_ret, gv_lb, gv_hg, gv_n3, gv_final]
    vec_w = [ffn1_norm, mix_norm, ret_norm_g, hgrn_lb_logits, hgrn_norm_g, ffn2_norm, final_norm[None, :]]
    vec_m = [m_ffn1_norm, m_mix_norm, m_ret_norm_g, m_hgrn_lb_logits, m_hgrn_norm_g, m_ffn2_norm, m_final_norm[None, :]]
    vec_v = [v_ffn1_norm, v_mix_norm, v_ret_norm_g, v_hgrn_lb_logits, v_hgrn_norm_g, v_ffn2_norm, v_final_norm[None, :]]
    cat = lambda ts: jnp.concatenate(ts, axis=1)
    (vec_all,) = _all_gather_call([cat(vec_g)], "gather_vector_grads")
    vres = _adamw_vector_call(vec_all[:, 0, :], cat(vec_w), cat(vec_m), cat(vec_v), "adamw_vectors")
    vec_out = {}
    off = 0
    for nm, t in zip(vec_names, vec_w):
        n = t.shape[1]
        parts = [r[:, off:off + n] for r in vres]
        if nm == "final_norm":
            parts = [p[0] for p in parts]
        vec_out[nm] = parts
        off += n

    loss = lax.psum(loss_part[0, 0], ("x", "y", "c"))
    order = ["ffn1_norm", "ffn1_w_gate", "ffn1_w_up", "ffn1_w_down", "mix_norm", "w_in", "ret_norm_g", "hgrn_lb_logits",
             "hgrn_norm_g", "w_out", "ffn2_norm", "ffn2_w_gate", "ffn2_w_up", "ffn2_w_down", "final_norm"]
    res = {**mat_out, **vec_out}
    outs = [loss, dx0[None]]
    for kind in range(4):
        outs += [res[nm][kind] for nm in order]
    return tuple(outs)
```

```python
import functools

import jax
import jax.numpy as jnp
from jax import lax
from jax.experimental import pallas as pl
from jax.experimental.pallas import tpu as pltpu

BF = jnp.bfloat16
F32 = jnp.float32
MESH = pl.DeviceIdType.MESH
HBM_SPEC = pl.BlockSpec(memory_space=pltpu.HBM)

N_DEV = 8
LANE = 128
EPS = 1e-6
ROPE_BASE = 10000.0
RET_HEADS = 4
HGRN_HEADS = 8
RET_CHUNK = 128
HGRN_BLOCK = 16
FFN_RESIDUAL_WEIGHT = 0.5
ADAM_LR = 0.001
ADAM_B1 = 0.9
ADAM_B2 = 0.999
ADAM_EPS = 1e-08
ADAM_WD = 0.01
ADAM_STEP = 10
VMEM_LIMIT = 56 * 1024 * 1024


def _tile(n, pref, mult=8):
    t = min(pref, n)
    t -= t % mult
    while t >= mult:
        if n % t == 0:
            return t
        t -= mult
    return n


def _params(*sem):
    return pltpu.CompilerParams(dimension_semantics=sem, vmem_limit_bytes=VMEM_LIMIT)


class _Comm:
    def __init__(self, operands, out_shape, n_sems, start, finish, aliases=None):
        self.operands = list(operands)
        self.out_shape = list(out_shape)
        self.n_sems = n_sems
        self.start = start
        self.finish = finish
        self.aliases = dict(aliases or {})


def _launch(body, *, name, grid, in_specs, out_specs, out_shape, sem, args, scratch_shapes=(), comm=None):
    in_specs, out_specs, out_shape = list(in_specs), list(out_specs), list(out_shape)
    scratch_shapes = list(scratch_shapes)
    if comm is None:
        res = pl.pallas_call(body, name=name, grid=grid, in_specs=in_specs, out_specs=out_specs, out_shape=out_shape,
                             scratch_shapes=scratch_shapes, compiler_params=_params(*sem))(*args)
        return list(res), []
    n_in, n_out, n_scr = len(in_specs), len(out_specs), len(scratch_shapes)
    ci, co = len(comm.operands), len(comm.out_shape)

    def carrying(*refs):
        bounds = [0, n_in, n_in + ci, n_in + ci + n_out, n_in + ci + n_out + co, n_in + ci + n_out + co + n_scr]
        ins, cins, outs, couts, scr = [refs[a:b] for a, b in zip(bounds[:-1], bounds[1:])]
        send_sems, recv_sems = refs[bounds[-1]:]
        ids = [pl.program_id(k) for k in range(len(grid))]
        first = functools.reduce(jnp.logical_and, [i == 0 for i in ids])
        last = functools.reduce(jnp.logical_and, [i == g - 1 for i, g in zip(ids, grid)])

        @pl.when(first)
        def _():
            comm.start(cins, couts, send_sems, recv_sems)

        body(*ins, *outs, *scr)

        @pl.when(last)
        def _():
            comm.finish(cins, couts, send_sems, recv_sems)

    res = pl.pallas_call(
        carrying, name=name, grid=grid,
        in_specs=in_specs + [HBM_SPEC] * ci, out_specs=out_specs + [HBM_SPEC] * co,
        out_shape=out_shape + comm.out_shape,
        scratch_shapes=scratch_shapes + [pltpu.SemaphoreType.DMA((comm.n_sems,)), pltpu.SemaphoreType.DMA((comm.n_sems,))],
        input_output_aliases={n_in + a: n_out + b for a, b in comm.aliases.items()},
        compiler_params=_params(*(["arbitrary"] * len(grid))),
    )(*args, *comm.operands)
    return list(res[:n_out]), list(res[n_out:])


def _comm_only_call(comm, name):
    def body(*refs):
        ci, co = len(comm.operands), len(comm.out_shape)
        cins, couts = refs[:ci], refs[ci:ci + co]
        send_sems, recv_sems = refs[ci + co:]
        comm.start(cins, couts, send_sems, recv_sems)
        comm.finish(cins, couts, send_sems, recv_sems)

    return pl.pallas_call(
        body, name=name,
        in_specs=[HBM_SPEC] * len(comm.operands), out_specs=[HBM_SPEC] * len(comm.out_shape),
        out_shape=comm.out_shape,
        scratch_shapes=[pltpu.SemaphoreType.DMA((comm.n_sems,)), pltpu.SemaphoreType.DMA((comm.n_sems,))],
        input_output_aliases=comm.aliases,
    )(*comm.operands)


def _sigmoid(v):
    return 1.0 / (1.0 + jnp.exp(-v))


def _dot(a, b):
    return jnp.dot(a, b, preferred_element_type=F32)


def _dot_nt(a, b):
    return lax.dot_general(a, b, (((1,), (1,)), ((), ())), preferred_element_type=F32)


def _dot_tn(a, b):
    return lax.dot_general(a, b, (((0,), (0,)), ((), ())), preferred_element_type=F32)


def _rmsnorm_call(x, gain, name):
    s, d = x.shape
    tm = _tile(s, 512)

    def body(x_ref, g_ref, o_ref):
        xv = x_ref[...]
        r = lax.rsqrt(jnp.mean(xv * xv, axis=-1, keepdims=True) + EPS)
        o_ref[...] = (xv * r * g_ref[...]).astype(BF)

    return pl.pallas_call(
        body, name=name, grid=(s // tm,),
        in_specs=[pl.BlockSpec((tm, d), lambda i: (i, 0)), pl.BlockSpec((1, d), lambda i: (0, 0))],
        out_specs=pl.BlockSpec((tm, d), lambda i: (i, 0)),
        out_shape=jax.ShapeDtypeStruct((s, d), BF),
        compiler_params=_params("parallel"),
    )(x, gain)


def _ffn_up_call(h, wg, wu, name, comm=None):
    s, d = h.shape
    nj, _, k = wg.shape
    tm = _tile(s, 512)

    def body(h_ref, wg_ref, wu_ref, g_ref, u_ref, a_ref):
        hv = h_ref[...]
        g = _dot(hv, wg_ref[...])
        u = _dot(hv, wu_ref[...])
        g_ref[...] = g
        u_ref[...] = u
        a_ref[...] = (g * _sigmoid(g) * u).astype(BF)

    act = pl.BlockSpec((tm, k), lambda i, j: (i, j))
    wsp = pl.BlockSpec((None, d, k), lambda i, j: (j, 0, 0))
    return _launch(
        body, name=name, grid=(s // tm, nj),
        in_specs=[pl.BlockSpec((tm, d), lambda i, j: (i, 0)), wsp, wsp],
        out_specs=[act, act, act],
        out_shape=[jax.ShapeDtypeStruct((s, nj * k), F32), jax.ShapeDtypeStruct((s, nj * k), F32),
                   jax.ShapeDtypeStruct((s, nj * k), BF)],
        sem=("parallel", "arbitrary"), args=(h, wg, wu), comm=comm)


def _proj_call(h, w, name, comm=None):
    s, d = h.shape
    nj, _, k = w.shape
    tm = _tile(s, 512)

    def body(h_ref, w_ref, o_ref):
        o_ref[...] = _dot(h_ref[...], w_ref[...])

    return _launch(
        body, name=name, grid=(s // tm, nj),
        in_specs=[pl.BlockSpec((tm, d), lambda i, j: (i, 0)), pl.BlockSpec((None, d, k), lambda i, j: (j, 0, 0))],
        out_specs=[pl.BlockSpec((tm, k), lambda i, j: (i, j))],
        out_shape=[jax.ShapeDtypeStruct((s, nj * k), F32)],
        sem=("parallel", "arbitrary"), args=(h, w), comm=comm)


def _down_call(a, w, resid, gain, scale, name, comm=None):
    s = a.shape[0]
    nj, k, d = w.shape
    tm = _tile(s, 512)
    with_norm = gain is not None

    def body(*refs):
        if with_norm:
            a_ref, w_ref, r_ref, g_ref, x_ref, h_ref, acc = refs
        else:
            a_ref, w_ref, r_ref, x_ref, acc = refs
        j = pl.program_id(1)

        @pl.when(j == 0)
        def _():
            acc[...] = jnp.zeros_like(acc)

        acc[...] += _dot(a_ref[...], w_ref[...])

        @pl.when(j == nj - 1)
        def _():
            xn = r_ref[...] + (scale * acc[...])
            x_ref[...] = xn
            if with_norm:
                r = lax.rsqrt(jnp.mean(xn * xn, axis=-1, keepdims=True) + EPS)
                h_ref[...] = (xn * r * g_ref[...]).astype(BF)

    row = pl.BlockSpec((tm, d), lambda i, j: (i, 0))
    in_specs = [pl.BlockSpec((tm, k), lambda i, j: (i, j)), pl.BlockSpec((None, k, d), lambda i, j: (j, 0, 0)), row]
    args = [a, w, resid]
    out_specs = [row]
    out_shape = [jax.ShapeDtypeStruct((s, d), F32)]
    if with_norm:
        in_specs.append(pl.BlockSpec((1, d), lambda i, j: (0, 0)))
        args.append(gain)
        out_specs.append(row)
        out_shape.append(jax.ShapeDtypeStruct((s, d), BF))
    return _launch(
        body, name=name, grid=(s // tm, nj),
        in_specs=in_specs, out_specs=out_specs, out_shape=out_shape,
        scratch_shapes=[pltpu.VMEM((tm, d), F32)],
        sem=("parallel", "arbitrary"), args=args, comm=comm)


def _loss_call(x, target, gain, name):
    s, d = x.shape
    tm = _tile(s, 512)

    def body(x_ref, t_ref, g_ref, loss_ref, dx_ref, dxb_ref, dg_ref):
        i = pl.program_id(0)

        @pl.when(i == 0)
        def _():
            loss_ref[...] = jnp.zeros_like(loss_ref)
            dg_ref[...] = jnp.zeros_like(dg_ref)

        xv = x_ref[...]
        gv = g_ref[...]
        r = lax.rsqrt(jnp.mean(xv * xv, axis=-1, keepdims=True) + EPS)
        xhat = xv * r
        err = xhat * gv - t_ref[...]
        per_tok = jnp.mean(err * err, axis=-1, keepdims=True)
        loss_ref[...] += 0.5 * jnp.sum(per_tok, axis=0, keepdims=True)
        dout = err * (1.0 / d)
        dg_ref[...] += jnp.sum(dout * xhat, axis=0, keepdims=True)
        dxhat = dout * gv
        dx = r * (dxhat - xhat * jnp.mean(dxhat * xhat, axis=-1, keepdims=True))
        dx_ref[...] = dx
        dxb_ref[...] = dx.astype(BF)

    row = pl.BlockSpec((tm, d), lambda i: (i, 0))
    vec = pl.BlockSpec((1, d), lambda i: (0, 0))
    return pl.pallas_call(
        body, name=name, grid=(s // tm,),
        in_specs=[row, row, vec],
        out_specs=[pl.BlockSpec((1, 1), lambda i: (0, 0)), row, row, vec],
        out_shape=[jax.ShapeDtypeStruct((1, 1), F32), jax.ShapeDtypeStruct((s, d), F32),
                   jax.ShapeDtypeStruct((s, d), BF), jax.ShapeDtypeStruct((1, d), F32)],
        compiler_params=_params("arbitrary"),
    )(x, target, gain)


def _bwd_up_call(dy, wdt, g, u, scale, name, comm=None):
    s, d = dy.shape
    nj, _, k = wdt.shape
    tm = _tile(s, 512)

    def body(dy_ref, w_ref, g_ref, u_ref, dg_ref, du_ref):
        da = scale * _dot(dy_ref[...], w_ref[...])
        gv = g_ref[...]
        sig = _sigmoid(gv)
        du_ref[...] = (da * gv * sig).astype(BF)
        dg_ref[...] = (da * u_ref[...] * sig * (1.0 + gv * (1.0 - sig))).astype(BF)

    act = pl.BlockSpec((tm, k), lambda i, j: (i, j))
    return _launch(
        body, name=name, grid=(s // tm, nj),
        in_specs=[pl.BlockSpec((tm, d), lambda i, j: (i, 0)), pl.BlockSpec((None, d, k), lambda i, j: (j, 0, 0)), act, act],
        out_specs=[act, act],
        out_shape=[jax.ShapeDtypeStruct((s, nj * k), BF), jax.ShapeDtypeStruct((s, nj * k), BF)],
        sem=("parallel", "arbitrary"), args=(dy, wdt, g, u), comm=comm)


def _bwd_down_call(pairs, dres, xin, gain, name, comm=None):
    s, d = xin.shape
    nj, k, _ = pairs[0][1].shape
    npair = len(pairs)
    tm = _tile(s, 512)
    strip = _tile(tm, 128)

    def body(*refs):
        a_refs = refs[0:2 * npair:2]
        w_refs = refs[1:2 * npair:2]
        dres_ref, x_ref, g_ref, dx_ref, dxb_ref, dg_ref, acc = refs[2 * npair:]
        i = pl.program_id(0)
        j = pl.program_id(1)

        @pl.when(j == 0)
        def _():
            acc[...] = jnp.zeros_like(acc)

        @pl.when((i == 0) & (j == 0))
        def _():
            dg_ref[...] = jnp.zeros_like(dg_ref)

        for a_ref, w_ref in zip(a_refs, w_refs):
            acc[...] += _dot(a_ref[...], w_ref[...])

        @pl.when(j == nj - 1)
        def _():
            for r0 in range(0, tm, strip):
                rows = slice(r0, r0 + strip)
                xv = x_ref[rows, :]
                r = lax.rsqrt(jnp.mean(xv * xv, axis=-1, keepdims=True) + EPS)
                xhat = xv * r
                dh = acc[rows, :]
                dg_ref[...] += jnp.sum(dh * xhat, axis=0, keepdims=True)
                dxhat = dh * g_ref[...]
                dx = dres_ref[rows, :] + r * (dxhat - xhat * jnp.mean(dxhat * xhat, axis=-1, keepdims=True))
                dx_ref[rows, :] = dx
                dxb_ref[rows, :] = dx.astype(BF)

    row = pl.BlockSpec((tm, d), lambda i, j: (i, 0))
    vec = pl.BlockSpec((1, d), lambda i, j: (0, 0))
    in_specs, args = [], []
    for a, w in pairs:
        in_specs += [pl.BlockSpec((tm, k), lambda i, j: (i, j)), pl.BlockSpec((None, k, d), lambda i, j: (j, 0, 0))]
        args += [a, w]
    once = pl.BlockSpec((tm, d), lambda i, j: (i, 0), pipeline_mode=pl.Buffered(1))
    in_specs += [once, once, vec]
    args += [dres, xin, gain]
    return _launch(
        body, name=name, grid=(s // tm, nj),
        in_specs=in_specs, out_specs=[row, row, vec],
        out_shape=[jax.ShapeDtypeStruct((s, d), F32), jax.ShapeDtypeStruct((s, d), BF), jax.ShapeDtypeStruct((1, d), F32)],
        scratch_shapes=[pltpu.VMEM((tm, d), F32)],
        sem=("arbitrary", "arbitrary"), args=args, comm=comm)


def _wgrad_call(a, b, nj, a_blocked, scale, name, comm=None):
    s = a.shape[0]
    ka = a.shape[1] // nj if a_blocked else a.shape[1]
    kb = b.shape[1] if a_blocked else b.shape[1] // nj
    ts = _tile(s, 1024)
    ns = s // ts

    def body(a_ref, b_ref, o_ref):
        t = pl.program_id(1)

        @pl.when(t == 0)
        def _():
            o_ref[...] = jnp.zeros_like(o_ref)

        o_ref[...] += _dot_tn(a_ref[...], b_ref[...])
        if scale != 1.0:
            @pl.when(t == ns - 1)
            def _():
                o_ref[...] = o_ref[...] * scale

    a_spec = pl.BlockSpec((ts, ka), (lambda j, t: (t, j)) if a_blocked else (lambda j, t: (t, 0)))
    b_spec = pl.BlockSpec((ts, kb), (lambda j, t: (t, 0)) if a_blocked else (lambda j, t: (t, j)))
    (out,), landed = _launch(
        body, name=name, grid=(nj, ns),
        in_specs=[a_spec, b_spec],
        out_specs=[pl.BlockSpec((None, ka, kb), lambda j, t: (j, 0, 0))],
        out_shape=[jax.ShapeDtypeStruct((nj, ka, kb), F32)],
        sem=("parallel", "arbitrary"), args=(a, b), comm=comm)
    return out if comm is None else (out, landed)


def _rope(v, cos, sin):
    half = v.shape[-1] // 2
    v1, v2 = v[:, :half], v[:, half:]
    return jnp.concatenate([v1 * cos - v2 * sin, v2 * cos + v1 * sin], axis=-1)


def _rope_bwd(dv, cos, sin):
    half = dv.shape[-1] // 2
    d1, d2 = dv[:, :half], dv[:, half:]
    return jnp.concatenate([d1 * cos + d2 * sin, d2 * cos - d1 * sin], axis=-1)


def _ret_consts(hd):
    c = RET_CHUNK
    log_gamma = jnp.log(1.0 - jnp.exp2(-5.0 - jnp.arange(RET_HEADS, dtype=F32)))
    idx = jnp.arange(c, dtype=F32)
    rel = idx[:, None] - idx[None, :]
    mask = rel >= 0
    decay = jnp.where(mask[None], jnp.exp(log_gamma[:, None, None] * jnp.where(mask, rel, 0.0)[None]), 0.0)
    qdec = jnp.exp(log_gamma[:, None] * (idx + 1.0)[None, :])
    kdec = jnp.exp(log_gamma[:, None] * (c - 1.0 - idx)[None, :])
    gchunk = jnp.exp(log_gamma * c)
    bc = lambda t: jnp.broadcast_to(t[:, :, None], (RET_HEADS, t.shape[1], hd))
    return decay, bc(qdec), bc(kdec), bc(gchunk[:, None])


def _rope_tables(s, hd):
    inv = jnp.power(ROPE_BASE, -jnp.arange(0, hd, 2, dtype=F32) / hd)
    ang = jnp.arange(s, dtype=F32)[:, None] * inv[None, :]
    return jnp.cos(ang), jnp.sin(ang)


def _ret_fwd_call(proj, cos, sin, consts, gret, name):
    s = proj.shape[0]
    w = proj.shape[1] // 8
    hd = w // RET_HEADS
    c = RET_CHUNK
    tt = _tile(s, 512, c)
    nc = tt // c
    decay, qdec, kdec, gch = consts
    scale = hd ** -0.5

    def body(q_ref, k_ref, v_ref, gate_ref, cos_ref, sin_ref, dec_ref, qd_ref, kd_ref, gc_ref, gn_ref,
             o_ref, m_ref, st_ref, state):
        @pl.when(pl.program_id(1) == 0)
        def _():
            state[...] = jnp.zeros_like(state)

        dec = dec_ref[...]
        for ci in range(nc):
            rows = slice(ci * c, (ci + 1) * c)
            cs, sn = cos_ref[rows, :], sin_ref[rows, :]
            q = _rope(q_ref[rows, :], cs, sn) * scale
            k = _rope(k_ref[rows, :], cs, sn)
            vb = v_ref[rows, :].astype(BF)
            sc = _dot_nt(q.astype(BF), k.astype(BF)) * dec
            prev = state[...]
            st_ref[ci] = prev
            o = _dot(sc.astype(BF), vb) + _dot((q * qd_ref[...]).astype(BF), prev.astype(BF))
            state[...] = gc_ref[...] * prev + _dot_tn((k * kd_ref[...]).astype(BF), vb)
            o_ref[rows, :] = o
            mu = jnp.mean(o, axis=-1, keepdims=True)
            cen = o - mu
            xhat = cen * lax.rsqrt(jnp.mean(cen * cen, axis=-1, keepdims=True) + EPS)
            gt = gate_ref[rows, :]
            m_ref[rows, :] = (xhat * gn_ref[...] * (gt * _sigmoid(gt))).astype(BF)

    nh = RET_HEADS
    comp = lambda j: pl.BlockSpec((tt, hd), lambda h, t, j=j: (t, j * nh + h))
    tab = pl.BlockSpec((tt, hd // 2), lambda h, t: (t, 0))
    per_head = lambda r: pl.BlockSpec((None, r, hd), lambda h, t: (h, 0, 0))
    return pl.pallas_call(
        body, name=name, grid=(nh, s // tt),
        in_specs=[comp(0), comp(1), comp(2), comp(3), tab, tab,
                  pl.BlockSpec((None, c, c), lambda h, t: (h, 0, 0)), per_head(c), per_head(c), per_head(1),
                  pl.BlockSpec((1, hd), lambda h, t: (0, h))],
        out_specs=[pl.BlockSpec((tt, hd), lambda h, t: (t, h)), pl.BlockSpec((tt, hd), lambda h, t: (t, h)),
                   pl.BlockSpec((None, nc, hd, hd), lambda h, t: (h, t, 0, 0))],
        out_shape=[jax.ShapeDtypeStruct((s, w), F32), jax.ShapeDtypeStruct((s, w), BF),
                   jax.ShapeDtypeStruct((nh, s // c, hd, hd), F32)],
        scratch_shapes=[pltpu.VMEM((hd, hd), F32)],
        compiler_params=_params("parallel", "arbitrary"),
    )(proj, proj, proj, proj, cos, sin, decay, qdec, kdec, gch, gret)


def _ret_bwd_call(proj, cos, sin, consts, gret, o_raw, states, dmerged, name):
    s = proj.shape[0]
    w = proj.shape[1] // 8
    hd = w // RET_HEADS
    c = RET_CHUNK
    tt = _tile(s, 512, c)
    nc = tt // c
    nt = s // tt
    decay, qdec, kdec, gch = consts
    scale = hd ** -0.5

    def body(q_ref, k_ref, v_ref, gate_ref, cos_ref, sin_ref, dec_ref, qd_ref, kd_ref, gc_ref, gn_ref,
             o_ref, st_ref, dm_ref, dq_ref, dk_ref, dv_ref, dgate_ref, dgn_ref, dstate):
        @pl.when(pl.program_id(1) == 0)
        def _():
            dstate[...] = jnp.zeros_like(dstate)
            dgn_ref[...] = jnp.zeros_like(dgn_ref)

        dec = dec_ref[...]
        gn = gn_ref[...]
        for ci in reversed(range(nc)):
            rows = slice(ci * c, (ci + 1) * c)
            cs, sn = cos_ref[rows, :], sin_ref[rows, :]
            q = _rope(q_ref[rows, :], cs, sn) * scale
            k = _rope(k_ref[rows, :], cs, sn)
            qb, kb = q.astype(BF), k.astype(BF)
            vb = v_ref[rows, :].astype(BF)
            sc = _dot_nt(qb, kb) * dec
            o = o_ref[rows, :]
            mu = jnp.mean(o, axis=-1, keepdims=True)
            cen = o - mu
            rstd = lax.rsqrt(jnp.mean(cen * cen, axis=-1, keepdims=True) + EPS)
            xhat = cen * rstd
            gt = gate_ref[rows, :]
            sig = _sigmoid(gt)
            sg = gt * sig
            dm = dm_ref[rows, :]
            dgn_ref[...] += jnp.sum(dm * xhat * sg, axis=0, keepdims=True)
            dgate_ref[rows, :] = (dm * xhat * gn * sig * (1.0 + gt * (1.0 - sig))).astype(BF)
            dxhat = dm * gn * sg
            do = rstd * (dxhat - jnp.mean(dxhat, axis=-1, keepdims=True)
                         - xhat * jnp.mean(dxhat * xhat, axis=-1, keepdims=True))
            dob = do.astype(BF)
            prev = st_ref[ci]
            ds = dstate[...]
            dsb = ds.astype(BF)
            dsc = (_dot_nt(dob, vb) * dec).astype(BF)
            dq = _dot(dsc, kb) + _dot_nt(dob, prev.astype(BF)) * qd_ref[...]
            dk = _dot_tn(dsc, qb) + _dot_nt(vb, dsb) * kd_ref[...]
            dv = _dot_tn(sc.astype(BF), dob) + _dot((k * kd_ref[...]).astype(BF), dsb)
            dstate[...] = gc_ref[...] * ds + _dot_tn((q * qd_ref[...]).astype(BF), dob)
            dq_ref[rows, :] = _rope_bwd(dq * scale, cs, sn).astype(BF)
            dk_ref[rows, :] = _rope_bwd(dk, cs, sn).astype(BF)
            dv_ref[rows, :] = dv.astype(BF)

    nh = RET_HEADS
    rev = lambda t: nt - 1 - t
    comp = lambda j: pl.BlockSpec((tt, hd), lambda h, t, j=j: (rev(t), j * nh + h))
    tab = pl.BlockSpec((tt, hd // 2), lambda h, t: (rev(t), 0))
    per_head = lambda r: pl.BlockSpec((None, r, hd), lambda h, t: (h, 0, 0))
    head_cols = pl.BlockSpec((tt, hd), lambda h, t: (rev(t), h))
    gvec = pl.BlockSpec((1, hd), lambda h, t: (0, h))
    act = jax.ShapeDtypeStruct((s, w), BF)
    return pl.pallas_call(
        body, name=name, grid=(nh, nt),
        in_specs=[comp(0), comp(1), comp(2), comp(3), tab, tab,
                  pl.BlockSpec((None, c, c), lambda h, t: (h, 0, 0)), per_head(c), per_head(c), per_head(1), gvec,
                  head_cols, pl.BlockSpec((None, nc, hd, hd), lambda h, t: (h, rev(t), 0, 0)), head_cols],
        out_specs=[head_cols, head_cols, head_cols, head_cols, gvec],
        out_shape=[act, act, act, act, jax.ShapeDtypeStruct((1, w), F32)],
        scratch_shapes=[pltpu.VMEM((hd, hd), F32)],
        compiler_params=_params("parallel", "arbitrary"),
    )(proj, proj, proj, proj, cos, sin, decay, qdec, kdec, gch, gret, o_raw, states, dmerged)


def _block_tri(n, bs, upper):
    r = jnp.arange(n)[:, None]
    cidx = jnp.arange(n)[None, :]
    same = (r // bs) == (cidx // bs)
    return jnp.where(same & ((cidx >= r) if upper else (cidx <= r)), 1.0, 0.0).astype(F32)


def _dot_exact(a, b):
    return jnp.dot(a, b, preferred_element_type=F32, precision=lax.Precision.HIGHEST)


def _hgrn_gates(z, lbv):
    sz = _sigmoid(z)
    oml = 1.0 - lbv
    f = lbv + oml * sz
    key = oml * (1.0 - sz)
    return sz, f, key


def _hgrn_fwd_call(proj, lb_logits, ghg, name):
    s = proj.shape[0]
    w = proj.shape[1] // 8
    nh = HGRN_HEADS
    hd = w // nh
    bs = HGRN_BLOCK
    tt = _tile(s, 256, bs)
    nb = tt // bs

    def body(q_ref, z_ref, v_ref, gate_ref, lb_ref, gn_ref, tril_ref, o_ref, m_ref, st_ref, state, upd):
        @pl.when(pl.program_id(1) == 0)
        def _():
            state[...] = jnp.zeros_like(state)

        lbv = _sigmoid(lb_ref[...])
        _, f, key = _hgrn_gates(z_ref[...], lbv)
        qr = q_ref[...]
        q = qr * _sigmoid(qr)
        v = v_ref[...]
        g = _dot_exact(tril_ref[...], jnp.log(f))
        blocks = lambda t: t.reshape(nb, bs, hd)
        g3, q3, k3, v3 = blocks(g), blocks(q), blocks(key), blocks(v)
        glast3 = g3[:, bs - 1:bs, :]
        row_id = lax.broadcasted_iota(jnp.int32, (nb, bs, hd), 1)
        o3 = jnp.zeros((nb, bs, hd), F32)
        for j in range(bs):
            wj = jnp.where(row_id >= j, jnp.exp(jnp.minimum(g3 - g3[:, j:j + 1, :], 0.0)), 0.0)
            a = jnp.sum(q3 * k3[:, j:j + 1, :] * wj, axis=-1, keepdims=True)
            o3 = o3 + a * v3[:, j:j + 1, :]
        ktb = (k3 * jnp.exp(glast3 - g3)).reshape(tt, hd).astype(BF)
        vb = v.astype(BF)
        for b in range(nb):
            rows = slice(b * bs, (b + 1) * bs)
            upd[b] = _dot_tn(vb[rows, :], ktb[rows, :])
        egl3 = jnp.exp(glast3)
        st = state[...]
        for b in range(nb):
            st_ref[b] = st
            st = st * egl3[b] + upd[b]
        state[...] = st
        qgb = (q * jnp.exp(g)).astype(BF)
        o_intra = o3.reshape(tt, hd)
        gn = gn_ref[...]
        for b in range(nb):
            rows = slice(b * bs, (b + 1) * bs)
            o = o_intra[rows, :] + _dot_nt(qgb[rows, :], st_ref[b].astype(BF))
            o_ref[rows, :] = o
            gt = gate_ref[rows, :]
            xhat = o * lax.rsqrt(jnp.mean(o * o, axis=-1, keepdims=True) + EPS)
            m_ref[rows, :] = (xhat * gn * (gt * _sigmoid(gt))).astype(BF)

    comp = lambda j: pl.BlockSpec((tt, hd), lambda h, t, j=j: (t, j * nh + h))
    gvec = pl.BlockSpec((1, hd), lambda h, t: (0, h))
    head_cols = pl.BlockSpec((tt, hd), lambda h, t: (t, h))
    return pl.pallas_call(
        body, name=name, grid=(nh, s // tt),
        in_specs=[comp(4), comp(5), comp(6), comp(7), gvec, gvec, pl.BlockSpec((tt, tt), lambda h, t: (0, 0))],
        out_specs=[head_cols, head_cols, pl.BlockSpec((None, nb, hd, hd), lambda h, t: (h, t, 0, 0))],
        out_shape=[jax.ShapeDtypeStruct((s, w), F32), jax.ShapeDtypeStruct((s, w), BF),
                   jax.ShapeDtypeStruct((nh, s // bs, hd, hd), F32)],
        scratch_shapes=[pltpu.VMEM((hd, hd), F32), pltpu.VMEM((nb, hd, hd), F32)],
        compiler_params=_params("parallel", "arbitrary"),
    )(proj, proj, proj, proj, lb_logits, ghg, _block_tri(tt, bs, upper=False))


def _hgrn_bwd_call(proj, lb_logits, ghg, o_raw, states, dmerged, name, comm=None):
    s = proj.shape[0]
    w = proj.shape[1] // 8
    nh = HGRN_HEADS
    hd = w // nh
    bs = HGRN_BLOCK
    tt = _tile(s, 256, bs)
    nb = tt // bs
    nt = s // tt

    def body(q_ref, z_ref, v_ref, gate_ref, lb_ref, gn_ref, tril_ref, triu_ref, o_ref, st_ref, dm_ref,
             dq_ref, dz_ref, dv_ref, dgate_ref, dlb_ref, dgn_ref,
             dstate, ds_all, inc, dq_s, dk_s, dv_s, dgl_s, dk_rows, dv_rows):
        @pl.when(pl.program_id(1) == 0)
        def _():
            dstate[...] = jnp.zeros_like(dstate)
            dlb_ref[...] = jnp.zeros_like(dlb_ref)
            dgn_ref[...] = jnp.zeros_like(dgn_ref)

        lbv = _sigmoid(lb_ref[...])
        oml = 1.0 - lbv
        gn = gn_ref[...]
        sz, f, key = _hgrn_gates(z_ref[...], lbv)
        qr = q_ref[...]
        sq = _sigmoid(qr)
        q = qr * sq
        v = v_ref[...]
        g = _dot_exact(tril_ref[...], jnp.log(f))
        eg = jnp.exp(g)
        blocks = lambda t: t.reshape(nb, bs, hd)
        g3, q3, k3, v3 = blocks(g), blocks(q), blocks(key), blocks(v)
        glast3 = g3[:, bs - 1:bs, :]
        egl3 = jnp.exp(glast3)
        ktail3 = jnp.exp(glast3 - g3)
        o = o_ref[...]
        rstd = lax.rsqrt(jnp.mean(o * o, axis=-1, keepdims=True) + EPS)
        xhat = o * rstd
        gt = gate_ref[...]
        sig = _sigmoid(gt)
        sg = gt * sig
        dm = dm_ref[...]
        dgn_ref[...] += jnp.sum(dm * xhat * sg, axis=0, keepdims=True)
        dgate_ref[...] = (dm * xhat * gn * sig * (1.0 + gt * (1.0 - sig))).astype(BF)
        dxhat = dm * gn * sg
        do = rstd * (dxhat - xhat * jnp.mean(dxhat * xhat, axis=-1, keepdims=True))
        dob = do.astype(BF)
        do3 = blocks(do)
        qgb = (q * eg).astype(BF)
        for b in range(nb):
            rows = slice(b * bs, (b + 1) * bs)
            inc[b] = _dot_tn(dob[rows, :], qgb[rows, :])
        ds = dstate[...]
        for b in reversed(range(nb)):
            ds_all[b] = ds
            ds = ds * egl3[b] + inc[b]
        dstate[...] = ds
        ktb = (k3 * ktail3).reshape(tt, hd).astype(BF)
        vb = v.astype(BF)
        for b in range(nb):
            rows = slice(b * bs, (b + 1) * bs)
            prev = st_ref[b]
            dsb = ds_all[b]
            dsbb = dsb.astype(BF)
            dq_s[rows, :] = _dot(dob[rows, :], prev.astype(BF))
            dk_s[rows, :] = _dot(vb[rows, :], dsbb)
            dv_s[rows, :] = _dot_nt(ktb[rows, :], dsbb)
            dgl_s[b] = jnp.sum(prev * dsb, axis=0, keepdims=True)
        dq3 = blocks(dq_s[...] * eg)
        dk3 = blocks(dk_s[...]) * ktail3
        dg_last3 = jnp.sum(k3 * dk3, axis=1, keepdims=True) + egl3 * dgl_s[...]
        row_id = lax.broadcasted_iota(jnp.int32, (nb, bs, hd), 1)
        for j in range(bs):
            wj = jnp.where(row_id >= j, jnp.exp(jnp.minimum(g3 - g3[:, j:j + 1, :], 0.0)), 0.0)
            kj = k3[:, j:j + 1, :]
            a = jnp.sum(q3 * kj * wj, axis=-1, keepdims=True)
            da = jnp.sum(do3 * v3[:, j:j + 1, :], axis=-1, keepdims=True)
            dv_rows[:, j:j + 1, :] = jnp.sum(a * do3, axis=1, keepdims=True)
            dq3 = dq3 + da * kj * wj
            dk_rows[:, j:j + 1, :] = jnp.sum(da * q3 * wj, axis=1, keepdims=True)
        dk3 = dk3 + dk_rows[...]
        dv = dv_s[...] + dv_rows[...].reshape(tt, hd)
        dg3 = q3 * dq3 - k3 * dk3 + jnp.where(row_id == bs - 1, dg_last3, 0.0)
        dlf = _dot_exact(triu_ref[...], dg3.reshape(tt, hd))
        dk = dk3.reshape(tt, hd)
        dfk = dlf / f - dk
        dlb_ref[...] += jnp.sum(dfk * (1.0 - sz), axis=0, keepdims=True) * (lbv * oml)
        dz_ref[...] = (dfk * oml * sz * (1.0 - sz)).astype(BF)
        dq_ref[...] = (dq3.reshape(tt, hd) * sq * (1.0 + qr * (1.0 - sq))).astype(BF)
        dv_ref[...] = dv.astype(BF)

    rev = lambda t: nt - 1 - t
    comp = lambda j: pl.BlockSpec((tt, hd), lambda h, t, j=j: (rev(t), j * nh + h))
    gvec = pl.BlockSpec((1, hd), lambda h, t: (0, h))
    head_cols = pl.BlockSpec((tt, hd), lambda h, t: (rev(t), h))
    tri = pl.BlockSpec((tt, tt), lambda h, t: (0, 0))
    act = jax.ShapeDtypeStruct((s, w), BF)
    vec = jax.ShapeDtypeStruct((1, w), F32)
    tile_f32 = pltpu.VMEM((tt, hd), F32)
    return _launch(
        body, name=name, grid=(nh, nt),
        in_specs=[comp(4), comp(5), comp(6), comp(7), gvec, gvec, tri, tri, head_cols,
                  pl.BlockSpec((None, nb, hd, hd), lambda h, t: (h, rev(t), 0, 0)),
                  pl.BlockSpec((tt, hd), lambda h, t: (rev(t), nh + h))],
        out_specs=[head_cols, head_cols, head_cols, head_cols, gvec, gvec],
        out_shape=[act, act, act, act, vec, vec],
        scratch_shapes=[pltpu.VMEM((hd, hd), F32), pltpu.VMEM((nb, hd, hd), F32), pltpu.VMEM((nb, hd, hd), F32),
                        tile_f32, tile_f32, tile_f32, pltpu.VMEM((nb, 1, hd), F32),
                        pltpu.VMEM((nb, bs, hd), F32), pltpu.VMEM((nb, bs, hd), F32)],
        sem=("parallel", "arbitrary"),
        args=(proj, proj, proj, proj, lb_logits, ghg, _block_tri(tt, bs, upper=False), _block_tri(tt, bs, upper=True),
              o_raw, states, dmerged), comm=comm)


def _position():
    return lax.axis_index("x"), lax.axis_index("y"), lax.axis_index("c")


def _all_gather_call(shards, name):
    n = len(shards)

    def body(*refs):
        ins, outs = refs[:n], refs[n:2 * n]
        send_sems, recv_sems, local_sems = refs[2 * n:]
        x, y, c = _position()
        me, sibling = (x, y, c), (x, y, 1 - c)
        chips = [(1 - x, y), (x, 1 - y), (1 - x, 1 - y)]

        def slot(a, p):
            return outs[a].at[4 * p[0] + 2 * p[1] + p[2]]

        def copy(a, k, block, to, src=None):
            return pltpu.make_async_remote_copy(
                src_ref=slot(a, block) if src is None else src, dst_ref=slot(a, block),
                send_sem=send_sems.at[a * 7 + k], recv_sem=recv_sems.at[a * 7 + k],
                device_id=to, device_id_type=MESH)

        mine = [pltpu.make_async_copy(ins[a], slot(a, me), local_sems.at[a]) for a in range(n)]
        for cp in mine:
            cp.start()
        first = []
        for a in range(n):
            first.append(copy(a, 0, me, sibling, src=ins[a]))
            first += [copy(a, 1 + j, me, (*chip, c), src=ins[a]) for j, chip in enumerate(chips)]
        for cp in first:
            cp.start()
        passed = []
        for j, chip in enumerate(chips):
            for a in range(n):
                copy(a, 1 + j, (*chip, c), me).wait_recv()
                fwd = copy(a, 4 + j, (*chip, c), sibling)
                fwd.start()
                passed.append(fwd)
        for a in range(n):
            copy(a, 0, sibling, me).wait_recv()
            for j, chip in enumerate(chips):
                copy(a, 4 + j, (*chip, 1 - c), me).wait_recv()
        for cp in first + passed:
            cp.wait_send()
        for cp in mine:
            cp.wait()

    return pl.pallas_call(
        body, name=name,
        in_specs=[HBM_SPEC] * n, out_specs=[HBM_SPEC] * n,
        out_shape=[jax.ShapeDtypeStruct((N_DEV,) + t.shape, t.dtype) for t in shards],
        scratch_shapes=[pltpu.SemaphoreType.DMA((7 * n,)), pltpu.SemaphoreType.DMA((7 * n,)),
                        pltpu.SemaphoreType.DMA((n,))],
    )(*shards)


def _slot(ref, p):
    return ref.at[4 * p[0] + 2 * p[1] + p[2]]


def _gather_round1(shards):
    n = len(shards)

    def plan(ins, outs, send_sems, recv_sems):
        x, y, c = _position()
        me = (x, y, c)
        peers = [(x, y, 1 - c), (1 - x, y, c), (x, 1 - y, c), (1 - x, 1 - y, c)]
        sends, recvs, local = [], [], []
        for a in range(n):
            local.append(pltpu.make_async_copy(ins[a], _slot(outs[a], me), send_sems.at[4 * n + a]))
            for k, peer in enumerate(peers):
                sems = dict(send_sem=send_sems.at[4 * a + k], recv_sem=recv_sems.at[4 * a + k],
                            device_id=peer, device_id_type=MESH)
                sends.append(pltpu.make_async_remote_copy(src_ref=ins[a], dst_ref=_slot(outs[a], me), **sems))
                recvs.append(pltpu.make_async_remote_copy(src_ref=ins[a], dst_ref=_slot(outs[a], peer), **sems))
        return sends, recvs, local

    def start(*refs):
        sends, _, local = plan(*refs)
        for cp in local + sends:
            cp.start()

    def finish(*refs):
        sends, recvs, local = plan(*refs)
        for cp in recvs:
            cp.wait_recv()
        for cp in sends:
            cp.wait_send()
        for cp in local:
            cp.wait()

    return _Comm(shards, [jax.ShapeDtypeStruct((N_DEV,) + t.shape, t.dtype) for t in shards], 5 * n, start, finish)


def _gather_round2(gathered):
    n = len(gathered)

    def plan(ins, outs, send_sems, recv_sems):
        x, y, c = _position()
        chips = [(1 - x, y), (x, 1 - y), (1 - x, 1 - y)]
        sends, recvs = [], []
        for a in range(n):
            for k, chip in enumerate(chips):
                sems = dict(send_sem=send_sems.at[3 * a + k], recv_sem=recv_sems.at[3 * a + k],
                            device_id=(x, y, 1 - c), device_id_type=MESH)
                sends.append(pltpu.make_async_remote_copy(
                    src_ref=_slot(ins[a], (*chip, c)), dst_ref=_slot(outs[a], (*chip, c)), **sems))
                recvs.append(pltpu.make_async_remote_copy(
                    src_ref=_slot(ins[a], (*chip, c)), dst_ref=_slot(outs[a], (*chip, 1 - c)), **sems))
        return sends, recvs

    def start(*refs):
        for cp in plan(*refs)[0]:
            cp.start()

    def finish(*refs):
        sends, recvs = plan(*refs)
        for cp in recvs:
            cp.wait_recv()
        for cp in sends:
            cp.wait_send()

    return _Comm(gathered, [jax.ShapeDtypeStruct(t.shape, t.dtype) for t in gathered], 3 * n, start, finish,
                 aliases={a: a for a in range(n)})


def _sibling_exchange(grads):
    n = len(grads)

    def plan(ins, outs, send_sems, recv_sems):
        x, y, c = _position()
        return [pltpu.make_async_remote_copy(
            src_ref=ins[a].at[2 * q + (1 - c)], dst_ref=outs[a].at[q],
            send_sem=send_sems.at[a * 4 + q], recv_sem=recv_sems.at[a * 4 + q],
            device_id=(x, y, 1 - c), device_id_type=MESH) for a in range(n) for q in range(4)]

    def start(*refs):
        for cp in plan(*refs):
            cp.start()

    def finish(*refs):
        for cp in plan(*refs):
            cp.wait()

    return _Comm(grads, [jax.ShapeDtypeStruct((4,) + t.shape[1:], t.dtype) for t in grads], 4 * n, start, finish)


def _chip_exchange(partials):
    n = len(partials)

    def plan(ins, outs, send_sems, recv_sems):
        x, y, c = _position()
        chips = [(1 - x, y), (x, 1 - y), (1 - x, 1 - y)]
        return [pltpu.make_async_remote_copy(
            src_ref=ins[a].at[2 * chip[0] + chip[1]], dst_ref=outs[a].at[k],
            send_sem=send_sems.at[a * 3 + k], recv_sem=recv_sems.at[a * 3 + k],
            device_id=(*chip, c), device_id_type=MESH) for a in range(n) for k, chip in enumerate(chips)]

    def start(*refs):
        for cp in plan(*refs):
            cp.start()

    def finish(*refs):
        for cp in plan(*refs):
            cp.wait()

    return _Comm(partials, [jax.ShapeDtypeStruct((3,) + t.shape[1:], t.dtype) for t in partials], 3 * n, start, finish)


class _SemWindow:
    def __init__(self, sems, offset):
        self._sems, self._offset = sems, offset

    @property
    def at(self):
        return self

    def __getitem__(self, i):
        return self._sems.at[self._offset + i]


def _join(parts):
    def each(fn_name, cins, couts, send_sems, recv_sems):
        i = o = sem = 0
        for p in parts:
            ni, no = len(p.operands), len(p.out_shape)
            getattr(p, fn_name)(cins[i:i + ni], couts[o:o + no], _SemWindow(send_sems, sem), _SemWindow(recv_sems, sem))
            i, o, sem = i + ni, o + no, sem + p.n_sems

    assert not any(p.aliases for p in parts)
    return _Comm([t for p in parts for t in p.operands], [t for p in parts for t in p.out_shape],
                 sum(p.n_sems for p in parts), functools.partial(each, "start"), functools.partial(each, "finish"))


def _pair_sum_call(grad, recv, parity, name):
    _, r, ccols = grad.shape
    tr = _tile(r, 256)

    def body(par_ref, g_ref, r_ref, p_ref, pb_ref):
        del par_ref
        p = g_ref[...] + r_ref[...]
        p_ref[...] = p
        pb_ref[...] = p.astype(BF)

    blk = lambda fn: pl.BlockSpec((None, tr, ccols), fn)
    return pl.pallas_call(
        body, name=name,
        grid_spec=pltpu.PrefetchScalarGridSpec(
            num_scalar_prefetch=1, grid=(4, r // tr),
            in_specs=[blk(lambda q, i, par: (2 * q + par[0], i, 0)), blk(lambda q, i, par: (q, i, 0))],
            out_specs=[blk(lambda q, i, par: (q, i, 0)), blk(lambda q, i, par: (q, i, 0))]),
        out_shape=[jax.ShapeDtypeStruct((4, r, ccols), F32), jax.ShapeDtypeStruct((4, r, ccols), BF)],
        compiler_params=_params("parallel", "parallel"),
    )(parity, grad, recv)


def _adamw_math(w, g, m, v):
    m = ADAM_B1 * m + (1.0 - ADAM_B1) * g
    v = ADAM_B2 * v + (1.0 - ADAM_B2) * (g * g)
    m_hat = m / (1.0 - ADAM_B1 ** ADAM_STEP)
    v_hat = v / (1.0 - ADAM_B2 ** ADAM_STEP)
    delta = -ADAM_LR * (m_hat / (jnp.sqrt(v_hat) + ADAM_EPS) + ADAM_WD * w)
    return delta, m, v


def _adamw_matrix_call(partial, recv, chip, w, m, v, name):
    r, ccols = w.shape
    gcols = partial.shape[2]
    tr = _tile(r, 256)

    def body(chip_ref, p_ref, r_ref, w_ref, m_ref, v_ref, g_out, d_out, m_out, v_out):
        del chip_ref
        cols = pl.ds(0, ccols)
        g = (p_ref[:, cols] + r_ref[0, :, cols].astype(F32) + r_ref[1, :, cols].astype(F32)
             + r_ref[2, :, cols].astype(F32))
        delta, mn, vn = _adamw_math(w_ref[...], g, m_ref[...], v_ref[...])
        g_out[...] = g
        d_out[...] = delta
        m_out[...] = mn
        v_out[...] = vn

    mat = pl.BlockSpec((tr, ccols), lambda i, ch: (i, 0))
    shp = jax.ShapeDtypeStruct((r, ccols), F32)
    return pl.pallas_call(
        body, name=name,
        grid_spec=pltpu.PrefetchScalarGridSpec(
            num_scalar_prefetch=1, grid=(r // tr,),
            in_specs=[pl.BlockSpec((None, tr, gcols), lambda i, ch: (ch[0], i, 0)),
                      pl.BlockSpec((3, tr, gcols), lambda i, ch: (0, i, 0)), mat, mat, mat],
            out_specs=[mat, mat, mat, mat]),
        out_shape=[shp, shp, shp, shp],
        compiler_params=_params("parallel"),
    )(chip, partial, recv, w, m, v)


def _adamw_vector_call(gathered, w, m, v, name):
    n = w.shape[1]

    def body(p_ref, w_ref, m_ref, v_ref, g_out, d_out, m_out, v_out):
        g = p_ref[0:1, :]
        for k in range(1, N_DEV):
            g = g + p_ref[k:k + 1, :]
        delta, mn, vn = _adamw_math(w_ref[...], g, m_ref[...], v_ref[...])
        g_out[...] = g
        d_out[...] = delta
        m_out[...] = mn
        v_out[...] = vn

    shp = jax.ShapeDtypeStruct((1, n), F32)
    return pl.pallas_call(body, name=name, out_shape=[shp, shp, shp, shp])(gathered, w, m, v)


def _round_up(n, mult):
    return (n + mult - 1) // mult * mult


def kernel(x, ffn1_norm, ffn1_w_gate, ffn1_w_up, ffn1_w_down, mix_norm, w_in, ret_norm_g, hgrn_lb_logits, hgrn_norm_g, w_out, ffn2_norm, ffn2_w_gate, ffn2_w_up, ffn2_w_down, final_norm, loss_target, m_ffn1_norm, m_ffn1_w_gate, m_ffn1_w_up, m_ffn1_w_down, m_mix_norm, m_w_in, m_ret_norm_g, m_hgrn_lb_logits, m_hgrn_norm_g, m_w_out, m_ffn2_norm, m_ffn2_w_gate, m_ffn2_w_up, m_ffn2_w_down, m_final_norm, v_ffn1_norm, v_ffn1_w_gate, v_ffn1_w_up, v_ffn1_w_down, v_mix_norm, v_w_in, v_ret_norm_g, v_hgrn_lb_logits, v_hgrn_norm_g, v_w_out, v_ffn2_norm, v_ffn2_w_gate, v_ffn2_w_up, v_ffn2_w_down, v_final_norm):
    xs = x[0]
    target = loss_target[0]
    s, d = xs.shape
    f_loc = ffn1_w_gate.shape[2]
    fp = _round_up(f_loc, LANE)
    pad_cols = lambda t: jnp.pad(t[0], ((0, 0), (0, fp - f_loc)))
    pad_rows = lambda t: jnp.pad(t[0], ((0, fp - f_loc), (0, 0)))

    mat_names = ["ffn1_w_gate", "ffn1_w_up", "ffn1_w_down", "w_in", "w_out", "ffn2_w_gate", "ffn2_w_up", "ffn2_w_down"]
    mat_pad = [pad_cols, pad_cols, pad_rows, lambda t: t[0], lambda t: t[0], pad_cols, pad_cols, pad_rows]
    mat_w = [ffn1_w_gate, ffn1_w_up, ffn1_w_down, w_in, w_out, ffn2_w_gate, ffn2_w_up, ffn2_w_down]
    mat_m = [m_ffn1_w_gate, m_ffn1_w_up, m_ffn1_w_down, m_w_in, m_w_out, m_ffn2_w_gate, m_ffn2_w_up, m_ffn2_w_down]
    mat_v = [v_ffn1_w_gate, v_ffn1_w_up, v_ffn1_w_down, v_w_in, v_w_out, v_ffn2_w_gate, v_ffn2_w_up, v_ffn2_w_down]

    cx, cy, cc = _position()
    parity = jnp.reshape(cc, (1,)).astype(jnp.int32)
    chip = jnp.reshape(2 * cx + cy, (1,)).astype(jnp.int32)
    mat_index = {nm: i for i, nm in enumerate(mat_names)}
    mat_out = {}

    def pair_sums(names, grads, from_sibling):
        return [_pair_sum_call(g, r, parity, "pair_sum_" + nm) for nm, g, r in zip(names, grads, from_sibling)]

    def update(names, sums, from_chips):
        for nm, (p, _), r in zip(names, sums, from_chips):
            i = mat_index[nm]
            res = _adamw_matrix_call(p, r, chip, mat_w[i][0], mat_m[i][0], mat_v[i][0], "adamw_" + nm)
            mat_out[nm] = [t[None] for t in res]

    shards = [p(t.astype(BF)) for p, t in zip(mat_pad, mat_w)]
    wg1, wu1, wd1 = _all_gather_call(shards[:3], "gather_ffn1")

    h1 = _rmsnorm_call(xs, ffn1_norm, "ffn1_norm")
    (g1, u1, a1), landed = _ffn_up_call(h1, wg1, wu1, "ffn1_up", comm=_gather_round1(shards[3:]))
    (x1, h2), (win, wout, wg2, wu2, wd2) = _down_call(a1, wd1, xs, mix_norm, FFN_RESIDUAL_WEIGHT, "ffn1_down",
                                                      comm=_gather_round2(landed))
    (proj,), _ = _proj_call(h2, win, "mix_in")
    wmix = proj.shape[1] // 8
    cos, sin = _rope_tables(s, wmix // RET_HEADS)
    consts = _ret_consts(wmix // RET_HEADS)
    o_ret, m_ret, st_ret = _ret_fwd_call(proj, cos, sin, consts, ret_norm_g, "ret_fwd")
    o_hg, m_hg, st_hg = _hgrn_fwd_call(proj, hgrn_lb_logits, hgrn_norm_g, "hgrn_fwd")
    merged = jnp.concatenate([m_ret, m_hg], axis=1)
    wout_wide = wout.reshape(2, wout.shape[0] * wout.shape[1] // 2, d)
    (x2, h3), _ = _down_call(merged, wout_wide, x1, ffn2_norm, 1.0, "mix_out")
    (g2, u2, a2), _ = _ffn_up_call(h3, wg2, wu2, "ffn2_up")
    (x3,), _ = _down_call(a2, wd2, x2, None, FFN_RESIDUAL_WEIGHT, "ffn2_down")
    loss_part, dx3, dx3b, gv_final = _loss_call(x3, target, final_norm[None, :], "loss_head")

    tr = lambda t: jnp.swapaxes(t, 1, 2)
    (dg2, du2), _ = _bwd_up_call(dx3b, tr(wd2), g2, u2, FFN_RESIDUAL_WEIGHT, "ffn2_bwd_up")
    (dx2, dx2b, gv_n3), _ = _bwd_down_call([(dg2, tr(wg2)), (du2, tr(wu2))], dx3, x2, ffn2_norm, "ffn2_bwd_down")
    names_a = ["ffn2_w_gate", "ffn2_w_up", "ffn2_w_down"]
    grads_a = [_wgrad_call(h3, dg2, N_DEV, False, 1.0, "ffn2_wgrad_gate"),
               _wgrad_call(h3, du2, N_DEV, False, 1.0, "ffn2_wgrad_up"),
               _wgrad_call(a2, dx3b, N_DEV, True, FFN_RESIDUAL_WEIGHT, "ffn2_wgrad_down")]

    (dmerged,), sib_a = _proj_call(dx2b, tr(wout_wide), "mix_out_bwd", comm=_sibling_exchange(grads_a))
    gm_out = _wgrad_call(merged, dx2b, 2, True, 1.0, "mix_out_wgrad").reshape(wout.shape)
    sums_a = pair_sums(names_a, grads_a, sib_a)
    drq, drk, drv, drg, gv_ret = _ret_bwd_call(proj, cos, sin, consts, ret_norm_g, o_ret, st_ret, dmerged, "ret_bwd")
    (dhq, dhf, dhi, dhg, gv_lb, gv_hg), chips_a = _hgrn_bwd_call(
        proj, hgrn_lb_logits, hgrn_norm_g, o_hg, st_hg, dmerged, "hgrn_bwd",
        comm=_chip_exchange([pb for _, pb in sums_a]))
    update(names_a, sums_a, chips_a)
    dproj = jnp.concatenate([drq, drk, drv, drg, dhq, dhf, dhi, dhg], axis=1)
    (dx1, dx1b, gv_n2), _ = _bwd_down_call([(dproj, tr(win))], dx2, x1, mix_norm, "mix_in_bwd")
    names_b = ["w_out", "w_in", "ffn1_w_down"]
    grads_b = [gm_out, _wgrad_call(h2, dproj, N_DEV, False, 1.0, "mix_in_wgrad"),
               _wgrad_call(a1, dx1b, N_DEV, True, FFN_RESIDUAL_WEIGHT, "ffn1_wgrad_down")]

    (dg1, du1), sib_b = _bwd_up_call(dx1b, tr(wd1), g1, u1, FFN_RESIDUAL_WEIGHT, "ffn1_bwd_up",
                                     comm=_sibling_exchange(grads_b))
    sums_b = pair_sums(names_b, grads_b, sib_b)
    gm_g1, chips_b = _wgrad_call(h1, dg1, N_DEV, False, 1.0, "ffn1_wgrad_gate",
                                 comm=_chip_exchange([pb for _, pb in sums_b]))
    update(names_b, sums_b, chips_b)
    gm_u1, sib_g = _wgrad_call(h1, du1, N_DEV, False, 1.0, "ffn1_wgrad_up", comm=_sibling_exchange([gm_g1]))
    sums_g = pair_sums(["ffn1_w_gate"], [gm_g1], sib_g)
    (dx0, _, gv_n1), landed = _bwd_down_call(
        [(dg1, tr(wg1)), (du1, tr(wu1))], dx1, xs, ffn1_norm, "ffn1_bwd_down",
        comm=_join([_chip_exchange([sums_g[0][1]]), _sibling_exchange([gm_u1])]))
    update(["ffn1_w_gate"], sums_g, landed[:1])
    sums_u = pair_sums(["ffn1_w_up"], [gm_u1], landed[1:])
    update(["ffn1_w_up"], sums_u, _comm_only_call(_chip_exchange([sums_u[0][1]]), "ffn1_up_grads_to_chips"))

    vec_names = ["ffn1_norm", "mix_norm", "ret_norm_g", "hgrn_lb_logits", "hgrn_norm_g", "ffn2_norm", "final_norm"]
    vec_g = [gv_n1, gv_n2, gv_ret, gv_lb, gv_hg, gv_n3, gv_final]
    vec_w = [ffn1_norm, mix_norm, ret_norm_g, hgrn_lb_logits, hgrn_norm_g, ffn2_norm, final_norm[None, :]]
    vec_m = [m_ffn1_norm, m_mix_norm, m_ret_norm_g, m_hgrn_lb_logits, m_hgrn_norm_g, m_ffn2_norm, m_final_norm[None, :]]
    vec_v = [v_ffn1_norm, v_mix_norm, v_ret_norm_g, v_hgrn_lb_logits, v_hgrn_norm_g, v_ffn2_norm, v_final_norm[None, :]]
    cat = lambda ts: jnp.concatenate(ts, axis=1)
    (vec_all,) = _all_gather_call([cat(vec_g)], "gather_vector_grads")
    vres = _adamw_vector_call(vec_all[:, 0, :], cat(vec_w), cat(vec_m), cat(vec_v), "adamw_vectors")
    vec_out = {}
    off = 0
    for nm, t in zip(vec_names, vec_w):
        n = t.shape[1]
        parts = [r[:, off:off + n] for r in vres]
        if nm == "final_norm":
            parts = [p[0] for p in parts]
        vec_out[nm] = parts
        off += n

    loss = lax.psum(loss_part[0, 0], ("x", "y", "c"))
    order = ["ffn1_norm", "ffn1_w_gate", "ffn1_w_up", "ffn1_w_down", "mix_norm", "w_in", "ret_norm_g", "hgrn_lb_logits",
             "hgrn_norm_g", "w_out", "ffn2_norm", "ffn2_w_gate", "ffn2_w_up", "ffn2_w_down", "final_norm"]
    res = {**mat_out, **vec_out}
    outs = [loss, dx0[None]]
    for kind in range(4):
        outs += [res[nm][kind] for nm in order]
    return tuple(outs)
```

```python
import functools

import jax
import jax.numpy as jnp
from jax import lax
from jax.experimental import pallas as pl
from jax.experimental.pallas import tpu as pltpu

BF = jnp.bfloat16
F32 = jnp.float32
MESH = pl.DeviceIdType.MESH
HBM_SPEC = pl.BlockSpec(memory_space=pltpu.HBM)

N_DEV = 8
LANE = 128
EPS = 1e-6
ROPE_BASE = 10000.0
RET_HEADS = 4
HGRN_HEADS = 8
RET_CHUNK = 128
HGRN_BLOCK = 16
FFN_RESIDUAL_WEIGHT = 0.5
ADAM_LR = 0.001
ADAM_B1 = 0.9
ADAM_B2 = 0.999
ADAM_EPS = 1e-08
ADAM_WD = 0.01
ADAM_STEP = 10
VMEM_LIMIT = 56 * 1024 * 1024


def _tile(n, pref, mult=8):
    t = min(pref, n)
    t -= t % mult
    while t >= mult:
        if n % t == 0:
            return t
        t -= mult
    return n


def _params(*sem):
    return pltpu.CompilerParams(dimension_semantics=sem, vmem_limit_bytes=VMEM_LIMIT)


class _Comm:
    def __init__(self, operands, out_shape, n_sems, start, finish, aliases=None, middle=None, middle_at=0.0):
        self.operands = list(operands)
        self.out_shape = list(out_shape)
        self.n_sems = n_sems
        self.start = start
        self.finish = finish
        self.aliases = dict(aliases or {})
        self.middle = middle
        self.middle_at = middle_at


def _launch(body, *, name, grid, in_specs, out_specs, out_shape, sem, args, scratch_shapes=(), comm=None):
    in_specs, out_specs, out_shape = list(in_specs), list(out_specs), list(out_shape)
    scratch_shapes = list(scratch_shapes)
    if comm is None:
        res = pl.pallas_call(body, name=name, grid=grid, in_specs=in_specs, out_specs=out_specs, out_shape=out_shape,
                             scratch_shapes=scratch_shapes, compiler_params=_params(*sem))(*args)
        return list(res), []
    n_in, n_out, n_scr = len(in_specs), len(out_specs), len(scratch_shapes)
    ci, co = len(comm.operands), len(comm.out_shape)

    def carrying(*refs):
        bounds = [0, n_in, n_in + ci, n_in + ci + n_out, n_in + ci + n_out + co, n_in + ci + n_out + co + n_scr]
        ins, cins, outs, couts, scr = [refs[a:b] for a, b in zip(bounds[:-1], bounds[1:])]
        send_sems, recv_sems = refs[bounds[-1]:]
        ids = [pl.program_id(k) for k in range(len(grid))]
        first = functools.reduce(jnp.logical_and, [i == 0 for i in ids])
        last = functools.reduce(jnp.logical_and, [i == g - 1 for i, g in zip(ids, grid)])

        @pl.when(first)
        def _():
            comm.start(cins, couts, send_sems, recv_sems)

        if comm.middle is not None:
            step, total = ids[0], grid[0]
            for i, g in zip(ids[1:], grid[1:]):
                step, total = step * g + i, total * g

            @pl.when(step == int(total * comm.middle_at))
            def _():
                comm.middle(cins, couts, send_sems, recv_sems)

        body(*ins, *outs, *scr)

        @pl.when(last)
        def _():
            comm.finish(cins, couts, send_sems, recv_sems)

    res = pl.pallas_call(
        carrying, name=name, grid=grid,
        in_specs=in_specs + [HBM_SPEC] * ci, out_specs=out_specs + [HBM_SPEC] * co,
        out_shape=out_shape + comm.out_shape,
        scratch_shapes=scratch_shapes + [pltpu.SemaphoreType.DMA((comm.n_sems,)), pltpu.SemaphoreType.DMA((comm.n_sems,))],
        input_output_aliases={n_in + a: n_out + b for a, b in comm.aliases.items()},
        compiler_params=_params(*(["arbitrary"] * len(grid))),
    )(*args, *comm.operands)
    return list(res[:n_out]), list(res[n_out:])


def _comm_only_call(comm, name):
    def body(*refs):
        ci, co = len(comm.operands), len(comm.out_shape)
        cins, couts = refs[:ci], refs[ci:ci + co]
        send_sems, recv_sems = refs[ci + co:]
        comm.start(cins, couts, send_sems, recv_sems)
        comm.finish(cins, couts, send_sems, recv_sems)

    return pl.pallas_call(
        body, name=name,
        in_specs=[HBM_SPEC] * len(comm.operands), out_specs=[HBM_SPEC] * len(comm.out_shape),
        out_shape=comm.out_shape,
        scratch_shapes=[pltpu.SemaphoreType.DMA((comm.n_sems,)), pltpu.SemaphoreType.DMA((comm.n_sems,))],
        input_output_aliases=comm.aliases,
    )(*comm.operands)


def _sigmoid(v):
    return 1.0 / (1.0 + jnp.exp(-v))


def _dot(a, b):
    return jnp.dot(a, b, preferred_element_type=F32)


def _dot_nt(a, b):
    return lax.dot_general(a, b, (((1,), (1,)), ((), ())), preferred_element_type=F32)


def _dot_tn(a, b):
    return lax.dot_general(a, b, (((0,), (0,)), ((), ())), preferred_element_type=F32)


def _rmsnorm_call(x, gain, name):
    s, d = x.shape
    tm = _tile(s, 512)

    def body(x_ref, g_ref, o_ref):
        xv = x_ref[...]
        r = lax.rsqrt(jnp.mean(xv * xv, axis=-1, keepdims=True) + EPS)
        o_ref[...] = (xv * r * g_ref[...]).astype(BF)

    return pl.pallas_call(
        body, name=name, grid=(s // tm,),
        in_specs=[pl.BlockSpec((tm, d), lambda i: (i, 0)), pl.BlockSpec((1, d), lambda i: (0, 0))],
        out_specs=pl.BlockSpec((tm, d), lambda i: (i, 0)),
        out_shape=jax.ShapeDtypeStruct((s, d), BF),
        compiler_params=_params("parallel"),
    )(x, gain)


def _ffn_up_call(h, wg, wu, name, comm=None):
    s, d = h.shape
    nj, _, k = wg.shape
    tm = _tile(s, 512)

    def body(h_ref, wg_ref, wu_ref, g_ref, u_ref, a_ref):
        hv = h_ref[...]
        g = _dot(hv, wg_ref[...])
        u = _dot(hv, wu_ref[...])
        g_ref[...] = g
        u_ref[...] = u
        a_ref[...] = (g * _sigmoid(g) * u).astype(BF)

    act = pl.BlockSpec((tm, k), lambda i, j: (i, j))
    wsp = pl.BlockSpec((None, d, k), lambda i, j: (j, 0, 0))
    return _launch(
        body, name=name, grid=(s // tm, nj),
        in_specs=[pl.BlockSpec((tm, d), lambda i, j: (i, 0)), wsp, wsp],
        out_specs=[act, act, act],
        out_shape=[jax.ShapeDtypeStruct((s, nj * k), F32), jax.ShapeDtypeStruct((s, nj * k), F32),
                   jax.ShapeDtypeStruct((s, nj * k), BF)],
        sem=("parallel", "arbitrary"), args=(h, wg, wu), comm=comm)


def _proj_call(h, w, name, comm=None):
    s, d = h.shape
    nj, _, k = w.shape
    tm = _tile(s, 512)

    def body(h_ref, w_ref, o_ref):
        o_ref[...] = _dot(h_ref[...], w_ref[...])

    return _launch(
        body, name=name, grid=(s // tm, nj),
        in_specs=[pl.BlockSpec((tm, d), lambda i, j: (i, 0)), pl.BlockSpec((None, d, k), lambda i, j: (j, 0, 0))],
        out_specs=[pl.BlockSpec((tm, k), lambda i, j: (i, j))],
        out_shape=[jax.ShapeDtypeStruct((s, nj * k), F32)],
        sem=("parallel", "arbitrary"), args=(h, w), comm=comm)


def _down_call(a, w, resid, gain, scale, name, comm=None):
    s = a.shape[0]
    nj, k, d = w.shape
    tm = _tile(s, 512)
    with_norm = gain is not None

    def body(*refs):
        if with_norm:
            a_ref, w_ref, r_ref, g_ref, x_ref, h_ref, acc = refs
        else:
            a_ref, w_ref, r_ref, x_ref, acc = refs
        j = pl.program_id(1)

        @pl.when(j == 0)
        def _():
            acc[...] = jnp.zeros_like(acc)

        acc[...] += _dot(a_ref[...], w_ref[...])

        @pl.when(j == nj - 1)
        def _():
            xn = r_ref[...] + (scale * acc[...])
            x_ref[...] = xn
            if with_norm:
                r = lax.rsqrt(jnp.mean(xn * xn, axis=-1, keepdims=True) + EPS)
                h_ref[...] = (xn * r * g_ref[...]).astype(BF)

    row = pl.BlockSpec((tm, d), lambda i, j: (i, 0))
    in_specs = [pl.BlockSpec((tm, k), lambda i, j: (i, j)), pl.BlockSpec((None, k, d), lambda i, j: (j, 0, 0)), row]
    args = [a, w, resid]
    out_specs = [row]
    out_shape = [jax.ShapeDtypeStruct((s, d), F32)]
    if with_norm:
        in_specs.append(pl.BlockSpec((1, d), lambda i, j: (0, 0)))
        args.append(gain)
        out_specs.append(row)
        out_shape.append(jax.ShapeDtypeStruct((s, d), BF))
    return _launch(
        body, name=name, grid=(s // tm, nj),
        in_specs=in_specs, out_specs=out_specs, out_shape=out_shape,
        scratch_shapes=[pltpu.VMEM((tm, d), F32)],
        sem=("parallel", "arbitrary"), args=args, comm=comm)


def _loss_call(x, target, gain, name):
    s, d = x.shape
    tm = _tile(s, 512)

    def body(x_ref, t_ref, g_ref, loss_ref, dx_ref, dxb_ref, dg_ref):
        i = pl.program_id(0)

        @pl.when(i == 0)
        def _():
            loss_ref[...] = jnp.zeros_like(loss_ref)
            dg_ref[...] = jnp.zeros_like(dg_ref)

        xv = x_ref[...]
        gv = g_ref[...]
        r = lax.rsqrt(jnp.mean(xv * xv, axis=-1, keepdims=True) + EPS)
        xhat = xv * r
        err = xhat * gv - t_ref[...]
        per_tok = jnp.mean(err * err, axis=-1, keepdims=True)
        loss_ref[...] += 0.5 * jnp.sum(per_tok, axis=0, keepdims=True)
        dout = err * (1.0 / d)
        dg_ref[...] += jnp.sum(dout * xhat, axis=0, keepdims=True)
        dxhat = dout * gv
        dx = r * (dxhat - xhat * jnp.mean(dxhat * xhat, axis=-1, keepdims=True))
        dx_ref[...] = dx
        dxb_ref[...] = dx.astype(BF)

    row = pl.BlockSpec((tm, d), lambda i: (i, 0))
    vec = pl.BlockSpec((1, d), lambda i: (0, 0))
    return pl.pallas_call(
        body, name=name, grid=(s // tm,),
        in_specs=[row, row, vec],
        out_specs=[pl.BlockSpec((1, 1), lambda i: (0, 0)), row, row, vec],
        out_shape=[jax.ShapeDtypeStruct((1, 1), F32), jax.ShapeDtypeStruct((s, d), F32),
                   jax.ShapeDtypeStruct((s, d), BF), jax.ShapeDtypeStruct((1, d), F32)],
        compiler_params=_params("arbitrary"),
    )(x, target, gain)


def _bwd_up_call(dy, wd, g, u, scale, name, comm=None):
    s, d = dy.shape
    nj, k, _ = wd.shape
    tm = _tile(s, 512)

    def body(dy_ref, w_ref, g_ref, u_ref, dg_ref, du_ref):
        da = scale * _dot_nt(dy_ref[...], w_ref[...])
        gv = g_ref[...]
        sig = _sigmoid(gv)
        du_ref[...] = (da * gv * sig).astype(BF)
        dg_ref[...] = (da * u_ref[...] * sig * (1.0 + gv * (1.0 - sig))).astype(BF)

    act = pl.BlockSpec((tm, k), lambda i, j: (i, j))
    return _launch(
        body, name=name, grid=(s // tm, nj),
        in_specs=[pl.BlockSpec((tm, d), lambda i, j: (i, 0)), pl.BlockSpec((None, k, d), lambda i, j: (j, 0, 0)), act, act],
        out_specs=[act, act],
        out_shape=[jax.ShapeDtypeStruct((s, nj * k), BF), jax.ShapeDtypeStruct((s, nj * k), BF)],
        sem=("parallel", "arbitrary"), args=(dy, wd, g, u), comm=comm)


def _bwd_down_call(pairs, dres, xin, gain, name, comm=None):
    s, d = xin.shape
    nj, _, k = pairs[0][1].shape
    npair = len(pairs)
    tm = _tile(s, 512)
    strip = _tile(tm, 128)

    def body(*refs):
        a_refs = refs[0:2 * npair:2]
        w_refs = refs[1:2 * npair:2]
        dres_ref, x_ref, g_ref, dx_ref, dxb_ref, dg_ref, acc = refs[2 * npair:]
        i = pl.program_id(0)
        j = pl.program_id(1)

        @pl.when(j == 0)
        def _():
            acc[...] = jnp.zeros_like(acc)

        @pl.when((i == 0) & (j == 0))
        def _():
            dg_ref[...] = jnp.zeros_like(dg_ref)

        for a_ref, w_ref in zip(a_refs, w_refs):
            acc[...] += _dot_nt(a_ref[...], w_ref[...])

        @pl.when(j == nj - 1)
        def _():
            for r0 in range(0, tm, strip):
                rows = slice(r0, r0 + strip)
                xv = x_ref[rows, :]
                r = lax.rsqrt(jnp.mean(xv * xv, axis=-1, keepdims=True) + EPS)
                xhat = xv * r
                dh = acc[rows, :]
                dg_ref[...] += jnp.sum(dh * xhat, axis=0, keepdims=True)
                dxhat = dh * g_ref[...]
                dx = dres_ref[rows, :] + r * (dxhat - xhat * jnp.mean(dxhat * xhat, axis=-1, keepdims=True))
                dx_ref[rows, :] = dx
                dxb_ref[rows, :] = dx.astype(BF)

    row = pl.BlockSpec((tm, d), lambda i, j: (i, 0))
    vec = pl.BlockSpec((1, d), lambda i, j: (0, 0))
    in_specs, args = [], []
    for a, w in pairs:
        in_specs += [pl.BlockSpec((tm, k), lambda i, j: (i, j)), pl.BlockSpec((None, d, k), lambda i, j: (j, 0, 0))]
        args += [a, w]
    once = pl.BlockSpec((tm, d), lambda i, j: (i, 0), pipeline_mode=pl.Buffered(1))
    in_specs += [once, once, vec]
    args += [dres, xin, gain]
    return _launch(
        body, name=name, grid=(s // tm, nj),
        in_specs=in_specs, out_specs=[row, row, vec],
        out_shape=[jax.ShapeDtypeStruct((s, d), F32), jax.ShapeDtypeStruct((s, d), BF), jax.ShapeDtypeStruct((1, d), F32)],
        scratch_shapes=[pltpu.VMEM((tm, d), F32)],
        sem=("arbitrary", "arbitrary"), args=args, comm=comm)


def _wgrad_call(a, b, nj, a_blocked, scale, name, comm=None):
    s = a.shape[0]
    ka = a.shape[1] // nj if a_blocked else a.shape[1]
    kb = b.shape[1] if a_blocked else b.shape[1] // nj
    ts = _tile(s, 1024)
    ns = s // ts

    def body(a_ref, b_ref, o_ref):
        t = pl.program_id(1)

        @pl.when(t == 0)
        def _():
            o_ref[...] = jnp.zeros_like(o_ref)

        o_ref[...] += _dot_tn(a_ref[...], b_ref[...])
        if scale != 1.0:
            @pl.when(t == ns - 1)
            def _():
                o_ref[...] = o_ref[...] * scale

    a_spec = pl.BlockSpec((ts, ka), (lambda j, t: (t, j)) if a_blocked else (lambda j, t: (t, 0)))
    b_spec = pl.BlockSpec((ts, kb), (lambda j, t: (t, 0)) if a_blocked else (lambda j, t: (t, j)))
    (out,), landed = _launch(
        body, name=name, grid=(nj, ns),
        in_specs=[a_spec, b_spec],
        out_specs=[pl.BlockSpec((None, ka, kb), lambda j, t: (j, 0, 0))],
        out_shape=[jax.ShapeDtypeStruct((nj, ka, kb), F32)],
        sem=("parallel", "arbitrary"), args=(a, b), comm=comm)
    return out if comm is None else (out, landed)


def _rope(v, cos, sin):
    half = v.shape[-1] // 2
    v1, v2 = v[:, :half], v[:, half:]
    return jnp.concatenate([v1 * cos - v2 * sin, v2 * cos + v1 * sin], axis=-1)


def _rope_bwd(dv, cos, sin):
    half = dv.shape[-1] // 2
    d1, d2 = dv[:, :half], dv[:, half:]
    return jnp.concatenate([d1 * cos + d2 * sin, d2 * cos - d1 * sin], axis=-1)


def _ret_consts(hd):
    c = RET_CHUNK
    log_gamma = jnp.log(1.0 - jnp.exp2(-5.0 - jnp.arange(RET_HEADS, dtype=F32)))
    idx = jnp.arange(c, dtype=F32)
    rel = idx[:, None] - idx[None, :]
    mask = rel >= 0
    decay = jnp.where(mask[None], jnp.exp(log_gamma[:, None, None] * jnp.where(mask, rel, 0.0)[None]), 0.0)
    qdec = jnp.exp(log_gamma[:, None] * (idx + 1.0)[None, :])
    kdec = jnp.exp(log_gamma[:, None] * (c - 1.0 - idx)[None, :])
    gchunk = jnp.exp(log_gamma * c)
    bc = lambda t: jnp.broadcast_to(t[:, :, None], (RET_HEADS, t.shape[1], hd))
    return decay, bc(qdec), bc(kdec), bc(gchunk[:, None])


def _rope_tables(s, hd):
    inv = jnp.power(ROPE_BASE, -jnp.arange(0, hd, 2, dtype=F32) / hd)
    ang = jnp.arange(s, dtype=F32)[:, None] * inv[None, :]
    return jnp.cos(ang), jnp.sin(ang)


def _ret_fwd_call(proj, cos, sin, consts, gret, name, comm=None):
    s = proj.shape[0]
    w = proj.shape[1] // 8
    hd = w // RET_HEADS
    c = RET_CHUNK
    tt = _tile(s, 512, c)
    nc = tt // c
    decay, qdec, kdec, gch = consts
    scale = hd ** -0.5

    def body(q_ref, k_ref, v_ref, gate_ref, cos_ref, sin_ref, dec_ref, qd_ref, kd_ref, gc_ref, gn_ref,
             o_ref, m_ref, st_ref, state):
        @pl.when(pl.program_id(1) == 0)
        def _():
            state[...] = jnp.zeros_like(state)

        dec = dec_ref[...]
        for ci in range(nc):
            rows = slice(ci * c, (ci + 1) * c)
            cs, sn = cos_ref[rows, :], sin_ref[rows, :]
            q = _rope(q_ref[rows, :], cs, sn) * scale
            k = _rope(k_ref[rows, :], cs, sn)
            vb = v_ref[rows, :].astype(BF)
            sc = _dot_nt(q.astype(BF), k.astype(BF)) * dec
            prev = state[...]
            st_ref[ci] = prev
            o = _dot(sc.astype(BF), vb) + _dot((q * qd_ref[...]).astype(BF), prev.astype(BF))
            state[...] = gc_ref[...] * prev + _dot_tn((k * kd_ref[...]).astype(BF), vb)
            o_ref[rows, :] = o
            mu = jnp.mean(o, axis=-1, keepdims=True)
            cen = o - mu
            xhat = cen * lax.rsqrt(jnp.mean(cen * cen, axis=-1, keepdims=True) + EPS)
            gt = gate_ref[rows, :]
            m_ref[rows, :] = (xhat * gn_ref[...] * (gt * _sigmoid(gt))).astype(BF)

    nh = RET_HEADS
    comp = lambda j: pl.BlockSpec((tt, hd), lambda h, t, j=j: (t, j * nh + h))
    tab = pl.BlockSpec((tt, hd // 2), lambda h, t: (t, 0))
    per_head = lambda r: pl.BlockSpec((None, r, hd), lambda h, t: (h, 0, 0))
    return _launch(
        body, name=name, grid=(nh, s // tt),
        in_specs=[comp(0), comp(1), comp(2), comp(3), tab, tab,
                  pl.BlockSpec((None, c, c), lambda h, t: (h, 0, 0)), per_head(c), per_head(c), per_head(1),
                  pl.BlockSpec((1, hd), lambda h, t: (0, h))],
        out_specs=[pl.BlockSpec((tt, hd), lambda h, t: (t, h)), pl.BlockSpec((tt, hd), lambda h, t: (t, h)),
                   pl.BlockSpec((None, nc, hd, hd), lambda h, t: (h, t, 0, 0))],
        out_shape=[jax.ShapeDtypeStruct((s, w), F32), jax.ShapeDtypeStruct((s, w), BF),
                   jax.ShapeDtypeStruct((nh, s // c, hd, hd), F32)],
        scratch_shapes=[pltpu.VMEM((hd, hd), F32)],
        sem=("parallel", "arbitrary"),
        args=(proj, proj, proj, proj, cos, sin, decay, qdec, kdec, gch, gret), comm=comm)


def _ret_bwd_call(proj, cos, sin, consts, gret, o_raw, states, dmerged, name):
    s = proj.shape[0]
    w = proj.shape[1] // 8
    hd = w // RET_HEADS
    c = RET_CHUNK
    tt = _tile(s, 512, c)
    nc = tt // c
    nt = s // tt
    decay, qdec, kdec, gch = consts
    scale = hd ** -0.5

    def body(q_ref, k_ref, v_ref, gate_ref, cos_ref, sin_ref, dec_ref, qd_ref, kd_ref, gc_ref, gn_ref,
             o_ref, st_ref, dm_ref, dq_ref, dk_ref, dv_ref, dgate_ref, dgn_ref, dstate):
        @pl.when(pl.program_id(1) == 0)
        def _():
            dstate[...] = jnp.zeros_like(dstate)
            dgn_ref[...] = jnp.zeros_like(dgn_ref)

        dec = dec_ref[...]
        gn = gn_ref[...]
        for ci in reversed(range(nc)):
            rows = slice(ci * c, (ci + 1) * c)
            cs, sn = cos_ref[rows, :], sin_ref[rows, :]
            q = _rope(q_ref[rows, :], cs, sn) * scale
            k = _rope(k_ref[rows, :], cs, sn)
            qb, kb = q.astype(BF), k.astype(BF)
            vb = v_ref[rows, :].astype(BF)
            sc = _dot_nt(qb, kb) * dec
            o = o_ref[rows, :]
            mu = jnp.mean(o, axis=-1, keepdims=True)
            cen = o - mu
            rstd = lax.rsqrt(jnp.mean(cen * cen, axis=-1, keepdims=True) + EPS)
            xhat = cen * rstd
            gt = gate_ref[rows, :]
            sig = _sigmoid(gt)
            sg = gt * sig
            dm = dm_ref[rows, :]
            dgn_ref[...] += jnp.sum(dm * xhat * sg, axis=0, keepdims=True)
            dgate_ref[rows, :] = (dm * xhat * gn * sig * (1.0 + gt * (1.0 - sig))).astype(BF)
            dxhat = dm * gn * sg
            do = rstd * (dxhat - jnp.mean(dxhat, axis=-1, keepdims=True)
                         - xhat * jnp.mean(dxhat * xhat, axis=-1, keepdims=True))
            dob = do.astype(BF)
            prev = st_ref[ci]
            ds = dstate[...]
            dsb = ds.astype(BF)
            dsc = (_dot_nt(dob, vb) * dec).astype(BF)
            dq = _dot(dsc, kb) + _dot_nt(dob, prev.astype(BF)) * qd_ref[...]
            dk = _dot_tn(dsc, qb) + _dot_nt(vb, dsb) * kd_ref[...]
            dv = _dot_tn(sc.astype(BF), dob) + _dot((k * kd_ref[...]).astype(BF), dsb)
            dstate[...] = gc_ref[...] * ds + _dot_tn((q * qd_ref[...]).astype(BF), dob)
            dq_ref[rows, :] = _rope_bwd(dq * scale, cs, sn).astype(BF)
            dk_ref[rows, :] = _rope_bwd(dk, cs, sn).astype(BF)
            dv_ref[rows, :] = dv.astype(BF)

    nh = RET_HEADS
    rev = lambda t: nt - 1 - t
    comp = lambda j: pl.BlockSpec((tt, hd), lambda h, t, j=j: (rev(t), j * nh + h))
    tab = pl.BlockSpec((tt, hd // 2), lambda h, t: (rev(t), 0))
    per_head = lambda r: pl.BlockSpec((None, r, hd), lambda h, t: (h, 0, 0))
    head_cols = pl.BlockSpec((tt, hd), lambda h, t: (rev(t), h))
    gvec = pl.BlockSpec((1, hd), lambda h, t: (0, h))
    act = jax.ShapeDtypeStruct((s, w), BF)
    return pl.pallas_call(
        body, name=name, grid=(nh, nt),
        in_specs=[comp(0), comp(1), comp(2), comp(3), tab, tab,
                  pl.BlockSpec((None, c, c), lambda h, t: (h, 0, 0)), per_head(c), per_head(c), per_head(1), gvec,
                  head_cols, pl.BlockSpec((None, nc, hd, hd), lambda h, t: (h, rev(t), 0, 0)), head_cols],
        out_specs=[head_cols, head_cols, head_cols, head_cols, gvec],
        out_shape=[act, act, act, act, jax.ShapeDtypeStruct((1, w), F32)],
        scratch_shapes=[pltpu.VMEM((hd, hd), F32)],
        compiler_params=_params("parallel", "arbitrary"),
    )(proj, proj, proj, proj, cos, sin, decay, qdec, kdec, gch, gret, o_raw, states, dmerged)


def _block_tri(n, bs, upper):
    r = jnp.arange(n)[:, None]
    cidx = jnp.arange(n)[None, :]
    same = (r // bs) == (cidx // bs)
    return jnp.where(same & ((cidx >= r) if upper else (cidx <= r)), 1.0, 0.0).astype(F32)


def _dot_exact(a, b):
    return jnp.dot(a, b, preferred_element_type=F32, precision=lax.Precision.HIGHEST)


def _hgrn_gates(z, lbv):
    sz = _sigmoid(z)
    oml = 1.0 - lbv
    f = lbv + oml * sz
    key = oml * (1.0 - sz)
    return sz, f, key


def _hgrn_fwd_call(proj, lb_logits, ghg, name, comm=None):
    s = proj.shape[0]
    w = proj.shape[1] // 8
    nh = HGRN_HEADS
    hd = w // nh
    bs = HGRN_BLOCK
    tt = _tile(s, 256, bs)
    nb = tt // bs

    def body(q_ref, z_ref, v_ref, gate_ref, lb_ref, gn_ref, tril_ref, o_ref, m_ref, st_ref, state, upd):
        @pl.when(pl.program_id(1) == 0)
        def _():
            state[...] = jnp.zeros_like(state)

        lbv = _sigmoid(lb_ref[...])
        _, f, key = _hgrn_gates(z_ref[...], lbv)
        qr = q_ref[...]
        q = qr * _sigmoid(qr)
        v = v_ref[...]
        g = _dot_exact(tril_ref[...], jnp.log(f))
        blocks = lambda t: t.reshape(nb, bs, hd)
        g3, q3, k3, v3 = blocks(g), blocks(q), blocks(key), blocks(v)
        glast3 = g3[:, bs - 1:bs, :]
        row_id = lax.broadcasted_iota(jnp.int32, (nb, bs, hd), 1)
        o3 = jnp.zeros((nb, bs, hd), F32)
        for j in range(bs):
            wj = jnp.where(row_id >= j, jnp.exp(jnp.minimum(g3 - g3[:, j:j + 1, :], 0.0)), 0.0)
            a = jnp.sum(q3 * k3[:, j:j + 1, :] * wj, axis=-1, keepdims=True)
            o3 = o3 + a * v3[:, j:j + 1, :]
        ktb = (k3 * jnp.exp(glast3 - g3)).reshape(tt, hd).astype(BF)
        vb = v.astype(BF)
        for b in range(nb):
            rows = slice(b * bs, (b + 1) * bs)
            upd[b] = _dot_tn(vb[rows, :], ktb[rows, :])
        egl3 = jnp.exp(glast3)
        st = state[...]
        for b in range(nb):
            st_ref[b] = st
            st = st * egl3[b] + upd[b]
        state[...] = st
        qgb = (q * jnp.exp(g)).astype(BF)
        o_intra = o3.reshape(tt, hd)
        gn = gn_ref[...]
        for b in range(nb):
            rows = slice(b * bs, (b + 1) * bs)
            o = o_intra[rows, :] + _dot_nt(qgb[rows, :], st_ref[b].astype(BF))
            o_ref[rows, :] = o
            gt = gate_ref[rows, :]
            xhat = o * lax.rsqrt(jnp.mean(o * o, axis=-1, keepdims=True) + EPS)
            m_ref[rows, :] = (xhat * gn * (gt * _sigmoid(gt))).astype(BF)

    comp = lambda j: pl.BlockSpec((tt, hd), lambda h, t, j=j: (t, j * nh + h))
    gvec = pl.BlockSpec((1, hd), lambda h, t: (0, h))
    head_cols = pl.BlockSpec((tt, hd), lambda h, t: (t, h))
    return _launch(
        body, name=name, grid=(nh, s // tt),
        in_specs=[comp(4), comp(5), comp(6), comp(7), gvec, gvec, pl.BlockSpec((tt, tt), lambda h, t: (0, 0))],
        out_specs=[head_cols, head_cols, pl.BlockSpec((None, nb, hd, hd), lambda h, t: (h, t, 0, 0))],
        out_shape=[jax.ShapeDtypeStruct((s, w), F32), jax.ShapeDtypeStruct((s, w), BF),
                   jax.ShapeDtypeStruct((nh, s // bs, hd, hd), F32)],
        scratch_shapes=[pltpu.VMEM((hd, hd), F32), pltpu.VMEM((nb, hd, hd), F32)],
        sem=("parallel", "arbitrary"),
        args=(proj, proj, proj, proj, lb_logits, ghg, _block_tri(tt, bs, upper=False)), comm=comm)


def _hgrn_bwd_call(proj, lb_logits, ghg, o_raw, states, dmerged, name, comm=None):
    s = proj.shape[0]
    w = proj.shape[1] // 8
    nh = HGRN_HEADS
    hd = w // nh
    bs = HGRN_BLOCK
    tt = _tile(s, 256, bs)
    nb = tt // bs
    nt = s // tt

    def body(q_ref, z_ref, v_ref, gate_ref, lb_ref, gn_ref, tril_ref, triu_ref, o_ref, st_ref, dm_ref,
             dq_ref, dz_ref, dv_ref, dgate_ref, dlb_ref, dgn_ref,
             dstate, ds_all, inc, dq_s, dk_s, dv_s, dgl_s, dk_rows, dv_rows):
        @pl.when(pl.program_id(1) == 0)
        def _():
            dstate[...] = jnp.zeros_like(dstate)
            dlb_ref[...] = jnp.zeros_like(dlb_ref)
            dgn_ref[...] = jnp.zeros_like(dgn_ref)

        lbv = _sigmoid(lb_ref[...])
        oml = 1.0 - lbv
        gn = gn_ref[...]
        sz, f, key = _hgrn_gates(z_ref[...], lbv)
        qr = q_ref[...]
        sq = _sigmoid(qr)
        q = qr * sq
        v = v_ref[...]
        g = _dot_exact(tril_ref[...], jnp.log(f))
        eg = jnp.exp(g)
        blocks = lambda t: t.reshape(nb, bs, hd)
        g3, q3, k3, v3 = blocks(g), blocks(q), blocks(key), blocks(v)
        glast3 = g3[:, bs - 1:bs, :]
        egl3 = jnp.exp(glast3)
        ktail3 = jnp.exp(glast3 - g3)
        o = o_ref[...]
        rstd = lax.rsqrt(jnp.mean(o * o, axis=-1, keepdims=True) + EPS)
        xhat = o * rstd
        gt = gate_ref[...]
        sig = _sigmoid(gt)
        sg = gt * sig
        dm = dm_ref[...]
        dgn_ref[...] += jnp.sum(dm * xhat * sg, axis=0, keepdims=True)
        dgate_ref[...] = (dm * xhat * gn * sig * (1.0 + gt * (1.0 - sig))).astype(BF)
        dxhat = dm * gn * sg
        do = rstd * (dxhat - xhat * jnp.mean(dxhat * xhat, axis=-1, keepdims=True))
        dob = do.astype(BF)
        do3 = blocks(do)
        qgb = (q * eg).astype(BF)
        for b in range(nb):
            rows = slice(b * bs, (b + 1) * bs)
            inc[b] = _dot_tn(dob[rows, :], qgb[rows, :])
        ds = dstate[...]
        for b in reversed(range(nb)):
            ds_all[b] = ds
            ds = ds * egl3[b] + inc[b]
        dstate[...] = ds
        ktb = (k3 * ktail3).reshape(tt, hd).astype(BF)
        vb = v.astype(BF)
        for b in range(nb):
            rows = slice(b * bs, (b + 1) * bs)
            prev = st_ref[b]
            dsb = ds_all[b]
            dsbb = dsb.astype(BF)
            dq_s[rows, :] = _dot(dob[rows, :], prev.astype(BF))
            dk_s[rows, :] = _dot(vb[rows, :], dsbb)
            dv_s[rows, :] = _dot_nt(ktb[rows, :], dsbb)
            dgl_s[b] = jnp.sum(prev * dsb, axis=0, keepdims=True)
        dq3 = blocks(dq_s[...] * eg)
        dk3 = blocks(dk_s[...]) * ktail3
        dg_last3 = jnp.sum(k3 * dk3, axis=1, keepdims=True) + egl3 * dgl_s[...]
        row_id = lax.broadcasted_iota(jnp.int32, (nb, bs, hd), 1)
        for j in range(bs):
            wj = jnp.where(row_id >= j, jnp.exp(jnp.minimum(g3 - g3[:, j:j + 1, :], 0.0)), 0.0)
            kj = k3[:, j:j + 1, :]
            a = jnp.sum(q3 * kj * wj, axis=-1, keepdims=True)
            da = jnp.sum(do3 * v3[:, j:j + 1, :], axis=-1, keepdims=True)
            dv_rows[:, j:j + 1, :] = jnp.sum(a * do3, axis=1, keepdims=True)
            dq3 = dq3 + da * kj * wj
            dk_rows[:, j:j + 1, :] = jnp.sum(da * q3 * wj, axis=1, keepdims=True)
        dk3 = dk3 + dk_rows[...]
        dv = dv_s[...] + dv_rows[...].reshape(tt, hd)
        dg3 = q3 * dq3 - k3 * dk3 + jnp.where(row_id == bs - 1, dg_last3, 0.0)
        dlf = _dot_exact(triu_ref[...], dg3.reshape(tt, hd))
        dk = dk3.reshape(tt, hd)
        dfk = dlf / f - dk
        dlb_ref[...] += jnp.sum(dfk * (1.0 - sz), axis=0, keepdims=True) * (lbv * oml)
        dz_ref[...] = (dfk * oml * sz * (1.0 - sz)).astype(BF)
        dq_ref[...] = (dq3.reshape(tt, hd) * sq * (1.0 + qr * (1.0 - sq))).astype(BF)
        dv_ref[...] = dv.astype(BF)

    rev = lambda t: nt - 1 - t
    comp = lambda j: pl.BlockSpec((tt, hd), lambda h, t, j=j: (rev(t), j * nh + h))
    gvec = pl.BlockSpec((1, hd), lambda h, t: (0, h))
    head_cols = pl.BlockSpec((tt, hd), lambda h, t: (rev(t), h))
    tri = pl.BlockSpec((tt, tt), lambda h, t: (0, 0))
    act = jax.ShapeDtypeStruct((s, w), BF)
    vec = jax.ShapeDtypeStruct((1, w), F32)
    tile_f32 = pltpu.VMEM((tt, hd), F32)
    return _launch(
        body, name=name, grid=(nh, nt),
        in_specs=[comp(4), comp(5), comp(6), comp(7), gvec, gvec, tri, tri, head_cols,
                  pl.BlockSpec((None, nb, hd, hd), lambda h, t: (h, rev(t), 0, 0)),
                  pl.BlockSpec((tt, hd), lambda h, t: (rev(t), nh + h))],
        out_specs=[head_cols, head_cols, head_cols, head_cols, gvec, gvec],
        out_shape=[act, act, act, act, vec, vec],
        scratch_shapes=[pltpu.VMEM((hd, hd), F32), pltpu.VMEM((nb, hd, hd), F32), pltpu.VMEM((nb, hd, hd), F32),
                        tile_f32, tile_f32, tile_f32, pltpu.VMEM((nb, 1, hd), F32),
                        pltpu.VMEM((nb, bs, hd), F32), pltpu.VMEM((nb, bs, hd), F32)],
        sem=("parallel", "arbitrary"),
        args=(proj, proj, proj, proj, lb_logits, ghg, _block_tri(tt, bs, upper=False), _block_tri(tt, bs, upper=True),
              o_raw, states, dmerged), comm=comm)


def _position():
    return lax.axis_index("x"), lax.axis_index("y"), lax.axis_index("c")


def _all_gather_call(shards, name):
    n = len(shards)

    def body(*refs):
        ins, outs = refs[:n], refs[n:2 * n]
        send_sems, recv_sems, local_sems = refs[2 * n:]
        x, y, c = _position()
        me, sibling = (x, y, c), (x, y, 1 - c)
        chips = [(1 - x, y), (x, 1 - y), (1 - x, 1 - y)]

        def slot(a, p):
            return outs[a].at[4 * p[0] + 2 * p[1] + p[2]]

        def copy(a, k, block, to, src=None):
            return pltpu.make_async_remote_copy(
                src_ref=slot(a, block) if src is None else src, dst_ref=slot(a, block),
                send_sem=send_sems.at[a * 7 + k], recv_sem=recv_sems.at[a * 7 + k],
                device_id=to, device_id_type=MESH)

        mine = [pltpu.make_async_copy(ins[a], slot(a, me), local_sems.at[a]) for a in range(n)]
        for cp in mine:
            cp.start()
        first = []
        for a in range(n):
            first.append(copy(a, 0, me, sibling, src=ins[a]))
            first += [copy(a, 1 + j, me, (*chip, c), src=ins[a]) for j, chip in enumerate(chips)]
        for cp in first:
            cp.start()
        passed = []
        for j, chip in enumerate(chips):
            for a in range(n):
                copy(a, 1 + j, (*chip, c), me).wait_recv()
                fwd = copy(a, 4 + j, (*chip, c), sibling)
                fwd.start()
                passed.append(fwd)
        for a in range(n):
            copy(a, 0, sibling, me).wait_recv()
            for j, chip in enumerate(chips):
                copy(a, 4 + j, (*chip, 1 - c), me).wait_recv()
        for cp in first + passed:
            cp.wait_send()
        for cp in mine:
            cp.wait()

    return pl.pallas_call(
        body, name=name,
        in_specs=[HBM_SPEC] * n, out_specs=[HBM_SPEC] * n,
        out_shape=[jax.ShapeDtypeStruct((N_DEV,) + t.shape, t.dtype) for t in shards],
        scratch_shapes=[pltpu.SemaphoreType.DMA((7 * n,)), pltpu.SemaphoreType.DMA((7 * n,)),
                        pltpu.SemaphoreType.DMA((n,))],
    )(*shards)


def _slot(ref, p):
    return ref.at[4 * p[0] + 2 * p[1] + p[2]]


def _gather_round1(shards):
    n = len(shards)

    def plan(ins, outs, send_sems, recv_sems):
        x, y, c = _position()
        me = (x, y, c)
        peers = [(x, y, 1 - c), (1 - x, y, c), (x, 1 - y, c), (1 - x, 1 - y, c)]
        sends, recvs, local = [], [], []
        for a in range(n):
            local.append(pltpu.make_async_copy(ins[a], _slot(outs[a], me), send_sems.at[4 * n + a]))
            for k, peer in enumerate(peers):
                sems = dict(send_sem=send_sems.at[4 * a + k], recv_sem=recv_sems.at[4 * a + k],
                            device_id=peer, device_id_type=MESH)
                sends.append(pltpu.make_async_remote_copy(src_ref=ins[a], dst_ref=_slot(outs[a], me), **sems))
                recvs.append(pltpu.make_async_remote_copy(src_ref=ins[a], dst_ref=_slot(outs[a], peer), **sems))
        return sends, recvs, local

    def start(*refs):
        sends, _, local = plan(*refs)
        for cp in local + sends:
            cp.start()

    def finish(*refs):
        sends, recvs, local = plan(*refs)
        for cp in recvs:
            cp.wait_recv()
        for cp in sends:
            cp.wait_send()
        for cp in local:
            cp.wait()

    return _Comm(shards, [jax.ShapeDtypeStruct((N_DEV,) + t.shape, t.dtype) for t in shards], 5 * n, start, finish)


def _gather_round2(gathered):
    n = len(gathered)

    def plan(ins, outs, send_sems, recv_sems):
        x, y, c = _position()
        chips = [(1 - x, y), (x, 1 - y), (1 - x, 1 - y)]
        sends, recvs = [], []
        for a in range(n):
            for k, chip in enumerate(chips):
                sems = dict(send_sem=send_sems.at[3 * a + k], recv_sem=recv_sems.at[3 * a + k],
                            device_id=(x, y, 1 - c), device_id_type=MESH)
                sends.append(pltpu.make_async_remote_copy(
                    src_ref=_slot(ins[a], (*chip, c)), dst_ref=_slot(outs[a], (*chip, c)), **sems))
                recvs.append(pltpu.make_async_remote_copy(
                    src_ref=_slot(ins[a], (*chip, c)), dst_ref=_slot(outs[a], (*chip, 1 - c)), **sems))
        return sends, recvs

    def start(*refs):
        for cp in plan(*refs)[0]:
            cp.start()

    def finish(*refs):
        sends, recvs = plan(*refs)
        for cp in recvs:
            cp.wait_recv()
        for cp in sends:
            cp.wait_send()

    return _Comm(gathered, [jax.ShapeDtypeStruct(t.shape, t.dtype) for t in gathered], 3 * n, start, finish,
                 aliases={a: a for a in range(n)})


def _gather_two_level(shards, forward_at):
    n = len(shards)
    first, second = _gather_round1(shards), _gather_round2(shards)

    def middle(ins, outs, send_sems, recv_sems):
        first.finish(ins, outs, send_sems, recv_sems)
        second.start(outs, outs, _SemWindow(send_sems, first.n_sems), _SemWindow(recv_sems, first.n_sems))

    def finish(ins, outs, send_sems, recv_sems):
        second.finish(outs, outs, _SemWindow(send_sems, first.n_sems), _SemWindow(recv_sems, first.n_sems))

    return _Comm(shards, first.out_shape, first.n_sems + second.n_sems, first.start, finish,
                 middle=middle, middle_at=forward_at)


def _sibling_exchange(grads):
    n = len(grads)

    def plan(ins, outs, send_sems, recv_sems):
        x, y, c = _position()
        return [pltpu.make_async_remote_copy(
            src_ref=ins[a].at[2 * q + (1 - c)], dst_ref=outs[a].at[q],
            send_sem=send_sems.at[a * 4 + q], recv_sem=recv_sems.at[a * 4 + q],
            device_id=(x, y, 1 - c), device_id_type=MESH) for a in range(n) for q in range(4)]

    def start(*refs):
        for cp in plan(*refs):
            cp.start()

    def finish(*refs):
        for cp in plan(*refs):
            cp.wait()

    return _Comm(grads, [jax.ShapeDtypeStruct((4,) + t.shape[1:], t.dtype) for t in grads], 4 * n, start, finish)


def _chip_exchange(partials):
    n = len(partials)

    def plan(ins, outs, send_sems, recv_sems):
        x, y, c = _position()
        chips = [(1 - x, y), (x, 1 - y), (1 - x, 1 - y)]
        return [pltpu.make_async_remote_copy(
            src_ref=ins[a].at[2 * chip[0] + chip[1]], dst_ref=outs[a].at[k],
            send_sem=send_sems.at[a * 3 + k], recv_sem=recv_sems.at[a * 3 + k],
            device_id=(*chip, c), device_id_type=MESH) for a in range(n) for k, chip in enumerate(chips)]

    def start(*refs):
        for cp in plan(*refs):
            cp.start()

    def finish(*refs):
        for cp in plan(*refs):
            cp.wait()

    return _Comm(partials, [jax.ShapeDtypeStruct((3,) + t.shape[1:], t.dtype) for t in partials], 3 * n, start, finish)


class _SemWindow:
    def __init__(self, sems, offset):
        self._sems, self._offset = sems, offset

    @property
    def at(self):
        return self

    def __getitem__(self, i):
        return self._sems.at[self._offset + i]


def _join(parts):
    def each(fn_name, cins, couts, send_sems, recv_sems):
        i = o = sem = 0
        for p in parts:
            ni, no = len(p.operands), len(p.out_shape)
            getattr(p, fn_name)(cins[i:i + ni], couts[o:o + no], _SemWindow(send_sems, sem), _SemWindow(recv_sems, sem))
            i, o, sem = i + ni, o + no, sem + p.n_sems

    assert not any(p.aliases for p in parts)
    return _Comm([t for p in parts for t in p.operands], [t for p in parts for t in p.out_shape],
                 sum(p.n_sems for p in parts), functools.partial(each, "start"), functools.partial(each, "finish"))


def _pair_sum_call(grad, recv, parity, name):
    _, r, ccols = grad.shape
    tr = _tile(r, 256)

    def body(par_ref, g_ref, r_ref, p_ref, pb_ref):
        del par_ref
        p = g_ref[...] + r_ref[...]
        p_ref[...] = p
        pb_ref[...] = p.astype(BF)

    blk = lambda fn: pl.BlockSpec((None, tr, ccols), fn)
    return pl.pallas_call(
        body, name=name,
        grid_spec=pltpu.PrefetchScalarGridSpec(
            num_scalar_prefetch=1, grid=(4, r // tr),
            in_specs=[blk(lambda q, i, par: (2 * q + par[0], i, 0)), blk(lambda q, i, par: (q, i, 0))],
            out_specs=[blk(lambda q, i, par: (q, i, 0)), blk(lambda q, i, par: (q, i, 0))]),
        out_shape=[jax.ShapeDtypeStruct((4, r, ccols), F32), jax.ShapeDtypeStruct((4, r, ccols), BF)],
        compiler_params=_params("parallel", "parallel"),
    )(parity, grad, recv)


def _adamw_math(w, g, m, v):
    m = ADAM_B1 * m + (1.0 - ADAM_B1) * g
    v = ADAM_B2 * v + (1.0 - ADAM_B2) * (g * g)
    m_hat = m / (1.0 - ADAM_B1 ** ADAM_STEP)
    v_hat = v / (1.0 - ADAM_B2 ** ADAM_STEP)
    delta = -ADAM_LR * (m_hat / (jnp.sqrt(v_hat) + ADAM_EPS) + ADAM_WD * w)
    return delta, m, v


def _adamw_matrix_call(partial, recv, chip, w, m, v, name):
    r, ccols = w.shape
    gcols = partial.shape[2]
    tr = _tile(r, 256)

    def body(chip_ref, p_ref, r_ref, w_ref, m_ref, v_ref, g_out, d_out, m_out, v_out):
        del chip_ref
        cols = pl.ds(0, ccols)
        g = (p_ref[:, cols] + r_ref[0, :, cols].astype(F32) + r_ref[1, :, cols].astype(F32)
             + r_ref[2, :, cols].astype(F32))
        delta, mn, vn = _adamw_math(w_ref[...], g, m_ref[...], v_ref[...])
        g_out[...] = g
        d_out[...] = delta
        m_out[...] = mn
        v_out[...] = vn

    mat = pl.BlockSpec((tr, ccols), lambda i, ch: (i, 0))
    shp = jax.ShapeDtypeStruct((r, ccols), F32)
    return pl.pallas_call(
        body, name=name,
        grid_spec=pltpu.PrefetchScalarGridSpec(
            num_scalar_prefetch=1, grid=(r // tr,),
            in_specs=[pl.BlockSpec((None, tr, gcols), lambda i, ch: (ch[0], i, 0)),
                      pl.BlockSpec((3, tr, gcols), lambda i, ch: (0, i, 0)), mat, mat, mat],
            out_specs=[mat, mat, mat, mat]),
        out_shape=[shp, shp, shp, shp],
        compiler_params=_params("parallel"),
    )(chip, partial, recv, w, m, v)


def _adamw_vector_call(gathered, w, m, v, name):
    n = w.shape[1]

    def body(p_ref, w_ref, m_ref, v_ref, g_out, d_out, m_out, v_out):
        g = p_ref[0:1, :]
        for k in range(1, N_DEV):
            g = g + p_ref[k:k + 1, :]
        delta, mn, vn = _adamw_math(w_ref[...], g, m_ref[...], v_ref[...])
        g_out[...] = g
        d_out[...] = delta
        m_out[...] = mn
        v_out[...] = vn

    shp = jax.ShapeDtypeStruct((1, n), F32)
    return pl.pallas_call(body, name=name, out_shape=[shp, shp, shp, shp])(gathered, w, m, v)


def _round_up(n, mult):
    return (n + mult - 1) // mult * mult


def kernel(x, ffn1_norm, ffn1_w_gate, ffn1_w_up, ffn1_w_down, mix_norm, w_in, ret_norm_g, hgrn_lb_logits, hgrn_norm_g, w_out, ffn2_norm, ffn2_w_gate, ffn2_w_up, ffn2_w_down, final_norm, loss_target, m_ffn1_norm, m_ffn1_w_gate, m_ffn1_w_up, m_ffn1_w_down, m_mix_norm, m_w_in, m_ret_norm_g, m_hgrn_lb_logits, m_hgrn_norm_g, m_w_out, m_ffn2_norm, m_ffn2_w_gate, m_ffn2_w_up, m_ffn2_w_down, m_final_norm, v_ffn1_norm, v_ffn1_w_gate, v_ffn1_w_up, v_ffn1_w_down, v_mix_norm, v_w_in, v_ret_norm_g, v_hgrn_lb_logits, v_hgrn_norm_g, v_w_out, v_ffn2_norm, v_ffn2_w_gate, v_ffn2_w_up, v_ffn2_w_down, v_final_norm):
    xs = x[0]
    target = loss_target[0]
    s, d = xs.shape
    f_loc = ffn1_w_gate.shape[2]
    fp = _round_up(f_loc, LANE)
    pad_cols = lambda t: jnp.pad(t[0], ((0, 0), (0, fp - f_loc)))
    pad_rows = lambda t: jnp.pad(t[0], ((0, fp - f_loc), (0, 0)))

    mat_names = ["ffn1_w_gate", "ffn1_w_up", "ffn1_w_down", "w_in", "w_out", "ffn2_w_gate", "ffn2_w_up", "ffn2_w_down"]
    mat_pad = [pad_cols, pad_cols, pad_rows, lambda t: t[0], lambda t: t[0], pad_cols, pad_cols, pad_rows]
    mat_w = [ffn1_w_gate, ffn1_w_up, ffn1_w_down, w_in, w_out, ffn2_w_gate, ffn2_w_up, ffn2_w_down]
    mat_m = [m_ffn1_w_gate, m_ffn1_w_up, m_ffn1_w_down, m_w_in, m_w_out, m_ffn2_w_gate, m_ffn2_w_up, m_ffn2_w_down]
    mat_v = [v_ffn1_w_gate, v_ffn1_w_up, v_ffn1_w_down, v_w_in, v_w_out, v_ffn2_w_gate, v_ffn2_w_up, v_ffn2_w_down]

    cx, cy, cc = _position()
    parity = jnp.reshape(cc, (1,)).astype(jnp.int32)
    chip = jnp.reshape(2 * cx + cy, (1,)).astype(jnp.int32)
    mat_index = {nm: i for i, nm in enumerate(mat_names)}
    mat_out = {}

    def pair_sums(names, grads, from_sibling):
        return [_pair_sum_call(g, r, parity, "pair_sum_" + nm) for nm, g, r in zip(names, grads, from_sibling)]

    def update(names, sums, from_chips):
        for nm, (p, _), r in zip(names, sums, from_chips):
            i = mat_index[nm]
            res = _adamw_matrix_call(p, r, chip, mat_w[i][0], mat_m[i][0], mat_v[i][0], "adamw_" + nm)
            mat_out[nm] = [t[None] for t in res]

    shards = [p(t.astype(BF)) for p, t in zip(mat_pad, mat_w)]
    wg1, wu1 = _all_gather_call(shards[:2], "gather_ffn1_up")

    h1 = _rmsnorm_call(xs, ffn1_norm, "ffn1_norm")
    (g1, u1, a1), (wd1, win) = _ffn_up_call(h1, wg1, wu1, "ffn1_up", comm=_gather_two_level(shards[2:4], 0.7))
    (x1, h2), _ = _down_call(a1, wd1, xs, mix_norm, FFN_RESIDUAL_WEIGHT, "ffn1_down")
    (proj,), _ = _proj_call(h2, win, "mix_in")
    wmix = proj.shape[1] // 8
    cos, sin = _rope_tables(s, wmix // RET_HEADS)
    consts = _ret_consts(wmix // RET_HEADS)
    (o_hg, m_hg, st_hg), landed = _hgrn_fwd_call(proj, hgrn_lb_logits, hgrn_norm_g, "hgrn_fwd",
                                                 comm=_gather_round1(shards[4:]))
    (o_ret, m_ret, st_ret), (wout, wg2, wu2, wd2) = _ret_fwd_call(proj, cos, sin, consts, ret_norm_g, "ret_fwd",
                                                                  comm=_gather_round2(landed))
    merged = jnp.concatenate([m_ret, m_hg], axis=1)
    wout_wide = wout.reshape(2, wout.shape[0] * wout.shape[1] // 2, d)
    (x2, h3), _ = _down_call(merged, wout_wide, x1, ffn2_norm, 1.0, "mix_out")
    (g2, u2, a2), _ = _ffn_up_call(h3, wg2, wu2, "ffn2_up")
    (x3,), _ = _down_call(a2, wd2, x2, None, FFN_RESIDUAL_WEIGHT, "ffn2_down")
    loss_part, dx3, dx3b, gv_final = _loss_call(x3, target, final_norm[None, :], "loss_head")

    (dg2, du2), _ = _bwd_up_call(dx3b, wd2, g2, u2, FFN_RESIDUAL_WEIGHT, "ffn2_bwd_up")
    (dx2, dx2b, gv_n3), _ = _bwd_down_call([(dg2, wg2), (du2, wu2)], dx3, x2, ffn2_norm, "ffn2_bwd_down")
    names_a = ["ffn2_w_gate", "ffn2_w_up", "ffn2_w_down"]
    grads_a = [_wgrad_call(h3, dg2, N_DEV, False, 1.0, "ffn2_wgrad_gate"),
               _wgrad_call(h3, du2, N_DEV, False, 1.0, "ffn2_wgrad_up"),
               _wgrad_call(a2, dx3b, N_DEV, True, FFN_RESIDUAL_WEIGHT, "ffn2_wgrad_down")]

    (dmerged,), sib_a = _proj_call(dx2b, jnp.swapaxes(wout_wide, 1, 2), "mix_out_bwd",
                                   comm=_sibling_exchange(grads_a))
    gm_out = _wgrad_call(merged, dx2b, 2, True, 1.0, "mix_out_wgrad").reshape(wout.shape)
    sums_a = pair_sums(names_a, grads_a, sib_a)
    drq, drk, drv, drg, gv_ret = _ret_bwd_call(proj, cos, sin, consts, ret_norm_g, o_ret, st_ret, dmerged, "ret_bwd")
    (dhq, dhf, dhi, dhg, gv_lb, gv_hg), landed = _hgrn_bwd_call(
        proj, hgrn_lb_logits, hgrn_norm_g, o_hg, st_hg, dmerged, "hgrn_bwd",
        comm=_join([_chip_exchange([pb for _, pb in sums_a]), _sibling_exchange([gm_out])]))
    update(names_a, sums_a, landed[:3])
    sums_out = pair_sums(["w_out"], [gm_out], landed[3:])
    dproj = jnp.concatenate([drq, drk, drv, drg, dhq, dhf, dhi, dhg], axis=1)
    (dx1, dx1b, gv_n2), chips_out = _bwd_down_call([(dproj, win)], dx2, x1, mix_norm, "mix_in_bwd",
                                                   comm=_chip_exchange([sums_out[0][1]]))
    update(["w_out"], sums_out, chips_out)

    gm_in = _wgrad_call(h2, dproj, N_DEV, False, 1.0, "mix_in_wgrad")
    gm_d1, sib_in = _wgrad_call(a1, dx1b, N_DEV, True, FFN_RESIDUAL_WEIGHT, "ffn1_wgrad_down",
                                comm=_sibling_exchange([gm_in]))
    sums_in = pair_sums(["w_in"], [gm_in], sib_in)
    (dg1, du1), landed = _bwd_up_call(dx1b, wd1, g1, u1, FFN_RESIDUAL_WEIGHT, "ffn1_bwd_up",
                                      comm=_join([_chip_exchange([sums_in[0][1]]), _sibling_exchange([gm_d1])]))
    update(["w_in"], sums_in, landed[:1])
    sums_d1 = pair_sums(["ffn1_w_down"], [gm_d1], landed[1:])
    gm_g1, chips_d1 = _wgrad_call(h1, dg1, N_DEV, False, 1.0, "ffn1_wgrad_gate",
                                  comm=_chip_exchange([sums_d1[0][1]]))
    update(["ffn1_w_down"], sums_d1, chips_d1)
    gm_u1, sib_g = _wgrad_call(h1, du1, N_DEV, False, 1.0, "ffn1_wgrad_up", comm=_sibling_exchange([gm_g1]))
    sums_g = pair_sums(["ffn1_w_gate"], [gm_g1], sib_g)
    (dx0, _, gv_n1), landed = _bwd_down_call(
        [(dg1, wg1), (du1, wu1)], dx1, xs, ffn1_norm, "ffn1_bwd_down",
        comm=_join([_chip_exchange([sums_g[0][1]]), _sibling_exchange([gm_u1])]))
    update(["ffn1_w_gate"], sums_g, landed[:1])
    sums_u = pair_sums(["ffn1_w_up"], [gm_u1], landed[1:])
    update(["ffn1_w_up"], sums_u, _comm_only_call(_chip_exchange([sums_u[0][1]]), "ffn1_up_grads_to_chips"))

    vec_names = ["ffn1_norm", "mix_norm", "ret_norm_g", "hgrn_lb_logits", "hgrn_norm_g", "ffn2_norm", "final_norm"]
    vec_g = [gv_n1, gv_n2, gv_ret, gv_lb, gv_hg, gv_n3, gv_final]
    vec_w = [ffn1_norm, mix_norm, ret_norm_g, hgrn_lb_logits, hgrn_norm_g, ffn2_norm, final_norm[None, :]]
    vec_m = [m_ffn1_norm, m_mix_norm, m_ret_norm_g, m_hgrn_lb_logits, m_hgrn_norm_g, m_ffn2_norm, m_final_norm[None, :]]
    vec_v = [v_ffn1_norm, v_mix_norm, v_ret_norm_g, v_hgrn_lb_logits, v_hgrn_norm_g, v_ffn2_norm, v_final_norm[None, :]]
    cat = lambda ts: jnp.concatenate(ts, axis=1)
    (vec_all,) = _all_gather_call([cat(vec_g)], "gather_vector_grads")
    vres = _adamw_vector_call(vec_all[:, 0, :], cat(vec_w), cat(vec_m), cat(vec_v), "adamw_vectors")
    vec_out = {}
    off = 0
    for nm, t in zip(vec_names, vec_w):
        n = t.shape[1]
        parts = [r[:, off:off + n] for r in vres]
        if nm == "final_norm":
            parts = [p[0] for p in parts]
        vec_out[nm] = parts
        off += n

    loss = lax.psum(loss_part[0, 0], ("x", "y", "c"))
    order = ["ffn1_norm", "ffn1_w_gate", "ffn1_w_up", "ffn1_w_down", "mix_norm", "w_in", "ret_norm_g", "hgrn_lb_logits",
             "hgrn_norm_g", "w_out", "ffn2_norm", "ffn2_w_gate", "ffn2_w_up", "ffn2_w_down", "final_norm"]
    res = {**mat_out, **vec_out}
    outs = [loss, dx0[None]]
    for kind in range(4):
        outs += [res[nm][kind] for nm in order]
    return tuple(outs)
```

```python
import functools

import jax
import jax.numpy as jnp
from jax import lax
from jax.experimental import pallas as pl
from jax.experimental.pallas import tpu as pltpu

BF = jnp.bfloat16
F32 = jnp.float32
MESH = pl.DeviceIdType.MESH
HBM_SPEC = pl.BlockSpec(memory_space=pltpu.HBM)

N_DEV = 8
LANE = 128
EPS = 1e-6
ROPE_BASE = 10000.0
RET_HEADS = 4
HGRN_HEADS = 8
RET_CHUNK = 128
HGRN_BLOCK = 16
FFN_RESIDUAL_WEIGHT = 0.5
ADAM_LR = 0.001
ADAM_B1 = 0.9
ADAM_B2 = 0.999
ADAM_EPS = 1e-08
ADAM_WD = 0.01
ADAM_STEP = 10
VMEM_LIMIT = 56 * 1024 * 1024


def _tile(n, pref, mult=8):
    t = min(pref, n)
    t -= t % mult
    while t >= mult:
        if n % t == 0:
            return t
        t -= mult
    return n


def _params(*sem):
    return pltpu.CompilerParams(dimension_semantics=sem, vmem_limit_bytes=VMEM_LIMIT)


class _Comm:
    def __init__(self, operands, out_shape, n_sems, start, finish, aliases=None, middle=None, middle_at=0.0):
        self.operands = list(operands)
        self.out_shape = list(out_shape)
        self.n_sems = n_sems
        self.start = start
        self.finish = finish
        self.aliases = dict(aliases or {})
        self.middle = middle
        self.middle_at = middle_at


def _launch(body, *, name, grid, in_specs, out_specs, out_shape, sem, args, scratch_shapes=(), aliases=None, comm=None):
    in_specs, out_specs, out_shape = list(in_specs), list(out_specs), list(out_shape)
    scratch_shapes = list(scratch_shapes)
    aliases = dict(aliases or {})
    if comm is None:
        res = pl.pallas_call(body, name=name, grid=grid, in_specs=in_specs, out_specs=out_specs, out_shape=out_shape,
                             scratch_shapes=scratch_shapes, input_output_aliases=aliases,
                             compiler_params=_params(*sem))(*args)
        return list(res), []
    n_in, n_out, n_scr = len(in_specs), len(out_specs), len(scratch_shapes)
    ci, co = len(comm.operands), len(comm.out_shape)

    def carrying(*refs):
        bounds = [0, n_in, n_in + ci, n_in + ci + n_out, n_in + ci + n_out + co, n_in + ci + n_out + co + n_scr]
        ins, cins, outs, couts, scr = [refs[a:b] for a, b in zip(bounds[:-1], bounds[1:])]
        send_sems, recv_sems = refs[bounds[-1]:]
        ids = [pl.program_id(k) for k in range(len(grid))]
        first = functools.reduce(jnp.logical_and, [i == 0 for i in ids])
        last = functools.reduce(jnp.logical_and, [i == g - 1 for i, g in zip(ids, grid)])

        @pl.when(first)
        def _():
            comm.start(cins, couts, send_sems, recv_sems)

        if comm.middle is not None:
            step, total = ids[0], grid[0]
            for i, g in zip(ids[1:], grid[1:]):
                step, total = step * g + i, total * g

            @pl.when(step == int(total * comm.middle_at))
            def _():
                comm.middle(cins, couts, send_sems, recv_sems)

        body(*ins, *outs, *scr)

        @pl.when(last)
        def _():
            comm.finish(cins, couts, send_sems, recv_sems)

    res = pl.pallas_call(
        carrying, name=name, grid=grid,
        in_specs=in_specs + [HBM_SPEC] * ci, out_specs=out_specs + [HBM_SPEC] * co,
        out_shape=out_shape + comm.out_shape,
        scratch_shapes=scratch_shapes + [pltpu.SemaphoreType.DMA((comm.n_sems,)), pltpu.SemaphoreType.DMA((comm.n_sems,))],
        input_output_aliases={**aliases, **{n_in + a: n_out + b for a, b in comm.aliases.items()}},
        compiler_params=_params(*(["arbitrary"] * len(grid))),
    )(*args, *comm.operands)
    return list(res[:n_out]), list(res[n_out:])


def _comm_only_call(comm, name):
    def body(*refs):
        ci, co = len(comm.operands), len(comm.out_shape)
        cins, couts = refs[:ci], refs[ci:ci + co]
        send_sems, recv_sems = refs[ci + co:]
        comm.start(cins, couts, send_sems, recv_sems)
        comm.finish(cins, couts, send_sems, recv_sems)

    return pl.pallas_call(
        body, name=name,
        in_specs=[HBM_SPEC] * len(comm.operands), out_specs=[HBM_SPEC] * len(comm.out_shape),
        out_shape=comm.out_shape,
        scratch_shapes=[pltpu.SemaphoreType.DMA((comm.n_sems,)), pltpu.SemaphoreType.DMA((comm.n_sems,))],
        input_output_aliases=comm.aliases,
    )(*comm.operands)


def _sigmoid(v):
    return 1.0 / (1.0 + jnp.exp(-v))


def _dot(a, b):
    return jnp.dot(a, b, preferred_element_type=F32)


def _dot_nt(a, b):
    return lax.dot_general(a, b, (((1,), (1,)), ((), ())), preferred_element_type=F32)


def _dot_tn(a, b):
    return lax.dot_general(a, b, (((0,), (0,)), ((), ())), preferred_element_type=F32)


def _rmsnorm_call(x, gain, name):
    s, d = x.shape
    tm = _tile(s, 512)

    def body(x_ref, g_ref, o_ref):
        xv = x_ref[...]
        r = lax.rsqrt(jnp.mean(xv * xv, axis=-1, keepdims=True) + EPS)
        o_ref[...] = (xv * r * g_ref[...]).astype(BF)

    return pl.pallas_call(
        body, name=name, grid=(s // tm,),
        in_specs=[pl.BlockSpec((tm, d), lambda i: (i, 0)), pl.BlockSpec((1, d), lambda i: (0, 0))],
        out_specs=pl.BlockSpec((tm, d), lambda i: (i, 0)),
        out_shape=jax.ShapeDtypeStruct((s, d), BF),
        compiler_params=_params("parallel"),
    )(x, gain)


def _ffn_up_call(h, wg, wu, name, comm=None):
    s, d = h.shape
    nj, _, k = wg.shape
    tm = _tile(s, 512)

    def body(h_ref, wg_ref, wu_ref, g_ref, u_ref, a_ref):
        hv = h_ref[...]
        g = _dot(hv, wg_ref[...])
        u = _dot(hv, wu_ref[...])
        g_ref[...] = g
        u_ref[...] = u
        a_ref[...] = (g * _sigmoid(g) * u).astype(BF)

    act = pl.BlockSpec((tm, k), lambda i, j: (i, j))
    wsp = pl.BlockSpec((None, d, k), lambda i, j: (j, 0, 0))
    return _launch(
        body, name=name, grid=(s // tm, nj),
        in_specs=[pl.BlockSpec((tm, d), lambda i, j: (i, 0)), wsp, wsp],
        out_specs=[act, act, act],
        out_shape=[jax.ShapeDtypeStruct((s, nj * k), F32), jax.ShapeDtypeStruct((s, nj * k), F32),
                   jax.ShapeDtypeStruct((s, nj * k), BF)],
        sem=("parallel", "arbitrary"), args=(h, wg, wu), comm=comm)


def _proj_call(h, w, name, comm=None):
    s, d = h.shape
    nj, _, k = w.shape
    tm = _tile(s, 512)

    def body(h_ref, w_ref, o_ref):
        o_ref[...] = _dot(h_ref[...], w_ref[...])

    return _launch(
        body, name=name, grid=(s // tm, nj),
        in_specs=[pl.BlockSpec((tm, d), lambda i, j: (i, 0)), pl.BlockSpec((None, d, k), lambda i, j: (j, 0, 0))],
        out_specs=[pl.BlockSpec((tm, k), lambda i, j: (i, j))],
        out_shape=[jax.ShapeDtypeStruct((s, nj * k), F32)],
        sem=("parallel", "arbitrary"), args=(h, w), comm=comm)


def _down_call(a, w, resid, gain, scale, name, comm=None):
    s = a.shape[0]
    nj, k, d = w.shape
    tm = _tile(s, 512)
    cn = _tile(d, 512, LANE)
    with_norm = gain is not None

    def body(*refs):
        if with_norm:
            a_ref, w_ref, r_ref, g_ref, x_ref, h_ref, acc = refs
        else:
            a_ref, w_ref, r_ref, x_ref, acc = refs
        j = pl.program_id(1)

        @pl.when(j == 0)
        def _():
            acc[...] = jnp.zeros_like(acc)

        av = a_ref[...]
        for n0 in range(0, d, cn):
            acc[:, n0:n0 + cn] += _dot(av, w_ref[:, n0:n0 + cn])

        @pl.when(j == nj - 1)
        def _():
            xn = r_ref[...] + (scale * acc[...])
            x_ref[...] = xn
            if with_norm:
                r = lax.rsqrt(jnp.mean(xn * xn, axis=-1, keepdims=True) + EPS)
                h_ref[...] = (xn * r * g_ref[...]).astype(BF)

    row = pl.BlockSpec((tm, d), lambda i, j: (i, 0))
    in_specs = [pl.BlockSpec((tm, k), lambda i, j: (i, j)), pl.BlockSpec((None, k, d), lambda i, j: (j, 0, 0)), row]
    args = [a, w, resid]
    out_specs = [row]
    out_shape = [jax.ShapeDtypeStruct((s, d), F32)]
    if with_norm:
        in_specs.append(pl.BlockSpec((1, d), lambda i, j: (0, 0)))
        args.append(gain)
        out_specs.append(row)
        out_shape.append(jax.ShapeDtypeStruct((s, d), BF))
    return _launch(
        body, name=name, grid=(s // tm, nj),
        in_specs=in_specs, out_specs=out_specs, out_shape=out_shape,
        scratch_shapes=[pltpu.VMEM((tm, d), F32)],
        sem=("parallel", "arbitrary"), args=args, comm=comm)


def _loss_call(x, target, gain, name):
    s, d = x.shape
    tm = _tile(s, 512)

    def body(x_ref, t_ref, g_ref, loss_ref, dx_ref, dxb_ref, dg_ref):
        i = pl.program_id(0)

        @pl.when(i == 0)
        def _():
            loss_ref[...] = jnp.zeros_like(loss_ref)
            dg_ref[...] = jnp.zeros_like(dg_ref)

        xv = x_ref[...]
        gv = g_ref[...]
        r = lax.rsqrt(jnp.mean(xv * xv, axis=-1, keepdims=True) + EPS)
        xhat = xv * r
        err = xhat * gv - t_ref[...]
        per_tok = jnp.mean(err * err, axis=-1, keepdims=True)
        loss_ref[...] += 0.5 * jnp.sum(per_tok, axis=0, keepdims=True)
        dout = err * (1.0 / d)
        dg_ref[...] += jnp.sum(dout * xhat, axis=0, keepdims=True)
        dxhat = dout * gv
        dx = r * (dxhat - xhat * jnp.mean(dxhat * xhat, axis=-1, keepdims=True))
        dx_ref[...] = dx
        dxb_ref[...] = dx.astype(BF)

    row = pl.BlockSpec((tm, d), lambda i: (i, 0))
    vec = pl.BlockSpec((1, d), lambda i: (0, 0))
    return pl.pallas_call(
        body, name=name, grid=(s // tm,),
        in_specs=[row, row, vec],
        out_specs=[pl.BlockSpec((1, 1), lambda i: (0, 0)), row, row, vec],
        out_shape=[jax.ShapeDtypeStruct((1, 1), F32), jax.ShapeDtypeStruct((s, d), F32),
                   jax.ShapeDtypeStruct((s, d), BF), jax.ShapeDtypeStruct((1, d), F32)],
        compiler_params=_params("arbitrary"),
    )(x, target, gain)


def _bwd_up_call(dy, wd, g, u, scale, name, comm=None):
    s, d = dy.shape
    nj, k, _ = wd.shape
    tm = _tile(s, 512)

    cw = _tile(k, 2 * LANE, LANE)

    def body(dy_ref, w_ref, g_ref, u_ref, dg_ref, du_ref):
        dy = dy_ref[...]
        for c0 in range(0, k, cw):
            cols = slice(c0, c0 + cw)
            da = scale * _dot_nt(dy, w_ref[cols, :])
            gv = g_ref[:, cols]
            sig = _sigmoid(gv)
            du_ref[:, cols] = (da * gv * sig).astype(BF)
            dg_ref[:, cols] = (da * u_ref[:, cols] * sig * (1.0 + gv * (1.0 - sig))).astype(BF)

    act = pl.BlockSpec((tm, k), lambda i, j: (i, j))
    return _launch(
        body, name=name, grid=(s // tm, nj),
        in_specs=[pl.BlockSpec((tm, d), lambda i, j: (i, 0)), pl.BlockSpec((None, k, d), lambda i, j: (j, 0, 0)), act, act],
        out_specs=[act, act],
        out_shape=[jax.ShapeDtypeStruct((s, nj * k), BF), jax.ShapeDtypeStruct((s, nj * k), BF)],
        sem=("parallel", "arbitrary"), args=(dy, wd, g, u), comm=comm)


def _bwd_down_call(pairs, dres, xin, gain, name, per_step=1, comm=None):
    s, d = xin.shape
    nblocks, _, k = pairs[0][1].shape
    nj = nblocks // per_step
    npair = len(pairs) * per_step
    tm = _tile(s, 512)
    strip = _tile(tm, 128)
    cn = _tile(d, 512, LANE)

    def body(*refs):
        a_refs = refs[0:2 * npair:2]
        w_refs = refs[1:2 * npair:2]
        dres_ref, x_ref, g_ref, dx_ref, dxb_ref, dg_ref, acc = refs[2 * npair:]
        i = pl.program_id(0)
        j = pl.program_id(1)

        @pl.when(j == 0)
        def _():
            acc[...] = jnp.zeros_like(acc)

        @pl.when((i == 0) & (j == 0))
        def _():
            dg_ref[...] = jnp.zeros_like(dg_ref)

        for a_ref, w_ref in zip(a_refs, w_refs):
            av = a_ref[...]
            for n0 in range(0, d, cn):
                acc[:, n0:n0 + cn] += _dot_nt(av, w_ref[n0:n0 + cn, :])

        @pl.when(j == nj - 1)
        def _():
            for r0 in range(0, tm, strip):
                rows = slice(r0, r0 + strip)
                xv = x_ref[rows, :]
                r = lax.rsqrt(jnp.mean(xv * xv, axis=-1, keepdims=True) + EPS)
                xhat = xv * r
                dh = acc[rows, :]
                dg_ref[...] += jnp.sum(dh * xhat, axis=0, keepdims=True)
                dxhat = dh * g_ref[...]
                dx = dres_ref[rows, :] + r * (dxhat - xhat * jnp.mean(dxhat * xhat, axis=-1, keepdims=True))
                dx_ref[rows, :] = dx
                dxb_ref[rows, :] = dx.astype(BF)

    row = pl.BlockSpec((tm, d), lambda i, j: (i, 0))
    vec = pl.BlockSpec((1, d), lambda i, j: (0, 0))
    in_specs, args = [], []
    for a, w in pairs:
        for r in range(per_step):
            in_specs += [pl.BlockSpec((None, tm, k), lambda i, j, r=r: (j * per_step + r, i, 0)) if a.ndim == 3 else
                         pl.BlockSpec((tm, k), lambda i, j, r=r: (i, j * per_step + r)),
                         pl.BlockSpec((None, d, k), lambda i, j, r=r: (j * per_step + r, 0, 0))]
            args += [a, w]
    once = pl.BlockSpec((tm, d), lambda i, j: (i, 0), pipeline_mode=pl.Buffered(1))
    in_specs += [once, once, vec]
    args += [dres, xin, gain]
    return _launch(
        body, name=name, grid=(s // tm, nj),
        in_specs=in_specs, out_specs=[row, row, vec],
        out_shape=[jax.ShapeDtypeStruct((s, d), F32), jax.ShapeDtypeStruct((s, d), BF), jax.ShapeDtypeStruct((1, d), F32)],
        scratch_shapes=[pltpu.VMEM((tm, d), F32)],
        sem=("arbitrary", "arbitrary"), args=args, comm=comm)


def _wgrad_call(a, b, nj, a_blocked, scale, name, comm=None):
    s = a.shape[0]
    ka = a.shape[1] // nj if a_blocked else a.shape[1]
    b_stacked = b.ndim == 3
    kb = b.shape[-1] if (a_blocked or b_stacked) else b.shape[1] // nj
    ts = _tile(s, 2048)
    ns = s // ts

    def body(a_ref, b_ref, o_ref):
        t = pl.program_id(1)

        @pl.when(t == 0)
        def _():
            o_ref[...] = jnp.zeros_like(o_ref)

        o_ref[...] += _dot_tn(a_ref[...], b_ref[...])
        if scale != 1.0:
            @pl.when(t == ns - 1)
            def _():
                o_ref[...] = o_ref[...] * scale

    a_spec = pl.BlockSpec((ts, ka), (lambda j, t: (t, j)) if a_blocked else (lambda j, t: (t, 0)))
    if b_stacked:
        b_spec = pl.BlockSpec((None, ts, kb), lambda j, t: (j, t, 0))
    else:
        b_spec = pl.BlockSpec((ts, kb), (lambda j, t: (t, 0)) if a_blocked else (lambda j, t: (t, j)))
    (out,), landed = _launch(
        body, name=name, grid=(nj, ns),
        in_specs=[a_spec, b_spec],
        out_specs=[pl.BlockSpec((None, ka, kb), lambda j, t: (j, 0, 0))],
        out_shape=[jax.ShapeDtypeStruct((nj, ka, kb), F32)],
        sem=("parallel", "arbitrary"), args=(a, b), comm=comm)
    return out if comm is None else (out, landed)


def _rope(v, cos, sin):
    half = v.shape[-1] // 2
    v1, v2 = v[:, :half], v[:, half:]
    return jnp.concatenate([v1 * cos - v2 * sin, v2 * cos + v1 * sin], axis=-1)


def _rope_bwd(dv, cos, sin):
    half = dv.shape[-1] // 2
    d1, d2 = dv[:, :half], dv[:, half:]
    return jnp.concatenate([d1 * cos + d2 * sin, d2 * cos - d1 * sin], axis=-1)


def _ret_consts(hd):
    c = RET_CHUNK
    log_gamma = jnp.log(1.0 - jnp.exp2(-5.0 - jnp.arange(RET_HEADS, dtype=F32)))
    idx = jnp.arange(c, dtype=F32)
    rel = idx[:, None] - idx[None, :]
    mask = rel >= 0
    decay = jnp.where(mask[None], jnp.exp(log_gamma[:, None, None] * jnp.where(mask, rel, 0.0)[None]), 0.0)
    qdec = jnp.exp(log_gamma[:, None] * (idx + 1.0)[None, :])
    kdec = jnp.exp(log_gamma[:, None] * (c - 1.0 - idx)[None, :])
    gchunk = jnp.exp(log_gamma * c)
    bc = lambda t: jnp.broadcast_to(t[:, :, None], (RET_HEADS, t.shape[1], hd))
    return decay, bc(qdec), bc(kdec), bc(gchunk[:, None])


def _rope_tables(s, hd):
    inv = jnp.power(ROPE_BASE, -jnp.arange(0, hd, 2, dtype=F32) / hd)
    ang = jnp.arange(s, dtype=F32)[:, None] * inv[None, :]
    return jnp.cos(ang), jnp.sin(ang)


def _ret_fwd_call(proj, cos, sin, consts, gret, name, comm=None):
    s = proj.shape[0]
    w = proj.shape[1] // 8
    hd = w // RET_HEADS
    c = RET_CHUNK
    tt = _tile(s, 512, c)
    nc = tt // c
    decay, qdec, kdec, gch = consts
    scale = hd ** -0.5

    def body(q_ref, k_ref, v_ref, gate_ref, cos_ref, sin_ref, dec_ref, qd_ref, kd_ref, gc_ref, gn_ref,
             o_ref, m_ref, st_ref, state):
        @pl.when(pl.program_id(1) == 0)
        def _():
            state[...] = jnp.zeros_like(state)

        dec = dec_ref[...]
        for ci in range(nc):
            rows = slice(ci * c, (ci + 1) * c)
            cs, sn = cos_ref[rows, :], sin_ref[rows, :]
            q = _rope(q_ref[rows, :], cs, sn) * scale
            k = _rope(k_ref[rows, :], cs, sn)
            vb = v_ref[rows, :].astype(BF)
            sc = _dot_nt(q.astype(BF), k.astype(BF)) * dec
            prev = state[...]
            st_ref[ci] = prev
            o = _dot(sc.astype(BF), vb) + _dot((q * qd_ref[...]).astype(BF), prev.astype(BF))
            state[...] = gc_ref[...] * prev + _dot_tn((k * kd_ref[...]).astype(BF), vb)
            o_ref[rows, :] = o
            mu = jnp.mean(o, axis=-1, keepdims=True)
            cen = o - mu
            xhat = cen * lax.rsqrt(jnp.mean(cen * cen, axis=-1, keepdims=True) + EPS)
            gt = gate_ref[rows, :]
            m_ref[rows, :] = (xhat * gn_ref[...] * (gt * _sigmoid(gt))).astype(BF)

    nh = RET_HEADS
    comp = lambda j: pl.BlockSpec((tt, hd), lambda h, t, j=j: (t, j * nh + h))
    tab = pl.BlockSpec((tt, hd // 2), lambda h, t: (t, 0))
    per_head = lambda r: pl.BlockSpec((None, r, hd), lambda h, t: (h, 0, 0))
    return _launch(
        body, name=name, grid=(nh, s // tt),
        in_specs=[comp(0), comp(1), comp(2), comp(3), tab, tab,
                  pl.BlockSpec((None, c, c), lambda h, t: (h, 0, 0)), per_head(c), per_head(c), per_head(1),
                  pl.BlockSpec((1, hd), lambda h, t: (0, h))],
        out_specs=[pl.BlockSpec((tt, hd), lambda h, t: (t, h)), pl.BlockSpec((tt, hd), lambda h, t: (t, h)),
                   pl.BlockSpec((None, nc, hd, hd), lambda h, t: (h, t, 0, 0))],
        out_shape=[jax.ShapeDtypeStruct((s, w), F32), jax.ShapeDtypeStruct((s, w), BF),
                   jax.ShapeDtypeStruct((nh, s // c, hd, hd), F32)],
        scratch_shapes=[pltpu.VMEM((hd, hd), F32)],
        sem=("parallel", "arbitrary"),
        args=(proj, proj, proj, proj, cos, sin, decay, qdec, kdec, gch, gret), comm=comm)


def _ret_bwd_call(proj, cos, sin, consts, gret, o_raw, states, dmerged, name):
    s = proj.shape[0]
    w = proj.shape[1] // 8
    hd = w // RET_HEADS
    c = RET_CHUNK
    tt = _tile(s, 512, c)
    nc = tt // c
    nt = s // tt
    decay, qdec, kdec, gch = consts
    scale = hd ** -0.5

    def body(q_ref, k_ref, v_ref, gate_ref, cos_ref, sin_ref, dec_ref, qd_ref, kd_ref, gc_ref, gn_ref,
             o_ref, st_ref, dm_ref, dp_ref, dgn_ref, dstate):
        @pl.when(pl.program_id(1) == 0)
        def _():
            dstate[...] = jnp.zeros_like(dstate)
            dgn_ref[...] = jnp.zeros_like(dgn_ref)

        dec = dec_ref[...]
        gn = gn_ref[...]
        for ci in reversed(range(nc)):
            rows = slice(ci * c, (ci + 1) * c)
            cs, sn = cos_ref[rows, :], sin_ref[rows, :]
            q = _rope(q_ref[rows, :], cs, sn) * scale
            k = _rope(k_ref[rows, :], cs, sn)
            qb, kb = q.astype(BF), k.astype(BF)
            vb = v_ref[rows, :].astype(BF)
            sc = _dot_nt(qb, kb) * dec
            o = o_ref[rows, :]
            mu = jnp.mean(o, axis=-1, keepdims=True)
            cen = o - mu
            rstd = lax.rsqrt(jnp.mean(cen * cen, axis=-1, keepdims=True) + EPS)
            xhat = cen * rstd
            gt = gate_ref[rows, :]
            sig = _sigmoid(gt)
            sg = gt * sig
            dm = dm_ref[rows, :]
            dgn_ref[...] += jnp.sum(dm * xhat * sg, axis=0, keepdims=True)
            dp_ref[3, rows, :] = (dm * xhat * gn * sig * (1.0 + gt * (1.0 - sig))).astype(BF)
            dxhat = dm * gn * sg
            do = rstd * (dxhat - jnp.mean(dxhat, axis=-1, keepdims=True)
                         - xhat * jnp.mean(dxhat * xhat, axis=-1, keepdims=True))
            dob = do.astype(BF)
            prev = st_ref[ci]
            ds = dstate[...]
            dsb = ds.astype(BF)
            dsc = (_dot_nt(dob, vb) * dec).astype(BF)
            dq = _dot(dsc, kb) + _dot_nt(dob, prev.astype(BF)) * qd_ref[...]
            dk = _dot_tn(dsc, qb) + _dot_nt(vb, dsb) * kd_ref[...]
            dv = _dot_tn(sc.astype(BF), dob) + _dot((k * kd_ref[...]).astype(BF), dsb)
            dstate[...] = gc_ref[...] * ds + _dot_tn((q * qd_ref[...]).astype(BF), dob)
            dp_ref[0, rows, :] = _rope_bwd(dq * scale, cs, sn).astype(BF)
            dp_ref[1, rows, :] = _rope_bwd(dk, cs, sn).astype(BF)
            dp_ref[2, rows, :] = dv.astype(BF)

    nh = RET_HEADS
    rev = lambda t: nt - 1 - t
    comp = lambda j: pl.BlockSpec((tt, hd), lambda h, t, j=j: (rev(t), j * nh + h))
    tab = pl.BlockSpec((tt, hd // 2), lambda h, t: (rev(t), 0))
    per_head = lambda r: pl.BlockSpec((None, r, hd), lambda h, t: (h, 0, 0))
    head_cols = pl.BlockSpec((tt, hd), lambda h, t: (rev(t), h))
    gvec = pl.BlockSpec((1, hd), lambda h, t: (0, h))
    act = jax.ShapeDtypeStruct((s, w), BF)
    return pl.pallas_call(
        body, name=name, grid=(nh, nt),
        in_specs=[comp(0), comp(1), comp(2), comp(3), tab, tab,
                  pl.BlockSpec((None, c, c), lambda h, t: (h, 0, 0)), per_head(c), per_head(c), per_head(1), gvec,
                  head_cols, pl.BlockSpec((None, nc, hd, hd), lambda h, t: (h, rev(t), 0, 0)), head_cols],
        out_specs=[pl.BlockSpec((4, tt, hd), lambda h, t: (0, rev(t), h)), gvec],
        out_shape=[jax.ShapeDtypeStruct((8, s, w), BF), jax.ShapeDtypeStruct((1, w), F32)],
        scratch_shapes=[pltpu.VMEM((hd, hd), F32)],
        compiler_params=_params("parallel", "arbitrary"),
    )(proj, proj, proj, proj, cos, sin, decay, qdec, kdec, gch, gret, o_raw, states, dmerged)


def _block_tri(n, bs, upper):
    r = jnp.arange(n)[:, None]
    cidx = jnp.arange(n)[None, :]
    same = (r // bs) == (cidx // bs)
    return jnp.where(same & ((cidx >= r) if upper else (cidx <= r)), 1.0, 0.0).astype(F32)


def _dot_exact(a, b):
    return jnp.dot(a, b, preferred_element_type=F32, precision=lax.Precision.HIGHEST)


def _hgrn_gates(z, lbv):
    sz = _sigmoid(z)
    oml = 1.0 - lbv
    f = lbv + oml * sz
    key = oml * (1.0 - sz)
    return sz, f, key


def _hgrn_fwd_call(proj, lb_logits, ghg, name, comm=None):
    s = proj.shape[0]
    w = proj.shape[1] // 8
    nh = HGRN_HEADS
    hd = w // nh
    bs = HGRN_BLOCK
    tt = _tile(s, 256, bs)
    nb = tt // bs

    def body(q_ref, z_ref, v_ref, gate_ref, lb_ref, gn_ref, tril_ref, o_ref, m_ref, st_ref, state, upd):
        @pl.when(pl.program_id(1) == 0)
        def _():
            state[...] = jnp.zeros_like(state)

        lbv = _sigmoid(lb_ref[...])
        _, f, key = _hgrn_gates(z_ref[...], lbv)
        qr = q_ref[...]
        q = qr * _sigmoid(qr)
        v = v_ref[...]
        g = _dot_exact(tril_ref[...], jnp.log(f))
        blocks = lambda t: t.reshape(nb, bs, hd)
        g3, q3, k3, v3 = blocks(g), blocks(q), blocks(key), blocks(v)
        glast3 = g3[:, bs - 1:bs, :]
        row_id = lax.broadcasted_iota(jnp.int32, (nb, bs, hd), 1)
        o3 = jnp.zeros((nb, bs, hd), F32)
        for j in range(bs):
            wj = jnp.where(row_id >= j, jnp.exp(jnp.minimum(g3 - g3[:, j:j + 1, :], 0.0)), 0.0)
            a = jnp.sum(q3 * k3[:, j:j + 1, :] * wj, axis=-1, keepdims=True)
            o3 = o3 + a * v3[:, j:j + 1, :]
        ktb = (k3 * jnp.exp(glast3 - g3)).reshape(tt, hd).astype(BF)
        vb = v.astype(BF)
        for b in range(nb):
            rows = slice(b * bs, (b + 1) * bs)
            upd[b] = _dot_tn(vb[rows, :], ktb[rows, :])
        egl3 = jnp.exp(glast3)
        st = state[...]
        for b in range(nb):
            st_ref[b] = st
            st = st * egl3[b] + upd[b]
        state[...] = st
        qgb = (q * jnp.exp(g)).astype(BF)
        o_intra = o3.reshape(tt, hd)
        gn = gn_ref[...]
        for b in range(nb):
            rows = slice(b * bs, (b + 1) * bs)
            o = o_intra[rows, :] + _dot_nt(qgb[rows, :], st_ref[b].astype(BF))
            o_ref[rows, :] = o
            gt = gate_ref[rows, :]
            xhat = o * lax.rsqrt(jnp.mean(o * o, axis=-1, keepdims=True) + EPS)
            m_ref[rows, :] = (xhat * gn * (gt * _sigmoid(gt))).astype(BF)

    comp = lambda j: pl.BlockSpec((tt, hd), lambda h, t, j=j: (t, j * nh + h))
    gvec = pl.BlockSpec((1, hd), lambda h, t: (0, h))
    head_cols = pl.BlockSpec((tt, hd), lambda h, t: (t, h))
    return _launch(
        body, name=name, grid=(nh, s // tt),
        in_specs=[comp(4), comp(5), comp(6), comp(7), gvec, gvec, pl.BlockSpec((tt, tt), lambda h, t: (0, 0))],
        out_specs=[head_cols, head_cols, pl.BlockSpec((None, nb, hd, hd), lambda h, t: (h, t, 0, 0))],
        out_shape=[jax.ShapeDtypeStruct((s, w), F32), jax.ShapeDtypeStruct((s, w), BF),
                   jax.ShapeDtypeStruct((nh, s // bs, hd, hd), F32)],
        scratch_shapes=[pltpu.VMEM((hd, hd), F32), pltpu.VMEM((nb, hd, hd), F32)],
        sem=("parallel", "arbitrary"),
        args=(proj, proj, proj, proj, lb_logits, ghg, _block_tri(tt, bs, upper=False)), comm=comm)


def _hgrn_bwd_call(proj, lb_logits, ghg, o_raw, states, dmerged, stack, name, comm=None):
    s = proj.shape[0]
    w = proj.shape[1] // 8
    nh = HGRN_HEADS
    hd = w // nh
    bs = HGRN_BLOCK
    tt = _tile(s, 256, bs)
    nb = tt // bs
    nt = s // tt

    def body(q_ref, z_ref, v_ref, gate_ref, lb_ref, gn_ref, tril_ref, triu_ref, o_ref, st_ref, dm_ref, stack_ref,
             dp_ref, dlb_ref, dgn_ref,
             dstate, ds_all, inc, dq_s, dk_s, dv_s, dgl_s, dk_rows, dv_rows):
        @pl.when(pl.program_id(1) == 0)
        def _():
            dstate[...] = jnp.zeros_like(dstate)
            dlb_ref[...] = jnp.zeros_like(dlb_ref)
            dgn_ref[...] = jnp.zeros_like(dgn_ref)

        lbv = _sigmoid(lb_ref[...])
        oml = 1.0 - lbv
        gn = gn_ref[...]
        sz, f, key = _hgrn_gates(z_ref[...], lbv)
        qr = q_ref[...]
        sq = _sigmoid(qr)
        q = qr * sq
        v = v_ref[...]
        g = _dot_exact(tril_ref[...], jnp.log(f))
        eg = jnp.exp(g)
        blocks = lambda t: t.reshape(nb, bs, hd)
        g3, q3, k3, v3 = blocks(g), blocks(q), blocks(key), blocks(v)
        glast3 = g3[:, bs - 1:bs, :]
        egl3 = jnp.exp(glast3)
        ktail3 = jnp.exp(glast3 - g3)
        o = o_ref[...]
        rstd = lax.rsqrt(jnp.mean(o * o, axis=-1, keepdims=True) + EPS)
        xhat = o * rstd
        gt = gate_ref[...]
        sig = _sigmoid(gt)
        sg = gt * sig
        dm = dm_ref[...]
        dgn_ref[...] += jnp.sum(dm * xhat * sg, axis=0, keepdims=True)
        del stack_ref
        dp_ref[3] = (dm * xhat * gn * sig * (1.0 + gt * (1.0 - sig))).astype(BF)
        dxhat = dm * gn * sg
        do = rstd * (dxhat - xhat * jnp.mean(dxhat * xhat, axis=-1, keepdims=True))
        dob = do.astype(BF)
        do3 = blocks(do)
        qgb = (q * eg).astype(BF)
        for b in range(nb):
            rows = slice(b * bs, (b + 1) * bs)
            inc[b] = _dot_tn(dob[rows, :], qgb[rows, :])
        ds = dstate[...]
        for b in reversed(range(nb)):
            ds_all[b] = ds
            ds = ds * egl3[b] + inc[b]
        dstate[...] = ds
        ktb = (k3 * ktail3).reshape(tt, hd).astype(BF)
        vb = v.astype(BF)
        for b in range(nb):
            rows = slice(b * bs, (b + 1) * bs)
            prev = st_ref[b]
            dsb = ds_all[b]
            dsbb = dsb.astype(BF)
            dq_s[rows, :] = _dot(dob[rows, :], prev.astype(BF))
            dk_s[rows, :] = _dot(vb[rows, :], dsbb)
            dv_s[rows, :] = _dot_nt(ktb[rows, :], dsbb)
            dgl_s[b] = jnp.sum(prev * dsb, axis=0, keepdims=True)
        dq3 = blocks(dq_s[...] * eg)
        dk3 = blocks(dk_s[...]) * ktail3
        dg_last3 = jnp.sum(k3 * dk3, axis=1, keepdims=True) + egl3 * dgl_s[...]
        row_id = lax.broadcasted_iota(jnp.int32, (nb, bs, hd), 1)
        for j in range(bs):
            wj = jnp.where(row_id >= j, jnp.exp(jnp.minimum(g3 - g3[:, j:j + 1, :], 0.0)), 0.0)
            kj = k3[:, j:j + 1, :]
            a = jnp.sum(q3 * kj * wj, axis=-1, keepdims=True)
            da = jnp.sum(do3 * v3[:, j:j + 1, :], axis=-1, keepdims=True)
            dv_rows[:, j:j + 1, :] = jnp.sum(a * do3, axis=1, keepdims=True)
            dq3 = dq3 + da * kj * wj
            dk_rows[:, j:j + 1, :] = jnp.sum(da * q3 * wj, axis=1, keepdims=True)
        dk3 = dk3 + dk_rows[...]
        dv = dv_s[...] + dv_rows[...].reshape(tt, hd)
        dg3 = q3 * dq3 - k3 * dk3 + jnp.where(row_id == bs - 1, dg_last3, 0.0)
        dlf = _dot_exact(triu_ref[...], dg3.reshape(tt, hd))
        dk = dk3.reshape(tt, hd)
        dfk = dlf / f - dk
        dlb_ref[...] += jnp.sum(dfk * (1.0 - sz), axis=0, keepdims=True) * (lbv * oml)
        dp_ref[1] = (dfk * oml * sz * (1.0 - sz)).astype(BF)
        dp_ref[0] = (dq3.reshape(tt, hd) * sq * (1.0 + qr * (1.0 - sq))).astype(BF)
        dp_ref[2] = dv.astype(BF)

    rev = lambda t: nt - 1 - t
    comp = lambda j: pl.BlockSpec((tt, hd), lambda h, t, j=j: (rev(t), j * nh + h))
    gvec = pl.BlockSpec((1, hd), lambda h, t: (0, h))
    head_cols = pl.BlockSpec((tt, hd), lambda h, t: (rev(t), h))
    tri = pl.BlockSpec((tt, tt), lambda h, t: (0, 0))
    act = jax.ShapeDtypeStruct((s, w), BF)
    vec = jax.ShapeDtypeStruct((1, w), F32)
    tile_f32 = pltpu.VMEM((tt, hd), F32)
    return _launch(
        body, name=name, grid=(nh, nt),
        in_specs=[comp(4), comp(5), comp(6), comp(7), gvec, gvec, tri, tri, head_cols,
                  pl.BlockSpec((None, nb, hd, hd), lambda h, t: (h, rev(t), 0, 0)),
                  pl.BlockSpec((tt, hd), lambda h, t: (rev(t), nh + h)),
                  pl.BlockSpec(memory_space=pl.ANY)],
        out_specs=[pl.BlockSpec((4, tt, hd), lambda h, t: (1, rev(t), h)), gvec, gvec],
        out_shape=[jax.ShapeDtypeStruct(stack.shape, stack.dtype), vec, vec],
        scratch_shapes=[pltpu.VMEM((hd, hd), F32), pltpu.VMEM((nb, hd, hd), F32), pltpu.VMEM((nb, hd, hd), F32),
                        tile_f32, tile_f32, tile_f32, pltpu.VMEM((nb, 1, hd), F32),
                        pltpu.VMEM((nb, bs, hd), F32), pltpu.VMEM((nb, bs, hd), F32)],
        sem=("parallel", "arbitrary"),
        args=(proj, proj, proj, proj, lb_logits, ghg, _block_tri(tt, bs, upper=False), _block_tri(tt, bs, upper=True),
              o_raw, states, dmerged, stack), aliases={11: 0}, comm=comm)


def _position():
    return lax.axis_index("x"), lax.axis_index("y"), lax.axis_index("c")


def _all_gather_call(shards, name):
    n = len(shards)

    def body(*refs):
        ins, outs = refs[:n], refs[n:2 * n]
        send_sems, recv_sems, local_sems = refs[2 * n:]
        x, y, c = _position()
        me, sibling = (x, y, c), (x, y, 1 - c)
        chips = [(1 - x, y), (x, 1 - y), (1 - x, 1 - y)]

        def slot(a, p):
            return outs[a].at[4 * p[0] + 2 * p[1] + p[2]]

        def copy(a, k, block, to, src=None):
            return pltpu.make_async_remote_copy(
                src_ref=slot(a, block) if src is None else src, dst_ref=slot(a, block),
                send_sem=send_sems.at[a * 7 + k], recv_sem=recv_sems.at[a * 7 + k],
                device_id=to, device_id_type=MESH)

        mine = [pltpu.make_async_copy(ins[a], slot(a, me), local_sems.at[a]) for a in range(n)]
        for cp in mine:
            cp.start()
        first = []
        for a in range(n):
            first.append(copy(a, 0, me, sibling, src=ins[a]))
            first += [copy(a, 1 + j, me, (*chip, c), src=ins[a]) for j, chip in enumerate(chips)]
        for cp in first:
            cp.start()
        passed = []
        for j, chip in enumerate(chips):
            for a in range(n):
                copy(a, 1 + j, (*chip, c), me).wait_recv()
                fwd = copy(a, 4 + j, (*chip, c), sibling)
                fwd.start()
                passed.append(fwd)
        for a in range(n):
            copy(a, 0, sibling, me).wait_recv()
            for j, chip in enumerate(chips):
                copy(a, 4 + j, (*chip, 1 - c), me).wait_recv()
        for cp in first + passed:
            cp.wait_send()
        for cp in mine:
            cp.wait()

    return pl.pallas_call(
        body, name=name,
        in_specs=[HBM_SPEC] * n, out_specs=[HBM_SPEC] * n,
        out_shape=[jax.ShapeDtypeStruct((N_DEV,) + t.shape, t.dtype) for t in shards],
        scratch_shapes=[pltpu.SemaphoreType.DMA((7 * n,)), pltpu.SemaphoreType.DMA((7 * n,)),
                        pltpu.SemaphoreType.DMA((n,))],
    )(*shards)


def _slot(ref, p):
    return ref.at[4 * p[0] + 2 * p[1] + p[2]]


def _gather_round1(shards):
    n = len(shards)

    def plan(ins, outs, send_sems, recv_sems):
        x, y, c = _position()
        me = (x, y, c)
        peers = [(x, y, 1 - c), (1 - x, y, c), (x, 1 - y, c), (1 - x, 1 - y, c)]
        sends, recvs, local = [], [], []
        for a in range(n):
            local.append(pltpu.make_async_copy(ins[a], _slot(outs[a], me), send_sems.at[4 * n + a]))
            for k, peer in enumerate(peers):
                sems = dict(send_sem=send_sems.at[4 * a + k], recv_sem=recv_sems.at[4 * a + k],
                            device_id=peer, device_id_type=MESH)
                sends.append(pltpu.make_async_remote_copy(src_ref=ins[a], dst_ref=_slot(outs[a], me), **sems))
                recvs.append(pltpu.make_async_remote_copy(src_ref=ins[a], dst_ref=_slot(outs[a], peer), **sems))
        return sends, recvs, local

    def start(*refs):
        sends, _, local = plan(*refs)
        for cp in local + sends:
            cp.start()

    def finish(*refs):
        sends, recvs, local = plan(*refs)
        for cp in recvs:
            cp.wait_recv()
        for cp in sends:
            cp.wait_send()
        for cp in local:
            cp.wait()

    return _Comm(shards, [jax.ShapeDtypeStruct((N_DEV,) + t.shape, t.dtype) for t in shards], 5 * n, start, finish)


def _gather_round2(gathered):
    n = len(gathered)

    def plan(ins, outs, send_sems, recv_sems):
        x, y, c = _position()
        chips = [(1 - x, y), (x, 1 - y), (1 - x, 1 - y)]
        sends, recvs = [], []
        for a in range(n):
            for k, chip in enumerate(chips):
                sems = dict(send_sem=send_sems.at[3 * a + k], recv_sem=recv_sems.at[3 * a + k],
                            device_id=(x, y, 1 - c), device_id_type=MESH)
                sends.append(pltpu.make_async_remote_copy(
                    src_ref=_slot(ins[a], (*chip, c)), dst_ref=_slot(outs[a], (*chip, c)), **sems))
                recvs.append(pltpu.make_async_remote_copy(
                    src_ref=_slot(ins[a], (*chip, c)), dst_ref=_slot(outs[a], (*chip, 1 - c)), **sems))
        return sends, recvs

    def start(*refs):
        for cp in plan(*refs)[0]:
            cp.start()

    def finish(*refs):
        sends, recvs = plan(*refs)
        for cp in recvs:
            cp.wait_recv()
        for cp in sends:
            cp.wait_send()

    return _Comm(gathered, [jax.ShapeDtypeStruct(t.shape, t.dtype) for t in gathered], 3 * n, start, finish,
                 aliases={a: a for a in range(n)})


def _gather_two_level(shards, forward_at):
    n = len(shards)
    first, second = _gather_round1(shards), _gather_round2(shards)

    def middle(ins, outs, send_sems, recv_sems):
        first.finish(ins, outs, send_sems, recv_sems)
        second.start(outs, outs, _SemWindow(send_sems, first.n_sems), _SemWindow(recv_sems, first.n_sems))

    def finish(ins, outs, send_sems, recv_sems):
        second.finish(outs, outs, _SemWindow(send_sems, first.n_sems), _SemWindow(recv_sems, first.n_sems))

    return _Comm(shards, first.out_shape, first.n_sems + second.n_sems, first.start, finish,
                 middle=middle, middle_at=forward_at)


def _sibling_exchange(grads):
    n = len(grads)

    def plan(ins, outs, send_sems, recv_sems):
        x, y, c = _position()
        return [pltpu.make_async_remote_copy(
            src_ref=ins[a].at[2 * q + (1 - c)], dst_ref=outs[a].at[q],
            send_sem=send_sems.at[a * 4 + q], recv_sem=recv_sems.at[a * 4 + q],
            device_id=(x, y, 1 - c), device_id_type=MESH) for a in range(n) for q in range(4)]

    def start(*refs):
        for cp in plan(*refs):
            cp.start()

    def finish(*refs):
        for cp in plan(*refs):
            cp.wait()

    return _Comm(grads, [jax.ShapeDtypeStruct((4,) + t.shape[1:], t.dtype) for t in grads], 4 * n, start, finish)


def _chip_exchange(partials):
    n = len(partials)

    def plan(ins, outs, send_sems, recv_sems):
        x, y, c = _position()
        chips = [(1 - x, y), (x, 1 - y), (1 - x, 1 - y)]
        return [pltpu.make_async_remote_copy(
            src_ref=ins[a].at[2 * chip[0] + chip[1]], dst_ref=outs[a].at[k],
            send_sem=send_sems.at[a * 3 + k], recv_sem=recv_sems.at[a * 3 + k],
            device_id=(*chip, c), device_id_type=MESH) for a in range(n) for k, chip in enumerate(chips)]

    def start(*refs):
        for cp in plan(*refs):
            cp.start()

    def finish(*refs):
        for cp in plan(*refs):
            cp.wait()

    return _Comm(partials, [jax.ShapeDtypeStruct((3,) + t.shape[1:], t.dtype) for t in partials], 3 * n, start, finish)


class _SemWindow:
    def __init__(self, sems, offset):
        self._sems, self._offset = sems, offset

    @property
    def at(self):
        return self

    def __getitem__(self, i):
        return self._sems.at[self._offset + i]


def _join(parts):
    def each(fn_name, cins, couts, send_sems, recv_sems):
        i = o = sem = 0
        for p in parts:
            ni, no = len(p.operands), len(p.out_shape)
            getattr(p, fn_name)(cins[i:i + ni], couts[o:o + no], _SemWindow(send_sems, sem), _SemWindow(recv_sems, sem))
            i, o, sem = i + ni, o + no, sem + p.n_sems

    assert not any(p.aliases for p in parts)
    return _Comm([t for p in parts for t in p.operands], [t for p in parts for t in p.out_shape],
                 sum(p.n_sems for p in parts), functools.partial(each, "start"), functools.partial(each, "finish"))


def _pair_sum_call(grad, recv, parity, name):
    _, r, ccols = grad.shape
    tr = _tile(r, 256)

    def body(par_ref, g_ref, r_ref, p_ref, pb_ref):
        del par_ref
        p = g_ref[...] + r_ref[...]
        p_ref[...] = p
        pb_ref[...] = p.astype(BF)

    blk = lambda fn: pl.BlockSpec((None, tr, ccols), fn)
    return pl.pallas_call(
        body, name=name,
        grid_spec=pltpu.PrefetchScalarGridSpec(
            num_scalar_prefetch=1, grid=(4, r // tr),
            in_specs=[blk(lambda q, i, par: (2 * q + par[0], i, 0)), blk(lambda q, i, par: (q, i, 0))],
            out_specs=[blk(lambda q, i, par: (q, i, 0)), blk(lambda q, i, par: (q, i, 0))]),
        out_shape=[jax.ShapeDtypeStruct((4, r, ccols), F32), jax.ShapeDtypeStruct((4, r, ccols), BF)],
        compiler_params=_params("parallel", "parallel"),
    )(parity, grad, recv)


def _adamw_math(w, g, m, v):
    m = ADAM_B1 * m + (1.0 - ADAM_B1) * g
    v = ADAM_B2 * v + (1.0 - ADAM_B2) * (g * g)
    m_hat = m / (1.0 - ADAM_B1 ** ADAM_STEP)
    v_hat = v / (1.0 - ADAM_B2 ** ADAM_STEP)
    delta = -ADAM_LR * (m_hat / (jnp.sqrt(v_hat) + ADAM_EPS) + ADAM_WD * w)
    return delta, m, v


def _adamw_matrix_call(partial, recv, chip, w, m, v, name):
    r, ccols = w.shape
    gcols = partial.shape[2]
    tr = _tile(r, 256)

    def body(chip_ref, p_ref, r_ref, w_ref, m_ref, v_ref, g_out, d_out, m_out, v_out):
        del chip_ref
        cols = pl.ds(0, ccols)
        g = (p_ref[:, cols] + r_ref[0, :, cols].astype(F32) + r_ref[1, :, cols].astype(F32)
             + r_ref[2, :, cols].astype(F32))
        delta, mn, vn = _adamw_math(w_ref[...], g, m_ref[...], v_ref[...])
        g_out[...] = g
        d_out[...] = delta
        m_out[...] = mn
        v_out[...] = vn

    mat = pl.BlockSpec((tr, ccols), lambda i, ch: (i, 0))
    shp = jax.ShapeDtypeStruct((r, ccols), F32)
    return pl.pallas_call(
        body, name=name,
        grid_spec=pltpu.PrefetchScalarGridSpec(
            num_scalar_prefetch=1, grid=(r // tr,),
            in_specs=[pl.BlockSpec((None, tr, gcols), lambda i, ch: (ch[0], i, 0)),
                      pl.BlockSpec((3, tr, gcols), lambda i, ch: (0, i, 0)), mat, mat, mat],
            out_specs=[mat, mat, mat, mat]),
        out_shape=[shp, shp, shp, shp],
        compiler_params=_params("parallel"),
    )(chip, partial, recv, w, m, v)


def _adamw_vector_call(gathered, w, m, v, name):
    n = w.shape[1]

    def body(p_ref, w_ref, m_ref, v_ref, g_out, d_out, m_out, v_out):
        g = p_ref[0:1, :]
        for k in range(1, N_DEV):
            g = g + p_ref[k:k + 1, :]
        delta, mn, vn = _adamw_math(w_ref[...], g, m_ref[...], v_ref[...])
        g_out[...] = g
        d_out[...] = delta
        m_out[...] = mn
        v_out[...] = vn

    shp = jax.ShapeDtypeStruct((1, n), F32)
    return pl.pallas_call(body, name=name, out_shape=[shp, shp, shp, shp])(gathered, w, m, v)


def _round_up(n, mult):
    return (n + mult - 1) // mult * mult


def kernel(x, ffn1_norm, ffn1_w_gate, ffn1_w_up, ffn1_w_down, mix_norm, w_in, ret_norm_g, hgrn_lb_logits, hgrn_norm_g, w_out, ffn2_norm, ffn2_w_gate, ffn2_w_up, ffn2_w_down, final_norm, loss_target, m_ffn1_norm, m_ffn1_w_gate, m_ffn1_w_up, m_ffn1_w_down, m_mix_norm, m_w_in, m_ret_norm_g, m_hgrn_lb_logits, m_hgrn_norm_g, m_w_out, m_ffn2_norm, m_ffn2_w_gate, m_ffn2_w_up, m_ffn2_w_down, m_final_norm, v_ffn1_norm, v_ffn1_w_gate, v_ffn1_w_up, v_ffn1_w_down, v_mix_norm, v_w_in, v_ret_norm_g, v_hgrn_lb_logits, v_hgrn_norm_g, v_w_out, v_ffn2_norm, v_ffn2_w_gate, v_ffn2_w_up, v_ffn2_w_down, v_final_norm):
    xs = x[0]
    target = loss_target[0]
    s, d = xs.shape
    f_loc = ffn1_w_gate.shape[2]
    fp = _round_up(f_loc, LANE)
    pad_cols = lambda t: jnp.pad(t[0], ((0, 0), (0, fp - f_loc)))
    pad_rows = lambda t: jnp.pad(t[0], ((0, fp - f_loc), (0, 0)))

    mat_names = ["ffn1_w_gate", "ffn1_w_up", "ffn1_w_down", "w_in", "w_out", "ffn2_w_gate", "ffn2_w_up", "ffn2_w_down"]
    mat_pad = [pad_cols, pad_cols, pad_rows, lambda t: t[0], lambda t: t[0], pad_cols, pad_cols, pad_rows]
    mat_w = [ffn1_w_gate, ffn1_w_up, ffn1_w_down, w_in, w_out, ffn2_w_gate, ffn2_w_up, ffn2_w_down]
    mat_m = [m_ffn1_w_gate, m_ffn1_w_up, m_ffn1_w_down, m_w_in, m_w_out, m_ffn2_w_gate, m_ffn2_w_up, m_ffn2_w_down]
    mat_v = [v_ffn1_w_gate, v_ffn1_w_up, v_ffn1_w_down, v_w_in, v_w_out, v_ffn2_w_gate, v_ffn2_w_up, v_ffn2_w_down]

    cx, cy, cc = _position()
    parity = jnp.reshape(cc, (1,)).astype(jnp.int32)
    chip = jnp.reshape(2 * cx + cy, (1,)).astype(jnp.int32)
    mat_index = {nm: i for i, nm in enumerate(mat_names)}
    mat_out = {}

    def pair_sums(names, grads, from_sibling):
        return [_pair_sum_call(g, r, parity, "pair_sum_" + nm) for nm, g, r in zip(names, grads, from_sibling)]

    def update(names, sums, from_chips):
        for nm, (p, _), r in zip(names, sums, from_chips):
            i = mat_index[nm]
            res = _adamw_matrix_call(p, r, chip, mat_w[i][0], mat_m[i][0], mat_v[i][0], "adamw_" + nm)
            mat_out[nm] = [t[None] for t in res]

    shards = [p(t.astype(BF)) for p, t in zip(mat_pad, mat_w)]
    wg1, wu1 = _all_gather_call(shards[:2], "gather_ffn1_up")

    h1 = _rmsnorm_call(xs, ffn1_norm, "ffn1_norm")
    (g1, u1, a1), (wd1, win) = _ffn_up_call(h1, wg1, wu1, "ffn1_up", comm=_gather_two_level(shards[2:4], 0.7))
    (x1, h2), _ = _down_call(a1, wd1, xs, mix_norm, FFN_RESIDUAL_WEIGHT, "ffn1_down")
    (proj,), _ = _proj_call(h2, win, "mix_in")
    wmix = proj.shape[1] // 8
    cos, sin = _rope_tables(s, wmix // RET_HEADS)
    consts = _ret_consts(wmix // RET_HEADS)
    (o_hg, m_hg, st_hg), landed = _hgrn_fwd_call(proj, hgrn_lb_logits, hgrn_norm_g, "hgrn_fwd",
                                                 comm=_gather_round1(shards[4:]))
    (o_ret, m_ret, st_ret), (wout, wg2, wu2, wd2) = _ret_fwd_call(proj, cos, sin, consts, ret_norm_g, "ret_fwd",
                                                                  comm=_gather_round2(landed))
    merged = jnp.concatenate([m_ret, m_hg], axis=1)
    wout_wide = wout.reshape(2, wout.shape[0] * wout.shape[1] // 2, d)
    (x2, h3), _ = _down_call(merged, wout_wide, x1, ffn2_norm, 1.0, "mix_out")
    (g2, u2, a2), _ = _ffn_up_call(h3, wg2, wu2, "ffn2_up")
    (x3,), _ = _down_call(a2, wd2, x2, None, FFN_RESIDUAL_WEIGHT, "ffn2_down")
    loss_part, dx3, dx3b, gv_final = _loss_call(x3, target, final_norm[None, :], "loss_head")

    (dg2, du2), _ = _bwd_up_call(dx3b, wd2, g2, u2, FFN_RESIDUAL_WEIGHT, "ffn2_bwd_up")
    (dx2, dx2b, gv_n3), _ = _bwd_down_call([(dg2, wg2), (du2, wu2)], dx3, x2, ffn2_norm, "ffn2_bwd_down")
    names_a = ["ffn2_w_gate", "ffn2_w_up", "ffn2_w_down"]
    grads_a = [_wgrad_call(h3, dg2, N_DEV, False, 1.0, "ffn2_wgrad_gate"),
               _wgrad_call(h3, du2, N_DEV, False, 1.0, "ffn2_wgrad_up"),
               _wgrad_call(a2, dx3b, N_DEV, True, FFN_RESIDUAL_WEIGHT, "ffn2_wgrad_down")]

    (dmerged,), sib_a = _proj_call(dx2b, jnp.swapaxes(wout_wide, 1, 2), "mix_out_bwd",
                                   comm=_sibling_exchange(grads_a))
    gm_out = _wgrad_call(merged, dx2b, 2, True, 1.0, "mix_out_wgrad").reshape(wout.shape)
    sums_a = pair_sums(names_a, grads_a, sib_a)
    dproj_half, gv_ret = _ret_bwd_call(proj, cos, sin, consts, ret_norm_g, o_ret, st_ret, dmerged, "ret_bwd")
    (dproj, gv_lb, gv_hg), landed = _hgrn_bwd_call(
        proj, hgrn_lb_logits, hgrn_norm_g, o_hg, st_hg, dmerged, dproj_half, "hgrn_bwd",
        comm=_join([_chip_exchange([pb for _, pb in sums_a]), _sibling_exchange([gm_out])]))
    update(names_a, sums_a, landed[:3])
    sums_out = pair_sums(["w_out"], [gm_out], landed[3:])
    (dx1, dx1b, gv_n2), chips_out = _bwd_down_call([(dproj, win)], dx2, x1, mix_norm, "mix_in_bwd", per_step=2,
                                                   comm=_chip_exchange([sums_out[0][1]]))
    update(["w_out"], sums_out, chips_out)

    gm_in = _wgrad_call(h2, dproj, N_DEV, False, 1.0, "mix_in_wgrad")
    gm_d1, sib_in = _wgrad_call(a1, dx1b, N_DEV, True, FFN_RESIDUAL_WEIGHT, "ffn1_wgrad_down",
                                comm=_sibling_exchange([gm_in]))
    sums_in = pair_sums(["w_in"], [gm_in], sib_in)
    (dg1, du1), landed = _bwd_up_call(dx1b, wd1, g1, u1, FFN_RESIDUAL_WEIGHT, "ffn1_bwd_up",
                                      comm=_join([_chip_exchange([sums_in[0][1]]), _sibling_exchange([gm_d1])]))
    update(["w_in"], sums_in, landed[:1])
    sums_d1 = pair_sums(["ffn1_w_down"], [gm_d1], landed[1:])
    gm_g1, chips_d1 = _wgrad_call(h1, dg1, N_DEV, False, 1.0, "ffn1_wgrad_gate",
                                  comm=_chip_exchange([sums_d1[0][1]]))
    update(["ffn1_w_down"], sums_d1, chips_d1)
    gm_u1, sib_g = _wgrad_call(h1, du1, N_DEV, False, 1.0, "ffn1_wgrad_up", comm=_sibling_exchange([gm_g1]))
    sums_g = pair_sums(["ffn1_w_gate"], [gm_g1], sib_g)
    (dx0, _, gv_n1), landed = _bwd_down_call(
        [(dg1, wg1), (du1, wu1)], dx1, xs, ffn1_norm, "ffn1_bwd_down",
        comm=_join([_chip_exchange([sums_g[0][1]]), _sibling_exchange([gm_u1])]))
    update(["ffn1_w_gate"], sums_g, landed[:1])
    sums_u = pair_sums(["ffn1_w_up"], [gm_u1], landed[1:])
    update(["ffn1_w_up"], sums_u, _comm_only_call(_chip_exchange([sums_u[0][1]]), "ffn1_up_grads_to_chips"))

    vec_names = ["ffn1_norm", "mix_norm", "ret_norm_g", "hgrn_lb_logits", "hgrn_norm_g", "ffn2_norm", "final_norm"]
    vec_g = [gv_n1, gv_n2, gv_ret, gv_lb, gv_hg, gv_n3, gv_final]
    vec_w = [ffn1_norm, mix_norm, ret_norm_g, hgrn_lb_logits, hgrn_norm_g, ffn2_norm, final_norm[None, :]]
    vec_m = [m_ffn1_norm, m_mix_norm, m_ret_norm_g, m_hgrn_lb_logits, m_hgrn_norm_g, m_ffn2_norm, m_final_norm[None, :]]
    vec_v = [v_ffn1_norm, v_mix_norm, v_ret_norm_g, v_hgrn_lb_logits, v_hgrn_norm_g, v_ffn2_norm, v_final_norm[None, :]]
    cat = lambda ts: jnp.concatenate(ts, axis=1)
    (vec_all,) = _all_gather_call([cat(vec_g)], "gather_vector_grads")
    vres = _adamw_vector_call(vec_all[:, 0, :], cat(vec_w), cat(vec_m), cat(vec_v), "adamw_vectors")
    vec_out = {}
    off = 0
    for nm, t in zip(vec_names, vec_w):
        n = t.shape[1]
        parts = [r[:, off:off + n] for r in vres]
        if nm == "final_norm":
            parts = [p[0] for p in parts]
        vec_out[nm] = parts
        off += n

    loss = lax.psum(loss_part[0, 0], ("x", "y", "c"))
    order = ["ffn1_norm", "ffn1_w_gate", "ffn1_w_up", "ffn1_w_down", "mix_norm", "w_in", "ret_norm_g", "hgrn_lb_logits",
             "hgrn_norm_g", "w_out", "ffn2_norm", "ffn2_w_gate", "ffn2_w_up", "ffn2_w_down", "final_norm"]
    res = {**mat_out, **vec_out}
    outs = [loss, dx0[None]]
    for kind in range(4):
        outs += [res[nm][kind] for nm in order]
    return tuple(outs)
```

```python
import functools

import jax
import jax.numpy as jnp
from jax import lax
from jax.experimental import pallas as pl
from jax.experimental.pallas import tpu as pltpu

BF = jnp.bfloat16
F32 = jnp.float32
MESH = pl.DeviceIdType.MESH
HBM_SPEC = pl.BlockSpec(memory_space=pltpu.HBM)

N_DEV = 8
LANE = 128
EPS = 1e-6
ROPE_BASE = 10000.0
RET_HEADS = 4
HGRN_HEADS = 8
RET_CHUNK = 128
HGRN_BLOCK = 16
FFN_RESIDUAL_WEIGHT = 0.5
ADAM_LR = 0.001
ADAM_B1 = 0.9
ADAM_B2 = 0.999
ADAM_EPS = 1e-08
ADAM_WD = 0.01
ADAM_STEP = 10
VMEM_LIMIT = 56 * 1024 * 1024


def _tile(n, pref, mult=8):
    t = min(pref, n)
    t -= t % mult
    while t >= mult:
        if n % t == 0:
            return t
        t -= mult
    return n


def _params(*sem):
    return pltpu.CompilerParams(dimension_semantics=sem, vmem_limit_bytes=VMEM_LIMIT)


class _Comm:
    def __init__(self, operands, out_shape, n_sems, start, finish, aliases=None, middle=None, middle_at=0.0):
        self.operands = list(operands)
        self.out_shape = list(out_shape)
        self.n_sems = n_sems
        self.start = start
        self.finish = finish
        self.aliases = dict(aliases or {})
        self.middle = middle
        self.middle_at = middle_at


def _launch(body, *, name, grid, in_specs, out_specs, out_shape, sem, args, scratch_shapes=(), aliases=None, comm=None):
    in_specs, out_specs, out_shape = list(in_specs), list(out_specs), list(out_shape)
    scratch_shapes = list(scratch_shapes)
    aliases = dict(aliases or {})
    if comm is None:
        res = pl.pallas_call(body, name=name, grid=grid, in_specs=in_specs, out_specs=out_specs, out_shape=out_shape,
                             scratch_shapes=scratch_shapes, input_output_aliases=aliases,
                             compiler_params=_params(*sem))(*args)
        return list(res), []
    n_in, n_out, n_scr = len(in_specs), len(out_specs), len(scratch_shapes)
    ci, co = len(comm.operands), len(comm.out_shape)

    def carrying(*refs):
        bounds = [0, n_in, n_in + ci, n_in + ci + n_out, n_in + ci + n_out + co, n_in + ci + n_out + co + n_scr]
        ins, cins, outs, couts, scr = [refs[a:b] for a, b in zip(bounds[:-1], bounds[1:])]
        send_sems, recv_sems = refs[bounds[-1]:]
        ids = [pl.program_id(k) for k in range(len(grid))]
        first = functools.reduce(jnp.logical_and, [i == 0 for i in ids])
        last = functools.reduce(jnp.logical_and, [i == g - 1 for i, g in zip(ids, grid)])

        @pl.when(first)
        def _():
            comm.start(cins, couts, send_sems, recv_sems)

        if comm.middle is not None:
            step, total = ids[0], grid[0]
            for i, g in zip(ids[1:], grid[1:]):
                step, total = step * g + i, total * g

            @pl.when(step == int(total * comm.middle_at))
            def _():
                comm.middle(cins, couts, send_sems, recv_sems)

        body(*ins, *outs, *scr)

        @pl.when(last)
        def _():
            comm.finish(cins, couts, send_sems, recv_sems)

    res = pl.pallas_call(
        carrying, name=name, grid=grid,
        in_specs=in_specs + [HBM_SPEC] * ci, out_specs=out_specs + [HBM_SPEC] * co,
        out_shape=out_shape + comm.out_shape,
        scratch_shapes=scratch_shapes + [pltpu.SemaphoreType.DMA((comm.n_sems,)), pltpu.SemaphoreType.DMA((comm.n_sems,))],
        input_output_aliases={**aliases, **{n_in + a: n_out + b for a, b in comm.aliases.items()}},
        compiler_params=_params(*(["arbitrary"] * len(grid))),
    )(*args, *comm.operands)
    return list(res[:n_out]), list(res[n_out:])


def _comm_only_call(comm, name):
    def body(*refs):
        ci, co = len(comm.operands), len(comm.out_shape)
        cins, couts = refs[:ci], refs[ci:ci + co]
        send_sems, recv_sems = refs[ci + co:]
        comm.start(cins, couts, send_sems, recv_sems)
        comm.finish(cins, couts, send_sems, recv_sems)

    return pl.pallas_call(
        body, name=name,
        in_specs=[HBM_SPEC] * len(comm.operands), out_specs=[HBM_SPEC] * len(comm.out_shape),
        out_shape=comm.out_shape,
        scratch_shapes=[pltpu.SemaphoreType.DMA((comm.n_sems,)), pltpu.SemaphoreType.DMA((comm.n_sems,))],
        input_output_aliases=comm.aliases,
    )(*comm.operands)


def _sigmoid(v):
    return 1.0 / (1.0 + jnp.exp(-v))


def _dot(a, b):
    return jnp.dot(a, b, preferred_element_type=F32)


def _dot_nt(a, b):
    return lax.dot_general(a, b, (((1,), (1,)), ((), ())), preferred_element_type=F32)


def _dot_tn(a, b):
    return lax.dot_general(a, b, (((0,), (0,)), ((), ())), preferred_element_type=F32)


def _rmsnorm_call(x, gain, name):
    s, d = x.shape
    tm = _tile(s, 512)

    def body(x_ref, g_ref, o_ref):
        xv = x_ref[...]
        r = lax.rsqrt(jnp.mean(xv * xv, axis=-1, keepdims=True) + EPS)
        o_ref[...] = (xv * r * g_ref[...]).astype(BF)

    return pl.pallas_call(
        body, name=name, grid=(s // tm,),
        in_specs=[pl.BlockSpec((tm, d), lambda i: (i, 0)), pl.BlockSpec((1, d), lambda i: (0, 0))],
        out_specs=pl.BlockSpec((tm, d), lambda i: (i, 0)),
        out_shape=jax.ShapeDtypeStruct((s, d), BF),
        compiler_params=_params("parallel"),
    )(x, gain)


def _ffn_up_call(h, wg, wu, name, comm=None):
    s, d = h.shape
    nj, _, k = wg.shape
    tm = _tile(s, 512)

    def body(h_ref, wg_ref, wu_ref, g_ref, u_ref, a_ref):
        hv = h_ref[...]
        g = _dot(hv, wg_ref[...])
        u = _dot(hv, wu_ref[...])
        g_ref[...] = g
        u_ref[...] = u
        a_ref[...] = (g * _sigmoid(g) * u).astype(BF)

    act = pl.BlockSpec((tm, k), lambda i, j: (i, j))
    wsp = pl.BlockSpec((None, d, k), lambda i, j: (j, 0, 0))
    return _launch(
        body, name=name, grid=(s // tm, nj),
        in_specs=[pl.BlockSpec((tm, d), lambda i, j: (i, 0)), wsp, wsp],
        out_specs=[act, act, act],
        out_shape=[jax.ShapeDtypeStruct((s, nj * k), F32), jax.ShapeDtypeStruct((s, nj * k), F32),
                   jax.ShapeDtypeStruct((s, nj * k), BF)],
        sem=("parallel", "arbitrary"), args=(h, wg, wu), comm=comm)


def _proj_call(h, w, name, comm=None):
    s, d = h.shape
    nj, _, k = w.shape
    tm = _tile(s, 512)

    def body(h_ref, w_ref, o_ref):
        o_ref[...] = _dot(h_ref[...], w_ref[...])

    return _launch(
        body, name=name, grid=(s // tm, nj),
        in_specs=[pl.BlockSpec((tm, d), lambda i, j: (i, 0)), pl.BlockSpec((None, d, k), lambda i, j: (j, 0, 0))],
        out_specs=[pl.BlockSpec((tm, k), lambda i, j: (i, j))],
        out_shape=[jax.ShapeDtypeStruct((s, nj * k), F32)],
        sem=("parallel", "arbitrary"), args=(h, w), comm=comm)


def _down_call(a, w, resid, gain, scale, name, comm=None):
    s = a.shape[0]
    nj, k, d = w.shape
    tm = _tile(s, 512)
    cn = _tile(d, 512, LANE)
    with_norm = gain is not None

    def body(*refs):
        if with_norm:
            a_ref, w_ref, r_ref, g_ref, x_ref, h_ref, acc = refs
        else:
            a_ref, w_ref, r_ref, x_ref, acc = refs
        j = pl.program_id(1)

        @pl.when(j == 0)
        def _():
            acc[...] = jnp.zeros_like(acc)

        av = a_ref[...]
        for n0 in range(0, d, cn):
            acc[:, n0:n0 + cn] += _dot(av, w_ref[:, n0:n0 + cn])

        @pl.when(j == nj - 1)
        def _():
            xn = r_ref[...] + (scale * acc[...])
            x_ref[...] = xn
            if with_norm:
                r = lax.rsqrt(jnp.mean(xn * xn, axis=-1, keepdims=True) + EPS)
                h_ref[...] = (xn * r * g_ref[...]).astype(BF)

    row = pl.BlockSpec((tm, d), lambda i, j: (i, 0))
    in_specs = [pl.BlockSpec((tm, k), lambda i, j: (i, j)), pl.BlockSpec((None, k, d), lambda i, j: (j, 0, 0)), row]
    args = [a, w, resid]
    out_specs = [row]
    out_shape = [jax.ShapeDtypeStruct((s, d), F32)]
    if with_norm:
        in_specs.append(pl.BlockSpec((1, d), lambda i, j: (0, 0)))
        args.append(gain)
        out_specs.append(row)
        out_shape.append(jax.ShapeDtypeStruct((s, d), BF))
    return _launch(
        body, name=name, grid=(s // tm, nj),
        in_specs=in_specs, out_specs=out_specs, out_shape=out_shape,
        scratch_shapes=[pltpu.VMEM((tm, d), F32)],
        sem=("parallel", "arbitrary"), args=args, comm=comm)


def _loss_call(x, target, gain, name):
    s, d = x.shape
    tm = _tile(s, 512)

    def body(x_ref, t_ref, g_ref, loss_ref, dx_ref, dxb_ref, dg_ref):
        i = pl.program_id(0)

        @pl.when(i == 0)
        def _():
            loss_ref[...] = jnp.zeros_like(loss_ref)
            dg_ref[...] = jnp.zeros_like(dg_ref)

        xv = x_ref[...]
        gv = g_ref[...]
        r = lax.rsqrt(jnp.mean(xv * xv, axis=-1, keepdims=True) + EPS)
        xhat = xv * r
        err = xhat * gv - t_ref[...]
        per_tok = jnp.mean(err * err, axis=-1, keepdims=True)
        loss_ref[...] += 0.5 * jnp.sum(per_tok, axis=0, keepdims=True)
        dout = err * (1.0 / d)
        dg_ref[...] += jnp.sum(dout * xhat, axis=0, keepdims=True)
        dxhat = dout * gv
        dx = r * (dxhat - xhat * jnp.mean(dxhat * xhat, axis=-1, keepdims=True))
        dx_ref[...] = dx
        dxb_ref[...] = dx.astype(BF)

    row = pl.BlockSpec((tm, d), lambda i: (i, 0))
    vec = pl.BlockSpec((1, d), lambda i: (0, 0))
    return pl.pallas_call(
        body, name=name, grid=(s // tm,),
        in_specs=[row, row, vec],
        out_specs=[pl.BlockSpec((1, 1), lambda i: (0, 0)), row, row, vec],
        out_shape=[jax.ShapeDtypeStruct((1, 1), F32), jax.ShapeDtypeStruct((s, d), F32),
                   jax.ShapeDtypeStruct((s, d), BF), jax.ShapeDtypeStruct((1, d), F32)],
        compiler_params=_params("arbitrary"),
    )(x, target, gain)


def _bwd_up_call(dy, wd, g, u, scale, name, comm=None):
    s, d = dy.shape
    nj, k, _ = wd.shape
    tm = _tile(s, 512)

    cw = _tile(k, 2 * LANE, LANE)

    def body(dy_ref, w_ref, g_ref, u_ref, dg_ref, du_ref):
        dy = dy_ref[...]
        for c0 in range(0, k, cw):
            cols = slice(c0, c0 + cw)
            da = scale * _dot_nt(dy, w_ref[cols, :])
            gv = g_ref[:, cols]
            sig = _sigmoid(gv)
            du_ref[:, cols] = (da * gv * sig).astype(BF)
            dg_ref[:, cols] = (da * u_ref[:, cols] * sig * (1.0 + gv * (1.0 - sig))).astype(BF)

    act = pl.BlockSpec((tm, k), lambda i, j: (i, j))
    return _launch(
        body, name=name, grid=(s // tm, nj),
        in_specs=[pl.BlockSpec((tm, d), lambda i, j: (i, 0)), pl.BlockSpec((None, k, d), lambda i, j: (j, 0, 0)), act, act],
        out_specs=[act, act],
        out_shape=[jax.ShapeDtypeStruct((s, nj * k), BF), jax.ShapeDtypeStruct((s, nj * k), BF)],
        sem=("parallel", "arbitrary"), args=(dy, wd, g, u), comm=comm)


def _bwd_down_call(pairs, dres, xin, gain, name, per_step=1, tiles=None, carry=None, comm=None):
    s, d = xin.shape
    nblocks, _, k = pairs[0][1].shape
    nj = nblocks // per_step
    npair = len(pairs) * per_step
    tm = _tile(s, 512)
    strip = _tile(tm, 128)
    cn = _tile(d, 512, LANE)

    first_tile, n_tiles = (0, s // tm) if tiles is None else tiles

    def body(*refs):
        a_refs = refs[0:2 * npair:2]
        w_refs = refs[1:2 * npair:2]
        if carry is None:
            dres_ref, x_ref, g_ref, dx_ref, dxb_ref, dg_ref, acc = refs[2 * npair:]
        else:
            dres_ref, x_ref, g_ref, _, _, dg_prev_ref, dx_ref, dxb_ref, dg_ref, acc = refs[2 * npair:]
        i = pl.program_id(0)
        j = pl.program_id(1)

        @pl.when(j == 0)
        def _():
            acc[...] = jnp.zeros_like(acc)

        @pl.when((i == 0) & (j == 0))
        def _():
            dg_ref[...] = jnp.zeros_like(dg_ref) if carry is None else dg_prev_ref[...]

        for a_ref, w_ref in zip(a_refs, w_refs):
            av = a_ref[...]
            for n0 in range(0, d, cn):
                acc[:, n0:n0 + cn] += _dot_nt(av, w_ref[n0:n0 + cn, :])

        @pl.when(j == nj - 1)
        def _():
            for r0 in range(0, tm, strip):
                rows = slice(r0, r0 + strip)
                xv = x_ref[rows, :]
                r = lax.rsqrt(jnp.mean(xv * xv, axis=-1, keepdims=True) + EPS)
                xhat = xv * r
                dh = acc[rows, :]
                dg_ref[...] += jnp.sum(dh * xhat, axis=0, keepdims=True)
                dxhat = dh * g_ref[...]
                dx = dres_ref[rows, :] + r * (dxhat - xhat * jnp.mean(dxhat * xhat, axis=-1, keepdims=True))
                dx_ref[rows, :] = dx
                dxb_ref[rows, :] = dx.astype(BF)

    row = pl.BlockSpec((tm, d), lambda i, j: (first_tile + i, 0))
    vec = pl.BlockSpec((1, d), lambda i, j: (0, 0))
    in_specs, args = [], []
    for a, w in pairs:
        for r in range(per_step):
            in_specs += [pl.BlockSpec((None, tm, k), lambda i, j, r=r: (j * per_step + r, first_tile + i, 0))
                         if a.ndim == 3 else
                         pl.BlockSpec((tm, k), lambda i, j, r=r: (first_tile + i, j * per_step + r)),
                         pl.BlockSpec((None, d, k), lambda i, j, r=r: (j * per_step + r, 0, 0))]
            args += [a, w]
    once = pl.BlockSpec((tm, d), lambda i, j: (first_tile + i, 0), pipeline_mode=pl.Buffered(1))
    in_specs += [once, once, vec]
    args += [dres, xin, gain]
    aliases = {}
    if carry is not None:
        n_before = len(in_specs)
        in_specs += [pl.BlockSpec(memory_space=pl.ANY), pl.BlockSpec(memory_space=pl.ANY), vec]
        args += list(carry)
        aliases = {n_before: 0, n_before + 1: 1}
    return _launch(
        body, name=name, grid=(n_tiles, nj),
        in_specs=in_specs, out_specs=[row, row, vec],
        out_shape=[jax.ShapeDtypeStruct((s, d), F32), jax.ShapeDtypeStruct((s, d), BF), jax.ShapeDtypeStruct((1, d), F32)],
        scratch_shapes=[pltpu.VMEM((tm, d), F32)],
        sem=("arbitrary", "arbitrary"), args=args, aliases=aliases, comm=comm)


def _wgrad_call(a, b, nj, a_blocked, scale, name, comm=None):
    s = a.shape[0]
    ka = a.shape[1] // nj if a_blocked else a.shape[1]
    b_stacked = b.ndim == 3
    kb = b.shape[-1] if (a_blocked or b_stacked) else b.shape[1] // nj
    ts = _tile(s, 2048)
    ns = s // ts

    def body(a_ref, b_ref, o_ref):
        t = pl.program_id(1)

        @pl.when(t == 0)
        def _():
            o_ref[...] = jnp.zeros_like(o_ref)

        o_ref[...] += _dot_tn(a_ref[...], b_ref[...])
        if scale != 1.0:
            @pl.when(t == ns - 1)
            def _():
                o_ref[...] = o_ref[...] * scale

    a_spec = pl.BlockSpec((ts, ka), (lambda j, t: (t, j)) if a_blocked else (lambda j, t: (t, 0)))
    if b_stacked:
        b_spec = pl.BlockSpec((None, ts, kb), lambda j, t: (j, t, 0))
    else:
        b_spec = pl.BlockSpec((ts, kb), (lambda j, t: (t, 0)) if a_blocked else (lambda j, t: (t, j)))
    (out,), landed = _launch(
        body, name=name, grid=(nj, ns),
        in_specs=[a_spec, b_spec],
        out_specs=[pl.BlockSpec((None, ka, kb), lambda j, t: (j, 0, 0))],
        out_shape=[jax.ShapeDtypeStruct((nj, ka, kb), F32)],
        sem=("parallel", "arbitrary"), args=(a, b), comm=comm)
    return out if comm is None else (out, landed)


def _rope(v, cos, sin):
    half = v.shape[-1] // 2
    v1, v2 = v[:, :half], v[:, half:]
    return jnp.concatenate([v1 * cos - v2 * sin, v2 * cos + v1 * sin], axis=-1)


def _rope_bwd(dv, cos, sin):
    half = dv.shape[-1] // 2
    d1, d2 = dv[:, :half], dv[:, half:]
    return jnp.concatenate([d1 * cos + d2 * sin, d2 * cos - d1 * sin], axis=-1)


def _ret_consts(hd):
    c = RET_CHUNK
    log_gamma = jnp.log(1.0 - jnp.exp2(-5.0 - jnp.arange(RET_HEADS, dtype=F32)))
    idx = jnp.arange(c, dtype=F32)
    rel = idx[:, None] - idx[None, :]
    mask = rel >= 0
    decay = jnp.where(mask[None], jnp.exp(log_gamma[:, None, None] * jnp.where(mask, rel, 0.0)[None]), 0.0)
    qdec = jnp.exp(log_gamma[:, None] * (idx + 1.0)[None, :])
    kdec = jnp.exp(log_gamma[:, None] * (c - 1.0 - idx)[None, :])
    gchunk = jnp.exp(log_gamma * c)
    bc = lambda t: jnp.broadcast_to(t[:, :, None], (RET_HEADS, t.shape[1], hd))
    return decay, bc(qdec), bc(kdec), bc(gchunk[:, None])


def _rope_tables(s, hd):
    inv = jnp.power(ROPE_BASE, -jnp.arange(0, hd, 2, dtype=F32) / hd)
    ang = jnp.arange(s, dtype=F32)[:, None] * inv[None, :]
    return jnp.cos(ang), jnp.sin(ang)


def _ret_fwd_call(proj, cos, sin, consts, gret, name, comm=None):
    s = proj.shape[0]
    w = proj.shape[1] // 8
    hd = w // RET_HEADS
    c = RET_CHUNK
    tt = _tile(s, 512, c)
    nc = tt // c
    decay, qdec, kdec, gch = consts
    scale = hd ** -0.5

    def body(q_ref, k_ref, v_ref, gate_ref, cos_ref, sin_ref, dec_ref, qd_ref, kd_ref, gc_ref, gn_ref,
             o_ref, m_ref, st_ref, state):
        @pl.when(pl.program_id(1) == 0)
        def _():
            state[...] = jnp.zeros_like(state)

        dec = dec_ref[...]
        for ci in range(nc):
            rows = slice(ci * c, (ci + 1) * c)
            cs, sn = cos_ref[rows, :], sin_ref[rows, :]
            q = _rope(q_ref[rows, :], cs, sn) * scale
            k = _rope(k_ref[rows, :], cs, sn)
            vb = v_ref[rows, :].astype(BF)
            sc = _dot_nt(q.astype(BF), k.astype(BF)) * dec
            prev = state[...]
            st_ref[ci] = prev
            o = _dot(sc.astype(BF), vb) + _dot((q * qd_ref[...]).astype(BF), prev.astype(BF))
            state[...] = gc_ref[...] * prev + _dot_tn((k * kd_ref[...]).astype(BF), vb)
            o_ref[rows, :] = o
            mu = jnp.mean(o, axis=-1, keepdims=True)
            cen = o - mu
            xhat = cen * lax.rsqrt(jnp.mean(cen * cen, axis=-1, keepdims=True) + EPS)
            gt = gate_ref[rows, :]
            m_ref[rows, :] = (xhat * gn_ref[...] * (gt * _sigmoid(gt))).astype(BF)

    nh = RET_HEADS
    comp = lambda j: pl.BlockSpec((tt, hd), lambda h, t, j=j: (t, j * nh + h))
    tab = pl.BlockSpec((tt, hd // 2), lambda h, t: (t, 0))
    per_head = lambda r: pl.BlockSpec((None, r, hd), lambda h, t: (h, 0, 0))
    return _launch(
        body, name=name, grid=(nh, s // tt),
        in_specs=[comp(0), comp(1), comp(2), comp(3), tab, tab,
                  pl.BlockSpec((None, c, c), lambda h, t: (h, 0, 0)), per_head(c), per_head(c), per_head(1),
                  pl.BlockSpec((1, hd), lambda h, t: (0, h))],
        out_specs=[pl.BlockSpec((tt, hd), lambda h, t: (t, h)), pl.BlockSpec((tt, hd), lambda h, t: (t, h)),
                   pl.BlockSpec((None, nc, hd, hd), lambda h, t: (h, t, 0, 0))],
        out_shape=[jax.ShapeDtypeStruct((s, w), F32), jax.ShapeDtypeStruct((s, w), BF),
                   jax.ShapeDtypeStruct((nh, s // c, hd, hd), F32)],
        scratch_shapes=[pltpu.VMEM((hd, hd), F32)],
        sem=("parallel", "arbitrary"),
        args=(proj, proj, proj, proj, cos, sin, decay, qdec, kdec, gch, gret), comm=comm)


def _ret_bwd_call(proj, cos, sin, consts, gret, o_raw, states, dmerged, name):
    s = proj.shape[0]
    w = proj.shape[1] // 8
    hd = w // RET_HEADS
    c = RET_CHUNK
    tt = _tile(s, 512, c)
    nc = tt // c
    nt = s // tt
    decay, qdec, kdec, gch = consts
    scale = hd ** -0.5

    def body(q_ref, k_ref, v_ref, gate_ref, cos_ref, sin_ref, dec_ref, qd_ref, kd_ref, gc_ref, gn_ref,
             o_ref, st_ref, dm_ref, dp_ref, dgn_ref, dstate):
        @pl.when(pl.program_id(1) == 0)
        def _():
            dstate[...] = jnp.zeros_like(dstate)
            dgn_ref[...] = jnp.zeros_like(dgn_ref)

        dec = dec_ref[...]
        gn = gn_ref[...]
        for ci in reversed(range(nc)):
            rows = slice(ci * c, (ci + 1) * c)
            cs, sn = cos_ref[rows, :], sin_ref[rows, :]
            q = _rope(q_ref[rows, :], cs, sn) * scale
            k = _rope(k_ref[rows, :], cs, sn)
            qb, kb = q.astype(BF), k.astype(BF)
            vb = v_ref[rows, :].astype(BF)
            sc = _dot_nt(qb, kb) * dec
            o = o_ref[rows, :]
            mu = jnp.mean(o, axis=-1, keepdims=True)
            cen = o - mu
            rstd = lax.rsqrt(jnp.mean(cen * cen, axis=-1, keepdims=True) + EPS)
            xhat = cen * rstd
            gt = gate_ref[rows, :]
            sig = _sigmoid(gt)
            sg = gt * sig
            dm = dm_ref[rows, :]
            dgn_ref[...] += jnp.sum(dm * xhat * sg, axis=0, keepdims=True)
            dp_ref[3, rows, :] = (dm * xhat * gn * sig * (1.0 + gt * (1.0 - sig))).astype(BF)
            dxhat = dm * gn * sg
            do = rstd * (dxhat - jnp.mean(dxhat, axis=-1, keepdims=True)
                         - xhat * jnp.mean(dxhat * xhat, axis=-1, keepdims=True))
            dob = do.astype(BF)
            prev = st_ref[ci]
            ds = dstate[...]
            dsb = ds.astype(BF)
            dsc = (_dot_nt(dob, vb) * dec).astype(BF)
            dq = _dot(dsc, kb) + _dot_nt(dob, prev.astype(BF)) * qd_ref[...]
            dk = _dot_tn(dsc, qb) + _dot_nt(vb, dsb) * kd_ref[...]
            dv = _dot_tn(sc.astype(BF), dob) + _dot((k * kd_ref[...]).astype(BF), dsb)
            dstate[...] = gc_ref[...] * ds + _dot_tn((q * qd_ref[...]).astype(BF), dob)
            dp_ref[0, rows, :] = _rope_bwd(dq * scale, cs, sn).astype(BF)
            dp_ref[1, rows, :] = _rope_bwd(dk, cs, sn).astype(BF)
            dp_ref[2, rows, :] = dv.astype(BF)

    nh = RET_HEADS
    rev = lambda t: nt - 1 - t
    comp = lambda j: pl.BlockSpec((tt, hd), lambda h, t, j=j: (rev(t), j * nh + h))
    tab = pl.BlockSpec((tt, hd // 2), lambda h, t: (rev(t), 0))
    per_head = lambda r: pl.BlockSpec((None, r, hd), lambda h, t: (h, 0, 0))
    head_cols = pl.BlockSpec((tt, hd), lambda h, t: (rev(t), h))
    gvec = pl.BlockSpec((1, hd), lambda h, t: (0, h))
    act = jax.ShapeDtypeStruct((s, w), BF)
    return pl.pallas_call(
        body, name=name, grid=(nh, nt),
        in_specs=[comp(0), comp(1), comp(2), comp(3), tab, tab,
                  pl.BlockSpec((None, c, c), lambda h, t: (h, 0, 0)), per_head(c), per_head(c), per_head(1), gvec,
                  head_cols, pl.BlockSpec((None, nc, hd, hd), lambda h, t: (h, rev(t), 0, 0)), head_cols],
        out_specs=[pl.BlockSpec((4, tt, hd), lambda h, t: (0, rev(t), h)), gvec],
        out_shape=[jax.ShapeDtypeStruct((8, s, w), BF), jax.ShapeDtypeStruct((1, w), F32)],
        scratch_shapes=[pltpu.VMEM((hd, hd), F32)],
        compiler_params=_params("parallel", "arbitrary"),
    )(proj, proj, proj, proj, cos, sin, decay, qdec, kdec, gch, gret, o_raw, states, dmerged)


def _block_tri(n, bs, upper):
    r = jnp.arange(n)[:, None]
    cidx = jnp.arange(n)[None, :]
    same = (r // bs) == (cidx // bs)
    return jnp.where(same & ((cidx >= r) if upper else (cidx <= r)), 1.0, 0.0).astype(F32)


def _dot_exact(a, b):
    return jnp.dot(a, b, preferred_element_type=F32, precision=lax.Precision.HIGHEST)


def _hgrn_gates(z, lbv):
    sz = _sigmoid(z)
    oml = 1.0 - lbv
    f = lbv + oml * sz
    key = oml * (1.0 - sz)
    return sz, f, key


def _hgrn_fwd_call(proj, lb_logits, ghg, name, comm=None):
    s = proj.shape[0]
    w = proj.shape[1] // 8
    nh = HGRN_HEADS
    hd = w // nh
    bs = HGRN_BLOCK
    tt = _tile(s, 256, bs)
    nb = tt // bs

    def body(q_ref, z_ref, v_ref, gate_ref, lb_ref, gn_ref, tril_ref, o_ref, m_ref, st_ref, state, upd):
        @pl.when(pl.program_id(1) == 0)
        def _():
            state[...] = jnp.zeros_like(state)

        lbv = _sigmoid(lb_ref[...])
        _, f, key = _hgrn_gates(z_ref[...], lbv)
        qr = q_ref[...]
        q = qr * _sigmoid(qr)
        v = v_ref[...]
        g = _dot_exact(tril_ref[...], jnp.log(f))
        blocks = lambda t: t.reshape(nb, bs, hd)
        g3, q3, k3, v3 = blocks(g), blocks(q), blocks(key), blocks(v)
        glast3 = g3[:, bs - 1:bs, :]
        row_id = lax.broadcasted_iota(jnp.int32, (nb, bs, hd), 1)
        o3 = jnp.zeros((nb, bs, hd), F32)
        for j in range(bs):
            wj = jnp.where(row_id >= j, jnp.exp(jnp.minimum(g3 - g3[:, j:j + 1, :], 0.0)), 0.0)
            a = jnp.sum(q3 * k3[:, j:j + 1, :] * wj, axis=-1, keepdims=True)
            o3 = o3 + a * v3[:, j:j + 1, :]
        ktb = (k3 * jnp.exp(glast3 - g3)).reshape(tt, hd).astype(BF)
        vb = v.astype(BF)
        for b in range(nb):
            rows = slice(b * bs, (b + 1) * bs)
            upd[b] = _dot_tn(vb[rows, :], ktb[rows, :])
        egl3 = jnp.exp(glast3)
        st = state[...]
        for b in range(nb):
            st_ref[b] = st
            st = st * egl3[b] + upd[b]
        state[...] = st
        qgb = (q * jnp.exp(g)).astype(BF)
        o_intra = o3.reshape(tt, hd)
        gn = gn_ref[...]
        for b in range(nb):
            rows = slice(b * bs, (b + 1) * bs)
            o = o_intra[rows, :] + _dot_nt(qgb[rows, :], st_ref[b].astype(BF))
            o_ref[rows, :] = o
            gt = gate_ref[rows, :]
            xhat = o * lax.rsqrt(jnp.mean(o * o, axis=-1, keepdims=True) + EPS)
            m_ref[rows, :] = (xhat * gn * (gt * _sigmoid(gt))).astype(BF)

    comp = lambda j: pl.BlockSpec((tt, hd), lambda h, t, j=j: (t, j * nh + h))
    gvec = pl.BlockSpec((1, hd), lambda h, t: (0, h))
    head_cols = pl.BlockSpec((tt, hd), lambda h, t: (t, h))
    return _launch(
        body, name=name, grid=(nh, s // tt),
        in_specs=[comp(4), comp(5), comp(6), comp(7), gvec, gvec, pl.BlockSpec((tt, tt), lambda h, t: (0, 0))],
        out_specs=[head_cols, head_cols, pl.BlockSpec((None, nb, hd, hd), lambda h, t: (h, t, 0, 0))],
        out_shape=[jax.ShapeDtypeStruct((s, w), F32), jax.ShapeDtypeStruct((s, w), BF),
                   jax.ShapeDtypeStruct((nh, s // bs, hd, hd), F32)],
        scratch_shapes=[pltpu.VMEM((hd, hd), F32), pltpu.VMEM((nb, hd, hd), F32)],
        sem=("parallel", "arbitrary"),
        args=(proj, proj, proj, proj, lb_logits, ghg, _block_tri(tt, bs, upper=False)), comm=comm)


def _hgrn_bwd_call(proj, lb_logits, ghg, o_raw, states, dmerged, stack, name, comm=None):
    s = proj.shape[0]
    w = proj.shape[1] // 8
    nh = HGRN_HEADS
    hd = w // nh
    bs = HGRN_BLOCK
    tt = _tile(s, 256, bs)
    nb = tt // bs
    nt = s // tt

    def body(q_ref, z_ref, v_ref, gate_ref, lb_ref, gn_ref, tril_ref, triu_ref, o_ref, st_ref, dm_ref, stack_ref,
             dp_ref, dlb_ref, dgn_ref,
             dstate, ds_all, inc, dq_s, dk_s, dv_s, dgl_s, dk_rows, dv_rows):
        @pl.when(pl.program_id(1) == 0)
        def _():
            dstate[...] = jnp.zeros_like(dstate)
            dlb_ref[...] = jnp.zeros_like(dlb_ref)
            dgn_ref[...] = jnp.zeros_like(dgn_ref)

        lbv = _sigmoid(lb_ref[...])
        oml = 1.0 - lbv
        gn = gn_ref[...]
        sz, f, key = _hgrn_gates(z_ref[...], lbv)
        qr = q_ref[...]
        sq = _sigmoid(qr)
        q = qr * sq
        v = v_ref[...]
        g = _dot_exact(tril_ref[...], jnp.log(f))
        eg = jnp.exp(g)
        blocks = lambda t: t.reshape(nb, bs, hd)
        g3, q3, k3, v3 = blocks(g), blocks(q), blocks(key), blocks(v)
        glast3 = g3[:, bs - 1:bs, :]
        egl3 = jnp.exp(glast3)
        ktail3 = jnp.exp(glast3 - g3)
        o = o_ref[...]
        rstd = lax.rsqrt(jnp.mean(o * o, axis=-1, keepdims=True) + EPS)
        xhat = o * rstd
        gt = gate_ref[...]
        sig = _sigmoid(gt)
        sg = gt * sig
        dm = dm_ref[...]
        dgn_ref[...] += jnp.sum(dm * xhat * sg, axis=0, keepdims=True)
        del stack_ref
        dp_ref[3] = (dm * xhat * gn * sig * (1.0 + gt * (1.0 - sig))).astype(BF)
        dxhat = dm * gn * sg
        do = rstd * (dxhat - xhat * jnp.mean(dxhat * xhat, axis=-1, keepdims=True))
        dob = do.astype(BF)
        do3 = blocks(do)
        qgb = (q * eg).astype(BF)
        for b in range(nb):
            rows = slice(b * bs, (b + 1) * bs)
            inc[b] = _dot_tn(dob[rows, :], qgb[rows, :])
        ds = dstate[...]
        for b in reversed(range(nb)):
            ds_all[b] = ds
            ds = ds * egl3[b] + inc[b]
        dstate[...] = ds
        ktb = (k3 * ktail3).reshape(tt, hd).astype(BF)
        vb = v.astype(BF)
        for b in range(nb):
            rows = slice(b * bs, (b + 1) * bs)
            prev = st_ref[b]
            dsb = ds_all[b]
            dsbb = dsb.astype(BF)
            dq_s[rows, :] = _dot(dob[rows, :], prev.astype(BF))
            dk_s[rows, :] = _dot(vb[rows, :], dsbb)
            dv_s[rows, :] = _dot_nt(ktb[rows, :], dsbb)
            dgl_s[b] = jnp.sum(prev * dsb, axis=0, keepdims=True)
        dq3 = blocks(dq_s[...] * eg)
        dk3 = blocks(dk_s[...]) * ktail3
        dg_last3 = jnp.sum(k3 * dk3, axis=1, keepdims=True) + egl3 * dgl_s[...]
        row_id = lax.broadcasted_iota(jnp.int32, (nb, bs, hd), 1)
        for j in range(bs):
            wj = jnp.where(row_id >= j, jnp.exp(jnp.minimum(g3 - g3[:, j:j + 1, :], 0.0)), 0.0)
            kj = k3[:, j:j + 1, :]
            a = jnp.sum(q3 * kj * wj, axis=-1, keepdims=True)
            da = jnp.sum(do3 * v3[:, j:j + 1, :], axis=-1, keepdims=True)
            dv_rows[:, j:j + 1, :] = jnp.sum(a * do3, axis=1, keepdims=True)
            dq3 = dq3 + da * kj * wj
            dk_rows[:, j:j + 1, :] = jnp.sum(da * q3 * wj, axis=1, keepdims=True)
        dk3 = dk3 + dk_rows[...]
        dv = dv_s[...] + dv_rows[...].reshape(tt, hd)
        dg3 = q3 * dq3 - k3 * dk3 + jnp.where(row_id == bs - 1, dg_last3, 0.0)
        dlf = _dot_exact(triu_ref[...], dg3.reshape(tt, hd))
        dk = dk3.reshape(tt, hd)
        dfk = dlf / f - dk
        dlb_ref[...] += jnp.sum(dfk * (1.0 - sz), axis=0, keepdims=True) * (lbv * oml)
        dp_ref[1] = (dfk * oml * sz * (1.0 - sz)).astype(BF)
        dp_ref[0] = (dq3.reshape(tt, hd) * sq * (1.0 + qr * (1.0 - sq))).astype(BF)
        dp_ref[2] = dv.astype(BF)

    rev = lambda t: nt - 1 - t
    comp = lambda j: pl.BlockSpec((tt, hd), lambda h, t, j=j: (rev(t), j * nh + h))
    gvec = pl.BlockSpec((1, hd), lambda h, t: (0, h))
    head_cols = pl.BlockSpec((tt, hd), lambda h, t: (rev(t), h))
    tri = pl.BlockSpec((tt, tt), lambda h, t: (0, 0))
    act = jax.ShapeDtypeStruct((s, w), BF)
    vec = jax.ShapeDtypeStruct((1, w), F32)
    tile_f32 = pltpu.VMEM((tt, hd), F32)
    return _launch(
        body, name=name, grid=(nh, nt),
        in_specs=[comp(4), comp(5), comp(6), comp(7), gvec, gvec, tri, tri, head_cols,
                  pl.BlockSpec((None, nb, hd, hd), lambda h, t: (h, rev(t), 0, 0)),
                  pl.BlockSpec((tt, hd), lambda h, t: (rev(t), nh + h)),
                  pl.BlockSpec(memory_space=pl.ANY)],
        out_specs=[pl.BlockSpec((4, tt, hd), lambda h, t: (1, rev(t), h)), gvec, gvec],
        out_shape=[jax.ShapeDtypeStruct(stack.shape, stack.dtype), vec, vec],
        scratch_shapes=[pltpu.VMEM((hd, hd), F32), pltpu.VMEM((nb, hd, hd), F32), pltpu.VMEM((nb, hd, hd), F32),
                        tile_f32, tile_f32, tile_f32, pltpu.VMEM((nb, 1, hd), F32),
                        pltpu.VMEM((nb, bs, hd), F32), pltpu.VMEM((nb, bs, hd), F32)],
        sem=("parallel", "arbitrary"),
        args=(proj, proj, proj, proj, lb_logits, ghg, _block_tri(tt, bs, upper=False), _block_tri(tt, bs, upper=True),
              o_raw, states, dmerged, stack), aliases={11: 0}, comm=comm)


def _position():
    return lax.axis_index("x"), lax.axis_index("y"), lax.axis_index("c")


def _all_gather_call(shards, name):
    n = len(shards)

    def body(*refs):
        ins, outs = refs[:n], refs[n:2 * n]
        send_sems, recv_sems, local_sems = refs[2 * n:]
        x, y, c = _position()
        me, sibling = (x, y, c), (x, y, 1 - c)
        chips = [(1 - x, y), (x, 1 - y), (1 - x, 1 - y)]

        def slot(a, p):
            return outs[a].at[4 * p[0] + 2 * p[1] + p[2]]

        def copy(a, k, block, to, src=None):
            return pltpu.make_async_remote_copy(
                src_ref=slot(a, block) if src is None else src, dst_ref=slot(a, block),
                send_sem=send_sems.at[a * 7 + k], recv_sem=recv_sems.at[a * 7 + k],
                device_id=to, device_id_type=MESH)

        mine = [pltpu.make_async_copy(ins[a], slot(a, me), local_sems.at[a]) for a in range(n)]
        for cp in mine:
            cp.start()
        first = []
        for a in range(n):
            first.append(copy(a, 0, me, sibling, src=ins[a]))
            first += [copy(a, 1 + j, me, (*chip, c), src=ins[a]) for j, chip in enumerate(chips)]
        for cp in first:
            cp.start()
        passed = []
        for j, chip in enumerate(chips):
            for a in range(n):
                copy(a, 1 + j, (*chip, c), me).wait_recv()
                fwd = copy(a, 4 + j, (*chip, c), sibling)
                fwd.start()
                passed.append(fwd)
        for a in range(n):
            copy(a, 0, sibling, me).wait_recv()
            for j, chip in enumerate(chips):
                copy(a, 4 + j, (*chip, 1 - c), me).wait_recv()
        for cp in first + passed:
            cp.wait_send()
        for cp in mine:
            cp.wait()

    return pl.pallas_call(
        body, name=name,
        in_specs=[HBM_SPEC] * n, out_specs=[HBM_SPEC] * n,
        out_shape=[jax.ShapeDtypeStruct((N_DEV,) + t.shape, t.dtype) for t in shards],
        scratch_shapes=[pltpu.SemaphoreType.DMA((7 * n,)), pltpu.SemaphoreType.DMA((7 * n,)),
                        pltpu.SemaphoreType.DMA((n,))],
    )(*shards)


def _slot(ref, p):
    return ref.at[4 * p[0] + 2 * p[1] + p[2]]


def _gather_round1(shards):
    n = len(shards)

    def plan(ins, outs, send_sems, recv_sems):
        x, y, c = _position()
        me = (x, y, c)
        peers = [(x, y, 1 - c), (1 - x, y, c), (x, 1 - y, c), (1 - x, 1 - y, c)]
        sends, recvs, local = [], [], []
        for a in range(n):
            local.append(pltpu.make_async_copy(ins[a], _slot(outs[a], me), send_sems.at[4 * n + a]))
            for k, peer in enumerate(peers):
                sems = dict(send_sem=send_sems.at[4 * a + k], recv_sem=recv_sems.at[4 * a + k],
                            device_id=peer, device_id_type=MESH)
                sends.append(pltpu.make_async_remote_copy(src_ref=ins[a], dst_ref=_slot(outs[a], me), **sems))
                recvs.append(pltpu.make_async_remote_copy(src_ref=ins[a], dst_ref=_slot(outs[a], peer), **sems))
        return sends, recvs, local

    def start(*refs):
        sends, _, local = plan(*refs)
        for cp in local + sends:
            cp.start()

    def finish(*refs):
        sends, recvs, local = plan(*refs)
        for cp in recvs:
            cp.wait_recv()
        for cp in sends:
            cp.wait_send()
        for cp in local:
            cp.wait()

    return _Comm(shards, [jax.ShapeDtypeStruct((N_DEV,) + t.shape, t.dtype) for t in shards], 5 * n, start, finish)


def _gather_round2(gathered):
    n = len(gathered)

    def plan(ins, outs, send_sems, recv_sems):
        x, y, c = _position()
        chips = [(1 - x, y), (x, 1 - y), (1 - x, 1 - y)]
        sends, recvs = [], []
        for a in range(n):
            for k, chip in enumerate(chips):
                sems = dict(send_sem=send_sems.at[3 * a + k], recv_sem=recv_sems.at[3 * a + k],
                            device_id=(x, y, 1 - c), device_id_type=MESH)
                sends.append(pltpu.make_async_remote_copy(
                    src_ref=_slot(ins[a], (*chip, c)), dst_ref=_slot(outs[a], (*chip, c)), **sems))
                recvs.append(pltpu.make_async_remote_copy(
                    src_ref=_slot(ins[a], (*chip, c)), dst_ref=_slot(outs[a], (*chip, 1 - c)), **sems))
        return sends, recvs

    def start(*refs):
        for cp in plan(*refs)[0]:
            cp.start()

    def finish(*refs):
        sends, recvs = plan(*refs)
        for cp in recvs:
            cp.wait_recv()
        for cp in sends:
            cp.wait_send()

    return _Comm(gathered, [jax.ShapeDtypeStruct(t.shape, t.dtype) for t in gathered], 3 * n, start, finish,
                 aliases={a: a for a in range(n)})


def _gather_two_level(shards, forward_at):
    n = len(shards)
    first, second = _gather_round1(shards), _gather_round2(shards)

    def middle(ins, outs, send_sems, recv_sems):
        first.finish(ins, outs, send_sems, recv_sems)
        second.start(outs, outs, _SemWindow(send_sems, first.n_sems), _SemWindow(recv_sems, first.n_sems))

    def finish(ins, outs, send_sems, recv_sems):
        second.finish(outs, outs, _SemWindow(send_sems, first.n_sems), _SemWindow(recv_sems, first.n_sems))

    return _Comm(shards, first.out_shape, first.n_sems + second.n_sems, first.start, finish,
                 middle=middle, middle_at=forward_at)


def _sibling_exchange(grads):
    n = len(grads)

    def plan(ins, outs, send_sems, recv_sems):
        x, y, c = _position()
        return [pltpu.make_async_remote_copy(
            src_ref=ins[a].at[2 * q + (1 - c)], dst_ref=outs[a].at[q],
            send_sem=send_sems.at[a * 4 + q], recv_sem=recv_sems.at[a * 4 + q],
            device_id=(x, y, 1 - c), device_id_type=MESH) for a in range(n) for q in range(4)]

    def start(*refs):
        for cp in plan(*refs):
            cp.start()

    def finish(*refs):
        for cp in plan(*refs):
            cp.wait()

    return _Comm(grads, [jax.ShapeDtypeStruct((4,) + t.shape[1:], t.dtype) for t in grads], 4 * n, start, finish)


def _chip_exchange(partials):
    n = len(partials)

    def plan(ins, outs, send_sems, recv_sems):
        x, y, c = _position()
        chips = [(1 - x, y), (x, 1 - y), (1 - x, 1 - y)]
        return [pltpu.make_async_remote_copy(
            src_ref=ins[a].at[2 * chip[0] + chip[1]], dst_ref=outs[a].at[k],
            send_sem=send_sems.at[a * 3 + k], recv_sem=recv_sems.at[a * 3 + k],
            device_id=(*chip, c), device_id_type=MESH) for a in range(n) for k, chip in enumerate(chips)]

    def start(*refs):
        for cp in plan(*refs):
            cp.start()

    def finish(*refs):
        for cp in plan(*refs):
            cp.wait()

    return _Comm(partials, [jax.ShapeDtypeStruct((3,) + t.shape[1:], t.dtype) for t in partials], 3 * n, start, finish)


class _SemWindow:
    def __init__(self, sems, offset):
        self._sems, self._offset = sems, offset

    @property
    def at(self):
        return self

    def __getitem__(self, i):
        return self._sems.at[self._offset + i]


def _join(parts):
    def each(fn_name, cins, couts, send_sems, recv_sems):
        i = o = sem = 0
        for p in parts:
            ni, no = len(p.operands), len(p.out_shape)
            getattr(p, fn_name)(cins[i:i + ni], couts[o:o + no], _SemWindow(send_sems, sem), _SemWindow(recv_sems, sem))
            i, o, sem = i + ni, o + no, sem + p.n_sems

    assert not any(p.aliases for p in parts)
    return _Comm([t for p in parts for t in p.operands], [t for p in parts for t in p.out_shape],
                 sum(p.n_sems for p in parts), functools.partial(each, "start"), functools.partial(each, "finish"))


def _pair_sum_call(grad, recv, parity, name):
    _, r, ccols = grad.shape
    tr = _tile(r, 256)

    def body(par_ref, g_ref, r_ref, p_ref, pb_ref):
        del par_ref
        p = g_ref[...] + r_ref[...]
        p_ref[...] = p
        pb_ref[...] = p.astype(BF)

    blk = lambda fn: pl.BlockSpec((None, tr, ccols), fn)
    return pl.pallas_call(
        body, name=name,
        grid_spec=pltpu.PrefetchScalarGridSpec(
            num_scalar_prefetch=1, grid=(4, r // tr),
            in_specs=[blk(lambda q, i, par: (2 * q + par[0], i, 0)), blk(lambda q, i, par: (q, i, 0))],
            out_specs=[blk(lambda q, i, par: (q, i, 0)), blk(lambda q, i, par: (q, i, 0))]),
        out_shape=[jax.ShapeDtypeStruct((4, r, ccols), F32), jax.ShapeDtypeStruct((4, r, ccols), BF)],
        compiler_params=_params("parallel", "parallel"),
    )(parity, grad, recv)


def _adamw_math(w, g, m, v):
    m = ADAM_B1 * m + (1.0 - ADAM_B1) * g
    v = ADAM_B2 * v + (1.0 - ADAM_B2) * (g * g)
    m_hat = m / (1.0 - ADAM_B1 ** ADAM_STEP)
    v_hat = v / (1.0 - ADAM_B2 ** ADAM_STEP)
    delta = -ADAM_LR * (m_hat / (jnp.sqrt(v_hat) + ADAM_EPS) + ADAM_WD * w)
    return delta, m, v


def _adamw_matrix_call(partial, recv, chip, w, m, v, name):
    r, ccols = w.shape
    gcols = partial.shape[2]
    tr = _tile(r, 256)

    def body(chip_ref, p_ref, r_ref, w_ref, m_ref, v_ref, g_out, d_out, m_out, v_out):
        del chip_ref
        cols = pl.ds(0, ccols)
        g = (p_ref[:, cols] + r_ref[0, :, cols].astype(F32) + r_ref[1, :, cols].astype(F32)
             + r_ref[2, :, cols].astype(F32))
        delta, mn, vn = _adamw_math(w_ref[...], g, m_ref[...], v_ref[...])
        g_out[...] = g
        d_out[...] = delta
        m_out[...] = mn
        v_out[...] = vn

    mat = pl.BlockSpec((tr, ccols), lambda i, ch: (i, 0))
    shp = jax.ShapeDtypeStruct((r, ccols), F32)
    return pl.pallas_call(
        body, name=name,
        grid_spec=pltpu.PrefetchScalarGridSpec(
            num_scalar_prefetch=1, grid=(r // tr,),
            in_specs=[pl.BlockSpec((None, tr, gcols), lambda i, ch: (ch[0], i, 0)),
                      pl.BlockSpec((3, tr, gcols), lambda i, ch: (0, i, 0)), mat, mat, mat],
            out_specs=[mat, mat, mat, mat]),
        out_shape=[shp, shp, shp, shp],
        compiler_params=_params("parallel"),
    )(chip, partial, recv, w, m, v)


def _adamw_vector_call(gathered, w, m, v, name):
    n = w.shape[1]

    def body(p_ref, w_ref, m_ref, v_ref, g_out, d_out, m_out, v_out):
        g = p_ref[0:1, :]
        for k in range(1, N_DEV):
            g = g + p_ref[k:k + 1, :]
        delta, mn, vn = _adamw_math(w_ref[...], g, m_ref[...], v_ref[...])
        g_out[...] = g
        d_out[...] = delta
        m_out[...] = mn
        v_out[...] = vn

    shp = jax.ShapeDtypeStruct((1, n), F32)
    return pl.pallas_call(body, name=name, out_shape=[shp, shp, shp, shp])(gathered, w, m, v)


def _round_up(n, mult):
    return (n + mult - 1) // mult * mult


def kernel(x, ffn1_norm, ffn1_w_gate, ffn1_w_up, ffn1_w_down, mix_norm, w_in, ret_norm_g, hgrn_lb_logits, hgrn_norm_g, w_out, ffn2_norm, ffn2_w_gate, ffn2_w_up, ffn2_w_down, final_norm, loss_target, m_ffn1_norm, m_ffn1_w_gate, m_ffn1_w_up, m_ffn1_w_down, m_mix_norm, m_w_in, m_ret_norm_g, m_hgrn_lb_logits, m_hgrn_norm_g, m_w_out, m_ffn2_norm, m_ffn2_w_gate, m_ffn2_w_up, m_ffn2_w_down, m_final_norm, v_ffn1_norm, v_ffn1_w_gate, v_ffn1_w_up, v_ffn1_w_down, v_mix_norm, v_w_in, v_ret_norm_g, v_hgrn_lb_logits, v_hgrn_norm_g, v_w_out, v_ffn2_norm, v_ffn2_w_gate, v_ffn2_w_up, v_ffn2_w_down, v_final_norm):
    xs = x[0]
    target = loss_target[0]
    s, d = xs.shape
    f_loc = ffn1_w_gate.shape[2]
    fp = _round_up(f_loc, LANE)
    pad_cols = lambda t: jnp.pad(t[0], ((0, 0), (0, fp - f_loc)))
    pad_rows = lambda t: jnp.pad(t[0], ((0, fp - f_loc), (0, 0)))

    mat_names = ["ffn1_w_gate", "ffn1_w_up", "ffn1_w_down", "w_in", "w_out", "ffn2_w_gate", "ffn2_w_up", "ffn2_w_down"]
    mat_pad = [pad_cols, pad_cols, pad_rows, lambda t: t[0], lambda t: t[0], pad_cols, pad_cols, pad_rows]
    mat_w = [ffn1_w_gate, ffn1_w_up, ffn1_w_down, w_in, w_out, ffn2_w_gate, ffn2_w_up, ffn2_w_down]
    mat_m = [m_ffn1_w_gate, m_ffn1_w_up, m_ffn1_w_down, m_w_in, m_w_out, m_ffn2_w_gate, m_ffn2_w_up, m_ffn2_w_down]
    mat_v = [v_ffn1_w_gate, v_ffn1_w_up, v_ffn1_w_down, v_w_in, v_w_out, v_ffn2_w_gate, v_ffn2_w_up, v_ffn2_w_down]

    cx, cy, cc = _position()
    parity = jnp.reshape(cc, (1,)).astype(jnp.int32)
    chip = jnp.reshape(2 * cx + cy, (1,)).astype(jnp.int32)
    mat_index = {nm: i for i, nm in enumerate(mat_names)}
    mat_out = {}

    def pair_sums(names, grads, from_sibling):
        return [_pair_sum_call(g, r, parity, "pair_sum_" + nm) for nm, g, r in zip(names, grads, from_sibling)]

    def update(names, sums, from_chips):
        for nm, (p, _), r in zip(names, sums, from_chips):
            i = mat_index[nm]
            res = _adamw_matrix_call(p, r, chip, mat_w[i][0], mat_m[i][0], mat_v[i][0], "adamw_" + nm)
            mat_out[nm] = [t[None] for t in res]

    shards = [p(t.astype(BF)) for p, t in zip(mat_pad, mat_w)]
    wg1, wu1 = _all_gather_call(shards[:2], "gather_ffn1_up")

    h1 = _rmsnorm_call(xs, ffn1_norm, "ffn1_norm")
    (g1, u1, a1), (wd1, win) = _ffn_up_call(h1, wg1, wu1, "ffn1_up", comm=_gather_two_level(shards[2:4], 0.7))
    (x1, h2), _ = _down_call(a1, wd1, xs, mix_norm, FFN_RESIDUAL_WEIGHT, "ffn1_down")
    (proj,), _ = _proj_call(h2, win, "mix_in")
    wmix = proj.shape[1] // 8
    cos, sin = _rope_tables(s, wmix // RET_HEADS)
    consts = _ret_consts(wmix // RET_HEADS)
    (o_hg, m_hg, st_hg), landed = _hgrn_fwd_call(proj, hgrn_lb_logits, hgrn_norm_g, "hgrn_fwd",
                                                 comm=_gather_round1(shards[4:]))
    (o_ret, m_ret, st_ret), (wout, wg2, wu2, wd2) = _ret_fwd_call(proj, cos, sin, consts, ret_norm_g, "ret_fwd",
                                                                  comm=_gather_round2(landed))
    merged = jnp.concatenate([m_ret, m_hg], axis=1)
    wout_wide = wout.reshape(2, wout.shape[0] * wout.shape[1] // 2, d)
    (x2, h3), _ = _down_call(merged, wout_wide, x1, ffn2_norm, 1.0, "mix_out")
    (g2, u2, a2), _ = _ffn_up_call(h3, wg2, wu2, "ffn2_up")
    (x3,), _ = _down_call(a2, wd2, x2, None, FFN_RESIDUAL_WEIGHT, "ffn2_down")
    loss_part, dx3, dx3b, gv_final = _loss_call(x3, target, final_norm[None, :], "loss_head")

    (dg2, du2), _ = _bwd_up_call(dx3b, wd2, g2, u2, FFN_RESIDUAL_WEIGHT, "ffn2_bwd_up")
    (dx2, dx2b, gv_n3), _ = _bwd_down_call([(dg2, wg2), (du2, wu2)], dx3, x2, ffn2_norm, "ffn2_bwd_down")
    names_a = ["ffn2_w_gate", "ffn2_w_up", "ffn2_w_down"]
    grads_a = [_wgrad_call(h3, dg2, N_DEV, False, 1.0, "ffn2_wgrad_gate"),
               _wgrad_call(h3, du2, N_DEV, False, 1.0, "ffn2_wgrad_up"),
               _wgrad_call(a2, dx3b, N_DEV, True, FFN_RESIDUAL_WEIGHT, "ffn2_wgrad_down")]

    (dmerged,), sib_a = _proj_call(dx2b, jnp.swapaxes(wout_wide, 1, 2), "mix_out_bwd",
                                   comm=_sibling_exchange(grads_a))
    gm_out = _wgrad_call(merged, dx2b, 2, True, 1.0, "mix_out_wgrad").reshape(wout.shape)
    sums_a = pair_sums(names_a, grads_a, sib_a)
    dproj_half, gv_ret = _ret_bwd_call(proj, cos, sin, consts, ret_norm_g, o_ret, st_ret, dmerged, "ret_bwd")
    (dproj, gv_lb, gv_hg), landed = _hgrn_bwd_call(
        proj, hgrn_lb_logits, hgrn_norm_g, o_hg, st_hg, dmerged, dproj_half, "hgrn_bwd",
        comm=_join([_chip_exchange([pb for _, pb in sums_a]), _sibling_exchange([gm_out])]))
    update(names_a, sums_a, landed[:3])
    sums_out = pair_sums(["w_out"], [gm_out], landed[3:])
    (dx1, dx1b, gv_n2), chips_out = _bwd_down_call([(dproj, win)], dx2, x1, mix_norm, "mix_in_bwd", per_step=2,
                                                   comm=_chip_exchange([sums_out[0][1]]))
    update(["w_out"], sums_out, chips_out)

    gm_in = _wgrad_call(h2, dproj, N_DEV, False, 1.0, "mix_in_wgrad")
    gm_d1, sib_in = _wgrad_call(a1, dx1b, N_DEV, True, FFN_RESIDUAL_WEIGHT, "ffn1_wgrad_down",
                                comm=_sibling_exchange([gm_in]))
    sums_in = pair_sums(["w_in"], [gm_in], sib_in)
    (dg1, du1), landed = _bwd_up_call(dx1b, wd1, g1, u1, FFN_RESIDUAL_WEIGHT, "ffn1_bwd_up",
                                      comm=_join([_chip_exchange([sums_in[0][1]]), _sibling_exchange([gm_d1])]))
    update(["w_in"], sums_in, landed[:1])
    sums_d1 = pair_sums(["ffn1_w_down"], [gm_d1], landed[1:])
    gm_g1, chips_d1 = _wgrad_call(h1, dg1, N_DEV, False, 1.0, "ffn1_wgrad_gate",
                                  comm=_chip_exchange([sums_d1[0][1]]))
    update(["ffn1_w_down"], sums_d1, chips_d1)
    gm_u1, sib_g = _wgrad_call(h1, du1, N_DEV, False, 1.0, "ffn1_wgrad_up", comm=_sibling_exchange([gm_g1]))
    sums_g = pair_sums(["ffn1_w_gate"], [gm_g1], sib_g)
    n_row_tiles = s // _tile(s, 512)
    assert n_row_tiles >= 2, "the sequence must span at least two row tiles"
    n_first = max(1, (5 * n_row_tiles) // 8)
    first_part, landed = _bwd_down_call(
        [(dg1, wg1), (du1, wu1)], dx1, xs, ffn1_norm, "ffn1_bwd_down_a", tiles=(0, n_first),
        comm=_join([_chip_exchange([sums_g[0][1]]), _sibling_exchange([gm_u1])]))
    update(["ffn1_w_gate"], sums_g, landed[:1])
    sums_u = pair_sums(["ffn1_w_up"], [gm_u1], landed[1:])
    (dx0, _, gv_n1), chips_u = _bwd_down_call(
        [(dg1, wg1), (du1, wu1)], dx1, xs, ffn1_norm, "ffn1_bwd_down_b", tiles=(n_first, n_row_tiles - n_first),
        carry=first_part, comm=_chip_exchange([sums_u[0][1]]))
    update(["ffn1_w_up"], sums_u, chips_u)

    vec_names = ["ffn1_norm", "mix_norm", "ret_norm_g", "hgrn_lb_logits", "hgrn_norm_g", "ffn2_norm", "final_norm"]
    vec_g = [gv_n1, gv_n2, gv_ret, gv_lb, gv_hg, gv_n3, gv_final]
    vec_w = [ffn1_norm, mix_norm, ret_norm_g, hgrn_lb_logits, hgrn_norm_g, ffn2_norm, final_norm[None, :]]
    vec_m = [m_ffn1_norm, m_mix_norm, m_ret_norm_g, m_hgrn_lb_logits, m_hgrn_norm_g, m_ffn2_norm, m_final_norm[None, :]]
    vec_v = [v_ffn1_norm, v_mix_norm, v_ret_norm_g, v_hgrn_lb_logits, v_hgrn_norm_g, v_ffn2_norm, v_final_norm[None, :]]
    cat = lambda ts: jnp.concatenate(ts, axis=1)
    (vec_all,) = _all_gather_call([cat(vec_g)], "gather_vector_grads")
    vres = _adamw_vector_call(vec_all[:, 0, :], cat(vec_w), cat(vec_m), cat(vec_v), "adamw_vectors")
    vec_out = {}
    off = 0
    for nm, t in zip(vec_names, vec_w):
        n = t.shape[1]
        parts = [r[:, off:off + n] for r in vres]
        if nm == "final_norm":
            parts = [p[0] for p in parts]
        vec_out[nm] = parts
        off += n

    loss = lax.psum(loss_part[0, 0], ("x", "y", "c"))
    order = ["ffn1_norm", "ffn1_w_gate", "ffn1_w_up", "ffn1_w_down", "mix_norm", "w_in", "ret_norm_g", "hgrn_lb_logits",
             "hgrn_norm_g", "w_out", "ffn2_norm", "ffn2_w_gate", "ffn2_w_up", "ffn2_w_down", "final_norm"]
    res = {**mat_out, **vec_out}
    outs = [loss, dx0[None]]
    for kind in range(4):
        outs += [res[nm][kind] for nm in order]
    return tuple(outs)
```

```python
import functools

import jax
import jax.numpy as jnp
from jax import lax
from jax.experimental import pallas as pl
from jax.experimental.pallas import tpu as pltpu

BF = jnp.bfloat16
F32 = jnp.float32
MESH = pl.DeviceIdType.MESH
HBM_SPEC = pl.BlockSpec(memory_space=pltpu.HBM)

N_DEV = 8
LANE = 128
EPS = 1e-6
ROPE_BASE = 10000.0
RET_HEADS = 4
HGRN_HEADS = 8
RET_CHUNK = 128
HGRN_BLOCK = 16
FFN_RESIDUAL_WEIGHT = 0.5
ADAM_LR = 0.001
ADAM_B1 = 0.9
ADAM_B2 = 0.999
ADAM_EPS = 1e-08
ADAM_WD = 0.01
ADAM_STEP = 10
VMEM_LIMIT = 56 * 1024 * 1024


def _tile(n, pref, mult=8):
    t = min(pref, n)
    t -= t % mult
    while t >= mult:
        if n % t == 0:
            return t
        t -= mult
    return n


def _params(*sem):
    return pltpu.CompilerParams(dimension_semantics=sem, vmem_limit_bytes=VMEM_LIMIT)


class _Comm:
    def __init__(self, operands, out_shape, n_sems, start, finish, aliases=None, middle=None, middle_at=0.0):
        self.operands = list(operands)
        self.out_shape = list(out_shape)
        self.n_sems = n_sems
        self.start = start
        self.finish = finish
        self.aliases = dict(aliases or {})
        self.middle = middle
        self.middle_at = middle_at


def _launch(body, *, name, grid, in_specs, out_specs, out_shape, sem, args, scratch_shapes=(), aliases=None, comm=None):
    in_specs, out_specs, out_shape = list(in_specs), list(out_specs), list(out_shape)
    scratch_shapes = list(scratch_shapes)
    aliases = dict(aliases or {})
    if comm is None:
        res = pl.pallas_call(body, name=name, grid=grid, in_specs=in_specs, out_specs=out_specs, out_shape=out_shape,
                             scratch_shapes=scratch_shapes, input_output_aliases=aliases,
                             compiler_params=_params(*sem))(*args)
        return list(res), []
    n_in, n_out, n_scr = len(in_specs), len(out_specs), len(scratch_shapes)
    ci, co = len(comm.operands), len(comm.out_shape)

    def carrying(*refs):
        bounds = [0, n_in, n_in + ci, n_in + ci + n_out, n_in + ci + n_out + co, n_in + ci + n_out + co + n_scr]
        ins, cins, outs, couts, scr = [refs[a:b] for a, b in zip(bounds[:-1], bounds[1:])]
        send_sems, recv_sems = refs[bounds[-1]:]
        ids = [pl.program_id(k) for k in range(len(grid))]
        first = functools.reduce(jnp.logical_and, [i == 0 for i in ids])
        last = functools.reduce(jnp.logical_and, [i == g - 1 for i, g in zip(ids, grid)])

        @pl.when(first)
        def _():
            comm.start(cins, couts, send_sems, recv_sems)

        if comm.middle is not None:
            step, total = ids[0], grid[0]
            for i, g in zip(ids[1:], grid[1:]):
                step, total = step * g + i, total * g

            @pl.when(step == int(total * comm.middle_at))
            def _():
                comm.middle(cins, couts, send_sems, recv_sems)

        body(*ins, *outs, *scr)

        @pl.when(last)
        def _():
            comm.finish(cins, couts, send_sems, recv_sems)

    res = pl.pallas_call(
        carrying, name=name, grid=grid,
        in_specs=in_specs + [HBM_SPEC] * ci, out_specs=out_specs + [HBM_SPEC] * co,
        out_shape=out_shape + comm.out_shape,
        scratch_shapes=scratch_shapes + [pltpu.SemaphoreType.DMA((comm.n_sems,)), pltpu.SemaphoreType.DMA((comm.n_sems,))],
        input_output_aliases={**aliases, **{n_in + a: n_out + b for a, b in comm.aliases.items()}},
        compiler_params=_params(*(["arbitrary"] * len(grid))),
    )(*args, *comm.operands)
    return list(res[:n_out]), list(res[n_out:])


def _comm_only_call(comm, name):
    def body(*refs):
        ci, co = len(comm.operands), len(comm.out_shape)
        cins, couts = refs[:ci], refs[ci:ci + co]
        send_sems, recv_sems = refs[ci + co:]
        comm.start(cins, couts, send_sems, recv_sems)
        comm.finish(cins, couts, send_sems, recv_sems)

    return pl.pallas_call(
        body, name=name,
        in_specs=[HBM_SPEC] * len(comm.operands), out_specs=[HBM_SPEC] * len(comm.out_shape),
        out_shape=comm.out_shape,
        scratch_shapes=[pltpu.SemaphoreType.DMA((comm.n_sems,)), pltpu.SemaphoreType.DMA((comm.n_sems,))],
        input_output_aliases=comm.aliases,
    )(*comm.operands)


def _sigmoid(v):
    return 0.5 * jnp.tanh(0.5 * v) + 0.5


def _dot(a, b):
    return jnp.dot(a, b, preferred_element_type=F32)


def _dot_nt(a, b):
    return lax.dot_general(a, b, (((1,), (1,)), ((), ())), preferred_element_type=F32)


def _dot_tn(a, b):
    return lax.dot_general(a, b, (((0,), (0,)), ((), ())), preferred_element_type=F32)


def _rmsnorm_call(x, gain, name):
    s, d = x.shape
    tm = _tile(s, 512)

    def body(x_ref, g_ref, o_ref):
        xv = x_ref[...]
        r = lax.rsqrt(jnp.mean(xv * xv, axis=-1, keepdims=True) + EPS)
        o_ref[...] = (xv * r * g_ref[...]).astype(BF)

    return pl.pallas_call(
        body, name=name, grid=(s // tm,),
        in_specs=[pl.BlockSpec((tm, d), lambda i: (i, 0)), pl.BlockSpec((1, d), lambda i: (0, 0))],
        out_specs=pl.BlockSpec((tm, d), lambda i: (i, 0)),
        out_shape=jax.ShapeDtypeStruct((s, d), BF),
        compiler_params=_params("parallel"),
    )(x, gain)


def _ffn_up_call(h, wg, wu, name, comm=None):
    s, d = h.shape
    nj, _, k = wg.shape
    tm = _tile(s, 1024)

    def body(h_ref, wg_ref, wu_ref, g_ref, u_ref, a_ref):
        hv = h_ref[...]
        g = _dot(hv, wg_ref[...])
        u = _dot(hv, wu_ref[...])
        g_ref[...] = g.astype(BF)
        u_ref[...] = u.astype(BF)
        a_ref[...] = (g * _sigmoid(g) * u).astype(BF)

    act = pl.BlockSpec((tm, k), lambda i, j: (i, j))
    wsp = pl.BlockSpec((None, d, k), lambda i, j: (j, 0, 0))
    return _launch(
        body, name=name, grid=(s // tm, nj),
        in_specs=[pl.BlockSpec((tm, d), lambda i, j: (i, 0)), wsp, wsp],
        out_specs=[act, act, act],
        out_shape=[jax.ShapeDtypeStruct((s, nj * k), BF)] * 3,
        sem=("parallel", "arbitrary"), args=(h, wg, wu), comm=comm)


def _proj_call(h, w, name, comm=None):
    s, d = h.shape
    nj, _, k = w.shape
    tm = _tile(s, 1024)

    def body(h_ref, w_ref, o_ref):
        o_ref[...] = _dot(h_ref[...], w_ref[...])

    return _launch(
        body, name=name, grid=(s // tm, nj),
        in_specs=[pl.BlockSpec((tm, d), lambda i, j: (i, 0)), pl.BlockSpec((None, d, k), lambda i, j: (j, 0, 0))],
        out_specs=[pl.BlockSpec((tm, k), lambda i, j: (i, j))],
        out_shape=[jax.ShapeDtypeStruct((s, nj * k), F32)],
        sem=("parallel", "arbitrary"), args=(h, w), comm=comm)


def _down_call(a, w, resid, gain, scale, name, comm=None):
    s = a.shape[0]
    nj, k, d = w.shape
    tm = _tile(s, 512)
    cn = _tile(d, 512, LANE)
    with_norm = gain is not None

    def body(*refs):
        if with_norm:
            a_ref, w_ref, r_ref, g_ref, x_ref, h_ref, acc = refs
        else:
            a_ref, w_ref, r_ref, x_ref, acc = refs
        j = pl.program_id(1)

        @pl.when(j == 0)
        def _():
            acc[...] = jnp.zeros_like(acc)

        av = a_ref[...]
        for n0 in range(0, d, cn):
            acc[:, n0:n0 + cn] += _dot(av, w_ref[:, n0:n0 + cn])

        @pl.when(j == nj - 1)
        def _():
            xn = r_ref[...] + (scale * acc[...])
            x_ref[...] = xn
            if with_norm:
                r = lax.rsqrt(jnp.mean(xn * xn, axis=-1, keepdims=True) + EPS)
                h_ref[...] = (xn * r * g_ref[...]).astype(BF)

    row = pl.BlockSpec((tm, d), lambda i, j: (i, 0))
    in_specs = [pl.BlockSpec((tm, k), lambda i, j: (i, j)), pl.BlockSpec((None, k, d), lambda i, j: (j, 0, 0)), row]
    args = [a, w, resid]
    out_specs = [row]
    out_shape = [jax.ShapeDtypeStruct((s, d), F32)]
    if with_norm:
        in_specs.append(pl.BlockSpec((1, d), lambda i, j: (0, 0)))
        args.append(gain)
        out_specs.append(row)
        out_shape.append(jax.ShapeDtypeStruct((s, d), BF))
    return _launch(
        body, name=name, grid=(s // tm, nj),
        in_specs=in_specs, out_specs=out_specs, out_shape=out_shape,
        scratch_shapes=[pltpu.VMEM((tm, d), F32)],
        sem=("parallel", "arbitrary"), args=args, comm=comm)


def _loss_call(x, target, gain, name):
    s, d = x.shape
    tm = _tile(s, 512)

    def body(x_ref, t_ref, g_ref, loss_ref, dx_ref, dxb_ref, dg_ref):
        i = pl.program_id(0)

        @pl.when(i == 0)
        def _():
            loss_ref[...] = jnp.zeros_like(loss_ref)
            dg_ref[...] = jnp.zeros_like(dg_ref)

        xv = x_ref[...]
        gv = g_ref[...]
        r = lax.rsqrt(jnp.mean(xv * xv, axis=-1, keepdims=True) + EPS)
        xhat = xv * r
        err = xhat * gv - t_ref[...]
        per_tok = jnp.mean(err * err, axis=-1, keepdims=True)
        loss_ref[...] += 0.5 * jnp.sum(per_tok, axis=0, keepdims=True)
        dout = err * (1.0 / d)
        dg_ref[...] += jnp.sum(dout * xhat, axis=0, keepdims=True)
        dxhat = dout * gv
        dx = r * (dxhat - xhat * jnp.mean(dxhat * xhat, axis=-1, keepdims=True))
        dx_ref[...] = dx
        dxb_ref[...] = dx.astype(BF)

    row = pl.BlockSpec((tm, d), lambda i: (i, 0))
    vec = pl.BlockSpec((1, d), lambda i: (0, 0))
    return pl.pallas_call(
        body, name=name, grid=(s // tm,),
        in_specs=[row, row, vec],
        out_specs=[pl.BlockSpec((1, 1), lambda i: (0, 0)), row, row, vec],
        out_shape=[jax.ShapeDtypeStruct((1, 1), F32), jax.ShapeDtypeStruct((s, d), F32),
                   jax.ShapeDtypeStruct((s, d), BF), jax.ShapeDtypeStruct((1, d), F32)],
        compiler_params=_params("arbitrary"),
    )(x, target, gain)


def _bwd_up_call(dy, wd, g, u, scale, name, comm=None):
    s, d = dy.shape
    nj, k, _ = wd.shape
    tm = _tile(s, 1024)

    def body(dy_ref, w_ref, g_ref, u_ref, dg_ref, du_ref):
        da = scale * _dot_nt(dy_ref[...], w_ref[...])
        gv = g_ref[...].astype(F32)
        sig = _sigmoid(gv)
        du_ref[...] = (da * gv * sig).astype(BF)
        dg_ref[...] = (da * u_ref[...].astype(F32) * sig * (1.0 + gv * (1.0 - sig))).astype(BF)

    act = pl.BlockSpec((tm, k), lambda i, j: (i, j))
    return _launch(
        body, name=name, grid=(s // tm, nj),
        in_specs=[pl.BlockSpec((tm, d), lambda i, j: (i, 0)), pl.BlockSpec((None, k, d), lambda i, j: (j, 0, 0)), act, act],
        out_specs=[act, act],
        out_shape=[jax.ShapeDtypeStruct((s, nj * k), BF), jax.ShapeDtypeStruct((s, nj * k), BF)],
        sem=("parallel", "arbitrary"), args=(dy, wd, g, u), comm=comm)


def _bwd_down_call(pairs, dres, xin, gain, name, per_step=1, tiles=None, carry=None, comm=None):
    s, d = xin.shape
    nblocks, _, k = pairs[0][1].shape
    nj = nblocks // per_step
    npair = len(pairs) * per_step
    tm = _tile(s, 512)
    strip = _tile(tm, 128)
    cn = _tile(d, 512, LANE)

    first_tile, n_tiles = (0, s // tm) if tiles is None else tiles

    def body(*refs):
        a_refs = refs[0:2 * npair:2]
        w_refs = refs[1:2 * npair:2]
        if carry is None:
            dres_ref, x_ref, g_ref, dx_ref, dxb_ref, dg_ref, acc = refs[2 * npair:]
        else:
            dres_ref, x_ref, g_ref, _, _, dg_prev_ref, dx_ref, dxb_ref, dg_ref, acc = refs[2 * npair:]
        i = pl.program_id(0)
        j = pl.program_id(1)

        @pl.when(j == 0)
        def _():
            acc[...] = jnp.zeros_like(acc)

        @pl.when((i == 0) & (j == 0))
        def _():
            dg_ref[...] = jnp.zeros_like(dg_ref) if carry is None else dg_prev_ref[...]

        for a_ref, w_ref in zip(a_refs, w_refs):
            av = a_ref[...]
            for n0 in range(0, d, cn):
                acc[:, n0:n0 + cn] += _dot_nt(av, w_ref[n0:n0 + cn, :])

        @pl.when(j == nj - 1)
        def _():
            for r0 in range(0, tm, strip):
                rows = slice(r0, r0 + strip)
                xv = x_ref[rows, :]
                r = lax.rsqrt(jnp.mean(xv * xv, axis=-1, keepdims=True) + EPS)
                xhat = xv * r
                dh = acc[rows, :]
                dg_ref[...] += jnp.sum(dh * xhat, axis=0, keepdims=True)
                dxhat = dh * g_ref[...]
                dx = dres_ref[rows, :] + r * (dxhat - xhat * jnp.mean(dxhat * xhat, axis=-1, keepdims=True))
                dx_ref[rows, :] = dx
                dxb_ref[rows, :] = dx.astype(BF)

    row = pl.BlockSpec((tm, d), lambda i, j: (first_tile + i, 0))
    vec = pl.BlockSpec((1, d), lambda i, j: (0, 0))
    in_specs, args = [], []
    for a, w in pairs:
        for r in range(per_step):
            in_specs += [pl.BlockSpec((None, tm, k), lambda i, j, r=r: (j * per_step + r, first_tile + i, 0))
                         if a.ndim == 3 else
                         pl.BlockSpec((tm, k), lambda i, j, r=r: (first_tile + i, j * per_step + r)),
                         pl.BlockSpec((None, d, k), lambda i, j, r=r: (j * per_step + r, 0, 0))]
            args += [a, w]
    once = pl.BlockSpec((tm, d), lambda i, j: (first_tile + i, 0), pipeline_mode=pl.Buffered(1))
    in_specs += [once, once, vec]
    args += [dres, xin, gain]
    aliases = {}
    if carry is not None:
        n_before = len(in_specs)
        in_specs += [pl.BlockSpec(memory_space=pl.ANY), pl.BlockSpec(memory_space=pl.ANY), vec]
        args += list(carry)
        aliases = {n_before: 0, n_before + 1: 1}
    return _launch(
        body, name=name, grid=(n_tiles, nj),
        in_specs=in_specs, out_specs=[row, row, vec],
        out_shape=[jax.ShapeDtypeStruct((s, d), F32), jax.ShapeDtypeStruct((s, d), BF), jax.ShapeDtypeStruct((1, d), F32)],
        scratch_shapes=[pltpu.VMEM((tm, d), F32)],
        sem=("arbitrary", "arbitrary"), args=args, aliases=aliases, comm=comm)


def _wgrad_call(a, b, nj, a_blocked, scale, name, comm=None):
    s = a.shape[0]
    ka = a.shape[1] // nj if a_blocked else a.shape[1]
    b_stacked = b.ndim == 3
    kb = b.shape[-1] if (a_blocked or b_stacked) else b.shape[1] // nj
    ts = _tile(s, 2048)
    ns = s // ts

    def body(a_ref, b_ref, o_ref):
        t = pl.program_id(1)

        @pl.when(t == 0)
        def _():
            o_ref[...] = jnp.zeros_like(o_ref)

        o_ref[...] += _dot_tn(a_ref[...], b_ref[...])
        if scale != 1.0:
            @pl.when(t == ns - 1)
            def _():
                o_ref[...] = o_ref[...] * scale

    a_spec = pl.BlockSpec((ts, ka), (lambda j, t: (t, j)) if a_blocked else (lambda j, t: (t, 0)))
    if b_stacked:
        b_spec = pl.BlockSpec((None, ts, kb), lambda j, t: (j, t, 0))
    else:
        b_spec = pl.BlockSpec((ts, kb), (lambda j, t: (t, 0)) if a_blocked else (lambda j, t: (t, j)))
    (out,), landed = _launch(
        body, name=name, grid=(nj, ns),
        in_specs=[a_spec, b_spec],
        out_specs=[pl.BlockSpec((None, ka, kb), lambda j, t: (j, 0, 0))],
        out_shape=[jax.ShapeDtypeStruct((nj, ka, kb), F32)],
        sem=("parallel", "arbitrary"), args=(a, b), comm=comm)
    return out if comm is None else (out, landed)


def _rope(v, cos, sin):
    half = v.shape[-1] // 2
    v1, v2 = v[:, :half], v[:, half:]
    return jnp.concatenate([v1 * cos - v2 * sin, v2 * cos + v1 * sin], axis=-1)


def _rope_bwd(dv, cos, sin):
    half = dv.shape[-1] // 2
    d1, d2 = dv[:, :half], dv[:, half:]
    return jnp.concatenate([d1 * cos + d2 * sin, d2 * cos - d1 * sin], axis=-1)


def _ret_consts(hd):
    c = RET_CHUNK
    log_gamma = jnp.log(1.0 - jnp.exp2(-5.0 - jnp.arange(RET_HEADS, dtype=F32)))
    idx = jnp.arange(c, dtype=F32)
    rel = idx[:, None] - idx[None, :]
    mask = rel >= 0
    decay = jnp.where(mask[None], jnp.exp(log_gamma[:, None, None] * jnp.where(mask, rel, 0.0)[None]), 0.0)
    qdec = jnp.exp(log_gamma[:, None] * (idx + 1.0)[None, :])
    kdec = jnp.exp(log_gamma[:, None] * (c - 1.0 - idx)[None, :])
    gchunk = jnp.exp(log_gamma * c)
    bc = lambda t: jnp.broadcast_to(t[:, :, None], (RET_HEADS, t.shape[1], hd))
    return decay, bc(qdec), bc(kdec), bc(gchunk[:, None])


def _rope_tables(s, hd):
    inv = jnp.power(ROPE_BASE, -jnp.arange(0, hd, 2, dtype=F32) / hd)
    ang = jnp.arange(s, dtype=F32)[:, None] * inv[None, :]
    return jnp.cos(ang), jnp.sin(ang)


def _ret_fwd_call(proj, cos, sin, consts, gret, name, comm=None):
    s = proj.shape[0]
    w = proj.shape[1] // 8
    hd = w // RET_HEADS
    c = RET_CHUNK
    tt = _tile(s, 512, c)
    nc = tt // c
    decay, qdec, kdec, gch = consts
    scale = hd ** -0.5

    def body(q_ref, k_ref, v_ref, gate_ref, cos_ref, sin_ref, dec_ref, qd_ref, kd_ref, gc_ref, gn_ref,
             o_ref, m_ref, st_ref, state):
        @pl.when(pl.program_id(1) == 0)
        def _():
            state[...] = jnp.zeros_like(state)

        dec = dec_ref[...]
        for ci in range(nc):
            rows = slice(ci * c, (ci + 1) * c)
            cs, sn = cos_ref[rows, :], sin_ref[rows, :]
            q = _rope(q_ref[rows, :], cs, sn) * scale
            k = _rope(k_ref[rows, :], cs, sn)
            vb = v_ref[rows, :].astype(BF)
            sc = _dot_nt(q.astype(BF), k.astype(BF)) * dec
            prev = state[...]
            st_ref[ci] = prev
            o = _dot(sc.astype(BF), vb) + _dot((q * qd_ref[...]).astype(BF), prev.astype(BF))
            state[...] = gc_ref[...] * prev + _dot_tn((k * kd_ref[...]).astype(BF), vb)
            o_ref[rows, :] = o
            mu = jnp.mean(o, axis=-1, keepdims=True)
            cen = o - mu
            xhat = cen * lax.rsqrt(jnp.mean(cen * cen, axis=-1, keepdims=True) + EPS)
            gt = gate_ref[rows, :]
            m_ref[rows, :] = (xhat * gn_ref[...] * (gt * _sigmoid(gt))).astype(BF)

    nh = RET_HEADS
    comp = lambda j: pl.BlockSpec((tt, hd), lambda h, t, j=j: (t, j * nh + h))
    tab = pl.BlockSpec((tt, hd // 2), lambda h, t: (t, 0))
    per_head = lambda r: pl.BlockSpec((None, r, hd), lambda h, t: (h, 0, 0))
    return _launch(
        body, name=name, grid=(nh, s // tt),
        in_specs=[comp(0), comp(1), comp(2), comp(3), tab, tab,
                  pl.BlockSpec((None, c, c), lambda h, t: (h, 0, 0)), per_head(c), per_head(c), per_head(1),
                  pl.BlockSpec((1, hd), lambda h, t: (0, h))],
        out_specs=[pl.BlockSpec((tt, hd), lambda h, t: (t, h)), pl.BlockSpec((tt, hd), lambda h, t: (t, h)),
                   pl.BlockSpec((None, nc, hd, hd), lambda h, t: (h, t, 0, 0))],
        out_shape=[jax.ShapeDtypeStruct((s, w), F32), jax.ShapeDtypeStruct((s, w), BF),
                   jax.ShapeDtypeStruct((nh, s // c, hd, hd), F32)],
        scratch_shapes=[pltpu.VMEM((hd, hd), F32)],
        sem=("parallel", "arbitrary"),
        args=(proj, proj, proj, proj, cos, sin, decay, qdec, kdec, gch, gret), comm=comm)


def _ret_bwd_call(proj, cos, sin, consts, gret, o_raw, states, dmerged, name):
    s = proj.shape[0]
    w = proj.shape[1] // 8
    hd = w // RET_HEADS
    c = RET_CHUNK
    tt = _tile(s, 512, c)
    nc = tt // c
    nt = s // tt
    decay, qdec, kdec, gch = consts
    scale = hd ** -0.5

    def body(q_ref, k_ref, v_ref, gate_ref, cos_ref, sin_ref, dec_ref, qd_ref, kd_ref, gc_ref, gn_ref,
             o_ref, st_ref, dm_ref, dp_ref, dgn_ref, dstate):
        @pl.when(pl.program_id(1) == 0)
        def _():
            dstate[...] = jnp.zeros_like(dstate)
            dgn_ref[...] = jnp.zeros_like(dgn_ref)

        dec = dec_ref[...]
        gn = gn_ref[...]
        for ci in reversed(range(nc)):
            rows = slice(ci * c, (ci + 1) * c)
            cs, sn = cos_ref[rows, :], sin_ref[rows, :]
            q = _rope(q_ref[rows, :], cs, sn) * scale
            k = _rope(k_ref[rows, :], cs, sn)
            qb, kb = q.astype(BF), k.astype(BF)
            vb = v_ref[rows, :].astype(BF)
            sc = _dot_nt(qb, kb) * dec
            o = o_ref[rows, :]
            mu = jnp.mean(o, axis=-1, keepdims=True)
            cen = o - mu
            rstd = lax.rsqrt(jnp.mean(cen * cen, axis=-1, keepdims=True) + EPS)
            xhat = cen * rstd
            gt = gate_ref[rows, :]
            sig = _sigmoid(gt)
            sg = gt * sig
            dm = dm_ref[rows, :]
            dgn_ref[...] += jnp.sum(dm * xhat * sg, axis=0, keepdims=True)
            dp_ref[3, rows, :] = (dm * xhat * gn * sig * (1.0 + gt * (1.0 - sig))).astype(BF)
            dxhat = dm * gn * sg
            do = rstd * (dxhat - jnp.mean(dxhat, axis=-1, keepdims=True)
                         - xhat * jnp.mean(dxhat * xhat, axis=-1, keepdims=True))
            dob = do.astype(BF)
            prev = st_ref[ci]
            ds = dstate[...]
            dsb = ds.astype(BF)
            dsc = (_dot_nt(dob, vb) * dec).astype(BF)
            dq = _dot(dsc, kb) + _dot_nt(dob, prev.astype(BF)) * qd_ref[...]
            dk = _dot_tn(dsc, qb) + _dot_nt(vb, dsb) * kd_ref[...]
            dv = _dot_tn(sc.astype(BF), dob) + _dot((k * kd_ref[...]).astype(BF), dsb)
            dstate[...] = gc_ref[...] * ds + _dot_tn((q * qd_ref[...]).astype(BF), dob)
            dp_ref[0, rows, :] = _rope_bwd(dq * scale, cs, sn).astype(BF)
            dp_ref[1, rows, :] = _rope_bwd(dk, cs, sn).astype(BF)
            dp_ref[2, rows, :] = dv.astype(BF)

    nh = RET_HEADS
    rev = lambda t: nt - 1 - t
    comp = lambda j: pl.BlockSpec((tt, hd), lambda h, t, j=j: (rev(t), j * nh + h))
    tab = pl.BlockSpec((tt, hd // 2), lambda h, t: (rev(t), 0))
    per_head = lambda r: pl.BlockSpec((None, r, hd), lambda h, t: (h, 0, 0))
    head_cols = pl.BlockSpec((tt, hd), lambda h, t: (rev(t), h))
    gvec = pl.BlockSpec((1, hd), lambda h, t: (0, h))
    act = jax.ShapeDtypeStruct((s, w), BF)
    return pl.pallas_call(
        body, name=name, grid=(nh, nt),
        in_specs=[comp(0), comp(1), comp(2), comp(3), tab, tab,
                  pl.BlockSpec((None, c, c), lambda h, t: (h, 0, 0)), per_head(c), per_head(c), per_head(1), gvec,
                  head_cols, pl.BlockSpec((None, nc, hd, hd), lambda h, t: (h, rev(t), 0, 0)), head_cols],
        out_specs=[pl.BlockSpec((4, tt, hd), lambda h, t: (0, rev(t), h)), gvec],
        out_shape=[jax.ShapeDtypeStruct((8, s, w), BF), jax.ShapeDtypeStruct((1, w), F32)],
        scratch_shapes=[pltpu.VMEM((hd, hd), F32)],
        compiler_params=_params("parallel", "arbitrary"),
    )(proj, proj, proj, proj, cos, sin, decay, qdec, kdec, gch, gret, o_raw, states, dmerged)


def _block_tri(n, bs, upper):
    r = jnp.arange(n)[:, None]
    cidx = jnp.arange(n)[None, :]
    same = (r // bs) == (cidx // bs)
    return jnp.where(same & ((cidx >= r) if upper else (cidx <= r)), 1.0, 0.0).astype(F32)


def _dot_exact(a, b):
    return jnp.dot(a, b, preferred_element_type=F32, precision=lax.Precision.HIGHEST)


def _hgrn_gates(z, lbv):
    sz = _sigmoid(z)
    oml = 1.0 - lbv
    f = lbv + oml * sz
    key = oml * (1.0 - sz)
    return sz, f, key


def _hgrn_fwd_call(proj, lb_logits, ghg, name, comm=None):
    s = proj.shape[0]
    w = proj.shape[1] // 8
    nh = HGRN_HEADS
    hd = w // nh
    bs = HGRN_BLOCK
    tt = _tile(s, 256, bs)
    nb = tt // bs

    def body(q_ref, z_ref, v_ref, gate_ref, lb_ref, gn_ref, tril_ref, o_ref, m_ref, st_ref, state, upd):
        @pl.when(pl.program_id(1) == 0)
        def _():
            state[...] = jnp.zeros_like(state)

        lbv = _sigmoid(lb_ref[...])
        _, f, key = _hgrn_gates(z_ref[...], lbv)
        qr = q_ref[...]
        q = qr * _sigmoid(qr)
        v = v_ref[...]
        g = _dot_exact(tril_ref[...], jnp.log(f))
        blocks = lambda t: t.reshape(nb, bs, hd)
        g3, q3, k3, v3 = blocks(g), blocks(q), blocks(key), blocks(v)
        glast3 = g3[:, bs - 1:bs, :]
        row_id = lax.broadcasted_iota(jnp.int32, (nb, bs, hd), 1)
        o3 = jnp.zeros((nb, bs, hd), F32)
        for j in range(bs):
            wj = jnp.where(row_id >= j, jnp.exp(jnp.minimum(g3 - g3[:, j:j + 1, :], 0.0)), 0.0)
            a = jnp.sum(q3 * k3[:, j:j + 1, :] * wj, axis=-1, keepdims=True)
            o3 = o3 + a * v3[:, j:j + 1, :]
        ktb = (k3 * jnp.exp(glast3 - g3)).reshape(tt, hd).astype(BF)
        vb = v.astype(BF)
        for b in range(nb):
            rows = slice(b * bs, (b + 1) * bs)
            upd[b] = _dot_tn(vb[rows, :], ktb[rows, :])
        egl3 = jnp.exp(glast3)
        st = state[...]
        for b in range(nb):
            st_ref[b] = st
            st = st * egl3[b] + upd[b]
        state[...] = st
        qgb = (q * jnp.exp(g)).astype(BF)
        o_intra = o3.reshape(tt, hd)
        gn = gn_ref[...]
        for b in range(nb):
            rows = slice(b * bs, (b + 1) * bs)
            o = o_intra[rows, :] + _dot_nt(qgb[rows, :], st_ref[b].astype(BF))
            o_ref[rows, :] = o
            gt = gate_ref[rows, :]
            xhat = o * lax.rsqrt(jnp.mean(o * o, axis=-1, keepdims=True) + EPS)
            m_ref[rows, :] = (xhat * gn * (gt * _sigmoid(gt))).astype(BF)

    comp = lambda j: pl.BlockSpec((tt, hd), lambda h, t, j=j: (t, j * nh + h))
    gvec = pl.BlockSpec((1, hd), lambda h, t: (0, h))
    head_cols = pl.BlockSpec((tt, hd), lambda h, t: (t, h))
    return _launch(
        body, name=name, grid=(nh, s // tt),
        in_specs=[comp(4), comp(5), comp(6), comp(7), gvec, gvec, pl.BlockSpec((tt, tt), lambda h, t: (0, 0))],
        out_specs=[head_cols, head_cols, pl.BlockSpec((None, nb, hd, hd), lambda h, t: (h, t, 0, 0))],
        out_shape=[jax.ShapeDtypeStruct((s, w), F32), jax.ShapeDtypeStruct((s, w), BF),
                   jax.ShapeDtypeStruct((nh, s // bs, hd, hd), F32)],
        scratch_shapes=[pltpu.VMEM((hd, hd), F32), pltpu.VMEM((nb, hd, hd), F32)],
        sem=("parallel", "arbitrary"),
        args=(proj, proj, proj, proj, lb_logits, ghg, _block_tri(tt, bs, upper=False)), comm=comm)


def _hgrn_bwd_call(proj, lb_logits, ghg, o_raw, states, dmerged, stack, name, comm=None):
    s = proj.shape[0]
    w = proj.shape[1] // 8
    nh = HGRN_HEADS
    hd = w // nh
    bs = HGRN_BLOCK
    tt = _tile(s, 256, bs)
    nb = tt // bs
    nt = s // tt

    def body(q_ref, z_ref, v_ref, gate_ref, lb_ref, gn_ref, tril_ref, triu_ref, o_ref, st_ref, dm_ref, stack_ref,
             dp_ref, dlb_ref, dgn_ref,
             dstate, ds_all, inc, dq_s, dk_s, dv_s, dgl_s, dk_rows, dv_rows):
        @pl.when(pl.program_id(1) == 0)
        def _():
            dstate[...] = jnp.zeros_like(dstate)
            dlb_ref[...] = jnp.zeros_like(dlb_ref)
            dgn_ref[...] = jnp.zeros_like(dgn_ref)

        lbv = _sigmoid(lb_ref[...])
        oml = 1.0 - lbv
        gn = gn_ref[...]
        sz, f, key = _hgrn_gates(z_ref[...], lbv)
        qr = q_ref[...]
        sq = _sigmoid(qr)
        q = qr * sq
        v = v_ref[...]
        g = _dot_exact(tril_ref[...], jnp.log(f))
        eg = jnp.exp(g)
        blocks = lambda t: t.reshape(nb, bs, hd)
        g3, q3, k3, v3 = blocks(g), blocks(q), blocks(key), blocks(v)
        glast3 = g3[:, bs - 1:bs, :]
        egl3 = jnp.exp(glast3)
        ktail3 = jnp.exp(glast3 - g3)
        o = o_ref[...]
        rstd = lax.rsqrt(jnp.mean(o * o, axis=-1, keepdims=True) + EPS)
        xhat = o * rstd
        gt = gate_ref[...]
        sig = _sigmoid(gt)
        sg = gt * sig
        dm = dm_ref[...]
        dgn_ref[...] += jnp.sum(dm * xhat * sg, axis=0, keepdims=True)
        del stack_ref
        dp_ref[3] = (dm * xhat * gn * sig * (1.0 + gt * (1.0 - sig))).astype(BF)
        dxhat = dm * gn * sg
        do = rstd * (dxhat - xhat * jnp.mean(dxhat * xhat, axis=-1, keepdims=True))
        dob = do.astype(BF)
        do3 = blocks(do)
        qgb = (q * eg).astype(BF)
        for b in range(nb):
            rows = slice(b * bs, (b + 1) * bs)
            inc[b] = _dot_tn(dob[rows, :], qgb[rows, :])
        ds = dstate[...]
        for b in reversed(range(nb)):
            ds_all[b] = ds
            ds = ds * egl3[b] + inc[b]
        dstate[...] = ds
        ktb = (k3 * ktail3).reshape(tt, hd).astype(BF)
        vb = v.astype(BF)
        for b in range(nb):
            rows = slice(b * bs, (b + 1) * bs)
            prev = st_ref[b]
            dsb = ds_all[b]
            dsbb = dsb.astype(BF)
            dq_s[rows, :] = _dot(dob[rows, :], prev.astype(BF))
            dk_s[rows, :] = _dot(vb[rows, :], dsbb)
            dv_s[rows, :] = _dot_nt(ktb[rows, :], dsbb)
            dgl_s[b] = jnp.sum(prev * dsb, axis=0, keepdims=True)
        dq3 = blocks(dq_s[...] * eg)
        dk3 = blocks(dk_s[...]) * ktail3
        dg_last3 = jnp.sum(k3 * dk3, axis=1, keepdims=True) + egl3 * dgl_s[...]
        row_id = lax.broadcasted_iota(jnp.int32, (nb, bs, hd), 1)
        for j in range(bs):
            wj = jnp.where(row_id >= j, jnp.exp(jnp.minimum(g3 - g3[:, j:j + 1, :], 0.0)), 0.0)
            kj = k3[:, j:j + 1, :]
            a = jnp.sum(q3 * kj * wj, axis=-1, keepdims=True)
            da = jnp.sum(do3 * v3[:, j:j + 1, :], axis=-1, keepdims=True)
            dv_rows[:, j:j + 1, :] = jnp.sum(a * do3, axis=1, keepdims=True)
            dq3 = dq3 + da * kj * wj
            dk_rows[:, j:j + 1, :] = jnp.sum(da * q3 * wj, axis=1, keepdims=True)
        dk3 = dk3 + dk_rows[...]
        dv = dv_s[...] + dv_rows[...].reshape(tt, hd)
        dg3 = q3 * dq3 - k3 * dk3 + jnp.where(row_id == bs - 1, dg_last3, 0.0)
        dlf = _dot_exact(triu_ref[...], dg3.reshape(tt, hd))
        dk = dk3.reshape(tt, hd)
        dfk = dlf / f - dk
        dlb_ref[...] += jnp.sum(dfk * (1.0 - sz), axis=0, keepdims=True) * (lbv * oml)
        dp_ref[1] = (dfk * oml * sz * (1.0 - sz)).astype(BF)
        dp_ref[0] = (dq3.reshape(tt, hd) * sq * (1.0 + qr * (1.0 - sq))).astype(BF)
        dp_ref[2] = dv.astype(BF)

    rev = lambda t: nt - 1 - t
    comp = lambda j: pl.BlockSpec((tt, hd), lambda h, t, j=j: (rev(t), j * nh + h))
    gvec = pl.BlockSpec((1, hd), lambda h, t: (0, h))
    head_cols = pl.BlockSpec((tt, hd), lambda h, t: (rev(t), h))
    tri = pl.BlockSpec((tt, tt), lambda h, t: (0, 0))
    act = jax.ShapeDtypeStruct((s, w), BF)
    vec = jax.ShapeDtypeStruct((1, w), F32)
    tile_f32 = pltpu.VMEM((tt, hd), F32)
    return _launch(
        body, name=name, grid=(nh, nt),
        in_specs=[comp(4), comp(5), comp(6), comp(7), gvec, gvec, tri, tri, head_cols,
                  pl.BlockSpec((None, nb, hd, hd), lambda h, t: (h, rev(t), 0, 0)),
                  pl.BlockSpec((tt, hd), lambda h, t: (rev(t), nh + h)),
                  pl.BlockSpec(memory_space=pl.ANY)],
        out_specs=[pl.BlockSpec((4, tt, hd), lambda h, t: (1, rev(t), h)), gvec, gvec],
        out_shape=[jax.ShapeDtypeStruct(stack.shape, stack.dtype), vec, vec],
        scratch_shapes=[pltpu.VMEM((hd, hd), F32), pltpu.VMEM((nb, hd, hd), F32), pltpu.VMEM((nb, hd, hd), F32),
                        tile_f32, tile_f32, tile_f32, pltpu.VMEM((nb, 1, hd), F32),
                        pltpu.VMEM((nb, bs, hd), F32), pltpu.VMEM((nb, bs, hd), F32)],
        sem=("parallel", "arbitrary"),
        args=(proj, proj, proj, proj, lb_logits, ghg, _block_tri(tt, bs, upper=False), _block_tri(tt, bs, upper=True),
              o_raw, states, dmerged, stack), aliases={11: 0}, comm=comm)


def _position():
    return lax.axis_index("x"), lax.axis_index("y"), lax.axis_index("c")


def _all_gather_call(shards, name):
    n = len(shards)

    def body(*refs):
        ins, outs = refs[:n], refs[n:2 * n]
        send_sems, recv_sems, local_sems = refs[2 * n:]
        x, y, c = _position()
        me, sibling = (x, y, c), (x, y, 1 - c)
        chips = [(1 - x, y), (x, 1 - y), (1 - x, 1 - y)]

        def slot(a, p):
            return outs[a].at[4 * p[0] + 2 * p[1] + p[2]]

        def copy(a, k, block, to, src=None):
            return pltpu.make_async_remote_copy(
                src_ref=slot(a, block) if src is None else src, dst_ref=slot(a, block),
                send_sem=send_sems.at[a * 7 + k], recv_sem=recv_sems.at[a * 7 + k],
                device_id=to, device_id_type=MESH)

        mine = [pltpu.make_async_copy(ins[a], slot(a, me), local_sems.at[a]) for a in range(n)]
        for cp in mine:
            cp.start()
        first = []
        for a in range(n):
            first.append(copy(a, 0, me, sibling, src=ins[a]))
            first += [copy(a, 1 + j, me, (*chip, c), src=ins[a]) for j, chip in enumerate(chips)]
        for cp in first:
            cp.start()
        passed = []
        for j, chip in enumerate(chips):
            for a in range(n):
                copy(a, 1 + j, (*chip, c), me).wait_recv()
                fwd = copy(a, 4 + j, (*chip, c), sibling)
                fwd.start()
                passed.append(fwd)
        for a in range(n):
            copy(a, 0, sibling, me).wait_recv()
            for j, chip in enumerate(chips):
                copy(a, 4 + j, (*chip, 1 - c), me).wait_recv()
        for cp in first + passed:
            cp.wait_send()
        for cp in mine:
            cp.wait()

    return pl.pallas_call(
        body, name=name,
        in_specs=[HBM_SPEC] * n, out_specs=[HBM_SPEC] * n,
        out_shape=[jax.ShapeDtypeStruct((N_DEV,) + t.shape, t.dtype) for t in shards],
        scratch_shapes=[pltpu.SemaphoreType.DMA((7 * n,)), pltpu.SemaphoreType.DMA((7 * n,)),
                        pltpu.SemaphoreType.DMA((n,))],
    )(*shards)


def _slot(ref, p):
    return ref.at[4 * p[0] + 2 * p[1] + p[2]]


def _gather_round1(shards):
    n = len(shards)

    def plan(ins, outs, send_sems, recv_sems):
        x, y, c = _position()
        me = (x, y, c)
        peers = [(x, y, 1 - c), (1 - x, y, c), (x, 1 - y, c), (1 - x, 1 - y, c)]
        sends, recvs, local = [], [], []
        for a in range(n):
            local.append(pltpu.make_async_copy(ins[a], _slot(outs[a], me), send_sems.at[4 * n + a]))
            for k, peer in enumerate(peers):
                sems = dict(send_sem=send_sems.at[4 * a + k], recv_sem=recv_sems.at[4 * a + k],
                            device_id=peer, device_id_type=MESH)
                sends.append(pltpu.make_async_remote_copy(src_ref=ins[a], dst_ref=_slot(outs[a], me), **sems))
                recvs.append(pltpu.make_async_remote_copy(src_ref=ins[a], dst_ref=_slot(outs[a], peer), **sems))
        return sends, recvs, local

    def start(*refs):
        sends, _, local = plan(*refs)
        for cp in local + sends:
            cp.start()

    def finish(*refs):
        sends, recvs, local = plan(*refs)
        for cp in recvs:
            cp.wait_recv()
        for cp in sends:
            cp.wait_send()
        for cp in local:
            cp.wait()

    return _Comm(shards, [jax.ShapeDtypeStruct((N_DEV,) + t.shape, t.dtype) for t in shards], 5 * n, start, finish)


def _gather_round2(gathered):
    n = len(gathered)

    def plan(ins, outs, send_sems, recv_sems):
        x, y, c = _position()
        chips = [(1 - x, y), (x, 1 - y), (1 - x, 1 - y)]
        sends, recvs = [], []
        for a in range(n):
            for k, chip in enumerate(chips):
                sems = dict(send_sem=send_sems.at[3 * a + k], recv_sem=recv_sems.at[3 * a + k],
                            device_id=(x, y, 1 - c), device_id_type=MESH)
                sends.append(pltpu.make_async_remote_copy(
                    src_ref=_slot(ins[a], (*chip, c)), dst_ref=_slot(outs[a], (*chip, c)), **sems))
                recvs.append(pltpu.make_async_remote_copy(
                    src_ref=_slot(ins[a], (*chip, c)), dst_ref=_slot(outs[a], (*chip, 1 - c)), **sems))
        return sends, recvs

    def start(*refs):
        for cp in plan(*refs)[0]:
            cp.start()

    def finish(*refs):
        sends, recvs = plan(*refs)
        for cp in recvs:
            cp.wait_recv()
        for cp in sends:
            cp.wait_send()

    return _Comm(gathered, [jax.ShapeDtypeStruct(t.shape, t.dtype) for t in gathered], 3 * n, start, finish,
                 aliases={a: a for a in range(n)})


def _gather_two_level(shards, forward_at):
    n = len(shards)
    first, second = _gather_round1(shards), _gather_round2(shards)

    def middle(ins, outs, send_sems, recv_sems):
        first.finish(ins, outs, send_sems, recv_sems)
        second.start(outs, outs, _SemWindow(send_sems, first.n_sems), _SemWindow(recv_sems, first.n_sems))

    def finish(ins, outs, send_sems, recv_sems):
        second.finish(outs, outs, _SemWindow(send_sems, first.n_sems), _SemWindow(recv_sems, first.n_sems))

    return _Comm(shards, first.out_shape, first.n_sems + second.n_sems, first.start, finish,
                 middle=middle, middle_at=forward_at)


def _sibling_exchange(grads):
    n = len(grads)

    def plan(ins, outs, send_sems, recv_sems):
        x, y, c = _position()
        return [pltpu.make_async_remote_copy(
            src_ref=ins[a].at[2 * q + (1 - c)], dst_ref=outs[a].at[q],
            send_sem=send_sems.at[a * 4 + q], recv_sem=recv_sems.at[a * 4 + q],
            device_id=(x, y, 1 - c), device_id_type=MESH) for a in range(n) for q in range(4)]

    def start(*refs):
        for cp in plan(*refs):
            cp.start()

    def finish(*refs):
        for cp in plan(*refs):
            cp.wait()

    return _Comm(grads, [jax.ShapeDtypeStruct((4,) + t.shape[1:], t.dtype) for t in grads], 4 * n, start, finish)


def _chip_exchange(partials):
    n = len(partials)

    def plan(ins, outs, send_sems, recv_sems):
        x, y, c = _position()
        chips = [(1 - x, y), (x, 1 - y), (1 - x, 1 - y)]
        return [pltpu.make_async_remote_copy(
            src_ref=ins[a].at[2 * chip[0] + chip[1]], dst_ref=outs[a].at[k],
            send_sem=send_sems.at[a * 3 + k], recv_sem=recv_sems.at[a * 3 + k],
            device_id=(*chip, c), device_id_type=MESH) for a in range(n) for k, chip in enumerate(chips)]

    def start(*refs):
        for cp in plan(*refs):
            cp.start()

    def finish(*refs):
        for cp in plan(*refs):
            cp.wait()

    return _Comm(partials, [jax.ShapeDtypeStruct((3,) + t.shape[1:], t.dtype) for t in partials], 3 * n, start, finish)


class _SemWindow:
    def __init__(self, sems, offset):
        self._sems, self._offset = sems, offset

    @property
    def at(self):
        return self

    def __getitem__(self, i):
        return self._sems.at[self._offset + i]


def _join(parts):
    def each(fn_name, cins, couts, send_sems, recv_sems):
        i = o = sem = 0
        for p in parts:
            ni, no = len(p.operands), len(p.out_shape)
            getattr(p, fn_name)(cins[i:i + ni], couts[o:o + no], _SemWindow(send_sems, sem), _SemWindow(recv_sems, sem))
            i, o, sem = i + ni, o + no, sem + p.n_sems

    assert not any(p.aliases for p in parts)
    return _Comm([t for p in parts for t in p.operands], [t for p in parts for t in p.out_shape],
                 sum(p.n_sems for p in parts), functools.partial(each, "start"), functools.partial(each, "finish"))


def _pair_sum_call(grad, recv, parity, name):
    _, r, ccols = grad.shape
    tr = _tile(r, 256)

    def body(par_ref, g_ref, r_ref, p_ref, pb_ref):
        del par_ref
        p = g_ref[...] + r_ref[...]
        p_ref[...] = p
        pb_ref[...] = p.astype(BF)

    blk = lambda fn: pl.BlockSpec((None, tr, ccols), fn)
    return pl.pallas_call(
        body, name=name,
        grid_spec=pltpu.PrefetchScalarGridSpec(
            num_scalar_prefetch=1, grid=(4, r // tr),
            in_specs=[blk(lambda q, i, par: (2 * q + par[0], i, 0)), blk(lambda q, i, par: (q, i, 0))],
            out_specs=[blk(lambda q, i, par: (q, i, 0)), blk(lambda q, i, par: (q, i, 0))]),
        out_shape=[jax.ShapeDtypeStruct((4, r, ccols), F32), jax.ShapeDtypeStruct((4, r, ccols), BF)],
        compiler_params=_params("parallel", "parallel"),
    )(parity, grad, recv)


def _adamw_math(w, g, m, v):
    m = ADAM_B1 * m + (1.0 - ADAM_B1) * g
    v = ADAM_B2 * v + (1.0 - ADAM_B2) * (g * g)
    m_hat = m / (1.0 - ADAM_B1 ** ADAM_STEP)
    v_hat = v / (1.0 - ADAM_B2 ** ADAM_STEP)
    delta = -ADAM_LR * (m_hat / (jnp.sqrt(v_hat) + ADAM_EPS) + ADAM_WD * w)
    return delta, m, v


def _adamw_matrix_call(partial, recv, chip, w, m, v, name):
    r, ccols = w.shape
    gcols = partial.shape[2]
    tr = _tile(r, 256)

    def body(chip_ref, p_ref, r_ref, w_ref, m_ref, v_ref, g_out, d_out, m_out, v_out):
        del chip_ref
        cols = pl.ds(0, ccols)
        g = (p_ref[:, cols] + r_ref[0, :, cols].astype(F32) + r_ref[1, :, cols].astype(F32)
             + r_ref[2, :, cols].astype(F32))
        delta, mn, vn = _adamw_math(w_ref[...], g, m_ref[...], v_ref[...])
        g_out[...] = g
        d_out[...] = delta
        m_out[...] = mn
        v_out[...] = vn

    mat = pl.BlockSpec((tr, ccols), lambda i, ch: (i, 0))
    shp = jax.ShapeDtypeStruct((r, ccols), F32)
    return pl.pallas_call(
        body, name=name,
        grid_spec=pltpu.PrefetchScalarGridSpec(
            num_scalar_prefetch=1, grid=(r // tr,),
            in_specs=[pl.BlockSpec((None, tr, gcols), lambda i, ch: (ch[0], i, 0)),
                      pl.BlockSpec((3, tr, gcols), lambda i, ch: (0, i, 0)), mat, mat, mat],
            out_specs=[mat, mat, mat, mat]),
        out_shape=[shp, shp, shp, shp],
        compiler_params=_params("parallel"),
    )(chip, partial, recv, w, m, v)


def _adamw_vector_call(gathered, w, m, v, name):
    n = w.shape[1]

    def body(p_ref, w_ref, m_ref, v_ref, g_out, d_out, m_out, v_out):
        g = p_ref[0:1, :]
        for k in range(1, N_DEV):
            g = g + p_ref[k:k + 1, :]
        delta, mn, vn = _adamw_math(w_ref[...], g, m_ref[...], v_ref[...])
        g_out[...] = g
        d_out[...] = delta
        m_out[...] = mn
        v_out[...] = vn

    shp = jax.ShapeDtypeStruct((1, n), F32)
    return pl.pallas_call(body, name=name, out_shape=[shp, shp, shp, shp])(gathered, w, m, v)


def _round_up(n, mult):
    return (n + mult - 1) // mult * mult


def kernel(x, ffn1_norm, ffn1_w_gate, ffn1_w_up, ffn1_w_down, mix_norm, w_in, ret_norm_g, hgrn_lb_logits, hgrn_norm_g, w_out, ffn2_norm, ffn2_w_gate, ffn2_w_up, ffn2_w_down, final_norm, loss_target, m_ffn1_norm, m_ffn1_w_gate, m_ffn1_w_up, m_ffn1_w_down, m_mix_norm, m_w_in, m_ret_norm_g, m_hgrn_lb_logits, m_hgrn_norm_g, m_w_out, m_ffn2_norm, m_ffn2_w_gate, m_ffn2_w_up, m_ffn2_w_down, m_final_norm, v_ffn1_norm, v_ffn1_w_gate, v_ffn1_w_up, v_ffn1_w_down, v_mix_norm, v_w_in, v_ret_norm_g, v_hgrn_lb_logits, v_hgrn_norm_g, v_w_out, v_ffn2_norm, v_ffn2_w_gate, v_ffn2_w_up, v_ffn2_w_down, v_final_norm):
    xs = x[0]
    target = loss_target[0]
    s, d = xs.shape
    f_loc = ffn1_w_gate.shape[2]
    fp = _round_up(f_loc, LANE)
    pad_cols = lambda t: jnp.pad(t[0], ((0, 0), (0, fp - f_loc)))
    pad_rows = lambda t: jnp.pad(t[0], ((0, fp - f_loc), (0, 0)))

    mat_names = ["ffn1_w_gate", "ffn1_w_up", "ffn1_w_down", "w_in", "w_out", "ffn2_w_gate", "ffn2_w_up", "ffn2_w_down"]
    mat_pad = [pad_cols, pad_cols, pad_rows, lambda t: t[0], lambda t: t[0], pad_cols, pad_cols, pad_rows]
    mat_w = [ffn1_w_gate, ffn1_w_up, ffn1_w_down, w_in, w_out, ffn2_w_gate, ffn2_w_up, ffn2_w_down]
    mat_m = [m_ffn1_w_gate, m_ffn1_w_up, m_ffn1_w_down, m_w_in, m_w_out, m_ffn2_w_gate, m_ffn2_w_up, m_ffn2_w_down]
    mat_v = [v_ffn1_w_gate, v_ffn1_w_up, v_ffn1_w_down, v_w_in, v_w_out, v_ffn2_w_gate, v_ffn2_w_up, v_ffn2_w_down]

    cx, cy, cc = _position()
    parity = jnp.reshape(cc, (1,)).astype(jnp.int32)
    chip = jnp.reshape(2 * cx + cy, (1,)).astype(jnp.int32)
    mat_index = {nm: i for i, nm in enumerate(mat_names)}
    mat_out = {}

    def pair_sums(names, grads, from_sibling):
        return [_pair_sum_call(g, r, parity, "pair_sum_" + nm) for nm, g, r in zip(names, grads, from_sibling)]

    def update(names, sums, from_chips):
        for nm, (p, _), r in zip(names, sums, from_chips):
            i = mat_index[nm]
            res = _adamw_matrix_call(p, r, chip, mat_w[i][0], mat_m[i][0], mat_v[i][0], "adamw_" + nm)
            mat_out[nm] = [t[None] for t in res]

    shards = [p(t.astype(BF)) for p, t in zip(mat_pad, mat_w)]
    wg1, wu1 = _all_gather_call(shards[:2], "gather_ffn1_up")

    h1 = _rmsnorm_call(xs, ffn1_norm, "ffn1_norm")
    (g1, u1, a1), (wd1, win) = _ffn_up_call(h1, wg1, wu1, "ffn1_up", comm=_gather_two_level(shards[2:4], 0.7))
    (x1, h2), _ = _down_call(a1, wd1, xs, mix_norm, FFN_RESIDUAL_WEIGHT, "ffn1_down")
    (proj,), _ = _proj_call(h2, win, "mix_in")
    wmix = proj.shape[1] // 8
    cos, sin = _rope_tables(s, wmix // RET_HEADS)
    consts = _ret_consts(wmix // RET_HEADS)
    (o_hg, m_hg, st_hg), landed = _hgrn_fwd_call(proj, hgrn_lb_logits, hgrn_norm_g, "hgrn_fwd",
                                                 comm=_gather_round1(shards[4:]))
    (o_ret, m_ret, st_ret), (wout, wg2, wu2, wd2) = _ret_fwd_call(proj, cos, sin, consts, ret_norm_g, "ret_fwd",
                                                                  comm=_gather_round2(landed))
    merged = jnp.concatenate([m_ret, m_hg], axis=1)
    wout_wide = wout.reshape(2, wout.shape[0] * wout.shape[1] // 2, d)
    (x2, h3), _ = _down_call(merged, wout_wide, x1, ffn2_norm, 1.0, "mix_out")
    (g2, u2, a2), _ = _ffn_up_call(h3, wg2, wu2, "ffn2_up")
    (x3,), _ = _down_call(a2, wd2, x2, None, FFN_RESIDUAL_WEIGHT, "ffn2_down")
    loss_part, dx3, dx3b, gv_final = _loss_call(x3, target, final_norm[None, :], "loss_head")

    (dg2, du2), _ = _bwd_up_call(dx3b, wd2, g2, u2, FFN_RESIDUAL_WEIGHT, "ffn2_bwd_up")
    (dx2, dx2b, gv_n3), _ = _bwd_down_call([(dg2, wg2), (du2, wu2)], dx3, x2, ffn2_norm, "ffn2_bwd_down")
    names_a = ["ffn2_w_gate", "ffn2_w_up", "ffn2_w_down"]
    grads_a = [_wgrad_call(h3, dg2, N_DEV, False, 1.0, "ffn2_wgrad_gate"),
               _wgrad_call(h3, du2, N_DEV, False, 1.0, "ffn2_wgrad_up"),
               _wgrad_call(a2, dx3b, N_DEV, True, FFN_RESIDUAL_WEIGHT, "ffn2_wgrad_down")]

    (dmerged,), sib_a = _proj_call(dx2b, jnp.swapaxes(wout_wide, 1, 2), "mix_out_bwd",
                                   comm=_sibling_exchange(grads_a))
    gm_out = _wgrad_call(merged, dx2b, 2, True, 1.0, "mix_out_wgrad").reshape(wout.shape)
    sums_a = pair_sums(names_a, grads_a, sib_a)
    dproj_half, gv_ret = _ret_bwd_call(proj, cos, sin, consts, ret_norm_g, o_ret, st_ret, dmerged, "ret_bwd")
    (dproj, gv_lb, gv_hg), landed = _hgrn_bwd_call(
        proj, hgrn_lb_logits, hgrn_norm_g, o_hg, st_hg, dmerged, dproj_half, "hgrn_bwd",
        comm=_join([_chip_exchange([pb for _, pb in sums_a]), _sibling_exchange([gm_out])]))
    update(names_a, sums_a, landed[:3])
    sums_out = pair_sums(["w_out"], [gm_out], landed[3:])
    (dx1, dx1b, gv_n2), chips_out = _bwd_down_call([(dproj, win)], dx2, x1, mix_norm, "mix_in_bwd", per_step=2,
                                                   comm=_chip_exchange([sums_out[0][1]]))
    update(["w_out"], sums_out, chips_out)

    gm_in = _wgrad_call(h2, dproj, N_DEV, False, 1.0, "mix_in_wgrad")
    gm_d1, sib_in = _wgrad_call(a1, dx1b, N_DEV, True, FFN_RESIDUAL_WEIGHT, "ffn1_wgrad_down",
                                comm=_sibling_exchange([gm_in]))
    sums_in = pair_sums(["w_in"], [gm_in], sib_in)
    (dg1, du1), landed = _bwd_up_call(dx1b, wd1, g1, u1, FFN_RESIDUAL_WEIGHT, "ffn1_bwd_up",
                                      comm=_join([_chip_exchange([sums_in[0][1]]), _sibling_exchange([gm_d1])]))
    update(["w_in"], sums_in, landed[:1])
    sums_d1 = pair_sums(["ffn1_w_down"], [gm_d1], landed[1:])
    gm_g1, chips_d1 = _wgrad_call(h1, dg1, N_DEV, False, 1.0, "ffn1_wgrad_gate",
                                  comm=_chip_exchange([sums_d1[0][1]]))
    update(["ffn1_w_down"], sums_d1, chips_d1)
    gm_u1, sib_g = _wgrad_call(h1, du1, N_DEV, False, 1.0, "ffn1_wgrad_up", comm=_sibling_exchange([gm_g1]))
    sums_g = pair_sums(["ffn1_w_gate"], [gm_g1], sib_g)
    n_row_tiles = s // _tile(s, 512)
    assert n_row_tiles >= 2, "the sequence must span at least two row tiles"
    n_first = max(1, (5 * n_row_tiles) // 8)
    first_part, landed = _bwd_down_call(
        [(dg1, wg1), (du1, wu1)], dx1, xs, ffn1_norm, "ffn1_bwd_down_a", tiles=(0, n_first),
        comm=_join([_chip_exchange([sums_g[0][1]]), _sibling_exchange([gm_u1])]))
    update(["ffn1_w_gate"], sums_g, landed[:1])
    sums_u = pair_sums(["ffn1_w_up"], [gm_u1], landed[1:])
    (dx0, _, gv_n1), chips_u = _bwd_down_call(
        [(dg1, wg1), (du1, wu1)], dx1, xs, ffn1_norm, "ffn1_bwd_down_b", tiles=(n_first, n_row_tiles - n_first),
        carry=first_part, comm=_chip_exchange([sums_u[0][1]]))
    update(["ffn1_w_up"], sums_u, chips_u)

    vec_names = ["ffn1_norm", "mix_norm", "ret_norm_g", "hgrn_lb_logits", "hgrn_norm_g", "ffn2_norm", "final_norm"]
    vec_g = [gv_n1, gv_n2, gv_ret, gv_lb, gv_hg, gv_n3, gv_final]
    vec_w = [ffn1_norm, mix_norm, ret_norm_g, hgrn_lb_logits, hgrn_norm_g, ffn2_norm, final_norm[None, :]]
    vec_m = [m_ffn1_norm, m_mix_norm, m_ret_norm_g, m_hgrn_lb_logits, m_hgrn_norm_g, m_ffn2_norm, m_final_norm[None, :]]
    vec_v = [v_ffn1_norm, v_mix_norm, v_ret_norm_g, v_hgrn_lb_logits, v_hgrn_norm_g, v_ffn2_norm, v_final_norm[None, :]]
    cat = lambda ts: jnp.concatenate(ts, axis=1)
    (vec_all,) = _all_gather_call([cat(vec_g)], "gather_vector_grads")
    vres = _adamw_vector_call(vec_all[:, 0, :], cat(vec_w), cat(vec_m), cat(vec_v), "adamw_vectors")
    vec_out = {}
    off = 0
    for nm, t in zip(vec_names, vec_w):
        n = t.shape[1]
        parts = [r[:, off:off + n] for r in vres]
        if nm == "final_norm":
            parts = [p[0] for p in parts]
        vec_out[nm] = parts
        off += n

    loss = lax.psum(loss_part[0, 0], ("x", "y", "c"))
    order = ["ffn1_norm", "ffn1_w_gate", "ffn1_w_up", "ffn1_w_down", "mix_norm", "w_in", "ret_norm_g", "hgrn_lb_logits",
             "hgrn_norm_g", "w_out", "ffn2_norm", "ffn2_w_gate", "ffn2_w_up", "ffn2_w_down", "final_norm"]
    res = {**mat_out, **vec_out}
    outs = [loss, dx0[None]]
    for kind in range(4):
        outs += [res[nm][kind] for nm in order]
    return tuple(outs)
```

```python
import functools

import jax
import jax.numpy as jnp
from jax import lax
from jax.experimental import pallas as pl
from jax.experimental.pallas import tpu as pltpu

BF = jnp.bfloat16
F32 = jnp.float32
MESH = pl.DeviceIdType.MESH
HBM_SPEC = pl.BlockSpec(memory_space=pltpu.HBM)

N_DEV = 8
LANE = 128
EPS = 1e-6
ROPE_BASE = 10000.0
RET_HEADS = 4
HGRN_HEADS = 8
RET_CHUNK = 128
HGRN_BLOCK = 16
FFN_RESIDUAL_WEIGHT = 0.5
ADAM_LR = 0.001
ADAM_B1 = 0.9
ADAM_B2 = 0.999
ADAM_EPS = 1e-08
ADAM_WD = 0.01
ADAM_STEP = 10
VMEM_LIMIT = 56 * 1024 * 1024


def _tile(n, pref, mult=8):
    t = min(pref, n)
    t -= t % mult
    while t >= mult:
        if n % t == 0:
            return t
        t -= mult
    return n


def _params(*sem):
    return pltpu.CompilerParams(dimension_semantics=sem, vmem_limit_bytes=VMEM_LIMIT)


class _Comm:
    def __init__(self, operands, out_shape, n_sems, start, finish, aliases=None, middle=None, middle_at=0.0):
        self.operands = list(operands)
        self.out_shape = list(out_shape)
        self.n_sems = n_sems
        self.start = start
        self.finish = finish
        self.aliases = dict(aliases or {})
        self.middle = middle
        self.middle_at = middle_at


def _launch(body, *, name, grid, in_specs, out_specs, out_shape, sem, args, scratch_shapes=(), aliases=None, comm=None):
    in_specs, out_specs, out_shape = list(in_specs), list(out_specs), list(out_shape)
    scratch_shapes = list(scratch_shapes)
    aliases = dict(aliases or {})
    if comm is None:
        res = pl.pallas_call(body, name=name, grid=grid, in_specs=in_specs, out_specs=out_specs, out_shape=out_shape,
                             scratch_shapes=scratch_shapes, input_output_aliases=aliases,
                             compiler_params=_params(*sem))(*args)
        return list(res), []
    n_in, n_out, n_scr = len(in_specs), len(out_specs), len(scratch_shapes)
    ci, co = len(comm.operands), len(comm.out_shape)

    def carrying(*refs):
        bounds = [0, n_in, n_in + ci, n_in + ci + n_out, n_in + ci + n_out + co, n_in + ci + n_out + co + n_scr]
        ins, cins, outs, couts, scr = [refs[a:b] for a, b in zip(bounds[:-1], bounds[1:])]
        send_sems, recv_sems = refs[bounds[-1]:]
        ids = [pl.program_id(k) for k in range(len(grid))]
        first = functools.reduce(jnp.logical_and, [i == 0 for i in ids])
        last = functools.reduce(jnp.logical_and, [i == g - 1 for i, g in zip(ids, grid)])

        @pl.when(first)
        def _():
            comm.start(cins, couts, send_sems, recv_sems)

        if comm.middle is not None:
            step, total = ids[0], grid[0]
            for i, g in zip(ids[1:], grid[1:]):
                step, total = step * g + i, total * g

            @pl.when(step == int(total * comm.middle_at))
            def _():
                comm.middle(cins, couts, send_sems, recv_sems)

        body(*ins, *outs, *scr)

        @pl.when(last)
        def _():
            comm.finish(cins, couts, send_sems, recv_sems)

    res = pl.pallas_call(
        carrying, name=name, grid=grid,
        in_specs=in_specs + [HBM_SPEC] * ci, out_specs=out_specs + [HBM_SPEC] * co,
        out_shape=out_shape + comm.out_shape,
        scratch_shapes=scratch_shapes + [pltpu.SemaphoreType.DMA((comm.n_sems,)), pltpu.SemaphoreType.DMA((comm.n_sems,))],
        input_output_aliases={**aliases, **{n_in + a: n_out + b for a, b in comm.aliases.items()}},
        compiler_params=_params(*(["arbitrary"] * len(grid))),
    )(*args, *comm.operands)
    return list(res[:n_out]), list(res[n_out:])


def _comm_only_call(comm, name):
    def body(*refs):
        ci, co = len(comm.operands), len(comm.out_shape)
        cins, couts = refs[:ci], refs[ci:ci + co]
        send_sems, recv_sems = refs[ci + co:]
        comm.start(cins, couts, send_sems, recv_sems)
        comm.finish(cins, couts, send_sems, recv_sems)

    return pl.pallas_call(
        body, name=name,
        in_specs=[HBM_SPEC] * len(comm.operands), out_specs=[HBM_SPEC] * len(comm.out_shape),
        out_shape=comm.out_shape,
        scratch_shapes=[pltpu.SemaphoreType.DMA((comm.n_sems,)), pltpu.SemaphoreType.DMA((comm.n_sems,))],
        input_output_aliases=comm.aliases,
    )(*comm.operands)


def _sigmoid(v):
    return 0.5 * jnp.tanh(0.5 * v) + 0.5


def _dot(a, b):
    return jnp.dot(a, b, preferred_element_type=F32)


def _dot_nt(a, b):
    return lax.dot_general(a, b, (((1,), (1,)), ((), ())), preferred_element_type=F32)


def _dot_tn(a, b):
    return lax.dot_general(a, b, (((0,), (0,)), ((), ())), preferred_element_type=F32)


def _rmsnorm_call(x, gain, name):
    s, d = x.shape
    tm = _tile(s, 512)

    def body(x_ref, g_ref, o_ref):
        xv = x_ref[...]
        r = lax.rsqrt(jnp.mean(xv * xv, axis=-1, keepdims=True) + EPS)
        o_ref[...] = (xv * r * g_ref[...]).astype(BF)

    return pl.pallas_call(
        body, name=name, grid=(s // tm,),
        in_specs=[pl.BlockSpec((tm, d), lambda i: (i, 0)), pl.BlockSpec((1, d), lambda i: (0, 0))],
        out_specs=pl.BlockSpec((tm, d), lambda i: (i, 0)),
        out_shape=jax.ShapeDtypeStruct((s, d), BF),
        compiler_params=_params("parallel"),
    )(x, gain)


def _ffn_up_call(h, wg, wu, name, comm=None):
    s, d = h.shape
    nj, _, k = wg.shape
    tm = _tile(s, 1024)

    def body(h_ref, wg_ref, wu_ref, g_ref, u_ref, a_ref):
        hv = h_ref[...]
        g = _dot(hv, wg_ref[...])
        u = _dot(hv, wu_ref[...])
        g_ref[...] = g.astype(BF)
        u_ref[...] = u.astype(BF)
        a_ref[...] = (g * _sigmoid(g) * u).astype(BF)

    act = pl.BlockSpec((tm, k), lambda i, j: (i, j))
    wsp = pl.BlockSpec((None, d, k), lambda i, j: (j, 0, 0))
    return _launch(
        body, name=name, grid=(s // tm, nj),
        in_specs=[pl.BlockSpec((tm, d), lambda i, j: (i, 0)), wsp, wsp],
        out_specs=[act, act, act],
        out_shape=[jax.ShapeDtypeStruct((s, nj * k), BF)] * 3,
        sem=("parallel", "arbitrary"), args=(h, wg, wu), comm=comm)


def _proj_call(h, w, name, comm=None):
    s, d = h.shape
    nj, _, k = w.shape
    tm = _tile(s, 1024)

    def body(h_ref, w_ref, o_ref):
        o_ref[...] = _dot(h_ref[...], w_ref[...])

    return _launch(
        body, name=name, grid=(s // tm, nj),
        in_specs=[pl.BlockSpec((tm, d), lambda i, j: (i, 0)), pl.BlockSpec((None, d, k), lambda i, j: (j, 0, 0))],
        out_specs=[pl.BlockSpec((tm, k), lambda i, j: (i, j))],
        out_shape=[jax.ShapeDtypeStruct((s, nj * k), F32)],
        sem=("parallel", "arbitrary"), args=(h, w), comm=comm)


def _down_call(a, w, resid, gain, scale, name, comm=None):
    s = a.shape[0]
    nj, k, d = w.shape
    tm = _tile(s, 1024)
    strip = _tile(tm, 128)
    cn = _tile(d, 512, LANE)
    with_norm = gain is not None

    def body(*refs):
        if with_norm:
            a_ref, w_ref, r_ref, g_ref, x_ref, h_ref = refs
        else:
            a_ref, w_ref, r_ref, x_ref = refs
        j = pl.program_id(1)

        @pl.when(j == 0)
        def _():
            x_ref[...] = jnp.zeros_like(x_ref)

        av = a_ref[...]
        for n0 in range(0, d, cn):
            x_ref[:, n0:n0 + cn] += _dot(av, w_ref[:, n0:n0 + cn])

        @pl.when(j == nj - 1)
        def _():
            for r0 in range(0, tm, strip):
                rows = slice(r0, r0 + strip)
                xn = r_ref[rows, :] + (scale * x_ref[rows, :])
                x_ref[rows, :] = xn
                if with_norm:
                    r = lax.rsqrt(jnp.mean(xn * xn, axis=-1, keepdims=True) + EPS)
                    h_ref[rows, :] = (xn * r * g_ref[...]).astype(BF)

    row = pl.BlockSpec((tm, d), lambda i, j: (i, 0))
    once = pl.BlockSpec((tm, d), lambda i, j: (i, 0), pipeline_mode=pl.Buffered(1))
    in_specs = [pl.BlockSpec((tm, k), lambda i, j: (i, j)), pl.BlockSpec((None, k, d), lambda i, j: (j, 0, 0)), once]
    args = [a, w, resid]
    out_specs = [row]
    out_shape = [jax.ShapeDtypeStruct((s, d), F32)]
    if with_norm:
        in_specs.append(pl.BlockSpec((1, d), lambda i, j: (0, 0)))
        args.append(gain)
        out_specs.append(row)
        out_shape.append(jax.ShapeDtypeStruct((s, d), BF))
    return _launch(
        body, name=name, grid=(s // tm, nj),
        in_specs=in_specs, out_specs=out_specs, out_shape=out_shape,
        sem=("parallel", "arbitrary"), args=args, comm=comm)


def _loss_call(x, target, gain, name):
    s, d = x.shape
    tm = _tile(s, 512)

    def body(x_ref, t_ref, g_ref, loss_ref, dx_ref, dxb_ref, dg_ref):
        i = pl.program_id(0)

        @pl.when(i == 0)
        def _():
            loss_ref[...] = jnp.zeros_like(loss_ref)
            dg_ref[...] = jnp.zeros_like(dg_ref)

        xv = x_ref[...]
        gv = g_ref[...]
        r = lax.rsqrt(jnp.mean(xv * xv, axis=-1, keepdims=True) + EPS)
        xhat = xv * r
        err = xhat * gv - t_ref[...]
        per_tok = jnp.mean(err * err, axis=-1, keepdims=True)
        loss_ref[...] += 0.5 * jnp.sum(per_tok, axis=0, keepdims=True)
        dout = err * (1.0 / d)
        dg_ref[...] += jnp.sum(dout * xhat, axis=0, keepdims=True)
        dxhat = dout * gv
        dx = r * (dxhat - xhat * jnp.mean(dxhat * xhat, axis=-1, keepdims=True))
        dx_ref[...] = dx
        dxb_ref[...] = dx.astype(BF)

    row = pl.BlockSpec((tm, d), lambda i: (i, 0))
    vec = pl.BlockSpec((1, d), lambda i: (0, 0))
    return pl.pallas_call(
        body, name=name, grid=(s // tm,),
        in_specs=[row, row, vec],
        out_specs=[pl.BlockSpec((1, 1), lambda i: (0, 0)), row, row, vec],
        out_shape=[jax.ShapeDtypeStruct((1, 1), F32), jax.ShapeDtypeStruct((s, d), F32),
                   jax.ShapeDtypeStruct((s, d), BF), jax.ShapeDtypeStruct((1, d), F32)],
        compiler_params=_params("arbitrary"),
    )(x, target, gain)


def _bwd_up_call(dy, wd, g, u, scale, name, comm=None):
    s, d = dy.shape
    nj, k, _ = wd.shape
    tm = _tile(s, 1024)

    def body(dy_ref, w_ref, g_ref, u_ref, dg_ref, du_ref):
        da = scale * _dot_nt(dy_ref[...], w_ref[...])
        gv = g_ref[...].astype(F32)
        sig = _sigmoid(gv)
        du_ref[...] = (da * gv * sig).astype(BF)
        dg_ref[...] = (da * u_ref[...].astype(F32) * sig * (1.0 + gv * (1.0 - sig))).astype(BF)

    act = pl.BlockSpec((tm, k), lambda i, j: (i, j))
    return _launch(
        body, name=name, grid=(s // tm, nj),
        in_specs=[pl.BlockSpec((tm, d), lambda i, j: (i, 0)), pl.BlockSpec((None, k, d), lambda i, j: (j, 0, 0)), act, act],
        out_specs=[act, act],
        out_shape=[jax.ShapeDtypeStruct((s, nj * k), BF), jax.ShapeDtypeStruct((s, nj * k), BF)],
        sem=("parallel", "arbitrary"), args=(dy, wd, g, u), comm=comm)


def _norm_bwd_call(dh, dres, xin, gain, name):
    s, d = xin.shape
    tm = _tile(s, 256)

    def body(dh_ref, dres_ref, x_ref, g_ref, dx_ref, dxb_ref, dg_ref):
        @pl.when(pl.program_id(0) == 0)
        def _():
            dg_ref[...] = jnp.zeros_like(dg_ref)

        xv = x_ref[...]
        r = lax.rsqrt(jnp.mean(xv * xv, axis=-1, keepdims=True) + EPS)
        xhat = xv * r
        dh_v = dh_ref[...]
        dg_ref[...] += jnp.sum(dh_v * xhat, axis=0, keepdims=True)
        dxhat = dh_v * g_ref[...]
        dx = dres_ref[...] + r * (dxhat - xhat * jnp.mean(dxhat * xhat, axis=-1, keepdims=True))
        dx_ref[...] = dx
        dxb_ref[...] = dx.astype(BF)

    row = pl.BlockSpec((tm, d), lambda i: (i, 0))
    vec = pl.BlockSpec((1, d), lambda i: (0, 0))
    return pl.pallas_call(
        body, name=name, grid=(s // tm,),
        in_specs=[row, row, row, vec], out_specs=[row, row, vec],
        out_shape=[jax.ShapeDtypeStruct((s, d), F32), jax.ShapeDtypeStruct((s, d), BF), jax.ShapeDtypeStruct((1, d), F32)],
        compiler_params=_params("arbitrary"),
    )(dh, dres, xin, gain)


def _bwd_mm_call(pairs, name, per_step=1, tiles=None, carry=None, comm=None):
    a0 = pairs[0][0]
    s = a0.shape[-2]
    nblocks, d, k = pairs[0][1].shape
    nj = nblocks // per_step
    npair = len(pairs) * per_step
    tm = _tile(s, 1024)
    cn = _tile(d, 512, LANE)

    first_tile, n_tiles = (0, s // tm) if tiles is None else tiles

    def body(*refs):
        a_refs = refs[0:2 * npair:2]
        w_refs = refs[1:2 * npair:2]
        dh_ref = refs[-1]

        @pl.when(pl.program_id(1) == 0)
        def _():
            dh_ref[...] = jnp.zeros_like(dh_ref)

        for a_ref, w_ref in zip(a_refs, w_refs):
            av = a_ref[...]
            for n0 in range(0, d, cn):
                dh_ref[:, n0:n0 + cn] += _dot_nt(av, w_ref[n0:n0 + cn, :])

    row = pl.BlockSpec((tm, d), lambda i, j: (first_tile + i, 0))
    in_specs, args = [], []
    for a, w in pairs:
        for r in range(per_step):
            in_specs += [pl.BlockSpec((None, tm, k), lambda i, j, r=r: (j * per_step + r, first_tile + i, 0))
                         if a.ndim == 3 else
                         pl.BlockSpec((tm, k), lambda i, j, r=r: (first_tile + i, j * per_step + r)),
                         pl.BlockSpec((None, d, k), lambda i, j, r=r: (j * per_step + r, 0, 0))]
            args += [a, w]
    aliases = {}
    if carry is not None:
        aliases = {len(in_specs): 0}
        in_specs.append(pl.BlockSpec(memory_space=pl.ANY))
        args.append(carry)
    (dh,), landed = _launch(
        body, name=name, grid=(n_tiles, nj),
        in_specs=in_specs, out_specs=[row], out_shape=[jax.ShapeDtypeStruct((s, d), F32)],
        sem=("parallel", "arbitrary"), args=args, aliases=aliases, comm=comm)
    return dh, landed


def _wgrad_call(a, b, nj, a_blocked, scale, name, comm=None):
    s = a.shape[0]
    ka = a.shape[1] // nj if a_blocked else a.shape[1]
    b_stacked = b.ndim == 3
    kb = b.shape[-1] if (a_blocked or b_stacked) else b.shape[1] // nj
    ts = _tile(s, 2048)
    ns = s // ts

    def body(a_ref, b_ref, o_ref):
        t = pl.program_id(1)

        @pl.when(t == 0)
        def _():
            o_ref[...] = jnp.zeros_like(o_ref)

        o_ref[...] += _dot_tn(a_ref[...], b_ref[...])
        if scale != 1.0:
            @pl.when(t == ns - 1)
            def _():
                o_ref[...] = o_ref[...] * scale

    a_spec = pl.BlockSpec((ts, ka), (lambda j, t: (t, j)) if a_blocked else (lambda j, t: (t, 0)))
    if b_stacked:
        b_spec = pl.BlockSpec((None, ts, kb), lambda j, t: (j, t, 0))
    else:
        b_spec = pl.BlockSpec((ts, kb), (lambda j, t: (t, 0)) if a_blocked else (lambda j, t: (t, j)))
    (out,), landed = _launch(
        body, name=name, grid=(nj, ns),
        in_specs=[a_spec, b_spec],
        out_specs=[pl.BlockSpec((None, ka, kb), lambda j, t: (j, 0, 0))],
        out_shape=[jax.ShapeDtypeStruct((nj, ka, kb), F32)],
        sem=("parallel", "arbitrary"), args=(a, b), comm=comm)
    return out if comm is None else (out, landed)


def _rope(v, cos, sin):
    half = v.shape[-1] // 2
    v1, v2 = v[:, :half], v[:, half:]
    return jnp.concatenate([v1 * cos - v2 * sin, v2 * cos + v1 * sin], axis=-1)


def _rope_bwd(dv, cos, sin):
    half = dv.shape[-1] // 2
    d1, d2 = dv[:, :half], dv[:, half:]
    return jnp.concatenate([d1 * cos + d2 * sin, d2 * cos - d1 * sin], axis=-1)


def _ret_consts(hd):
    c = RET_CHUNK
    log_gamma = jnp.log(1.0 - jnp.exp2(-5.0 - jnp.arange(RET_HEADS, dtype=F32)))
    idx = jnp.arange(c, dtype=F32)
    rel = idx[:, None] - idx[None, :]
    mask = rel >= 0
    decay = jnp.where(mask[None], jnp.exp(log_gamma[:, None, None] * jnp.where(mask, rel, 0.0)[None]), 0.0)
    qdec = jnp.exp(log_gamma[:, None] * (idx + 1.0)[None, :])
    kdec = jnp.exp(log_gamma[:, None] * (c - 1.0 - idx)[None, :])
    gchunk = jnp.exp(log_gamma * c)
    bc = lambda t: jnp.broadcast_to(t[:, :, None], (RET_HEADS, t.shape[1], hd))
    return decay, bc(qdec), bc(kdec), bc(gchunk[:, None])


def _rope_tables(s, hd):
    inv = jnp.power(ROPE_BASE, -jnp.arange(0, hd, 2, dtype=F32) / hd)
    ang = jnp.arange(s, dtype=F32)[:, None] * inv[None, :]
    return jnp.cos(ang), jnp.sin(ang)


def _ret_fwd_call(proj, cos, sin, consts, gret, name, comm=None):
    s = proj.shape[0]
    w = proj.shape[1] // 8
    hd = w // RET_HEADS
    c = RET_CHUNK
    tt = _tile(s, 512, c)
    nc = tt // c
    decay, qdec, kdec, gch = consts
    scale = hd ** -0.5

    def body(q_ref, k_ref, v_ref, gate_ref, cos_ref, sin_ref, dec_ref, qd_ref, kd_ref, gc_ref, gn_ref,
             o_ref, m_ref, st_ref, state):
        @pl.when(pl.program_id(1) == 0)
        def _():
            state[...] = jnp.zeros_like(state)

        dec = dec_ref[...]
        for ci in range(nc):
            rows = slice(ci * c, (ci + 1) * c)
            cs, sn = cos_ref[rows, :], sin_ref[rows, :]
            q = _rope(q_ref[rows, :], cs, sn) * scale
            k = _rope(k_ref[rows, :], cs, sn)
            vb = v_ref[rows, :].astype(BF)
            sc = _dot_nt(q.astype(BF), k.astype(BF)) * dec
            prev = state[...]
            st_ref[ci] = prev
            o = _dot(sc.astype(BF), vb) + _dot((q * qd_ref[...]).astype(BF), prev.astype(BF))
            state[...] = gc_ref[...] * prev + _dot_tn((k * kd_ref[...]).astype(BF), vb)
            o_ref[rows, :] = o
            mu = jnp.mean(o, axis=-1, keepdims=True)
            cen = o - mu
            xhat = cen * lax.rsqrt(jnp.mean(cen * cen, axis=-1, keepdims=True) + EPS)
            gt = gate_ref[rows, :]
            m_ref[rows, :] = (xhat * gn_ref[...] * (gt * _sigmoid(gt))).astype(BF)

    nh = RET_HEADS
    comp = lambda j: pl.BlockSpec((tt, hd), lambda h, t, j=j: (t, j * nh + h))
    tab = pl.BlockSpec((tt, hd // 2), lambda h, t: (t, 0))
    per_head = lambda r: pl.BlockSpec((None, r, hd), lambda h, t: (h, 0, 0))
    return _launch(
        body, name=name, grid=(nh, s // tt),
        in_specs=[comp(0), comp(1), comp(2), comp(3), tab, tab,
                  pl.BlockSpec((None, c, c), lambda h, t: (h, 0, 0)), per_head(c), per_head(c), per_head(1),
                  pl.BlockSpec((1, hd), lambda h, t: (0, h))],
        out_specs=[pl.BlockSpec((tt, hd), lambda h, t: (t, h)), pl.BlockSpec((tt, hd), lambda h, t: (t, h)),
                   pl.BlockSpec((None, nc, hd, hd), lambda h, t: (h, t, 0, 0))],
        out_shape=[jax.ShapeDtypeStruct((s, w), F32), jax.ShapeDtypeStruct((s, w), BF),
                   jax.ShapeDtypeStruct((nh, s // c, hd, hd), F32)],
        scratch_shapes=[pltpu.VMEM((hd, hd), F32)],
        sem=("parallel", "arbitrary"),
        args=(proj, proj, proj, proj, cos, sin, decay, qdec, kdec, gch, gret), comm=comm)


def _ret_bwd_call(proj, cos, sin, consts, gret, o_raw, states, dmerged, name):
    s = proj.shape[0]
    w = proj.shape[1] // 8
    hd = w // RET_HEADS
    c = RET_CHUNK
    tt = _tile(s, 512, c)
    nc = tt // c
    nt = s // tt
    decay, qdec, kdec, gch = consts
    scale = hd ** -0.5

    def body(q_ref, k_ref, v_ref, gate_ref, cos_ref, sin_ref, dec_ref, qd_ref, kd_ref, gc_ref, gn_ref,
             o_ref, st_ref, dm_ref, dp_ref, dgn_ref, dstate):
        @pl.when(pl.program_id(1) == 0)
        def _():
            dstate[...] = jnp.zeros_like(dstate)
            dgn_ref[...] = jnp.zeros_like(dgn_ref)

        dec = dec_ref[...]
        gn = gn_ref[...]
        for ci in reversed(range(nc)):
            rows = slice(ci * c, (ci + 1) * c)
            cs, sn = cos_ref[rows, :], sin_ref[rows, :]
            q = _rope(q_ref[rows, :], cs, sn) * scale
            k = _rope(k_ref[rows, :], cs, sn)
            qb, kb = q.astype(BF), k.astype(BF)
            vb = v_ref[rows, :].astype(BF)
            sc = _dot_nt(qb, kb) * dec
            o = o_ref[rows, :]
            mu = jnp.mean(o, axis=-1, keepdims=True)
            cen = o - mu
            rstd = lax.rsqrt(jnp.mean(cen * cen, axis=-1, keepdims=True) + EPS)
            xhat = cen * rstd
            gt = gate_ref[rows, :]
            sig = _sigmoid(gt)
            sg = gt * sig
            dm = dm_ref[rows, :]
            dgn_ref[...] += jnp.sum(dm * xhat * sg, axis=0, keepdims=True)
            dp_ref[3, rows, :] = (dm * xhat * gn * sig * (1.0 + gt * (1.0 - sig))).astype(BF)
            dxhat = dm * gn * sg
            do = rstd * (dxhat - jnp.mean(dxhat, axis=-1, keepdims=True)
                         - xhat * jnp.mean(dxhat * xhat, axis=-1, keepdims=True))
            dob = do.astype(BF)
            prev = st_ref[ci]
            ds = dstate[...]
            dsb = ds.astype(BF)
            dsc = (_dot_nt(dob, vb) * dec).astype(BF)
            dq = _dot(dsc, kb) + _dot_nt(dob, prev.astype(BF)) * qd_ref[...]
            dk = _dot_tn(dsc, qb) + _dot_nt(vb, dsb) * kd_ref[...]
            dv = _dot_tn(sc.astype(BF), dob) + _dot((k * kd_ref[...]).astype(BF), dsb)
            dstate[...] = gc_ref[...] * ds + _dot_tn((q * qd_ref[...]).astype(BF), dob)
            dp_ref[0, rows, :] = _rope_bwd(dq * scale, cs, sn).astype(BF)
            dp_ref[1, rows, :] = _rope_bwd(dk, cs, sn).astype(BF)
            dp_ref[2, rows, :] = dv.astype(BF)

    nh = RET_HEADS
    rev = lambda t: nt - 1 - t
    comp = lambda j: pl.BlockSpec((tt, hd), lambda h, t, j=j: (rev(t), j * nh + h))
    tab = pl.BlockSpec((tt, hd // 2), lambda h, t: (rev(t), 0))
    per_head = lambda r: pl.BlockSpec((None, r, hd), lambda h, t: (h, 0, 0))
    head_cols = pl.BlockSpec((tt, hd), lambda h, t: (rev(t), h))
    gvec = pl.BlockSpec((1, hd), lambda h, t: (0, h))
    act = jax.ShapeDtypeStruct((s, w), BF)
    return pl.pallas_call(
        body, name=name, grid=(nh, nt),
        in_specs=[comp(0), comp(1), comp(2), comp(3), tab, tab,
                  pl.BlockSpec((None, c, c), lambda h, t: (h, 0, 0)), per_head(c), per_head(c), per_head(1), gvec,
                  head_cols, pl.BlockSpec((None, nc, hd, hd), lambda h, t: (h, rev(t), 0, 0)), head_cols],
        out_specs=[pl.BlockSpec((4, tt, hd), lambda h, t: (0, rev(t), h)), gvec],
        out_shape=[jax.ShapeDtypeStruct((8, s, w), BF), jax.ShapeDtypeStruct((1, w), F32)],
        scratch_shapes=[pltpu.VMEM((hd, hd), F32)],
        compiler_params=_params("parallel", "arbitrary"),
    )(proj, proj, proj, proj, cos, sin, decay, qdec, kdec, gch, gret, o_raw, states, dmerged)


def _block_tri(n, bs, upper):
    r = jnp.arange(n)[:, None]
    cidx = jnp.arange(n)[None, :]
    same = (r // bs) == (cidx // bs)
    return jnp.where(same & ((cidx >= r) if upper else (cidx <= r)), 1.0, 0.0).astype(F32)


def _dot_exact(a, b):
    return jnp.dot(a, b, preferred_element_type=F32, precision=lax.Precision.HIGHEST)


def _hgrn_gates(z, lbv):
    sz = _sigmoid(z)
    oml = 1.0 - lbv
    f = lbv + oml * sz
    key = oml * (1.0 - sz)
    return sz, f, key


def _hgrn_fwd_call(proj, lb_logits, ghg, name, comm=None):
    s = proj.shape[0]
    w = proj.shape[1] // 8
    nh = HGRN_HEADS
    hd = w // nh
    bs = HGRN_BLOCK
    tt = _tile(s, 256, bs)
    nb = tt // bs

    def body(q_ref, z_ref, v_ref, gate_ref, lb_ref, gn_ref, tril_ref, o_ref, m_ref, st_ref, state, upd):
        @pl.when(pl.program_id(1) == 0)
        def _():
            state[...] = jnp.zeros_like(state)

        lbv = _sigmoid(lb_ref[...])
        _, f, key = _hgrn_gates(z_ref[...], lbv)
        qr = q_ref[...]
        q = qr * _sigmoid(qr)
        v = v_ref[...]
        g = _dot_exact(tril_ref[...], jnp.log(f))
        blocks = lambda t: t.reshape(nb, bs, hd)
        g3, q3, k3, v3 = blocks(g), blocks(q), blocks(key), blocks(v)
        glast3 = g3[:, bs - 1:bs, :]
        row_id = lax.broadcasted_iota(jnp.int32, (nb, bs, hd), 1)
        o3 = jnp.zeros((nb, bs, hd), F32)
        for j in range(bs):
            wj = jnp.where(row_id >= j, jnp.exp(jnp.minimum(g3 - g3[:, j:j + 1, :], 0.0)), 0.0)
            a = jnp.sum(q3 * k3[:, j:j + 1, :] * wj, axis=-1, keepdims=True)
            o3 = o3 + a * v3[:, j:j + 1, :]
        ktb = (k3 * jnp.exp(glast3 - g3)).reshape(tt, hd).astype(BF)
        vb = v.astype(BF)
        for b in range(nb):
            rows = slice(b * bs, (b + 1) * bs)
            upd[b] = _dot_tn(vb[rows, :], ktb[rows, :])
        egl3 = jnp.exp(glast3)
        st = state[...]
        for b in range(nb):
            st_ref[b] = st
            st = st * egl3[b] + upd[b]
        state[...] = st
        qgb = (q * jnp.exp(g)).astype(BF)
        o_intra = o3.reshape(tt, hd)
        gn = gn_ref[...]
        for b in range(nb):
            rows = slice(b * bs, (b + 1) * bs)
            o = o_intra[rows, :] + _dot_nt(qgb[rows, :], st_ref[b].astype(BF))
            o_ref[rows, :] = o
            gt = gate_ref[rows, :]
            xhat = o * lax.rsqrt(jnp.mean(o * o, axis=-1, keepdims=True) + EPS)
            m_ref[rows, :] = (xhat * gn * (gt * _sigmoid(gt))).astype(BF)

    comp = lambda j: pl.BlockSpec((tt, hd), lambda h, t, j=j: (t, j * nh + h))
    gvec = pl.BlockSpec((1, hd), lambda h, t: (0, h))
    head_cols = pl.BlockSpec((tt, hd), lambda h, t: (t, h))
    return _launch(
        body, name=name, grid=(nh, s // tt),
        in_specs=[comp(4), comp(5), comp(6), comp(7), gvec, gvec, pl.BlockSpec((tt, tt), lambda h, t: (0, 0))],
        out_specs=[head_cols, head_cols, pl.BlockSpec((None, nb, hd, hd), lambda h, t: (h, t, 0, 0))],
        out_shape=[jax.ShapeDtypeStruct((s, w), F32), jax.ShapeDtypeStruct((s, w), BF),
                   jax.ShapeDtypeStruct((nh, s // bs, hd, hd), F32)],
        scratch_shapes=[pltpu.VMEM((hd, hd), F32), pltpu.VMEM((nb, hd, hd), F32)],
        sem=("parallel", "arbitrary"),
        args=(proj, proj, proj, proj, lb_logits, ghg, _block_tri(tt, bs, upper=False)), comm=comm)


def _hgrn_bwd_call(proj, lb_logits, ghg, o_raw, states, dmerged, stack, name, comm=None):
    s = proj.shape[0]
    w = proj.shape[1] // 8
    nh = HGRN_HEADS
    hd = w // nh
    bs = HGRN_BLOCK
    tt = _tile(s, 256, bs)
    nb = tt // bs
    nt = s // tt

    def body(q_ref, z_ref, v_ref, gate_ref, lb_ref, gn_ref, tril_ref, triu_ref, o_ref, st_ref, dm_ref, stack_ref,
             dp_ref, dlb_ref, dgn_ref,
             dstate, ds_all, inc, dq_s, dk_s, dv_s, dgl_s, dk_rows, dv_rows):
        @pl.when(pl.program_id(1) == 0)
        def _():
            dstate[...] = jnp.zeros_like(dstate)
            dlb_ref[...] = jnp.zeros_like(dlb_ref)
            dgn_ref[...] = jnp.zeros_like(dgn_ref)

        lbv = _sigmoid(lb_ref[...])
        oml = 1.0 - lbv
        gn = gn_ref[...]
        sz, f, key = _hgrn_gates(z_ref[...], lbv)
        qr = q_ref[...]
        sq = _sigmoid(qr)
        q = qr * sq
        v = v_ref[...]
        g = _dot_exact(tril_ref[...], jnp.log(f))
        eg = jnp.exp(g)
        blocks = lambda t: t.reshape(nb, bs, hd)
        g3, q3, k3, v3 = blocks(g), blocks(q), blocks(key), blocks(v)
        glast3 = g3[:, bs - 1:bs, :]
        egl3 = jnp.exp(glast3)
        ktail3 = jnp.exp(glast3 - g3)
        o = o_ref[...]
        rstd = lax.rsqrt(jnp.mean(o * o, axis=-1, keepdims=True) + EPS)
        xhat = o * rstd
        gt = gate_ref[...]
        sig = _sigmoid(gt)
        sg = gt * sig
        dm = dm_ref[...]
        dgn_ref[...] += jnp.sum(dm * xhat * sg, axis=0, keepdims=True)
        del stack_ref
        dp_ref[3] = (dm * xhat * gn * sig * (1.0 + gt * (1.0 - sig))).astype(BF)
        dxhat = dm * gn * sg
        do = rstd * (dxhat - xhat * jnp.mean(dxhat * xhat, axis=-1, keepdims=True))
        dob = do.astype(BF)
        do3 = blocks(do)
        qgb = (q * eg).astype(BF)
        for b in range(nb):
            rows = slice(b * bs, (b + 1) * bs)
            inc[b] = _dot_tn(dob[rows, :], qgb[rows, :])
        ds = dstate[...]
        for b in reversed(range(nb)):
            ds_all[b] = ds
            ds = ds * egl3[b] + inc[b]
        dstate[...] = ds
        ktb = (k3 * ktail3).reshape(tt, hd).astype(BF)
        vb = v.astype(BF)
        for b in range(nb):
            rows = slice(b * bs, (b + 1) * bs)
            prev = st_ref[b]
            dsb = ds_all[b]
            dsbb = dsb.astype(BF)
            dq_s[rows, :] = _dot(dob[rows, :], prev.astype(BF))
            dk_s[rows, :] = _dot(vb[rows, :], dsbb)
            dv_s[rows, :] = _dot_nt(ktb[rows, :], dsbb)
            dgl_s[b] = jnp.sum(prev * dsb, axis=0, keepdims=True)
        dq3 = blocks(dq_s[...] * eg)
        dk3 = blocks(dk_s[...]) * ktail3
        dg_last3 = jnp.sum(k3 * dk3, axis=1, keepdims=True) + egl3 * dgl_s[...]
        row_id = lax.broadcasted_iota(jnp.int32, (nb, bs, hd), 1)
        for j in range(bs):
            wj = jnp.where(row_id >= j, jnp.exp(jnp.minimum(g3 - g3[:, j:j + 1, :], 0.0)), 0.0)
            kj = k3[:, j:j + 1, :]
            a = jnp.sum(q3 * kj * wj, axis=-1, keepdims=True)
            da = jnp.sum(do3 * v3[:, j:j + 1, :], axis=-1, keepdims=True)
            dv_rows[:, j:j + 1, :] = jnp.sum(a * do3, axis=1, keepdims=True)
            dq3 = dq3 + da * kj * wj
            dk_rows[:, j:j + 1, :] = jnp.sum(da * q3 * wj, axis=1, keepdims=True)
        dk3 = dk3 + dk_rows[...]
        dv = dv_s[...] + dv_rows[...].reshape(tt, hd)
        dg3 = q3 * dq3 - k3 * dk3 + jnp.where(row_id == bs - 1, dg_last3, 0.0)
        dlf = _dot_exact(triu_ref[...], dg3.reshape(tt, hd))
        dk = dk3.reshape(tt, hd)
        dfk = dlf / f - dk
        dlb_ref[...] += jnp.sum(dfk * (1.0 - sz), axis=0, keepdims=True) * (lbv * oml)
        dp_ref[1] = (dfk * oml * sz * (1.0 - sz)).astype(BF)
        dp_ref[0] = (dq3.reshape(tt, hd) * sq * (1.0 + qr * (1.0 - sq))).astype(BF)
        dp_ref[2] = dv.astype(BF)

    rev = lambda t: nt - 1 - t
    comp = lambda j: pl.BlockSpec((tt, hd), lambda h, t, j=j: (rev(t), j * nh + h))
    gvec = pl.BlockSpec((1, hd), lambda h, t: (0, h))
    head_cols = pl.BlockSpec((tt, hd), lambda h, t: (rev(t), h))
    tri = pl.BlockSpec((tt, tt), lambda h, t: (0, 0))
    act = jax.ShapeDtypeStruct((s, w), BF)
    vec = jax.ShapeDtypeStruct((1, w), F32)
    tile_f32 = pltpu.VMEM((tt, hd), F32)
    return _launch(
        body, name=name, grid=(nh, nt),
        in_specs=[comp(4), comp(5), comp(6), comp(7), gvec, gvec, tri, tri, head_cols,
                  pl.BlockSpec((None, nb, hd, hd), lambda h, t: (h, rev(t), 0, 0)),
                  pl.BlockSpec((tt, hd), lambda h, t: (rev(t), nh + h)),
                  pl.BlockSpec(memory_space=pl.ANY)],
        out_specs=[pl.BlockSpec((4, tt, hd), lambda h, t: (1, rev(t), h)), gvec, gvec],
        out_shape=[jax.ShapeDtypeStruct(stack.shape, stack.dtype), vec, vec],
        scratch_shapes=[pltpu.VMEM((hd, hd), F32), pltpu.VMEM((nb, hd, hd), F32), pltpu.VMEM((nb, hd, hd), F32),
                        tile_f32, tile_f32, tile_f32, pltpu.VMEM((nb, 1, hd), F32),
                        pltpu.VMEM((nb, bs, hd), F32), pltpu.VMEM((nb, bs, hd), F32)],
        sem=("parallel", "arbitrary"),
        args=(proj, proj, proj, proj, lb_logits, ghg, _block_tri(tt, bs, upper=False), _block_tri(tt, bs, upper=True),
              o_raw, states, dmerged, stack), aliases={11: 0}, comm=comm)


def _position():
    return lax.axis_index("x"), lax.axis_index("y"), lax.axis_index("c")


def _all_gather_call(shards, name):
    n = len(shards)

    def body(*refs):
        ins, outs = refs[:n], refs[n:2 * n]
        send_sems, recv_sems, local_sems = refs[2 * n:]
        x, y, c = _position()
        me, sibling = (x, y, c), (x, y, 1 - c)
        chips = [(1 - x, y), (x, 1 - y), (1 - x, 1 - y)]

        def slot(a, p):
            return outs[a].at[4 * p[0] + 2 * p[1] + p[2]]

        def copy(a, k, block, to, src=None):
            return pltpu.make_async_remote_copy(
                src_ref=slot(a, block) if src is None else src, dst_ref=slot(a, block),
                send_sem=send_sems.at[a * 7 + k], recv_sem=recv_sems.at[a * 7 + k],
                device_id=to, device_id_type=MESH)

        mine = [pltpu.make_async_copy(ins[a], slot(a, me), local_sems.at[a]) for a in range(n)]
        for cp in mine:
            cp.start()
        first = []
        for a in range(n):
            first.append(copy(a, 0, me, sibling, src=ins[a]))
            first += [copy(a, 1 + j, me, (*chip, c), src=ins[a]) for j, chip in enumerate(chips)]
        for cp in first:
            cp.start()
        passed = []
        for j, chip in enumerate(chips):
            for a in range(n):
                copy(a, 1 + j, (*chip, c), me).wait_recv()
                fwd = copy(a, 4 + j, (*chip, c), sibling)
                fwd.start()
                passed.append(fwd)
        for a in range(n):
            copy(a, 0, sibling, me).wait_recv()
            for j, chip in enumerate(chips):
                copy(a, 4 + j, (*chip, 1 - c), me).wait_recv()
        for cp in first + passed:
            cp.wait_send()
        for cp in mine:
            cp.wait()

    return pl.pallas_call(
        body, name=name,
        in_specs=[HBM_SPEC] * n, out_specs=[HBM_SPEC] * n,
        out_shape=[jax.ShapeDtypeStruct((N_DEV,) + t.shape, t.dtype) for t in shards],
        scratch_shapes=[pltpu.SemaphoreType.DMA((7 * n,)), pltpu.SemaphoreType.DMA((7 * n,)),
                        pltpu.SemaphoreType.DMA((n,))],
    )(*shards)


def _slot(ref, p):
    return ref.at[4 * p[0] + 2 * p[1] + p[2]]


def _gather_round1(shards):
    n = len(shards)

    def plan(ins, outs, send_sems, recv_sems):
        x, y, c = _position()
        me = (x, y, c)
        peers = [(x, y, 1 - c), (1 - x, y, c), (x, 1 - y, c), (1 - x, 1 - y, c)]
        sends, recvs, local = [], [], []
        for a in range(n):
            local.append(pltpu.make_async_copy(ins[a], _slot(outs[a], me), send_sems.at[4 * n + a]))
            for k, peer in enumerate(peers):
                sems = dict(send_sem=send_sems.at[4 * a + k], recv_sem=recv_sems.at[4 * a + k],
                            device_id=peer, device_id_type=MESH)
                sends.append(pltpu.make_async_remote_copy(src_ref=ins[a], dst_ref=_slot(outs[a], me), **sems))
                recvs.append(pltpu.make_async_remote_copy(src_ref=ins[a], dst_ref=_slot(outs[a], peer), **sems))
        return sends, recvs, local

    def start(*refs):
        sends, _, local = plan(*refs)
        for cp in local + sends:
            cp.start()

    def finish(*refs):
        sends, recvs, local = plan(*refs)
        for cp in recvs:
            cp.wait_recv()
        for cp in sends:
            cp.wait_send()
        for cp in local:
            cp.wait()

    return _Comm(shards, [jax.ShapeDtypeStruct((N_DEV,) + t.shape, t.dtype) for t in shards], 5 * n, start, finish)


def _gather_round2(gathered):
    n = len(gathered)

    def plan(ins, outs, send_sems, recv_sems):
        x, y, c = _position()
        chips = [(1 - x, y), (x, 1 - y), (1 - x, 1 - y)]
        sends, recvs = [], []
        for a in range(n):
            for k, chip in enumerate(chips):
                sems = dict(send_sem=send_sems.at[3 * a + k], recv_sem=recv_sems.at[3 * a + k],
                            device_id=(x, y, 1 - c), device_id_type=MESH)
                sends.append(pltpu.make_async_remote_copy(
                    src_ref=_slot(ins[a], (*chip, c)), dst_ref=_slot(outs[a], (*chip, c)), **sems))
                recvs.append(pltpu.make_async_remote_copy(
                    src_ref=_slot(ins[a], (*chip, c)), dst_ref=_slot(outs[a], (*chip, 1 - c)), **sems))
        return sends, recvs

    def start(*refs):
        for cp in plan(*refs)[0]:
            cp.start()

    def finish(*refs):
        sends, recvs = plan(*refs)
        for cp in recvs:
            cp.wait_recv()
        for cp in sends:
            cp.wait_send()

    return _Comm(gathered, [jax.ShapeDtypeStruct(t.shape, t.dtype) for t in gathered], 3 * n, start, finish,
                 aliases={a: a for a in range(n)})


def _gather_two_level(shards, forward_at):
    n = len(shards)
    first, second = _gather_round1(shards), _gather_round2(shards)

    def middle(ins, outs, send_sems, recv_sems):
        first.finish(ins, outs, send_sems, recv_sems)
        second.start(outs, outs, _SemWindow(send_sems, first.n_sems), _SemWindow(recv_sems, first.n_sems))

    def finish(ins, outs, send_sems, recv_sems):
        second.finish(outs, outs, _SemWindow(send_sems, first.n_sems), _SemWindow(recv_sems, first.n_sems))

    return _Comm(shards, first.out_shape, first.n_sems + second.n_sems, first.start, finish,
                 middle=middle, middle_at=forward_at)


def _sibling_exchange(grads):
    n = len(grads)

    def plan(ins, outs, send_sems, recv_sems):
        x, y, c = _position()
        return [pltpu.make_async_remote_copy(
            src_ref=ins[a].at[2 * q + (1 - c)], dst_ref=outs[a].at[q],
            send_sem=send_sems.at[a * 4 + q], recv_sem=recv_sems.at[a * 4 + q],
            device_id=(x, y, 1 - c), device_id_type=MESH) for a in range(n) for q in range(4)]

    def start(*refs):
        for cp in plan(*refs):
            cp.start()

    def finish(*refs):
        for cp in plan(*refs):
            cp.wait()

    return _Comm(grads, [jax.ShapeDtypeStruct((4,) + t.shape[1:], t.dtype) for t in grads], 4 * n, start, finish)


def _chip_exchange(partials):
    n = len(partials)

    def plan(ins, outs, send_sems, recv_sems):
        x, y, c = _position()
        chips = [(1 - x, y), (x, 1 - y), (1 - x, 1 - y)]
        return [pltpu.make_async_remote_copy(
            src_ref=ins[a].at[2 * chip[0] + chip[1]], dst_ref=outs[a].at[k],
            send_sem=send_sems.at[a * 3 + k], recv_sem=recv_sems.at[a * 3 + k],
            device_id=(*chip, c), device_id_type=MESH) for a in range(n) for k, chip in enumerate(chips)]

    def start(*refs):
        for cp in plan(*refs):
            cp.start()

    def finish(*refs):
        for cp in plan(*refs):
            cp.wait()

    return _Comm(partials, [jax.ShapeDtypeStruct((3,) + t.shape[1:], t.dtype) for t in partials], 3 * n, start, finish)


class _SemWindow:
    def __init__(self, sems, offset):
        self._sems, self._offset = sems, offset

    @property
    def at(self):
        return self

    def __getitem__(self, i):
        return self._sems.at[self._offset + i]


def _join(parts):
    def each(fn_name, cins, couts, send_sems, recv_sems):
        i = o = sem = 0
        for p in parts:
            ni, no = len(p.operands), len(p.out_shape)
            getattr(p, fn_name)(cins[i:i + ni], couts[o:o + no], _SemWindow(send_sems, sem), _SemWindow(recv_sems, sem))
            i, o, sem = i + ni, o + no, sem + p.n_sems

    assert not any(p.aliases for p in parts)
    return _Comm([t for p in parts for t in p.operands], [t for p in parts for t in p.out_shape],
                 sum(p.n_sems for p in parts), functools.partial(each, "start"), functools.partial(each, "finish"))


def _pair_sum_call(grad, recv, parity, name):
    _, r, ccols = grad.shape
    tr = _tile(r, 256)

    def body(par_ref, g_ref, r_ref, p_ref, pb_ref):
        del par_ref
        p = g_ref[...] + r_ref[...]
        p_ref[...] = p
        pb_ref[...] = p.astype(BF)

    blk = lambda fn: pl.BlockSpec((None, tr, ccols), fn)
    return pl.pallas_call(
        body, name=name,
        grid_spec=pltpu.PrefetchScalarGridSpec(
            num_scalar_prefetch=1, grid=(4, r // tr),
            in_specs=[blk(lambda q, i, par: (2 * q + par[0], i, 0)), blk(lambda q, i, par: (q, i, 0))],
            out_specs=[blk(lambda q, i, par: (q, i, 0)), blk(lambda q, i, par: (q, i, 0))]),
        out_shape=[jax.ShapeDtypeStruct((4, r, ccols), F32), jax.ShapeDtypeStruct((4, r, ccols), BF)],
        compiler_params=_params("parallel", "parallel"),
    )(parity, grad, recv)


def _adamw_math(w, g, m, v):
    m = ADAM_B1 * m + (1.0 - ADAM_B1) * g
    v = ADAM_B2 * v + (1.0 - ADAM_B2) * (g * g)
    m_hat = m / (1.0 - ADAM_B1 ** ADAM_STEP)
    v_hat = v / (1.0 - ADAM_B2 ** ADAM_STEP)
    delta = -ADAM_LR * (m_hat / (jnp.sqrt(v_hat) + ADAM_EPS) + ADAM_WD * w)
    return delta, m, v


def _adamw_matrix_call(partial, recv, chip, w, m, v, name):
    r, ccols = w.shape
    gcols = partial.shape[2]
    tr = _tile(r, 256)

    def body(chip_ref, p_ref, r_ref, w_ref, m_ref, v_ref, g_out, d_out, m_out, v_out):
        del chip_ref
        cols = pl.ds(0, ccols)
        g = (p_ref[:, cols] + r_ref[0, :, cols].astype(F32) + r_ref[1, :, cols].astype(F32)
             + r_ref[2, :, cols].astype(F32))
        delta, mn, vn = _adamw_math(w_ref[...], g, m_ref[...], v_ref[...])
        g_out[...] = g
        d_out[...] = delta
        m_out[...] = mn
        v_out[...] = vn

    mat = pl.BlockSpec((tr, ccols), lambda i, ch: (i, 0))
    shp = jax.ShapeDtypeStruct((r, ccols), F32)
    return pl.pallas_call(
        body, name=name,
        grid_spec=pltpu.PrefetchScalarGridSpec(
            num_scalar_prefetch=1, grid=(r // tr,),
            in_specs=[pl.BlockSpec((None, tr, gcols), lambda i, ch: (ch[0], i, 0)),
                      pl.BlockSpec((3, tr, gcols), lambda i, ch: (0, i, 0)), mat, mat, mat],
            out_specs=[mat, mat, mat, mat]),
        out_shape=[shp, shp, shp, shp],
        compiler_params=_params("parallel"),
    )(chip, partial, recv, w, m, v)


def _adamw_vector_call(gathered, w, m, v, name):
    n = w.shape[1]

    def body(p_ref, w_ref, m_ref, v_ref, g_out, d_out, m_out, v_out):
        g = p_ref[0:1, :]
        for k in range(1, N_DEV):
            g = g + p_ref[k:k + 1, :]
        delta, mn, vn = _adamw_math(w_ref[...], g, m_ref[...], v_ref[...])
        g_out[...] = g
        d_out[...] = delta
        m_out[...] = mn
        v_out[...] = vn

    shp = jax.ShapeDtypeStruct((1, n), F32)
    return pl.pallas_call(body, name=name, out_shape=[shp, shp, shp, shp])(gathered, w, m, v)


def _round_up(n, mult):
    return (n + mult - 1) // mult * mult


def kernel(x, ffn1_norm, ffn1_w_gate, ffn1_w_up, ffn1_w_down, mix_norm, w_in, ret_norm_g, hgrn_lb_logits, hgrn_norm_g, w_out, ffn2_norm, ffn2_w_gate, ffn2_w_up, ffn2_w_down, final_norm, loss_target, m_ffn1_norm, m_ffn1_w_gate, m_ffn1_w_up, m_ffn1_w_down, m_mix_norm, m_w_in, m_ret_norm_g, m_hgrn_lb_logits, m_hgrn_norm_g, m_w_out, m_ffn2_norm, m_ffn2_w_gate, m_ffn2_w_up, m_ffn2_w_down, m_final_norm, v_ffn1_norm, v_ffn1_w_gate, v_ffn1_w_up, v_ffn1_w_down, v_mix_norm, v_w_in, v_ret_norm_g, v_hgrn_lb_logits, v_hgrn_norm_g, v_w_out, v_ffn2_norm, v_ffn2_w_gate, v_ffn2_w_up, v_ffn2_w_down, v_final_norm):
    xs = x[0]
    target = loss_target[0]
    s, d = xs.shape
    f_loc = ffn1_w_gate.shape[2]
    fp = _round_up(f_loc, LANE)
    pad_cols = lambda t: jnp.pad(t[0], ((0, 0), (0, fp - f_loc)))
    pad_rows = lambda t: jnp.pad(t[0], ((0, fp - f_loc), (0, 0)))

    mat_names = ["ffn1_w_gate", "ffn1_w_up", "ffn1_w_down", "w_in", "w_out", "ffn2_w_gate", "ffn2_w_up", "ffn2_w_down"]
    mat_pad = [pad_cols, pad_cols, pad_rows, lambda t: t[0], lambda t: t[0], pad_cols, pad_cols, pad_rows]
    mat_w = [ffn1_w_gate, ffn1_w_up, ffn1_w_down, w_in, w_out, ffn2_w_gate, ffn2_w_up, ffn2_w_down]
    mat_m = [m_ffn1_w_gate, m_ffn1_w_up, m_ffn1_w_down, m_w_in, m_w_out, m_ffn2_w_gate, m_ffn2_w_up, m_ffn2_w_down]
    mat_v = [v_ffn1_w_gate, v_ffn1_w_up, v_ffn1_w_down, v_w_in, v_w_out, v_ffn2_w_gate, v_ffn2_w_up, v_ffn2_w_down]

    cx, cy, cc = _position()
    parity = jnp.reshape(cc, (1,)).astype(jnp.int32)
    chip = jnp.reshape(2 * cx + cy, (1,)).astype(jnp.int32)
    mat_index = {nm: i for i, nm in enumerate(mat_names)}
    mat_out = {}

    def pair_sums(names, grads, from_sibling):
        return [_pair_sum_call(g, r, parity, "pair_sum_" + nm) for nm, g, r in zip(names, grads, from_sibling)]

    def update(names, sums, from_chips):
        for nm, (p, _), r in zip(names, sums, from_chips):
            i = mat_index[nm]
            res = _adamw_matrix_call(p, r, chip, mat_w[i][0], mat_m[i][0], mat_v[i][0], "adamw_" + nm)
            mat_out[nm] = [t[None] for t in res]

    shards = [p(t.astype(BF)) for p, t in zip(mat_pad, mat_w)]
    wg1, wu1 = _all_gather_call(shards[:2], "gather_ffn1_up")

    h1 = _rmsnorm_call(xs, ffn1_norm, "ffn1_norm")
    (g1, u1, a1), (wd1, win) = _ffn_up_call(h1, wg1, wu1, "ffn1_up", comm=_gather_two_level(shards[2:4], 0.7))
    (x1, h2), _ = _down_call(a1, wd1, xs, mix_norm, FFN_RESIDUAL_WEIGHT, "ffn1_down")
    (proj,), _ = _proj_call(h2, win, "mix_in")
    wmix = proj.shape[1] // 8
    cos, sin = _rope_tables(s, wmix // RET_HEADS)
    consts = _ret_consts(wmix // RET_HEADS)
    (o_hg, m_hg, st_hg), landed = _hgrn_fwd_call(proj, hgrn_lb_logits, hgrn_norm_g, "hgrn_fwd",
                                                 comm=_gather_round1(shards[4:]))
    (o_ret, m_ret, st_ret), (wout, wg2, wu2, wd2) = _ret_fwd_call(proj, cos, sin, consts, ret_norm_g, "ret_fwd",
                                                                  comm=_gather_round2(landed))
    merged = jnp.concatenate([m_ret, m_hg], axis=1)
    wout_wide = wout.reshape(2, wout.shape[0] * wout.shape[1] // 2, d)
    (x2, h3), _ = _down_call(merged, wout_wide, x1, ffn2_norm, 1.0, "mix_out")
    (g2, u2, a2), _ = _ffn_up_call(h3, wg2, wu2, "ffn2_up")
    (x3,), _ = _down_call(a2, wd2, x2, None, FFN_RESIDUAL_WEIGHT, "ffn2_down")
    loss_part, dx3, dx3b, gv_final = _loss_call(x3, target, final_norm[None, :], "loss_head")

    (dg2, du2), _ = _bwd_up_call(dx3b, wd2, g2, u2, FFN_RESIDUAL_WEIGHT, "ffn2_bwd_up")
    dh3, _ = _bwd_mm_call([(dg2, wg2), (du2, wu2)], "ffn2_bwd_down")
    dx2, dx2b, gv_n3 = _norm_bwd_call(dh3, dx3, x2, ffn2_norm, "ffn2_norm_bwd")
    names_a = ["ffn2_w_gate", "ffn2_w_up", "ffn2_w_down"]
    grads_a = [_wgrad_call(h3, dg2, N_DEV, False, 1.0, "ffn2_wgrad_gate"),
               _wgrad_call(h3, du2, N_DEV, False, 1.0, "ffn2_wgrad_up"),
               _wgrad_call(a2, dx3b, N_DEV, True, FFN_RESIDUAL_WEIGHT, "ffn2_wgrad_down")]

    (dmerged,), sib_a = _proj_call(dx2b, jnp.swapaxes(wout_wide, 1, 2), "mix_out_bwd",
                                   comm=_sibling_exchange(grads_a))
    gm_out = _wgrad_call(merged, dx2b, 2, True, 1.0, "mix_out_wgrad").reshape(wout.shape)
    sums_a = pair_sums(names_a, grads_a, sib_a)
    dproj_half, gv_ret = _ret_bwd_call(proj, cos, sin, consts, ret_norm_g, o_ret, st_ret, dmerged, "ret_bwd")
    (dproj, gv_lb, gv_hg), landed = _hgrn_bwd_call(
        proj, hgrn_lb_logits, hgrn_norm_g, o_hg, st_hg, dmerged, dproj_half, "hgrn_bwd",
        comm=_join([_chip_exchange([pb for _, pb in sums_a]), _sibling_exchange([gm_out])]))
    update(names_a, sums_a, landed[:3])
    sums_out = pair_sums(["w_out"], [gm_out], landed[3:])
    dh2, chips_out = _bwd_mm_call([(dproj, win)], "mix_in_bwd", per_step=2, comm=_chip_exchange([sums_out[0][1]]))
    dx1, dx1b, gv_n2 = _norm_bwd_call(dh2, dx2, x1, mix_norm, "mix_norm_bwd")
    update(["w_out"], sums_out, chips_out)

    gm_in = _wgrad_call(h2, dproj, N_DEV, False, 1.0, "mix_in_wgrad")
    gm_d1, sib_in = _wgrad_call(a1, dx1b, N_DEV, True, FFN_RESIDUAL_WEIGHT, "ffn1_wgrad_down",
                                comm=_sibling_exchange([gm_in]))
    sums_in = pair_sums(["w_in"], [gm_in], sib_in)
    (dg1, du1), landed = _bwd_up_call(dx1b, wd1, g1, u1, FFN_RESIDUAL_WEIGHT, "ffn1_bwd_up",
                                      comm=_join([_chip_exchange([sums_in[0][1]]), _sibling_exchange([gm_d1])]))
    update(["w_in"], sums_in, landed[:1])
    sums_d1 = pair_sums(["ffn1_w_down"], [gm_d1], landed[1:])
    gm_g1, chips_d1 = _wgrad_call(h1, dg1, N_DEV, False, 1.0, "ffn1_wgrad_gate",
                                  comm=_chip_exchange([sums_d1[0][1]]))
    update(["ffn1_w_down"], sums_d1, chips_d1)
    gm_u1, sib_g = _wgrad_call(h1, du1, N_DEV, False, 1.0, "ffn1_wgrad_up", comm=_sibling_exchange([gm_g1]))
    sums_g = pair_sums(["ffn1_w_gate"], [gm_g1], sib_g)
    n_row_tiles = s // _tile(s, 1024)
    assert n_row_tiles >= 2, "the sequence must span at least two row tiles"
    n_first = n_row_tiles // 2
    dh1_part, landed = _bwd_mm_call(
        [(dg1, wg1), (du1, wu1)], "ffn1_bwd_down_a", tiles=(0, n_first),
        comm=_join([_chip_exchange([sums_g[0][1]]), _sibling_exchange([gm_u1])]))
    update(["ffn1_w_gate"], sums_g, landed[:1])
    sums_u = pair_sums(["ffn1_w_up"], [gm_u1], landed[1:])
    dh1, chips_u = _bwd_mm_call(
        [(dg1, wg1), (du1, wu1)], "ffn1_bwd_down_b", tiles=(n_first, n_row_tiles - n_first),
        carry=dh1_part, comm=_chip_exchange([sums_u[0][1]]))
    update(["ffn1_w_up"], sums_u, chips_u)
    dx0, _, gv_n1 = _norm_bwd_call(dh1, dx1, xs, ffn1_norm, "ffn1_norm_bwd")

    vec_names = ["ffn1_norm", "mix_norm", "ret_norm_g", "hgrn_lb_logits", "hgrn_norm_g", "ffn2_norm", "final_norm"]
    vec_g = [gv_n1, gv_n2, gv_ret, gv_lb, gv_hg, gv_n3, gv_final]
    vec_w = [ffn1_norm, mix_norm, ret_norm_g, hgrn_lb_logits, hgrn_norm_g, ffn2_norm, final_norm[None, :]]
    vec_m = [m_ffn1_norm, m_mix_norm, m_ret_norm_g, m_hgrn_lb_logits, m_hgrn_norm_g, m_ffn2_norm, m_final_norm[None, :]]
    vec_v = [v_ffn1_norm, v_mix_norm, v_ret_norm_g, v_hgrn_lb_logits, v_hgrn_norm_g, v_ffn2_norm, v_final_norm[None, :]]
    cat = lambda ts: jnp.concatenate(ts, axis=1)
    (vec_all,) = _all_gather_call([cat(vec_g)], "gather_vector_grads")
    vres = _adamw_vector_call(vec_all[:, 0, :], cat(vec_w), cat(vec_m), cat(vec_v), "adamw_vectors")
    vec_out = {}
    off = 0
    for nm, t in zip(vec_names, vec_w):
        n = t.shape[1]
        parts = [r[:, off:off + n] for r in vres]
        if nm == "final_norm":
            parts = [p[0] for p in parts]
        vec_out[nm] = parts
        off += n

    loss = lax.psum(loss_part[0, 0], ("x", "y", "c"))
    order = ["ffn1_norm", "ffn1_w_gate", "ffn1_w_up", "ffn1_w_down", "mix_norm", "w_in", "ret_norm_g", "hgrn_lb_logits",
             "hgrn_norm_g", "w_out", "ffn2_norm", "ffn2_w_gate", "ffn2_w_up", "ffn2_w_down", "final_norm"]
    res = {**mat_out, **vec_out}
    outs = [loss, dx0[None]]
    for kind in range(4):
        outs += [res[nm][kind] for nm in order]
    return tuple(outs)
```

```python
import functools

import jax
import jax.numpy as jnp
from jax import lax
from jax.experimental import pallas as pl
from jax.experimental.pallas import tpu as pltpu

BF = jnp.bfloat16
F32 = jnp.float32
MESH = pl.DeviceIdType.MESH
HBM_SPEC = pl.BlockSpec(memory_space=pltpu.HBM)

N_DEV = 8
LANE = 128
EPS = 1e-6
ROPE_BASE = 10000.0
RET_HEADS = 4
HGRN_HEADS = 8
RET_CHUNK = 128
HGRN_BLOCK = 16
FFN_RESIDUAL_WEIGHT = 0.5
ADAM_LR = 0.001
ADAM_B1 = 0.9
ADAM_B2 = 0.999
ADAM_EPS = 1e-08
ADAM_WD = 0.01
ADAM_STEP = 10
VMEM_LIMIT = 56 * 1024 * 1024


def _tile(n, pref, mult=8):
    t = min(pref, n)
    t -= t % mult
    while t >= mult:
        if n % t == 0:
            return t
        t -= mult
    return n


def _params(*sem):
    return pltpu.CompilerParams(dimension_semantics=sem, vmem_limit_bytes=VMEM_LIMIT)


class _Comm:
    def __init__(self, operands, out_shape, n_sems, start, finish, aliases=None, middle=None, middle_at=0.0):
        self.operands = list(operands)
        self.out_shape = list(out_shape)
        self.n_sems = n_sems
        self.start = start
        self.finish = finish
        self.aliases = dict(aliases or {})
        self.middle = middle
        self.middle_at = middle_at


def _launch(body, *, name, grid, in_specs, out_specs, out_shape, sem, args, scratch_shapes=(), aliases=None, comm=None):
    in_specs, out_specs, out_shape = list(in_specs), list(out_specs), list(out_shape)
    scratch_shapes = list(scratch_shapes)
    aliases = dict(aliases or {})
    if comm is None:
        res = pl.pallas_call(body, name=name, grid=grid, in_specs=in_specs, out_specs=out_specs, out_shape=out_shape,
                             scratch_shapes=scratch_shapes, input_output_aliases=aliases,
                             compiler_params=_params(*sem))(*args)
        return list(res), []
    n_in, n_out, n_scr = len(in_specs), len(out_specs), len(scratch_shapes)
    ci, co = len(comm.operands), len(comm.out_shape)

    def carrying(*refs):
        bounds = [0, n_in, n_in + ci, n_in + ci + n_out, n_in + ci + n_out + co, n_in + ci + n_out + co + n_scr]
        ins, cins, outs, couts, scr = [refs[a:b] for a, b in zip(bounds[:-1], bounds[1:])]
        send_sems, recv_sems = refs[bounds[-1]:]
        ids = [pl.program_id(k) for k in range(len(grid))]
        first = functools.reduce(jnp.logical_and, [i == 0 for i in ids])
        last = functools.reduce(jnp.logical_and, [i == g - 1 for i, g in zip(ids, grid)])

        @pl.when(first)
        def _():
            comm.start(cins, couts, send_sems, recv_sems)

        if comm.middle is not None:
            step, total = ids[0], grid[0]
            for i, g in zip(ids[1:], grid[1:]):
                step, total = step * g + i, total * g

            @pl.when(step == int(total * comm.middle_at))
            def _():
                comm.middle(cins, couts, send_sems, recv_sems)

        body(*ins, *outs, *scr)

        @pl.when(last)
        def _():
            comm.finish(cins, couts, send_sems, recv_sems)

    res = pl.pallas_call(
        carrying, name=name, grid=grid,
        in_specs=in_specs + [HBM_SPEC] * ci, out_specs=out_specs + [HBM_SPEC] * co,
        out_shape=out_shape + comm.out_shape,
        scratch_shapes=scratch_shapes + [pltpu.SemaphoreType.DMA((comm.n_sems,)), pltpu.SemaphoreType.DMA((comm.n_sems,))],
        input_output_aliases={**aliases, **{n_in + a: n_out + b for a, b in comm.aliases.items()}},
        compiler_params=_params(*(["arbitrary"] * len(grid))),
    )(*args, *comm.operands)
    return list(res[:n_out]), list(res[n_out:])


def _comm_only_call(comm, name):
    def body(*refs):
        ci, co = len(comm.operands), len(comm.out_shape)
        cins, couts = refs[:ci], refs[ci:ci + co]
        send_sems, recv_sems = refs[ci + co:]
        comm.start(cins, couts, send_sems, recv_sems)
        comm.finish(cins, couts, send_sems, recv_sems)

    return pl.pallas_call(
        body, name=name,
        in_specs=[HBM_SPEC] * len(comm.operands), out_specs=[HBM_SPEC] * len(comm.out_shape),
        out_shape=comm.out_shape,
        scratch_shapes=[pltpu.SemaphoreType.DMA((comm.n_sems,)), pltpu.SemaphoreType.DMA((comm.n_sems,))],
        input_output_aliases=comm.aliases,
    )(*comm.operands)


def _sigmoid(v):
    return 0.5 * jnp.tanh(0.5 * v) + 0.5


def _dot(a, b):
    return jnp.dot(a, b, preferred_element_type=F32)


def _dot_nt(a, b):
    return lax.dot_general(a, b, (((1,), (1,)), ((), ())), preferred_element_type=F32)


def _dot_tn(a, b):
    return lax.dot_general(a, b, (((0,), (0,)), ((), ())), preferred_element_type=F32)


def _rmsnorm_call(x, gain, name):
    s, d = x.shape
    tm = _tile(s, 512)

    def body(x_ref, g_ref, o_ref):
        xv = x_ref[...]
        r = lax.rsqrt(jnp.mean(xv * xv, axis=-1, keepdims=True) + EPS)
        o_ref[...] = (xv * r * g_ref[...]).astype(BF)

    return pl.pallas_call(
        body, name=name, grid=(s // tm,),
        in_specs=[pl.BlockSpec((tm, d), lambda i: (i, 0)), pl.BlockSpec((1, d), lambda i: (0, 0))],
        out_specs=pl.BlockSpec((tm, d), lambda i: (i, 0)),
        out_shape=jax.ShapeDtypeStruct((s, d), BF),
        compiler_params=_params("parallel"),
    )(x, gain)


def _ffn_up_call(h, wg, wu, name, comm=None):
    s, d = h.shape
    nj, _, k = wg.shape
    tm = _tile(s, 1024)

    def body(h_ref, wg_ref, wu_ref, g_ref, u_ref, a_ref):
        hv = h_ref[...]
        g = _dot(hv, wg_ref[...])
        u = _dot(hv, wu_ref[...])
        g_ref[...] = g.astype(BF)
        u_ref[...] = u.astype(BF)
        a_ref[...] = (g * _sigmoid(g) * u).astype(BF)

    act = pl.BlockSpec((tm, k), lambda i, j: (i, j))
    wsp = pl.BlockSpec((None, d, k), lambda i, j: (j, 0, 0))
    return _launch(
        body, name=name, grid=(s // tm, nj),
        in_specs=[pl.BlockSpec((tm, d), lambda i, j: (i, 0)), wsp, wsp],
        out_specs=[act, act, act],
        out_shape=[jax.ShapeDtypeStruct((s, nj * k), BF)] * 3,
        sem=("parallel", "arbitrary"), args=(h, wg, wu), comm=comm)


def _proj_call(h, w, name, comm=None):
    s, d = h.shape
    nj, _, k = w.shape
    tm = _tile(s, 1024)

    def body(h_ref, w_ref, o_ref):
        o_ref[...] = _dot(h_ref[...], w_ref[...])

    return _launch(
        body, name=name, grid=(s // tm, nj),
        in_specs=[pl.BlockSpec((tm, d), lambda i, j: (i, 0)), pl.BlockSpec((None, d, k), lambda i, j: (j, 0, 0))],
        out_specs=[pl.BlockSpec((tm, k), lambda i, j: (i, j))],
        out_shape=[jax.ShapeDtypeStruct((s, nj * k), F32)],
        sem=("parallel", "arbitrary"), args=(h, w), comm=comm)


def _down_call(a, w, resid, gain, scale, name, comm=None):
    s = a.shape[0]
    nj, k, d = w.shape
    tm = _tile(s, 1024)
    strip = _tile(tm, 128)
    cn = _tile(d, 512, LANE)
    with_norm = gain is not None

    def body(*refs):
        if with_norm:
            a_ref, w_ref, r_ref, g_ref, x_ref, h_ref = refs
        else:
            a_ref, w_ref, r_ref, x_ref = refs
        j = pl.program_id(1)

        av = a_ref[...]

        @pl.when(j == 0)
        def _():
            for n0 in range(0, d, cn):
                x_ref[:, n0:n0 + cn] = _dot(av, w_ref[:, n0:n0 + cn])

        @pl.when(j > 0)
        def _():
            for n0 in range(0, d, cn):
                x_ref[:, n0:n0 + cn] += _dot(av, w_ref[:, n0:n0 + cn])

        @pl.when(j == nj - 1)
        def _():
            for r0 in range(0, tm, strip):
                rows = slice(r0, r0 + strip)
                xn = r_ref[rows, :] + (scale * x_ref[rows, :])
                x_ref[rows, :] = xn
                if with_norm:
                    r = lax.rsqrt(jnp.mean(xn * xn, axis=-1, keepdims=True) + EPS)
                    h_ref[rows, :] = (xn * r * g_ref[...]).astype(BF)

    row = pl.BlockSpec((tm, d), lambda i, j: (i, 0))
    once = pl.BlockSpec((tm, d), lambda i, j: (i, 0), pipeline_mode=pl.Buffered(1))
    in_specs = [pl.BlockSpec((tm, k), lambda i, j: (i, j)), pl.BlockSpec((None, k, d), lambda i, j: (j, 0, 0)), once]
    args = [a, w, resid]
    out_specs = [row]
    out_shape = [jax.ShapeDtypeStruct((s, d), F32)]
    if with_norm:
        in_specs.append(pl.BlockSpec((1, d), lambda i, j: (0, 0)))
        args.append(gain)
        out_specs.append(row)
        out_shape.append(jax.ShapeDtypeStruct((s, d), BF))
    return _launch(
        body, name=name, grid=(s // tm, nj),
        in_specs=in_specs, out_specs=out_specs, out_shape=out_shape,
        sem=("parallel", "arbitrary"), args=args, comm=comm)


def _loss_call(x, target, gain, name):
    s, d = x.shape
    tm = _tile(s, 512)

    def body(x_ref, t_ref, g_ref, loss_ref, dx_ref, dxb_ref, dg_ref):
        i = pl.program_id(0)

        @pl.when(i == 0)
        def _():
            loss_ref[...] = jnp.zeros_like(loss_ref)
            dg_ref[...] = jnp.zeros_like(dg_ref)

        xv = x_ref[...]
        gv = g_ref[...]
        r = lax.rsqrt(jnp.mean(xv * xv, axis=-1, keepdims=True) + EPS)
        xhat = xv * r
        err = xhat * gv - t_ref[...]
        per_tok = jnp.mean(err * err, axis=-1, keepdims=True)
        loss_ref[...] += 0.5 * jnp.sum(per_tok, axis=0, keepdims=True)
        dout = err * (1.0 / d)
        dg_ref[...] += jnp.sum(dout * xhat, axis=0, keepdims=True)
        dxhat = dout * gv
        dx = r * (dxhat - xhat * jnp.mean(dxhat * xhat, axis=-1, keepdims=True))
        dx_ref[...] = dx
        dxb_ref[...] = dx.astype(BF)

    row = pl.BlockSpec((tm, d), lambda i: (i, 0))
    vec = pl.BlockSpec((1, d), lambda i: (0, 0))
    return pl.pallas_call(
        body, name=name, grid=(s // tm,),
        in_specs=[row, row, vec],
        out_specs=[pl.BlockSpec((1, 1), lambda i: (0, 0)), row, row, vec],
        out_shape=[jax.ShapeDtypeStruct((1, 1), F32), jax.ShapeDtypeStruct((s, d), F32),
                   jax.ShapeDtypeStruct((s, d), BF), jax.ShapeDtypeStruct((1, d), F32)],
        compiler_params=_params("arbitrary"),
    )(x, target, gain)


def _bwd_up_call(dy, wd, g, u, scale, name, comm=None):
    s, d = dy.shape
    nj, k, _ = wd.shape
    tm = _tile(s, 1024)

    def body(dy_ref, w_ref, g_ref, u_ref, dg_ref, du_ref):
        da = scale * _dot_nt(dy_ref[...], w_ref[...])
        gv = g_ref[...].astype(F32)
        sig = _sigmoid(gv)
        du_ref[...] = (da * gv * sig).astype(BF)
        dg_ref[...] = (da * u_ref[...].astype(F32) * sig * (1.0 + gv * (1.0 - sig))).astype(BF)

    act = pl.BlockSpec((tm, k), lambda i, j: (i, j))
    return _launch(
        body, name=name, grid=(s // tm, nj),
        in_specs=[pl.BlockSpec((tm, d), lambda i, j: (i, 0)), pl.BlockSpec((None, k, d), lambda i, j: (j, 0, 0)), act, act],
        out_specs=[act, act],
        out_shape=[jax.ShapeDtypeStruct((s, nj * k), BF), jax.ShapeDtypeStruct((s, nj * k), BF)],
        sem=("parallel", "arbitrary"), args=(dy, wd, g, u), comm=comm)


def _norm_bwd_call(dh, dres, xin, gain, name):
    s, d = xin.shape
    tm = _tile(s, 256)

    def body(dh_ref, dres_ref, x_ref, g_ref, dx_ref, dxb_ref, dg_ref):
        @pl.when(pl.program_id(0) == 0)
        def _():
            dg_ref[...] = jnp.zeros_like(dg_ref)

        xv = x_ref[...]
        r = lax.rsqrt(jnp.mean(xv * xv, axis=-1, keepdims=True) + EPS)
        xhat = xv * r
        dh_v = dh_ref[...]
        dg_ref[...] += jnp.sum(dh_v * xhat, axis=0, keepdims=True)
        dxhat = dh_v * g_ref[...]
        dx = dres_ref[...] + r * (dxhat - xhat * jnp.mean(dxhat * xhat, axis=-1, keepdims=True))
        dx_ref[...] = dx
        dxb_ref[...] = dx.astype(BF)

    row = pl.BlockSpec((tm, d), lambda i: (i, 0))
    vec = pl.BlockSpec((1, d), lambda i: (0, 0))
    return pl.pallas_call(
        body, name=name, grid=(s // tm,),
        in_specs=[row, row, row, vec], out_specs=[row, row, vec],
        out_shape=[jax.ShapeDtypeStruct((s, d), F32), jax.ShapeDtypeStruct((s, d), BF), jax.ShapeDtypeStruct((1, d), F32)],
        compiler_params=_params("arbitrary"),
    )(dh, dres, xin, gain)


def _bwd_mm_call(pairs, name, per_step=1, tiles=None, carry=None, comm=None):
    a0 = pairs[0][0]
    s = a0.shape[-2]
    nblocks, d, k = pairs[0][1].shape
    nj = nblocks // per_step
    npair = len(pairs) * per_step
    tm = _tile(s, 1024)
    cn = _tile(d, 512, LANE)

    first_tile, n_tiles = (0, s // tm) if tiles is None else tiles

    def body(*refs):
        a_refs = refs[0:2 * npair:2]
        w_refs = refs[1:2 * npair:2]
        dh_ref = refs[-1]

        @pl.when(pl.program_id(1) == 0)
        def _():
            dh_ref[...] = jnp.zeros_like(dh_ref)

        for a_ref, w_ref in zip(a_refs, w_refs):
            av = a_ref[...]
            for n0 in range(0, d, cn):
                dh_ref[:, n0:n0 + cn] += _dot_nt(av, w_ref[n0:n0 + cn, :])

    row = pl.BlockSpec((tm, d), lambda i, j: (first_tile + i, 0))
    in_specs, args = [], []
    for a, w in pairs:
        for r in range(per_step):
            in_specs += [pl.BlockSpec((None, tm, k), lambda i, j, r=r: (j * per_step + r, first_tile + i, 0))
                         if a.ndim == 3 else
                         pl.BlockSpec((tm, k), lambda i, j, r=r: (first_tile + i, j * per_step + r)),
                         pl.BlockSpec((None, d, k), lambda i, j, r=r: (j * per_step + r, 0, 0))]
            args += [a, w]
    aliases = {}
    if carry is not None:
        aliases = {len(in_specs): 0}
        in_specs.append(pl.BlockSpec(memory_space=pl.ANY))
        args.append(carry)
    (dh,), landed = _launch(
        body, name=name, grid=(n_tiles, nj),
        in_specs=in_specs, out_specs=[row], out_shape=[jax.ShapeDtypeStruct((s, d), F32)],
        sem=("parallel", "arbitrary"), args=args, aliases=aliases, comm=comm)
    return dh, landed


def _wgrad_call(a, b, nj, a_blocked, scale, name, comm=None):
    s = a.shape[0]
    ka = a.shape[1] // nj if a_blocked else a.shape[1]
    b_stacked = b.ndim == 3
    kb = b.shape[-1] if (a_blocked or b_stacked) else b.shape[1] // nj
    ts = _tile(s, 2048)
    ns = s // ts

    def body(a_ref, b_ref, o_ref):
        t = pl.program_id(1)

        @pl.when(t == 0)
        def _():
            o_ref[...] = jnp.zeros_like(o_ref)

        o_ref[...] += _dot_tn(a_ref[...], b_ref[...])
        if scale != 1.0:
            @pl.when(t == ns - 1)
            def _():
                o_ref[...] = o_ref[...] * scale

    a_spec = pl.BlockSpec((ts, ka), (lambda j, t: (t, j)) if a_blocked else (lambda j, t: (t, 0)))
    if b_stacked:
        b_spec = pl.BlockSpec((None, ts, kb), lambda j, t: (j, t, 0))
    else:
        b_spec = pl.BlockSpec((ts, kb), (lambda j, t: (t, 0)) if a_blocked else (lambda j, t: (t, j)))
    (out,), landed = _launch(
        body, name=name, grid=(nj, ns),
        in_specs=[a_spec, b_spec],
        out_specs=[pl.BlockSpec((None, ka, kb), lambda j, t: (j, 0, 0))],
        out_shape=[jax.ShapeDtypeStruct((nj, ka, kb), F32)],
        sem=("parallel", "arbitrary"), args=(a, b), comm=comm)
    return out if comm is None else (out, landed)


def _rope(v, cos, sin):
    half = v.shape[-1] // 2
    v1, v2 = v[:, :half], v[:, half:]
    return jnp.concatenate([v1 * cos - v2 * sin, v2 * cos + v1 * sin], axis=-1)


def _rope_bwd(dv, cos, sin):
    half = dv.shape[-1] // 2
    d1, d2 = dv[:, :half], dv[:, half:]
    return jnp.concatenate([d1 * cos + d2 * sin, d2 * cos - d1 * sin], axis=-1)


def _ret_consts(hd):
    c = RET_CHUNK
    log_gamma = jnp.log(1.0 - jnp.exp2(-5.0 - jnp.arange(RET_HEADS, dtype=F32)))
    idx = jnp.arange(c, dtype=F32)
    rel = idx[:, None] - idx[None, :]
    mask = rel >= 0
    decay = jnp.where(mask[None], jnp.exp(log_gamma[:, None, None] * jnp.where(mask, rel, 0.0)[None]), 0.0)
    qdec = jnp.exp(log_gamma[:, None] * (idx + 1.0)[None, :])
    kdec = jnp.exp(log_gamma[:, None] * (c - 1.0 - idx)[None, :])
    gchunk = jnp.exp(log_gamma * c)
    bc = lambda t: jnp.broadcast_to(t[:, :, None], (RET_HEADS, t.shape[1], hd))
    return decay, bc(qdec), bc(kdec), bc(gchunk[:, None])


def _rope_tables(s, hd):
    inv = jnp.power(ROPE_BASE, -jnp.arange(0, hd, 2, dtype=F32) / hd)
    ang = jnp.arange(s, dtype=F32)[:, None] * inv[None, :]
    return jnp.cos(ang), jnp.sin(ang)


def _ret_fwd_call(proj, cos, sin, consts, gret, name, comm=None):
    s = proj.shape[0]
    w = proj.shape[1] // 8
    hd = w // RET_HEADS
    c = RET_CHUNK
    tt = _tile(s, 512, c)
    nc = tt // c
    decay, qdec, kdec, gch = consts
    scale = hd ** -0.5

    def body(q_ref, k_ref, v_ref, gate_ref, cos_ref, sin_ref, dec_ref, qd_ref, kd_ref, gc_ref, gn_ref,
             o_ref, m_ref, st_ref, state):
        @pl.when(pl.program_id(1) == 0)
        def _():
            state[...] = jnp.zeros_like(state)

        dec = dec_ref[...]
        for ci in range(nc):
            rows = slice(ci * c, (ci + 1) * c)
            cs, sn = cos_ref[rows, :], sin_ref[rows, :]
            q = _rope(q_ref[rows, :], cs, sn) * scale
            k = _rope(k_ref[rows, :], cs, sn)
            vb = v_ref[rows, :].astype(BF)
            sc = _dot_nt(q.astype(BF), k.astype(BF)) * dec
            prev = state[...]
            st_ref[ci] = prev
            o = _dot(sc.astype(BF), vb) + _dot((q * qd_ref[...]).astype(BF), prev.astype(BF))
            state[...] = gc_ref[...] * prev + _dot_tn((k * kd_ref[...]).astype(BF), vb)
            o_ref[rows, :] = o
            mu = jnp.mean(o, axis=-1, keepdims=True)
            cen = o - mu
            xhat = cen * lax.rsqrt(jnp.mean(cen * cen, axis=-1, keepdims=True) + EPS)
            gt = gate_ref[rows, :]
            m_ref[rows, :] = (xhat * gn_ref[...] * (gt * _sigmoid(gt))).astype(BF)

    nh = RET_HEADS
    comp = lambda j: pl.BlockSpec((tt, hd), lambda h, t, j=j: (t, j * nh + h))
    tab = pl.BlockSpec((tt, hd // 2), lambda h, t: (t, 0))
    per_head = lambda r: pl.BlockSpec((None, r, hd), lambda h, t: (h, 0, 0))
    return _launch(
        body, name=name, grid=(nh, s // tt),
        in_specs=[comp(0), comp(1), comp(2), comp(3), tab, tab,
                  pl.BlockSpec((None, c, c), lambda h, t: (h, 0, 0)), per_head(c), per_head(c), per_head(1),
                  pl.BlockSpec((1, hd), lambda h, t: (0, h))],
        out_specs=[pl.BlockSpec((tt, hd), lambda h, t: (t, h)), pl.BlockSpec((tt, hd), lambda h, t: (t, h)),
                   pl.BlockSpec((None, nc, hd, hd), lambda h, t: (h, t, 0, 0))],
        out_shape=[jax.ShapeDtypeStruct((s, w), F32), jax.ShapeDtypeStruct((s, w), BF),
                   jax.ShapeDtypeStruct((nh, s // c, hd, hd), F32)],
        scratch_shapes=[pltpu.VMEM((hd, hd), F32)],
        sem=("parallel", "arbitrary"),
        args=(proj, proj, proj, proj, cos, sin, decay, qdec, kdec, gch, gret), comm=comm)


def _ret_bwd_call(proj, cos, sin, consts, gret, o_raw, states, dmerged, name):
    s = proj.shape[0]
    w = proj.shape[1] // 8
    hd = w // RET_HEADS
    c = RET_CHUNK
    tt = _tile(s, 512, c)
    nc = tt // c
    nt = s // tt
    decay, qdec, kdec, gch = consts
    scale = hd ** -0.5

    def body(q_ref, k_ref, v_ref, gate_ref, cos_ref, sin_ref, dec_ref, qd_ref, kd_ref, gc_ref, gn_ref,
             o_ref, st_ref, dm_ref, dp_ref, dgn_ref, dstate):
        @pl.when(pl.program_id(1) == 0)
        def _():
            dstate[...] = jnp.zeros_like(dstate)
            dgn_ref[...] = jnp.zeros_like(dgn_ref)

        dec = dec_ref[...]
        gn = gn_ref[...]
        for ci in reversed(range(nc)):
            rows = slice(ci * c, (ci + 1) * c)
            cs, sn = cos_ref[rows, :], sin_ref[rows, :]
            q = _rope(q_ref[rows, :], cs, sn) * scale
            k = _rope(k_ref[rows, :], cs, sn)
            qb, kb = q.astype(BF), k.astype(BF)
            vb = v_ref[rows, :].astype(BF)
            sc = _dot_nt(qb, kb) * dec
            o = o_ref[rows, :]
            mu = jnp.mean(o, axis=-1, keepdims=True)
            cen = o - mu
            rstd = lax.rsqrt(jnp.mean(cen * cen, axis=-1, keepdims=True) + EPS)
            xhat = cen * rstd
            gt = gate_ref[rows, :]
            sig = _sigmoid(gt)
            sg = gt * sig
            dm = dm_ref[rows, :]
            dgn_ref[...] += jnp.sum(dm * xhat * sg, axis=0, keepdims=True)
            dp_ref[3, rows, :] = (dm * xhat * gn * sig * (1.0 + gt * (1.0 - sig))).astype(BF)
            dxhat = dm * gn * sg
            do = rstd * (dxhat - jnp.mean(dxhat, axis=-1, keepdims=True)
                         - xhat * jnp.mean(dxhat * xhat, axis=-1, keepdims=True))
            dob = do.astype(BF)
            prev = st_ref[ci]
            ds = dstate[...]
            dsb = ds.astype(BF)
            dsc = (_dot_nt(dob, vb) * dec).astype(BF)
            dq = _dot(dsc, kb) + _dot_nt(dob, prev.astype(BF)) * qd_ref[...]
            dk = _dot_tn(dsc, qb) + _dot_nt(vb, dsb) * kd_ref[...]
            dv = _dot_tn(sc.astype(BF), dob) + _dot((k * kd_ref[...]).astype(BF), dsb)
            dstate[...] = gc_ref[...] * ds + _dot_tn((q * qd_ref[...]).astype(BF), dob)
            dp_ref[0, rows, :] = _rope_bwd(dq * scale, cs, sn).astype(BF)
            dp_ref[1, rows, :] = _rope_bwd(dk, cs, sn).astype(BF)
            dp_ref[2, rows, :] = dv.astype(BF)

    nh = RET_HEADS
    rev = lambda t: nt - 1 - t
    comp = lambda j: pl.BlockSpec((tt, hd), lambda h, t, j=j: (rev(t), j * nh + h))
    tab = pl.BlockSpec((tt, hd // 2), lambda h, t: (rev(t), 0))
    per_head = lambda r: pl.BlockSpec((None, r, hd), lambda h, t: (h, 0, 0))
    head_cols = pl.BlockSpec((tt, hd), lambda h, t: (rev(t), h))
    gvec = pl.BlockSpec((1, hd), lambda h, t: (0, h))
    act = jax.ShapeDtypeStruct((s, w), BF)
    return pl.pallas_call(
        body, name=name, grid=(nh, nt),
        in_specs=[comp(0), comp(1), comp(2), comp(3), tab, tab,
                  pl.BlockSpec((None, c, c), lambda h, t: (h, 0, 0)), per_head(c), per_head(c), per_head(1), gvec,
                  head_cols, pl.BlockSpec((None, nc, hd, hd), lambda h, t: (h, rev(t), 0, 0)), head_cols],
        out_specs=[pl.BlockSpec((4, tt, hd), lambda h, t: (0, rev(t), h)), gvec],
        out_shape=[jax.ShapeDtypeStruct((8, s, w), BF), jax.ShapeDtypeStruct((1, w), F32)],
        scratch_shapes=[pltpu.VMEM((hd, hd), F32)],
        compiler_params=_params("parallel", "arbitrary"),
    )(proj, proj, proj, proj, cos, sin, decay, qdec, kdec, gch, gret, o_raw, states, dmerged)


def _block_tri(n, bs, upper):
    r = jnp.arange(n)[:, None]
    cidx = jnp.arange(n)[None, :]
    same = (r // bs) == (cidx // bs)
    return jnp.where(same & ((cidx >= r) if upper else (cidx <= r)), 1.0, 0.0).astype(BF)


def _dot_exact(tri, v):
    hi = v.astype(BF)
    rest = v - hi.astype(F32)
    mid = rest.astype(BF)
    lo = (rest - mid.astype(F32)).astype(BF)
    return _dot(tri, hi) + _dot(tri, mid) + _dot(tri, lo)


def _hgrn_gates(z, lbv):
    sz = _sigmoid(z)
    oml = 1.0 - lbv
    f = lbv + oml * sz
    key = oml * (1.0 - sz)
    return sz, f, key


def _hgrn_fwd_call(proj, lb_logits, ghg, name, comm=None):
    s = proj.shape[0]
    w = proj.shape[1] // 8
    nh = HGRN_HEADS
    hd = w // nh
    bs = HGRN_BLOCK
    tt = _tile(s, 256, bs)
    nb = tt // bs

    def body(q_ref, z_ref, v_ref, gate_ref, lb_ref, gn_ref, tril_ref, o_ref, m_ref, st_ref, state, upd):
        @pl.when(pl.program_id(1) == 0)
        def _():
            state[...] = jnp.zeros_like(state)

        lbv = _sigmoid(lb_ref[...])
        _, f, key = _hgrn_gates(z_ref[...], lbv)
        qr = q_ref[...]
        q = qr * _sigmoid(qr)
        v = v_ref[...]
        g = _dot_exact(tril_ref[...], jnp.log(f))
        blocks = lambda t: t.reshape(nb, bs, hd)
        g3, q3, k3, v3 = blocks(g), blocks(q), blocks(key), blocks(v)
        glast3 = g3[:, bs - 1:bs, :]
        row_id = lax.broadcasted_iota(jnp.int32, (nb, bs, hd), 1)
        o3 = jnp.zeros((nb, bs, hd), F32)
        for j in range(bs):
            wj = jnp.where(row_id >= j, jnp.exp(jnp.minimum(g3 - g3[:, j:j + 1, :], 0.0)), 0.0)
            a = jnp.sum(q3 * k3[:, j:j + 1, :] * wj, axis=-1, keepdims=True)
            o3 = o3 + a * v3[:, j:j + 1, :]
        ktb = (k3 * jnp.exp(glast3 - g3)).reshape(tt, hd).astype(BF)
        vb = v.astype(BF)
        for b in range(nb):
            rows = slice(b * bs, (b + 1) * bs)
            upd[b] = _dot_tn(vb[rows, :], ktb[rows, :])
        egl3 = jnp.exp(glast3)
        st = state[...]
        for b in range(nb):
            st_ref[b] = st
            st = st * egl3[b] + upd[b]
        state[...] = st
        qgb = (q * jnp.exp(g)).astype(BF)
        o_intra = o3.reshape(tt, hd)
        gn = gn_ref[...]
        for b in range(nb):
            rows = slice(b * bs, (b + 1) * bs)
            o = o_intra[rows, :] + _dot_nt(qgb[rows, :], st_ref[b].astype(BF))
            o_ref[rows, :] = o
            gt = gate_ref[rows, :]
            xhat = o * lax.rsqrt(jnp.mean(o * o, axis=-1, keepdims=True) + EPS)
            m_ref[rows, :] = (xhat * gn * (gt * _sigmoid(gt))).astype(BF)

    comp = lambda j: pl.BlockSpec((tt, hd), lambda h, t, j=j: (t, j * nh + h))
    gvec = pl.BlockSpec((1, hd), lambda h, t: (0, h))
    head_cols = pl.BlockSpec((tt, hd), lambda h, t: (t, h))
    return _launch(
        body, name=name, grid=(nh, s // tt),
        in_specs=[comp(4), comp(5), comp(6), comp(7), gvec, gvec, pl.BlockSpec((tt, tt), lambda h, t: (0, 0))],
        out_specs=[head_cols, head_cols, pl.BlockSpec((None, nb, hd, hd), lambda h, t: (h, t, 0, 0))],
        out_shape=[jax.ShapeDtypeStruct((s, w), F32), jax.ShapeDtypeStruct((s, w), BF),
                   jax.ShapeDtypeStruct((nh, s // bs, hd, hd), F32)],
        scratch_shapes=[pltpu.VMEM((hd, hd), F32), pltpu.VMEM((nb, hd, hd), F32)],
        sem=("parallel", "arbitrary"),
        args=(proj, proj, proj, proj, lb_logits, ghg, _block_tri(tt, bs, upper=False)), comm=comm)


def _hgrn_bwd_call(proj, lb_logits, ghg, o_raw, states, dmerged, stack, name, comm=None):
    s = proj.shape[0]
    w = proj.shape[1] // 8
    nh = HGRN_HEADS
    hd = w // nh
    bs = HGRN_BLOCK
    tt = _tile(s, 256, bs)
    nb = tt // bs
    nt = s // tt

    def body(q_ref, z_ref, v_ref, gate_ref, lb_ref, gn_ref, tril_ref, triu_ref, o_ref, st_ref, dm_ref, stack_ref,
             dp_ref, dlb_ref, dgn_ref,
             dstate, ds_all, inc, dq_s, dk_s, dv_s, dgl_s, dk_rows, dv_rows):
        @pl.when(pl.program_id(1) == 0)
        def _():
            dstate[...] = jnp.zeros_like(dstate)
            dlb_ref[...] = jnp.zeros_like(dlb_ref)
            dgn_ref[...] = jnp.zeros_like(dgn_ref)

        lbv = _sigmoid(lb_ref[...])
        oml = 1.0 - lbv
        gn = gn_ref[...]
        sz, f, key = _hgrn_gates(z_ref[...], lbv)
        qr = q_ref[...]
        sq = _sigmoid(qr)
        q = qr * sq
        v = v_ref[...]
        g = _dot_exact(tril_ref[...], jnp.log(f))
        eg = jnp.exp(g)
        blocks = lambda t: t.reshape(nb, bs, hd)
        g3, q3, k3, v3 = blocks(g), blocks(q), blocks(key), blocks(v)
        glast3 = g3[:, bs - 1:bs, :]
        egl3 = jnp.exp(glast3)
        ktail3 = jnp.exp(glast3 - g3)
        o = o_ref[...]
        rstd = lax.rsqrt(jnp.mean(o * o, axis=-1, keepdims=True) + EPS)
        xhat = o * rstd
        gt = gate_ref[...]
        sig = _sigmoid(gt)
        sg = gt * sig
        dm = dm_ref[...]
        dgn_ref[...] += jnp.sum(dm * xhat * sg, axis=0, keepdims=True)
        del stack_ref
        dp_ref[3] = (dm * xhat * gn * sig * (1.0 + gt * (1.0 - sig))).astype(BF)
        dxhat = dm * gn * sg
        do = rstd * (dxhat - xhat * jnp.mean(dxhat * xhat, axis=-1, keepdims=True))
        dob = do.astype(BF)
        do3 = blocks(do)
        qgb = (q * eg).astype(BF)
        for b in range(nb):
            rows = slice(b * bs, (b + 1) * bs)
            inc[b] = _dot_tn(dob[rows, :], qgb[rows, :])
        ds = dstate[...]
        for b in reversed(range(nb)):
            ds_all[b] = ds
            ds = ds * egl3[b] + inc[b]
        dstate[...] = ds
        ktb = (k3 * ktail3).reshape(tt, hd).astype(BF)
        vb = v.astype(BF)
        for b in range(nb):
            rows = slice(b * bs, (b + 1) * bs)
            prev = st_ref[b]
            dsb = ds_all[b]
            dsbb = dsb.astype(BF)
            dq_s[rows, :] = _dot(dob[rows, :], prev.astype(BF))
            dk_s[rows, :] = _dot(vb[rows, :], dsbb)
            dv_s[rows, :] = _dot_nt(ktb[rows, :], dsbb)
            dgl_s[b] = jnp.sum(prev * dsb, axis=0, keepdims=True)
        dq3 = blocks(dq_s[...] * eg)
        dk3 = blocks(dk_s[...]) * ktail3
        dg_last3 = jnp.sum(k3 * dk3, axis=1, keepdims=True) + egl3 * dgl_s[...]
        row_id = lax.broadcasted_iota(jnp.int32, (nb, bs, hd), 1)
        for j in range(bs):
            wj = jnp.where(row_id >= j, jnp.exp(jnp.minimum(g3 - g3[:, j:j + 1, :], 0.0)), 0.0)
            kj = k3[:, j:j + 1, :]
            a = jnp.sum(q3 * kj * wj, axis=-1, keepdims=True)
            da = jnp.sum(do3 * v3[:, j:j + 1, :], axis=-1, keepdims=True)
            dv_rows[:, j:j + 1, :] = jnp.sum(a * do3, axis=1, keepdims=True)
            dq3 = dq3 + da * kj * wj
            dk_rows[:, j:j + 1, :] = jnp.sum(da * q3 * wj, axis=1, keepdims=True)
        dk3 = dk3 + dk_rows[...]
        dv = dv_s[...] + dv_rows[...].reshape(tt, hd)
        dg3 = q3 * dq3 - k3 * dk3 + jnp.where(row_id == bs - 1, dg_last3, 0.0)
        dlf = _dot_exact(triu_ref[...], dg3.reshape(tt, hd))
        dk = dk3.reshape(tt, hd)
        dfk = dlf / f - dk
        dlb_ref[...] += jnp.sum(dfk * (1.0 - sz), axis=0, keepdims=True) * (lbv * oml)
        dp_ref[1] = (dfk * oml * sz * (1.0 - sz)).astype(BF)
        dp_ref[0] = (dq3.reshape(tt, hd) * sq * (1.0 + qr * (1.0 - sq))).astype(BF)
        dp_ref[2] = dv.astype(BF)

    rev = lambda t: nt - 1 - t
    comp = lambda j: pl.BlockSpec((tt, hd), lambda h, t, j=j: (rev(t), j * nh + h))
    gvec = pl.BlockSpec((1, hd), lambda h, t: (0, h))
    head_cols = pl.BlockSpec((tt, hd), lambda h, t: (rev(t), h))
    tri = pl.BlockSpec((tt, tt), lambda h, t: (0, 0))
    act = jax.ShapeDtypeStruct((s, w), BF)
    vec = jax.ShapeDtypeStruct((1, w), F32)
    tile_f32 = pltpu.VMEM((tt, hd), F32)
    return _launch(
        body, name=name, grid=(nh, nt),
        in_specs=[comp(4), comp(5), comp(6), comp(7), gvec, gvec, tri, tri, head_cols,
                  pl.BlockSpec((None, nb, hd, hd), lambda h, t: (h, rev(t), 0, 0)),
                  pl.BlockSpec((tt, hd), lambda h, t: (rev(t), nh + h)),
                  pl.BlockSpec(memory_space=pl.ANY)],
        out_specs=[pl.BlockSpec((4, tt, hd), lambda h, t: (1, rev(t), h)), gvec, gvec],
        out_shape=[jax.ShapeDtypeStruct(stack.shape, stack.dtype), vec, vec],
        scratch_shapes=[pltpu.VMEM((hd, hd), F32), pltpu.VMEM((nb, hd, hd), F32), pltpu.VMEM((nb, hd, hd), F32),
                        tile_f32, tile_f32, tile_f32, pltpu.VMEM((nb, 1, hd), F32),
                        pltpu.VMEM((nb, bs, hd), F32), pltpu.VMEM((nb, bs, hd), F32)],
        sem=("parallel", "arbitrary"),
        args=(proj, proj, proj, proj, lb_logits, ghg, _block_tri(tt, bs, upper=False), _block_tri(tt, bs, upper=True),
              o_raw, states, dmerged, stack), aliases={11: 0}, comm=comm)


def _position():
    return lax.axis_index("x"), lax.axis_index("y"), lax.axis_index("c")


def _all_gather_call(shards, name):
    n = len(shards)

    def body(*refs):
        ins, outs = refs[:n], refs[n:2 * n]
        send_sems, recv_sems, local_sems = refs[2 * n:]
        x, y, c = _position()
        me, sibling = (x, y, c), (x, y, 1 - c)
        chips = [(1 - x, y), (x, 1 - y), (1 - x, 1 - y)]

        def slot(a, p):
            return outs[a].at[4 * p[0] + 2 * p[1] + p[2]]

        def copy(a, k, block, to, src=None):
            return pltpu.make_async_remote_copy(
                src_ref=slot(a, block) if src is None else src, dst_ref=slot(a, block),
                send_sem=send_sems.at[a * 7 + k], recv_sem=recv_sems.at[a * 7 + k],
                device_id=to, device_id_type=MESH)

        mine = [pltpu.make_async_copy(ins[a], slot(a, me), local_sems.at[a]) for a in range(n)]
        for cp in mine:
            cp.start()
        first = []
        for a in range(n):
            first.append(copy(a, 0, me, sibling, src=ins[a]))
            first += [copy(a, 1 + j, me, (*chip, c), src=ins[a]) for j, chip in enumerate(chips)]
        for cp in first:
            cp.start()
        passed = []
        for j, chip in enumerate(chips):
            for a in range(n):
                copy(a, 1 + j, (*chip, c), me).wait_recv()
                fwd = copy(a, 4 + j, (*chip, c), sibling)
                fwd.start()
                passed.append(fwd)
        for a in range(n):
            copy(a, 0, sibling, me).wait_recv()
            for j, chip in enumerate(chips):
                copy(a, 4 + j, (*chip, 1 - c), me).wait_recv()
        for cp in first + passed:
            cp.wait_send()
        for cp in mine:
            cp.wait()

    return pl.pallas_call(
        body, name=name,
        in_specs=[HBM_SPEC] * n, out_specs=[HBM_SPEC] * n,
        out_shape=[jax.ShapeDtypeStruct((N_DEV,) + t.shape, t.dtype) for t in shards],
        scratch_shapes=[pltpu.SemaphoreType.DMA((7 * n,)), pltpu.SemaphoreType.DMA((7 * n,)),
                        pltpu.SemaphoreType.DMA((n,))],
    )(*shards)


def _slot(ref, p):
    return ref.at[4 * p[0] + 2 * p[1] + p[2]]


def _gather_round1(shards):
    n = len(shards)

    def plan(ins, outs, send_sems, recv_sems):
        x, y, c = _position()
        me = (x, y, c)
        peers = [(x, y, 1 - c), (1 - x, y, c), (x, 1 - y, c), (1 - x, 1 - y, c)]
        sends, recvs, local = [], [], []
        for a in range(n):
            local.append(pltpu.make_async_copy(ins[a], _slot(outs[a], me), send_sems.at[4 * n + a]))
            for k, peer in enumerate(peers):
                sems = dict(send_sem=send_sems.at[4 * a + k], recv_sem=recv_sems.at[4 * a + k],
                            device_id=peer, device_id_type=MESH)
                sends.append(pltpu.make_async_remote_copy(src_ref=ins[a], dst_ref=_slot(outs[a], me), **sems))
                recvs.append(pltpu.make_async_remote_copy(src_ref=ins[a], dst_ref=_slot(outs[a], peer), **sems))
        return sends, recvs, local

    def start(*refs):
        sends, _, local = plan(*refs)
        for cp in local + sends:
            cp.start()

    def finish(*refs):
        sends, recvs, local = plan(*refs)
        for cp in recvs:
            cp.wait_recv()
        for cp in sends:
            cp.wait_send()
        for cp in local:
            cp.wait()

    return _Comm(shards, [jax.ShapeDtypeStruct((N_DEV,) + t.shape, t.dtype) for t in shards], 5 * n, start, finish)


def _gather_round2(gathered):
    n = len(gathered)

    def plan(ins, outs, send_sems, recv_sems):
        x, y, c = _position()
        chips = [(1 - x, y), (x, 1 - y), (1 - x, 1 - y)]
        sends, recvs = [], []
        for a in range(n):
            for k, chip in enumerate(chips):
                sems = dict(send_sem=send_sems.at[3 * a + k], recv_sem=recv_sems.at[3 * a + k],
                            device_id=(x, y, 1 - c), device_id_type=MESH)
                sends.append(pltpu.make_async_remote_copy(
                    src_ref=_slot(ins[a], (*chip, c)), dst_ref=_slot(outs[a], (*chip, c)), **sems))
                recvs.append(pltpu.make_async_remote_copy(
                    src_ref=_slot(ins[a], (*chip, c)), dst_ref=_slot(outs[a], (*chip, 1 - c)), **sems))
        return sends, recvs

    def start(*refs):
        for cp in plan(*refs)[0]:
            cp.start()

    def finish(*refs):
        sends, recvs = plan(*refs)
        for cp in recvs:
            cp.wait_recv()
        for cp in sends:
            cp.wait_send()

    return _Comm(gathered, [jax.ShapeDtypeStruct(t.shape, t.dtype) for t in gathered], 3 * n, start, finish,
                 aliases={a: a for a in range(n)})


def _gather_two_level(shards, forward_at):
    n = len(shards)
    first, second = _gather_round1(shards), _gather_round2(shards)

    def middle(ins, outs, send_sems, recv_sems):
        first.finish(ins, outs, send_sems, recv_sems)
        second.start(outs, outs, _SemWindow(send_sems, first.n_sems), _SemWindow(recv_sems, first.n_sems))

    def finish(ins, outs, send_sems, recv_sems):
        second.finish(outs, outs, _SemWindow(send_sems, first.n_sems), _SemWindow(recv_sems, first.n_sems))

    return _Comm(shards, first.out_shape, first.n_sems + second.n_sems, first.start, finish,
                 middle=middle, middle_at=forward_at)


def _sibling_exchange(grads):
    n = len(grads)

    def plan(ins, outs, send_sems, recv_sems):
        x, y, c = _position()
        return [pltpu.make_async_remote_copy(
            src_ref=ins[a].at[2 * q + (1 - c)], dst_ref=outs[a].at[q],
            send_sem=send_sems.at[a * 4 + q], recv_sem=recv_sems.at[a * 4 + q],
            device_id=(x, y, 1 - c), device_id_type=MESH) for a in range(n) for q in range(4)]

    def start(*refs):
        for cp in plan(*refs):
            cp.start()

    def finish(*refs):
        for cp in plan(*refs):
            cp.wait()

    return _Comm(grads, [jax.ShapeDtypeStruct((4,) + t.shape[1:], t.dtype) for t in grads], 4 * n, start, finish)


def _chip_exchange(partials):
    n = len(partials)

    def plan(ins, outs, send_sems, recv_sems):
        x, y, c = _position()
        chips = [(1 - x, y), (x, 1 - y), (1 - x, 1 - y)]
        return [pltpu.make_async_remote_copy(
            src_ref=ins[a].at[2 * chip[0] + chip[1]], dst_ref=outs[a].at[k],
            send_sem=send_sems.at[a * 3 + k], recv_sem=recv_sems.at[a * 3 + k],
            device_id=(*chip, c), device_id_type=MESH) for a in range(n) for k, chip in enumerate(chips)]

    def start(*refs):
        for cp in plan(*refs):
            cp.start()

    def finish(*refs):
        for cp in plan(*refs):
            cp.wait()

    return _Comm(partials, [jax.ShapeDtypeStruct((3,) + t.shape[1:], t.dtype) for t in partials], 3 * n, start, finish)


class _SemWindow:
    def __init__(self, sems, offset):
        self._sems, self._offset = sems, offset

    @property
    def at(self):
        return self

    def __getitem__(self, i):
        return self._sems.at[self._offset + i]


def _join(parts):
    def each(fn_name, cins, couts, send_sems, recv_sems):
        i = o = sem = 0
        for p in parts:
            ni, no = len(p.operands), len(p.out_shape)
            getattr(p, fn_name)(cins[i:i + ni], couts[o:o + no], _SemWindow(send_sems, sem), _SemWindow(recv_sems, sem))
            i, o, sem = i + ni, o + no, sem + p.n_sems

    assert not any(p.aliases for p in parts)
    return _Comm([t for p in parts for t in p.operands], [t for p in parts for t in p.out_shape],
                 sum(p.n_sems for p in parts), functools.partial(each, "start"), functools.partial(each, "finish"))


def _pair_sum_call(grad, recv, parity, name):
    _, r, ccols = grad.shape
    tr = _tile(r, 256)

    def body(par_ref, g_ref, r_ref, p_ref, pb_ref):
        del par_ref
        p = g_ref[...] + r_ref[...]
        p_ref[...] = p
        pb_ref[...] = p.astype(BF)

    blk = lambda fn: pl.BlockSpec((None, tr, ccols), fn)
    return pl.pallas_call(
        body, name=name,
        grid_spec=pltpu.PrefetchScalarGridSpec(
            num_scalar_prefetch=1, grid=(4, r // tr),
            in_specs=[blk(lambda q, i, par: (2 * q + par[0], i, 0)), blk(lambda q, i, par: (q, i, 0))],
            out_specs=[blk(lambda q, i, par: (q, i, 0)), blk(lambda q, i, par: (q, i, 0))]),
        out_shape=[jax.ShapeDtypeStruct((4, r, ccols), F32), jax.ShapeDtypeStruct((4, r, ccols), BF)],
        compiler_params=_params("parallel", "parallel"),
    )(parity, grad, recv)


def _adamw_math(w, g, m, v):
    m = ADAM_B1 * m + (1.0 - ADAM_B1) * g
    v = ADAM_B2 * v + (1.0 - ADAM_B2) * (g * g)
    m_hat = m / (1.0 - ADAM_B1 ** ADAM_STEP)
    v_hat = v / (1.0 - ADAM_B2 ** ADAM_STEP)
    delta = -ADAM_LR * (m_hat / (jnp.sqrt(v_hat) + ADAM_EPS) + ADAM_WD * w)
    return delta, m, v


def _adamw_matrix_call(partial, recv, chip, w, m, v, name):
    _, r, ccols = w.shape
    gcols = partial.shape[2]
    tr = _tile(r, 256)

    def body(chip_ref, p_ref, r_ref, w_ref, m_ref, v_ref, g_out, d_out, m_out, v_out):
        del chip_ref
        cols = pl.ds(0, ccols)
        g = (p_ref[:, cols] + r_ref[0, :, cols].astype(F32) + r_ref[1, :, cols].astype(F32)
             + r_ref[2, :, cols].astype(F32))
        delta, mn, vn = _adamw_math(w_ref[...], g, m_ref[...], v_ref[...])
        g_out[...] = g
        d_out[...] = delta
        m_out[...] = mn
        v_out[...] = vn

    mat = pl.BlockSpec((None, tr, ccols), lambda i, ch: (0, i, 0))
    shp = jax.ShapeDtypeStruct((1, r, ccols), F32)
    return pl.pallas_call(
        body, name=name,
        grid_spec=pltpu.PrefetchScalarGridSpec(
            num_scalar_prefetch=1, grid=(r // tr,),
            in_specs=[pl.BlockSpec((None, tr, gcols), lambda i, ch: (ch[0], i, 0)),
                      pl.BlockSpec((3, tr, gcols), lambda i, ch: (0, i, 0)), mat, mat, mat],
            out_specs=[mat, mat, mat, mat]),
        out_shape=[shp, shp, shp, shp],
        compiler_params=_params("parallel"),
    )(chip, partial, recv, w, m, v)


def _adamw_vector_call(gathered, w, m, v, name):
    n = w.shape[1]

    def body(p_ref, w_ref, m_ref, v_ref, g_out, d_out, m_out, v_out):
        g = p_ref[0:1, :]
        for k in range(1, N_DEV):
            g = g + p_ref[k:k + 1, :]
        delta, mn, vn = _adamw_math(w_ref[...], g, m_ref[...], v_ref[...])
        g_out[...] = g
        d_out[...] = delta
        m_out[...] = mn
        v_out[...] = vn

    shp = jax.ShapeDtypeStruct((1, n), F32)
    return pl.pallas_call(body, name=name, out_shape=[shp, shp, shp, shp])(gathered, w, m, v)


def _round_up(n, mult):
    return (n + mult - 1) // mult * mult


def kernel(x, ffn1_norm, ffn1_w_gate, ffn1_w_up, ffn1_w_down, mix_norm, w_in, ret_norm_g, hgrn_lb_logits, hgrn_norm_g, w_out, ffn2_norm, ffn2_w_gate, ffn2_w_up, ffn2_w_down, final_norm, loss_target, m_ffn1_norm, m_ffn1_w_gate, m_ffn1_w_up, m_ffn1_w_down, m_mix_norm, m_w_in, m_ret_norm_g, m_hgrn_lb_logits, m_hgrn_norm_g, m_w_out, m_ffn2_norm, m_ffn2_w_gate, m_ffn2_w_up, m_ffn2_w_down, m_final_norm, v_ffn1_norm, v_ffn1_w_gate, v_ffn1_w_up, v_ffn1_w_down, v_mix_norm, v_w_in, v_ret_norm_g, v_hgrn_lb_logits, v_hgrn_norm_g, v_w_out, v_ffn2_norm, v_ffn2_w_gate, v_ffn2_w_up, v_ffn2_w_down, v_final_norm):
    xs = x[0]
    target = loss_target[0]
    s, d = xs.shape
    f_loc = ffn1_w_gate.shape[2]
    fp = _round_up(f_loc, LANE)
    pad_cols = lambda t: jnp.pad(t[0], ((0, 0), (0, fp - f_loc)))
    pad_rows = lambda t: jnp.pad(t[0], ((0, fp - f_loc), (0, 0)))

    mat_names = ["ffn1_w_gate", "ffn1_w_up", "ffn1_w_down", "w_in", "w_out", "ffn2_w_gate", "ffn2_w_up", "ffn2_w_down"]
    mat_pad = [pad_cols, pad_cols, pad_rows, lambda t: t[0], lambda t: t[0], pad_cols, pad_cols, pad_rows]
    mat_w = [ffn1_w_gate, ffn1_w_up, ffn1_w_down, w_in, w_out, ffn2_w_gate, ffn2_w_up, ffn2_w_down]
    mat_m = [m_ffn1_w_gate, m_ffn1_w_up, m_ffn1_w_down, m_w_in, m_w_out, m_ffn2_w_gate, m_ffn2_w_up, m_ffn2_w_down]
    mat_v = [v_ffn1_w_gate, v_ffn1_w_up, v_ffn1_w_down, v_w_in, v_w_out, v_ffn2_w_gate, v_ffn2_w_up, v_ffn2_w_down]

    cx, cy, cc = _position()
    parity = jnp.reshape(cc, (1,)).astype(jnp.int32)
    chip = jnp.reshape(2 * cx + cy, (1,)).astype(jnp.int32)
    mat_index = {nm: i for i, nm in enumerate(mat_names)}
    mat_out = {}

    def pair_sums(names, grads, from_sibling):
        return [_pair_sum_call(g, r, parity, "pair_sum_" + nm) for nm, g, r in zip(names, grads, from_sibling)]

    def update(names, sums, from_chips):
        for nm, (p, _), r in zip(names, sums, from_chips):
            i = mat_index[nm]
            mat_out[nm] = _adamw_matrix_call(p, r, chip, mat_w[i], mat_m[i], mat_v[i], "adamw_" + nm)

    shards = [p(t.astype(BF)) for p, t in zip(mat_pad, mat_w)]
    wg1, wu1 = _all_gather_call(shards[:2], "gather_ffn1_up")

    h1 = _rmsnorm_call(xs, ffn1_norm, "ffn1_norm")
    (g1, u1, a1), (wd1, win) = _ffn_up_call(h1, wg1, wu1, "ffn1_up", comm=_gather_two_level(shards[2:4], 0.7))
    (x1, h2), _ = _down_call(a1, wd1, xs, mix_norm, FFN_RESIDUAL_WEIGHT, "ffn1_down")
    (proj,), _ = _proj_call(h2, win, "mix_in")
    wmix = proj.shape[1] // 8
    cos, sin = _rope_tables(s, wmix // RET_HEADS)
    consts = _ret_consts(wmix // RET_HEADS)
    (o_hg, m_hg, st_hg), landed = _hgrn_fwd_call(proj, hgrn_lb_logits, hgrn_norm_g, "hgrn_fwd",
                                                 comm=_gather_round1(shards[4:]))
    (o_ret, m_ret, st_ret), (wout, wg2, wu2, wd2) = _ret_fwd_call(proj, cos, sin, consts, ret_norm_g, "ret_fwd",
                                                                  comm=_gather_round2(landed))
    merged = jnp.concatenate([m_ret, m_hg], axis=1)
    wout_wide = wout.reshape(2, wout.shape[0] * wout.shape[1] // 2, d)
    (x2, h3), _ = _down_call(merged, wout_wide, x1, ffn2_norm, 1.0, "mix_out")
    (g2, u2, a2), _ = _ffn_up_call(h3, wg2, wu2, "ffn2_up")
    (x3,), _ = _down_call(a2, wd2, x2, None, FFN_RESIDUAL_WEIGHT, "ffn2_down")
    loss_part, dx3, dx3b, gv_final = _loss_call(x3, target, final_norm[None, :], "loss_head")

    (dg2, du2), _ = _bwd_up_call(dx3b, wd2, g2, u2, FFN_RESIDUAL_WEIGHT, "ffn2_bwd_up")
    dh3, _ = _bwd_mm_call([(dg2, wg2), (du2, wu2)], "ffn2_bwd_down")
    dx2, dx2b, gv_n3 = _norm_bwd_call(dh3, dx3, x2, ffn2_norm, "ffn2_norm_bwd")
    names_a = ["ffn2_w_gate", "ffn2_w_up", "ffn2_w_down"]
    grads_a = [_wgrad_call(h3, dg2, N_DEV, False, 1.0, "ffn2_wgrad_gate"),
               _wgrad_call(h3, du2, N_DEV, False, 1.0, "ffn2_wgrad_up"),
               _wgrad_call(a2, dx3b, N_DEV, True, FFN_RESIDUAL_WEIGHT, "ffn2_wgrad_down")]

    (dmerged,), sib_a = _proj_call(dx2b, jnp.swapaxes(wout_wide, 1, 2), "mix_out_bwd",
                                   comm=_sibling_exchange(grads_a))
    gm_out = _wgrad_call(merged, dx2b, 2, True, 1.0, "mix_out_wgrad").reshape(wout.shape)
    sums_a = pair_sums(names_a, grads_a, sib_a)
    dproj_half, gv_ret = _ret_bwd_call(proj, cos, sin, consts, ret_norm_g, o_ret, st_ret, dmerged, "ret_bwd")
    (dproj, gv_lb, gv_hg), landed = _hgrn_bwd_call(
        proj, hgrn_lb_logits, hgrn_norm_g, o_hg, st_hg, dmerged, dproj_half, "hgrn_bwd",
        comm=_join([_chip_exchange([pb for _, pb in sums_a]), _sibling_exchange([gm_out])]))
    update(names_a, sums_a, landed[:3])
    sums_out = pair_sums(["w_out"], [gm_out], landed[3:])
    dh2, chips_out = _bwd_mm_call([(dproj, win)], "mix_in_bwd", per_step=2, comm=_chip_exchange([sums_out[0][1]]))
    dx1, dx1b, gv_n2 = _norm_bwd_call(dh2, dx2, x1, mix_norm, "mix_norm_bwd")
    update(["w_out"], sums_out, chips_out)

    gm_in = _wgrad_call(h2, dproj, N_DEV, False, 1.0, "mix_in_wgrad")
    gm_d1, sib_in = _wgrad_call(a1, dx1b, N_DEV, True, FFN_RESIDUAL_WEIGHT, "ffn1_wgrad_down",
                                comm=_sibling_exchange([gm_in]))
    sums_in = pair_sums(["w_in"], [gm_in], sib_in)
    (dg1, du1), landed = _bwd_up_call(dx1b, wd1, g1, u1, FFN_RESIDUAL_WEIGHT, "ffn1_bwd_up",
                                      comm=_join([_chip_exchange([sums_in[0][1]]), _sibling_exchange([gm_d1])]))
    update(["w_in"], sums_in, landed[:1])
    sums_d1 = pair_sums(["ffn1_w_down"], [gm_d1], landed[1:])
    gm_g1, chips_d1 = _wgrad_call(h1, dg1, N_DEV, False, 1.0, "ffn1_wgrad_gate",
                                  comm=_chip_exchange([sums_d1[0][1]]))
    update(["ffn1_w_down"], sums_d1, chips_d1)
    gm_u1, sib_g = _wgrad_call(h1, du1, N_DEV, False, 1.0, "ffn1_wgrad_up", comm=_sibling_exchange([gm_g1]))
    sums_g = pair_sums(["ffn1_w_gate"], [gm_g1], sib_g)
    n_row_tiles = s // _tile(s, 1024)
    assert n_row_tiles >= 2, "the sequence must span at least two row tiles"
    n_first = n_row_tiles // 2
    dh1_part, landed = _bwd_mm_call(
        [(dg1, wg1), (du1, wu1)], "ffn1_bwd_down_a", tiles=(0, n_first),
        comm=_join([_chip_exchange([sums_g[0][1]]), _sibling_exchange([gm_u1])]))
    update(["ffn1_w_gate"], sums_g, landed[:1])
    sums_u = pair_sums(["ffn1_w_up"], [gm_u1], landed[1:])
    dh1, chips_u = _bwd_mm_call(
        [(dg1, wg1), (du1, wu1)], "ffn1_bwd_down_b", tiles=(n_first, n_row_tiles - n_first),
        carry=dh1_part, comm=_chip_exchange([sums_u[0][1]]))
    update(["ffn1_w_up"], sums_u, chips_u)
    dx0, _, gv_n1 = _norm_bwd_call(dh1, dx1, xs, ffn1_norm, "ffn1_norm_bwd")

    vec_names = ["ffn1_norm", "mix_norm", "ret_norm_g", "hgrn_lb_logits", "hgrn_norm_g", "ffn2_norm", "final_norm"]
    vec_g = [gv_n1, gv_n2, gv_ret, gv_lb, gv_hg, gv_n3, gv_final]
    vec_w = [ffn1_norm, mix_norm, ret_norm_g, hgrn_lb_logits, hgrn_norm_g, ffn2_norm, final_norm[None, :]]
    vec_m = [m_ffn1_norm, m_mix_norm, m_ret_norm_g, m_hgrn_lb_logits, m_hgrn_norm_g, m_ffn2_norm, m_final_norm[None, :]]
    vec_v = [v_ffn1_norm, v_mix_norm, v_ret_norm_g, v_hgrn_lb_logits, v_hgrn_norm_g, v_ffn2_norm, v_final_norm[None, :]]
    cat = lambda ts: jnp.concatenate(ts, axis=1)
    (vec_all,) = _all_gather_call([cat(vec_g)], "gather_vector_grads")
    vres = _adamw_vector_call(vec_all[:, 0, :], cat(vec_w), cat(vec_m), cat(vec_v), "adamw_vectors")
    vec_out = {}
    off = 0
    for nm, t in zip(vec_names, vec_w):
        n = t.shape[1]
        parts = [r[:, off:off + n] for r in vres]
        if nm == "final_norm":
            parts = [p[0] for p in parts]
        vec_out[nm] = parts
        off += n

    loss = lax.psum(loss_part[0, 0], ("x", "y", "c"))
    order = ["ffn1_norm", "ffn1_w_gate", "ffn1_w_up", "ffn1_w_down", "mix_norm", "w_in", "ret_norm_g", "hgrn_lb_logits",
             "hgrn_norm_g", "w_out", "ffn2_norm", "ffn2_w_gate", "ffn2_w_up", "ffn2_w_down", "final_norm"]
    res = {**mat_out, **vec_out}
    outs = [loss, dx0[None]]
    for kind in range(4):
        outs += [res[nm][kind] for nm in order]
    return tuple(outs)
```

```python
import functools

import jax
import jax.numpy as jnp
from jax import lax
from jax.experimental import pallas as pl
from jax.experimental.pallas import tpu as pltpu

BF = jnp.bfloat16
F32 = jnp.float32
MESH = pl.DeviceIdType.MESH
HBM_SPEC = pl.BlockSpec(memory_space=pltpu.HBM)

N_DEV = 8
LANE = 128
EPS = 1e-6
ROPE_BASE = 10000.0
RET_HEADS = 4
HGRN_HEADS = 8
RET_CHUNK = 128
HGRN_BLOCK = 16
FFN_RESIDUAL_WEIGHT = 0.5
ADAM_LR = 0.001
ADAM_B1 = 0.9
ADAM_B2 = 0.999
ADAM_EPS = 1e-08
ADAM_WD = 0.01
ADAM_STEP = 10
VMEM_LIMIT = 56 * 1024 * 1024


def _tile(n, pref, mult=8):
    t = min(pref, n)
    t -= t % mult
    while t >= mult:
        if n % t == 0:
            return t
        t -= mult
    return n


def _params(*sem):
    return pltpu.CompilerParams(dimension_semantics=sem, vmem_limit_bytes=VMEM_LIMIT)


class _Comm:
    def __init__(self, operands, out_shape, n_sems, start, finish, aliases=None, middle=None, middle_at=0.0):
        self.operands = list(operands)
        self.out_shape = list(out_shape)
        self.n_sems = n_sems
        self.start = start
        self.finish = finish
        self.aliases = dict(aliases or {})
        self.middle = middle
        self.middle_at = middle_at


def _launch(body, *, name, grid, in_specs, out_specs, out_shape, sem, args, scratch_shapes=(), aliases=None, comm=None):
    in_specs, out_specs, out_shape = list(in_specs), list(out_specs), list(out_shape)
    scratch_shapes = list(scratch_shapes)
    aliases = dict(aliases or {})
    if comm is None:
        res = pl.pallas_call(body, name=name, grid=grid, in_specs=in_specs, out_specs=out_specs, out_shape=out_shape,
                             scratch_shapes=scratch_shapes, input_output_aliases=aliases,
                             compiler_params=_params(*sem))(*args)
        return list(res), []
    n_in, n_out, n_scr = len(in_specs), len(out_specs), len(scratch_shapes)
    ci, co = len(comm.operands), len(comm.out_shape)

    def carrying(*refs):
        bounds = [0, n_in, n_in + ci, n_in + ci + n_out, n_in + ci + n_out + co, n_in + ci + n_out + co + n_scr]
        ins, cins, outs, couts, scr = [refs[a:b] for a, b in zip(bounds[:-1], bounds[1:])]
        send_sems, recv_sems = refs[bounds[-1]:]
        ids = [pl.program_id(k) for k in range(len(grid))]
        first = functools.reduce(jnp.logical_and, [i == 0 for i in ids])
        last = functools.reduce(jnp.logical_and, [i == g - 1 for i, g in zip(ids, grid)])

        @pl.when(first)
        def _():
            comm.start(cins, couts, send_sems, recv_sems)

        if comm.middle is not None:
            step, total = ids[0], grid[0]
            for i, g in zip(ids[1:], grid[1:]):
                step, total = step * g + i, total * g

            @pl.when(step == int(total * comm.middle_at))
            def _():
                comm.middle(cins, couts, send_sems, recv_sems)

        body(*ins, *outs, *scr)

        @pl.when(last)
        def _():
            comm.finish(cins, couts, send_sems, recv_sems)

    res = pl.pallas_call(
        carrying, name=name, grid=grid,
        in_specs=in_specs + [HBM_SPEC] * ci, out_specs=out_specs + [HBM_SPEC] * co,
        out_shape=out_shape + comm.out_shape,
        scratch_shapes=scratch_shapes + [pltpu.SemaphoreType.DMA((comm.n_sems,)), pltpu.SemaphoreType.DMA((comm.n_sems,))],
        input_output_aliases={**aliases, **{n_in + a: n_out + b for a, b in comm.aliases.items()}},
        compiler_params=_params(*(["arbitrary"] * len(grid))),
    )(*args, *comm.operands)
    return list(res[:n_out]), list(res[n_out:])


def _comm_only_call(comm, name):
    def body(*refs):
        ci, co = len(comm.operands), len(comm.out_shape)
        cins, couts = refs[:ci], refs[ci:ci + co]
        send_sems, recv_sems = refs[ci + co:]
        comm.start(cins, couts, send_sems, recv_sems)
        comm.finish(cins, couts, send_sems, recv_sems)

    return pl.pallas_call(
        body, name=name,
        in_specs=[HBM_SPEC] * len(comm.operands), out_specs=[HBM_SPEC] * len(comm.out_shape),
        out_shape=comm.out_shape,
        scratch_shapes=[pltpu.SemaphoreType.DMA((comm.n_sems,)), pltpu.SemaphoreType.DMA((comm.n_sems,))],
        input_output_aliases=comm.aliases,
    )(*comm.operands)


def _sigmoid(v):
    return 0.5 * jnp.tanh(0.5 * v) + 0.5


def _dot(a, b):
    return jnp.dot(a, b, preferred_element_type=F32)


def _dot_nt(a, b):
    return lax.dot_general(a, b, (((1,), (1,)), ((), ())), preferred_element_type=F32)


def _dot_tn(a, b):
    return lax.dot_general(a, b, (((0,), (0,)), ((), ())), preferred_element_type=F32)


def _rmsnorm_call(x, gain, name):
    s, d = x.shape
    tm = _tile(s, 512)

    def body(x_ref, g_ref, o_ref):
        xv = x_ref[...]
        r = lax.rsqrt(jnp.mean(xv * xv, axis=-1, keepdims=True) + EPS)
        o_ref[...] = (xv * r * g_ref[...]).astype(BF)

    return pl.pallas_call(
        body, name=name, grid=(s // tm,),
        in_specs=[pl.BlockSpec((tm, d), lambda i: (i, 0)), pl.BlockSpec((1, d), lambda i: (0, 0))],
        out_specs=pl.BlockSpec((tm, d), lambda i: (i, 0)),
        out_shape=jax.ShapeDtypeStruct((s, d), BF),
        compiler_params=_params("parallel"),
    )(x, gain)


def _ffn_up_call(h, wg, wu, name, comm=None):
    s, d = h.shape
    nj, _, k = wg.shape
    tm = _tile(s, 1024)

    def body(h_ref, wg_ref, wu_ref, g_ref, u_ref, a_ref):
        hv = h_ref[...]
        g = _dot(hv, wg_ref[...])
        u = _dot(hv, wu_ref[...])
        g_ref[...] = g.astype(BF)
        u_ref[...] = u.astype(BF)
        a_ref[...] = (g * _sigmoid(g) * u).astype(BF)

    act = pl.BlockSpec((tm, k), lambda i, j: (i, j))
    wsp = pl.BlockSpec((None, d, k), lambda i, j: (j, 0, 0))
    return _launch(
        body, name=name, grid=(s // tm, nj),
        in_specs=[pl.BlockSpec((tm, d), lambda i, j: (i, 0)), wsp, wsp],
        out_specs=[act, act, act],
        out_shape=[jax.ShapeDtypeStruct((s, nj * k), BF)] * 3,
        sem=("parallel", "arbitrary"), args=(h, wg, wu), comm=comm)


def _proj_call(h, w, name, comm=None):
    s, d = h.shape
    nj, _, k = w.shape
    tm = _tile(s, 1024)

    def body(h_ref, w_ref, o_ref):
        o_ref[...] = _dot(h_ref[...], w_ref[...])

    return _launch(
        body, name=name, grid=(s // tm, nj),
        in_specs=[pl.BlockSpec((tm, d), lambda i, j: (i, 0)), pl.BlockSpec((None, d, k), lambda i, j: (j, 0, 0))],
        out_specs=[pl.BlockSpec((tm, k), lambda i, j: (i, j))],
        out_shape=[jax.ShapeDtypeStruct((s, nj * k), F32)],
        sem=("parallel", "arbitrary"), args=(h, w), comm=comm)


def _down_call(a, w, resid, gain, scale, name, comm=None):
    s = a.shape[0]
    nj, k, d = w.shape
    tm = _tile(s, 1024)
    strip = _tile(tm, 128)
    cn = _tile(d, 512, LANE)
    with_norm = gain is not None

    def body(*refs):
        if with_norm:
            a_ref, w_ref, r_ref, g_ref, x_ref, h_ref = refs
        else:
            a_ref, w_ref, r_ref, x_ref = refs
        j = pl.program_id(1)

        av = a_ref[...]

        @pl.when(j == 0)
        def _():
            for n0 in range(0, d, cn):
                x_ref[:, n0:n0 + cn] = _dot(av, w_ref[:, n0:n0 + cn])

        @pl.when(j > 0)
        def _():
            for n0 in range(0, d, cn):
                x_ref[:, n0:n0 + cn] += _dot(av, w_ref[:, n0:n0 + cn])

        @pl.when(j == nj - 1)
        def _():
            for r0 in range(0, tm, strip):
                rows = slice(r0, r0 + strip)
                xn = r_ref[rows, :] + (scale * x_ref[rows, :])
                x_ref[rows, :] = xn
                if with_norm:
                    r = lax.rsqrt(jnp.mean(xn * xn, axis=-1, keepdims=True) + EPS)
                    h_ref[rows, :] = (xn * r * g_ref[...]).astype(BF)

    row = pl.BlockSpec((tm, d), lambda i, j: (i, 0))
    once = pl.BlockSpec((tm, d), lambda i, j: (i, 0), pipeline_mode=pl.Buffered(1))
    in_specs = [pl.BlockSpec((tm, k), lambda i, j: (i, j)), pl.BlockSpec((None, k, d), lambda i, j: (j, 0, 0)), once]
    args = [a, w, resid]
    out_specs = [row]
    out_shape = [jax.ShapeDtypeStruct((s, d), F32)]
    if with_norm:
        in_specs.append(pl.BlockSpec((1, d), lambda i, j: (0, 0)))
        args.append(gain)
        out_specs.append(row)
        out_shape.append(jax.ShapeDtypeStruct((s, d), BF))
    return _launch(
        body, name=name, grid=(s // tm, nj),
        in_specs=in_specs, out_specs=out_specs, out_shape=out_shape,
        sem=("parallel", "arbitrary"), args=args, comm=comm)


def _loss_call(x, target, gain, name):
    s, d = x.shape
    tm = _tile(s, 512)

    def body(x_ref, t_ref, g_ref, loss_ref, dx_ref, dxb_ref, dg_ref):
        i = pl.program_id(0)

        @pl.when(i == 0)
        def _():
            loss_ref[...] = jnp.zeros_like(loss_ref)
            dg_ref[...] = jnp.zeros_like(dg_ref)

        xv = x_ref[...]
        gv = g_ref[...]
        r = lax.rsqrt(jnp.mean(xv * xv, axis=-1, keepdims=True) + EPS)
        xhat = xv * r
        err = xhat * gv - t_ref[...]
        per_tok = jnp.mean(err * err, axis=-1, keepdims=True)
        loss_ref[...] += 0.5 * jnp.sum(per_tok, axis=0, keepdims=True)
        dout = err * (1.0 / d)
        dg_ref[...] += jnp.sum(dout * xhat, axis=0, keepdims=True)
        dxhat = dout * gv
        dx = r * (dxhat - xhat * jnp.mean(dxhat * xhat, axis=-1, keepdims=True))
        dx_ref[...] = dx
        dxb_ref[...] = dx.astype(BF)

    row = pl.BlockSpec((tm, d), lambda i: (i, 0))
    vec = pl.BlockSpec((1, d), lambda i: (0, 0))
    return pl.pallas_call(
        body, name=name, grid=(s // tm,),
        in_specs=[row, row, vec],
        out_specs=[pl.BlockSpec((1, 1), lambda i: (0, 0)), row, row, vec],
        out_shape=[jax.ShapeDtypeStruct((1, 1), F32), jax.ShapeDtypeStruct((s, d), F32),
                   jax.ShapeDtypeStruct((s, d), BF), jax.ShapeDtypeStruct((1, d), F32)],
        compiler_params=_params("arbitrary"),
    )(x, target, gain)


def _bwd_up_call(dy, wd, g, u, scale, name, comm=None):
    s, d = dy.shape
    nj, k, _ = wd.shape
    tm = _tile(s, 1024)

    def body(dy_ref, w_ref, g_ref, u_ref, dg_ref, du_ref):
        da = scale * _dot_nt(dy_ref[...], w_ref[...])
        gv = g_ref[...].astype(F32)
        sig = _sigmoid(gv)
        du_ref[...] = (da * gv * sig).astype(BF)
        dg_ref[...] = (da * u_ref[...].astype(F32) * sig * (1.0 + gv * (1.0 - sig))).astype(BF)

    act = pl.BlockSpec((tm, k), lambda i, j: (i, j))
    return _launch(
        body, name=name, grid=(s // tm, nj),
        in_specs=[pl.BlockSpec((tm, d), lambda i, j: (i, 0)), pl.BlockSpec((None, k, d), lambda i, j: (j, 0, 0)), act, act],
        out_specs=[act, act],
        out_shape=[jax.ShapeDtypeStruct((s, nj * k), BF), jax.ShapeDtypeStruct((s, nj * k), BF)],
        sem=("parallel", "arbitrary"), args=(dy, wd, g, u), comm=comm)


def _norm_bwd_call(dh, dres, xin, gain, name):
    s, d = xin.shape
    tm = _tile(s, 256)

    def body(dh_ref, dres_ref, x_ref, g_ref, dx_ref, dxb_ref, dg_ref):
        @pl.when(pl.program_id(0) == 0)
        def _():
            dg_ref[...] = jnp.zeros_like(dg_ref)

        xv = x_ref[...]
        r = lax.rsqrt(jnp.mean(xv * xv, axis=-1, keepdims=True) + EPS)
        xhat = xv * r
        dh_v = dh_ref[...]
        dg_ref[...] += jnp.sum(dh_v * xhat, axis=0, keepdims=True)
        dxhat = dh_v * g_ref[...]
        dx = dres_ref[...] + r * (dxhat - xhat * jnp.mean(dxhat * xhat, axis=-1, keepdims=True))
        dx_ref[...] = dx
        dxb_ref[...] = dx.astype(BF)

    row = pl.BlockSpec((tm, d), lambda i: (i, 0))
    vec = pl.BlockSpec((1, d), lambda i: (0, 0))
    return pl.pallas_call(
        body, name=name, grid=(s // tm,),
        in_specs=[row, row, row, vec], out_specs=[row, row, vec],
        out_shape=[jax.ShapeDtypeStruct((s, d), F32), jax.ShapeDtypeStruct((s, d), BF), jax.ShapeDtypeStruct((1, d), F32)],
        compiler_params=_params("arbitrary"),
    )(dh, dres, xin, gain)


def _bwd_mm_call(pairs, name, per_step=1, tiles=None, carry=None, comm=None):
    a0 = pairs[0][0]
    s = a0.shape[-2]
    nblocks, d, k = pairs[0][1].shape
    nj = nblocks // per_step
    npair = len(pairs) * per_step
    tm = _tile(s, 1024)
    cn = _tile(d, 512, LANE)

    first_tile, n_tiles = (0, s // tm) if tiles is None else tiles

    def body(*refs):
        a_refs = refs[0:2 * npair:2]
        w_refs = refs[1:2 * npair:2]
        dh_ref = refs[-1]

        @pl.when(pl.program_id(1) == 0)
        def _():
            dh_ref[...] = jnp.zeros_like(dh_ref)

        for a_ref, w_ref in zip(a_refs, w_refs):
            av = a_ref[...]
            for n0 in range(0, d, cn):
                dh_ref[:, n0:n0 + cn] += _dot_nt(av, w_ref[n0:n0 + cn, :])

    row = pl.BlockSpec((tm, d), lambda i, j: (first_tile + i, 0))
    in_specs, args = [], []
    for a, w in pairs:
        for r in range(per_step):
            in_specs += [pl.BlockSpec((None, tm, k), lambda i, j, r=r: (j * per_step + r, first_tile + i, 0))
                         if a.ndim == 3 else
                         pl.BlockSpec((tm, k), lambda i, j, r=r: (first_tile + i, j * per_step + r)),
                         pl.BlockSpec((None, d, k), lambda i, j, r=r: (j * per_step + r, 0, 0))]
            args += [a, w]
    aliases = {}
    if carry is not None:
        aliases = {len(in_specs): 0}
        in_specs.append(pl.BlockSpec(memory_space=pl.ANY))
        args.append(carry)
    (dh,), landed = _launch(
        body, name=name, grid=(n_tiles, nj),
        in_specs=in_specs, out_specs=[row], out_shape=[jax.ShapeDtypeStruct((s, d), F32)],
        sem=("parallel", "arbitrary"), args=args, aliases=aliases, comm=comm)
    return dh, landed


def _wgrad_call(a, b, nj, a_blocked, scale, name, comm=None):
    s = a.shape[0]
    ka = a.shape[1] // nj if a_blocked else a.shape[1]
    b_stacked = b.ndim == 3
    kb = b.shape[-1] if (a_blocked or b_stacked) else b.shape[1] // nj
    ts = _tile(s, 2048)
    ns = s // ts

    def body(a_ref, b_ref, o_ref):
        t = pl.program_id(1)

        @pl.when(t == 0)
        def _():
            o_ref[...] = jnp.zeros_like(o_ref)

        o_ref[...] += _dot_tn(a_ref[...], b_ref[...])
        if scale != 1.0:
            @pl.when(t == ns - 1)
            def _():
                o_ref[...] = o_ref[...] * scale

    a_spec = pl.BlockSpec((ts, ka), (lambda j, t: (t, j)) if a_blocked else (lambda j, t: (t, 0)))
    if b_stacked:
        b_spec = pl.BlockSpec((None, ts, kb), lambda j, t: (j, t, 0))
    else:
        b_spec = pl.BlockSpec((ts, kb), (lambda j, t: (t, 0)) if a_blocked else (lambda j, t: (t, j)))
    (out,), landed = _launch(
        body, name=name, grid=(nj, ns),
        in_specs=[a_spec, b_spec],
        out_specs=[pl.BlockSpec((None, ka, kb), lambda j, t: (j, 0, 0))],
        out_shape=[jax.ShapeDtypeStruct((nj, ka, kb), F32)],
        sem=("parallel", "arbitrary"), args=(a, b), comm=comm)
    return out if comm is None else (out, landed)


def _rope(v, cos, sin):
    half = v.shape[-1] // 2
    v1, v2 = v[:, :half], v[:, half:]
    return jnp.concatenate([v1 * cos - v2 * sin, v2 * cos + v1 * sin], axis=-1)


def _rope_bwd(dv, cos, sin):
    half = dv.shape[-1] // 2
    d1, d2 = dv[:, :half], dv[:, half:]
    return jnp.concatenate([d1 * cos + d2 * sin, d2 * cos - d1 * sin], axis=-1)


def _ret_consts(hd):
    c = RET_CHUNK
    log_gamma = jnp.log(1.0 - jnp.exp2(-5.0 - jnp.arange(RET_HEADS, dtype=F32)))
    idx = jnp.arange(c, dtype=F32)
    rel = idx[:, None] - idx[None, :]
    mask = rel >= 0
    decay = jnp.where(mask[None], jnp.exp(log_gamma[:, None, None] * jnp.where(mask, rel, 0.0)[None]), 0.0)
    qdec = jnp.exp(log_gamma[:, None] * (idx + 1.0)[None, :])
    kdec = jnp.exp(log_gamma[:, None] * (c - 1.0 - idx)[None, :])
    gchunk = jnp.exp(log_gamma * c)
    bc = lambda t: jnp.broadcast_to(t[:, :, None], (RET_HEADS, t.shape[1], hd))
    return decay, bc(qdec), bc(kdec), bc(gchunk[:, None])


def _rope_tables(s, hd):
    inv = jnp.power(ROPE_BASE, -jnp.arange(0, hd, 2, dtype=F32) / hd)
    ang = jnp.arange(s, dtype=F32)[:, None] * inv[None, :]
    return jnp.cos(ang), jnp.sin(ang)


def _ret_fwd_call(proj, cos, sin, consts, gret, name, comm=None):
    s = proj.shape[0]
    w = proj.shape[1] // 8
    hd = w // RET_HEADS
    c = RET_CHUNK
    tt = _tile(s, 512, c)
    nc = tt // c
    decay, qdec, kdec, gch = consts
    scale = hd ** -0.5

    def body(q_ref, k_ref, v_ref, gate_ref, cos_ref, sin_ref, dec_ref, qd_ref, kd_ref, gc_ref, gn_ref,
             o_ref, m_ref, st_ref, state):
        @pl.when(pl.program_id(1) == 0)
        def _():
            state[...] = jnp.zeros_like(state)

        dec = dec_ref[...]
        for ci in range(nc):
            rows = slice(ci * c, (ci + 1) * c)
            cs, sn = cos_ref[rows, :], sin_ref[rows, :]
            q = _rope(q_ref[rows, :], cs, sn) * scale
            k = _rope(k_ref[rows, :], cs, sn)
            vb = v_ref[rows, :].astype(BF)
            sc = _dot_nt(q.astype(BF), k.astype(BF)) * dec
            prev = state[...]
            st_ref[ci] = prev
            o = _dot(sc.astype(BF), vb) + _dot((q * qd_ref[...]).astype(BF), prev.astype(BF))
            state[...] = gc_ref[...] * prev + _dot_tn((k * kd_ref[...]).astype(BF), vb)
            o_ref[rows, :] = o
            mu = jnp.mean(o, axis=-1, keepdims=True)
            cen = o - mu
            xhat = cen * lax.rsqrt(jnp.mean(cen * cen, axis=-1, keepdims=True) + EPS)
            gt = gate_ref[rows, :]
            m_ref[rows, :] = (xhat * gn_ref[...] * (gt * _sigmoid(gt))).astype(BF)

    nh = RET_HEADS
    comp = lambda j: pl.BlockSpec((tt, hd), lambda h, t, j=j: (t, j * nh + h))
    tab = pl.BlockSpec((tt, hd // 2), lambda h, t: (t, 0))
    per_head = lambda r: pl.BlockSpec((None, r, hd), lambda h, t: (h, 0, 0))
    return _launch(
        body, name=name, grid=(nh, s // tt),
        in_specs=[comp(0), comp(1), comp(2), comp(3), tab, tab,
                  pl.BlockSpec((None, c, c), lambda h, t: (h, 0, 0)), per_head(c), per_head(c), per_head(1),
                  pl.BlockSpec((1, hd), lambda h, t: (0, h))],
        out_specs=[pl.BlockSpec((tt, hd), lambda h, t: (t, h)), pl.BlockSpec((tt, hd), lambda h, t: (t, h)),
                   pl.BlockSpec((None, nc, hd, hd), lambda h, t: (h, t, 0, 0))],
        out_shape=[jax.ShapeDtypeStruct((s, w), F32), jax.ShapeDtypeStruct((s, w), BF),
                   jax.ShapeDtypeStruct((nh, s // c, hd, hd), F32)],
        scratch_shapes=[pltpu.VMEM((hd, hd), F32)],
        sem=("parallel", "arbitrary"),
        args=(proj, proj, proj, proj, cos, sin, decay, qdec, kdec, gch, gret), comm=comm)


def _ret_bwd_call(proj, cos, sin, consts, gret, o_raw, states, dmerged, name):
    s = proj.shape[0]
    w = proj.shape[1] // 8
    hd = w // RET_HEADS
    c = RET_CHUNK
    tt = _tile(s, 512, c)
    nc = tt // c
    nt = s // tt
    decay, qdec, kdec, gch = consts
    scale = hd ** -0.5

    def body(q_ref, k_ref, v_ref, gate_ref, cos_ref, sin_ref, dec_ref, qd_ref, kd_ref, gc_ref, gn_ref,
             o_ref, st_ref, dm_ref, dp_ref, dgn_ref, dstate):
        @pl.when(pl.program_id(1) == 0)
        def _():
            dstate[...] = jnp.zeros_like(dstate)
            dgn_ref[...] = jnp.zeros_like(dgn_ref)

        dec = dec_ref[...]
        gn = gn_ref[...]
        for ci in reversed(range(nc)):
            rows = slice(ci * c, (ci + 1) * c)
            cs, sn = cos_ref[rows, :], sin_ref[rows, :]
            q = _rope(q_ref[rows, :], cs, sn) * scale
            k = _rope(k_ref[rows, :], cs, sn)
            qb, kb = q.astype(BF), k.astype(BF)
            vb = v_ref[rows, :].astype(BF)
            sc = _dot_nt(qb, kb) * dec
            o = o_ref[rows, :]
            mu = jnp.mean(o, axis=-1, keepdims=True)
            cen = o - mu
            rstd = lax.rsqrt(jnp.mean(cen * cen, axis=-1, keepdims=True) + EPS)
            xhat = cen * rstd
            gt = gate_ref[rows, :]
            sig = _sigmoid(gt)
            sg = gt * sig
            dm = dm_ref[rows, :]
            dgn_ref[...] += jnp.sum(dm * xhat * sg, axis=0, keepdims=True)
            dp_ref[3, rows, :] = (dm * xhat * gn * sig * (1.0 + gt * (1.0 - sig))).astype(BF)
            dxhat = dm * gn * sg
            do = rstd * (dxhat - jnp.mean(dxhat, axis=-1, keepdims=True)
                         - xhat * jnp.mean(dxhat * xhat, axis=-1, keepdims=True))
            dob = do.astype(BF)
            prev = st_ref[ci]
            ds = dstate[...]
            dsb = ds.astype(BF)
            dsc = (_dot_nt(dob, vb) * dec).astype(BF)
            dq = _dot(dsc, kb) + _dot_nt(dob, prev.astype(BF)) * qd_ref[...]
            dk = _dot_tn(dsc, qb) + _dot_nt(vb, dsb) * kd_ref[...]
            dv = _dot_tn(sc.astype(BF), dob) + _dot((k * kd_ref[...]).astype(BF), dsb)
            dstate[...] = gc_ref[...] * ds + _dot_tn((q * qd_ref[...]).astype(BF), dob)
            dp_ref[0, rows, :] = _rope_bwd(dq * scale, cs, sn).astype(BF)
            dp_ref[1, rows, :] = _rope_bwd(dk, cs, sn).astype(BF)
            dp_ref[2, rows, :] = dv.astype(BF)

    nh = RET_HEADS
    rev = lambda t: nt - 1 - t
    comp = lambda j: pl.BlockSpec((tt, hd), lambda h, t, j=j: (rev(t), j * nh + h))
    tab = pl.BlockSpec((tt, hd // 2), lambda h, t: (rev(t), 0))
    per_head = lambda r: pl.BlockSpec((None, r, hd), lambda h, t: (h, 0, 0))
    head_cols = pl.BlockSpec((tt, hd), lambda h, t: (rev(t), h))
    gvec = pl.BlockSpec((1, hd), lambda h, t: (0, h))
    act = jax.ShapeDtypeStruct((s, w), BF)
    return pl.pallas_call(
        body, name=name, grid=(nh, nt),
        in_specs=[comp(0), comp(1), comp(2), comp(3), tab, tab,
                  pl.BlockSpec((None, c, c), lambda h, t: (h, 0, 0)), per_head(c), per_head(c), per_head(1), gvec,
                  head_cols, pl.BlockSpec((None, nc, hd, hd), lambda h, t: (h, rev(t), 0, 0)), head_cols],
        out_specs=[pl.BlockSpec((4, tt, hd), lambda h, t: (0, rev(t), h)), gvec],
        out_shape=[jax.ShapeDtypeStruct((8, s, w), BF), jax.ShapeDtypeStruct((1, w), F32)],
        scratch_shapes=[pltpu.VMEM((hd, hd), F32)],
        compiler_params=_params("parallel", "arbitrary"),
    )(proj, proj, proj, proj, cos, sin, decay, qdec, kdec, gch, gret, o_raw, states, dmerged)


def _block_tri(n, bs, upper):
    r = jnp.arange(n)[:, None]
    cidx = jnp.arange(n)[None, :]
    same = (r // bs) == (cidx // bs)
    return jnp.where(same & ((cidx >= r) if upper else (cidx <= r)), 1.0, 0.0).astype(BF)


def _dot_exact(tri, v):
    hi = v.astype(BF)
    rest = v - hi.astype(F32)
    mid = rest.astype(BF)
    lo = (rest - mid.astype(F32)).astype(BF)
    return _dot(tri, hi) + _dot(tri, mid) + _dot(tri, lo)


def _hgrn_gates(z, lbv):
    sz = _sigmoid(z)
    oml = 1.0 - lbv
    f = lbv + oml * sz
    key = oml * (1.0 - sz)
    return sz, f, key


def _hgrn_fwd_call(proj, lb_logits, ghg, name, comm=None):
    s = proj.shape[0]
    w = proj.shape[1] // 8
    nh = HGRN_HEADS
    hd = w // nh
    bs = HGRN_BLOCK
    tt = _tile(s, 256, bs)
    nb = tt // bs

    def body(q_ref, z_ref, v_ref, gate_ref, lb_ref, gn_ref, tril_ref, o_ref, m_ref, st_ref, state, upd):
        @pl.when(pl.program_id(1) == 0)
        def _():
            state[...] = jnp.zeros_like(state)

        lbv = _sigmoid(lb_ref[...])
        _, f, key = _hgrn_gates(z_ref[...], lbv)
        qr = q_ref[...]
        q = qr * _sigmoid(qr)
        v = v_ref[...]
        g = _dot_exact(tril_ref[...], jnp.log(f))
        blocks = lambda t: t.reshape(nb, bs, hd)
        g3, q3, k3, v3 = blocks(g), blocks(q), blocks(key), blocks(v)
        glast3 = g3[:, bs - 1:bs, :]
        row_id = lax.broadcasted_iota(jnp.int32, (nb, bs, hd), 1)
        o3 = jnp.zeros((nb, bs, hd), F32)
        for j in range(bs):
            wj = jnp.where(row_id >= j, jnp.exp(jnp.minimum(g3 - g3[:, j:j + 1, :], 0.0)), 0.0)
            a = jnp.sum(q3 * k3[:, j:j + 1, :] * wj, axis=-1, keepdims=True)
            o3 = o3 + a * v3[:, j:j + 1, :]
        ktb = (k3 * jnp.exp(glast3 - g3)).reshape(tt, hd).astype(BF)
        vb = v.astype(BF)
        for b in range(nb):
            rows = slice(b * bs, (b + 1) * bs)
            upd[b] = _dot_tn(vb[rows, :], ktb[rows, :])
        egl3 = jnp.exp(glast3)
        st = state[...]
        for b in range(nb):
            st_ref[b] = st
            st = st * egl3[b] + upd[b]
        state[...] = st
        qgb = (q * jnp.exp(g)).astype(BF)
        o_intra = o3.reshape(tt, hd)
        gn = gn_ref[...]
        for b in range(nb):
            rows = slice(b * bs, (b + 1) * bs)
            o = o_intra[rows, :] + _dot_nt(qgb[rows, :], st_ref[b].astype(BF))
            o_ref[rows, :] = o
            gt = gate_ref[rows, :]
            xhat = o * lax.rsqrt(jnp.mean(o * o, axis=-1, keepdims=True) + EPS)
            m_ref[rows, :] = (xhat * gn * (gt * _sigmoid(gt))).astype(BF)

    comp = lambda j: pl.BlockSpec((tt, hd), lambda h, t, j=j: (t, j * nh + h))
    gvec = pl.BlockSpec((1, hd), lambda h, t: (0, h))
    head_cols = pl.BlockSpec((tt, hd), lambda h, t: (t, h))
    return _launch(
        body, name=name, grid=(nh, s // tt),
        in_specs=[comp(4), comp(5), comp(6), comp(7), gvec, gvec, pl.BlockSpec((tt, tt), lambda h, t: (0, 0))],
        out_specs=[head_cols, head_cols, pl.BlockSpec((None, nb, hd, hd), lambda h, t: (h, t, 0, 0))],
        out_shape=[jax.ShapeDtypeStruct((s, w), F32), jax.ShapeDtypeStruct((s, w), BF),
                   jax.ShapeDtypeStruct((nh, s // bs, hd, hd), F32)],
        scratch_shapes=[pltpu.VMEM((hd, hd), F32), pltpu.VMEM((nb, hd, hd), F32)],
        sem=("parallel", "arbitrary"),
        args=(proj, proj, proj, proj, lb_logits, ghg, _block_tri(tt, bs, upper=False)), comm=comm)


def _hgrn_bwd_call(proj, lb_logits, ghg, o_raw, states, dmerged, stack, name, comm=None):
    s = proj.shape[0]
    w = proj.shape[1] // 8
    nh = HGRN_HEADS
    hd = w // nh
    bs = HGRN_BLOCK
    tt = _tile(s, 256, bs)
    nb = tt // bs
    nt = s // tt

    def body(q_ref, z_ref, v_ref, gate_ref, lb_ref, gn_ref, tril_ref, triu_ref, o_ref, st_ref, dm_ref, stack_ref,
             dp_ref, dlb_ref, dgn_ref,
             dstate, ds_all, inc, dq_s, dk_s, dv_s, dgl_s, dk_rows, dv_rows):
        @pl.when(pl.program_id(1) == 0)
        def _():
            dstate[...] = jnp.zeros_like(dstate)
            dlb_ref[...] = jnp.zeros_like(dlb_ref)
            dgn_ref[...] = jnp.zeros_like(dgn_ref)

        lbv = _sigmoid(lb_ref[...])
        oml = 1.0 - lbv
        gn = gn_ref[...]
        sz, f, key = _hgrn_gates(z_ref[...], lbv)
        qr = q_ref[...]
        sq = _sigmoid(qr)
        q = qr * sq
        v = v_ref[...]
        g = _dot_exact(tril_ref[...], jnp.log(f))
        eg = jnp.exp(g)
        blocks = lambda t: t.reshape(nb, bs, hd)
        g3, q3, k3, v3 = blocks(g), blocks(q), blocks(key), blocks(v)
        glast3 = g3[:, bs - 1:bs, :]
        egl3 = jnp.exp(glast3)
        ktail3 = jnp.exp(glast3 - g3)
        o = o_ref[...]
        rstd = lax.rsqrt(jnp.mean(o * o, axis=-1, keepdims=True) + EPS)
        xhat = o * rstd
        gt = gate_ref[...]
        sig = _sigmoid(gt)
        sg = gt * sig
        dm = dm_ref[...]
        dgn_ref[...] += jnp.sum(dm * xhat * sg, axis=0, keepdims=True)
        del stack_ref
        dp_ref[3] = (dm * xhat * gn * sig * (1.0 + gt * (1.0 - sig))).astype(BF)
        dxhat = dm * gn * sg
        do = rstd * (dxhat - xhat * jnp.mean(dxhat * xhat, axis=-1, keepdims=True))
        dob = do.astype(BF)
        do3 = blocks(do)
        qgb = (q * eg).astype(BF)
        for b in range(nb):
            rows = slice(b * bs, (b + 1) * bs)
            inc[b] = _dot_tn(dob[rows, :], qgb[rows, :])
        ds = dstate[...]
        for b in reversed(range(nb)):
            ds_all[b] = ds
            ds = ds * egl3[b] + inc[b]
        dstate[...] = ds
        ktb = (k3 * ktail3).reshape(tt, hd).astype(BF)
        vb = v.astype(BF)
        for b in range(nb):
            rows = slice(b * bs, (b + 1) * bs)
            prev = st_ref[b]
            dsb = ds_all[b]
            dsbb = dsb.astype(BF)
            dq_s[rows, :] = _dot(dob[rows, :], prev.astype(BF))
            dk_s[rows, :] = _dot(vb[rows, :], dsbb)
            dv_s[rows, :] = _dot_nt(ktb[rows, :], dsbb)
            dgl_s[b] = jnp.sum(prev * dsb, axis=0, keepdims=True)
        dq3 = blocks(dq_s[...] * eg)
        dk3 = blocks(dk_s[...]) * ktail3
        dg_last3 = jnp.sum(k3 * dk3, axis=1, keepdims=True) + egl3 * dgl_s[...]
        row_id = lax.broadcasted_iota(jnp.int32, (nb, bs, hd), 1)
        for j in range(bs):
            wj = jnp.where(row_id >= j, jnp.exp(jnp.minimum(g3 - g3[:, j:j + 1, :], 0.0)), 0.0)
            kj = k3[:, j:j + 1, :]
            a = jnp.sum(q3 * kj * wj, axis=-1, keepdims=True)
            da = jnp.sum(do3 * v3[:, j:j + 1, :], axis=-1, keepdims=True)
            dv_rows[:, j:j + 1, :] = jnp.sum(a * do3, axis=1, keepdims=True)
            dq3 = dq3 + da * kj * wj
            dk_rows[:, j:j + 1, :] = jnp.sum(da * q3 * wj, axis=1, keepdims=True)
        dk3 = dk3 + dk_rows[...]
        dv = dv_s[...] + dv_rows[...].reshape(tt, hd)
        dg3 = q3 * dq3 - k3 * dk3 + jnp.where(row_id == bs - 1, dg_last3, 0.0)
        dlf = _dot_exact(triu_ref[...], dg3.reshape(tt, hd))
        dk = dk3.reshape(tt, hd)
        dfk = dlf / f - dk
        dlb_ref[...] += jnp.sum(dfk * (1.0 - sz), axis=0, keepdims=True) * (lbv * oml)
        dp_ref[1] = (dfk * oml * sz * (1.0 - sz)).astype(BF)
        dp_ref[0] = (dq3.reshape(tt, hd) * sq * (1.0 + qr * (1.0 - sq))).astype(BF)
        dp_ref[2] = dv.astype(BF)

    rev = lambda t: nt - 1 - t
    comp = lambda j: pl.BlockSpec((tt, hd), lambda h, t, j=j: (rev(t), j * nh + h))
    gvec = pl.BlockSpec((1, hd), lambda h, t: (0, h))
    head_cols = pl.BlockSpec((tt, hd), lambda h, t: (rev(t), h))
    tri = pl.BlockSpec((tt, tt), lambda h, t: (0, 0))
    act = jax.ShapeDtypeStruct((s, w), BF)
    vec = jax.ShapeDtypeStruct((1, w), F32)
    tile_f32 = pltpu.VMEM((tt, hd), F32)
    return _launch(
        body, name=name, grid=(nh, nt),
        in_specs=[comp(4), comp(5), comp(6), comp(7), gvec, gvec, tri, tri, head_cols,
                  pl.BlockSpec((None, nb, hd, hd), lambda h, t: (h, rev(t), 0, 0)),
                  pl.BlockSpec((tt, hd), lambda h, t: (rev(t), nh + h)),
                  pl.BlockSpec(memory_space=pl.ANY)],
        out_specs=[pl.BlockSpec((4, tt, hd), lambda h, t: (1, rev(t), h)), gvec, gvec],
        out_shape=[jax.ShapeDtypeStruct(stack.shape, stack.dtype), vec, vec],
        scratch_shapes=[pltpu.VMEM((hd, hd), F32), pltpu.VMEM((nb, hd, hd), F32), pltpu.VMEM((nb, hd, hd), F32),
                        tile_f32, tile_f32, tile_f32, pltpu.VMEM((nb, 1, hd), F32),
                        pltpu.VMEM((nb, bs, hd), F32), pltpu.VMEM((nb, bs, hd), F32)],
        sem=("parallel", "arbitrary"),
        args=(proj, proj, proj, proj, lb_logits, ghg, _block_tri(tt, bs, upper=False), _block_tri(tt, bs, upper=True),
              o_raw, states, dmerged, stack), aliases={11: 0}, comm=comm)


def _position():
    return lax.axis_index("x"), lax.axis_index("y"), lax.axis_index("c")


def _all_gather_call(shards, name):
    n = len(shards)

    def body(*refs):
        ins, outs = refs[:n], refs[n:2 * n]
        send_sems, recv_sems, local_sems = refs[2 * n:]
        x, y, c = _position()
        me, sibling = (x, y, c), (x, y, 1 - c)
        chips = [(1 - x, y), (x, 1 - y), (1 - x, 1 - y)]

        def slot(a, p):
            return outs[a].at[4 * p[0] + 2 * p[1] + p[2]]

        def copy(a, k, block, to, src=None):
            return pltpu.make_async_remote_copy(
                src_ref=slot(a, block) if src is None else src, dst_ref=slot(a, block),
                send_sem=send_sems.at[a * 7 + k], recv_sem=recv_sems.at[a * 7 + k],
                device_id=to, device_id_type=MESH)

        mine = [pltpu.make_async_copy(ins[a], slot(a, me), local_sems.at[a]) for a in range(n)]
        for cp in mine:
            cp.start()
        first = []
        for a in range(n):
            first.append(copy(a, 0, me, sibling, src=ins[a]))
            first += [copy(a, 1 + j, me, (*chip, c), src=ins[a]) for j, chip in enumerate(chips)]
        for cp in first:
            cp.start()
        passed = []
        for j, chip in enumerate(chips):
            for a in range(n):
                copy(a, 1 + j, (*chip, c), me).wait_recv()
                fwd = copy(a, 4 + j, (*chip, c), sibling)
                fwd.start()
                passed.append(fwd)
        for a in range(n):
            copy(a, 0, sibling, me).wait_recv()
            for j, chip in enumerate(chips):
                copy(a, 4 + j, (*chip, 1 - c), me).wait_recv()
        for cp in first + passed:
            cp.wait_send()
        for cp in mine:
            cp.wait()

    return pl.pallas_call(
        body, name=name,
        in_specs=[HBM_SPEC] * n, out_specs=[HBM_SPEC] * n,
        out_shape=[jax.ShapeDtypeStruct((N_DEV,) + t.shape, t.dtype) for t in shards],
        scratch_shapes=[pltpu.SemaphoreType.DMA((7 * n,)), pltpu.SemaphoreType.DMA((7 * n,)),
                        pltpu.SemaphoreType.DMA((n,))],
    )(*shards)


def _slot(ref, p):
    return ref.at[4 * p[0] + 2 * p[1] + p[2]]


def _gather_round1(shards):
    n = len(shards)

    def plan(ins, outs, send_sems, recv_sems):
        x, y, c = _position()
        me = (x, y, c)
        peers = [(x, y, 1 - c), (1 - x, y, c), (x, 1 - y, c), (1 - x, 1 - y, c)]
        sends, recvs, local = [], [], []
        for a in range(n):
            local.append(pltpu.make_async_copy(ins[a], _slot(outs[a], me), send_sems.at[4 * n + a]))
            for k, peer in enumerate(peers):
                sems = dict(send_sem=send_sems.at[4 * a + k], recv_sem=recv_sems.at[4 * a + k],
                            device_id=peer, device_id_type=MESH)
                sends.append(pltpu.make_async_remote_copy(src_ref=ins[a], dst_ref=_slot(outs[a], me), **sems))
                recvs.append(pltpu.make_async_remote_copy(src_ref=ins[a], dst_ref=_slot(outs[a], peer), **sems))
        return sends, recvs, local

    def start(*refs):
        sends, _, local = plan(*refs)
        for cp in local + sends:
            cp.start()

    def finish(*refs):
        sends, recvs, local = plan(*refs)
        for cp in recvs:
            cp.wait_recv()
        for cp in sends:
            cp.wait_send()
        for cp in local:
            cp.wait()

    return _Comm(shards, [jax.ShapeDtypeStruct((N_DEV,) + t.shape, t.dtype) for t in shards], 5 * n, start, finish)


def _gather_round2(gathered):
    n = len(gathered)

    def plan(ins, outs, send_sems, recv_sems):
        x, y, c = _position()
        chips = [(1 - x, y), (x, 1 - y), (1 - x, 1 - y)]
        sends, recvs = [], []
        for a in range(n):
            for k, chip in enumerate(chips):
                sems = dict(send_sem=send_sems.at[3 * a + k], recv_sem=recv_sems.at[3 * a + k],
                            device_id=(x, y, 1 - c), device_id_type=MESH)
                sends.append(pltpu.make_async_remote_copy(
                    src_ref=_slot(ins[a], (*chip, c)), dst_ref=_slot(outs[a], (*chip, c)), **sems))
                recvs.append(pltpu.make_async_remote_copy(
                    src_ref=_slot(ins[a], (*chip, c)), dst_ref=_slot(outs[a], (*chip, 1 - c)), **sems))
        return sends, recvs

    def start(*refs):
        for cp in plan(*refs)[0]:
            cp.start()

    def finish(*refs):
        sends, recvs = plan(*refs)
        for cp in recvs:
            cp.wait_recv()
        for cp in sends:
            cp.wait_send()

    return _Comm(gathered, [jax.ShapeDtypeStruct(t.shape, t.dtype) for t in gathered], 3 * n, start, finish,
                 aliases={a: a for a in range(n)})


def _gather_two_level(shards, forward_at):
    n = len(shards)
    first, second = _gather_round1(shards), _gather_round2(shards)

    def middle(ins, outs, send_sems, recv_sems):
        first.finish(ins, outs, send_sems, recv_sems)
        second.start(outs, outs, _SemWindow(send_sems, first.n_sems), _SemWindow(recv_sems, first.n_sems))

    def finish(ins, outs, send_sems, recv_sems):
        second.finish(outs, outs, _SemWindow(send_sems, first.n_sems), _SemWindow(recv_sems, first.n_sems))

    return _Comm(shards, first.out_shape, first.n_sems + second.n_sems, first.start, finish,
                 middle=middle, middle_at=forward_at)


def _sibling_exchange(grads):
    n = len(grads)

    def plan(ins, outs, send_sems, recv_sems):
        x, y, c = _position()
        return [pltpu.make_async_remote_copy(
            src_ref=ins[a].at[2 * q + (1 - c)], dst_ref=outs[a].at[q],
            send_sem=send_sems.at[a * 4 + q], recv_sem=recv_sems.at[a * 4 + q],
            device_id=(x, y, 1 - c), device_id_type=MESH) for a in range(n) for q in range(4)]

    def start(*refs):
        for cp in plan(*refs):
            cp.start()

    def finish(*refs):
        for cp in plan(*refs):
            cp.wait()

    return _Comm(grads, [jax.ShapeDtypeStruct((4,) + t.shape[1:], t.dtype) for t in grads], 4 * n, start, finish)


def _chip_exchange(partials):
    n = len(partials)

    def plan(ins, outs, send_sems, recv_sems):
        x, y, c = _position()
        chips = [(1 - x, y), (x, 1 - y), (1 - x, 1 - y)]
        return [pltpu.make_async_remote_copy(
            src_ref=ins[a].at[2 * chip[0] + chip[1]], dst_ref=outs[a].at[k],
            send_sem=send_sems.at[a * 3 + k], recv_sem=recv_sems.at[a * 3 + k],
            device_id=(*chip, c), device_id_type=MESH) for a in range(n) for k, chip in enumerate(chips)]

    def start(*refs):
        for cp in plan(*refs):
            cp.start()

    def finish(*refs):
        for cp in plan(*refs):
            cp.wait()

    return _Comm(partials, [jax.ShapeDtypeStruct((3,) + t.shape[1:], t.dtype) for t in partials], 3 * n, start, finish)


class _SemWindow:
    def __init__(self, sems, offset):
        self._sems, self._offset = sems, offset

    @property
    def at(self):
        return self

    def __getitem__(self, i):
        return self._sems.at[self._offset + i]


def _join(parts):
    def each(fn_name, cins, couts, send_sems, recv_sems):
        i = o = sem = 0
        for p in parts:
            ni, no = len(p.operands), len(p.out_shape)
            getattr(p, fn_name)(cins[i:i + ni], couts[o:o + no], _SemWindow(send_sems, sem), _SemWindow(recv_sems, sem))
            i, o, sem = i + ni, o + no, sem + p.n_sems

    assert not any(p.aliases for p in parts)
    return _Comm([t for p in parts for t in p.operands], [t for p in parts for t in p.out_shape],
                 sum(p.n_sems for p in parts), functools.partial(each, "start"), functools.partial(each, "finish"))


def _pair_sum_call(grad, recv, parity, name):
    _, r, ccols = grad.shape
    tr = _tile(r, 256)

    def body(par_ref, g_ref, r_ref, p_ref, pb_ref):
        del par_ref
        p = g_ref[...] + r_ref[...]
        p_ref[...] = p
        pb_ref[...] = p.astype(BF)

    blk = lambda fn: pl.BlockSpec((None, tr, ccols), fn)
    return pl.pallas_call(
        body, name=name,
        grid_spec=pltpu.PrefetchScalarGridSpec(
            num_scalar_prefetch=1, grid=(4, r // tr),
            in_specs=[blk(lambda q, i, par: (2 * q + par[0], i, 0)), blk(lambda q, i, par: (q, i, 0))],
            out_specs=[blk(lambda q, i, par: (q, i, 0)), blk(lambda q, i, par: (q, i, 0))]),
        out_shape=[jax.ShapeDtypeStruct((4, r, ccols), F32), jax.ShapeDtypeStruct((4, r, ccols), BF)],
        compiler_params=_params("parallel", "parallel"),
    )(parity, grad, recv)


def _adamw_math(w, g, m, v):
    m = ADAM_B1 * m + (1.0 - ADAM_B1) * g
    v = ADAM_B2 * v + (1.0 - ADAM_B2) * (g * g)
    m_hat = m / (1.0 - ADAM_B1 ** ADAM_STEP)
    v_hat = v / (1.0 - ADAM_B2 ** ADAM_STEP)
    delta = -ADAM_LR * (m_hat / (jnp.sqrt(v_hat) + ADAM_EPS) + ADAM_WD * w)
    return delta, m, v


def _adamw_matrix_call(partial, recv, chip, w, m, v, name):
    r, ccols = w.shape
    gcols = partial.shape[2]
    tr = _tile(r, 256)

    def body(chip_ref, p_ref, r_ref, w_ref, m_ref, v_ref, g_out, d_out, m_out, v_out):
        del chip_ref
        cols = pl.ds(0, ccols)
        g = (p_ref[:, cols] + r_ref[0, :, cols].astype(F32) + r_ref[1, :, cols].astype(F32)
             + r_ref[2, :, cols].astype(F32))
        delta, mn, vn = _adamw_math(w_ref[...], g, m_ref[...], v_ref[...])
        g_out[...] = g
        d_out[...] = delta
        m_out[...] = mn
        v_out[...] = vn

    mat = pl.BlockSpec((tr, ccols), lambda i, ch: (i, 0))
    shp = jax.ShapeDtypeStruct((r, ccols), F32)
    return pl.pallas_call(
        body, name=name,
        grid_spec=pltpu.PrefetchScalarGridSpec(
            num_scalar_prefetch=1, grid=(r // tr,),
            in_specs=[pl.BlockSpec((None, tr, gcols), lambda i, ch: (ch[0], i, 0)),
                      pl.BlockSpec((3, tr, gcols), lambda i, ch: (0, i, 0)), mat, mat, mat],
            out_specs=[mat, mat, mat, mat]),
        out_shape=[shp, shp, shp, shp],
        compiler_params=_params("parallel"),
    )(chip, partial, recv, w, m, v)


def _adamw_vector_call(gathered, w, m, v, name):
    n = w.shape[1]

    def body(p_ref, w_ref, m_ref, v_ref, g_out, d_out, m_out, v_out):
        g = p_ref[0:1, :]
        for k in range(1, N_DEV):
            g = g + p_ref[k:k + 1, :]
        delta, mn, vn = _adamw_math(w_ref[...], g, m_ref[...], v_ref[...])
        g_out[...] = g
        d_out[...] = delta
        m_out[...] = mn
        v_out[...] = vn

    shp = jax.ShapeDtypeStruct((1, n), F32)
    return pl.pallas_call(body, name=name, out_shape=[shp, shp, shp, shp])(gathered, w, m, v)


def _round_up(n, mult):
    return (n + mult - 1) // mult * mult


def kernel(x, ffn1_norm, ffn1_w_gate, ffn1_w_up, ffn1_w_down, mix_norm, w_in, ret_norm_g, hgrn_lb_logits, hgrn_norm_g, w_out, ffn2_norm, ffn2_w_gate, ffn2_w_up, ffn2_w_down, final_norm, loss_target, m_ffn1_norm, m_ffn1_w_gate, m_ffn1_w_up, m_ffn1_w_down, m_mix_norm, m_w_in, m_ret_norm_g, m_hgrn_lb_logits, m_hgrn_norm_g, m_w_out, m_ffn2_norm, m_ffn2_w_gate, m_ffn2_w_up, m_ffn2_w_down, m_final_norm, v_ffn1_norm, v_ffn1_w_gate, v_ffn1_w_up, v_ffn1_w_down, v_mix_norm, v_w_in, v_ret_norm_g, v_hgrn_lb_logits, v_hgrn_norm_g, v_w_out, v_ffn2_norm, v_ffn2_w_gate, v_ffn2_w_up, v_ffn2_w_down, v_final_norm):
    xs = x[0]
    target = loss_target[0]
    s, d = xs.shape
    f_loc = ffn1_w_gate.shape[2]
    fp = _round_up(f_loc, LANE)
    pad_cols = lambda t: jnp.pad(t[0], ((0, 0), (0, fp - f_loc)))
    pad_rows = lambda t: jnp.pad(t[0], ((0, fp - f_loc), (0, 0)))

    mat_names = ["ffn1_w_gate", "ffn1_w_up", "ffn1_w_down", "w_in", "w_out", "ffn2_w_gate", "ffn2_w_up", "ffn2_w_down"]
    mat_pad = [pad_cols, pad_cols, pad_rows, lambda t: t[0], lambda t: t[0], pad_cols, pad_cols, pad_rows]
    mat_w = [ffn1_w_gate, ffn1_w_up, ffn1_w_down, w_in, w_out, ffn2_w_gate, ffn2_w_up, ffn2_w_down]
    mat_m = [m_ffn1_w_gate, m_ffn1_w_up, m_ffn1_w_down, m_w_in, m_w_out, m_ffn2_w_gate, m_ffn2_w_up, m_ffn2_w_down]
    mat_v = [v_ffn1_w_gate, v_ffn1_w_up, v_ffn1_w_down, v_w_in, v_w_out, v_ffn2_w_gate, v_ffn2_w_up, v_ffn2_w_down]

    cx, cy, cc = _position()
    parity = jnp.reshape(cc, (1,)).astype(jnp.int32)
    chip = jnp.reshape(2 * cx + cy, (1,)).astype(jnp.int32)
    mat_index = {nm: i for i, nm in enumerate(mat_names)}
    mat_out = {}

    def pair_sums(names, grads, from_sibling):
        return [_pair_sum_call(g, r, parity, "pair_sum_" + nm) for nm, g, r in zip(names, grads, from_sibling)]

    def update(names, sums, from_chips):
        for nm, (p, _), r in zip(names, sums, from_chips):
            i = mat_index[nm]
            res = _adamw_matrix_call(p, r, chip, mat_w[i][0], mat_m[i][0], mat_v[i][0], "adamw_" + nm)
            mat_out[nm] = [t[None] for t in res]

    shards = [p(t.astype(BF)) for p, t in zip(mat_pad, mat_w)]
    wg1, wu1 = _all_gather_call(shards[:2], "gather_ffn1_up")

    h1 = _rmsnorm_call(xs, ffn1_norm, "ffn1_norm")
    (g1, u1, a1), (wd1, win) = _ffn_up_call(h1, wg1, wu1, "ffn1_up", comm=_gather_two_level(shards[2:4], 0.7))
    (x1, h2), _ = _down_call(a1, wd1, xs, mix_norm, FFN_RESIDUAL_WEIGHT, "ffn1_down")
    (proj,), landed_a = _proj_call(h2, win, "mix_in", comm=_gather_round1([shards[4], shards[7]]))
    wmix = proj.shape[1] // 8
    cos, sin = _rope_tables(s, wmix // RET_HEADS)
    consts = _ret_consts(wmix // RET_HEADS)
    (o_hg, m_hg, st_hg), landed_b = _hgrn_fwd_call(proj, hgrn_lb_logits, hgrn_norm_g, "hgrn_fwd",
                                                   comm=_gather_round1(shards[5:7]))
    (o_ret, m_ret, st_ret), (wout, wd2, wg2, wu2) = _ret_fwd_call(proj, cos, sin, consts, ret_norm_g, "ret_fwd",
                                                                  comm=_gather_round2(landed_a + landed_b))
    merged = jnp.concatenate([m_ret, m_hg], axis=1)
    wout_wide = wout.reshape(2, wout.shape[0] * wout.shape[1] // 2, d)
    (x2, h3), _ = _down_call(merged, wout_wide, x1, ffn2_norm, 1.0, "mix_out")
    (g2, u2, a2), _ = _ffn_up_call(h3, wg2, wu2, "ffn2_up")
    (x3,), _ = _down_call(a2, wd2, x2, None, FFN_RESIDUAL_WEIGHT, "ffn2_down")
    loss_part, dx3, dx3b, gv_final = _loss_call(x3, target, final_norm[None, :], "loss_head")

    (dg2, du2), _ = _bwd_up_call(dx3b, wd2, g2, u2, FFN_RESIDUAL_WEIGHT, "ffn2_bwd_up")
    dh3, _ = _bwd_mm_call([(dg2, wg2), (du2, wu2)], "ffn2_bwd_down")
    dx2, dx2b, gv_n3 = _norm_bwd_call(dh3, dx3, x2, ffn2_norm, "ffn2_norm_bwd")
    names_a = ["ffn2_w_gate", "ffn2_w_up", "ffn2_w_down"]
    grads_a = [_wgrad_call(h3, dg2, N_DEV, False, 1.0, "ffn2_wgrad_gate"),
               _wgrad_call(h3, du2, N_DEV, False, 1.0, "ffn2_wgrad_up"),
               _wgrad_call(a2, dx3b, N_DEV, True, FFN_RESIDUAL_WEIGHT, "ffn2_wgrad_down")]

    (dmerged,), sib_a = _proj_call(dx2b, jnp.swapaxes(wout_wide, 1, 2), "mix_out_bwd",
                                   comm=_sibling_exchange(grads_a))
    gm_out = _wgrad_call(merged, dx2b, 2, True, 1.0, "mix_out_wgrad").reshape(wout.shape)
    sums_a = pair_sums(names_a, grads_a, sib_a)
    dproj_half, gv_ret = _ret_bwd_call(proj, cos, sin, consts, ret_norm_g, o_ret, st_ret, dmerged, "ret_bwd")
    (dproj, gv_lb, gv_hg), landed = _hgrn_bwd_call(
        proj, hgrn_lb_logits, hgrn_norm_g, o_hg, st_hg, dmerged, dproj_half, "hgrn_bwd",
        comm=_join([_chip_exchange([pb for _, pb in sums_a]), _sibling_exchange([gm_out])]))
    update(names_a, sums_a, landed[:3])
    sums_out = pair_sums(["w_out"], [gm_out], landed[3:])
    dh2, chips_out = _bwd_mm_call([(dproj, win)], "mix_in_bwd", per_step=2, comm=_chip_exchange([sums_out[0][1]]))
    dx1, dx1b, gv_n2 = _norm_bwd_call(dh2, dx2, x1, mix_norm, "mix_norm_bwd")
    update(["w_out"], sums_out, chips_out)

    gm_in = _wgrad_call(h2, dproj, N_DEV, False, 1.0, "mix_in_wgrad")
    gm_d1, sib_in = _wgrad_call(a1, dx1b, N_DEV, True, FFN_RESIDUAL_WEIGHT, "ffn1_wgrad_down",
                                comm=_sibling_exchange([gm_in]))
    sums_in = pair_sums(["w_in"], [gm_in], sib_in)
    (dg1, du1), landed = _bwd_up_call(dx1b, wd1, g1, u1, FFN_RESIDUAL_WEIGHT, "ffn1_bwd_up",
                                      comm=_join([_chip_exchange([sums_in[0][1]]), _sibling_exchange([gm_d1])]))
    update(["w_in"], sums_in, landed[:1])
    sums_d1 = pair_sums(["ffn1_w_down"], [gm_d1], landed[1:])
    gm_g1, chips_d1 = _wgrad_call(h1, dg1, N_DEV, False, 1.0, "ffn1_wgrad_gate",
                                  comm=_chip_exchange([sums_d1[0][1]]))
    update(["ffn1_w_down"], sums_d1, chips_d1)
    gm_u1, sib_g = _wgrad_call(h1, du1, N_DEV, False, 1.0, "ffn1_wgrad_up", comm=_sibling_exchange([gm_g1]))
    sums_g = pair_sums(["ffn1_w_gate"], [gm_g1], sib_g)
    n_row_tiles = s // _tile(s, 1024)
    assert n_row_tiles >= 2, "the sequence must span at least two row tiles"
    n_first = n_row_tiles // 2
    dh1_part, landed = _bwd_mm_call(
        [(dg1, wg1), (du1, wu1)], "ffn1_bwd_down_a", tiles=(0, n_first),
        comm=_join([_chip_exchange([sums_g[0][1]]), _sibling_exchange([gm_u1])]))
    update(["ffn1_w_gate"], sums_g, landed[:1])
    sums_u = pair_sums(["ffn1_w_up"], [gm_u1], landed[1:])
    dh1, chips_u = _bwd_mm_call(
        [(dg1, wg1), (du1, wu1)], "ffn1_bwd_down_b", tiles=(n_first, n_row_tiles - n_first),
        carry=dh1_part, comm=_chip_exchange([sums_u[0][1]]))
    update(["ffn1_w_up"], sums_u, chips_u)
    dx0, _, gv_n1 = _norm_bwd_call(dh1, dx1, xs, ffn1_norm, "ffn1_norm_bwd")

    vec_names = ["ffn1_norm", "mix_norm", "ret_norm_g", "hgrn_lb_logits", "hgrn_norm_g", "ffn2_norm", "final_norm"]
    vec_g = [gv_n1, gv_n2, gv_ret, gv_lb, gv_hg, gv_n3, gv_final]
    vec_w = [ffn1_norm, mix_norm, ret_norm_g, hgrn_lb_logits, hgrn_norm_g, ffn2_norm, final_norm[None, :]]
    vec_m = [m_ffn1_norm, m_mix_norm, m_ret_norm_g, m_hgrn_lb_logits, m_hgrn_norm_g, m_ffn2_norm, m_final_norm[None, :]]
    vec_v = [v_ffn1_norm, v_mix_norm, v_ret_norm_g, v_hgrn_lb_logits, v_hgrn_norm_g, v_ffn2_norm, v_final_norm[None, :]]
    cat = lambda ts: jnp.concatenate(ts, axis=1)
    (vec_all,) = _all_gather_call([cat(vec_g)], "gather_vector_grads")
    vres = _adamw_vector_call(vec_all[:, 0, :], cat(vec_w), cat(vec_m), cat(vec_v), "adamw_vectors")
    vec_out = {}
    off = 0
    for nm, t in zip(vec_names, vec_w):
        n = t.shape[1]
        parts = [r[:, off:off + n] for r in vres]
        if nm == "final_norm":
            parts = [p[0] for p in parts]
        vec_out[nm] = parts
        off += n

    loss = lax.psum(loss_part[0, 0], ("x", "y", "c"))
    order = ["ffn1_norm", "ffn1_w_gate", "ffn1_w_up", "ffn1_w_down", "mix_norm", "w_in", "ret_norm_g", "hgrn_lb_logits",
             "hgrn_norm_g", "w_out", "ffn2_norm", "ffn2_w_gate", "ffn2_w_up", "ffn2_w_down", "final_norm"]
    res = {**mat_out, **vec_out}
    outs = [loss, dx0[None]]
    for kind in range(4):
        outs += [res[nm][kind] for nm in order]
    return tuple(outs)
```

```python
import functools

import jax
import jax.numpy as jnp
from jax import lax
from jax.experimental import pallas as pl
from jax.experimental.pallas import tpu as pltpu

BF = jnp.bfloat16
F32 = jnp.float32
MESH = pl.DeviceIdType.MESH
HBM_SPEC = pl.BlockSpec(memory_space=pltpu.HBM)

N_DEV = 8
LANE = 128
EPS = 1e-6
ROPE_BASE = 10000.0
RET_HEADS = 4
HGRN_HEADS = 8
RET_CHUNK = 128
HGRN_BLOCK = 16
FFN_RESIDUAL_WEIGHT = 0.5
ADAM_LR = 0.001
ADAM_B1 = 0.9
ADAM_B2 = 0.999
ADAM_EPS = 1e-08
ADAM_WD = 0.01
ADAM_STEP = 10
VMEM_LIMIT = 56 * 1024 * 1024


def _tile(n, pref, mult=8):
    t = min(pref, n)
    t -= t % mult
    while t >= mult:
        if n % t == 0:
            return t
        t -= mult
    return n


def _params(*sem):
    return pltpu.CompilerParams(dimension_semantics=sem, vmem_limit_bytes=VMEM_LIMIT)


class _Comm:
    def __init__(self, operands, out_shape, n_sems, start, finish, aliases=None, middle=None, middle_at=0.0):
        self.operands = list(operands)
        self.out_shape = list(out_shape)
        self.n_sems = n_sems
        self.start = start
        self.finish = finish
        self.aliases = dict(aliases or {})
        self.middle = middle
        self.middle_at = middle_at


def _launch(body, *, name, grid, in_specs, out_specs, out_shape, sem, args, scratch_shapes=(), aliases=None, comm=None):
    in_specs, out_specs, out_shape = list(in_specs), list(out_specs), list(out_shape)
    scratch_shapes = list(scratch_shapes)
    aliases = dict(aliases or {})
    if comm is None:
        res = pl.pallas_call(body, name=name, grid=grid, in_specs=in_specs, out_specs=out_specs, out_shape=out_shape,
                             scratch_shapes=scratch_shapes, input_output_aliases=aliases,
                             compiler_params=_params(*sem))(*args)
        return list(res), []
    n_in, n_out, n_scr = len(in_specs), len(out_specs), len(scratch_shapes)
    ci, co = len(comm.operands), len(comm.out_shape)

    def carrying(*refs):
        bounds = [0, n_in, n_in + ci, n_in + ci + n_out, n_in + ci + n_out + co, n_in + ci + n_out + co + n_scr]
        ins, cins, outs, couts, scr = [refs[a:b] for a, b in zip(bounds[:-1], bounds[1:])]
        send_sems, recv_sems = refs[bounds[-1]:]
        ids = [pl.program_id(k) for k in range(len(grid))]
        first = functools.reduce(jnp.logical_and, [i == 0 for i in ids])
        last = functools.reduce(jnp.logical_and, [i == g - 1 for i, g in zip(ids, grid)])

        @pl.when(first)
        def _():
            comm.start(cins, couts, send_sems, recv_sems)

        if comm.middle is not None:
            step, total = ids[0], grid[0]
            for i, g in zip(ids[1:], grid[1:]):
                step, total = step * g + i, total * g

            @pl.when(step == int(total * comm.middle_at))
            def _():
                comm.middle(cins, couts, send_sems, recv_sems)

        body(*ins, *outs, *scr)

        @pl.when(last)
        def _():
            comm.finish(cins, couts, send_sems, recv_sems)

    res = pl.pallas_call(
        carrying, name=name, grid=grid,
        in_specs=in_specs + [HBM_SPEC] * ci, out_specs=out_specs + [HBM_SPEC] * co,
        out_shape=out_shape + comm.out_shape,
        scratch_shapes=scratch_shapes + [pltpu.SemaphoreType.DMA((comm.n_sems,)), pltpu.SemaphoreType.DMA((comm.n_sems,))],
        input_output_aliases={**aliases, **{n_in + a: n_out + b for a, b in comm.aliases.items()}},
        compiler_params=_params(*(["arbitrary"] * len(grid))),
    )(*args, *comm.operands)
    return list(res[:n_out]), list(res[n_out:])


def _comm_only_call(comm, name):
    def body(*refs):
        ci, co = len(comm.operands), len(comm.out_shape)
        cins, couts = refs[:ci], refs[ci:ci + co]
        send_sems, recv_sems = refs[ci + co:]
        comm.start(cins, couts, send_sems, recv_sems)
        comm.finish(cins, couts, send_sems, recv_sems)

    return pl.pallas_call(
        body, name=name,
        in_specs=[HBM_SPEC] * len(comm.operands), out_specs=[HBM_SPEC] * len(comm.out_shape),
        out_shape=comm.out_shape,
        scratch_shapes=[pltpu.SemaphoreType.DMA((comm.n_sems,)), pltpu.SemaphoreType.DMA((comm.n_sems,))],
        input_output_aliases=comm.aliases,
    )(*comm.operands)


def _sigmoid(v):
    return 0.5 * jnp.tanh(0.5 * v) + 0.5


def _dot(a, b):
    return jnp.dot(a, b, preferred_element_type=F32)


def _dot_nt(a, b):
    return lax.dot_general(a, b, (((1,), (1,)), ((), ())), preferred_element_type=F32)


def _dot_tn(a, b):
    return lax.dot_general(a, b, (((0,), (0,)), ((), ())), preferred_element_type=F32)


def _rmsnorm_call(x, gain, name):
    s, d = x.shape
    tm = _tile(s, 512)

    def body(x_ref, g_ref, o_ref):
        xv = x_ref[...]
        r = lax.rsqrt(jnp.mean(xv * xv, axis=-1, keepdims=True) + EPS)
        o_ref[...] = (xv * r * g_ref[...]).astype(BF)

    return pl.pallas_call(
        body, name=name, grid=(s // tm,),
        in_specs=[pl.BlockSpec((tm, d), lambda i: (i, 0)), pl.BlockSpec((1, d), lambda i: (0, 0))],
        out_specs=pl.BlockSpec((tm, d), lambda i: (i, 0)),
        out_shape=jax.ShapeDtypeStruct((s, d), BF),
        compiler_params=_params("parallel"),
    )(x, gain)


def _ffn_up_call(h, wg, wu, name, comm=None):
    s, d = h.shape
    nj, _, k = wg.shape
    tm = _tile(s, 1024)

    def body(h_ref, wg_ref, wu_ref, g_ref, u_ref, a_ref):
        hv = h_ref[...]
        g = _dot(hv, wg_ref[...])
        u = _dot(hv, wu_ref[...])
        g_ref[...] = g.astype(BF)
        u_ref[...] = u.astype(BF)
        a_ref[...] = (g * _sigmoid(g) * u).astype(BF)

    act = pl.BlockSpec((tm, k), lambda i, j: (i, j))
    wsp = pl.BlockSpec((None, d, k), lambda i, j: (j, 0, 0))
    return _launch(
        body, name=name, grid=(s // tm, nj),
        in_specs=[pl.BlockSpec((tm, d), lambda i, j: (i, 0)), wsp, wsp],
        out_specs=[act, act, act],
        out_shape=[jax.ShapeDtypeStruct((s, nj * k), BF)] * 3,
        sem=("parallel", "arbitrary"), args=(h, wg, wu), comm=comm)


def _proj_call(h, w, name, comm=None):
    s, d = h.shape
    nj, _, k = w.shape
    tm = _tile(s, 1024)

    def body(h_ref, w_ref, o_ref):
        o_ref[...] = _dot(h_ref[...], w_ref[...])

    return _launch(
        body, name=name, grid=(s // tm, nj),
        in_specs=[pl.BlockSpec((tm, d), lambda i, j: (i, 0)), pl.BlockSpec((None, d, k), lambda i, j: (j, 0, 0))],
        out_specs=[pl.BlockSpec((tm, k), lambda i, j: (i, j))],
        out_shape=[jax.ShapeDtypeStruct((s, nj * k), F32)],
        sem=("parallel", "arbitrary"), args=(h, w), comm=comm)


def _down_call(a, w, resid, gain, scale, name, comm=None):
    s = a.shape[0]
    nj, k, d = w.shape
    tm = _tile(s, 1024)
    strip = _tile(tm, 128)
    cn = _tile(d, 512, LANE)
    with_norm = gain is not None

    def body(*refs):
        if with_norm:
            a_ref, w_ref, r_ref, g_ref, x_ref, h_ref = refs
        else:
            a_ref, w_ref, r_ref, x_ref = refs
        j = pl.program_id(1)

        av = a_ref[...]

        @pl.when(j == 0)
        def _():
            for n0 in range(0, d, cn):
                x_ref[:, n0:n0 + cn] = _dot(av, w_ref[:, n0:n0 + cn])

        @pl.when(j > 0)
        def _():
            for n0 in range(0, d, cn):
                x_ref[:, n0:n0 + cn] += _dot(av, w_ref[:, n0:n0 + cn])

        @pl.when(j == nj - 1)
        def _():
            for r0 in range(0, tm, strip):
                rows = slice(r0, r0 + strip)
                xn = r_ref[rows, :] + (scale * x_ref[rows, :])
                x_ref[rows, :] = xn
                if with_norm:
                    r = lax.rsqrt(jnp.mean(xn * xn, axis=-1, keepdims=True) + EPS)
                    h_ref[rows, :] = (xn * r * g_ref[...]).astype(BF)

    row = pl.BlockSpec((tm, d), lambda i, j: (i, 0))
    in_specs = [pl.BlockSpec((tm, k), lambda i, j: (i, j)), pl.BlockSpec((None, k, d), lambda i, j: (j, 0, 0)), row]
    args = [a, w, resid]
    out_specs = [row]
    out_shape = [jax.ShapeDtypeStruct((s, d), F32)]
    if with_norm:
        in_specs.append(pl.BlockSpec((1, d), lambda i, j: (0, 0)))
        args.append(gain)
        out_specs.append(row)
        out_shape.append(jax.ShapeDtypeStruct((s, d), BF))
    return _launch(
        body, name=name, grid=(s // tm, nj),
        in_specs=in_specs, out_specs=out_specs, out_shape=out_shape,
        sem=("parallel", "arbitrary"), args=args, comm=comm)


def _loss_call(x, target, gain, name):
    s, d = x.shape
    tm = _tile(s, 512)

    def body(x_ref, t_ref, g_ref, loss_ref, dx_ref, dxb_ref, dg_ref):
        i = pl.program_id(0)

        @pl.when(i == 0)
        def _():
            loss_ref[...] = jnp.zeros_like(loss_ref)
            dg_ref[...] = jnp.zeros_like(dg_ref)

        xv = x_ref[...]
        gv = g_ref[...]
        r = lax.rsqrt(jnp.mean(xv * xv, axis=-1, keepdims=True) + EPS)
        xhat = xv * r
        err = xhat * gv - t_ref[...]
        per_tok = jnp.mean(err * err, axis=-1, keepdims=True)
        loss_ref[...] += 0.5 * jnp.sum(per_tok, axis=0, keepdims=True)
        dout = err * (1.0 / d)
        dg_ref[...] += jnp.sum(dout * xhat, axis=0, keepdims=True)
        dxhat = dout * gv
        dx = r * (dxhat - xhat * jnp.mean(dxhat * xhat, axis=-1, keepdims=True))
        dx_ref[...] = dx
        dxb_ref[...] = dx.astype(BF)

    row = pl.BlockSpec((tm, d), lambda i: (i, 0))
    vec = pl.BlockSpec((1, d), lambda i: (0, 0))
    return pl.pallas_call(
        body, name=name, grid=(s // tm,),
        in_specs=[row, row, vec],
        out_specs=[pl.BlockSpec((1, 1), lambda i: (0, 0)), row, row, vec],
        out_shape=[jax.ShapeDtypeStruct((1, 1), F32), jax.ShapeDtypeStruct((s, d), F32),
                   jax.ShapeDtypeStruct((s, d), BF), jax.ShapeDtypeStruct((1, d), F32)],
        compiler_params=_params("arbitrary"),
    )(x, target, gain)


def _bwd_up_call(dy, wd, g, u, scale, name, comm=None):
    s, d = dy.shape
    nj, k, _ = wd.shape
    tm = _tile(s, 1024)

    def body(dy_ref, w_ref, g_ref, u_ref, dg_ref, du_ref):
        da = scale * _dot_nt(dy_ref[...], w_ref[...])
        gv = g_ref[...].astype(F32)
        sig = _sigmoid(gv)
        du_ref[...] = (da * gv * sig).astype(BF)
        dg_ref[...] = (da * u_ref[...].astype(F32) * sig * (1.0 + gv * (1.0 - sig))).astype(BF)

    act = pl.BlockSpec((tm, k), lambda i, j: (i, j))
    return _launch(
        body, name=name, grid=(s // tm, nj),
        in_specs=[pl.BlockSpec((tm, d), lambda i, j: (i, 0)), pl.BlockSpec((None, k, d), lambda i, j: (j, 0, 0)), act, act],
        out_specs=[act, act],
        out_shape=[jax.ShapeDtypeStruct((s, nj * k), BF), jax.ShapeDtypeStruct((s, nj * k), BF)],
        sem=("parallel", "arbitrary"), args=(dy, wd, g, u), comm=comm)


def _norm_bwd_call(dh, dres, xin, gain, name):
    s, d = xin.shape
    tm = _tile(s, 256)

    def body(dh_ref, dres_ref, x_ref, g_ref, dx_ref, dxb_ref, dg_ref):
        @pl.when(pl.program_id(0) == 0)
        def _():
            dg_ref[...] = jnp.zeros_like(dg_ref)

        xv = x_ref[...]
        r = lax.rsqrt(jnp.mean(xv * xv, axis=-1, keepdims=True) + EPS)
        xhat = xv * r
        dh_v = dh_ref[...]
        dg_ref[...] += jnp.sum(dh_v * xhat, axis=0, keepdims=True)
        dxhat = dh_v * g_ref[...]
        dx = dres_ref[...] + r * (dxhat - xhat * jnp.mean(dxhat * xhat, axis=-1, keepdims=True))
        dx_ref[...] = dx
        dxb_ref[...] = dx.astype(BF)

    row = pl.BlockSpec((tm, d), lambda i: (i, 0))
    vec = pl.BlockSpec((1, d), lambda i: (0, 0))
    return pl.pallas_call(
        body, name=name, grid=(s // tm,),
        in_specs=[row, row, row, vec], out_specs=[row, row, vec],
        out_shape=[jax.ShapeDtypeStruct((s, d), F32), jax.ShapeDtypeStruct((s, d), BF), jax.ShapeDtypeStruct((1, d), F32)],
        compiler_params=_params("arbitrary"),
    )(dh, dres, xin, gain)


def _bwd_mm_call(pairs, name, per_step=1, tiles=None, carry=None, comm=None):
    a0 = pairs[0][0]
    s = a0.shape[-2]
    nblocks, d, k = pairs[0][1].shape
    nj = nblocks // per_step
    npair = len(pairs) * per_step
    tm = _tile(s, 1024)
    cn = _tile(d, 512, LANE)

    first_tile, n_tiles = (0, s // tm) if tiles is None else tiles

    def body(*refs):
        a_refs = refs[0:2 * npair:2]
        w_refs = refs[1:2 * npair:2]
        dh_ref = refs[-1]

        @pl.when(pl.program_id(1) == 0)
        def _():
            dh_ref[...] = jnp.zeros_like(dh_ref)

        for a_ref, w_ref in zip(a_refs, w_refs):
            av = a_ref[...]
            for n0 in range(0, d, cn):
                dh_ref[:, n0:n0 + cn] += _dot_nt(av, w_ref[n0:n0 + cn, :])

    row = pl.BlockSpec((tm, d), lambda i, j: (first_tile + i, 0))
    in_specs, args = [], []
    for a, w in pairs:
        for r in range(per_step):
            in_specs += [pl.BlockSpec((None, tm, k), lambda i, j, r=r: (j * per_step + r, first_tile + i, 0))
                         if a.ndim == 3 else
                         pl.BlockSpec((tm, k), lambda i, j, r=r: (first_tile + i, j * per_step + r)),
                         pl.BlockSpec((None, d, k), lambda i, j, r=r: (j * per_step + r, 0, 0))]
            args += [a, w]
    aliases = {}
    if carry is not None:
        aliases = {len(in_specs): 0}
        in_specs.append(pl.BlockSpec(memory_space=pl.ANY))
        args.append(carry)
    (dh,), landed = _launch(
        body, name=name, grid=(n_tiles, nj),
        in_specs=in_specs, out_specs=[row], out_shape=[jax.ShapeDtypeStruct((s, d), F32)],
        sem=("parallel", "arbitrary"), args=args, aliases=aliases, comm=comm)
    return dh, landed


def _wgrad_call(a, b, nj, a_blocked, scale, name, comm=None):
    s = a.shape[0]
    ka = a.shape[1] // nj if a_blocked else a.shape[1]
    b_stacked = b.ndim == 3
    kb = b.shape[-1] if (a_blocked or b_stacked) else b.shape[1] // nj
    ts = _tile(s, 2048)
    ns = s // ts

    def body(a_ref, b_ref, o_ref):
        t = pl.program_id(1)

        @pl.when(t == 0)
        def _():
            o_ref[...] = _dot_tn(a_ref[...], b_ref[...])

        @pl.when(t > 0)
        def _():
            o_ref[...] += _dot_tn(a_ref[...], b_ref[...])

        if scale != 1.0:
            @pl.when(t == ns - 1)
            def _():
                o_ref[...] = o_ref[...] * scale

    a_spec = pl.BlockSpec((ts, ka), (lambda j, t: (t, j)) if a_blocked else (lambda j, t: (t, 0)))
    if b_stacked:
        b_spec = pl.BlockSpec((None, ts, kb), lambda j, t: (j, t, 0))
    else:
        b_spec = pl.BlockSpec((ts, kb), (lambda j, t: (t, 0)) if a_blocked else (lambda j, t: (t, j)))
    (out,), landed = _launch(
        body, name=name, grid=(nj, ns),
        in_specs=[a_spec, b_spec],
        out_specs=[pl.BlockSpec((None, ka, kb), lambda j, t: (j, 0, 0))],
        out_shape=[jax.ShapeDtypeStruct((nj, ka, kb), F32)],
        sem=("parallel", "arbitrary"), args=(a, b), comm=comm)
    return out if comm is None else (out, landed)


def _rope(v, cos, sin):
    half = v.shape[-1] // 2
    v1, v2 = v[:, :half], v[:, half:]
    return jnp.concatenate([v1 * cos - v2 * sin, v2 * cos + v1 * sin], axis=-1)


def _rope_bwd(dv, cos, sin):
    half = dv.shape[-1] // 2
    d1, d2 = dv[:, :half], dv[:, half:]
    return jnp.concatenate([d1 * cos + d2 * sin, d2 * cos - d1 * sin], axis=-1)


def _ret_consts(hd):
    c = RET_CHUNK
    log_gamma = jnp.log(1.0 - jnp.exp2(-5.0 - jnp.arange(RET_HEADS, dtype=F32)))
    idx = jnp.arange(c, dtype=F32)
    rel = idx[:, None] - idx[None, :]
    mask = rel >= 0
    decay = jnp.where(mask[None], jnp.exp(log_gamma[:, None, None] * jnp.where(mask, rel, 0.0)[None]), 0.0)
    qdec = jnp.exp(log_gamma[:, None] * (idx + 1.0)[None, :])
    kdec = jnp.exp(log_gamma[:, None] * (c - 1.0 - idx)[None, :])
    gchunk = jnp.exp(log_gamma * c)
    bc = lambda t: jnp.broadcast_to(t[:, :, None], (RET_HEADS, t.shape[1], hd))
    return decay, bc(qdec), bc(kdec), bc(gchunk[:, None])


def _rope_tables(s, hd):
    inv = jnp.power(ROPE_BASE, -jnp.arange(0, hd, 2, dtype=F32) / hd)
    ang = jnp.arange(s, dtype=F32)[:, None] * inv[None, :]
    return jnp.cos(ang), jnp.sin(ang)


def _ret_fwd_call(proj, cos, sin, consts, gret, name, comm=None):
    s = proj.shape[0]
    w = proj.shape[1] // 8
    hd = w // RET_HEADS
    c = RET_CHUNK
    tt = _tile(s, 512, c)
    nc = tt // c
    decay, qdec, kdec, gch = consts
    scale = hd ** -0.5

    def body(q_ref, k_ref, v_ref, gate_ref, cos_ref, sin_ref, dec_ref, qd_ref, kd_ref, gc_ref, gn_ref,
             o_ref, m_ref, st_ref, state):
        @pl.when(pl.program_id(1) == 0)
        def _():
            state[...] = jnp.zeros_like(state)

        dec = dec_ref[...]
        for ci in range(nc):
            rows = slice(ci * c, (ci + 1) * c)
            cs, sn = cos_ref[rows, :], sin_ref[rows, :]
            q = _rope(q_ref[rows, :], cs, sn) * scale
            k = _rope(k_ref[rows, :], cs, sn)
            vb = v_ref[rows, :].astype(BF)
            sc = _dot_nt(q.astype(BF), k.astype(BF)) * dec
            prev = state[...]
            st_ref[ci] = prev
            o = _dot(sc.astype(BF), vb) + _dot((q * qd_ref[...]).astype(BF), prev.astype(BF))
            state[...] = gc_ref[...] * prev + _dot_tn((k * kd_ref[...]).astype(BF), vb)
            o_ref[rows, :] = o
            mu = jnp.mean(o, axis=-1, keepdims=True)
            cen = o - mu
            xhat = cen * lax.rsqrt(jnp.mean(cen * cen, axis=-1, keepdims=True) + EPS)
            gt = gate_ref[rows, :]
            m_ref[rows, :] = (xhat * gn_ref[...] * (gt * _sigmoid(gt))).astype(BF)

    nh = RET_HEADS
    comp = lambda j: pl.BlockSpec((tt, hd), lambda h, t, j=j: (t, j * nh + h))
    tab = pl.BlockSpec((tt, hd // 2), lambda h, t: (t, 0))
    per_head = lambda r: pl.BlockSpec((None, r, hd), lambda h, t: (h, 0, 0))
    return _launch(
        body, name=name, grid=(nh, s // tt),
        in_specs=[comp(0), comp(1), comp(2), comp(3), tab, tab,
                  pl.BlockSpec((None, c, c), lambda h, t: (h, 0, 0)), per_head(c), per_head(c), per_head(1),
                  pl.BlockSpec((1, hd), lambda h, t: (0, h))],
        out_specs=[pl.BlockSpec((tt, hd), lambda h, t: (t, h)), pl.BlockSpec((tt, hd), lambda h, t: (t, h)),
                   pl.BlockSpec((None, nc, hd, hd), lambda h, t: (h, t, 0, 0))],
        out_shape=[jax.ShapeDtypeStruct((s, w), F32), jax.ShapeDtypeStruct((s, w), BF),
                   jax.ShapeDtypeStruct((nh, s // c, hd, hd), F32)],
        scratch_shapes=[pltpu.VMEM((hd, hd), F32)],
        sem=("parallel", "arbitrary"),
        args=(proj, proj, proj, proj, cos, sin, decay, qdec, kdec, gch, gret), comm=comm)


def _ret_bwd_call(proj, cos, sin, consts, gret, o_raw, states, dmerged, name):
    s = proj.shape[0]
    w = proj.shape[1] // 8
    hd = w // RET_HEADS
    c = RET_CHUNK
    tt = _tile(s, 512, c)
    nc = tt // c
    nt = s // tt
    decay, qdec, kdec, gch = consts
    scale = hd ** -0.5

    def body(q_ref, k_ref, v_ref, gate_ref, cos_ref, sin_ref, dec_ref, qd_ref, kd_ref, gc_ref, gn_ref,
             o_ref, st_ref, dm_ref, dp_ref, dgn_ref, dstate):
        @pl.when(pl.program_id(1) == 0)
        def _():
            dstate[...] = jnp.zeros_like(dstate)
            dgn_ref[...] = jnp.zeros_like(dgn_ref)

        dec = dec_ref[...]
        gn = gn_ref[...]
        for ci in reversed(range(nc)):
            rows = slice(ci * c, (ci + 1) * c)
            cs, sn = cos_ref[rows, :], sin_ref[rows, :]
            q = _rope(q_ref[rows, :], cs, sn) * scale
            k = _rope(k_ref[rows, :], cs, sn)
            qb, kb = q.astype(BF), k.astype(BF)
            vb = v_ref[rows, :].astype(BF)
            sc = _dot_nt(qb, kb) * dec
            o = o_ref[rows, :]
            mu = jnp.mean(o, axis=-1, keepdims=True)
            cen = o - mu
            rstd = lax.rsqrt(jnp.mean(cen * cen, axis=-1, keepdims=True) + EPS)
            xhat = cen * rstd
            gt = gate_ref[rows, :]
            sig = _sigmoid(gt)
            sg = gt * sig
            dm = dm_ref[rows, :]
            dgn_ref[...] += jnp.sum(dm * xhat * sg, axis=0, keepdims=True)
            dp_ref[3, rows, :] = (dm * xhat * gn * sig * (1.0 + gt * (1.0 - sig))).astype(BF)
            dxhat = dm * gn * sg
            do = rstd * (dxhat - jnp.mean(dxhat, axis=-1, keepdims=True)
                         - xhat * jnp.mean(dxhat * xhat, axis=-1, keepdims=True))
            dob = do.astype(BF)
            prev = st_ref[ci]
            ds = dstate[...]
            dsb = ds.astype(BF)
            dsc = (_dot_nt(dob, vb) * dec).astype(BF)
            dq = _dot(dsc, kb) + _dot_nt(dob, prev.astype(BF)) * qd_ref[...]
            dk = _dot_tn(dsc, qb) + _dot_nt(vb, dsb) * kd_ref[...]
            dv = _dot_tn(sc.astype(BF), dob) + _dot((k * kd_ref[...]).astype(BF), dsb)
            dstate[...] = gc_ref[...] * ds + _dot_tn((q * qd_ref[...]).astype(BF), dob)
            dp_ref[0, rows, :] = _rope_bwd(dq * scale, cs, sn).astype(BF)
            dp_ref[1, rows, :] = _rope_bwd(dk, cs, sn).astype(BF)
            dp_ref[2, rows, :] = dv.astype(BF)

    nh = RET_HEADS
    rev = lambda t: nt - 1 - t
    comp = lambda j: pl.BlockSpec((tt, hd), lambda h, t, j=j: (rev(t), j * nh + h))
    tab = pl.BlockSpec((tt, hd // 2), lambda h, t: (rev(t), 0))
    per_head = lambda r: pl.BlockSpec((None, r, hd), lambda h, t: (h, 0, 0))
    head_cols = pl.BlockSpec((tt, hd), lambda h, t: (rev(t), h))
    gvec = pl.BlockSpec((1, hd), lambda h, t: (0, h))
    act = jax.ShapeDtypeStruct((s, w), BF)
    return pl.pallas_call(
        body, name=name, grid=(nh, nt),
        in_specs=[comp(0), comp(1), comp(2), comp(3), tab, tab,
                  pl.BlockSpec((None, c, c), lambda h, t: (h, 0, 0)), per_head(c), per_head(c), per_head(1), gvec,
                  head_cols, pl.BlockSpec((None, nc, hd, hd), lambda h, t: (h, rev(t), 0, 0)), head_cols],
        out_specs=[pl.BlockSpec((4, tt, hd), lambda h, t: (0, rev(t), h)), gvec],
        out_shape=[jax.ShapeDtypeStruct((8, s, w), BF), jax.ShapeDtypeStruct((1, w), F32)],
        scratch_shapes=[pltpu.VMEM((hd, hd), F32)],
        compiler_params=_params("parallel", "arbitrary"),
    )(proj, proj, proj, proj, cos, sin, decay, qdec, kdec, gch, gret, o_raw, states, dmerged)


def _block_tri(n, bs, upper):
    r = jnp.arange(n)[:, None]
    cidx = jnp.arange(n)[None, :]
    same = (r // bs) == (cidx // bs)
    return jnp.where(same & ((cidx >= r) if upper else (cidx <= r)), 1.0, 0.0).astype(BF)


def _dot_exact(tri, v):
    hi = v.astype(BF)
    rest = v - hi.astype(F32)
    mid = rest.astype(BF)
    lo = (rest - mid.astype(F32)).astype(BF)
    return _dot(tri, hi) + _dot(tri, mid) + _dot(tri, lo)


def _hgrn_gates(z, lbv):
    sz = _sigmoid(z)
    oml = 1.0 - lbv
    f = lbv + oml * sz
    key = oml * (1.0 - sz)
    return sz, f, key


def _hgrn_fwd_call(proj, lb_logits, ghg, name, comm=None):
    s = proj.shape[0]
    w = proj.shape[1] // 8
    nh = HGRN_HEADS
    hd = w // nh
    bs = HGRN_BLOCK
    tt = _tile(s, 256, bs)
    nb = tt // bs

    def body(q_ref, z_ref, v_ref, gate_ref, lb_ref, gn_ref, tril_ref, o_ref, m_ref, st_ref, state, upd):
        @pl.when(pl.program_id(1) == 0)
        def _():
            state[...] = jnp.zeros_like(state)

        lbv = _sigmoid(lb_ref[...])
        _, f, key = _hgrn_gates(z_ref[...], lbv)
        qr = q_ref[...]
        q = qr * _sigmoid(qr)
        v = v_ref[...]
        g = _dot_exact(tril_ref[...], jnp.log(f))
        blocks = lambda t: t.reshape(nb, bs, hd)
        g3, q3, k3, v3 = blocks(g), blocks(q), blocks(key), blocks(v)
        glast3 = g3[:, bs - 1:bs, :]
        row_id = lax.broadcasted_iota(jnp.int32, (nb, bs, hd), 1)
        o3 = jnp.zeros((nb, bs, hd), F32)
        for j in range(bs):
            wj = jnp.where(row_id >= j, jnp.exp(jnp.minimum(g3 - g3[:, j:j + 1, :], 0.0)), 0.0)
            a = jnp.sum(q3 * k3[:, j:j + 1, :] * wj, axis=-1, keepdims=True)
            o3 = o3 + a * v3[:, j:j + 1, :]
        ktb = (k3 * jnp.exp(glast3 - g3)).reshape(tt, hd).astype(BF)
        vb = v.astype(BF)
        for b in range(nb):
            rows = slice(b * bs, (b + 1) * bs)
            upd[b] = _dot_tn(vb[rows, :], ktb[rows, :])
        egl3 = jnp.exp(glast3)
        st = state[...]
        for b in range(nb):
            st_ref[b] = st
            st = st * egl3[b] + upd[b]
        state[...] = st
        qgb = (q * jnp.exp(g)).astype(BF)
        o_intra = o3.reshape(tt, hd)
        gn = gn_ref[...]
        for b in range(nb):
            rows = slice(b * bs, (b + 1) * bs)
            o = o_intra[rows, :] + _dot_nt(qgb[rows, :], st_ref[b].astype(BF))
            o_ref[rows, :] = o
            gt = gate_ref[rows, :]
            xhat = o * lax.rsqrt(jnp.mean(o * o, axis=-1, keepdims=True) + EPS)
            m_ref[rows, :] = (xhat * gn * (gt * _sigmoid(gt))).astype(BF)

    comp = lambda j: pl.BlockSpec((tt, hd), lambda h, t, j=j: (t, j * nh + h))
    gvec = pl.BlockSpec((1, hd), lambda h, t: (0, h))
    head_cols = pl.BlockSpec((tt, hd), lambda h, t: (t, h))
    return _launch(
        body, name=name, grid=(nh, s // tt),
        in_specs=[comp(4), comp(5), comp(6), comp(7), gvec, gvec, pl.BlockSpec((tt, tt), lambda h, t: (0, 0))],
        out_specs=[head_cols, head_cols, pl.BlockSpec((None, nb, hd, hd), lambda h, t: (h, t, 0, 0))],
        out_shape=[jax.ShapeDtypeStruct((s, w), F32), jax.ShapeDtypeStruct((s, w), BF),
                   jax.ShapeDtypeStruct((nh, s // bs, hd, hd), F32)],
        scratch_shapes=[pltpu.VMEM((hd, hd), F32), pltpu.VMEM((nb, hd, hd), F32)],
        sem=("parallel", "arbitrary"),
        args=(proj, proj, proj, proj, lb_logits, ghg, _block_tri(tt, bs, upper=False)), comm=comm)


def _hgrn_bwd_call(proj, lb_logits, ghg, o_raw, states, dmerged, stack, name, comm=None):
    s = proj.shape[0]
    w = proj.shape[1] // 8
    nh = HGRN_HEADS
    hd = w // nh
    bs = HGRN_BLOCK
    tt = _tile(s, 256, bs)
    nb = tt // bs
    nt = s // tt

    def body(q_ref, z_ref, v_ref, gate_ref, lb_ref, gn_ref, tril_ref, triu_ref, o_ref, st_ref, dm_ref, stack_ref,
             dp_ref, dlb_ref, dgn_ref,
             dstate, ds_all, inc, dq_s, dk_s, dv_s, dgl_s, dk_rows, dv_rows):
        @pl.when(pl.program_id(1) == 0)
        def _():
            dstate[...] = jnp.zeros_like(dstate)
            dlb_ref[...] = jnp.zeros_like(dlb_ref)
            dgn_ref[...] = jnp.zeros_like(dgn_ref)

        lbv = _sigmoid(lb_ref[...])
        oml = 1.0 - lbv
        gn = gn_ref[...]
        sz, f, key = _hgrn_gates(z_ref[...], lbv)
        qr = q_ref[...]
        sq = _sigmoid(qr)
        q = qr * sq
        v = v_ref[...]
        g = _dot_exact(tril_ref[...], jnp.log(f))
        eg = jnp.exp(g)
        blocks = lambda t: t.reshape(nb, bs, hd)
        g3, q3, k3, v3 = blocks(g), blocks(q), blocks(key), blocks(v)
        glast3 = g3[:, bs - 1:bs, :]
        egl3 = jnp.exp(glast3)
        ktail3 = jnp.exp(glast3 - g3)
        o = o_ref[...]
        rstd = lax.rsqrt(jnp.mean(o * o, axis=-1, keepdims=True) + EPS)
        xhat = o * rstd
        gt = gate_ref[...]
        sig = _sigmoid(gt)
        sg = gt * sig
        dm = dm_ref[...]
        dgn_ref[...] += jnp.sum(dm * xhat * sg, axis=0, keepdims=True)
        del stack_ref
        dp_ref[3] = (dm * xhat * gn * sig * (1.0 + gt * (1.0 - sig))).astype(BF)
        dxhat = dm * gn * sg
        do = rstd * (dxhat - xhat * jnp.mean(dxhat * xhat, axis=-1, keepdims=True))
        dob = do.astype(BF)
        do3 = blocks(do)
        qgb = (q * eg).astype(BF)
        for b in range(nb):
            rows = slice(b * bs, (b + 1) * bs)
            inc[b] = _dot_tn(dob[rows, :], qgb[rows, :])
        ds = dstate[...]
        for b in reversed(range(nb)):
            ds_all[b] = ds
            ds = ds * egl3[b] + inc[b]
        dstate[...] = ds
        ktb = (k3 * ktail3).reshape(tt, hd).astype(BF)
        vb = v.astype(BF)
        for b in range(nb):
            rows = slice(b * bs, (b + 1) * bs)
            prev = st_ref[b]
            dsb = ds_all[b]
            dsbb = dsb.astype(BF)
            dq_s[rows, :] = _dot(dob[rows, :], prev.astype(BF))
            dk_s[rows, :] = _dot(vb[rows, :], dsbb)
            dv_s[rows, :] = _dot_nt(ktb[rows, :], dsbb)
            dgl_s[b] = jnp.sum(prev * dsb, axis=0, keepdims=True)
        dq3 = blocks(dq_s[...] * eg)
        dk3 = blocks(dk_s[...]) * ktail3
        dg_last3 = jnp.sum(k3 * dk3, axis=1, keepdims=True) + egl3 * dgl_s[...]
        row_id = lax.broadcasted_iota(jnp.int32, (nb, bs, hd), 1)
        for j in range(bs):
            wj = jnp.where(row_id >= j, jnp.exp(jnp.minimum(g3 - g3[:, j:j + 1, :], 0.0)), 0.0)
            kj = k3[:, j:j + 1, :]
            a = jnp.sum(q3 * kj * wj, axis=-1, keepdims=True)
            da = jnp.sum(do3 * v3[:, j:j + 1, :], axis=-1, keepdims=True)
            dv_rows[:, j:j + 1, :] = jnp.sum(a * do3, axis=1, keepdims=True)
            dq3 = dq3 + da * kj * wj
            dk_rows[:, j:j + 1, :] = jnp.sum(da * q3 * wj, axis=1, keepdims=True)
        dk3 = dk3 + dk_rows[...]
        dv = dv_s[...] + dv_rows[...].reshape(tt, hd)
        dg3 = q3 * dq3 - k3 * dk3 + jnp.where(row_id == bs - 1, dg_last3, 0.0)
        dlf = _dot_exact(triu_ref[...], dg3.reshape(tt, hd))
        dk = dk3.reshape(tt, hd)
        dfk = dlf / f - dk
        dlb_ref[...] += jnp.sum(dfk * (1.0 - sz), axis=0, keepdims=True) * (lbv * oml)
        dp_ref[1] = (dfk * oml * sz * (1.0 - sz)).astype(BF)
        dp_ref[0] = (dq3.reshape(tt, hd) * sq * (1.0 + qr * (1.0 - sq))).astype(BF)
        dp_ref[2] = dv.astype(BF)

    rev = lambda t: nt - 1 - t
    comp = lambda j: pl.BlockSpec((tt, hd), lambda h, t, j=j: (rev(t), j * nh + h))
    gvec = pl.BlockSpec((1, hd), lambda h, t: (0, h))
    head_cols = pl.BlockSpec((tt, hd), lambda h, t: (rev(t), h))
    tri = pl.BlockSpec((tt, tt), lambda h, t: (0, 0))
    act = jax.ShapeDtypeStruct((s, w), BF)
    vec = jax.ShapeDtypeStruct((1, w), F32)
    tile_f32 = pltpu.VMEM((tt, hd), F32)
    return _launch(
        body, name=name, grid=(nh, nt),
        in_specs=[comp(4), comp(5), comp(6), comp(7), gvec, gvec, tri, tri, head_cols,
                  pl.BlockSpec((None, nb, hd, hd), lambda h, t: (h, rev(t), 0, 0)),
                  pl.BlockSpec((tt, hd), lambda h, t: (rev(t), nh + h)),
                  pl.BlockSpec(memory_space=pl.ANY)],
        out_specs=[pl.BlockSpec((4, tt, hd), lambda h, t: (1, rev(t), h)), gvec, gvec],
        out_shape=[jax.ShapeDtypeStruct(stack.shape, stack.dtype), vec, vec],
        scratch_shapes=[pltpu.VMEM((hd, hd), F32), pltpu.VMEM((nb, hd, hd), F32), pltpu.VMEM((nb, hd, hd), F32),
                        tile_f32, tile_f32, tile_f32, pltpu.VMEM((nb, 1, hd), F32),
                        pltpu.VMEM((nb, bs, hd), F32), pltpu.VMEM((nb, bs, hd), F32)],
        sem=("parallel", "arbitrary"),
        args=(proj, proj, proj, proj, lb_logits, ghg, _block_tri(tt, bs, upper=False), _block_tri(tt, bs, upper=True),
              o_raw, states, dmerged, stack), aliases={11: 0}, comm=comm)


def _position():
    return lax.axis_index("x"), lax.axis_index("y"), lax.axis_index("c")


def _all_gather_call(shards, name):
    n = len(shards)

    def body(*refs):
        ins, outs = refs[:n], refs[n:2 * n]
        send_sems, recv_sems, local_sems = refs[2 * n:]
        x, y, c = _position()
        me, sibling = (x, y, c), (x, y, 1 - c)
        chips = [(1 - x, y), (x, 1 - y), (1 - x, 1 - y)]

        def slot(a, p):
            return outs[a].at[4 * p[0] + 2 * p[1] + p[2]]

        def copy(a, k, block, to, src=None):
            return pltpu.make_async_remote_copy(
                src_ref=slot(a, block) if src is None else src, dst_ref=slot(a, block),
                send_sem=send_sems.at[a * 7 + k], recv_sem=recv_sems.at[a * 7 + k],
                device_id=to, device_id_type=MESH)

        mine = [pltpu.make_async_copy(ins[a], slot(a, me), local_sems.at[a]) for a in range(n)]
        for cp in mine:
            cp.start()
        first = []
        for a in range(n):
            first.append(copy(a, 0, me, sibling, src=ins[a]))
            first += [copy(a, 1 + j, me, (*chip, c), src=ins[a]) for j, chip in enumerate(chips)]
        for cp in first:
            cp.start()
        passed = []
        for j, chip in enumerate(chips):
            for a in range(n):
                copy(a, 1 + j, (*chip, c), me).wait_recv()
                fwd = copy(a, 4 + j, (*chip, c), sibling)
                fwd.start()
                passed.append(fwd)
        for a in range(n):
            copy(a, 0, sibling, me).wait_recv()
            for j, chip in enumerate(chips):
                copy(a, 4 + j, (*chip, 1 - c), me).wait_recv()
        for cp in first + passed:
            cp.wait_send()
        for cp in mine:
            cp.wait()

    return pl.pallas_call(
        body, name=name,
        in_specs=[HBM_SPEC] * n, out_specs=[HBM_SPEC] * n,
        out_shape=[jax.ShapeDtypeStruct((N_DEV,) + t.shape, t.dtype) for t in shards],
        scratch_shapes=[pltpu.SemaphoreType.DMA((7 * n,)), pltpu.SemaphoreType.DMA((7 * n,)),
                        pltpu.SemaphoreType.DMA((n,))],
    )(*shards)


def _slot(ref, p):
    return ref.at[4 * p[0] + 2 * p[1] + p[2]]


def _gather_round1(shards):
    n = len(shards)

    def plan(ins, outs, send_sems, recv_sems):
        x, y, c = _position()
        me = (x, y, c)
        peers = [(x, y, 1 - c), (1 - x, y, c), (x, 1 - y, c), (1 - x, 1 - y, c)]
        sends, recvs, local = [], [], []
        for a in range(n):
            local.append(pltpu.make_async_copy(ins[a], _slot(outs[a], me), send_sems.at[4 * n + a]))
            for k, peer in enumerate(peers):
                sems = dict(send_sem=send_sems.at[4 * a + k], recv_sem=recv_sems.at[4 * a + k],
                            device_id=peer, device_id_type=MESH)
                sends.append(pltpu.make_async_remote_copy(src_ref=ins[a], dst_ref=_slot(outs[a], me), **sems))
                recvs.append(pltpu.make_async_remote_copy(src_ref=ins[a], dst_ref=_slot(outs[a], peer), **sems))
        return sends, recvs, local

    def start(*refs):
        sends, _, local = plan(*refs)
        for cp in local + sends:
            cp.start()

    def finish(*refs):
        sends, recvs, local = plan(*refs)
        for cp in recvs:
            cp.wait_recv()
        for cp in sends:
            cp.wait_send()
        for cp in local:
            cp.wait()

    return _Comm(shards, [jax.ShapeDtypeStruct((N_DEV,) + t.shape, t.dtype) for t in shards], 5 * n, start, finish)


def _gather_round2(gathered):
    n = len(gathered)

    def plan(ins, outs, send_sems, recv_sems):
        x, y, c = _position()
        chips = [(1 - x, y), (x, 1 - y), (1 - x, 1 - y)]
        sends, recvs = [], []
        for a in range(n):
            for k, chip in enumerate(chips):
                sems = dict(send_sem=send_sems.at[3 * a + k], recv_sem=recv_sems.at[3 * a + k],
                            device_id=(x, y, 1 - c), device_id_type=MESH)
                sends.append(pltpu.make_async_remote_copy(
                    src_ref=_slot(ins[a], (*chip, c)), dst_ref=_slot(outs[a], (*chip, c)), **sems))
                recvs.append(pltpu.make_async_remote_copy(
                    src_ref=_slot(ins[a], (*chip, c)), dst_ref=_slot(outs[a], (*chip, 1 - c)), **sems))
        return sends, recvs

    def start(*refs):
        for cp in plan(*refs)[0]:
            cp.start()

    def finish(*refs):
        sends, recvs = plan(*refs)
        for cp in recvs:
            cp.wait_recv()
        for cp in sends:
            cp.wait_send()

    return _Comm(gathered, [jax.ShapeDtypeStruct(t.shape, t.dtype) for t in gathered], 3 * n, start, finish,
                 aliases={a: a for a in range(n)})


def _gather_two_level(shards, forward_at):
    n = len(shards)
    first, second = _gather_round1(shards), _gather_round2(shards)

    def middle(ins, outs, send_sems, recv_sems):
        first.finish(ins, outs, send_sems, recv_sems)
        second.start(outs, outs, _SemWindow(send_sems, first.n_sems), _SemWindow(recv_sems, first.n_sems))

    def finish(ins, outs, send_sems, recv_sems):
        second.finish(outs, outs, _SemWindow(send_sems, first.n_sems), _SemWindow(recv_sems, first.n_sems))

    return _Comm(shards, first.out_shape, first.n_sems + second.n_sems, first.start, finish,
                 middle=middle, middle_at=forward_at)


def _sibling_exchange(grads):
    n = len(grads)

    def plan(ins, outs, send_sems, recv_sems):
        x, y, c = _position()
        return [pltpu.make_async_remote_copy(
            src_ref=ins[a].at[2 * q + (1 - c)], dst_ref=outs[a].at[q],
            send_sem=send_sems.at[a * 4 + q], recv_sem=recv_sems.at[a * 4 + q],
            device_id=(x, y, 1 - c), device_id_type=MESH) for a in range(n) for q in range(4)]

    def start(*refs):
        for cp in plan(*refs):
            cp.start()

    def finish(*refs):
        for cp in plan(*refs):
            cp.wait()

    return _Comm(grads, [jax.ShapeDtypeStruct((4,) + t.shape[1:], t.dtype) for t in grads], 4 * n, start, finish)


def _chip_exchange(partials):
    n = len(partials)

    def plan(ins, outs, send_sems, recv_sems):
        x, y, c = _position()
        chips = [(1 - x, y), (x, 1 - y), (1 - x, 1 - y)]
        return [pltpu.make_async_remote_copy(
            src_ref=ins[a].at[2 * chip[0] + chip[1]], dst_ref=outs[a].at[k],
            send_sem=send_sems.at[a * 3 + k], recv_sem=recv_sems.at[a * 3 + k],
            device_id=(*chip, c), device_id_type=MESH) for a in range(n) for k, chip in enumerate(chips)]

    def start(*refs):
        for cp in plan(*refs):
            cp.start()

    def finish(*refs):
        for cp in plan(*refs):
            cp.wait()

    return _Comm(partials, [jax.ShapeDtypeStruct((3,) + t.shape[1:], t.dtype) for t in partials], 3 * n, start, finish)


class _SemWindow:
    def __init__(self, sems, offset):
        self._sems, self._offset = sems, offset

    @property
    def at(self):
        return self

    def __getitem__(self, i):
        return self._sems.at[self._offset + i]


def _join(parts):
    def each(fn_name, cins, couts, send_sems, recv_sems):
        i = o = sem = 0
        for p in parts:
            ni, no = len(p.operands), len(p.out_shape)
            getattr(p, fn_name)(cins[i:i + ni], couts[o:o + no], _SemWindow(send_sems, sem), _SemWindow(recv_sems, sem))
            i, o, sem = i + ni, o + no, sem + p.n_sems

    assert not any(p.aliases for p in parts)
    return _Comm([t for p in parts for t in p.operands], [t for p in parts for t in p.out_shape],
                 sum(p.n_sems for p in parts), functools.partial(each, "start"), functools.partial(each, "finish"))


def _pair_sum_call(grad, recv, parity, name):
    _, r, ccols = grad.shape
    tr = _tile(r, 256)

    def body(par_ref, g_ref, r_ref, p_ref, pb_ref):
        del par_ref
        p = g_ref[...] + r_ref[...]
        p_ref[...] = p
        pb_ref[...] = p.astype(BF)

    blk = lambda fn: pl.BlockSpec((None, tr, ccols), fn)
    return pl.pallas_call(
        body, name=name,
        grid_spec=pltpu.PrefetchScalarGridSpec(
            num_scalar_prefetch=1, grid=(4, r // tr),
            in_specs=[blk(lambda q, i, par: (2 * q + par[0], i, 0)), blk(lambda q, i, par: (q, i, 0))],
            out_specs=[blk(lambda q, i, par: (q, i, 0)), blk(lambda q, i, par: (q, i, 0))]),
        out_shape=[jax.ShapeDtypeStruct((4, r, ccols), F32), jax.ShapeDtypeStruct((4, r, ccols), BF)],
        compiler_params=_params("parallel", "parallel"),
    )(parity, grad, recv)


def _adamw_math(w, g, m, v):
    m = ADAM_B1 * m + (1.0 - ADAM_B1) * g
    v = ADAM_B2 * v + (1.0 - ADAM_B2) * (g * g)
    m_hat = m / (1.0 - ADAM_B1 ** ADAM_STEP)
    v_hat = v / (1.0 - ADAM_B2 ** ADAM_STEP)
    delta = -ADAM_LR * (m_hat / (jnp.sqrt(v_hat) + ADAM_EPS) + ADAM_WD * w)
    return delta, m, v


def _adamw_matrix_call(partial, recv, chip, w, m, v, name):
    r, ccols = w.shape
    gcols = partial.shape[2]
    tr = _tile(r, 256)

    def body(chip_ref, p_ref, r_ref, w_ref, m_ref, v_ref, g_out, d_out, m_out, v_out):
        del chip_ref
        cols = pl.ds(0, ccols)
        g = (p_ref[:, cols] + r_ref[0, :, cols].astype(F32) + r_ref[1, :, cols].astype(F32)
             + r_ref[2, :, cols].astype(F32))
        delta, mn, vn = _adamw_math(w_ref[...], g, m_ref[...], v_ref[...])
        g_out[...] = g
        d_out[...] = delta
        m_out[...] = mn
        v_out[...] = vn

    mat = pl.BlockSpec((tr, ccols), lambda i, ch: (i, 0))
    shp = jax.ShapeDtypeStruct((r, ccols), F32)
    return pl.pallas_call(
        body, name=name,
        grid_spec=pltpu.PrefetchScalarGridSpec(
            num_scalar_prefetch=1, grid=(r // tr,),
            in_specs=[pl.BlockSpec((None, tr, gcols), lambda i, ch: (ch[0], i, 0)),
                      pl.BlockSpec((3, tr, gcols), lambda i, ch: (0, i, 0)), mat, mat, mat],
            out_specs=[mat, mat, mat, mat]),
        out_shape=[shp, shp, shp, shp],
        compiler_params=_params("parallel"),
    )(chip, partial, recv, w, m, v)


def _adamw_vector_call(gathered, w, m, v, name):
    n = w.shape[1]

    def body(p_ref, w_ref, m_ref, v_ref, g_out, d_out, m_out, v_out):
        g = p_ref[0:1, :]
        for k in range(1, N_DEV):
            g = g + p_ref[k:k + 1, :]
        delta, mn, vn = _adamw_math(w_ref[...], g, m_ref[...], v_ref[...])
        g_out[...] = g
        d_out[...] = delta
        m_out[...] = mn
        v_out[...] = vn

    shp = jax.ShapeDtypeStruct((1, n), F32)
    return pl.pallas_call(body, name=name, out_shape=[shp, shp, shp, shp])(gathered, w, m, v)


def _round_up(n, mult):
    return (n + mult - 1) // mult * mult


def kernel(x, ffn1_norm, ffn1_w_gate, ffn1_w_up, ffn1_w_down, mix_norm, w_in, ret_norm_g, hgrn_lb_logits, hgrn_norm_g, w_out, ffn2_norm, ffn2_w_gate, ffn2_w_up, ffn2_w_down, final_norm, loss_target, m_ffn1_norm, m_ffn1_w_gate, m_ffn1_w_up, m_ffn1_w_down, m_mix_norm, m_w_in, m_ret_norm_g, m_hgrn_lb_logits, m_hgrn_norm_g, m_w_out, m_ffn2_norm, m_ffn2_w_gate, m_ffn2_w_up, m_ffn2_w_down, m_final_norm, v_ffn1_norm, v_ffn1_w_gate, v_ffn1_w_up, v_ffn1_w_down, v_mix_norm, v_w_in, v_ret_norm_g, v_hgrn_lb_logits, v_hgrn_norm_g, v_w_out, v_ffn2_norm, v_ffn2_w_gate, v_ffn2_w_up, v_ffn2_w_down, v_final_norm):
    xs = x[0]
    target = loss_target[0]
    s, d = xs.shape
    f_loc = ffn1_w_gate.shape[2]
    fp = _round_up(f_loc, LANE)
    pad_cols = lambda t: jnp.pad(t[0], ((0, 0), (0, fp - f_loc)))
    pad_rows = lambda t: jnp.pad(t[0], ((0, fp - f_loc), (0, 0)))

    mat_names = ["ffn1_w_gate", "ffn1_w_up", "ffn1_w_down", "w_in", "w_out", "ffn2_w_gate", "ffn2_w_up", "ffn2_w_down"]
    mat_pad = [pad_cols, pad_cols, pad_rows, lambda t: t[0], lambda t: t[0], pad_cols, pad_cols, pad_rows]
    mat_w = [ffn1_w_gate, ffn1_w_up, ffn1_w_down, w_in, w_out, ffn2_w_gate, ffn2_w_up, ffn2_w_down]
    mat_m = [m_ffn1_w_gate, m_ffn1_w_up, m_ffn1_w_down, m_w_in, m_w_out, m_ffn2_w_gate, m_ffn2_w_up, m_ffn2_w_down]
    mat_v = [v_ffn1_w_gate, v_ffn1_w_up, v_ffn1_w_down, v_w_in, v_w_out, v_ffn2_w_gate, v_ffn2_w_up, v_ffn2_w_down]

    cx, cy, cc = _position()
    parity = jnp.reshape(cc, (1,)).astype(jnp.int32)
    chip = jnp.reshape(2 * cx + cy, (1,)).astype(jnp.int32)
    mat_index = {nm: i for i, nm in enumerate(mat_names)}
    mat_out = {}

    def pair_sums(names, grads, from_sibling):
        return [_pair_sum_call(g, r, parity, "pair_sum_" + nm) for nm, g, r in zip(names, grads, from_sibling)]

    def update(names, sums, from_chips):
        for nm, (p, _), r in zip(names, sums, from_chips):
            i = mat_index[nm]
            res = _adamw_matrix_call(p, r, chip, mat_w[i][0], mat_m[i][0], mat_v[i][0], "adamw_" + nm)
            mat_out[nm] = [t[None] for t in res]

    shards = [p(t.astype(BF)) for p, t in zip(mat_pad, mat_w)]
    wg1, wu1 = _all_gather_call(shards[:2], "gather_ffn1_up")

    h1 = _rmsnorm_call(xs, ffn1_norm, "ffn1_norm")
    (g1, u1, a1), (wd1, win) = _ffn_up_call(h1, wg1, wu1, "ffn1_up", comm=_gather_two_level(shards[2:4], 0.7))
    (x1, h2), _ = _down_call(a1, wd1, xs, mix_norm, FFN_RESIDUAL_WEIGHT, "ffn1_down")
    (proj,), landed_a = _proj_call(h2, win, "mix_in", comm=_gather_round1([shards[4], shards[7]]))
    wmix = proj.shape[1] // 8
    cos, sin = _rope_tables(s, wmix // RET_HEADS)
    consts = _ret_consts(wmix // RET_HEADS)
    (o_hg, m_hg, st_hg), landed_b = _hgrn_fwd_call(proj, hgrn_lb_logits, hgrn_norm_g, "hgrn_fwd",
                                                   comm=_gather_round1(shards[5:7]))
    (o_ret, m_ret, st_ret), (wout, wd2, wg2, wu2) = _ret_fwd_call(proj, cos, sin, consts, ret_norm_g, "ret_fwd",
                                                                  comm=_gather_round2(landed_a + landed_b))
    merged = jnp.concatenate([m_ret, m_hg], axis=1)
    wout_wide = wout.reshape(2, wout.shape[0] * wout.shape[1] // 2, d)
    (x2, h3), _ = _down_call(merged, wout_wide, x1, ffn2_norm, 1.0, "mix_out")
    (g2, u2, a2), _ = _ffn_up_call(h3, wg2, wu2, "ffn2_up")
    (x3,), _ = _down_call(a2, wd2, x2, None, FFN_RESIDUAL_WEIGHT, "ffn2_down")
    loss_part, dx3, dx3b, gv_final = _loss_call(x3, target, final_norm[None, :], "loss_head")

    (dg2, du2), _ = _bwd_up_call(dx3b, wd2, g2, u2, FFN_RESIDUAL_WEIGHT, "ffn2_bwd_up")
    dh3, _ = _bwd_mm_call([(dg2, wg2), (du2, wu2)], "ffn2_bwd_down")
    dx2, dx2b, gv_n3 = _norm_bwd_call(dh3, dx3, x2, ffn2_norm, "ffn2_norm_bwd")
    names_a = ["ffn2_w_gate", "ffn2_w_up", "ffn2_w_down"]
    grads_a = [_wgrad_call(h3, dg2, N_DEV, False, 1.0, "ffn2_wgrad_gate"),
               _wgrad_call(h3, du2, N_DEV, False, 1.0, "ffn2_wgrad_up"),
               _wgrad_call(a2, dx3b, N_DEV, True, FFN_RESIDUAL_WEIGHT, "ffn2_wgrad_down")]

    (dmerged,), sib_a = _proj_call(dx2b, jnp.swapaxes(wout_wide, 1, 2), "mix_out_bwd",
                                   comm=_sibling_exchange(grads_a))
    gm_out = _wgrad_call(merged, dx2b, 2, True, 1.0, "mix_out_wgrad").reshape(wout.shape)
    sums_a = pair_sums(names_a, grads_a, sib_a)
    dproj_half, gv_ret = _ret_bwd_call(proj, cos, sin, consts, ret_norm_g, o_ret, st_ret, dmerged, "ret_bwd")
    (dproj, gv_lb, gv_hg), landed = _hgrn_bwd_call(
        proj, hgrn_lb_logits, hgrn_norm_g, o_hg, st_hg, dmerged, dproj_half, "hgrn_bwd",
        comm=_join([_chip_exchange([pb for _, pb in sums_a]), _sibling_exchange([gm_out])]))
    update(names_a, sums_a, landed[:3])
    sums_out = pair_sums(["w_out"], [gm_out], landed[3:])
    dh2, chips_out = _bwd_mm_call([(dproj, win)], "mix_in_bwd", per_step=2, comm=_chip_exchange([sums_out[0][1]]))
    dx1, dx1b, gv_n2 = _norm_bwd_call(dh2, dx2, x1, mix_norm, "mix_norm_bwd")
    update(["w_out"], sums_out, chips_out)

    gm_in = _wgrad_call(h2, dproj, N_DEV, False, 1.0, "mix_in_wgrad")
    gm_d1, sib_in = _wgrad_call(a1, dx1b, N_DEV, True, FFN_RESIDUAL_WEIGHT, "ffn1_wgrad_down",
                                comm=_sibling_exchange([gm_in]))
    sums_in = pair_sums(["w_in"], [gm_in], sib_in)
    (dg1, du1), landed = _bwd_up_call(dx1b, wd1, g1, u1, FFN_RESIDUAL_WEIGHT, "ffn1_bwd_up",
                                      comm=_join([_chip_exchange([sums_in[0][1]]), _sibling_exchange([gm_d1])]))
    update(["w_in"], sums_in, landed[:1])
    sums_d1 = pair_sums(["ffn1_w_down"], [gm_d1], landed[1:])
    gm_g1, chips_d1 = _wgrad_call(h1, dg1, N_DEV, False, 1.0, "ffn1_wgrad_gate",
                                  comm=_chip_exchange([sums_d1[0][1]]))
    update(["ffn1_w_down"], sums_d1, chips_d1)
    gm_u1, sib_g = _wgrad_call(h1, du1, N_DEV, False, 1.0, "ffn1_wgrad_up", comm=_sibling_exchange([gm_g1]))
    sums_g = pair_sums(["ffn1_w_gate"], [gm_g1], sib_g)
    n_row_tiles = s // _tile(s, 1024)
    assert n_row_tiles >= 2, "the sequence must span at least two row tiles"
    n_first = n_row_tiles // 2
    dh1_part, landed = _bwd_mm_call(
        [(dg1, wg1), (du1, wu1)], "ffn1_bwd_down_a", tiles=(0, n_first),
        comm=_join([_chip_exchange([sums_g[0][1]]), _sibling_exchange([gm_u1])]))
    update(["ffn1_w_gate"], sums_g, landed[:1])
    sums_u = pair_sums(["ffn1_w_up"], [gm_u1], landed[1:])
    dh1, chips_u = _bwd_mm_call(
        [(dg1, wg1), (du1, wu1)], "ffn1_bwd_down_b", tiles=(n_first, n_row_tiles - n_first),
        carry=dh1_part, comm=_chip_exchange([sums_u[0][1]]))
    update(["ffn1_w_up"], sums_u, chips_u)
    dx0, _, gv_n1 = _norm_bwd_call(dh1, dx1, xs, ffn1_norm, "ffn1_norm_bwd")

    vec_names = ["ffn1_norm", "mix_norm", "ret_norm_g", "hgrn_lb_logits", "hgrn_norm_g", "ffn2_norm", "final_norm"]
    vec_g = [gv_n1, gv_n2, gv_ret, gv_lb, gv_hg, gv_n3, gv_final]
    vec_w = [ffn1_norm, mix_norm, ret_norm_g, hgrn_lb_logits, hgrn_norm_g, ffn2_norm, final_norm[None, :]]
    vec_m = [m_ffn1_norm, m_mix_norm, m_ret_norm_g, m_hgrn_lb_logits, m_hgrn_norm_g, m_ffn2_norm, m_final_norm[None, :]]
    vec_v = [v_ffn1_norm, v_mix_norm, v_ret_norm_g, v_hgrn_lb_logits, v_hgrn_norm_g, v_ffn2_norm, v_final_norm[None, :]]
    cat = lambda ts: jnp.concatenate(ts, axis=1)
    (vec_all,) = _all_gather_call([cat(vec_g)], "gather_vector_grads")
    vres = _adamw_vector_call(vec_all[:, 0, :], cat(vec_w), cat(vec_m), cat(vec_v), "adamw_vectors")
    vec_out = {}
    off = 0
    for nm, t in zip(vec_names, vec_w):
        n = t.shape[1]
        parts = [r[:, off:off + n] for r in vres]
        if nm == "final_norm":
            parts = [p[0] for p in parts]
        vec_out[nm] = parts
        off += n

    loss = lax.psum(loss_part[0, 0], ("x", "y", "c"))
    order = ["ffn1_norm", "ffn1_w_gate", "ffn1_w_up", "ffn1_w_down", "mix_norm", "w_in", "ret_norm_g", "hgrn_lb_logits",
             "hgrn_norm_g", "w_out", "ffn2_norm", "ffn2_w_gate", "ffn2_w_up", "ffn2_w_down", "final_norm"]
    res = {**mat_out, **vec_out}
    outs = [loss, dx0[None]]
    for kind in range(4):
        outs += [res[nm][kind] for nm in order]
    return tuple(outs)
```

```python
import functools

import jax
import jax.numpy as jnp
from jax import lax
from jax.experimental import pallas as pl
from jax.experimental.pallas import tpu as pltpu

BF = jnp.bfloat16
F32 = jnp.float32
MESH = pl.DeviceIdType.MESH
HBM_SPEC = pl.BlockSpec(memory_space=pltpu.HBM)

N_DEV = 8
LANE = 128
EPS = 1e-6
ROPE_BASE = 10000.0
RET_HEADS = 4
HGRN_HEADS = 8
RET_CHUNK = 128
HGRN_BLOCK = 16
FFN_RESIDUAL_WEIGHT = 0.5
ADAM_LR = 0.001
ADAM_B1 = 0.9
ADAM_B2 = 0.999
ADAM_EPS = 1e-08
ADAM_WD = 0.01
ADAM_STEP = 10
VMEM_LIMIT = 56 * 1024 * 1024


def _tile(n, pref, mult=8):
    t = min(pref, n)
    t -= t % mult
    while t >= mult:
        if n % t == 0:
            return t
        t -= mult
    return n


def _params(*sem):
    return pltpu.CompilerParams(dimension_semantics=sem, vmem_limit_bytes=VMEM_LIMIT)


class _Comm:
    def __init__(self, operands, out_shape, n_sems, start, finish, aliases=None, middle=None, middle_at=0.0):
        self.operands = list(operands)
        self.out_shape = list(out_shape)
        self.n_sems = n_sems
        self.start = start
        self.finish = finish
        self.aliases = dict(aliases or {})
        self.middle = middle
        self.middle_at = middle_at


def _launch(body, *, name, grid, in_specs, out_specs, out_shape, sem, args, scratch_shapes=(), aliases=None, comm=None):
    in_specs, out_specs, out_shape = list(in_specs), list(out_specs), list(out_shape)
    scratch_shapes = list(scratch_shapes)
    aliases = dict(aliases or {})
    if comm is None:
        res = pl.pallas_call(body, name=name, grid=grid, in_specs=in_specs, out_specs=out_specs, out_shape=out_shape,
                             scratch_shapes=scratch_shapes, input_output_aliases=aliases,
                             compiler_params=_params(*sem))(*args)
        return list(res), []
    n_in, n_out, n_scr = len(in_specs), len(out_specs), len(scratch_shapes)
    ci, co = len(comm.operands), len(comm.out_shape)

    def carrying(*refs):
        bounds = [0, n_in, n_in + ci, n_in + ci + n_out, n_in + ci + n_out + co, n_in + ci + n_out + co + n_scr]
        ins, cins, outs, couts, scr = [refs[a:b] for a, b in zip(bounds[:-1], bounds[1:])]
        send_sems, recv_sems = refs[bounds[-1]:]
        ids = [pl.program_id(k) for k in range(len(grid))]
        first = functools.reduce(jnp.logical_and, [i == 0 for i in ids])
        last = functools.reduce(jnp.logical_and, [i == g - 1 for i, g in zip(ids, grid)])

        @pl.when(first)
        def _():
            comm.start(cins, couts, send_sems, recv_sems)

        if comm.middle is not None:
            step, total = ids[0], grid[0]
            for i, g in zip(ids[1:], grid[1:]):
                step, total = step * g + i, total * g

            @pl.when(step == int(total * comm.middle_at))
            def _():
                comm.middle(cins, couts, send_sems, recv_sems)

        body(*ins, *outs, *scr)

        @pl.when(last)
        def _():
            comm.finish(cins, couts, send_sems, recv_sems)

    res = pl.pallas_call(
        carrying, name=name, grid=grid,
        in_specs=in_specs + [HBM_SPEC] * ci, out_specs=out_specs + [HBM_SPEC] * co,
        out_shape=out_shape + comm.out_shape,
        scratch_shapes=scratch_shapes + [pltpu.SemaphoreType.DMA((comm.n_sems,)), pltpu.SemaphoreType.DMA((comm.n_sems,))],
        input_output_aliases={**aliases, **{n_in + a: n_out + b for a, b in comm.aliases.items()}},
        compiler_params=_params(*(["arbitrary"] * len(grid))),
    )(*args, *comm.operands)
    return list(res[:n_out]), list(res[n_out:])


def _comm_only_call(comm, name):
    def body(*refs):
        ci, co = len(comm.operands), len(comm.out_shape)
        cins, couts = refs[:ci], refs[ci:ci + co]
        send_sems, recv_sems = refs[ci + co:]
        comm.start(cins, couts, send_sems, recv_sems)
        comm.finish(cins, couts, send_sems, recv_sems)

    return pl.pallas_call(
        body, name=name,
        in_specs=[HBM_SPEC] * len(comm.operands), out_specs=[HBM_SPEC] * len(comm.out_shape),
        out_shape=comm.out_shape,
        scratch_shapes=[pltpu.SemaphoreType.DMA((comm.n_sems,)), pltpu.SemaphoreType.DMA((comm.n_sems,))],
        input_output_aliases=comm.aliases,
    )(*comm.operands)


def _sigmoid(v):
    return 0.5 * jnp.tanh(0.5 * v) + 0.5


def _dot(a, b):
    return jnp.dot(a, b, preferred_element_type=F32)


def _dot_nt(a, b):
    return lax.dot_general(a, b, (((1,), (1,)), ((), ())), preferred_element_type=F32)


def _dot_tn(a, b):
    return lax.dot_general(a, b, (((0,), (0,)), ((), ())), preferred_element_type=F32)


def _rmsnorm_call(x, gain, name):
    s, d = x.shape
    tm = _tile(s, 512)

    def body(x_ref, g_ref, o_ref):
        xv = x_ref[...]
        r = lax.rsqrt(jnp.mean(xv * xv, axis=-1, keepdims=True) + EPS)
        o_ref[...] = (xv * r * g_ref[...]).astype(BF)

    return pl.pallas_call(
        body, name=name, grid=(s // tm,),
        in_specs=[pl.BlockSpec((tm, d), lambda i: (i, 0)), pl.BlockSpec((1, d), lambda i: (0, 0))],
        out_specs=pl.BlockSpec((tm, d), lambda i: (i, 0)),
        out_shape=jax.ShapeDtypeStruct((s, d), BF),
        compiler_params=_params("parallel"),
    )(x, gain)


def _ffn_up_call(h, wg, wu, name, comm=None):
    s, d = h.shape
    nj, _, k = wg.shape
    tm = _tile(s, 1024)

    def body(h_ref, wg_ref, wu_ref, g_ref, u_ref, a_ref):
        hv = h_ref[...]
        g = _dot(hv, wg_ref[...])
        u = _dot(hv, wu_ref[...])
        g_ref[...] = g.astype(BF)
        u_ref[...] = u.astype(BF)
        a_ref[...] = (g * _sigmoid(g) * u).astype(BF)

    act = pl.BlockSpec((tm, k), lambda i, j: (i, j))
    wsp = pl.BlockSpec((None, d, k), lambda i, j: (j, 0, 0))
    return _launch(
        body, name=name, grid=(s // tm, nj),
        in_specs=[pl.BlockSpec((tm, d), lambda i, j: (i, 0)), wsp, wsp],
        out_specs=[act, act, act],
        out_shape=[jax.ShapeDtypeStruct((s, nj * k), BF)] * 3,
        sem=("parallel", "arbitrary"), args=(h, wg, wu), comm=comm)


def _proj_call(h, w, name, comm=None):
    s, d = h.shape
    nj, _, k = w.shape
    tm = _tile(s, 1024)

    def body(h_ref, w_ref, o_ref):
        o_ref[...] = _dot(h_ref[...], w_ref[...])

    return _launch(
        body, name=name, grid=(s // tm, nj),
        in_specs=[pl.BlockSpec((tm, d), lambda i, j: (i, 0)), pl.BlockSpec((None, d, k), lambda i, j: (j, 0, 0))],
        out_specs=[pl.BlockSpec((tm, k), lambda i, j: (i, j))],
        out_shape=[jax.ShapeDtypeStruct((s, nj * k), F32)],
        sem=("parallel", "arbitrary"), args=(h, w), comm=comm)


def _down_call(a, w, resid, gain, scale, name, comm=None):
    s = a.shape[0]
    nj, k, d = w.shape
    tm = _tile(s, 1024)
    strip = _tile(tm, 128)
    cn = _tile(d, 512, LANE)
    with_norm = gain is not None

    def body(*refs):
        if with_norm:
            a_ref, w_ref, r_ref, g_ref, x_ref, h_ref = refs
        else:
            a_ref, w_ref, r_ref, x_ref = refs
        j = pl.program_id(1)

        av = a_ref[...]

        @pl.when(j == 0)
        def _():
            for n0 in range(0, d, cn):
                x_ref[:, n0:n0 + cn] = _dot(av, w_ref[:, n0:n0 + cn])

        @pl.when(j > 0)
        def _():
            for n0 in range(0, d, cn):
                x_ref[:, n0:n0 + cn] += _dot(av, w_ref[:, n0:n0 + cn])

        @pl.when(j == nj - 1)
        def _():
            for r0 in range(0, tm, strip):
                rows = slice(r0, r0 + strip)
                xn = r_ref[rows, :] + (scale * x_ref[rows, :])
                x_ref[rows, :] = xn
                if with_norm:
                    r = lax.rsqrt(jnp.mean(xn * xn, axis=-1, keepdims=True) + EPS)
                    h_ref[rows, :] = (xn * r * g_ref[...]).astype(BF)

    row = pl.BlockSpec((tm, d), lambda i, j: (i, 0))
    in_specs = [pl.BlockSpec((tm, k), lambda i, j: (i, j)), pl.BlockSpec((None, k, d), lambda i, j: (j, 0, 0)), row]
    args = [a, w, resid]
    out_specs = [row]
    out_shape = [jax.ShapeDtypeStruct((s, d), F32)]
    if with_norm:
        in_specs.append(pl.BlockSpec((1, d), lambda i, j: (0, 0)))
        args.append(gain)
        out_specs.append(row)
        out_shape.append(jax.ShapeDtypeStruct((s, d), BF))
    return _launch(
        body, name=name, grid=(s // tm, nj),
        in_specs=in_specs, out_specs=out_specs, out_shape=out_shape,
        sem=("parallel", "arbitrary"), args=args, comm=comm)


def _loss_call(x, target, gain, name):
    s, d = x.shape
    tm = _tile(s, 512)

    def body(x_ref, t_ref, g_ref, loss_ref, dx_ref, dxb_ref, dg_ref):
        i = pl.program_id(0)

        @pl.when(i == 0)
        def _():
            loss_ref[...] = jnp.zeros_like(loss_ref)
            dg_ref[...] = jnp.zeros_like(dg_ref)

        xv = x_ref[...]
        gv = g_ref[...]
        r = lax.rsqrt(jnp.mean(xv * xv, axis=-1, keepdims=True) + EPS)
        xhat = xv * r
        err = xhat * gv - t_ref[...]
        per_tok = jnp.mean(err * err, axis=-1, keepdims=True)
        loss_ref[...] += 0.5 * jnp.sum(per_tok, axis=0, keepdims=True)
        dout = err * (1.0 / d)
        dg_ref[...] += jnp.sum(dout * xhat, axis=0, keepdims=True)
        dxhat = dout * gv
        dx = r * (dxhat - xhat * jnp.mean(dxhat * xhat, axis=-1, keepdims=True))
        dx_ref[...] = dx
        dxb_ref[...] = dx.astype(BF)

    row = pl.BlockSpec((tm, d), lambda i: (i, 0))
    vec = pl.BlockSpec((1, d), lambda i: (0, 0))
    return pl.pallas_call(
        body, name=name, grid=(s // tm,),
        in_specs=[row, row, vec],
        out_specs=[pl.BlockSpec((1, 1), lambda i: (0, 0)), row, row, vec],
        out_shape=[jax.ShapeDtypeStruct((1, 1), F32), jax.ShapeDtypeStruct((s, d), F32),
                   jax.ShapeDtypeStruct((s, d), BF), jax.ShapeDtypeStruct((1, d), F32)],
        compiler_params=_params("arbitrary"),
    )(x, target, gain)


def _bwd_up_call(dy, wd, g, u, scale, name, comm=None):
    s, d = dy.shape
    nj, k, _ = wd.shape
    tm = _tile(s, 1024)

    def body(dy_ref, w_ref, g_ref, u_ref, dg_ref, du_ref):
        da = scale * _dot_nt(dy_ref[...], w_ref[...])
        gv = g_ref[...].astype(F32)
        sig = _sigmoid(gv)
        du_ref[...] = (da * gv * sig).astype(BF)
        dg_ref[...] = (da * u_ref[...].astype(F32) * sig * (1.0 + gv * (1.0 - sig))).astype(BF)

    act = pl.BlockSpec((tm, k), lambda i, j: (i, j))
    return _launch(
        body, name=name, grid=(s // tm, nj),
        in_specs=[pl.BlockSpec((tm, d), lambda i, j: (i, 0)), pl.BlockSpec((None, k, d), lambda i, j: (j, 0, 0)), act, act],
        out_specs=[act, act],
        out_shape=[jax.ShapeDtypeStruct((s, nj * k), BF), jax.ShapeDtypeStruct((s, nj * k), BF)],
        sem=("parallel", "arbitrary"), args=(dy, wd, g, u), comm=comm)


def _norm_bwd_call(dh, dres, xin, gain, name):
    s, d = xin.shape
    tm = _tile(s, 256)

    def body(dh_ref, dres_ref, x_ref, g_ref, dx_ref, dxb_ref, dg_ref):
        @pl.when(pl.program_id(0) == 0)
        def _():
            dg_ref[...] = jnp.zeros_like(dg_ref)

        xv = x_ref[...]
        r = lax.rsqrt(jnp.mean(xv * xv, axis=-1, keepdims=True) + EPS)
        xhat = xv * r
        dh_v = dh_ref[...]
        dg_ref[...] += jnp.sum(dh_v * xhat, axis=0, keepdims=True)
        dxhat = dh_v * g_ref[...]
        dx = dres_ref[...] + r * (dxhat - xhat * jnp.mean(dxhat * xhat, axis=-1, keepdims=True))
        dx_ref[...] = dx
        dxb_ref[...] = dx.astype(BF)

    row = pl.BlockSpec((tm, d), lambda i: (i, 0))
    vec = pl.BlockSpec((1, d), lambda i: (0, 0))
    return pl.pallas_call(
        body, name=name, grid=(s // tm,),
        in_specs=[row, row, row, vec], out_specs=[row, row, vec],
        out_shape=[jax.ShapeDtypeStruct((s, d), F32), jax.ShapeDtypeStruct((s, d), BF), jax.ShapeDtypeStruct((1, d), F32)],
        compiler_params=_params("arbitrary"),
    )(dh, dres, xin, gain)


def _bwd_mm_call(pairs, name, per_step=1, tiles=None, carry=None, comm=None):
    a0 = pairs[0][0]
    s = a0.shape[-2]
    nblocks, d, k = pairs[0][1].shape
    nj = nblocks // per_step
    npair = len(pairs) * per_step
    tm = _tile(s, 1024)
    cn = _tile(d, 512, LANE)

    first_tile, n_tiles = (0, s // tm) if tiles is None else tiles

    def body(*refs):
        a_refs = refs[0:2 * npair:2]
        w_refs = refs[1:2 * npair:2]
        dh_ref = refs[-1]

        @pl.when(pl.program_id(1) == 0)
        def _():
            dh_ref[...] = jnp.zeros_like(dh_ref)

        for a_ref, w_ref in zip(a_refs, w_refs):
            av = a_ref[...]
            for n0 in range(0, d, cn):
                dh_ref[:, n0:n0 + cn] += _dot_nt(av, w_ref[n0:n0 + cn, :])

    row = pl.BlockSpec((tm, d), lambda i, j: (first_tile + i, 0))
    in_specs, args = [], []
    for a, w in pairs:
        for r in range(per_step):
            in_specs += [pl.BlockSpec((None, tm, k), lambda i, j, r=r: (j * per_step + r, first_tile + i, 0))
                         if a.ndim == 3 else
                         pl.BlockSpec((tm, k), lambda i, j, r=r: (first_tile + i, j * per_step + r)),
                         pl.BlockSpec((None, d, k), lambda i, j, r=r: (j * per_step + r, 0, 0))]
            args += [a, w]
    aliases = {}
    if carry is not None:
        aliases = {len(in_specs): 0}
        in_specs.append(pl.BlockSpec(memory_space=pl.ANY))
        args.append(carry)
    (dh,), landed = _launch(
        body, name=name, grid=(n_tiles, nj),
        in_specs=in_specs, out_specs=[row], out_shape=[jax.ShapeDtypeStruct((s, d), F32)],
        sem=("parallel", "arbitrary"), args=args, aliases=aliases, comm=comm)
    return dh, landed


def _wgrad_call(a, b, nj, a_blocked, scale, name, comm=None):
    s = a.shape[0]
    ka = a.shape[1] // nj if a_blocked else a.shape[1]
    b_stacked = b.ndim == 3
    kb = b.shape[-1] if (a_blocked or b_stacked) else b.shape[1] // nj
    ts = _tile(s, 2048)
    ns = s // ts

    def body(a_ref, b_ref, o_ref):
        t = pl.program_id(1)

        @pl.when(t == 0)
        def _():
            o_ref[...] = _dot_tn(a_ref[...], b_ref[...])

        @pl.when(t > 0)
        def _():
            o_ref[...] += _dot_tn(a_ref[...], b_ref[...])

        if scale != 1.0:
            @pl.when(t == ns - 1)
            def _():
                o_ref[...] = o_ref[...] * scale

    a_spec = pl.BlockSpec((ts, ka), (lambda j, t: (t, j)) if a_blocked else (lambda j, t: (t, 0)))
    if b_stacked:
        b_spec = pl.BlockSpec((None, ts, kb), lambda j, t: (j, t, 0))
    else:
        b_spec = pl.BlockSpec((ts, kb), (lambda j, t: (t, 0)) if a_blocked else (lambda j, t: (t, j)))
    (out,), landed = _launch(
        body, name=name, grid=(nj, ns),
        in_specs=[a_spec, b_spec],
        out_specs=[pl.BlockSpec((None, ka, kb), lambda j, t: (j, 0, 0))],
        out_shape=[jax.ShapeDtypeStruct((nj, ka, kb), F32)],
        sem=("parallel", "arbitrary"), args=(a, b), comm=comm)
    return out if comm is None else (out, landed)


def _rope(v, cos, sin):
    half = v.shape[-1] // 2
    v1, v2 = v[:, :half], v[:, half:]
    return jnp.concatenate([v1 * cos - v2 * sin, v2 * cos + v1 * sin], axis=-1)


def _rope_bwd(dv, cos, sin):
    half = dv.shape[-1] // 2
    d1, d2 = dv[:, :half], dv[:, half:]
    return jnp.concatenate([d1 * cos + d2 * sin, d2 * cos - d1 * sin], axis=-1)


def _ret_consts(hd):
    c = RET_CHUNK
    log_gamma = jnp.log(1.0 - jnp.exp2(-5.0 - jnp.arange(RET_HEADS, dtype=F32)))
    idx = jnp.arange(c, dtype=F32)
    rel = idx[:, None] - idx[None, :]
    mask = rel >= 0
    decay = jnp.where(mask[None], jnp.exp(log_gamma[:, None, None] * jnp.where(mask, rel, 0.0)[None]), 0.0)
    qdec = jnp.exp(log_gamma[:, None] * (idx + 1.0)[None, :])
    kdec = jnp.exp(log_gamma[:, None] * (c - 1.0 - idx)[None, :])
    gchunk = jnp.exp(log_gamma * c)
    bc = lambda t: jnp.broadcast_to(t[:, :, None], (RET_HEADS, t.shape[1], hd))
    return decay, bc(qdec), bc(kdec), bc(gchunk[:, None])


def _rope_tables(s, hd):
    inv = jnp.power(ROPE_BASE, -jnp.arange(0, hd, 2, dtype=F32) / hd)
    ang = jnp.arange(s, dtype=F32)[:, None] * inv[None, :]
    return jnp.cos(ang), jnp.sin(ang)


def _ret_fwd_call(proj, cos, sin, consts, gret, name, comm=None):
    s = proj.shape[0]
    w = proj.shape[1] // 8
    hd = w // RET_HEADS
    c = RET_CHUNK
    tt = _tile(s, 512, c)
    nc = tt // c
    decay, qdec, kdec, gch = consts
    scale = hd ** -0.5

    def body(q_ref, k_ref, v_ref, gate_ref, cos_ref, sin_ref, dec_ref, qd_ref, kd_ref, gc_ref, gn_ref,
             o_ref, m_ref, st_ref, state):
        @pl.when(pl.program_id(1) == 0)
        def _():
            state[...] = jnp.zeros_like(state)

        dec = dec_ref[...]
        for ci in range(nc):
            rows = slice(ci * c, (ci + 1) * c)
            cs, sn = cos_ref[rows, :], sin_ref[rows, :]
            q = _rope(q_ref[rows, :], cs, sn) * scale
            k = _rope(k_ref[rows, :], cs, sn)
            vb = v_ref[rows, :].astype(BF)
            sc = _dot_nt(q.astype(BF), k.astype(BF)) * dec
            prev = state[...]
            st_ref[ci] = prev
            o = _dot(sc.astype(BF), vb) + _dot((q * qd_ref[...]).astype(BF), prev.astype(BF))
            state[...] = gc_ref[...] * prev + _dot_tn((k * kd_ref[...]).astype(BF), vb)
            o_ref[rows, :] = o
            mu = jnp.mean(o, axis=-1, keepdims=True)
            cen = o - mu
            xhat = cen * lax.rsqrt(jnp.mean(cen * cen, axis=-1, keepdims=True) + EPS)
            gt = gate_ref[rows, :]
            m_ref[rows, :] = (xhat * gn_ref[...] * (gt * _sigmoid(gt))).astype(BF)

    nh = RET_HEADS
    comp = lambda j: pl.BlockSpec((tt, hd), lambda h, t, j=j: (t, j * nh + h))
    tab = pl.BlockSpec((tt, hd // 2), lambda h, t: (t, 0))
    per_head = lambda r: pl.BlockSpec((None, r, hd), lambda h, t: (h, 0, 0))
    return _launch(
        body, name=name, grid=(nh, s // tt),
        in_specs=[comp(0), comp(1), comp(2), comp(3), tab, tab,
                  pl.BlockSpec((None, c, c), lambda h, t: (h, 0, 0)), per_head(c), per_head(c), per_head(1),
                  pl.BlockSpec((1, hd), lambda h, t: (0, h))],
        out_specs=[pl.BlockSpec((tt, hd), lambda h, t: (t, h)), pl.BlockSpec((tt, hd), lambda h, t: (t, h)),
                   pl.BlockSpec((None, nc, hd, hd), lambda h, t: (h, t, 0, 0))],
        out_shape=[jax.ShapeDtypeStruct((s, w), F32), jax.ShapeDtypeStruct((s, w), BF),
                   jax.ShapeDtypeStruct((nh, s // c, hd, hd), F32)],
        scratch_shapes=[pltpu.VMEM((hd, hd), F32)],
        sem=("parallel", "arbitrary"),
        args=(proj, proj, proj, proj, cos, sin, decay, qdec, kdec, gch, gret), comm=comm)


def _ret_bwd_call(proj, cos, sin, consts, gret, o_raw, states, dmerged, name):
    s = proj.shape[0]
    w = proj.shape[1] // 8
    hd = w // RET_HEADS
    c = RET_CHUNK
    tt = _tile(s, 512, c)
    nc = tt // c
    nt = s // tt
    decay, qdec, kdec, gch = consts
    scale = hd ** -0.5

    def body(q_ref, k_ref, v_ref, gate_ref, cos_ref, sin_ref, dec_ref, qd_ref, kd_ref, gc_ref, gn_ref,
             o_ref, st_ref, dm_ref, dp_ref, dgn_ref, dstate):
        @pl.when(pl.program_id(1) == 0)
        def _():
            dstate[...] = jnp.zeros_like(dstate)
            dgn_ref[...] = jnp.zeros_like(dgn_ref)

        dec = dec_ref[...]
        gn = gn_ref[...]
        for ci in reversed(range(nc)):
            rows = slice(ci * c, (ci + 1) * c)
            cs, sn = cos_ref[rows, :], sin_ref[rows, :]
            q = _rope(q_ref[rows, :], cs, sn) * scale
            k = _rope(k_ref[rows, :], cs, sn)
            qb, kb = q.astype(BF), k.astype(BF)
            vb = v_ref[rows, :].astype(BF)
            sc = _dot_nt(qb, kb) * dec
            o = o_ref[rows, :]
            mu = jnp.mean(o, axis=-1, keepdims=True)
            cen = o - mu
            rstd = lax.rsqrt(jnp.mean(cen * cen, axis=-1, keepdims=True) + EPS)
            xhat = cen * rstd
            gt = gate_ref[rows, :]
            sig = _sigmoid(gt)
            sg = gt * sig
            dm = dm_ref[rows, :]
            dgn_ref[...] += jnp.sum(dm * xhat * sg, axis=0, keepdims=True)
            dp_ref[3, rows, :] = (dm * xhat * gn * sig * (1.0 + gt * (1.0 - sig))).astype(BF)
            dxhat = dm * gn * sg
            do = rstd * (dxhat - jnp.mean(dxhat, axis=-1, keepdims=True)
                         - xhat * jnp.mean(dxhat * xhat, axis=-1, keepdims=True))
            dob = do.astype(BF)
            prev = st_ref[ci]
            ds = dstate[...]
            dsb = ds.astype(BF)
            dsc = (_dot_nt(dob, vb) * dec).astype(BF)
            dq = _dot(dsc, kb) + _dot_nt(dob, prev.astype(BF)) * qd_ref[...]
            dk = _dot_tn(dsc, qb) + _dot_nt(vb, dsb) * kd_ref[...]
            dv = _dot_tn(sc.astype(BF), dob) + _dot((k * kd_ref[...]).astype(BF), dsb)
            dstate[...] = gc_ref[...] * ds + _dot_tn((q * qd_ref[...]).astype(BF), dob)
            dp_ref[0, rows, :] = _rope_bwd(dq * scale, cs, sn).astype(BF)
            dp_ref[1, rows, :] = _rope_bwd(dk, cs, sn).astype(BF)
            dp_ref[2, rows, :] = dv.astype(BF)

    nh = RET_HEADS
    rev = lambda t: nt - 1 - t
    comp = lambda j: pl.BlockSpec((tt, hd), lambda h, t, j=j: (rev(t), j * nh + h))
    tab = pl.BlockSpec((tt, hd // 2), lambda h, t: (rev(t), 0))
    per_head = lambda r: pl.BlockSpec((None, r, hd), lambda h, t: (h, 0, 0))
    head_cols = pl.BlockSpec((tt, hd), lambda h, t: (rev(t), h))
    gvec = pl.BlockSpec((1, hd), lambda h, t: (0, h))
    act = jax.ShapeDtypeStruct((s, w), BF)
    return pl.pallas_call(
        body, name=name, grid=(nh, nt),
        in_specs=[comp(0), comp(1), comp(2), comp(3), tab, tab,
                  pl.BlockSpec((None, c, c), lambda h, t: (h, 0, 0)), per_head(c), per_head(c), per_head(1), gvec,
                  head_cols, pl.BlockSpec((None, nc, hd, hd), lambda h, t: (h, rev(t), 0, 0)), head_cols],
        out_specs=[pl.BlockSpec((4, tt, hd), lambda h, t: (0, rev(t), h)), gvec],
        out_shape=[jax.ShapeDtypeStruct((8, s, w), BF), jax.ShapeDtypeStruct((1, w), F32)],
        scratch_shapes=[pltpu.VMEM((hd, hd), F32)],
        compiler_params=_params("parallel", "arbitrary"),
    )(proj, proj, proj, proj, cos, sin, decay, qdec, kdec, gch, gret, o_raw, states, dmerged)


def _block_tri(n, bs, upper):
    r = jnp.arange(n)[:, None]
    cidx = jnp.arange(n)[None, :]
    same = (r // bs) == (cidx // bs)
    return jnp.where(same & ((cidx >= r) if upper else (cidx <= r)), 1.0, 0.0).astype(BF)


def _dot_exact(tri, v):
    hi = v.astype(BF)
    rest = v - hi.astype(F32)
    mid = rest.astype(BF)
    lo = (rest - mid.astype(F32)).astype(BF)
    return _dot(tri, hi) + _dot(tri, mid) + _dot(tri, lo)


def _hgrn_gates(z, lbv):
    sz = _sigmoid(z)
    oml = 1.0 - lbv
    f = lbv + oml * sz
    key = oml * (1.0 - sz)
    return sz, f, key


def _hgrn_fwd_call(proj, lb_logits, ghg, name, comm=None):
    s = proj.shape[0]
    w = proj.shape[1] // 8
    nh = HGRN_HEADS
    hd = w // nh
    bs = HGRN_BLOCK
    tt = _tile(s, 256, bs)
    nb = tt // bs

    def body(q_ref, z_ref, v_ref, gate_ref, lb_ref, gn_ref, tril_ref, o_ref, m_ref, st_ref, state, upd):
        @pl.when(pl.program_id(1) == 0)
        def _():
            state[...] = jnp.zeros_like(state)

        lbv = _sigmoid(lb_ref[...])
        _, f, key = _hgrn_gates(z_ref[...], lbv)
        qr = q_ref[...]
        q = qr * _sigmoid(qr)
        v = v_ref[...]
        g = _dot_exact(tril_ref[...], jnp.log(f))
        blocks = lambda t: t.reshape(nb, bs, hd)
        g3, q3, k3, v3 = blocks(g), blocks(q), blocks(key), blocks(v)
        glast3 = g3[:, bs - 1:bs, :]
        row_id = lax.broadcasted_iota(jnp.int32, (nb, bs, hd), 1)
        o3 = jnp.zeros((nb, bs, hd), F32)
        for j in range(bs):
            wj = jnp.where(row_id >= j, jnp.exp(jnp.minimum(g3 - g3[:, j:j + 1, :], 0.0)), 0.0)
            a = jnp.sum(q3 * k3[:, j:j + 1, :] * wj, axis=-1, keepdims=True)
            o3 = o3 + a * v3[:, j:j + 1, :]
        ktb = (k3 * jnp.exp(glast3 - g3)).reshape(tt, hd).astype(BF)
        vb = v.astype(BF)
        for b in range(nb):
            rows = slice(b * bs, (b + 1) * bs)
            upd[b] = _dot_tn(vb[rows, :], ktb[rows, :])
        egl3 = jnp.exp(glast3)
        st = state[...]
        for b in range(nb):
            st_ref[b] = st
            st = st * egl3[b] + upd[b]
        state[...] = st
        qgb = (q * jnp.exp(g)).astype(BF)
        o_intra = o3.reshape(tt, hd)
        gn = gn_ref[...]
        for b in range(nb):
            rows = slice(b * bs, (b + 1) * bs)
            o = o_intra[rows, :] + _dot_nt(qgb[rows, :], st_ref[b].astype(BF))
            o_ref[rows, :] = o
            gt = gate_ref[rows, :]
            xhat = o * lax.rsqrt(jnp.mean(o * o, axis=-1, keepdims=True) + EPS)
            m_ref[rows, :] = (xhat * gn * (gt * _sigmoid(gt))).astype(BF)

    comp = lambda j: pl.BlockSpec((tt, hd), lambda h, t, j=j: (t, j * nh + h))
    gvec = pl.BlockSpec((1, hd), lambda h, t: (0, h))
    head_cols = pl.BlockSpec((tt, hd), lambda h, t: (t, h))
    return _launch(
        body, name=name, grid=(nh, s // tt),
        in_specs=[comp(4), comp(5), comp(6), comp(7), gvec, gvec, pl.BlockSpec((tt, tt), lambda h, t: (0, 0))],
        out_specs=[head_cols, head_cols, pl.BlockSpec((None, nb, hd, hd), lambda h, t: (h, t, 0, 0))],
        out_shape=[jax.ShapeDtypeStruct((s, w), F32), jax.ShapeDtypeStruct((s, w), BF),
                   jax.ShapeDtypeStruct((nh, s // bs, hd, hd), F32)],
        scratch_shapes=[pltpu.VMEM((hd, hd), F32), pltpu.VMEM((nb, hd, hd), F32)],
        sem=("parallel", "arbitrary"),
        args=(proj, proj, proj, proj, lb_logits, ghg, _block_tri(tt, bs, upper=False)), comm=comm)


def _hgrn_bwd_call(proj, lb_logits, ghg, o_raw, states, dmerged, stack, name, comm=None):
    s = proj.shape[0]
    w = proj.shape[1] // 8
    nh = HGRN_HEADS
    hd = w // nh
    bs = HGRN_BLOCK
    tt = _tile(s, 256, bs)
    nb = tt // bs
    nt = s // tt

    def body(q_ref, z_ref, v_ref, gate_ref, lb_ref, gn_ref, tril_ref, triu_ref, o_ref, st_ref, dm_ref, stack_ref,
             dp_ref, dlb_ref, dgn_ref,
             dstate, ds_all, inc, dq_s, dk_s, dv_s, dgl_s, dk_rows, dv_rows):
        @pl.when(pl.program_id(1) == 0)
        def _():
            dstate[...] = jnp.zeros_like(dstate)
            dlb_ref[...] = jnp.zeros_like(dlb_ref)
            dgn_ref[...] = jnp.zeros_like(dgn_ref)

        lbv = _sigmoid(lb_ref[...])
        oml = 1.0 - lbv
        gn = gn_ref[...]
        sz, f, key = _hgrn_gates(z_ref[...], lbv)
        qr = q_ref[...]
        sq = _sigmoid(qr)
        q = qr * sq
        v = v_ref[...]
        g = _dot_exact(tril_ref[...], jnp.log(f))
        eg = jnp.exp(g)
        blocks = lambda t: t.reshape(nb, bs, hd)
        g3, q3, k3, v3 = blocks(g), blocks(q), blocks(key), blocks(v)
        glast3 = g3[:, bs - 1:bs, :]
        egl3 = jnp.exp(glast3)
        ktail3 = jnp.exp(glast3 - g3)
        o = o_ref[...]
        rstd = lax.rsqrt(jnp.mean(o * o, axis=-1, keepdims=True) + EPS)
        xhat = o * rstd
        gt = gate_ref[...]
        sig = _sigmoid(gt)
        sg = gt * sig
        dm = dm_ref[...]
        dgn_ref[...] += jnp.sum(dm * xhat * sg, axis=0, keepdims=True)
        del stack_ref
        dp_ref[3] = (dm * xhat * gn * sig * (1.0 + gt * (1.0 - sig))).astype(BF)
        dxhat = dm * gn * sg
        do = rstd * (dxhat - xhat * jnp.mean(dxhat * xhat, axis=-1, keepdims=True))
        dob = do.astype(BF)
        do3 = blocks(do)
        qgb = (q * eg).astype(BF)
        for b in range(nb):
            rows = slice(b * bs, (b + 1) * bs)
            inc[b] = _dot_tn(dob[rows, :], qgb[rows, :])
        ds = dstate[...]
        for b in reversed(range(nb)):
            ds_all[b] = ds
            ds = ds * egl3[b] + inc[b]
        dstate[...] = ds
        ktb = (k3 * ktail3).reshape(tt, hd).astype(BF)
        vb = v.astype(BF)
        for b in range(nb):
            rows = slice(b * bs, (b + 1) * bs)
            prev = st_ref[b]
            dsb = ds_all[b]
            dsbb = dsb.astype(BF)
            dq_s[rows, :] = _dot(dob[rows, :], prev.astype(BF))
            dk_s[rows, :] = _dot(vb[rows, :], dsbb)
            dv_s[rows, :] = _dot_nt(ktb[rows, :], dsbb)
            dgl_s[b] = jnp.sum(prev * dsb, axis=0, keepdims=True)
        dq3 = blocks(dq_s[...] * eg)
        dk3 = blocks(dk_s[...]) * ktail3
        dg_last3 = jnp.sum(k3 * dk3, axis=1, keepdims=True) + egl3 * dgl_s[...]
        row_id = lax.broadcasted_iota(jnp.int32, (nb, bs, hd), 1)
        for j in range(bs):
            wj = jnp.where(row_id >= j, jnp.exp(jnp.minimum(g3 - g3[:, j:j + 1, :], 0.0)), 0.0)
            kj = k3[:, j:j + 1, :]
            a = jnp.sum(q3 * kj * wj, axis=-1, keepdims=True)
            da = jnp.sum(do3 * v3[:, j:j + 1, :], axis=-1, keepdims=True)
            dv_rows[:, j:j + 1, :] = jnp.sum(a * do3, axis=1, keepdims=True)
            dq3 = dq3 + da * kj * wj
            dk_rows[:, j:j + 1, :] = jnp.sum(da * q3 * wj, axis=1, keepdims=True)
        dk3 = dk3 + dk_rows[...]
        dv = dv_s[...] + dv_rows[...].reshape(tt, hd)
        dg3 = q3 * dq3 - k3 * dk3 + jnp.where(row_id == bs - 1, dg_last3, 0.0)
        dlf = _dot_exact(triu_ref[...], dg3.reshape(tt, hd))
        dk = dk3.reshape(tt, hd)
        dfk = dlf / f - dk
        dlb_ref[...] += jnp.sum(dfk * (1.0 - sz), axis=0, keepdims=True) * (lbv * oml)
        dp_ref[1] = (dfk * oml * sz * (1.0 - sz)).astype(BF)
        dp_ref[0] = (dq3.reshape(tt, hd) * sq * (1.0 + qr * (1.0 - sq))).astype(BF)
        dp_ref[2] = dv.astype(BF)

    rev = lambda t: nt - 1 - t
    comp = lambda j: pl.BlockSpec((tt, hd), lambda h, t, j=j: (rev(t), j * nh + h))
    gvec = pl.BlockSpec((1, hd), lambda h, t: (0, h))
    head_cols = pl.BlockSpec((tt, hd), lambda h, t: (rev(t), h))
    tri = pl.BlockSpec((tt, tt), lambda h, t: (0, 0))
    act = jax.ShapeDtypeStruct((s, w), BF)
    vec = jax.ShapeDtypeStruct((1, w), F32)
    tile_f32 = pltpu.VMEM((tt, hd), F32)
    return _launch(
        body, name=name, grid=(nh, nt),
        in_specs=[comp(4), comp(5), comp(6), comp(7), gvec, gvec, tri, tri, head_cols,
                  pl.BlockSpec((None, nb, hd, hd), lambda h, t: (h, rev(t), 0, 0)),
                  pl.BlockSpec((tt, hd), lambda h, t: (rev(t), nh + h)),
                  pl.BlockSpec(memory_space=pl.ANY)],
        out_specs=[pl.BlockSpec((4, tt, hd), lambda h, t: (1, rev(t), h)), gvec, gvec],
        out_shape=[jax.ShapeDtypeStruct(stack.shape, stack.dtype), vec, vec],
        scratch_shapes=[pltpu.VMEM((hd, hd), F32), pltpu.VMEM((nb, hd, hd), F32), pltpu.VMEM((nb, hd, hd), F32),
                        tile_f32, tile_f32, tile_f32, pltpu.VMEM((nb, 1, hd), F32),
                        pltpu.VMEM((nb, bs, hd), F32), pltpu.VMEM((nb, bs, hd), F32)],
        sem=("parallel", "arbitrary"),
        args=(proj, proj, proj, proj, lb_logits, ghg, _block_tri(tt, bs, upper=False), _block_tri(tt, bs, upper=True),
              o_raw, states, dmerged, stack), aliases={11: 0}, comm=comm)


def _position():
    return lax.axis_index("x"), lax.axis_index("y"), lax.axis_index("c")


def _all_gather_call(shards, name):
    n = len(shards)

    def body(*refs):
        ins, outs = refs[:n], refs[n:2 * n]
        send_sems, recv_sems, local_sems = refs[2 * n:]
        x, y, c = _position()
        me, sibling = (x, y, c), (x, y, 1 - c)
        chips = [(1 - x, y), (x, 1 - y), (1 - x, 1 - y)]

        def slot(a, p):
            return outs[a].at[4 * p[0] + 2 * p[1] + p[2]]

        def copy(a, k, block, to, src=None):
            return pltpu.make_async_remote_copy(
                src_ref=slot(a, block) if src is None else src, dst_ref=slot(a, block),
                send_sem=send_sems.at[a * 7 + k], recv_sem=recv_sems.at[a * 7 + k],
                device_id=to, device_id_type=MESH)

        mine = [pltpu.make_async_copy(ins[a], slot(a, me), local_sems.at[a]) for a in range(n)]
        for cp in mine:
            cp.start()
        first = []
        for a in range(n):
            first.append(copy(a, 0, me, sibling, src=ins[a]))
            first += [copy(a, 1 + j, me, (*chip, c), src=ins[a]) for j, chip in enumerate(chips)]
        for cp in first:
            cp.start()
        passed = []
        for j, chip in enumerate(chips):
            for a in range(n):
                copy(a, 1 + j, (*chip, c), me).wait_recv()
                fwd = copy(a, 4 + j, (*chip, c), sibling)
                fwd.start()
                passed.append(fwd)
        for a in range(n):
            copy(a, 0, sibling, me).wait_recv()
            for j, chip in enumerate(chips):
                copy(a, 4 + j, (*chip, 1 - c), me).wait_recv()
        for cp in first + passed:
            cp.wait_send()
        for cp in mine:
            cp.wait()

    return pl.pallas_call(
        body, name=name,
        in_specs=[HBM_SPEC] * n, out_specs=[HBM_SPEC] * n,
        out_shape=[jax.ShapeDtypeStruct((N_DEV,) + t.shape, t.dtype) for t in shards],
        scratch_shapes=[pltpu.SemaphoreType.DMA((7 * n,)), pltpu.SemaphoreType.DMA((7 * n,)),
                        pltpu.SemaphoreType.DMA((n,))],
    )(*shards)


def _slot(ref, p):
    return ref.at[4 * p[0] + 2 * p[1] + p[2]]


def _gather_round1(shards):
    n = len(shards)

    def plan(ins, outs, send_sems, recv_sems):
        x, y, c = _position()
        me = (x, y, c)
        peers = [(x, y, 1 - c), (1 - x, y, c), (x, 1 - y, c), (1 - x, 1 - y, c)]
        sends, recvs, local = [], [], []
        for a in range(n):
            local.append(pltpu.make_async_copy(ins[a], _slot(outs[a], me), send_sems.at[4 * n + a]))
            for k, peer in enumerate(peers):
                sems = dict(send_sem=send_sems.at[4 * a + k], recv_sem=recv_sems.at[4 * a + k],
                            device_id=peer, device_id_type=MESH)
                sends.append(pltpu.make_async_remote_copy(src_ref=ins[a], dst_ref=_slot(outs[a], me), **sems))
                recvs.append(pltpu.make_async_remote_copy(src_ref=ins[a], dst_ref=_slot(outs[a], peer), **sems))
        return sends, recvs, local

    def start(*refs):
        sends, _, local = plan(*refs)
        for cp in local + sends:
            cp.start()

    def finish(*refs):
        sends, recvs, local = plan(*refs)
        for cp in recvs:
            cp.wait_recv()
        for cp in sends:
            cp.wait_send()
        for cp in local:
            cp.wait()

    return _Comm(shards, [jax.ShapeDtypeStruct((N_DEV,) + t.shape, t.dtype) for t in shards], 5 * n, start, finish)


def _gather_round2(gathered):
    n = len(gathered)

    def plan(ins, outs, send_sems, recv_sems):
        x, y, c = _position()
        chips = [(1 - x, y), (x, 1 - y), (1 - x, 1 - y)]
        sends, recvs = [], []
        for a in range(n):
            for k, chip in enumerate(chips):
                sems = dict(send_sem=send_sems.at[3 * a + k], recv_sem=recv_sems.at[3 * a + k],
                            device_id=(x, y, 1 - c), device_id_type=MESH)
                sends.append(pltpu.make_async_remote_copy(
                    src_ref=_slot(ins[a], (*chip, c)), dst_ref=_slot(outs[a], (*chip, c)), **sems))
                recvs.append(pltpu.make_async_remote_copy(
                    src_ref=_slot(ins[a], (*chip, c)), dst_ref=_slot(outs[a], (*chip, 1 - c)), **sems))
        return sends, recvs

    def start(*refs):
        for cp in plan(*refs)[0]:
            cp.start()

    def finish(*refs):
        sends, recvs = plan(*refs)
        for cp in recvs:
            cp.wait_recv()
        for cp in sends:
            cp.wait_send()

    return _Comm(gathered, [jax.ShapeDtypeStruct(t.shape, t.dtype) for t in gathered], 3 * n, start, finish,
                 aliases={a: a for a in range(n)})


def _gather_two_level(shards, forward_at):
    n = len(shards)
    first, second = _gather_round1(shards), _gather_round2(shards)

    def middle(ins, outs, send_sems, recv_sems):
        first.finish(ins, outs, send_sems, recv_sems)
        second.start(outs, outs, _SemWindow(send_sems, first.n_sems), _SemWindow(recv_sems, first.n_sems))

    def finish(ins, outs, send_sems, recv_sems):
        second.finish(outs, outs, _SemWindow(send_sems, first.n_sems), _SemWindow(recv_sems, first.n_sems))

    return _Comm(shards, first.out_shape, first.n_sems + second.n_sems, first.start, finish,
                 middle=middle, middle_at=forward_at)


def _sibling_exchange(grads):
    n = len(grads)

    def plan(ins, outs, send_sems, recv_sems):
        x, y, c = _position()
        return [pltpu.make_async_remote_copy(
            src_ref=ins[a].at[2 * q + (1 - c)], dst_ref=outs[a].at[q],
            send_sem=send_sems.at[a * 4 + q], recv_sem=recv_sems.at[a * 4 + q],
            device_id=(x, y, 1 - c), device_id_type=MESH) for a in range(n) for q in range(4)]

    def start(*refs):
        for cp in plan(*refs):
            cp.start()

    def finish(*refs):
        for cp in plan(*refs):
            cp.wait()

    return _Comm(grads, [jax.ShapeDtypeStruct((4,) + t.shape[1:], t.dtype) for t in grads], 4 * n, start, finish)


def _chip_exchange(partials):
    n = len(partials)

    def plan(ins, outs, send_sems, recv_sems):
        x, y, c = _position()
        chips = [(1 - x, y), (x, 1 - y), (1 - x, 1 - y)]
        return [pltpu.make_async_remote_copy(
            src_ref=ins[a].at[2 * chip[0] + chip[1]], dst_ref=outs[a].at[k],
            send_sem=send_sems.at[a * 3 + k], recv_sem=recv_sems.at[a * 3 + k],
            device_id=(*chip, c), device_id_type=MESH) for a in range(n) for k, chip in enumerate(chips)]

    def start(*refs):
        for cp in plan(*refs):
            cp.start()

    def finish(*refs):
        for cp in plan(*refs):
            cp.wait()

    return _Comm(partials, [jax.ShapeDtypeStruct((3,) + t.shape[1:], t.dtype) for t in partials], 3 * n, start, finish)


class _SemWindow:
    def __init__(self, sems, offset):
        self._sems, self._offset = sems, offset

    @property
    def at(self):
        return self

    def __getitem__(self, i):
        return self._sems.at[self._offset + i]


def _join(parts):
    def each(fn_name, cins, couts, send_sems, recv_sems):
        i = o = sem = 0
        for p in parts:
            ni, no = len(p.operands), len(p.out_shape)
            getattr(p, fn_name)(cins[i:i + ni], couts[o:o + no], _SemWindow(send_sems, sem), _SemWindow(recv_sems, sem))
            i, o, sem = i + ni, o + no, sem + p.n_sems

    assert not any(p.aliases for p in parts)
    return _Comm([t for p in parts for t in p.operands], [t for p in parts for t in p.out_shape],
                 sum(p.n_sems for p in parts), functools.partial(each, "start"), functools.partial(each, "finish"))


def _pair_sum_call(grad, recv, place, name):
    _, r, ccols = grad.shape
    tr = _tile(r, 256)

    def body(place_ref, g_ref, r_ref, own_ref, pb_ref):
        p = g_ref[...] + r_ref[...]
        pb_ref[...] = p.astype(BF)

        @pl.when(pl.program_id(1) == place_ref[1])
        def _():
            own_ref[...] = p

    blk = lambda fn: pl.BlockSpec((None, tr, ccols), fn)
    return pl.pallas_call(
        body, name=name,
        grid_spec=pltpu.PrefetchScalarGridSpec(
            num_scalar_prefetch=1, grid=(r // tr, 4),
            in_specs=[blk(lambda i, q, place: (2 * q + place[0], i, 0)), blk(lambda i, q, place: (q, i, 0))],
            out_specs=[pl.BlockSpec((tr, ccols), lambda i, q, place: (i, 0)), blk(lambda i, q, place: (q, i, 0))]),
        out_shape=[jax.ShapeDtypeStruct((r, ccols), F32), jax.ShapeDtypeStruct((4, r, ccols), BF)],
        compiler_params=_params("parallel", "arbitrary"),
    )(place, grad, recv)


def _adamw_math(w, g, m, v):
    m = ADAM_B1 * m + (1.0 - ADAM_B1) * g
    v = ADAM_B2 * v + (1.0 - ADAM_B2) * (g * g)
    m_hat = m / (1.0 - ADAM_B1 ** ADAM_STEP)
    v_hat = v / (1.0 - ADAM_B2 ** ADAM_STEP)
    delta = -ADAM_LR * (m_hat / (jnp.sqrt(v_hat) + ADAM_EPS) + ADAM_WD * w)
    return delta, m, v


def _adamw_matrix_call(own, recv, w, m, v, name):
    r, ccols = w.shape
    gcols = own.shape[1]
    tr = _tile(r, 256)

    def body(p_ref, r_ref, w_ref, m_ref, v_ref, g_out, d_out, m_out, v_out):
        cols = pl.ds(0, ccols)
        g = (p_ref[:, cols] + r_ref[0, :, cols].astype(F32) + r_ref[1, :, cols].astype(F32)
             + r_ref[2, :, cols].astype(F32))
        delta, mn, vn = _adamw_math(w_ref[...], g, m_ref[...], v_ref[...])
        g_out[...] = g
        d_out[...] = delta
        m_out[...] = mn
        v_out[...] = vn

    mat = pl.BlockSpec((tr, ccols), lambda i: (i, 0))
    shp = jax.ShapeDtypeStruct((r, ccols), F32)
    return pl.pallas_call(
        body, name=name, grid=(r // tr,),
        in_specs=[pl.BlockSpec((tr, gcols), lambda i: (i, 0)), pl.BlockSpec((3, tr, gcols), lambda i: (0, i, 0)),
                  mat, mat, mat],
        out_specs=[mat, mat, mat, mat],
        out_shape=[shp, shp, shp, shp],
        compiler_params=_params("parallel"),
    )(own, recv, w, m, v)


def _adamw_vector_call(gathered, w, m, v, name):
    n = w.shape[1]

    def body(p_ref, w_ref, m_ref, v_ref, g_out, d_out, m_out, v_out):
        g = p_ref[0:1, :]
        for k in range(1, N_DEV):
            g = g + p_ref[k:k + 1, :]
        delta, mn, vn = _adamw_math(w_ref[...], g, m_ref[...], v_ref[...])
        g_out[...] = g
        d_out[...] = delta
        m_out[...] = mn
        v_out[...] = vn

    shp = jax.ShapeDtypeStruct((1, n), F32)
    return pl.pallas_call(body, name=name, out_shape=[shp, shp, shp, shp])(gathered, w, m, v)


def _round_up(n, mult):
    return (n + mult - 1) // mult * mult


def kernel(x, ffn1_norm, ffn1_w_gate, ffn1_w_up, ffn1_w_down, mix_norm, w_in, ret_norm_g, hgrn_lb_logits, hgrn_norm_g, w_out, ffn2_norm, ffn2_w_gate, ffn2_w_up, ffn2_w_down, final_norm, loss_target, m_ffn1_norm, m_ffn1_w_gate, m_ffn1_w_up, m_ffn1_w_down, m_mix_norm, m_w_in, m_ret_norm_g, m_hgrn_lb_logits, m_hgrn_norm_g, m_w_out, m_ffn2_norm, m_ffn2_w_gate, m_ffn2_w_up, m_ffn2_w_down, m_final_norm, v_ffn1_norm, v_ffn1_w_gate, v_ffn1_w_up, v_ffn1_w_down, v_mix_norm, v_w_in, v_ret_norm_g, v_hgrn_lb_logits, v_hgrn_norm_g, v_w_out, v_ffn2_norm, v_ffn2_w_gate, v_ffn2_w_up, v_ffn2_w_down, v_final_norm):
    xs = x[0]
    target = loss_target[0]
    s, d = xs.shape
    f_loc = ffn1_w_gate.shape[2]
    fp = _round_up(f_loc, LANE)
    pad_cols = lambda t: jnp.pad(t[0], ((0, 0), (0, fp - f_loc)))
    pad_rows = lambda t: jnp.pad(t[0], ((0, fp - f_loc), (0, 0)))

    mat_names = ["ffn1_w_gate", "ffn1_w_up", "ffn1_w_down", "w_in", "w_out", "ffn2_w_gate", "ffn2_w_up", "ffn2_w_down"]
    mat_pad = [pad_cols, pad_cols, pad_rows, lambda t: t[0], lambda t: t[0], pad_cols, pad_cols, pad_rows]
    mat_w = [ffn1_w_gate, ffn1_w_up, ffn1_w_down, w_in, w_out, ffn2_w_gate, ffn2_w_up, ffn2_w_down]
    mat_m = [m_ffn1_w_gate, m_ffn1_w_up, m_ffn1_w_down, m_w_in, m_w_out, m_ffn2_w_gate, m_ffn2_w_up, m_ffn2_w_down]
    mat_v = [v_ffn1_w_gate, v_ffn1_w_up, v_ffn1_w_down, v_w_in, v_w_out, v_ffn2_w_gate, v_ffn2_w_up, v_ffn2_w_down]

    cx, cy, cc = _position()
    place = jnp.stack([cc, 2 * cx + cy]).astype(jnp.int32)
    mat_index = {nm: i for i, nm in enumerate(mat_names)}
    mat_out = {}

    def pair_sums(names, grads, from_sibling):
        return [_pair_sum_call(g, r, place, "pair_sum_" + nm) for nm, g, r in zip(names, grads, from_sibling)]

    def update(names, sums, from_chips):
        for nm, (p, _), r in zip(names, sums, from_chips):
            i = mat_index[nm]
            res = _adamw_matrix_call(p, r, mat_w[i][0], mat_m[i][0], mat_v[i][0], "adamw_" + nm)
            mat_out[nm] = [t[None] for t in res]

    shards = [p(t.astype(BF)) for p, t in zip(mat_pad, mat_w)]
    wg1, wu1 = _all_gather_call(shards[:2], "gather_ffn1_up")

    h1 = _rmsnorm_call(xs, ffn1_norm, "ffn1_norm")
    (g1, u1, a1), (wd1, win) = _ffn_up_call(h1, wg1, wu1, "ffn1_up", comm=_gather_two_level(shards[2:4], 0.7))
    (x1, h2), _ = _down_call(a1, wd1, xs, mix_norm, FFN_RESIDUAL_WEIGHT, "ffn1_down")
    (proj,), landed_a = _proj_call(h2, win, "mix_in", comm=_gather_round1([shards[4], shards[7]]))
    wmix = proj.shape[1] // 8
    cos, sin = _rope_tables(s, wmix // RET_HEADS)
    consts = _ret_consts(wmix // RET_HEADS)
    (o_hg, m_hg, st_hg), landed_b = _hgrn_fwd_call(proj, hgrn_lb_logits, hgrn_norm_g, "hgrn_fwd",
                                                   comm=_gather_round1(shards[5:7]))
    (o_ret, m_ret, st_ret), (wout, wd2, wg2, wu2) = _ret_fwd_call(proj, cos, sin, consts, ret_norm_g, "ret_fwd",
                                                                  comm=_gather_round2(landed_a + landed_b))
    merged = jnp.concatenate([m_ret, m_hg], axis=1)
    wout_wide = wout.reshape(2, wout.shape[0] * wout.shape[1] // 2, d)
    (x2, h3), _ = _down_call(merged, wout_wide, x1, ffn2_norm, 1.0, "mix_out")
    (g2, u2, a2), _ = _ffn_up_call(h3, wg2, wu2, "ffn2_up")
    (x3,), _ = _down_call(a2, wd2, x2, None, FFN_RESIDUAL_WEIGHT, "ffn2_down")
    loss_part, dx3, dx3b, gv_final = _loss_call(x3, target, final_norm[None, :], "loss_head")

    (dg2, du2), _ = _bwd_up_call(dx3b, wd2, g2, u2, FFN_RESIDUAL_WEIGHT, "ffn2_bwd_up")
    dh3, _ = _bwd_mm_call([(dg2, wg2), (du2, wu2)], "ffn2_bwd_down")
    dx2, dx2b, gv_n3 = _norm_bwd_call(dh3, dx3, x2, ffn2_norm, "ffn2_norm_bwd")
    names_a = ["ffn2_w_gate", "ffn2_w_up", "ffn2_w_down"]
    grads_a = [_wgrad_call(h3, dg2, N_DEV, False, 1.0, "ffn2_wgrad_gate"),
               _wgrad_call(h3, du2, N_DEV, False, 1.0, "ffn2_wgrad_up"),
               _wgrad_call(a2, dx3b, N_DEV, True, FFN_RESIDUAL_WEIGHT, "ffn2_wgrad_down")]

    (dmerged,), sib_a = _proj_call(dx2b, jnp.swapaxes(wout_wide, 1, 2), "mix_out_bwd",
                                   comm=_sibling_exchange(grads_a))
    gm_out = _wgrad_call(merged, dx2b, 2, True, 1.0, "mix_out_wgrad").reshape(wout.shape)
    sums_a = pair_sums(names_a, grads_a, sib_a)
    dproj_half, gv_ret = _ret_bwd_call(proj, cos, sin, consts, ret_norm_g, o_ret, st_ret, dmerged, "ret_bwd")
    (dproj, gv_lb, gv_hg), landed = _hgrn_bwd_call(
        proj, hgrn_lb_logits, hgrn_norm_g, o_hg, st_hg, dmerged, dproj_half, "hgrn_bwd",
        comm=_join([_chip_exchange([pb for _, pb in sums_a]), _sibling_exchange([gm_out])]))
    update(names_a, sums_a, landed[:3])
    sums_out = pair_sums(["w_out"], [gm_out], landed[3:])
    dh2, chips_out = _bwd_mm_call([(dproj, win)], "mix_in_bwd", per_step=2, comm=_chip_exchange([sums_out[0][1]]))
    dx1, dx1b, gv_n2 = _norm_bwd_call(dh2, dx2, x1, mix_norm, "mix_norm_bwd")
    update(["w_out"], sums_out, chips_out)

    gm_in = _wgrad_call(h2, dproj, N_DEV, False, 1.0, "mix_in_wgrad")
    gm_d1, sib_in = _wgrad_call(a1, dx1b, N_DEV, True, FFN_RESIDUAL_WEIGHT, "ffn1_wgrad_down",
                                comm=_sibling_exchange([gm_in]))
    sums_in = pair_sums(["w_in"], [gm_in], sib_in)
    (dg1, du1), landed = _bwd_up_call(dx1b, wd1, g1, u1, FFN_RESIDUAL_WEIGHT, "ffn1_bwd_up",
                                      comm=_join([_chip_exchange([sums_in[0][1]]), _sibling_exchange([gm_d1])]))
    update(["w_in"], sums_in, landed[:1])
    sums_d1 = pair_sums(["ffn1_w_down"], [gm_d1], landed[1:])
    gm_g1, chips_d1 = _wgrad_call(h1, dg1, N_DEV, False, 1.0, "ffn1_wgrad_gate",
                                  comm=_chip_exchange([sums_d1[0][1]]))
    update(["ffn1_w_down"], sums_d1, chips_d1)
    gm_u1, sib_g = _wgrad_call(h1, du1, N_DEV, False, 1.0, "ffn1_wgrad_up", comm=_sibling_exchange([gm_g1]))
    sums_g = pair_sums(["ffn1_w_gate"], [gm_g1], sib_g)
    n_row_tiles = s // _tile(s, 1024)
    assert n_row_tiles >= 2, "the sequence must span at least two row tiles"
    n_first = n_row_tiles // 2
    dh1_part, landed = _bwd_mm_call(
        [(dg1, wg1), (du1, wu1)], "ffn1_bwd_down_a", tiles=(0, n_first),
        comm=_join([_chip_exchange([sums_g[0][1]]), _sibling_exchange([gm_u1])]))
    update(["ffn1_w_gate"], sums_g, landed[:1])
    sums_u = pair_sums(["ffn1_w_up"], [gm_u1], landed[1:])
    dh1, chips_u = _bwd_mm_call(
        [(dg1, wg1), (du1, wu1)], "ffn1_bwd_down_b", tiles=(n_first, n_row_tiles - n_first),
        carry=dh1_part, comm=_chip_exchange([sums_u[0][1]]))
    update(["ffn1_w_up"], sums_u, chips_u)
    dx0, _, gv_n1 = _norm_bwd_call(dh1, dx1, xs, ffn1_norm, "ffn1_norm_bwd")

    vec_names = ["ffn1_norm", "mix_norm", "ret_norm_g", "hgrn_lb_logits", "hgrn_norm_g", "ffn2_norm", "final_norm"]
    vec_g = [gv_n1, gv_n2, gv_ret, gv_lb, gv_hg, gv_n3, gv_final]
    vec_w = [ffn1_norm, mix_norm, ret_norm_g, hgrn_lb_logits, hgrn_norm_g, ffn2_norm, final_norm[None, :]]
    vec_m = [m_ffn1_norm, m_mix_norm, m_ret_norm_g, m_hgrn_lb_logits, m_hgrn_norm_g, m_ffn2_norm, m_final_norm[None, :]]
    vec_v = [v_ffn1_norm, v_mix_norm, v_ret_norm_g, v_hgrn_lb_logits, v_hgrn_norm_g, v_ffn2_norm, v_final_norm[None, :]]
    cat = lambda ts: jnp.concatenate(ts, axis=1)
    (vec_all,) = _all_gather_call([cat(vec_g)], "gather_vector_grads")
    vres = _adamw_vector_call(vec_all[:, 0, :], cat(vec_w), cat(vec_m), cat(vec_v), "adamw_vectors")
    vec_out = {}
    off = 0
    for nm, t in zip(vec_names, vec_w):
        n = t.shape[1]
        parts = [r[:, off:off + n] for r in vres]
        if nm == "final_norm":
            parts = [p[0] for p in parts]
        vec_out[nm] = parts
        off += n

    loss = lax.psum(loss_part[0, 0], ("x", "y", "c"))
    order = ["ffn1_norm", "ffn1_w_gate", "ffn1_w_up", "ffn1_w_down", "mix_norm", "w_in", "ret_norm_g", "hgrn_lb_logits",
             "hgrn_norm_g", "w_out", "ffn2_norm", "ffn2_w_gate", "ffn2_w_up", "ffn2_w_down", "final_norm"]
    res = {**mat_out, **vec_out}
    outs = [loss, dx0[None]]
    for kind in range(4):
        outs += [res[nm][kind] for nm in order]
    return tuple(outs)
```

```python
import functools

import jax
import jax.numpy as jnp
from jax import lax
from jax.experimental import pallas as pl
from jax.experimental.pallas import tpu as pltpu

BF = jnp.bfloat16
F32 = jnp.float32
MESH = pl.DeviceIdType.MESH
HBM_SPEC = pl.BlockSpec(memory_space=pltpu.HBM)

N_DEV = 8
LANE = 128
EPS = 1e-6
ROPE_BASE = 10000.0
RET_HEADS = 4
HGRN_HEADS = 8
RET_CHUNK = 128
HGRN_BLOCK = 16
FFN_RESIDUAL_WEIGHT = 0.5
ADAM_LR = 0.001
ADAM_B1 = 0.9
ADAM_B2 = 0.999
ADAM_EPS = 1e-08
ADAM_WD = 0.01
ADAM_STEP = 10
VMEM_LIMIT = 56 * 1024 * 1024


def _tile(n, pref, mult=8):
    t = min(pref, n)
    t -= t % mult
    while t >= mult:
        if n % t == 0:
            return t
        t -= mult
    return n


def _params(*sem):
    return pltpu.CompilerParams(dimension_semantics=sem, vmem_limit_bytes=VMEM_LIMIT)


class _Comm:
    def __init__(self, operands, out_shape, n_sems, start, finish, aliases=None, middle=None, middle_at=0.0):
        self.operands = list(operands)
        self.out_shape = list(out_shape)
        self.n_sems = n_sems
        self.start = start
        self.finish = finish
        self.aliases = dict(aliases or {})
        self.middle = middle
        self.middle_at = middle_at


def _launch(body, *, name, grid, in_specs, out_specs, out_shape, sem, args, scratch_shapes=(), aliases=None, comm=None):
    in_specs, out_specs, out_shape = list(in_specs), list(out_specs), list(out_shape)
    scratch_shapes = list(scratch_shapes)
    aliases = dict(aliases or {})
    if comm is None:
        res = pl.pallas_call(body, name=name, grid=grid, in_specs=in_specs, out_specs=out_specs, out_shape=out_shape,
                             scratch_shapes=scratch_shapes, input_output_aliases=aliases,
                             compiler_params=_params(*sem))(*args)
        return list(res), []
    n_in, n_out, n_scr = len(in_specs), len(out_specs), len(scratch_shapes)
    ci, co = len(comm.operands), len(comm.out_shape)

    def carrying(*refs):
        bounds = [0, n_in, n_in + ci, n_in + ci + n_out, n_in + ci + n_out + co, n_in + ci + n_out + co + n_scr]
        ins, cins, outs, couts, scr = [refs[a:b] for a, b in zip(bounds[:-1], bounds[1:])]
        send_sems, recv_sems = refs[bounds[-1]:]
        ids = [pl.program_id(k) for k in range(len(grid))]
        first = functools.reduce(jnp.logical_and, [i == 0 for i in ids])
        last = functools.reduce(jnp.logical_and, [i == g - 1 for i, g in zip(ids, grid)])

        @pl.when(first)
        def _():
            comm.start(cins, couts, send_sems, recv_sems)

        if comm.middle is not None:
            step, total = ids[0], grid[0]
            for i, g in zip(ids[1:], grid[1:]):
                step, total = step * g + i, total * g

            @pl.when(step == int(total * comm.middle_at))
            def _():
                comm.middle(cins, couts, send_sems, recv_sems)

        body(*ins, *outs, *scr)

        @pl.when(last)
        def _():
            comm.finish(cins, couts, send_sems, recv_sems)

    res = pl.pallas_call(
        carrying, name=name, grid=grid,
        in_specs=in_specs + [HBM_SPEC] * ci, out_specs=out_specs + [HBM_SPEC] * co,
        out_shape=out_shape + comm.out_shape,
        scratch_shapes=scratch_shapes + [pltpu.SemaphoreType.DMA((comm.n_sems,)), pltpu.SemaphoreType.DMA((comm.n_sems,))],
        input_output_aliases={**aliases, **{n_in + a: n_out + b for a, b in comm.aliases.items()}},
        compiler_params=_params(*(["arbitrary"] * len(grid))),
    )(*args, *comm.operands)
    return list(res[:n_out]), list(res[n_out:])


def _comm_only_call(comm, name):
    def body(*refs):
        ci, co = len(comm.operands), len(comm.out_shape)
        cins, couts = refs[:ci], refs[ci:ci + co]
        send_sems, recv_sems = refs[ci + co:]
        comm.start(cins, couts, send_sems, recv_sems)
        comm.finish(cins, couts, send_sems, recv_sems)

    return pl.pallas_call(
        body, name=name,
        in_specs=[HBM_SPEC] * len(comm.operands), out_specs=[HBM_SPEC] * len(comm.out_shape),
        out_shape=comm.out_shape,
        scratch_shapes=[pltpu.SemaphoreType.DMA((comm.n_sems,)), pltpu.SemaphoreType.DMA((comm.n_sems,))],
        input_output_aliases=comm.aliases,
    )(*comm.operands)


def _sigmoid(v):
    return 0.5 * jnp.tanh(0.5 * v) + 0.5


def _dot(a, b):
    return jnp.dot(a, b, preferred_element_type=F32)


def _dot_nt(a, b):
    return lax.dot_general(a, b, (((1,), (1,)), ((), ())), preferred_element_type=F32)


def _dot_tn(a, b):
    return lax.dot_general(a, b, (((0,), (0,)), ((), ())), preferred_element_type=F32)


def _rmsnorm_call(x, gain, name, comm=None):
    s, d = x.shape
    tm = _tile(s, 512)

    def body(x_ref, g_ref, o_ref):
        xv = x_ref[...]
        r = lax.rsqrt(jnp.mean(xv * xv, axis=-1, keepdims=True) + EPS)
        o_ref[...] = (xv * r * g_ref[...]).astype(BF)

    return _launch(
        body, name=name, grid=(s // tm,),
        in_specs=[pl.BlockSpec((tm, d), lambda i: (i, 0)), pl.BlockSpec((1, d), lambda i: (0, 0))],
        out_specs=[pl.BlockSpec((tm, d), lambda i: (i, 0))],
        out_shape=[jax.ShapeDtypeStruct((s, d), BF)],
        sem=("parallel",), args=(x, gain), comm=comm)


def _ffn_up_call(h, wg, wu, name, comm=None):
    s, d = h.shape
    nj, _, k = wg.shape
    tm = _tile(s, 1024)

    def body(h_ref, wg_ref, wu_ref, g_ref, u_ref, a_ref):
        hv = h_ref[...]
        g = _dot(hv, wg_ref[...])
        u = _dot(hv, wu_ref[...])
        g_ref[...] = g.astype(BF)
        u_ref[...] = u.astype(BF)
        a_ref[...] = (g * _sigmoid(g) * u).astype(BF)

    act = pl.BlockSpec((tm, k), lambda i, j: (i, j))
    wsp = pl.BlockSpec((None, d, k), lambda i, j: (j, 0, 0))
    return _launch(
        body, name=name, grid=(s // tm, nj),
        in_specs=[pl.BlockSpec((tm, d), lambda i, j: (i, 0)), wsp, wsp],
        out_specs=[act, act, act],
        out_shape=[jax.ShapeDtypeStruct((s, nj * k), BF)] * 3,
        sem=("parallel", "arbitrary"), args=(h, wg, wu), comm=comm)


def _proj_call(h, w, name, comm=None):
    s, d = h.shape
    nj, _, k = w.shape
    tm = _tile(s, 1024)

    def body(h_ref, w_ref, o_ref):
        o_ref[...] = _dot(h_ref[...], w_ref[...])

    return _launch(
        body, name=name, grid=(s // tm, nj),
        in_specs=[pl.BlockSpec((tm, d), lambda i, j: (i, 0)), pl.BlockSpec((None, d, k), lambda i, j: (j, 0, 0))],
        out_specs=[pl.BlockSpec((tm, k), lambda i, j: (i, j))],
        out_shape=[jax.ShapeDtypeStruct((s, nj * k), F32)],
        sem=("parallel", "arbitrary"), args=(h, w), comm=comm)


def _down_call(a, w, resid, gain, scale, name, comm=None):
    s = a.shape[0]
    nj, k, d = w.shape
    tm = _tile(s, 1024)
    strip = _tile(tm, 128)
    cn = _tile(d, 512, LANE)
    with_norm = gain is not None

    def body(*refs):
        if with_norm:
            a_ref, w_ref, r_ref, g_ref, x_ref, h_ref = refs
        else:
            a_ref, w_ref, r_ref, x_ref = refs
        j = pl.program_id(1)

        av = a_ref[...]

        @pl.when(j == 0)
        def _():
            for n0 in range(0, d, cn):
                x_ref[:, n0:n0 + cn] = _dot(av, w_ref[:, n0:n0 + cn])

        @pl.when(j > 0)
        def _():
            for n0 in range(0, d, cn):
                x_ref[:, n0:n0 + cn] += _dot(av, w_ref[:, n0:n0 + cn])

        @pl.when(j == nj - 1)
        def _():
            for r0 in range(0, tm, strip):
                rows = slice(r0, r0 + strip)
                xn = r_ref[rows, :] + (scale * x_ref[rows, :])
                x_ref[rows, :] = xn
                if with_norm:
                    r = lax.rsqrt(jnp.mean(xn * xn, axis=-1, keepdims=True) + EPS)
                    h_ref[rows, :] = (xn * r * g_ref[...]).astype(BF)

    row = pl.BlockSpec((tm, d), lambda i, j: (i, 0))
    in_specs = [pl.BlockSpec((tm, k), lambda i, j: (i, j)), pl.BlockSpec((None, k, d), lambda i, j: (j, 0, 0)), row]
    args = [a, w, resid]
    out_specs = [row]
    out_shape = [jax.ShapeDtypeStruct((s, d), F32)]
    if with_norm:
        in_specs.append(pl.BlockSpec((1, d), lambda i, j: (0, 0)))
        args.append(gain)
        out_specs.append(row)
        out_shape.append(jax.ShapeDtypeStruct((s, d), BF))
    return _launch(
        body, name=name, grid=(s // tm, nj),
        in_specs=in_specs, out_specs=out_specs, out_shape=out_shape,
        sem=("parallel", "arbitrary"), args=args, comm=comm)


def _loss_call(x, target, gain, name):
    s, d = x.shape
    tm = _tile(s, 512)

    def body(x_ref, t_ref, g_ref, loss_ref, dx_ref, dxb_ref, dg_ref):
        i = pl.program_id(0)

        @pl.when(i == 0)
        def _():
            loss_ref[...] = jnp.zeros_like(loss_ref)
            dg_ref[...] = jnp.zeros_like(dg_ref)

        xv = x_ref[...]
        gv = g_ref[...]
        r = lax.rsqrt(jnp.mean(xv * xv, axis=-1, keepdims=True) + EPS)
        xhat = xv * r
        err = xhat * gv - t_ref[...]
        per_tok = jnp.mean(err * err, axis=-1, keepdims=True)
        loss_ref[...] += 0.5 * jnp.sum(per_tok, axis=0, keepdims=True)
        dout = err * (1.0 / d)
        dg_ref[...] += jnp.sum(dout * xhat, axis=0, keepdims=True)
        dxhat = dout * gv
        dx = r * (dxhat - xhat * jnp.mean(dxhat * xhat, axis=-1, keepdims=True))
        dx_ref[...] = dx
        dxb_ref[...] = dx.astype(BF)

    row = pl.BlockSpec((tm, d), lambda i: (i, 0))
    vec = pl.BlockSpec((1, d), lambda i: (0, 0))
    return pl.pallas_call(
        body, name=name, grid=(s // tm,),
        in_specs=[row, row, vec],
        out_specs=[pl.BlockSpec((1, 1), lambda i: (0, 0)), row, row, vec],
        out_shape=[jax.ShapeDtypeStruct((1, 1), F32), jax.ShapeDtypeStruct((s, d), F32),
                   jax.ShapeDtypeStruct((s, d), BF), jax.ShapeDtypeStruct((1, d), F32)],
        compiler_params=_params("arbitrary"),
    )(x, target, gain)


def _bwd_up_call(dy, wd, g, u, scale, name, comm=None):
    s, d = dy.shape
    nj, k, _ = wd.shape
    tm = _tile(s, 1024)

    def body(dy_ref, w_ref, g_ref, u_ref, dg_ref, du_ref):
        da = scale * _dot_nt(dy_ref[...], w_ref[...])
        gv = g_ref[...].astype(F32)
        sig = _sigmoid(gv)
        du_ref[...] = (da * gv * sig).astype(BF)
        dg_ref[...] = (da * u_ref[...].astype(F32) * sig * (1.0 + gv * (1.0 - sig))).astype(BF)

    act = pl.BlockSpec((tm, k), lambda i, j: (i, j))
    return _launch(
        body, name=name, grid=(s // tm, nj),
        in_specs=[pl.BlockSpec((tm, d), lambda i, j: (i, 0)), pl.BlockSpec((None, k, d), lambda i, j: (j, 0, 0)), act, act],
        out_specs=[act, act],
        out_shape=[jax.ShapeDtypeStruct((s, nj * k), BF), jax.ShapeDtypeStruct((s, nj * k), BF)],
        sem=("parallel", "arbitrary"), args=(dy, wd, g, u), comm=comm)


def _norm_bwd_call(dh, dres, xin, gain, name):
    s, d = xin.shape
    tm = _tile(s, 256)

    def body(dh_ref, dres_ref, x_ref, g_ref, dx_ref, dxb_ref, dg_ref):
        @pl.when(pl.program_id(0) == 0)
        def _():
            dg_ref[...] = jnp.zeros_like(dg_ref)

        xv = x_ref[...]
        r = lax.rsqrt(jnp.mean(xv * xv, axis=-1, keepdims=True) + EPS)
        xhat = xv * r
        dh_v = dh_ref[...]
        dg_ref[...] += jnp.sum(dh_v * xhat, axis=0, keepdims=True)
        dxhat = dh_v * g_ref[...]
        dx = dres_ref[...] + r * (dxhat - xhat * jnp.mean(dxhat * xhat, axis=-1, keepdims=True))
        dx_ref[...] = dx
        dxb_ref[...] = dx.astype(BF)

    row = pl.BlockSpec((tm, d), lambda i: (i, 0))
    vec = pl.BlockSpec((1, d), lambda i: (0, 0))
    return pl.pallas_call(
        body, name=name, grid=(s // tm,),
        in_specs=[row, row, row, vec], out_specs=[row, row, vec],
        out_shape=[jax.ShapeDtypeStruct((s, d), F32), jax.ShapeDtypeStruct((s, d), BF), jax.ShapeDtypeStruct((1, d), F32)],
        compiler_params=_params("arbitrary"),
    )(dh, dres, xin, gain)


def _bwd_mm_call(pairs, name, per_step=1, tiles=None, carry=None, comm=None):
    a0 = pairs[0][0]
    s = a0.shape[-2]
    nblocks, d, k = pairs[0][1].shape
    nj = nblocks // per_step
    npair = len(pairs) * per_step
    tm = _tile(s, 1024)
    cn = _tile(d, 512, LANE)

    first_tile, n_tiles = (0, s // tm) if tiles is None else tiles

    def body(*refs):
        a_refs = refs[0:2 * npair:2]
        w_refs = refs[1:2 * npair:2]
        dh_ref = refs[-1]

        @pl.when(pl.program_id(1) == 0)
        def _():
            dh_ref[...] = jnp.zeros_like(dh_ref)

        for a_ref, w_ref in zip(a_refs, w_refs):
            av = a_ref[...]
            for n0 in range(0, d, cn):
                dh_ref[:, n0:n0 + cn] += _dot_nt(av, w_ref[n0:n0 + cn, :])

    row = pl.BlockSpec((tm, d), lambda i, j: (first_tile + i, 0))
    in_specs, args = [], []
    for a, w in pairs:
        for r in range(per_step):
            in_specs += [pl.BlockSpec((None, tm, k), lambda i, j, r=r: (j * per_step + r, first_tile + i, 0))
                         if a.ndim == 3 else
                         pl.BlockSpec((tm, k), lambda i, j, r=r: (first_tile + i, j * per_step + r)),
                         pl.BlockSpec((None, d, k), lambda i, j, r=r: (j * per_step + r, 0, 0))]
            args += [a, w]
    aliases = {}
    if carry is not None:
        aliases = {len(in_specs): 0}
        in_specs.append(pl.BlockSpec(memory_space=pl.ANY))
        args.append(carry)
    (dh,), landed = _launch(
        body, name=name, grid=(n_tiles, nj),
        in_specs=in_specs, out_specs=[row], out_shape=[jax.ShapeDtypeStruct((s, d), F32)],
        sem=("parallel", "arbitrary"), args=args, aliases=aliases, comm=comm)
    return dh, landed


def _wgrad_call(a, b, nj, a_blocked, scale, name, comm=None):
    s = a.shape[0]
    ka = a.shape[1] // nj if a_blocked else a.shape[1]
    b_stacked = b.ndim == 3
    kb = b.shape[-1] if (a_blocked or b_stacked) else b.shape[1] // nj
    ts = _tile(s, 2048)
    ns = s // ts

    def body(a_ref, b_ref, o_ref):
        t = pl.program_id(1)

        @pl.when(t == 0)
        def _():
            o_ref[...] = _dot_tn(a_ref[...], b_ref[...])

        @pl.when(t > 0)
        def _():
            o_ref[...] += _dot_tn(a_ref[...], b_ref[...])

        if scale != 1.0:
            @pl.when(t == ns - 1)
            def _():
                o_ref[...] = o_ref[...] * scale

    a_spec = pl.BlockSpec((ts, ka), (lambda j, t: (t, j)) if a_blocked else (lambda j, t: (t, 0)))
    if b_stacked:
        b_spec = pl.BlockSpec((None, ts, kb), lambda j, t: (j, t, 0))
    else:
        b_spec = pl.BlockSpec((ts, kb), (lambda j, t: (t, 0)) if a_blocked else (lambda j, t: (t, j)))
    (out,), landed = _launch(
        body, name=name, grid=(nj, ns),
        in_specs=[a_spec, b_spec],
        out_specs=[pl.BlockSpec((None, ka, kb), lambda j, t: (j, 0, 0))],
        out_shape=[jax.ShapeDtypeStruct((nj, ka, kb), F32)],
        sem=("parallel", "arbitrary"), args=(a, b), comm=comm)
    return out if comm is None else (out, landed)


def _rope(v, cos, sin):
    half = v.shape[-1] // 2
    v1, v2 = v[:, :half], v[:, half:]
    return jnp.concatenate([v1 * cos - v2 * sin, v2 * cos + v1 * sin], axis=-1)


def _rope_bwd(dv, cos, sin):
    half = dv.shape[-1] // 2
    d1, d2 = dv[:, :half], dv[:, half:]
    return jnp.concatenate([d1 * cos + d2 * sin, d2 * cos - d1 * sin], axis=-1)


def _ret_consts(hd):
    c = RET_CHUNK
    log_gamma = jnp.log(1.0 - jnp.exp2(-5.0 - jnp.arange(RET_HEADS, dtype=F32)))
    idx = jnp.arange(c, dtype=F32)
    rel = idx[:, None] - idx[None, :]
    mask = rel >= 0
    decay = jnp.where(mask[None], jnp.exp(log_gamma[:, None, None] * jnp.where(mask, rel, 0.0)[None]), 0.0)
    qdec = jnp.exp(log_gamma[:, None] * (idx + 1.0)[None, :])
    kdec = jnp.exp(log_gamma[:, None] * (c - 1.0 - idx)[None, :])
    gchunk = jnp.exp(log_gamma * c)
    bc = lambda t: jnp.broadcast_to(t[:, :, None], (RET_HEADS, t.shape[1], hd))
    return decay, bc(qdec), bc(kdec), bc(gchunk[:, None])


def _rope_tables(s, hd):
    inv = jnp.power(ROPE_BASE, -jnp.arange(0, hd, 2, dtype=F32) / hd)
    ang = jnp.arange(s, dtype=F32)[:, None] * inv[None, :]
    return jnp.cos(ang), jnp.sin(ang)


def _ret_fwd_call(proj, cos, sin, consts, gret, name, comm=None):
    s = proj.shape[0]
    w = proj.shape[1] // 8
    hd = w // RET_HEADS
    c = RET_CHUNK
    tt = _tile(s, 512, c)
    nc = tt // c
    decay, qdec, kdec, gch = consts
    scale = hd ** -0.5

    def body(q_ref, k_ref, v_ref, gate_ref, cos_ref, sin_ref, dec_ref, qd_ref, kd_ref, gc_ref, gn_ref,
             o_ref, m_ref, st_ref, state):
        @pl.when(pl.program_id(1) == 0)
        def _():
            state[...] = jnp.zeros_like(state)

        dec = dec_ref[...]
        for ci in range(nc):
            rows = slice(ci * c, (ci + 1) * c)
            cs, sn = cos_ref[rows, :], sin_ref[rows, :]
            q = _rope(q_ref[rows, :], cs, sn) * scale
            k = _rope(k_ref[rows, :], cs, sn)
            vb = v_ref[rows, :].astype(BF)
            sc = _dot_nt(q.astype(BF), k.astype(BF)) * dec
            prev = state[...]
            st_ref[ci] = prev
            o = _dot(sc.astype(BF), vb) + _dot((q * qd_ref[...]).astype(BF), prev.astype(BF))
            state[...] = gc_ref[...] * prev + _dot_tn((k * kd_ref[...]).astype(BF), vb)
            o_ref[rows, :] = o
            mu = jnp.mean(o, axis=-1, keepdims=True)
            cen = o - mu
            xhat = cen * lax.rsqrt(jnp.mean(cen * cen, axis=-1, keepdims=True) + EPS)
            gt = gate_ref[rows, :]
            m_ref[rows, :] = (xhat * gn_ref[...] * (gt * _sigmoid(gt))).astype(BF)

    nh = RET_HEADS
    comp = lambda j: pl.BlockSpec((tt, hd), lambda h, t, j=j: (t, j * nh + h))
    tab = pl.BlockSpec((tt, hd // 2), lambda h, t: (t, 0))
    per_head = lambda r: pl.BlockSpec((None, r, hd), lambda h, t: (h, 0, 0))
    return _launch(
        body, name=name, grid=(nh, s // tt),
        in_specs=[comp(0), comp(1), comp(2), comp(3), tab, tab,
                  pl.BlockSpec((None, c, c), lambda h, t: (h, 0, 0)), per_head(c), per_head(c), per_head(1),
                  pl.BlockSpec((1, hd), lambda h, t: (0, h))],
        out_specs=[pl.BlockSpec((tt, hd), lambda h, t: (t, h)), pl.BlockSpec((tt, hd), lambda h, t: (t, h)),
                   pl.BlockSpec((None, nc, hd, hd), lambda h, t: (h, t, 0, 0))],
        out_shape=[jax.ShapeDtypeStruct((s, w), F32), jax.ShapeDtypeStruct((s, w), BF),
                   jax.ShapeDtypeStruct((nh, s // c, hd, hd), F32)],
        scratch_shapes=[pltpu.VMEM((hd, hd), F32)],
        sem=("parallel", "arbitrary"),
        args=(proj, proj, proj, proj, cos, sin, decay, qdec, kdec, gch, gret), comm=comm)


def _ret_bwd_call(proj, cos, sin, consts, gret, o_raw, states, dmerged, name):
    s = proj.shape[0]
    w = proj.shape[1] // 8
    hd = w // RET_HEADS
    c = RET_CHUNK
    tt = _tile(s, 512, c)
    nc = tt // c
    nt = s // tt
    decay, qdec, kdec, gch = consts
    scale = hd ** -0.5

    def body(q_ref, k_ref, v_ref, gate_ref, cos_ref, sin_ref, dec_ref, qd_ref, kd_ref, gc_ref, gn_ref,
             o_ref, st_ref, dm_ref, dp_ref, dgn_ref, dstate):
        @pl.when(pl.program_id(1) == 0)
        def _():
            dstate[...] = jnp.zeros_like(dstate)
            dgn_ref[...] = jnp.zeros_like(dgn_ref)

        dec = dec_ref[...]
        gn = gn_ref[...]
        for ci in reversed(range(nc)):
            rows = slice(ci * c, (ci + 1) * c)
            cs, sn = cos_ref[rows, :], sin_ref[rows, :]
            q = _rope(q_ref[rows, :], cs, sn) * scale
            k = _rope(k_ref[rows, :], cs, sn)
            qb, kb = q.astype(BF), k.astype(BF)
            vb = v_ref[rows, :].astype(BF)
            sc = _dot_nt(qb, kb) * dec
            o = o_ref[rows, :]
            mu = jnp.mean(o, axis=-1, keepdims=True)
            cen = o - mu
            rstd = lax.rsqrt(jnp.mean(cen * cen, axis=-1, keepdims=True) + EPS)
            xhat = cen * rstd
            gt = gate_ref[rows, :]
            sig = _sigmoid(gt)
            sg = gt * sig
            dm = dm_ref[rows, :]
            dgn_ref[...] += jnp.sum(dm * xhat * sg, axis=0, keepdims=True)
            dp_ref[3, rows, :] = (dm * xhat * gn * sig * (1.0 + gt * (1.0 - sig))).astype(BF)
            dxhat = dm * gn * sg
            do = rstd * (dxhat - jnp.mean(dxhat, axis=-1, keepdims=True)
                         - xhat * jnp.mean(dxhat * xhat, axis=-1, keepdims=True))
            dob = do.astype(BF)
            prev = st_ref[ci]
            ds = dstate[...]
            dsb = ds.astype(BF)
            dsc = (_dot_nt(dob, vb) * dec).astype(BF)
            dq = _dot(dsc, kb) + _dot_nt(dob, prev.astype(BF)) * qd_ref[...]
            dk = _dot_tn(dsc, qb) + _dot_nt(vb, dsb) * kd_ref[...]
            dv = _dot_tn(sc.astype(BF), dob) + _dot((k * kd_ref[...]).astype(BF), dsb)
            dstate[...] = gc_ref[...] * ds + _dot_tn((q * qd_ref[...]).astype(BF), dob)
            dp_ref[0, rows, :] = _rope_bwd(dq * scale, cs, sn).astype(BF)
            dp_ref[1, rows, :] = _rope_bwd(dk, cs, sn).astype(BF)
            dp_ref[2, rows, :] = dv.astype(BF)

    nh = RET_HEADS
    rev = lambda t: nt - 1 - t
    comp = lambda j: pl.BlockSpec((tt, hd), lambda h, t, j=j: (rev(t), j * nh + h))
    tab = pl.BlockSpec((tt, hd // 2), lambda h, t: (rev(t), 0))
    per_head = lambda r: pl.BlockSpec((None, r, hd), lambda h, t: (h, 0, 0))
    head_cols = pl.BlockSpec((tt, hd), lambda h, t: (rev(t), h))
    gvec = pl.BlockSpec((1, hd), lambda h, t: (0, h))
    act = jax.ShapeDtypeStruct((s, w), BF)
    return pl.pallas_call(
        body, name=name, grid=(nh, nt),
        in_specs=[comp(0), comp(1), comp(2), comp(3), tab, tab,
                  pl.BlockSpec((None, c, c), lambda h, t: (h, 0, 0)), per_head(c), per_head(c), per_head(1), gvec,
                  head_cols, pl.BlockSpec((None, nc, hd, hd), lambda h, t: (h, rev(t), 0, 0)), head_cols],
        out_specs=[pl.BlockSpec((4, tt, hd), lambda h, t: (0, rev(t), h)), gvec],
        out_shape=[jax.ShapeDtypeStruct((8, s, w), BF), jax.ShapeDtypeStruct((1, w), F32)],
        scratch_shapes=[pltpu.VMEM((hd, hd), F32)],
        compiler_params=_params("parallel", "arbitrary"),
    )(proj, proj, proj, proj, cos, sin, decay, qdec, kdec, gch, gret, o_raw, states, dmerged)


def _block_tri(n, bs, upper):
    r = jnp.arange(n)[:, None]
    cidx = jnp.arange(n)[None, :]
    same = (r // bs) == (cidx // bs)
    return jnp.where(same & ((cidx >= r) if upper else (cidx <= r)), 1.0, 0.0).astype(BF)


def _dot_exact(tri, v):
    hi = v.astype(BF)
    rest = v - hi.astype(F32)
    mid = rest.astype(BF)
    lo = (rest - mid.astype(F32)).astype(BF)
    return _dot(tri, hi) + _dot(tri, mid) + _dot(tri, lo)


def _hgrn_gates(z, lbv):
    sz = _sigmoid(z)
    oml = 1.0 - lbv
    f = lbv + oml * sz
    key = oml * (1.0 - sz)
    return sz, f, key


def _hgrn_fwd_call(proj, lb_logits, ghg, name, comm=None):
    s = proj.shape[0]
    w = proj.shape[1] // 8
    nh = HGRN_HEADS
    hd = w // nh
    bs = HGRN_BLOCK
    tt = _tile(s, 256, bs)
    nb = tt // bs

    def body(q_ref, z_ref, v_ref, gate_ref, lb_ref, gn_ref, tril_ref, o_ref, m_ref, st_ref, state, upd):
        @pl.when(pl.program_id(1) == 0)
        def _():
            state[...] = jnp.zeros_like(state)

        lbv = _sigmoid(lb_ref[...])
        _, f, key = _hgrn_gates(z_ref[...], lbv)
        qr = q_ref[...]
        q = qr * _sigmoid(qr)
        v = v_ref[...]
        g = _dot_exact(tril_ref[...], jnp.log(f))
        blocks = lambda t: t.reshape(nb, bs, hd)
        g3, q3, k3, v3 = blocks(g), blocks(q), blocks(key), blocks(v)
        glast3 = g3[:, bs - 1:bs, :]
        row_id = lax.broadcasted_iota(jnp.int32, (nb, bs, hd), 1)
        o3 = jnp.zeros((nb, bs, hd), F32)
        for j in range(bs):
            wj = jnp.where(row_id >= j, jnp.exp(jnp.minimum(g3 - g3[:, j:j + 1, :], 0.0)), 0.0)
            a = jnp.sum(q3 * k3[:, j:j + 1, :] * wj, axis=-1, keepdims=True)
            o3 = o3 + a * v3[:, j:j + 1, :]
        ktb = (k3 * jnp.exp(glast3 - g3)).reshape(tt, hd).astype(BF)
        vb = v.astype(BF)
        for b in range(nb):
            rows = slice(b * bs, (b + 1) * bs)
            upd[b] = _dot_tn(vb[rows, :], ktb[rows, :])
        egl3 = jnp.exp(glast3)
        st = state[...]
        for b in range(nb):
            st_ref[b] = st
            st = st * egl3[b] + upd[b]
        state[...] = st
        qgb = (q * jnp.exp(g)).astype(BF)
        o_intra = o3.reshape(tt, hd)
        gn = gn_ref[...]
        for b in range(nb):
            rows = slice(b * bs, (b + 1) * bs)
            o = o_intra[rows, :] + _dot_nt(qgb[rows, :], st_ref[b].astype(BF))
            o_ref[rows, :] = o
            gt = gate_ref[rows, :]
            xhat = o * lax.rsqrt(jnp.mean(o * o, axis=-1, keepdims=True) + EPS)
            m_ref[rows, :] = (xhat * gn * (gt * _sigmoid(gt))).astype(BF)

    comp = lambda j: pl.BlockSpec((tt, hd), lambda h, t, j=j: (t, j * nh + h))
    gvec = pl.BlockSpec((1, hd), lambda h, t: (0, h))
    head_cols = pl.BlockSpec((tt, hd), lambda h, t: (t, h))
    return _launch(
        body, name=name, grid=(nh, s // tt),
        in_specs=[comp(4), comp(5), comp(6), comp(7), gvec, gvec, pl.BlockSpec((tt, tt), lambda h, t: (0, 0))],
        out_specs=[head_cols, head_cols, pl.BlockSpec((None, nb, hd, hd), lambda h, t: (h, t, 0, 0))],
        out_shape=[jax.ShapeDtypeStruct((s, w), F32), jax.ShapeDtypeStruct((s, w), BF),
                   jax.ShapeDtypeStruct((nh, s // bs, hd, hd), F32)],
        scratch_shapes=[pltpu.VMEM((hd, hd), F32), pltpu.VMEM((nb, hd, hd), F32)],
        sem=("parallel", "arbitrary"),
        args=(proj, proj, proj, proj, lb_logits, ghg, _block_tri(tt, bs, upper=False)), comm=comm)


def _hgrn_bwd_call(proj, lb_logits, ghg, o_raw, states, dmerged, stack, name, comm=None):
    s = proj.shape[0]
    w = proj.shape[1] // 8
    nh = HGRN_HEADS
    hd = w // nh
    bs = HGRN_BLOCK
    tt = _tile(s, 256, bs)
    nb = tt // bs
    nt = s // tt

    def body(q_ref, z_ref, v_ref, gate_ref, lb_ref, gn_ref, tril_ref, triu_ref, o_ref, st_ref, dm_ref, stack_ref,
             dp_ref, dlb_ref, dgn_ref,
             dstate, ds_all, inc, dq_s, dk_s, dv_s, dgl_s, dk_rows, dv_rows):
        @pl.when(pl.program_id(1) == 0)
        def _():
            dstate[...] = jnp.zeros_like(dstate)
            dlb_ref[...] = jnp.zeros_like(dlb_ref)
            dgn_ref[...] = jnp.zeros_like(dgn_ref)

        lbv = _sigmoid(lb_ref[...])
        oml = 1.0 - lbv
        gn = gn_ref[...]
        sz, f, key = _hgrn_gates(z_ref[...], lbv)
        qr = q_ref[...]
        sq = _sigmoid(qr)
        q = qr * sq
        v = v_ref[...]
        g = _dot_exact(tril_ref[...], jnp.log(f))
        eg = jnp.exp(g)
        blocks = lambda t: t.reshape(nb, bs, hd)
        g3, q3, k3, v3 = blocks(g), blocks(q), blocks(key), blocks(v)
        glast3 = g3[:, bs - 1:bs, :]
        egl3 = jnp.exp(glast3)
        ktail3 = jnp.exp(glast3 - g3)
        o = o_ref[...]
        rstd = lax.rsqrt(jnp.mean(o * o, axis=-1, keepdims=True) + EPS)
        xhat = o * rstd
        gt = gate_ref[...]
        sig = _sigmoid(gt)
        sg = gt * sig
        dm = dm_ref[...]
        dgn_ref[...] += jnp.sum(dm * xhat * sg, axis=0, keepdims=True)
        del stack_ref
        dp_ref[3] = (dm * xhat * gn * sig * (1.0 + gt * (1.0 - sig))).astype(BF)
        dxhat = dm * gn * sg
        do = rstd * (dxhat - xhat * jnp.mean(dxhat * xhat, axis=-1, keepdims=True))
        dob = do.astype(BF)
        do3 = blocks(do)
        qgb = (q * eg).astype(BF)
        for b in range(nb):
            rows = slice(b * bs, (b + 1) * bs)
            inc[b] = _dot_tn(dob[rows, :], qgb[rows, :])
        ds = dstate[...]
        for b in reversed(range(nb)):
            ds_all[b] = ds
            ds = ds * egl3[b] + inc[b]
        dstate[...] = ds
        ktb = (k3 * ktail3).reshape(tt, hd).astype(BF)
        vb = v.astype(BF)
        for b in range(nb):
            rows = slice(b * bs, (b + 1) * bs)
            prev = st_ref[b]
            dsb = ds_all[b]
            dsbb = dsb.astype(BF)
            dq_s[rows, :] = _dot(dob[rows, :], prev.astype(BF))
            dk_s[rows, :] = _dot(vb[rows, :], dsbb)
            dv_s[rows, :] = _dot_nt(ktb[rows, :], dsbb)
            dgl_s[b] = jnp.sum(prev * dsb, axis=0, keepdims=True)
        dq3 = blocks(dq_s[...] * eg)
        dk3 = blocks(dk_s[...]) * ktail3
        dg_last3 = jnp.sum(k3 * dk3, axis=1, keepdims=True) + egl3 * dgl_s[...]
        row_id = lax.broadcasted_iota(jnp.int32, (nb, bs, hd), 1)
        for j in range(bs):
            wj = jnp.where(row_id >= j, jnp.exp(jnp.minimum(g3 - g3[:, j:j + 1, :], 0.0)), 0.0)
            kj = k3[:, j:j + 1, :]
            a = jnp.sum(q3 * kj * wj, axis=-1, keepdims=True)
            da = jnp.sum(do3 * v3[:, j:j + 1, :], axis=-1, keepdims=True)
            dv_rows[:, j:j + 1, :] = jnp.sum(a * do3, axis=1, keepdims=True)
            dq3 = dq3 + da * kj * wj
            dk_rows[:, j:j + 1, :] = jnp.sum(da * q3 * wj, axis=1, keepdims=True)
        dk3 = dk3 + dk_rows[...]
        dv = dv_s[...] + dv_rows[...].reshape(tt, hd)
        dg3 = q3 * dq3 - k3 * dk3 + jnp.where(row_id == bs - 1, dg_last3, 0.0)
        dlf = _dot_exact(triu_ref[...], dg3.reshape(tt, hd))
        dk = dk3.reshape(tt, hd)
        dfk = dlf / f - dk
        dlb_ref[...] += jnp.sum(dfk * (1.0 - sz), axis=0, keepdims=True) * (lbv * oml)
        dp_ref[1] = (dfk * oml * sz * (1.0 - sz)).astype(BF)
        dp_ref[0] = (dq3.reshape(tt, hd) * sq * (1.0 + qr * (1.0 - sq))).astype(BF)
        dp_ref[2] = dv.astype(BF)

    rev = lambda t: nt - 1 - t
    comp = lambda j: pl.BlockSpec((tt, hd), lambda h, t, j=j: (rev(t), j * nh + h))
    gvec = pl.BlockSpec((1, hd), lambda h, t: (0, h))
    head_cols = pl.BlockSpec((tt, hd), lambda h, t: (rev(t), h))
    tri = pl.BlockSpec((tt, tt), lambda h, t: (0, 0))
    act = jax.ShapeDtypeStruct((s, w), BF)
    vec = jax.ShapeDtypeStruct((1, w), F32)
    tile_f32 = pltpu.VMEM((tt, hd), F32)
    return _launch(
        body, name=name, grid=(nh, nt),
        in_specs=[comp(4), comp(5), comp(6), comp(7), gvec, gvec, tri, tri, head_cols,
                  pl.BlockSpec((None, nb, hd, hd), lambda h, t: (h, rev(t), 0, 0)),
                  pl.BlockSpec((tt, hd), lambda h, t: (rev(t), nh + h)),
                  pl.BlockSpec(memory_space=pl.ANY)],
        out_specs=[pl.BlockSpec((4, tt, hd), lambda h, t: (1, rev(t), h)), gvec, gvec],
        out_shape=[jax.ShapeDtypeStruct(stack.shape, stack.dtype), vec, vec],
        scratch_shapes=[pltpu.VMEM((hd, hd), F32), pltpu.VMEM((nb, hd, hd), F32), pltpu.VMEM((nb, hd, hd), F32),
                        tile_f32, tile_f32, tile_f32, pltpu.VMEM((nb, 1, hd), F32),
                        pltpu.VMEM((nb, bs, hd), F32), pltpu.VMEM((nb, bs, hd), F32)],
        sem=("parallel", "arbitrary"),
        args=(proj, proj, proj, proj, lb_logits, ghg, _block_tri(tt, bs, upper=False), _block_tri(tt, bs, upper=True),
              o_raw, states, dmerged, stack), aliases={11: 0}, comm=comm)


def _position():
    return lax.axis_index("x"), lax.axis_index("y"), lax.axis_index("c")


def _all_gather_call(shards, name):
    n = len(shards)

    def body(*refs):
        ins, outs = refs[:n], refs[n:2 * n]
        send_sems, recv_sems, local_sems = refs[2 * n:]
        x, y, c = _position()
        me, sibling = (x, y, c), (x, y, 1 - c)
        chips = [(1 - x, y), (x, 1 - y), (1 - x, 1 - y)]

        def slot(a, p):
            return outs[a].at[4 * p[0] + 2 * p[1] + p[2]]

        def copy(a, k, block, to, src=None):
            return pltpu.make_async_remote_copy(
                src_ref=slot(a, block) if src is None else src, dst_ref=slot(a, block),
                send_sem=send_sems.at[a * 7 + k], recv_sem=recv_sems.at[a * 7 + k],
                device_id=to, device_id_type=MESH)

        mine = [pltpu.make_async_copy(ins[a], slot(a, me), local_sems.at[a]) for a in range(n)]
        for cp in mine:
            cp.start()
        first = []
        for a in range(n):
            first.append(copy(a, 0, me, sibling, src=ins[a]))
            first += [copy(a, 1 + j, me, (*chip, c), src=ins[a]) for j, chip in enumerate(chips)]
        for cp in first:
            cp.start()
        passed = []
        for j, chip in enumerate(chips):
            for a in range(n):
                copy(a, 1 + j, (*chip, c), me).wait_recv()
                fwd = copy(a, 4 + j, (*chip, c), sibling)
                fwd.start()
                passed.append(fwd)
        for a in range(n):
            copy(a, 0, sibling, me).wait_recv()
            for j, chip in enumerate(chips):
                copy(a, 4 + j, (*chip, 1 - c), me).wait_recv()
        for cp in first + passed:
            cp.wait_send()
        for cp in mine:
            cp.wait()

    return pl.pallas_call(
        body, name=name,
        in_specs=[HBM_SPEC] * n, out_specs=[HBM_SPEC] * n,
        out_shape=[jax.ShapeDtypeStruct((N_DEV,) + t.shape, t.dtype) for t in shards],
        scratch_shapes=[pltpu.SemaphoreType.DMA((7 * n,)), pltpu.SemaphoreType.DMA((7 * n,)),
                        pltpu.SemaphoreType.DMA((n,))],
    )(*shards)


def _slot(ref, p):
    return ref.at[4 * p[0] + 2 * p[1] + p[2]]


def _gather_round1(shards):
    n = len(shards)

    def plan(ins, outs, send_sems, recv_sems):
        x, y, c = _position()
        me = (x, y, c)
        peers = [(x, y, 1 - c), (1 - x, y, c), (x, 1 - y, c), (1 - x, 1 - y, c)]
        sends, recvs, local = [], [], []
        for a in range(n):
            local.append(pltpu.make_async_copy(ins[a], _slot(outs[a], me), send_sems.at[4 * n + a]))
            for k, peer in enumerate(peers):
                sems = dict(send_sem=send_sems.at[4 * a + k], recv_sem=recv_sems.at[4 * a + k],
                            device_id=peer, device_id_type=MESH)
                sends.append(pltpu.make_async_remote_copy(src_ref=ins[a], dst_ref=_slot(outs[a], me), **sems))
                recvs.append(pltpu.make_async_remote_copy(src_ref=ins[a], dst_ref=_slot(outs[a], peer), **sems))
        return sends, recvs, local

    def start(*refs):
        sends, _, local = plan(*refs)
        for cp in local + sends:
            cp.start()

    def finish(*refs):
        sends, recvs, local = plan(*refs)
        for cp in recvs:
            cp.wait_recv()
        for cp in sends:
            cp.wait_send()
        for cp in local:
            cp.wait()

    return _Comm(shards, [jax.ShapeDtypeStruct((N_DEV,) + t.shape, t.dtype) for t in shards], 5 * n, start, finish)


def _gather_round2(gathered):
    n = len(gathered)

    def plan(ins, outs, send_sems, recv_sems):
        x, y, c = _position()
        chips = [(1 - x, y), (x, 1 - y), (1 - x, 1 - y)]
        sends, recvs = [], []
        for a in range(n):
            for k, chip in enumerate(chips):
                sems = dict(send_sem=send_sems.at[3 * a + k], recv_sem=recv_sems.at[3 * a + k],
                            device_id=(x, y, 1 - c), device_id_type=MESH)
                sends.append(pltpu.make_async_remote_copy(
                    src_ref=_slot(ins[a], (*chip, c)), dst_ref=_slot(outs[a], (*chip, c)), **sems))
                recvs.append(pltpu.make_async_remote_copy(
                    src_ref=_slot(ins[a], (*chip, c)), dst_ref=_slot(outs[a], (*chip, 1 - c)), **sems))
        return sends, recvs

    def start(*refs):
        for cp in plan(*refs)[0]:
            cp.start()

    def finish(*refs):
        sends, recvs = plan(*refs)
        for cp in recvs:
            cp.wait_recv()
        for cp in sends:
            cp.wait_send()

    return _Comm(gathered, [jax.ShapeDtypeStruct(t.shape, t.dtype) for t in gathered], 3 * n, start, finish,
                 aliases={a: a for a in range(n)})


def _gather_two_level(shards, forward_at):
    n = len(shards)
    first, second = _gather_round1(shards), _gather_round2(shards)

    def middle(ins, outs, send_sems, recv_sems):
        first.finish(ins, outs, send_sems, recv_sems)
        second.start(outs, outs, _SemWindow(send_sems, first.n_sems), _SemWindow(recv_sems, first.n_sems))

    def finish(ins, outs, send_sems, recv_sems):
        second.finish(outs, outs, _SemWindow(send_sems, first.n_sems), _SemWindow(recv_sems, first.n_sems))

    return _Comm(shards, first.out_shape, first.n_sems + second.n_sems, first.start, finish,
                 middle=middle, middle_at=forward_at)


def _sibling_exchange(grads):
    n = len(grads)

    def plan(ins, outs, send_sems, recv_sems):
        x, y, c = _position()
        return [pltpu.make_async_remote_copy(
            src_ref=ins[a].at[2 * q + (1 - c)], dst_ref=outs[a].at[q],
            send_sem=send_sems.at[a * 4 + q], recv_sem=recv_sems.at[a * 4 + q],
            device_id=(x, y, 1 - c), device_id_type=MESH) for a in range(n) for q in range(4)]

    def start(*refs):
        for cp in plan(*refs):
            cp.start()

    def finish(*refs):
        for cp in plan(*refs):
            cp.wait()

    return _Comm(grads, [jax.ShapeDtypeStruct((4,) + t.shape[1:], t.dtype) for t in grads], 4 * n, start, finish)


def _chip_exchange(partials):
    n = len(partials)

    def plan(ins, outs, send_sems, recv_sems):
        x, y, c = _position()
        chips = [(1 - x, y), (x, 1 - y), (1 - x, 1 - y)]
        return [pltpu.make_async_remote_copy(
            src_ref=ins[a].at[2 * chip[0] + chip[1]], dst_ref=outs[a].at[k],
            send_sem=send_sems.at[a * 3 + k], recv_sem=recv_sems.at[a * 3 + k],
            device_id=(*chip, c), device_id_type=MESH) for a in range(n) for k, chip in enumerate(chips)]

    def start(*refs):
        for cp in plan(*refs):
            cp.start()

    def finish(*refs):
        for cp in plan(*refs):
            cp.wait()

    return _Comm(partials, [jax.ShapeDtypeStruct((3,) + t.shape[1:], t.dtype) for t in partials], 3 * n, start, finish)


class _SemWindow:
    def __init__(self, sems, offset):
        self._sems, self._offset = sems, offset

    @property
    def at(self):
        return self

    def __getitem__(self, i):
        return self._sems.at[self._offset + i]


def _join(parts):
    def each(fn_name, cins, couts, send_sems, recv_sems):
        i = o = sem = 0
        for p in parts:
            ni, no = len(p.operands), len(p.out_shape)
            getattr(p, fn_name)(cins[i:i + ni], couts[o:o + no], _SemWindow(send_sems, sem), _SemWindow(recv_sems, sem))
            i, o, sem = i + ni, o + no, sem + p.n_sems

    assert not any(p.aliases for p in parts)
    return _Comm([t for p in parts for t in p.operands], [t for p in parts for t in p.out_shape],
                 sum(p.n_sems for p in parts), functools.partial(each, "start"), functools.partial(each, "finish"))


def _pair_sum_call(grad, recv, place, name):
    _, r, ccols = grad.shape
    tr = _tile(r, 256)

    def body(place_ref, g_ref, r_ref, own_ref, pb_ref):
        p = g_ref[...] + r_ref[...]
        pb_ref[...] = p.astype(BF)

        @pl.when(pl.program_id(1) == place_ref[1])
        def _():
            own_ref[...] = p

    blk = lambda fn: pl.BlockSpec((None, tr, ccols), fn)
    return pl.pallas_call(
        body, name=name,
        grid_spec=pltpu.PrefetchScalarGridSpec(
            num_scalar_prefetch=1, grid=(r // tr, 4),
            in_specs=[blk(lambda i, q, place: (2 * q + place[0], i, 0)), blk(lambda i, q, place: (q, i, 0))],
            out_specs=[pl.BlockSpec((tr, ccols), lambda i, q, place: (i, 0)), blk(lambda i, q, place: (q, i, 0))]),
        out_shape=[jax.ShapeDtypeStruct((r, ccols), F32), jax.ShapeDtypeStruct((4, r, ccols), BF)],
        compiler_params=_params("parallel", "arbitrary"),
    )(place, grad, recv)


def _adamw_math(w, g, m, v):
    m = ADAM_B1 * m + (1.0 - ADAM_B1) * g
    v = ADAM_B2 * v + (1.0 - ADAM_B2) * (g * g)
    m_hat = m / (1.0 - ADAM_B1 ** ADAM_STEP)
    v_hat = v / (1.0 - ADAM_B2 ** ADAM_STEP)
    delta = -ADAM_LR * (m_hat / (jnp.sqrt(v_hat) + ADAM_EPS) + ADAM_WD * w)
    return delta, m, v


def _adamw_matrix_call(own, recv, w, m, v, name):
    r, ccols = w.shape
    gcols = own.shape[1]
    tr = _tile(r, 256)

    def body(p_ref, r_ref, w_ref, m_ref, v_ref, g_out, d_out, m_out, v_out):
        cols = pl.ds(0, ccols)
        g = (p_ref[:, cols] + r_ref[0, :, cols].astype(F32) + r_ref[1, :, cols].astype(F32)
             + r_ref[2, :, cols].astype(F32))
        delta, mn, vn = _adamw_math(w_ref[...], g, m_ref[...], v_ref[...])
        g_out[...] = g
        d_out[...] = delta
        m_out[...] = mn
        v_out[...] = vn

    mat = pl.BlockSpec((tr, ccols), lambda i: (i, 0))
    shp = jax.ShapeDtypeStruct((r, ccols), F32)
    return pl.pallas_call(
        body, name=name, grid=(r // tr,),
        in_specs=[pl.BlockSpec((tr, gcols), lambda i: (i, 0)), pl.BlockSpec((3, tr, gcols), lambda i: (0, i, 0)),
                  mat, mat, mat],
        out_specs=[mat, mat, mat, mat],
        out_shape=[shp, shp, shp, shp],
        compiler_params=_params("parallel"),
    )(own, recv, w, m, v)


def _adamw_vector_call(gathered, w, m, v, name):
    n = w.shape[1]

    def body(p_ref, w_ref, m_ref, v_ref, g_out, d_out, m_out, v_out):
        g = p_ref[0:1, :]
        for k in range(1, N_DEV):
            g = g + p_ref[k:k + 1, :]
        delta, mn, vn = _adamw_math(w_ref[...], g, m_ref[...], v_ref[...])
        g_out[...] = g
        d_out[...] = delta
        m_out[...] = mn
        v_out[...] = vn

    shp = jax.ShapeDtypeStruct((1, n), F32)
    return pl.pallas_call(body, name=name, out_shape=[shp, shp, shp, shp])(gathered, w, m, v)


def _round_up(n, mult):
    return (n + mult - 1) // mult * mult


def kernel(x, ffn1_norm, ffn1_w_gate, ffn1_w_up, ffn1_w_down, mix_norm, w_in, ret_norm_g, hgrn_lb_logits, hgrn_norm_g, w_out, ffn2_norm, ffn2_w_gate, ffn2_w_up, ffn2_w_down, final_norm, loss_target, m_ffn1_norm, m_ffn1_w_gate, m_ffn1_w_up, m_ffn1_w_down, m_mix_norm, m_w_in, m_ret_norm_g, m_hgrn_lb_logits, m_hgrn_norm_g, m_w_out, m_ffn2_norm, m_ffn2_w_gate, m_ffn2_w_up, m_ffn2_w_down, m_final_norm, v_ffn1_norm, v_ffn1_w_gate, v_ffn1_w_up, v_ffn1_w_down, v_mix_norm, v_w_in, v_ret_norm_g, v_hgrn_lb_logits, v_hgrn_norm_g, v_w_out, v_ffn2_norm, v_ffn2_w_gate, v_ffn2_w_up, v_ffn2_w_down, v_final_norm):
    xs = x[0]
    target = loss_target[0]
    s, d = xs.shape
    f_loc = ffn1_w_gate.shape[2]
    fp = _round_up(f_loc, LANE)
    pad_cols = lambda t: jnp.pad(t[0], ((0, 0), (0, fp - f_loc)))
    pad_rows = lambda t: jnp.pad(t[0], ((0, fp - f_loc), (0, 0)))

    mat_names = ["ffn1_w_gate", "ffn1_w_up", "ffn1_w_down", "w_in", "w_out", "ffn2_w_gate", "ffn2_w_up", "ffn2_w_down"]
    mat_pad = [pad_cols, pad_cols, pad_rows, lambda t: t[0], lambda t: t[0], pad_cols, pad_cols, pad_rows]
    mat_w = [ffn1_w_gate, ffn1_w_up, ffn1_w_down, w_in, w_out, ffn2_w_gate, ffn2_w_up, ffn2_w_down]
    mat_m = [m_ffn1_w_gate, m_ffn1_w_up, m_ffn1_w_down, m_w_in, m_w_out, m_ffn2_w_gate, m_ffn2_w_up, m_ffn2_w_down]
    mat_v = [v_ffn1_w_gate, v_ffn1_w_up, v_ffn1_w_down, v_w_in, v_w_out, v_ffn2_w_gate, v_ffn2_w_up, v_ffn2_w_down]

    cx, cy, cc = _position()
    place = jnp.stack([cc, 2 * cx + cy]).astype(jnp.int32)
    mat_index = {nm: i for i, nm in enumerate(mat_names)}
    mat_out = {}

    def pair_sums(names, grads, from_sibling):
        return [_pair_sum_call(g, r, place, "pair_sum_" + nm) for nm, g, r in zip(names, grads, from_sibling)]

    def update(names, sums, from_chips):
        for nm, (p, _), r in zip(names, sums, from_chips):
            i = mat_index[nm]
            res = _adamw_matrix_call(p, r, mat_w[i][0], mat_m[i][0], mat_v[i][0], "adamw_" + nm)
            mat_out[nm] = [t[None] for t in res]

    shards = [p(t.astype(BF)) for p, t in zip(mat_pad, mat_w)]

    (h1,), (wg1, wu1) = _rmsnorm_call(xs, ffn1_norm, "ffn1_norm", comm=_gather_two_level(shards[:2], 0.5))
    (g1, u1, a1), (wd1, win) = _ffn_up_call(h1, wg1, wu1, "ffn1_up", comm=_gather_two_level(shards[2:4], 0.7))
    (x1, h2), _ = _down_call(a1, wd1, xs, mix_norm, FFN_RESIDUAL_WEIGHT, "ffn1_down")
    (proj,), landed_a = _proj_call(h2, win, "mix_in", comm=_gather_round1([shards[4], shards[7]]))
    wmix = proj.shape[1] // 8
    cos, sin = _rope_tables(s, wmix // RET_HEADS)
    consts = _ret_consts(wmix // RET_HEADS)
    (o_hg, m_hg, st_hg), landed_b = _hgrn_fwd_call(proj, hgrn_lb_logits, hgrn_norm_g, "hgrn_fwd",
                                                   comm=_gather_round1(shards[5:7]))
    (o_ret, m_ret, st_ret), (wout, wd2, wg2, wu2) = _ret_fwd_call(proj, cos, sin, consts, ret_norm_g, "ret_fwd",
                                                                  comm=_gather_round2(landed_a + landed_b))
    merged = jnp.concatenate([m_ret, m_hg], axis=1)
    wout_wide = wout.reshape(2, wout.shape[0] * wout.shape[1] // 2, d)
    (x2, h3), _ = _down_call(merged, wout_wide, x1, ffn2_norm, 1.0, "mix_out")
    (g2, u2, a2), _ = _ffn_up_call(h3, wg2, wu2, "ffn2_up")
    (x3,), _ = _down_call(a2, wd2, x2, None, FFN_RESIDUAL_WEIGHT, "ffn2_down")
    loss_part, dx3, dx3b, gv_final = _loss_call(x3, target, final_norm[None, :], "loss_head")

    (dg2, du2), _ = _bwd_up_call(dx3b, wd2, g2, u2, FFN_RESIDUAL_WEIGHT, "ffn2_bwd_up")
    dh3, _ = _bwd_mm_call([(dg2, wg2), (du2, wu2)], "ffn2_bwd_down")
    dx2, dx2b, gv_n3 = _norm_bwd_call(dh3, dx3, x2, ffn2_norm, "ffn2_norm_bwd")
    names_a = ["ffn2_w_gate", "ffn2_w_up", "ffn2_w_down"]
    grads_a = [_wgrad_call(h3, dg2, N_DEV, False, 1.0, "ffn2_wgrad_gate"),
               _wgrad_call(h3, du2, N_DEV, False, 1.0, "ffn2_wgrad_up"),
               _wgrad_call(a2, dx3b, N_DEV, True, FFN_RESIDUAL_WEIGHT, "ffn2_wgrad_down")]

    (dmerged,), sib_a = _proj_call(dx2b, jnp.swapaxes(wout_wide, 1, 2), "mix_out_bwd",
                                   comm=_sibling_exchange(grads_a))
    gm_out = _wgrad_call(merged, dx2b, 2, True, 1.0, "mix_out_wgrad").reshape(wout.shape)
    sums_a = pair_sums(names_a, grads_a, sib_a)
    dproj_half, gv_ret = _ret_bwd_call(proj, cos, sin, consts, ret_norm_g, o_ret, st_ret, dmerged, "ret_bwd")
    (dproj, gv_lb, gv_hg), landed = _hgrn_bwd_call(
        proj, hgrn_lb_logits, hgrn_norm_g, o_hg, st_hg, dmerged, dproj_half, "hgrn_bwd",
        comm=_join([_chip_exchange([pb for _, pb in sums_a]), _sibling_exchange([gm_out])]))
    update(names_a, sums_a, landed[:3])
    sums_out = pair_sums(["w_out"], [gm_out], landed[3:])
    dh2, chips_out = _bwd_mm_call([(dproj, win)], "mix_in_bwd", per_step=2, comm=_chip_exchange([sums_out[0][1]]))
    dx1, dx1b, gv_n2 = _norm_bwd_call(dh2, dx2, x1, mix_norm, "mix_norm_bwd")
    update(["w_out"], sums_out, chips_out)

    gm_in = _wgrad_call(h2, dproj, N_DEV, False, 1.0, "mix_in_wgrad")
    gm_d1, sib_in = _wgrad_call(a1, dx1b, N_DEV, True, FFN_RESIDUAL_WEIGHT, "ffn1_wgrad_down",
                                comm=_sibling_exchange([gm_in]))
    sums_in = pair_sums(["w_in"], [gm_in], sib_in)
    (dg1, du1), landed = _bwd_up_call(dx1b, wd1, g1, u1, FFN_RESIDUAL_WEIGHT, "ffn1_bwd_up",
                                      comm=_join([_chip_exchange([sums_in[0][1]]), _sibling_exchange([gm_d1])]))
    update(["w_in"], sums_in, landed[:1])
    sums_d1 = pair_sums(["ffn1_w_down"], [gm_d1], landed[1:])
    gm_g1, chips_d1 = _wgrad_call(h1, dg1, N_DEV, False, 1.0, "ffn1_wgrad_gate",
                                  comm=_chip_exchange([sums_d1[0][1]]))
    update(["ffn1_w_down"], sums_d1, chips_d1)
    gm_u1, sib_g = _wgrad_call(h1, du1, N_DEV, False, 1.0, "ffn1_wgrad_up", comm=_sibling_exchange([gm_g1]))
    sums_g = pair_sums(["ffn1_w_gate"], [gm_g1], sib_g)
    n_row_tiles = s // _tile(s, 1024)
    assert n_row_tiles >= 2, "the sequence must span at least two row tiles"
    n_first = n_row_tiles // 2
    dh1_part, landed = _bwd_mm_call(
        [(dg1, wg1), (du1, wu1)], "ffn1_bwd_down_a", tiles=(0, n_first),
        comm=_join([_chip_exchange([sums_g[0][1]]), _sibling_exchange([gm_u1])]))
    update(["ffn1_w_gate"], sums_g, landed[:1])
    sums_u = pair_sums(["ffn1_w_up"], [gm_u1], landed[1:])
    dh1, chips_u = _bwd_mm_call(
        [(dg1, wg1), (du1, wu1)], "ffn1_bwd_down_b", tiles=(n_first, n_row_tiles - n_first),
        carry=dh1_part, comm=_chip_exchange([sums_u[0][1]]))
    update(["ffn1_w_up"], sums_u, chips_u)
    dx0, _, gv_n1 = _norm_bwd_call(dh1, dx1, xs, ffn1_norm, "ffn1_norm_bwd")

    vec_names = ["ffn1_norm", "mix_norm", "ret_norm_g", "hgrn_lb_logits", "hgrn_norm_g", "ffn2_norm", "final_norm"]
    vec_g = [gv_n1, gv_n2, gv_ret, gv_lb, gv_hg, gv_n3, gv_final]
    vec_w = [ffn1_norm, mix_norm, ret_norm_g, hgrn_lb_logits, hgrn_norm_g, ffn2_norm, final_norm[None, :]]
    vec_m = [m_ffn1_norm, m_mix_norm, m_ret_norm_g, m_hgrn_lb_logits, m_hgrn_norm_g, m_ffn2_norm, m_final_norm[None, :]]
    vec_v = [v_ffn1_norm, v_mix_norm, v_ret_norm_g, v_hgrn_lb_logits, v_hgrn_norm_g, v_ffn2_norm, v_final_norm[None, :]]
    cat = lambda ts: jnp.concatenate(ts, axis=1)
    (vec_all,) = _all_gather_call([cat(vec_g)], "gather_vector_grads")
    vres = _adamw_vector_call(vec_all[:, 0, :], cat(vec_w), cat(vec_m), cat(vec_v), "adamw_vectors")
    vec_out = {}
    off = 0
    for nm, t in zip(vec_names, vec_w):
        n = t.shape[1]
        parts = [r[:, off:off + n] for r in vres]
        if nm == "final_norm":
            parts = [p[0] for p in parts]
        vec_out[nm] = parts
        off += n

    loss = lax.psum(loss_part[0, 0], ("x", "y", "c"))
    order = ["ffn1_norm", "ffn1_w_gate", "ffn1_w_up", "ffn1_w_down", "mix_norm", "w_in", "ret_norm_g", "hgrn_lb_logits",
             "hgrn_norm_g", "w_out", "ffn2_norm", "ffn2_w_gate", "ffn2_w_up", "ffn2_w_down", "final_norm"]
    res = {**mat_out, **vec_out}
    outs = [loss, dx0[None]]
    for kind in range(4):
        outs += [res[nm][kind] for nm in order]
    return tuple(outs)
```

```python
import functools

import jax
import jax.numpy as jnp
from jax import lax
from jax.experimental import pallas as pl
from jax.experimental.pallas import tpu as pltpu

BF = jnp.bfloat16
F32 = jnp.float32
MESH = pl.DeviceIdType.MESH
HBM_SPEC = pl.BlockSpec(memory_space=pltpu.HBM)

N_DEV = 8
LANE = 128
EPS = 1e-6
ROPE_BASE = 10000.0
RET_HEADS = 4
HGRN_HEADS = 8
RET_CHUNK = 128
HGRN_BLOCK = 16
FFN_RESIDUAL_WEIGHT = 0.5
ADAM_LR = 0.001
ADAM_B1 = 0.9
ADAM_B2 = 0.999
ADAM_EPS = 1e-08
ADAM_WD = 0.01
ADAM_STEP = 10
VMEM_LIMIT = 56 * 1024 * 1024


def _tile(n, pref, mult=8):
    t = min(pref, n)
    t -= t % mult
    while t >= mult:
        if n % t == 0:
            return t
        t -= mult
    return n


def _params(*sem):
    return pltpu.CompilerParams(dimension_semantics=sem, vmem_limit_bytes=VMEM_LIMIT)


class _Comm:
    def __init__(self, operands, out_shape, n_sems, start, finish, aliases=None, middle=None, middle_at=0.0):
        self.operands = list(operands)
        self.out_shape = list(out_shape)
        self.n_sems = n_sems
        self.start = start
        self.finish = finish
        self.aliases = dict(aliases or {})
        self.middle = middle
        self.middle_at = middle_at


def _launch(body, *, name, grid, in_specs, out_specs, out_shape, sem, args, scratch_shapes=(), aliases=None, comm=None):
    in_specs, out_specs, out_shape = list(in_specs), list(out_specs), list(out_shape)
    scratch_shapes = list(scratch_shapes)
    aliases = dict(aliases or {})
    if comm is None:
        res = pl.pallas_call(body, name=name, grid=grid, in_specs=in_specs, out_specs=out_specs, out_shape=out_shape,
                             scratch_shapes=scratch_shapes, input_output_aliases=aliases,
                             compiler_params=_params(*sem))(*args)
        return list(res), []
    n_in, n_out, n_scr = len(in_specs), len(out_specs), len(scratch_shapes)
    ci, co = len(comm.operands), len(comm.out_shape)

    def carrying(*refs):
        bounds = [0, n_in, n_in + ci, n_in + ci + n_out, n_in + ci + n_out + co, n_in + ci + n_out + co + n_scr]
        ins, cins, outs, couts, scr = [refs[a:b] for a, b in zip(bounds[:-1], bounds[1:])]
        send_sems, recv_sems = refs[bounds[-1]:]
        ids = [pl.program_id(k) for k in range(len(grid))]
        first = functools.reduce(jnp.logical_and, [i == 0 for i in ids])
        last = functools.reduce(jnp.logical_and, [i == g - 1 for i, g in zip(ids, grid)])

        @pl.when(first)
        def _():
            comm.start(cins, couts, send_sems, recv_sems)

        if comm.middle is not None:
            step, total = ids[0], grid[0]
            for i, g in zip(ids[1:], grid[1:]):
                step, total = step * g + i, total * g

            @pl.when(step == int(total * comm.middle_at))
            def _():
                comm.middle(cins, couts, send_sems, recv_sems)

        body(*ins, *outs, *scr)

        @pl.when(last)
        def _():
            comm.finish(cins, couts, send_sems, recv_sems)

    res = pl.pallas_call(
        carrying, name=name, grid=grid,
        in_specs=in_specs + [HBM_SPEC] * ci, out_specs=out_specs + [HBM_SPEC] * co,
        out_shape=out_shape + comm.out_shape,
        scratch_shapes=scratch_shapes + [pltpu.SemaphoreType.DMA((comm.n_sems,)), pltpu.SemaphoreType.DMA((comm.n_sems,))],
        input_output_aliases={**aliases, **{n_in + a: n_out + b for a, b in comm.aliases.items()}},
        compiler_params=_params(*(["arbitrary"] * len(grid))),
    )(*args, *comm.operands)
    return list(res[:n_out]), list(res[n_out:])


def _comm_only_call(comm, name):
    def body(*refs):
        ci, co = len(comm.operands), len(comm.out_shape)
        cins, couts = refs[:ci], refs[ci:ci + co]
        send_sems, recv_sems = refs[ci + co:]
        comm.start(cins, couts, send_sems, recv_sems)
        comm.finish(cins, couts, send_sems, recv_sems)

    return pl.pallas_call(
        body, name=name,
        in_specs=[HBM_SPEC] * len(comm.operands), out_specs=[HBM_SPEC] * len(comm.out_shape),
        out_shape=comm.out_shape,
        scratch_shapes=[pltpu.SemaphoreType.DMA((comm.n_sems,)), pltpu.SemaphoreType.DMA((comm.n_sems,))],
        input_output_aliases=comm.aliases,
    )(*comm.operands)


def _sigmoid(v):
    return 0.5 * jnp.tanh(0.5 * v) + 0.5


def _dot(a, b):
    return jnp.dot(a, b, preferred_element_type=F32)


def _dot_nt(a, b):
    return lax.dot_general(a, b, (((1,), (1,)), ((), ())), preferred_element_type=F32)


def _dot_tn(a, b):
    return lax.dot_general(a, b, (((0,), (0,)), ((), ())), preferred_element_type=F32)


def _rmsnorm_call(x, gain, name):
    s, d = x.shape
    tm = _tile(s, 512)

    def body(x_ref, g_ref, o_ref):
        xv = x_ref[...]
        r = lax.rsqrt(jnp.mean(xv * xv, axis=-1, keepdims=True) + EPS)
        o_ref[...] = (xv * r * g_ref[...]).astype(BF)

    return pl.pallas_call(
        body, name=name, grid=(s // tm,),
        in_specs=[pl.BlockSpec((tm, d), lambda i: (i, 0)), pl.BlockSpec((1, d), lambda i: (0, 0))],
        out_specs=pl.BlockSpec((tm, d), lambda i: (i, 0)),
        out_shape=jax.ShapeDtypeStruct((s, d), BF),
        compiler_params=_params("parallel"),
    )(x, gain)


def _ffn_up_call(h, wg, wu, name, comm=None):
    s, d = h.shape
    nj, _, k = wg.shape
    tm = _tile(s, 1024)

    def body(h_ref, wg_ref, wu_ref, g_ref, u_ref, a_ref):
        hv = h_ref[...]
        g = _dot(hv, wg_ref[...])
        u = _dot(hv, wu_ref[...])
        g_ref[...] = g.astype(BF)
        u_ref[...] = u.astype(BF)
        a_ref[...] = (g * _sigmoid(g) * u).astype(BF)

    act = pl.BlockSpec((tm, k), lambda i, j: (i, j))
    wsp = pl.BlockSpec((None, d, k), lambda i, j: (j, 0, 0))
    return _launch(
        body, name=name, grid=(s // tm, nj),
        in_specs=[pl.BlockSpec((tm, d), lambda i, j: (i, 0)), wsp, wsp],
        out_specs=[act, act, act],
        out_shape=[jax.ShapeDtypeStruct((s, nj * k), BF)] * 3,
        sem=("parallel", "arbitrary"), args=(h, wg, wu), comm=comm)


def _proj_call(h, w, name, comm=None):
    s, d = h.shape
    nj, _, k = w.shape
    tm = _tile(s, 1024)

    def body(h_ref, w_ref, o_ref):
        o_ref[...] = _dot(h_ref[...], w_ref[...])

    return _launch(
        body, name=name, grid=(s // tm, nj),
        in_specs=[pl.BlockSpec((tm, d), lambda i, j: (i, 0)), pl.BlockSpec((None, d, k), lambda i, j: (j, 0, 0))],
        out_specs=[pl.BlockSpec((tm, k), lambda i, j: (i, j))],
        out_shape=[jax.ShapeDtypeStruct((s, nj * k), F32)],
        sem=("parallel", "arbitrary"), args=(h, w), comm=comm)


def _down_call(a, w, resid, gain, scale, name, comm=None):
    s = a.shape[0]
    nj, k, d = w.shape
    tm = _tile(s, 1024)
    strip = _tile(tm, 128)
    cn = _tile(d, 512, LANE)
    with_norm = gain is not None

    def body(*refs):
        if with_norm:
            a_ref, w_ref, r_ref, g_ref, x_ref, h_ref = refs
        else:
            a_ref, w_ref, r_ref, x_ref = refs
        j = pl.program_id(1)

        av = a_ref[...]

        @pl.when(j == 0)
        def _():
            for n0 in range(0, d, cn):
                x_ref[:, n0:n0 + cn] = _dot(av, w_ref[:, n0:n0 + cn])

        @pl.when(j > 0)
        def _():
            for n0 in range(0, d, cn):
                x_ref[:, n0:n0 + cn] += _dot(av, w_ref[:, n0:n0 + cn])

        @pl.when(j == nj - 1)
        def _():
            for r0 in range(0, tm, strip):
                rows = slice(r0, r0 + strip)
                xn = r_ref[rows, :] + (scale * x_ref[rows, :])
                x_ref[rows, :] = xn
                if with_norm:
                    r = lax.rsqrt(jnp.mean(xn * xn, axis=-1, keepdims=True) + EPS)
                    h_ref[rows, :] = (xn * r * g_ref[...]).astype(BF)

    row = pl.BlockSpec((tm, d), lambda i, j: (i, 0))
    in_specs = [pl.BlockSpec((tm, k), lambda i, j: (i, j)), pl.BlockSpec((None, k, d), lambda i, j: (j, 0, 0)), row]
    args = [a, w, resid]
    out_specs = [row]
    out_shape = [jax.ShapeDtypeStruct((s, d), F32)]
    if with_norm:
        in_specs.append(pl.BlockSpec((1, d), lambda i, j: (0, 0)))
        args.append(gain)
        out_specs.append(row)
        out_shape.append(jax.ShapeDtypeStruct((s, d), BF))
    return _launch(
        body, name=name, grid=(s // tm, nj),
        in_specs=in_specs, out_specs=out_specs, out_shape=out_shape,
        sem=("parallel", "arbitrary"), args=args, comm=comm)


def _loss_call(x, target, gain, name):
    s, d = x.shape
    tm = _tile(s, 512)

    def body(x_ref, t_ref, g_ref, loss_ref, dx_ref, dxb_ref, dg_ref):
        i = pl.program_id(0)

        @pl.when(i == 0)
        def _():
            loss_ref[...] = jnp.zeros_like(loss_ref)
            dg_ref[...] = jnp.zeros_like(dg_ref)

        xv = x_ref[...]
        gv = g_ref[...]
        r = lax.rsqrt(jnp.mean(xv * xv, axis=-1, keepdims=True) + EPS)
        xhat = xv * r
        err = xhat * gv - t_ref[...]
        per_tok = jnp.mean(err * err, axis=-1, keepdims=True)
        loss_ref[...] += 0.5 * jnp.sum(per_tok, axis=0, keepdims=True)
        dout = err * (1.0 / d)
        dg_ref[...] += jnp.sum(dout * xhat, axis=0, keepdims=True)
        dxhat = dout * gv
        dx = r * (dxhat - xhat * jnp.mean(dxhat * xhat, axis=-1, keepdims=True))
        dx_ref[...] = dx
        dxb_ref[...] = dx.astype(BF)

    row = pl.BlockSpec((tm, d), lambda i: (i, 0))
    vec = pl.BlockSpec((1, d), lambda i: (0, 0))
    return pl.pallas_call(
        body, name=name, grid=(s // tm,),
        in_specs=[row, row, vec],
        out_specs=[pl.BlockSpec((1, 1), lambda i: (0, 0)), row, row, vec],
        out_shape=[jax.ShapeDtypeStruct((1, 1), F32), jax.ShapeDtypeStruct((s, d), F32),
                   jax.ShapeDtypeStruct((s, d), BF), jax.ShapeDtypeStruct((1, d), F32)],
        compiler_params=_params("arbitrary"),
    )(x, target, gain)


def _bwd_up_call(dy, wd, g, u, scale, name, comm=None):
    s, d = dy.shape
    nj, k, _ = wd.shape
    tm = _tile(s, 1024)

    def body(dy_ref, w_ref, g_ref, u_ref, dg_ref, du_ref):
        da = scale * _dot_nt(dy_ref[...], w_ref[...])
        gv = g_ref[...].astype(F32)
        sig = _sigmoid(gv)
        du_ref[...] = (da * gv * sig).astype(BF)
        dg_ref[...] = (da * u_ref[...].astype(F32) * sig * (1.0 + gv * (1.0 - sig))).astype(BF)

    act = pl.BlockSpec((tm, k), lambda i, j: (i, j))
    return _launch(
        body, name=name, grid=(s // tm, nj),
        in_specs=[pl.BlockSpec((tm, d), lambda i, j: (i, 0)), pl.BlockSpec((None, k, d), lambda i, j: (j, 0, 0)), act, act],
        out_specs=[act, act],
        out_shape=[jax.ShapeDtypeStruct((s, nj * k), BF), jax.ShapeDtypeStruct((s, nj * k), BF)],
        sem=("parallel", "arbitrary"), args=(dy, wd, g, u), comm=comm)


def _norm_bwd_call(dh, dres, xin, gain, name):
    s, d = xin.shape
    tm = _tile(s, 256)

    def body(dh_ref, dres_ref, x_ref, g_ref, dx_ref, dxb_ref, dg_ref):
        @pl.when(pl.program_id(0) == 0)
        def _():
            dg_ref[...] = jnp.zeros_like(dg_ref)

        xv = x_ref[...]
        r = lax.rsqrt(jnp.mean(xv * xv, axis=-1, keepdims=True) + EPS)
        xhat = xv * r
        dh_v = dh_ref[...]
        dg_ref[...] += jnp.sum(dh_v * xhat, axis=0, keepdims=True)
        dxhat = dh_v * g_ref[...]
        dx = dres_ref[...] + r * (dxhat - xhat * jnp.mean(dxhat * xhat, axis=-1, keepdims=True))
        dx_ref[...] = dx
        dxb_ref[...] = dx.astype(BF)

    row = pl.BlockSpec((tm, d), lambda i: (i, 0))
    vec = pl.BlockSpec((1, d), lambda i: (0, 0))
    return pl.pallas_call(
        body, name=name, grid=(s // tm,),
        in_specs=[row, row, row, vec], out_specs=[row, row, vec],
        out_shape=[jax.ShapeDtypeStruct((s, d), F32), jax.ShapeDtypeStruct((s, d), BF), jax.ShapeDtypeStruct((1, d), F32)],
        compiler_params=_params("arbitrary"),
    )(dh, dres, xin, gain)


def _bwd_mm_call(pairs, name, per_step=1, tiles=None, carry=None, comm=None):
    a0 = pairs[0][0]
    s = a0.shape[-2]
    nblocks, d, k = pairs[0][1].shape
    nj = nblocks // per_step
    npair = len(pairs) * per_step
    tm = _tile(s, 1024)
    cn = _tile(d, 512, LANE)

    first_tile, n_tiles = (0, s // tm) if tiles is None else tiles

    def body(*refs):
        a_refs = refs[0:2 * npair:2]
        w_refs = refs[1:2 * npair:2]
        dh_ref = refs[-1]

        @pl.when(pl.program_id(1) == 0)
        def _():
            dh_ref[...] = jnp.zeros_like(dh_ref)

        for a_ref, w_ref in zip(a_refs, w_refs):
            av = a_ref[...]
            for n0 in range(0, d, cn):
                dh_ref[:, n0:n0 + cn] += _dot_nt(av, w_ref[n0:n0 + cn, :])

    row = pl.BlockSpec((tm, d), lambda i, j: (first_tile + i, 0))
    in_specs, args = [], []
    for a, w in pairs:
        for r in range(per_step):
            in_specs += [pl.BlockSpec((None, tm, k), lambda i, j, r=r: (j * per_step + r, first_tile + i, 0))
                         if a.ndim == 3 else
                         pl.BlockSpec((tm, k), lambda i, j, r=r: (first_tile + i, j * per_step + r)),
                         pl.BlockSpec((None, d, k), lambda i, j, r=r: (j * per_step + r, 0, 0))]
            args += [a, w]
    aliases = {}
    if carry is not None:
        aliases = {len(in_specs): 0}
        in_specs.append(pl.BlockSpec(memory_space=pl.ANY))
        args.append(carry)
    (dh,), landed = _launch(
        body, name=name, grid=(n_tiles, nj),
        in_specs=in_specs, out_specs=[row], out_shape=[jax.ShapeDtypeStruct((s, d), F32)],
        sem=("parallel", "arbitrary"), args=args, aliases=aliases, comm=comm)
    return dh, landed


def _wgrad_call(a, b, nj, a_blocked, scale, name, comm=None):
    s = a.shape[0]
    ka = a.shape[1] // nj if a_blocked else a.shape[1]
    b_stacked = b.ndim == 3
    kb = b.shape[-1] if (a_blocked or b_stacked) else b.shape[1] // nj
    ts = _tile(s, 2048)
    ns = s // ts

    def body(a_ref, b_ref, o_ref):
        t = pl.program_id(1)

        @pl.when(t == 0)
        def _():
            o_ref[...] = _dot_tn(a_ref[...], b_ref[...])

        @pl.when(t > 0)
        def _():
            o_ref[...] += _dot_tn(a_ref[...], b_ref[...])

        if scale != 1.0:
            @pl.when(t == ns - 1)
            def _():
                o_ref[...] = o_ref[...] * scale

    a_spec = pl.BlockSpec((ts, ka), (lambda j, t: (t, j)) if a_blocked else (lambda j, t: (t, 0)))
    if b_stacked:
        b_spec = pl.BlockSpec((None, ts, kb), lambda j, t: (j, t, 0))
    else:
        b_spec = pl.BlockSpec((ts, kb), (lambda j, t: (t, 0)) if a_blocked else (lambda j, t: (t, j)))
    (out,), landed = _launch(
        body, name=name, grid=(nj, ns),
        in_specs=[a_spec, b_spec],
        out_specs=[pl.BlockSpec((None, ka, kb), lambda j, t: (j, 0, 0))],
        out_shape=[jax.ShapeDtypeStruct((nj, ka, kb), F32)],
        sem=("parallel", "arbitrary"), args=(a, b), comm=comm)
    return out if comm is None else (out, landed)


def _rope(v, cos, sin):
    half = v.shape[-1] // 2
    v1, v2 = v[:, :half], v[:, half:]
    return jnp.concatenate([v1 * cos - v2 * sin, v2 * cos + v1 * sin], axis=-1)


def _rope_bwd(dv, cos, sin):
    half = dv.shape[-1] // 2
    d1, d2 = dv[:, :half], dv[:, half:]
    return jnp.concatenate([d1 * cos + d2 * sin, d2 * cos - d1 * sin], axis=-1)


def _ret_consts(hd):
    c = RET_CHUNK
    log_gamma = jnp.log(1.0 - jnp.exp2(-5.0 - jnp.arange(RET_HEADS, dtype=F32)))
    idx = jnp.arange(c, dtype=F32)
    rel = idx[:, None] - idx[None, :]
    mask = rel >= 0
    decay = jnp.where(mask[None], jnp.exp(log_gamma[:, None, None] * jnp.where(mask, rel, 0.0)[None]), 0.0)
    qdec = jnp.exp(log_gamma[:, None] * (idx + 1.0)[None, :])
    kdec = jnp.exp(log_gamma[:, None] * (c - 1.0 - idx)[None, :])
    gchunk = jnp.exp(log_gamma * c)
    bc = lambda t: jnp.broadcast_to(t[:, :, None], (RET_HEADS, t.shape[1], hd))
    return decay, bc(qdec), bc(kdec), bc(gchunk[:, None])


def _rope_tables(s, hd):
    inv = jnp.power(ROPE_BASE, -jnp.arange(0, hd, 2, dtype=F32) / hd)
    ang = jnp.arange(s, dtype=F32)[:, None] * inv[None, :]
    return jnp.cos(ang), jnp.sin(ang)


def _ret_fwd_call(proj, cos, sin, consts, gret, name, comm=None):
    s = proj.shape[0]
    w = proj.shape[1] // 8
    hd = w // RET_HEADS
    c = RET_CHUNK
    tt = _tile(s, 512, c)
    nc = tt // c
    decay, qdec, kdec, gch = consts
    scale = hd ** -0.5

    def body(q_ref, k_ref, v_ref, gate_ref, cos_ref, sin_ref, dec_ref, qd_ref, kd_ref, gc_ref, gn_ref,
             o_ref, m_ref, st_ref, state):
        @pl.when(pl.program_id(1) == 0)
        def _():
            state[...] = jnp.zeros_like(state)

        dec = dec_ref[...]
        for ci in range(nc):
            rows = slice(ci * c, (ci + 1) * c)
            cs, sn = cos_ref[rows, :], sin_ref[rows, :]
            q = _rope(q_ref[rows, :], cs, sn) * scale
            k = _rope(k_ref[rows, :], cs, sn)
            vb = v_ref[rows, :].astype(BF)
            sc = _dot_nt(q.astype(BF), k.astype(BF)) * dec
            prev = state[...]
            st_ref[ci] = prev
            o = _dot(sc.astype(BF), vb) + _dot((q * qd_ref[...]).astype(BF), prev.astype(BF))
            state[...] = gc_ref[...] * prev + _dot_tn((k * kd_ref[...]).astype(BF), vb)
            o_ref[rows, :] = o
            mu = jnp.mean(o, axis=-1, keepdims=True)
            cen = o - mu
            xhat = cen * lax.rsqrt(jnp.mean(cen * cen, axis=-1, keepdims=True) + EPS)
            gt = gate_ref[rows, :]
            m_ref[rows, :] = (xhat * gn_ref[...] * (gt * _sigmoid(gt))).astype(BF)

    nh = RET_HEADS
    comp = lambda j: pl.BlockSpec((tt, hd), lambda h, t, j=j: (t, j * nh + h))
    tab = pl.BlockSpec((tt, hd // 2), lambda h, t: (t, 0))
    per_head = lambda r: pl.BlockSpec((None, r, hd), lambda h, t: (h, 0, 0))
    return _launch(
        body, name=name, grid=(nh, s // tt),
        in_specs=[comp(0), comp(1), comp(2), comp(3), tab, tab,
                  pl.BlockSpec((None, c, c), lambda h, t: (h, 0, 0)), per_head(c), per_head(c), per_head(1),
                  pl.BlockSpec((1, hd), lambda h, t: (0, h))],
        out_specs=[pl.BlockSpec((tt, hd), lambda h, t: (t, h)), pl.BlockSpec((tt, hd), lambda h, t: (t, h)),
                   pl.BlockSpec((None, nc, hd, hd), lambda h, t: (h, t, 0, 0))],
        out_shape=[jax.ShapeDtypeStruct((s, w), F32), jax.ShapeDtypeStruct((s, w), BF),
                   jax.ShapeDtypeStruct((nh, s // c, hd, hd), F32)],
        scratch_shapes=[pltpu.VMEM((hd, hd), F32)],
        sem=("parallel", "arbitrary"),
        args=(proj, proj, proj, proj, cos, sin, decay, qdec, kdec, gch, gret), comm=comm)


def _ret_bwd_call(proj, cos, sin, consts, gret, o_raw, states, dmerged, name):
    s = proj.shape[0]
    w = proj.shape[1] // 8
    hd = w // RET_HEADS
    c = RET_CHUNK
    tt = _tile(s, 512, c)
    nc = tt // c
    nt = s // tt
    decay, qdec, kdec, gch = consts
    scale = hd ** -0.5

    def body(q_ref, k_ref, v_ref, gate_ref, cos_ref, sin_ref, dec_ref, qd_ref, kd_ref, gc_ref, gn_ref,
             o_ref, st_ref, dm_ref, dp_ref, dgn_ref, dstate):
        @pl.when(pl.program_id(1) == 0)
        def _():
            dstate[...] = jnp.zeros_like(dstate)
            dgn_ref[...] = jnp.zeros_like(dgn_ref)

        dec = dec_ref[...]
        gn = gn_ref[...]
        for ci in reversed(range(nc)):
            rows = slice(ci * c, (ci + 1) * c)
            cs, sn = cos_ref[rows, :], sin_ref[rows, :]
            q = _rope(q_ref[rows, :], cs, sn) * scale
            k = _rope(k_ref[rows, :], cs, sn)
            qb, kb = q.astype(BF), k.astype(BF)
            vb = v_ref[rows, :].astype(BF)
            sc = _dot_nt(qb, kb) * dec
            o = o_ref[rows, :]
            mu = jnp.mean(o, axis=-1, keepdims=True)
            cen = o - mu
            rstd = lax.rsqrt(jnp.mean(cen * cen, axis=-1, keepdims=True) + EPS)
            xhat = cen * rstd
            gt = gate_ref[rows, :]
            sig = _sigmoid(gt)
            sg = gt * sig
            dm = dm_ref[rows, :]
            dgn_ref[...] += jnp.sum(dm * xhat * sg, axis=0, keepdims=True)
            dp_ref[3, rows, :] = (dm * xhat * gn * sig * (1.0 + gt * (1.0 - sig))).astype(BF)
            dxhat = dm * gn * sg
            do = rstd * (dxhat - jnp.mean(dxhat, axis=-1, keepdims=True)
                         - xhat * jnp.mean(dxhat * xhat, axis=-1, keepdims=True))
            dob = do.astype(BF)
            prev = st_ref[ci]
            ds = dstate[...]
            dsb = ds.astype(BF)
            dsc = (_dot_nt(dob, vb) * dec).astype(BF)
            dq = _dot(dsc, kb) + _dot_nt(dob, prev.astype(BF)) * qd_ref[...]
            dk = _dot_tn(dsc, qb) + _dot_nt(vb, dsb) * kd_ref[...]
            dv = _dot_tn(sc.astype(BF), dob) + _dot((k * kd_ref[...]).astype(BF), dsb)
            dstate[...] = gc_ref[...] * ds + _dot_tn((q * qd_ref[...]).astype(BF), dob)
            dp_ref[0, rows, :] = _rope_bwd(dq * scale, cs, sn).astype(BF)
            dp_ref[1, rows, :] = _rope_bwd(dk, cs, sn).astype(BF)
            dp_ref[2, rows, :] = dv.astype(BF)

    nh = RET_HEADS
    rev = lambda t: nt - 1 - t
    comp = lambda j: pl.BlockSpec((tt, hd), lambda h, t, j=j: (rev(t), j * nh + h))
    tab = pl.BlockSpec((tt, hd // 2), lambda h, t: (rev(t), 0))
    per_head = lambda r: pl.BlockSpec((None, r, hd), lambda h, t: (h, 0, 0))
    head_cols = pl.BlockSpec((tt, hd), lambda h, t: (rev(t), h))
    gvec = pl.BlockSpec((1, hd), lambda h, t: (0, h))
    act = jax.ShapeDtypeStruct((s, w), BF)
    return pl.pallas_call(
        body, name=name, grid=(nh, nt),
        in_specs=[comp(0), comp(1), comp(2), comp(3), tab, tab,
                  pl.BlockSpec((None, c, c), lambda h, t: (h, 0, 0)), per_head(c), per_head(c), per_head(1), gvec,
                  head_cols, pl.BlockSpec((None, nc, hd, hd), lambda h, t: (h, rev(t), 0, 0)), head_cols],
        out_specs=[pl.BlockSpec((4, tt, hd), lambda h, t: (0, rev(t), h)), gvec],
        out_shape=[jax.ShapeDtypeStruct((8, s, w), BF), jax.ShapeDtypeStruct((1, w), F32)],
        scratch_shapes=[pltpu.VMEM((hd, hd), F32)],
        compiler_params=_params("parallel", "arbitrary"),
    )(proj, proj, proj, proj, cos, sin, decay, qdec, kdec, gch, gret, o_raw, states, dmerged)


def _block_tri(n, bs, upper):
    r = jnp.arange(n)[:, None]
    cidx = jnp.arange(n)[None, :]
    same = (r // bs) == (cidx // bs)
    return jnp.where(same & ((cidx >= r) if upper else (cidx <= r)), 1.0, 0.0).astype(BF)


def _dot_exact(tri, v):
    hi = v.astype(BF)
    rest = v - hi.astype(F32)
    mid = rest.astype(BF)
    lo = (rest - mid.astype(F32)).astype(BF)
    return _dot(tri, hi) + _dot(tri, mid) + _dot(tri, lo)


def _hgrn_gates(z, lbv):
    sz = _sigmoid(z)
    oml = 1.0 - lbv
    f = lbv + oml * sz
    key = oml * (1.0 - sz)
    return sz, f, key


def _hgrn_fwd_call(proj, lb_logits, ghg, name, comm=None):
    s = proj.shape[0]
    w = proj.shape[1] // 8
    nh = HGRN_HEADS
    hd = w // nh
    bs = HGRN_BLOCK
    tt = _tile(s, 256, bs)
    nb = tt // bs

    def body(q_ref, z_ref, v_ref, gate_ref, lb_ref, gn_ref, tril_ref, o_ref, m_ref, st_ref, state, upd):
        @pl.when(pl.program_id(1) == 0)
        def _():
            state[...] = jnp.zeros_like(state)

        lbv = _sigmoid(lb_ref[...])
        _, f, key = _hgrn_gates(z_ref[...], lbv)
        qr = q_ref[...]
        q = qr * _sigmoid(qr)
        v = v_ref[...]
        g = _dot_exact(tril_ref[...], jnp.log(f))
        blocks = lambda t: t.reshape(nb, bs, hd)
        g3, q3, k3, v3 = blocks(g), blocks(q), blocks(key), blocks(v)
        glast3 = g3[:, bs - 1:bs, :]
        grp = _tile(nb, 4, 1)
        rid = lax.broadcasted_iota(jnp.int32, (grp, bs, hd), 1)
        o_parts = []
        for b0 in range(0, nb, grp):
            gs = slice(b0, b0 + grp)
            g_g, q_g, k_g, v_g = g3[gs], q3[gs], k3[gs], v3[gs]
            o_g = jnp.zeros((grp, bs, hd), F32)
            for j in range(bs):
                wj = jnp.where(rid >= j, jnp.exp(jnp.minimum(g_g - g_g[:, j:j + 1, :], 0.0)), 0.0)
                a = jnp.sum(q_g * k_g[:, j:j + 1, :] * wj, axis=-1, keepdims=True)
                o_g = o_g + a * v_g[:, j:j + 1, :]
            o_parts.append(o_g)
        o3 = jnp.concatenate(o_parts, axis=0)
        ktb = (k3 * jnp.exp(glast3 - g3)).reshape(tt, hd).astype(BF)
        vb = v.astype(BF)
        for b in range(nb):
            rows = slice(b * bs, (b + 1) * bs)
            upd[b] = _dot_tn(vb[rows, :], ktb[rows, :])
        egl3 = jnp.exp(glast3)
        st = state[...]
        for b in range(nb):
            st_ref[b] = st
            st = st * egl3[b] + upd[b]
        state[...] = st
        qgb = (q * jnp.exp(g)).astype(BF)
        o_intra = o3.reshape(tt, hd)
        gn = gn_ref[...]
        for b in range(nb):
            rows = slice(b * bs, (b + 1) * bs)
            o = o_intra[rows, :] + _dot_nt(qgb[rows, :], st_ref[b].astype(BF))
            o_ref[rows, :] = o
            gt = gate_ref[rows, :]
            xhat = o * lax.rsqrt(jnp.mean(o * o, axis=-1, keepdims=True) + EPS)
            m_ref[rows, :] = (xhat * gn * (gt * _sigmoid(gt))).astype(BF)

    comp = lambda j: pl.BlockSpec((tt, hd), lambda h, t, j=j: (t, j * nh + h))
    gvec = pl.BlockSpec((1, hd), lambda h, t: (0, h))
    head_cols = pl.BlockSpec((tt, hd), lambda h, t: (t, h))
    return _launch(
        body, name=name, grid=(nh, s // tt),
        in_specs=[comp(4), comp(5), comp(6), comp(7), gvec, gvec, pl.BlockSpec((tt, tt), lambda h, t: (0, 0))],
        out_specs=[head_cols, head_cols, pl.BlockSpec((None, nb, hd, hd), lambda h, t: (h, t, 0, 0))],
        out_shape=[jax.ShapeDtypeStruct((s, w), F32), jax.ShapeDtypeStruct((s, w), BF),
                   jax.ShapeDtypeStruct((nh, s // bs, hd, hd), F32)],
        scratch_shapes=[pltpu.VMEM((hd, hd), F32), pltpu.VMEM((nb, hd, hd), F32)],
        sem=("parallel", "arbitrary"),
        args=(proj, proj, proj, proj, lb_logits, ghg, _block_tri(tt, bs, upper=False)), comm=comm)


def _hgrn_bwd_call(proj, lb_logits, ghg, o_raw, states, dmerged, stack, name, comm=None):
    s = proj.shape[0]
    w = proj.shape[1] // 8
    nh = HGRN_HEADS
    hd = w // nh
    bs = HGRN_BLOCK
    tt = _tile(s, 256, bs)
    nb = tt // bs
    nt = s // tt

    def body(q_ref, z_ref, v_ref, gate_ref, lb_ref, gn_ref, tril_ref, triu_ref, o_ref, st_ref, dm_ref, stack_ref,
             dp_ref, dlb_ref, dgn_ref,
             dstate, ds_all, inc, dq_s, dk_s, dv_s, dgl_s, dk_rows, dv_rows):
        @pl.when(pl.program_id(1) == 0)
        def _():
            dstate[...] = jnp.zeros_like(dstate)
            dlb_ref[...] = jnp.zeros_like(dlb_ref)
            dgn_ref[...] = jnp.zeros_like(dgn_ref)

        lbv = _sigmoid(lb_ref[...])
        oml = 1.0 - lbv
        gn = gn_ref[...]
        sz, f, key = _hgrn_gates(z_ref[...], lbv)
        qr = q_ref[...]
        sq = _sigmoid(qr)
        q = qr * sq
        v = v_ref[...]
        g = _dot_exact(tril_ref[...], jnp.log(f))
        eg = jnp.exp(g)
        blocks = lambda t: t.reshape(nb, bs, hd)
        g3, q3, k3, v3 = blocks(g), blocks(q), blocks(key), blocks(v)
        glast3 = g3[:, bs - 1:bs, :]
        egl3 = jnp.exp(glast3)
        ktail3 = jnp.exp(glast3 - g3)
        o = o_ref[...]
        rstd = lax.rsqrt(jnp.mean(o * o, axis=-1, keepdims=True) + EPS)
        xhat = o * rstd
        gt = gate_ref[...]
        sig = _sigmoid(gt)
        sg = gt * sig
        dm = dm_ref[...]
        dgn_ref[...] += jnp.sum(dm * xhat * sg, axis=0, keepdims=True)
        del stack_ref
        dp_ref[3] = (dm * xhat * gn * sig * (1.0 + gt * (1.0 - sig))).astype(BF)
        dxhat = dm * gn * sg
        do = rstd * (dxhat - xhat * jnp.mean(dxhat * xhat, axis=-1, keepdims=True))
        dob = do.astype(BF)
        do3 = blocks(do)
        qgb = (q * eg).astype(BF)
        for b in range(nb):
            rows = slice(b * bs, (b + 1) * bs)
            inc[b] = _dot_tn(dob[rows, :], qgb[rows, :])
        ds = dstate[...]
        for b in reversed(range(nb)):
            ds_all[b] = ds
            ds = ds * egl3[b] + inc[b]
        dstate[...] = ds
        ktb = (k3 * ktail3).reshape(tt, hd).astype(BF)
        vb = v.astype(BF)
        for b in range(nb):
            rows = slice(b * bs, (b + 1) * bs)
            prev = st_ref[b]
            dsb = ds_all[b]
            dsbb = dsb.astype(BF)
            dq_s[rows, :] = _dot(dob[rows, :], prev.astype(BF))
            dk_s[rows, :] = _dot(vb[rows, :], dsbb)
            dv_s[rows, :] = _dot_nt(ktb[rows, :], dsbb)
            dgl_s[b] = jnp.sum(prev * dsb, axis=0, keepdims=True)
        dq3 = blocks(dq_s[...] * eg)
        dk3 = blocks(dk_s[...]) * ktail3
        dg_last3 = jnp.sum(k3 * dk3, axis=1, keepdims=True) + egl3 * dgl_s[...]
        row_id = lax.broadcasted_iota(jnp.int32, (nb, bs, hd), 1)
        grp = _tile(nb, 4, 1)
        dq_parts = []
        for b0 in range(0, nb, grp):
            gs = slice(b0, b0 + grp)
            g_g, q_g, k_g, v_g, do_g, dq_g, rid = g3[gs], q3[gs], k3[gs], v3[gs], do3[gs], dq3[gs], row_id[gs]
            for j in range(bs):
                wj = jnp.where(rid >= j, jnp.exp(jnp.minimum(g_g - g_g[:, j:j + 1, :], 0.0)), 0.0)
                kj = k_g[:, j:j + 1, :]
                a = jnp.sum(q_g * kj * wj, axis=-1, keepdims=True)
                da = jnp.sum(do_g * v_g[:, j:j + 1, :], axis=-1, keepdims=True)
                dv_rows[gs, j:j + 1, :] = jnp.sum(a * do_g, axis=1, keepdims=True)
                dq_g = dq_g + da * kj * wj
                dk_rows[gs, j:j + 1, :] = jnp.sum(da * q_g * wj, axis=1, keepdims=True)
            dq_parts.append(dq_g)
        dq3 = jnp.concatenate(dq_parts, axis=0)
        dk3 = dk3 + dk_rows[...]
        dv = dv_s[...] + dv_rows[...].reshape(tt, hd)
        dg3 = q3 * dq3 - k3 * dk3 + jnp.where(row_id == bs - 1, dg_last3, 0.0)
        dlf = _dot_exact(triu_ref[...], dg3.reshape(tt, hd))
        dk = dk3.reshape(tt, hd)
        dfk = dlf / f - dk
        dlb_ref[...] += jnp.sum(dfk * (1.0 - sz), axis=0, keepdims=True) * (lbv * oml)
        dp_ref[1] = (dfk * oml * sz * (1.0 - sz)).astype(BF)
        dp_ref[0] = (dq3.reshape(tt, hd) * sq * (1.0 + qr * (1.0 - sq))).astype(BF)
        dp_ref[2] = dv.astype(BF)

    rev = lambda t: nt - 1 - t
    comp = lambda j: pl.BlockSpec((tt, hd), lambda h, t, j=j: (rev(t), j * nh + h))
    gvec = pl.BlockSpec((1, hd), lambda h, t: (0, h))
    head_cols = pl.BlockSpec((tt, hd), lambda h, t: (rev(t), h))
    tri = pl.BlockSpec((tt, tt), lambda h, t: (0, 0))
    act = jax.ShapeDtypeStruct((s, w), BF)
    vec = jax.ShapeDtypeStruct((1, w), F32)
    tile_f32 = pltpu.VMEM((tt, hd), F32)
    return _launch(
        body, name=name, grid=(nh, nt),
        in_specs=[comp(4), comp(5), comp(6), comp(7), gvec, gvec, tri, tri, head_cols,
                  pl.BlockSpec((None, nb, hd, hd), lambda h, t: (h, rev(t), 0, 0)),
                  pl.BlockSpec((tt, hd), lambda h, t: (rev(t), nh + h)),
                  pl.BlockSpec(memory_space=pl.ANY)],
        out_specs=[pl.BlockSpec((4, tt, hd), lambda h, t: (1, rev(t), h)), gvec, gvec],
        out_shape=[jax.ShapeDtypeStruct(stack.shape, stack.dtype), vec, vec],
        scratch_shapes=[pltpu.VMEM((hd, hd), F32), pltpu.VMEM((nb, hd, hd), F32), pltpu.VMEM((nb, hd, hd), F32),
                        tile_f32, tile_f32, tile_f32, pltpu.VMEM((nb, 1, hd), F32),
                        pltpu.VMEM((nb, bs, hd), F32), pltpu.VMEM((nb, bs, hd), F32)],
        sem=("parallel", "arbitrary"),
        args=(proj, proj, proj, proj, lb_logits, ghg, _block_tri(tt, bs, upper=False), _block_tri(tt, bs, upper=True),
              o_raw, states, dmerged, stack), aliases={11: 0}, comm=comm)


def _position():
    return lax.axis_index("x"), lax.axis_index("y"), lax.axis_index("c")


def _all_gather_call(shards, name):
    n = len(shards)

    def body(*refs):
        ins, outs = refs[:n], refs[n:2 * n]
        send_sems, recv_sems, local_sems = refs[2 * n:]
        x, y, c = _position()
        me, sibling = (x, y, c), (x, y, 1 - c)
        chips = [(1 - x, y), (x, 1 - y), (1 - x, 1 - y)]

        def slot(a, p):
            return outs[a].at[4 * p[0] + 2 * p[1] + p[2]]

        def copy(a, k, block, to, src=None):
            return pltpu.make_async_remote_copy(
                src_ref=slot(a, block) if src is None else src, dst_ref=slot(a, block),
                send_sem=send_sems.at[a * 7 + k], recv_sem=recv_sems.at[a * 7 + k],
                device_id=to, device_id_type=MESH)

        mine = [pltpu.make_async_copy(ins[a], slot(a, me), local_sems.at[a]) for a in range(n)]
        for cp in mine:
            cp.start()
        first = []
        for a in range(n):
            first.append(copy(a, 0, me, sibling, src=ins[a]))
            first += [copy(a, 1 + j, me, (*chip, c), src=ins[a]) for j, chip in enumerate(chips)]
        for cp in first:
            cp.start()
        passed = []
        for j, chip in enumerate(chips):
            for a in range(n):
                copy(a, 1 + j, (*chip, c), me).wait_recv()
                fwd = copy(a, 4 + j, (*chip, c), sibling)
                fwd.start()
                passed.append(fwd)
        for a in range(n):
            copy(a, 0, sibling, me).wait_recv()
            for j, chip in enumerate(chips):
                copy(a, 4 + j, (*chip, 1 - c), me).wait_recv()
        for cp in first + passed:
            cp.wait_send()
        for cp in mine:
            cp.wait()

    return pl.pallas_call(
        body, name=name,
        in_specs=[HBM_SPEC] * n, out_specs=[HBM_SPEC] * n,
        out_shape=[jax.ShapeDtypeStruct((N_DEV,) + t.shape, t.dtype) for t in shards],
        scratch_shapes=[pltpu.SemaphoreType.DMA((7 * n,)), pltpu.SemaphoreType.DMA((7 * n,)),
                        pltpu.SemaphoreType.DMA((n,))],
    )(*shards)


def _slot(ref, p):
    return ref.at[4 * p[0] + 2 * p[1] + p[2]]


def _gather_round1(shards):
    n = len(shards)

    def plan(ins, outs, send_sems, recv_sems):
        x, y, c = _position()
        me = (x, y, c)
        peers = [(x, y, 1 - c), (1 - x, y, c), (x, 1 - y, c), (1 - x, 1 - y, c)]
        sends, recvs, local = [], [], []
        for a in range(n):
            local.append(pltpu.make_async_copy(ins[a], _slot(outs[a], me), send_sems.at[4 * n + a]))
            for k, peer in enumerate(peers):
                sems = dict(send_sem=send_sems.at[4 * a + k], recv_sem=recv_sems.at[4 * a + k],
                            device_id=peer, device_id_type=MESH)
                sends.append(pltpu.make_async_remote_copy(src_ref=ins[a], dst_ref=_slot(outs[a], me), **sems))
                recvs.append(pltpu.make_async_remote_copy(src_ref=ins[a], dst_ref=_slot(outs[a], peer), **sems))
        return sends, recvs, local

    def start(*refs):
        sends, _, local = plan(*refs)
        for cp in local + sends:
            cp.start()

    def finish(*refs):
        sends, recvs, local = plan(*refs)
        for cp in recvs:
            cp.wait_recv()
        for cp in sends:
            cp.wait_send()
        for cp in local:
            cp.wait()

    return _Comm(shards, [jax.ShapeDtypeStruct((N_DEV,) + t.shape, t.dtype) for t in shards], 5 * n, start, finish)


def _gather_round2(gathered):
    n = len(gathered)

    def plan(ins, outs, send_sems, recv_sems):
        x, y, c = _position()
        chips = [(1 - x, y), (x, 1 - y), (1 - x, 1 - y)]
        sends, recvs = [], []
        for a in range(n):
            for k, chip in enumerate(chips):
                sems = dict(send_sem=send_sems.at[3 * a + k], recv_sem=recv_sems.at[3 * a + k],
                            device_id=(x, y, 1 - c), device_id_type=MESH)
                sends.append(pltpu.make_async_remote_copy(
                    src_ref=_slot(ins[a], (*chip, c)), dst_ref=_slot(outs[a], (*chip, c)), **sems))
                recvs.append(pltpu.make_async_remote_copy(
                    src_ref=_slot(ins[a], (*chip, c)), dst_ref=_slot(outs[a], (*chip, 1 - c)), **sems))
        return sends, recvs

    def start(*refs):
        for cp in plan(*refs)[0]:
            cp.start()

    def finish(*refs):
        sends, recvs = plan(*refs)
        for cp in recvs:
            cp.wait_recv()
        for cp in sends:
            cp.wait_send()

    return _Comm(gathered, [jax.ShapeDtypeStruct(t.shape, t.dtype) for t in gathered], 3 * n, start, finish,
                 aliases={a: a for a in range(n)})


def _gather_two_level(shards, forward_at):
    n = len(shards)
    first, second = _gather_round1(shards), _gather_round2(shards)

    def middle(ins, outs, send_sems, recv_sems):
        first.finish(ins, outs, send_sems, recv_sems)
        second.start(outs, outs, _SemWindow(send_sems, first.n_sems), _SemWindow(recv_sems, first.n_sems))

    def finish(ins, outs, send_sems, recv_sems):
        second.finish(outs, outs, _SemWindow(send_sems, first.n_sems), _SemWindow(recv_sems, first.n_sems))

    return _Comm(shards, first.out_shape, first.n_sems + second.n_sems, first.start, finish,
                 middle=middle, middle_at=forward_at)


def _sibling_exchange(grads):
    n = len(grads)

    def plan(ins, outs, send_sems, recv_sems):
        x, y, c = _position()
        return [pltpu.make_async_remote_copy(
            src_ref=ins[a].at[2 * q + (1 - c)], dst_ref=outs[a].at[q],
            send_sem=send_sems.at[a * 4 + q], recv_sem=recv_sems.at[a * 4 + q],
            device_id=(x, y, 1 - c), device_id_type=MESH) for a in range(n) for q in range(4)]

    def start(*refs):
        for cp in plan(*refs):
            cp.start()

    def finish(*refs):
        for cp in plan(*refs):
            cp.wait()

    return _Comm(grads, [jax.ShapeDtypeStruct((4,) + t.shape[1:], t.dtype) for t in grads], 4 * n, start, finish)


def _chip_exchange(partials):
    n = len(partials)

    def plan(ins, outs, send_sems, recv_sems):
        x, y, c = _position()
        chips = [(1 - x, y), (x, 1 - y), (1 - x, 1 - y)]
        return [pltpu.make_async_remote_copy(
            src_ref=ins[a].at[2 * chip[0] + chip[1]], dst_ref=outs[a].at[k],
            send_sem=send_sems.at[a * 3 + k], recv_sem=recv_sems.at[a * 3 + k],
            device_id=(*chip, c), device_id_type=MESH) for a in range(n) for k, chip in enumerate(chips)]

    def start(*refs):
        for cp in plan(*refs):
            cp.start()

    def finish(*refs):
        for cp in plan(*refs):
            cp.wait()

    return _Comm(partials, [jax.ShapeDtypeStruct((3,) + t.shape[1:], t.dtype) for t in partials], 3 * n, start, finish)


class _SemWindow:
    def __init__(self, sems, offset):
        self._sems, self._offset = sems, offset

    @property
    def at(self):
        return self

    def __getitem__(self, i):
        return self._sems.at[self._offset + i]


def _join(parts):
    def each(fn_name, cins, couts, send_sems, recv_sems):
        i = o = sem = 0
        for p in parts:
            ni, no = len(p.operands), len(p.out_shape)
            getattr(p, fn_name)(cins[i:i + ni], couts[o:o + no], _SemWindow(send_sems, sem), _SemWindow(recv_sems, sem))
            i, o, sem = i + ni, o + no, sem + p.n_sems

    assert not any(p.aliases for p in parts)
    return _Comm([t for p in parts for t in p.operands], [t for p in parts for t in p.out_shape],
                 sum(p.n_sems for p in parts), functools.partial(each, "start"), functools.partial(each, "finish"))


def _pair_sum_call(grad, recv, place, name):
    _, r, ccols = grad.shape
    tr = _tile(r, 256)

    def body(place_ref, g_ref, r_ref, own_ref, pb_ref):
        p = g_ref[...] + r_ref[...]
        pb_ref[...] = p.astype(BF)

        @pl.when(pl.program_id(1) == place_ref[1])
        def _():
            own_ref[...] = p

    blk = lambda fn: pl.BlockSpec((None, tr, ccols), fn)
    return pl.pallas_call(
        body, name=name,
        grid_spec=pltpu.PrefetchScalarGridSpec(
            num_scalar_prefetch=1, grid=(r // tr, 4),
            in_specs=[blk(lambda i, q, place: (2 * q + place[0], i, 0)), blk(lambda i, q, place: (q, i, 0))],
            out_specs=[pl.BlockSpec((tr, ccols), lambda i, q, place: (i, 0)), blk(lambda i, q, place: (q, i, 0))]),
        out_shape=[jax.ShapeDtypeStruct((r, ccols), F32), jax.ShapeDtypeStruct((4, r, ccols), BF)],
        compiler_params=_params("parallel", "arbitrary"),
    )(place, grad, recv)


def _adamw_math(w, g, m, v):
    m = ADAM_B1 * m + (1.0 - ADAM_B1) * g
    v = ADAM_B2 * v + (1.0 - ADAM_B2) * (g * g)
    m_hat = m / (1.0 - ADAM_B1 ** ADAM_STEP)
    v_hat = v / (1.0 - ADAM_B2 ** ADAM_STEP)
    delta = -ADAM_LR * (m_hat / (jnp.sqrt(v_hat) + ADAM_EPS) + ADAM_WD * w)
    return delta, m, v


def _adamw_matrix_call(own, recv, w, m, v, name):
    r, ccols = w.shape
    gcols = own.shape[1]
    tr = _tile(r, 256)

    def body(p_ref, r_ref, w_ref, m_ref, v_ref, g_out, d_out, m_out, v_out):
        cols = pl.ds(0, ccols)
        g = (p_ref[:, cols] + r_ref[0, :, cols].astype(F32) + r_ref[1, :, cols].astype(F32)
             + r_ref[2, :, cols].astype(F32))
        delta, mn, vn = _adamw_math(w_ref[...], g, m_ref[...], v_ref[...])
        g_out[...] = g
        d_out[...] = delta
        m_out[...] = mn
        v_out[...] = vn

    mat = pl.BlockSpec((tr, ccols), lambda i: (i, 0))
    shp = jax.ShapeDtypeStruct((r, ccols), F32)
    return pl.pallas_call(
        body, name=name, grid=(r // tr,),
        in_specs=[pl.BlockSpec((tr, gcols), lambda i: (i, 0)), pl.BlockSpec((3, tr, gcols), lambda i: (0, i, 0)),
                  mat, mat, mat],
        out_specs=[mat, mat, mat, mat],
        out_shape=[shp, shp, shp, shp],
        compiler_params=_params("parallel"),
    )(own, recv, w, m, v)


def _adamw_vector_call(gathered, w, m, v, name):
    n = w.shape[1]

    def body(p_ref, w_ref, m_ref, v_ref, g_out, d_out, m_out, v_out):
        g = p_ref[0:1, :]
        for k in range(1, N_DEV):
            g = g + p_ref[k:k + 1, :]
        delta, mn, vn = _adamw_math(w_ref[...], g, m_ref[...], v_ref[...])
        g_out[...] = g
        d_out[...] = delta
        m_out[...] = mn
        v_out[...] = vn

    shp = jax.ShapeDtypeStruct((1, n), F32)
    return pl.pallas_call(body, name=name, out_shape=[shp, shp, shp, shp])(gathered, w, m, v)


def _round_up(n, mult):
    return (n + mult - 1) // mult * mult


def kernel(x, ffn1_norm, ffn1_w_gate, ffn1_w_up, ffn1_w_down, mix_norm, w_in, ret_norm_g, hgrn_lb_logits, hgrn_norm_g, w_out, ffn2_norm, ffn2_w_gate, ffn2_w_up, ffn2_w_down, final_norm, loss_target, m_ffn1_norm, m_ffn1_w_gate, m_ffn1_w_up, m_ffn1_w_down, m_mix_norm, m_w_in, m_ret_norm_g, m_hgrn_lb_logits, m_hgrn_norm_g, m_w_out, m_ffn2_norm, m_ffn2_w_gate, m_ffn2_w_up, m_ffn2_w_down, m_final_norm, v_ffn1_norm, v_ffn1_w_gate, v_ffn1_w_up, v_ffn1_w_down, v_mix_norm, v_w_in, v_ret_norm_g, v_hgrn_lb_logits, v_hgrn_norm_g, v_w_out, v_ffn2_norm, v_ffn2_w_gate, v_ffn2_w_up, v_ffn2_w_down, v_final_norm):
    xs = x[0]
    target = loss_target[0]
    s, d = xs.shape
    f_loc = ffn1_w_gate.shape[2]
    fp = _round_up(f_loc, LANE)
    pad_cols = lambda t: jnp.pad(t[0], ((0, 0), (0, fp - f_loc)))
    pad_rows = lambda t: jnp.pad(t[0], ((0, fp - f_loc), (0, 0)))

    mat_names = ["ffn1_w_gate", "ffn1_w_up", "ffn1_w_down", "w_in", "w_out", "ffn2_w_gate", "ffn2_w_up", "ffn2_w_down"]
    mat_pad = [pad_cols, pad_cols, pad_rows, lambda t: t[0], lambda t: t[0], pad_cols, pad_cols, pad_rows]
    mat_w = [ffn1_w_gate, ffn1_w_up, ffn1_w_down, w_in, w_out, ffn2_w_gate, ffn2_w_up, ffn2_w_down]
    mat_m = [m_ffn1_w_gate, m_ffn1_w_up, m_ffn1_w_down, m_w_in, m_w_out, m_ffn2_w_gate, m_ffn2_w_up, m_ffn2_w_down]
    mat_v = [v_ffn1_w_gate, v_ffn1_w_up, v_ffn1_w_down, v_w_in, v_w_out, v_ffn2_w_gate, v_ffn2_w_up, v_ffn2_w_down]

    cx, cy, cc = _position()
    place = jnp.stack([cc, 2 * cx + cy]).astype(jnp.int32)
    mat_index = {nm: i for i, nm in enumerate(mat_names)}
    mat_out = {}

    def pair_sums(names, grads, from_sibling):
        return [_pair_sum_call(g, r, place, "pair_sum_" + nm) for nm, g, r in zip(names, grads, from_sibling)]

    def update(names, sums, from_chips):
        for nm, (p, _), r in zip(names, sums, from_chips):
            i = mat_index[nm]
            res = _adamw_matrix_call(p, r, mat_w[i][0], mat_m[i][0], mat_v[i][0], "adamw_" + nm)
            mat_out[nm] = [t[None] for t in res]

    shards = [p(t.astype(BF)) for p, t in zip(mat_pad, mat_w)]
    wg1, wu1 = _all_gather_call(shards[:2], "gather_ffn1_up")

    h1 = _rmsnorm_call(xs, ffn1_norm, "ffn1_norm")
    (g1, u1, a1), (wd1, win) = _ffn_up_call(h1, wg1, wu1, "ffn1_up", comm=_gather_two_level(shards[2:4], 0.7))
    (x1, h2), _ = _down_call(a1, wd1, xs, mix_norm, FFN_RESIDUAL_WEIGHT, "ffn1_down")
    (proj,), landed_a = _proj_call(h2, win, "mix_in", comm=_gather_round1([shards[4], shards[7]]))
    wmix = proj.shape[1] // 8
    cos, sin = _rope_tables(s, wmix // RET_HEADS)
    consts = _ret_consts(wmix // RET_HEADS)
    (o_hg, m_hg, st_hg), landed_b = _hgrn_fwd_call(proj, hgrn_lb_logits, hgrn_norm_g, "hgrn_fwd",
                                                   comm=_gather_round1(shards[5:7]))
    (o_ret, m_ret, st_ret), (wout, wd2, wg2, wu2) = _ret_fwd_call(proj, cos, sin, consts, ret_norm_g, "ret_fwd",
                                                                  comm=_gather_round2(landed_a + landed_b))
    merged = jnp.concatenate([m_ret, m_hg], axis=1)
    wout_wide = wout.reshape(2, wout.shape[0] * wout.shape[1] // 2, d)
    (x2, h3), _ = _down_call(merged, wout_wide, x1, ffn2_norm, 1.0, "mix_out")
    (g2, u2, a2), _ = _ffn_up_call(h3, wg2, wu2, "ffn2_up")
    (x3,), _ = _down_call(a2, wd2, x2, None, FFN_RESIDUAL_WEIGHT, "ffn2_down")
    loss_part, dx3, dx3b, gv_final = _loss_call(x3, target, final_norm[None, :], "loss_head")

    (dg2, du2), _ = _bwd_up_call(dx3b, wd2, g2, u2, FFN_RESIDUAL_WEIGHT, "ffn2_bwd_up")
    dh3, _ = _bwd_mm_call([(dg2, wg2), (du2, wu2)], "ffn2_bwd_down")
    dx2, dx2b, gv_n3 = _norm_bwd_call(dh3, dx3, x2, ffn2_norm, "ffn2_norm_bwd")
    names_a = ["ffn2_w_gate", "ffn2_w_up", "ffn2_w_down"]
    grads_a = [_wgrad_call(h3, dg2, N_DEV, False, 1.0, "ffn2_wgrad_gate"),
               _wgrad_call(h3, du2, N_DEV, False, 1.0, "ffn2_wgrad_up"),
               _wgrad_call(a2, dx3b, N_DEV, True, FFN_RESIDUAL_WEIGHT, "ffn2_wgrad_down")]

    (dmerged,), sib_a = _proj_call(dx2b, jnp.swapaxes(wout_wide, 1, 2), "mix_out_bwd",
                                   comm=_sibling_exchange(grads_a))
    gm_out = _wgrad_call(merged, dx2b, 2, True, 1.0, "mix_out_wgrad").reshape(wout.shape)
    sums_a = pair_sums(names_a, grads_a, sib_a)
    dproj_half, gv_ret = _ret_bwd_call(proj, cos, sin, consts, ret_norm_g, o_ret, st_ret, dmerged, "ret_bwd")
    (dproj, gv_lb, gv_hg), landed = _hgrn_bwd_call(
        proj, hgrn_lb_logits, hgrn_norm_g, o_hg, st_hg, dmerged, dproj_half, "hgrn_bwd",
        comm=_join([_chip_exchange([pb for _, pb in sums_a]), _sibling_exchange([gm_out])]))
    update(names_a, sums_a, landed[:3])
    sums_out = pair_sums(["w_out"], [gm_out], landed[3:])
    dh2, chips_out = _bwd_mm_call([(dproj, win)], "mix_in_bwd", per_step=2, comm=_chip_exchange([sums_out[0][1]]))
    dx1, dx1b, gv_n2 = _norm_bwd_call(dh2, dx2, x1, mix_norm, "mix_norm_bwd")
    update(["w_out"], sums_out, chips_out)

    gm_in = _wgrad_call(h2, dproj, N_DEV, False, 1.0, "mix_in_wgrad")
    gm_d1, sib_in = _wgrad_call(a1, dx1b, N_DEV, True, FFN_RESIDUAL_WEIGHT, "ffn1_wgrad_down",
                                comm=_sibling_exchange([gm_in]))
    sums_in = pair_sums(["w_in"], [gm_in], sib_in)
    (dg1, du1), landed = _bwd_up_call(dx1b, wd1, g1, u1, FFN_RESIDUAL_WEIGHT, "ffn1_bwd_up",
                                      comm=_join([_chip_exchange([sums_in[0][1]]), _sibling_exchange([gm_d1])]))
    update(["w_in"], sums_in, landed[:1])
    sums_d1 = pair_sums(["ffn1_w_down"], [gm_d1], landed[1:])
    gm_g1, chips_d1 = _wgrad_call(h1, dg1, N_DEV, False, 1.0, "ffn1_wgrad_gate",
                                  comm=_chip_exchange([sums_d1[0][1]]))
    update(["ffn1_w_down"], sums_d1, chips_d1)
    gm_u1, sib_g = _wgrad_call(h1, du1, N_DEV, False, 1.0, "ffn1_wgrad_up", comm=_sibling_exchange([gm_g1]))
    sums_g = pair_sums(["ffn1_w_gate"], [gm_g1], sib_g)
    n_row_tiles = s // _tile(s, 1024)
    assert n_row_tiles >= 2, "the sequence must span at least two row tiles"
    n_first = n_row_tiles // 2
    dh1_part, landed = _bwd_mm_call(
        [(dg1, wg1), (du1, wu1)], "ffn1_bwd_down_a", tiles=(0, n_first),
        comm=_join([_chip_exchange([sums_g[0][1]]), _sibling_exchange([gm_u1])]))
    update(["ffn1_w_gate"], sums_g, landed[:1])
    sums_u = pair_sums(["ffn1_w_up"], [gm_u1], landed[1:])
    dh1, chips_u = _bwd_mm_call(
        [(dg1, wg1), (du1, wu1)], "ffn1_bwd_down_b", tiles=(n_first, n_row_tiles - n_first),
        carry=dh1_part, comm=_chip_exchange([sums_u[0][1]]))
    update(["ffn1_w_up"], sums_u, chips_u)
    dx0, _, gv_n1 = _norm_bwd_call(dh1, dx1, xs, ffn1_norm, "ffn1_norm_bwd")

    vec_names = ["ffn1_norm", "mix_norm", "ret_norm_g", "hgrn_lb_logits", "hgrn_norm_g", "ffn2_norm", "final_norm"]
    vec_g = [gv_n1, gv_n2, gv_ret, gv_lb, gv_hg, gv_n3, gv_final]
    vec_w = [ffn1_norm, mix_norm, ret_norm_g, hgrn_lb_logits, hgrn_norm_g, ffn2_norm, final_norm[None, :]]
    vec_m = [m_ffn1_norm, m_mix_norm, m_ret_norm_g, m_hgrn_lb_logits, m_hgrn_norm_g, m_ffn2_norm, m_final_norm[None, :]]
    vec_v = [v_ffn1_norm, v_mix_norm, v_ret_norm_g, v_hgrn_lb_logits, v_hgrn_norm_g, v_ffn2_norm, v_final_norm[None, :]]
    cat = lambda ts: jnp.concatenate(ts, axis=1)
    (vec_all,) = _all_gather_call([cat(vec_g)], "gather_vector_grads")
    vres = _adamw_vector_call(vec_all[:, 0, :], cat(vec_w), cat(vec_m), cat(vec_v), "adamw_vectors")
    vec_out = {}
    off = 0
    for nm, t in zip(vec_names, vec_w):
        n = t.shape[1]
        parts = [r[:, off:off + n] for r in vres]
        if nm == "final_norm":
            parts = [p[0] for p in parts]
        vec_out[nm] = parts
        off += n

    loss = lax.psum(loss_part[0, 0], ("x", "y", "c"))
    order = ["ffn1_norm", "ffn1_w_gate", "ffn1_w_up", "ffn1_w_down", "mix_norm", "w_in", "ret_norm_g", "hgrn_lb_logits",
             "hgrn_norm_g", "w_out", "ffn2_norm", "ffn2_w_gate", "ffn2_w_up", "ffn2_w_down", "final_norm"]
    res = {**mat_out, **vec_out}
    outs = [loss, dx0[None]]
    for kind in range(4):
        outs += [res[nm][kind] for nm in order]
    return tuple(outs)
```
